```python
import jax, jax.numpy as jnp
from jax import lax
import numpy as np

D_MODEL = 1024
BATCH = 8
SEQ = 2048
DEPTH = 4

MEM_LEN = 256
EPS = 1e-6
ATTN_HEADS = 4
ATTN_HEAD_DIM = 64
ATTN_WIDTH = ATTN_HEADS * ATTN_HEAD_DIM
DILATED_PATTERNS = ((128, 1), (512, 4), (2048, 16))
WINDOW_BLOCK = 128
ROPE_THETA = 500000.0
ROPE_DIM = ATTN_HEAD_DIM // 4
CONV_GROUPS = 4
CONV_WIDTH = D_MODEL // 4
CONV_K = 3
GDN_HEADS = 4
GDN_HEAD_DIM = 128
GDN_WIDTH = GDN_HEADS * GDN_HEAD_DIM
GDN_CONV_K = 4
GDN_CHUNK = 64
MIX_WIDTH = ATTN_WIDTH + CONV_WIDTH + GDN_WIDTH
IN_SPLITS = (ATTN_WIDTH, ATTN_WIDTH, ATTN_WIDTH,
             CONV_WIDTH, CONV_WIDTH, CONV_WIDTH,
             GDN_WIDTH, GDN_WIDTH, GDN_WIDTH, GDN_HEADS, GDN_HEADS, GDN_WIDTH)
IN_WIDTH = 3 * ATTN_WIDTH + 3 * CONV_WIDTH + 4 * GDN_WIDTH + 2 * GDN_HEADS
XATTN_HEADS = 4
XATTN_HEAD_DIM = D_MODEL // XATTN_HEADS
XATTN_WIDTH = XATTN_HEADS * XATTN_HEAD_DIM
FFN_HIDDEN = -(-8 * D_MODEL // (3 * 256)) * 256

kernel_name = 'hybrid_dilated_conv_deltanet_block'


def rms_norm(x, w):
    x32 = x.astype(jnp.float32)
    y = x32 * lax.rsqrt(jnp.mean(x32 * x32, axis=-1, keepdims=True) + EPS)
    return (y * w.astype(jnp.float32)).astype(x.dtype)


def causal_depthwise_conv(x, w):
    K, C = w.shape
    return lax.conv_general_dilated(x, w[:, None, :].astype(x.dtype), window_strides=(1,),
                                    padding=[(K - 1, 0)], dimension_numbers=('NWC', 'WIO', 'NWC'),
                                    feature_group_count=C)


def rotary_tables(positions):
    inv_freq = jnp.float32(ROPE_THETA) ** (-jnp.arange(0, ROPE_DIM, 2, dtype=jnp.float32) / ROPE_DIM)
    ang = positions.astype(jnp.float32)[..., None] * inv_freq
    return jnp.cos(ang)[:, :, None, :], jnp.sin(ang)[:, :, None, :]


def apply_partial_rotary(x, cos, sin):
    half = ROPE_DIM // 2
    x1 = x[..., :half].astype(jnp.float32)
    x2 = x[..., half:ROPE_DIM].astype(jnp.float32)
    rot = jnp.concatenate([x1 * cos - x2 * sin, x2 * cos + x1 * sin], axis=-1).astype(x.dtype)
    return jnp.concatenate([rot, x[..., ROPE_DIM:]], axis=-1)


def dilated_window_attention(q, k, v, dilation, n_back):
    B, S, H, Dh = q.shape
    QB = WINDOW_BLOCK
    L = S // dilation
    nb = -(-L // QB)
    Lp = nb * QB

    def to_residue(t):
        t = t.reshape(B, L, dilation, H, Dh).transpose(0, 2, 1, 3, 4).reshape(B * dilation, L, H, Dh)
        t = jnp.pad(t, ((0, 0), (0, Lp - L), (0, 0), (0, 0)))
        return t.reshape(B * dilation, nb, QB, H, Dh)

    def with_prev(t):
        prev = jnp.pad(t[:, :-1], ((0, 0), (1, 0), (0, 0), (0, 0), (0, 0)))
        return jnp.concatenate([prev, t], axis=2)

    qb = to_residue(q)
    kw = with_prev(to_residue(k))
    vw = with_prev(to_residue(v))
    s = jnp.einsum('bnqhd,bnkhd->bnhqk', qb, kw, preferred_element_type=jnp.float32)
    qi = jnp.arange(nb)[:, None, None] * QB + jnp.arange(QB)[None, :, None]
    kj = jnp.arange(nb)[:, None, None] * QB - QB + jnp.arange(2 * QB)[None, None, :]
    dist = qi - kj
    mask = (dist >= 0) & (dist <= n_back) & (kj >= 0)
    s = jnp.where(mask[None, :, None], s, -jnp.inf)
    m = jnp.max(s, axis=-1, keepdims=True)
    p = jnp.exp(s - m)
    l = jnp.sum(p, axis=-1, keepdims=True)
    o = jnp.einsum('bnhqk,bnkhd->bnqhd', (p / l).astype(v.dtype), vw)
    lse = (m + jnp.log(l))[..., 0]
    o = o.reshape(B, dilation, Lp, H, Dh)[:, :, :L].transpose(0, 2, 1, 3, 4).reshape(B, S, H, Dh)
    lse = lse.transpose(0, 1, 3, 2).reshape(B, dilation, Lp, H)[:, :, :L]
    lse = lse.transpose(0, 2, 1, 3).reshape(B, S, H)
    return o, lse


def dilated_attention_mixer(q, k, v, cos, sin):
    B, S, _ = q.shape
    q = q.reshape(B, S, ATTN_HEADS, ATTN_HEAD_DIM)
    k = k.reshape(B, S, ATTN_HEADS, ATTN_HEAD_DIM)
    v = v.reshape(B, S, ATTN_HEADS, ATTN_HEAD_DIM)
    q = apply_partial_rotary(q, cos, sin) * (ATTN_HEAD_DIM ** -0.5)
    k = apply_partial_rotary(k, cos, sin)
    outs, lses = [], []
    for window, dilation in DILATED_PATTERNS:
        o, lse = dilated_window_attention(q, k, v, dilation, window // dilation)
        outs.append(o.astype(jnp.float32))
        lses.append(lse)
    wts = jax.nn.softmax(jnp.stack(lses, axis=0), axis=0)
    o = jnp.einsum('pbsh,pbshd->bshd', wts, jnp.stack(outs, axis=0))
    return o.reshape(B, S, ATTN_WIDTH).astype(q.dtype)


def short_conv_mixer(b_gate, c_gate, xv, conv_w):
    return b_gate * causal_depthwise_conv(c_gate * xv, conv_w)


def gated_delta_rule(q, k, v, g, beta):
    B, S, H, Dk = q.shape
    Dv = v.shape[-1]
    C = GDN_CHUNK
    N = S // C

    def chunks(t):
        return jnp.moveaxis(t.reshape(B, N, C, H, *t.shape[3:]), 3, 1)

    q, k, v, g, beta = chunks(q), chunks(k), chunks(v), chunks(g), chunks(beta)
    decay = jnp.cumsum(g, axis=-1)
    causal = jnp.tril(jnp.ones((C, C), dtype=bool))
    strict = jnp.tril(jnp.ones((C, C), dtype=bool), -1)
    rel = jnp.exp(jnp.where(causal, decay[..., :, None] - decay[..., None, :], -jnp.inf))
    k_beta = k * beta[..., None]
    a = jnp.where(strict, jnp.einsum('bhnik,bhnjk->bhnij', k_beta, k) * rel, 0.0)
    eye = jnp.broadcast_to(jnp.eye(C, dtype=jnp.float32), a.shape)
    rhs = jnp.concatenate([v * beta[..., None], k_beta * jnp.exp(decay)[..., None]], axis=-1)
    sol = lax.linalg.triangular_solve(eye + a, rhs, left_side=True, lower=True, unit_diagonal=True)
    u, w = sol[..., :Dv], sol[..., Dv:]
    attn = jnp.where(causal, jnp.einsum('bhnik,bhnjk->bhnij', q, k) * rel, 0.0)
    q_dec = q * jnp.exp(decay)[..., None]
    k_dec = k * jnp.exp(decay[..., -1:] - decay)[..., None]
    chunk_decay = jnp.exp(decay[..., -1])

    def step(state, inp):
        q_i, k_i, u_i, w_i, attn_i, cd_i = inp
        v_new = u_i - jnp.einsum('bhck,bhkv->bhcv', w_i, state)
        o_i = (jnp.einsum('bhck,bhkv->bhcv', q_i, state)
               + jnp.einsum('bhij,bhjv->bhiv', attn_i, v_new))
        state = state * cd_i[..., None, None] + jnp.einsum('bhck,bhcv->bhkv', k_i, v_new)
        return state, o_i

    xs = tuple(jnp.moveaxis(t, 2, 0) for t in (q_dec, k_dec, u, w, attn, chunk_decay))
    state0 = jnp.zeros((B, H, Dk, Dv), jnp.float32)
    _, o = lax.scan(step, state0, xs)
    return o.transpose(1, 0, 3, 2, 4).reshape(B, S, H, Dv)


def gated_deltanet_mixer(q, k, v, a, b, gate, conv_w, a_log, dt_bias, norm_w):
    B, S, _ = q.shape
    qkv = jax.nn.silu(causal_depthwise_conv(jnp.concatenate([q, k, v], axis=-1), conv_w))
    q, k, v = jnp.split(qkv.astype(jnp.float32), 3, axis=-1)
    q = q.reshape(B, S, GDN_HEADS, GDN_HEAD_DIM)
    k = k.reshape(B, S, GDN_HEADS, GDN_HEAD_DIM)
    v = v.reshape(B, S, GDN_HEADS, GDN_HEAD_DIM)
    q = q * lax.rsqrt(jnp.sum(q * q, axis=-1, keepdims=True) + EPS) * (GDN_HEAD_DIM ** -0.5)
    k = k * lax.rsqrt(jnp.sum(k * k, axis=-1, keepdims=True) + EPS)
    g = -jnp.exp(a_log.astype(jnp.float32)) * jax.nn.softplus(a.astype(jnp.float32) + dt_bias.astype(jnp.float32))
    beta = jax.nn.sigmoid(b.astype(jnp.float32))
    o = gated_delta_rule(q, k, v, g, beta)
    gate = jax.nn.silu(gate.astype(jnp.float32)).reshape(B, S, GDN_HEADS, GDN_HEAD_DIM)
    o = rms_norm(o, norm_w) * gate
    return o.reshape(B, S, GDN_WIDTH).astype(gate.dtype if gate.dtype == q.dtype else gate.dtype)


def memory_cross_attention(h, mem, w_q, w_kv, w_o):
    B, S, _ = h.shape
    M = mem.shape[1]
    q = (h @ w_q).reshape(B, S, XATTN_HEADS, XATTN_HEAD_DIM)
    k, v = jnp.split(mem @ w_kv, 2, axis=-1)
    k = k.reshape(B, M, XATTN_HEADS, XATTN_HEAD_DIM)
    v = v.reshape(B, M, XATTN_HEADS, XATTN_HEAD_DIM)
    s = jnp.einsum('bshd,bmhd->bhsm', q, k, preferred_element_type=jnp.float32) * (XATTN_HEAD_DIM ** -0.5)
    p = jax.nn.softmax(s, axis=-1)
    o = jnp.einsum('bhsm,bmhd->bshd', p.astype(v.dtype), v).reshape(B, S, XATTN_WIDTH)
    return o @ w_o


def swiglu_ffn(h, w_gate_up, w_down):
    gate, up = jnp.split(h @ w_gate_up, 2, axis=-1)
    return (jax.nn.silu(gate) * up) @ w_down


def _fwd_setup_inputs(seed: int = 0) -> dict:
    key = jax.random.key(seed)
    ks = jax.random.split(key, 24)
    f32 = jnp.float32

    def dense(k, shape, fan_in):
        return jax.random.normal(k, shape, f32) * (fan_in ** -0.5)

    def gain(k, shape):
        return 1.0 + 0.02 * jax.random.normal(k, shape, f32)

    x = jax.random.normal(ks[0], (BATCH, SEQ, D_MODEL), f32)
    mem = jax.random.normal(ks[1], (BATCH, MEM_LEN, D_MODEL), f32)
    positions = (jax.random.randint(ks[2], (BATCH, 1), 0, 4096, dtype=jnp.int32)
                 + jnp.arange(SEQ, dtype=jnp.int32)[None, :])
    dt = jnp.exp(jax.random.uniform(ks[3], (DEPTH, GDN_HEADS), f32, np.log(1e-3), np.log(1e-1)))
    return {
        'x': x,
        'mem': mem,
        'positions': positions,
        'norm_mix_pre': gain(ks[4], (DEPTH, D_MODEL)),
        'norm_mix_post': gain(ks[5], (DEPTH, D_MODEL)),
        'w_in': dense(ks[6], (DEPTH, D_MODEL, IN_WIDTH), D_MODEL),
        'conv_short': dense(ks[7], (DEPTH, CONV_K, CONV_WIDTH), CONV_K),
        'conv_gdn': dense(ks[8], (DEPTH, GDN_CONV_K, 3 * GDN_WIDTH), GDN_CONV_K),
        'gdn_a_log': jnp.log(jax.random.uniform(ks[9], (DEPTH, GDN_HEADS), f32, 1.0, 16.0)),
        'gdn_dt_bias': dt + jnp.log(-jnp.expm1(-dt)),
        'gdn_norm': gain(ks[10], (DEPTH, GDN_HEAD_DIM)),
        'w_out': dense(ks[11], (DEPTH, MIX_WIDTH, D_MODEL), MIX_WIDTH),
        'norm_mem': gain(ks[12], (DEPTH, D_MODEL)),
        'norm_xattn_pre': gain(ks[13], (DEPTH, D_MODEL)),
        'norm_xattn_post': gain(ks[14], (DEPTH, D_MODEL)),
        'w_xq': dense(ks[15], (DEPTH, D_MODEL, XATTN_WIDTH), D_MODEL),
        'w_xkv': dense(ks[16], (DEPTH, D_MODEL, 2 * XATTN_WIDTH), D_MODEL),
        'w_xo': dense(ks[17], (DEPTH, XATTN_WIDTH, D_MODEL), XATTN_WIDTH),
        'norm_ffn_pre': gain(ks[18], (DEPTH, D_MODEL)),
        'norm_ffn_post': gain(ks[19], (DEPTH, D_MODEL)),
        'w_gate_up': dense(ks[20], (DEPTH, D_MODEL, 2 * FFN_HIDDEN), D_MODEL),
        'w_down': dense(ks[21], (DEPTH, FFN_HIDDEN, D_MODEL), FFN_HIDDEN),
    }


def _fwd_reference(x, mem, positions, norm_mix_pre, norm_mix_post, w_in, conv_short, conv_gdn,
              gdn_a_log, gdn_dt_bias, gdn_norm, w_out, norm_mem, norm_xattn_pre, norm_xattn_post,
              w_xq, w_xkv, w_xo, norm_ffn_pre, norm_ffn_post, w_gate_up, w_down):
    cos, sin = rotary_tables(positions)
    split_idx = [int(i) for i in np.cumsum(IN_SPLITS)[:-1]]
    h = x
    for l in range(DEPTH):
        hn = rms_norm(h, norm_mix_pre[l])
        proj = hn @ w_in[l]
        (aq, ak, av, cb, cc, cx, gq, gk, gv, ga, gb, gg) = jnp.split(proj, split_idx, axis=-1)
        y_attn = dilated_attention_mixer(aq, ak, av, cos, sin)
        y_conv = short_conv_mixer(cb, cc, cx, conv_short[l])
        y_gdn = gated_deltanet_mixer(gq, gk, gv, ga, gb, gg, conv_gdn[l],
                                     gdn_a_log[l], gdn_dt_bias[l], gdn_norm[l]).astype(proj.dtype)
        mix = jnp.concatenate([y_attn.astype(proj.dtype), y_conv, y_gdn], axis=-1) @ w_out[l]
        h = h + rms_norm(mix, norm_mix_post[l])
        hn = rms_norm(h, norm_xattn_pre[l])
        xa = memory_cross_attention(hn, rms_norm(mem, norm_mem[l]), w_xq[l], w_xkv[l], w_xo[l])
        h = h + rms_norm(xa, norm_xattn_post[l])
        hn = rms_norm(h, norm_ffn_pre[l])
        h = h + rms_norm(swiglu_ffn(hn, w_gate_up[l], w_down[l]), norm_ffn_post[l])
    return h


import jax as _jax
import jax.numpy as _jnp

TWIN_FORMAT = 'train_step'
FWD_PARAMS = ['x', 'mem', 'positions', 'norm_mix_pre', 'norm_mix_post', 'w_in', 'conv_short', 'conv_gdn', 'gdn_a_log', 'gdn_dt_bias', 'gdn_norm', 'w_out', 'norm_mem', 'norm_xattn_pre', 'norm_xattn_post', 'w_xq', 'w_xkv', 'w_xo', 'norm_ffn_pre', 'norm_ffn_post', 'w_gate_up', 'w_down']
TWIN_WEIGHTS = ['norm_mix_pre', 'norm_mix_post', 'w_in', 'conv_short', 'conv_gdn', 'gdn_a_log', 'gdn_dt_bias', 'gdn_norm', 'w_out', 'norm_mem', 'norm_xattn_pre', 'norm_xattn_post', 'w_xq', 'w_xkv', 'w_xo', 'norm_ffn_pre', 'norm_ffn_post', 'w_gate_up', 'w_down']
TWIN_DIFF_INPUT = 'x'
TWIN_INPUTS = ['x', 'mem', 'positions', 'norm_mix_pre', 'norm_mix_post', 'w_in', 'conv_short', 'conv_gdn', 'gdn_a_log', 'gdn_dt_bias', 'gdn_norm', 'w_out', 'norm_mem', 'norm_xattn_pre', 'norm_xattn_post', 'w_xq', 'w_xkv', 'w_xo', 'norm_ffn_pre', 'norm_ffn_post', 'w_gate_up', 'w_down', 'loss_target', 'm_norm_mix_pre', 'm_norm_mix_post', 'm_w_in', 'm_conv_short', 'm_conv_gdn', 'm_gdn_a_log', 'm_gdn_dt_bias', 'm_gdn_norm', 'm_w_out', 'm_norm_mem', 'm_norm_xattn_pre', 'm_norm_xattn_post', 'm_w_xq', 'm_w_xkv', 'm_w_xo', 'm_norm_ffn_pre', 'm_norm_ffn_post', 'm_w_gate_up', 'm_w_down', 'v_norm_mix_pre', 'v_norm_mix_post', 'v_w_in', 'v_conv_short', 'v_conv_gdn', 'v_gdn_a_log', 'v_gdn_dt_bias', 'v_gdn_norm', 'v_w_out', 'v_norm_mem', 'v_norm_xattn_pre', 'v_norm_xattn_post', 'v_w_xq', 'v_w_xkv', 'v_w_xo', 'v_norm_ffn_pre', 'v_norm_ffn_post', 'v_w_gate_up', 'v_w_down']
TWIN_OUTPUTS = ['loss', 'grad_x', 'grad_norm_mix_pre', 'grad_norm_mix_post', 'grad_w_in', 'grad_conv_short', 'grad_conv_gdn', 'grad_gdn_a_log', 'grad_gdn_dt_bias', 'grad_gdn_norm', 'grad_w_out', 'grad_norm_mem', 'grad_norm_xattn_pre', 'grad_norm_xattn_post', 'grad_w_xq', 'grad_w_xkv', 'grad_w_xo', 'grad_norm_ffn_pre', 'grad_norm_ffn_post', 'grad_w_gate_up', 'grad_w_down', 'delta_norm_mix_pre', 'delta_norm_mix_post', 'delta_w_in', 'delta_conv_short', 'delta_conv_gdn', 'delta_gdn_a_log', 'delta_gdn_dt_bias', 'delta_gdn_norm', 'delta_w_out', 'delta_norm_mem', 'delta_norm_xattn_pre', 'delta_norm_xattn_post', 'delta_w_xq', 'delta_w_xkv', 'delta_w_xo', 'delta_norm_ffn_pre', 'delta_norm_ffn_post', 'delta_w_gate_up', 'delta_w_down', 'new_m_norm_mix_pre', 'new_m_norm_mix_post', 'new_m_w_in', 'new_m_conv_short', 'new_m_conv_gdn', 'new_m_gdn_a_log', 'new_m_gdn_dt_bias', 'new_m_gdn_norm', 'new_m_w_out', 'new_m_norm_mem', 'new_m_norm_xattn_pre', 'new_m_norm_xattn_post', 'new_m_w_xq', 'new_m_w_xkv', 'new_m_w_xo', 'new_m_norm_ffn_pre', 'new_m_norm_ffn_post', 'new_m_w_gate_up', 'new_m_w_down', 'new_v_norm_mix_pre', 'new_v_norm_mix_post', 'new_v_w_in', 'new_v_conv_short', 'new_v_conv_gdn', 'new_v_gdn_a_log', 'new_v_gdn_dt_bias', 'new_v_gdn_norm', 'new_v_w_out', 'new_v_norm_mem', 'new_v_norm_xattn_pre', 'new_v_norm_xattn_post', 'new_v_w_xq', 'new_v_w_xkv', 'new_v_w_xo', 'new_v_norm_ffn_pre', 'new_v_norm_ffn_post', 'new_v_w_gate_up', 'new_v_w_down']
TWIN_LEAF_KINDS = {'loss': 'loss', 'grad_x': 'grad_x', 'grad_norm_mix_pre': 'grad_w', 'grad_norm_mix_post': 'grad_w', 'grad_w_in': 'grad_w', 'grad_conv_short': 'grad_w', 'grad_conv_gdn': 'grad_w', 'grad_gdn_a_log': 'grad_w', 'grad_gdn_dt_bias': 'grad_w', 'grad_gdn_norm': 'grad_w', 'grad_w_out': 'grad_w', 'grad_norm_mem': 'grad_w', 'grad_norm_xattn_pre': 'grad_w', 'grad_norm_xattn_post': 'grad_w', 'grad_w_xq': 'grad_w', 'grad_w_xkv': 'grad_w', 'grad_w_xo': 'grad_w', 'grad_norm_ffn_pre': 'grad_w', 'grad_norm_ffn_post': 'grad_w', 'grad_w_gate_up': 'grad_w', 'grad_w_down': 'grad_w', 'delta_norm_mix_pre': 'delta_w', 'delta_norm_mix_post': 'delta_w', 'delta_w_in': 'delta_w', 'delta_conv_short': 'delta_w', 'delta_conv_gdn': 'delta_w', 'delta_gdn_a_log': 'delta_w', 'delta_gdn_dt_bias': 'delta_w', 'delta_gdn_norm': 'delta_w', 'delta_w_out': 'delta_w', 'delta_norm_mem': 'delta_w', 'delta_norm_xattn_pre': 'delta_w', 'delta_norm_xattn_post': 'delta_w', 'delta_w_xq': 'delta_w', 'delta_w_xkv': 'delta_w', 'delta_w_xo': 'delta_w', 'delta_norm_ffn_pre': 'delta_w', 'delta_norm_ffn_post': 'delta_w', 'delta_w_gate_up': 'delta_w', 'delta_w_down': 'delta_w', 'new_m_norm_mix_pre': 'new_m', 'new_m_norm_mix_post': 'new_m', 'new_m_w_in': 'new_m', 'new_m_conv_short': 'new_m', 'new_m_conv_gdn': 'new_m', 'new_m_gdn_a_log': 'new_m', 'new_m_gdn_dt_bias': 'new_m', 'new_m_gdn_norm': 'new_m', 'new_m_w_out': 'new_m', 'new_m_norm_mem': 'new_m', 'new_m_norm_xattn_pre': 'new_m', 'new_m_norm_xattn_post': 'new_m', 'new_m_w_xq': 'new_m', 'new_m_w_xkv': 'new_m', 'new_m_w_xo': 'new_m', 'new_m_norm_ffn_pre': 'new_m', 'new_m_norm_ffn_post': 'new_m', 'new_m_w_gate_up': 'new_m', 'new_m_w_down': 'new_m', 'new_v_norm_mix_pre': 'new_v', 'new_v_norm_mix_post': 'new_v', 'new_v_w_in': 'new_v', 'new_v_conv_short': 'new_v', 'new_v_conv_gdn': 'new_v', 'new_v_gdn_a_log': 'new_v', 'new_v_gdn_dt_bias': 'new_v', 'new_v_gdn_norm': 'new_v', 'new_v_w_out': 'new_v', 'new_v_norm_mem': 'new_v', 'new_v_norm_xattn_pre': 'new_v', 'new_v_norm_xattn_post': 'new_v', 'new_v_w_xq': 'new_v', 'new_v_w_xkv': 'new_v', 'new_v_w_xo': 'new_v', 'new_v_norm_ffn_pre': 'new_v', 'new_v_norm_ffn_post': 'new_v', 'new_v_w_gate_up': 'new_v', 'new_v_w_down': 'new_v'}


def _forward(args):
    return _fwd_reference(*[args[k] for k in FWD_PARAMS])


def _output_shape():
    out = _jax.eval_shape(lambda: _forward(_fwd_setup_inputs(0)))
    return out.shape, out.dtype

N_MICROBATCH = 1
ADAM_LR = 0.001
ADAM_B1 = 0.9
ADAM_B2 = 0.999
ADAM_EPS = 1e-08
ADAM_WD = 0.01
ADAM_STEP = 10
PER_EXAMPLE_BATCH_AXIS = {'x': 0, 'mem': 0, 'positions': 0, 'loss_target': 0}
SHARED_INPUTS = []
_WEIGHT_DTYPES = {'norm_mix_pre': _jnp.float32, 'norm_mix_post': _jnp.float32, 'w_in': _jnp.float32, 'conv_short': _jnp.float32, 'conv_gdn': _jnp.float32, 'gdn_a_log': _jnp.float32, 'gdn_dt_bias': _jnp.float32, 'gdn_norm': _jnp.float32, 'w_out': _jnp.float32, 'norm_mem': _jnp.float32, 'norm_xattn_pre': _jnp.float32, 'norm_xattn_post': _jnp.float32, 'w_xq': _jnp.float32, 'w_xkv': _jnp.float32, 'w_xo': _jnp.float32, 'norm_ffn_pre': _jnp.float32, 'norm_ffn_post': _jnp.float32, 'w_gate_up': _jnp.float32, 'w_down': _jnp.float32}
MOMENT_SCALE = {'norm_mix_pre': 3.459603e+00, 'norm_mix_post': 1.624245e+01, 'w_in': 1.778864e+00, 'conv_short': 2.169827e+00, 'conv_gdn': 1.452899e+00, 'gdn_a_log': 4.522546e+00, 'gdn_dt_bias': 4.363860e+00, 'gdn_norm': 5.523733e+00, 'w_out': 3.296563e+00, 'norm_mem': 7.724940e+00, 'norm_xattn_pre': 2.167866e+00, 'norm_xattn_post': 1.834533e+01, 'w_xq': 2.212934e+00, 'w_xkv': 5.359139e+00, 'w_xo': 7.425032e+00, 'norm_ffn_pre': 2.835614e+00, 'norm_ffn_post': 1.596097e+01, 'w_gate_up': 1.170472e+00, 'w_down': 2.203234e+00}


def _to_microbatches(a, axis):
    t = _jnp.moveaxis(a, axis, 0)
    t = t.reshape((N_MICROBATCH, t.shape[0] // N_MICROBATCH) + t.shape[1:])
    return _jnp.moveaxis(t, 1, axis + 1)


def setup_inputs(seed: int = 0) -> dict:
    inp = _fwd_setup_inputs(seed)
    key = _jax.random.fold_in(_jax.random.key(seed), 7919)
    shape, _ = _output_shape()
    out = dict(inp)
    out["loss_target"] = _jax.random.normal(_jax.random.fold_in(key, 0), shape, _jnp.float32)
    for i, name in enumerate(TWIN_WEIGHTS):
        w = inp[name].astype(_jnp.float32)
        if MOMENT_SCALE is None:
            s = _jnp.sqrt(_jnp.mean(_jnp.square(w)) + 1e-30)
        else:
            s = MOMENT_SCALE[name]
        km, kv = _jax.random.split(_jax.random.fold_in(key, i + 1))
        out[name] = w
        out["m_" + name] = s * _jax.random.normal(km, w.shape, _jnp.float32)
        out["v_" + name] = (s * s) * _jax.random.uniform(kv, w.shape, _jnp.float32, 0.5, 1.5)
    if N_MICROBATCH > 1:
        for name, axis in PER_EXAMPLE_BATCH_AXIS.items():
            out[name] = _to_microbatches(out[name], axis)
    return {'x': out['x'], 'mem': out['mem'], 'positions': out['positions'], 'norm_mix_pre': out['norm_mix_pre'], 'norm_mix_post': out['norm_mix_post'], 'w_in': out['w_in'], 'conv_short': out['conv_short'], 'conv_gdn': out['conv_gdn'], 'gdn_a_log': out['gdn_a_log'], 'gdn_dt_bias': out['gdn_dt_bias'], 'gdn_norm': out['gdn_norm'], 'w_out': out['w_out'], 'norm_mem': out['norm_mem'], 'norm_xattn_pre': out['norm_xattn_pre'], 'norm_xattn_post': out['norm_xattn_post'], 'w_xq': out['w_xq'], 'w_xkv': out['w_xkv'], 'w_xo': out['w_xo'], 'norm_ffn_pre': out['norm_ffn_pre'], 'norm_ffn_post': out['norm_ffn_post'], 'w_gate_up': out['w_gate_up'], 'w_down': out['w_down'], 'loss_target': out['loss_target'], 'm_norm_mix_pre': out['m_norm_mix_pre'], 'm_norm_mix_post': out['m_norm_mix_post'], 'm_w_in': out['m_w_in'], 'm_conv_short': out['m_conv_short'], 'm_conv_gdn': out['m_conv_gdn'], 'm_gdn_a_log': out['m_gdn_a_log'], 'm_gdn_dt_bias': out['m_gdn_dt_bias'], 'm_gdn_norm': out['m_gdn_norm'], 'm_w_out': out['m_w_out'], 'm_norm_mem': out['m_norm_mem'], 'm_norm_xattn_pre': out['m_norm_xattn_pre'], 'm_norm_xattn_post': out['m_norm_xattn_post'], 'm_w_xq': out['m_w_xq'], 'm_w_xkv': out['m_w_xkv'], 'm_w_xo': out['m_w_xo'], 'm_norm_ffn_pre': out['m_norm_ffn_pre'], 'm_norm_ffn_post': out['m_norm_ffn_post'], 'm_w_gate_up': out['m_w_gate_up'], 'm_w_down': out['m_w_down'], 'v_norm_mix_pre': out['v_norm_mix_pre'], 'v_norm_mix_post': out['v_norm_mix_post'], 'v_w_in': out['v_w_in'], 'v_conv_short': out['v_conv_short'], 'v_conv_gdn': out['v_conv_gdn'], 'v_gdn_a_log': out['v_gdn_a_log'], 'v_gdn_dt_bias': out['v_gdn_dt_bias'], 'v_gdn_norm': out['v_gdn_norm'], 'v_w_out': out['v_w_out'], 'v_norm_mem': out['v_norm_mem'], 'v_norm_xattn_pre': out['v_norm_xattn_pre'], 'v_norm_xattn_post': out['v_norm_xattn_post'], 'v_w_xq': out['v_w_xq'], 'v_w_xkv': out['v_w_xkv'], 'v_w_xo': out['v_w_xo'], 'v_norm_ffn_pre': out['v_norm_ffn_pre'], 'v_norm_ffn_post': out['v_norm_ffn_post'], 'v_w_gate_up': out['v_w_gate_up'], 'v_w_down': out['v_w_down']}


def _loss(weights, diff, rest, loss_target):
    with _jax.named_scope("forward"):
        args = {**rest, TWIN_DIFF_INPUT: diff, **{k: w.astype(_WEIGHT_DTYPES[k]) for k, w in weights.items()}}
        y = _forward(args)
    with _jax.named_scope("loss_head"):
        err = _jnp.square(y.astype(_jnp.float32) - loss_target)
        return 0.5 * _jnp.sum(_jnp.mean(err, axis=-1)) if err.ndim else 0.5 * err


def _adamw(w, g, m, v):
    m = ADAM_B1 * m + (1.0 - ADAM_B1) * g
    v = ADAM_B2 * v + (1.0 - ADAM_B2) * _jnp.square(g)
    m_hat = m / (1.0 - ADAM_B1 ** ADAM_STEP)
    v_hat = v / (1.0 - ADAM_B2 ** ADAM_STEP)
    delta = -ADAM_LR * (m_hat / (_jnp.sqrt(v_hat) + ADAM_EPS) + ADAM_WD * w)
    return delta, m, v


def reference(x, mem, positions, norm_mix_pre, norm_mix_post, w_in, conv_short, conv_gdn, gdn_a_log, gdn_dt_bias, gdn_norm, w_out, norm_mem, norm_xattn_pre, norm_xattn_post, w_xq, w_xkv, w_xo, norm_ffn_pre, norm_ffn_post, w_gate_up, w_down, loss_target, m_norm_mix_pre, m_norm_mix_post, m_w_in, m_conv_short, m_conv_gdn, m_gdn_a_log, m_gdn_dt_bias, m_gdn_norm, m_w_out, m_norm_mem, m_norm_xattn_pre, m_norm_xattn_post, m_w_xq, m_w_xkv, m_w_xo, m_norm_ffn_pre, m_norm_ffn_post, m_w_gate_up, m_w_down, v_norm_mix_pre, v_norm_mix_post, v_w_in, v_conv_short, v_conv_gdn, v_gdn_a_log, v_gdn_dt_bias, v_gdn_norm, v_w_out, v_norm_mem, v_norm_xattn_pre, v_norm_xattn_post, v_w_xq, v_w_xkv, v_w_xo, v_norm_ffn_pre, v_norm_ffn_post, v_w_gate_up, v_w_down):
    given = dict(x=x, mem=mem, positions=positions, norm_mix_pre=norm_mix_pre, norm_mix_post=norm_mix_post, w_in=w_in, conv_short=conv_short, conv_gdn=conv_gdn, gdn_a_log=gdn_a_log, gdn_dt_bias=gdn_dt_bias, gdn_norm=gdn_norm, w_out=w_out, norm_mem=norm_mem, norm_xattn_pre=norm_xattn_pre, norm_xattn_post=norm_xattn_post, w_xq=w_xq, w_xkv=w_xkv, w_xo=w_xo, norm_ffn_pre=norm_ffn_pre, norm_ffn_post=norm_ffn_post, w_gate_up=w_gate_up, w_down=w_down, loss_target=loss_target, m_norm_mix_pre=m_norm_mix_pre, m_norm_mix_post=m_norm_mix_post, m_w_in=m_w_in, m_conv_short=m_conv_short, m_conv_gdn=m_conv_gdn, m_gdn_a_log=m_gdn_a_log, m_gdn_dt_bias=m_gdn_dt_bias, m_gdn_norm=m_gdn_norm, m_w_out=m_w_out, m_norm_mem=m_norm_mem, m_norm_xattn_pre=m_norm_xattn_pre, m_norm_xattn_post=m_norm_xattn_post, m_w_xq=m_w_xq, m_w_xkv=m_w_xkv, m_w_xo=m_w_xo, m_norm_ffn_pre=m_norm_ffn_pre, m_norm_ffn_post=m_norm_ffn_post, m_w_gate_up=m_w_gate_up, m_w_down=m_w_down, v_norm_mix_pre=v_norm_mix_pre, v_norm_mix_post=v_norm_mix_post, v_w_in=v_w_in, v_conv_short=v_conv_short, v_conv_gdn=v_conv_gdn, v_gdn_a_log=v_gdn_a_log, v_gdn_dt_bias=v_gdn_dt_bias, v_gdn_norm=v_gdn_norm, v_w_out=v_w_out, v_norm_mem=v_norm_mem, v_norm_xattn_pre=v_norm_xattn_pre, v_norm_xattn_post=v_norm_xattn_post, v_w_xq=v_w_xq, v_w_xkv=v_w_xkv, v_w_xo=v_w_xo, v_norm_ffn_pre=v_norm_ffn_pre, v_norm_ffn_post=v_norm_ffn_post, v_w_gate_up=v_w_gate_up, v_w_down=v_w_down)
    weights = {n: given[n] for n in TWIN_WEIGHTS}
    shared = {n: given[n] for n in SHARED_INPUTS}
    per_example = {n: given[n] for n in ['x', 'mem', 'positions']}
    grad_fn = _jax.value_and_grad(_loss, argnums=(0, 1))

    def one_microbatch(ex, loss_target):
        ex = dict(ex)
        diff = ex.pop(TWIN_DIFF_INPUT)
        return grad_fn(weights, diff, {**shared, **ex}, loss_target)

    if N_MICROBATCH == 1:
        loss, (grad_w, grad_x) = one_microbatch(per_example, given["loss_target"])
    else:
        def body(carry, xs):
            loss_sum, grad_sum = carry
            l_k, (gw_k, gx_k) = one_microbatch(xs[0], xs[1])
            with _jax.named_scope("update"):
                return (loss_sum + l_k, _jax.tree.map(_jnp.add, grad_sum, gw_k)), gx_k

        init = (_jnp.zeros((), _jnp.float32), _jax.tree.map(_jnp.zeros_like, weights))
        (loss, grad_w), grad_x = _jax.lax.scan(body, init, (per_example, given["loss_target"]))
    with _jax.named_scope("update"):
        delta_w, new_m, new_v = {}, {}, {}
        for n in TWIN_WEIGHTS:
            delta_w[n], new_m[n], new_v[n] = _adamw(weights[n], grad_w[n], given["m_" + n], given["v_" + n])
    return (loss, grad_x, *[grad_w[n] for n in TWIN_WEIGHTS], *[delta_w[n] for n in TWIN_WEIGHTS],
            *[new_m[n] for n in TWIN_WEIGHTS], *[new_v[n] for n in TWIN_WEIGHTS])
```

```python
import functools
import math

import jax
import jax.numpy as jnp
from jax import lax
from jax.experimental import pallas as pl
from jax.experimental.pallas import tpu as pltpu

F32 = jnp.float32
BF16 = jnp.bfloat16
BS = pl.BlockSpec
SDS = jax.ShapeDtypeStruct
HIGHEST = lax.Precision.HIGHEST

D_MODEL = 1024
DEPTH = 4
EPS = 1e-6
ATTN_HEADS = 4
ATTN_HEAD_DIM = 64
ATTN_WIDTH = 256
ROPE_THETA = 500000.0
ROPE_DIM = 16
CONV_WIDTH = 256
CONV_K = 3
GDN_HEADS = 4
GDN_HEAD_DIM = 128
GDN_WIDTH = 512
GDN_CONV_K = 4
GDN_CHUNK = 64
IN_WIDTH = 3592
XATTN_HEADS = 4
XATTN_HEAD_DIM = 256
FFN_HIDDEN = 2816
ADAM_LR = 0.001
ADAM_B1 = 0.9
ADAM_B2 = 0.999
ADAM_EPS = 1e-08
ADAM_WD = 0.01
ADAM_STEP = 10

IN_PAD = 3840
COL_GDN = 1536
COL_GATE = 3072
COL_AB = 3584

VMEM_LIMIT_V7X = 56 * 1024 * 1024
LANE = 128


def _cp(**kw):
    return pltpu.CompilerParams(vmem_limit_bytes=VMEM_LIMIT_V7X, **kw)


def _tile(n, cap):
    if n <= cap:
        return n
    best = None
    for t in range(LANE, cap + 1, LANE):
        if n % t == 0:
            best = t
    assert best is not None, (n, cap)
    return best


def _dot(a, b, ca, cb, precise=False):
    dims = (((ca,), (cb,)), ((), ()))
    if precise:
        return lax.dot_general(a.astype(F32), b.astype(F32), dims, precision=HIGHEST,
                               preferred_element_type=F32)
    return lax.dot_general(a.astype(BF16), b.astype(BF16), dims, preferred_element_type=F32)


def _sigmoid(x):
    return 1.0 / (1.0 + jnp.exp(-x))


def _mm(a, b, *, ta=False, tb=False, out_dtype=F32, name):
    m, k = (a.shape[1], a.shape[0]) if ta else a.shape
    n = b.shape[0] if tb else b.shape[1]
    assert (b.shape[1] if tb else b.shape[0]) == k
    tm = _tile(m, 512)
    tn = _tile(n, max(LANE, min(1024, (4 * 1024 * 1024) // (k * b.dtype.itemsize) // LANE * LANE)))
    ca, cb = (0 if ta else 1), (1 if tb else 0)

    def body(a_ref, b_ref, o_ref):
        o_ref[...] = _dot(a_ref[...], b_ref[...], ca, cb).astype(out_dtype)

    a_spec = BS((k, tm), lambda i, j: (0, i)) if ta else BS((tm, k), lambda i, j: (i, 0))
    b_spec = BS((tn, k), lambda i, j: (j, 0)) if tb else BS((k, tn), lambda i, j: (0, j))
    return pl.pallas_call(
        body, grid=(m // tm, n // tn), in_specs=[a_spec, b_spec],
        out_specs=BS((tm, tn), lambda i, j: (i, j)), out_shape=SDS((m, n), out_dtype),
        compiler_params=_cp(), name=name)(a, b)


def _rmsnorm(x, w, *, name):
    r, d = x.shape
    tr = _tile(r, 512)

    def body(x_ref, w_ref, o_ref):
        xv = x_ref[...]
        rs = lax.rsqrt(jnp.mean(xv * xv, axis=-1, keepdims=True) + EPS)
        o_ref[...] = (xv * rs * w_ref[...]).astype(BF16)

    return pl.pallas_call(
        body, grid=(r // tr,), in_specs=[BS((tr, d), lambda i: (i, 0)), BS((1, d), lambda i: (0, 0))],
        out_specs=BS((tr, d), lambda i: (i, 0)), out_shape=SDS((r, d), BF16),
        compiler_params=_cp(), name=name)(x, w)


def _resnorm(h, m, w, *, name):
    r, d = h.shape
    tr = _tile(r, 512)

    def body(h_ref, m_ref, w_ref, o_ref):
        mv = m_ref[...]
        rs = lax.rsqrt(jnp.mean(mv * mv, axis=-1, keepdims=True) + EPS)
        o_ref[...] = h_ref[...] + mv * rs * w_ref[...]

    row = BS((tr, d), lambda i: (i, 0))
    return pl.pallas_call(
        body, grid=(r // tr,), in_specs=[row, row, BS((1, d), lambda i: (0, 0))],
        out_specs=row, out_shape=SDS((r, d), F32), compiler_params=_cp(), name=name)(h, m, w)


def _rmsnorm_bwd(x, w, dy, res=None, *, name):
    r, d = x.shape
    tr = _tile(r, 512)
    has_res = res is not None

    def body(*refs):
        if has_res:
            x_ref, w_ref, dy_ref, res_ref, dx_ref, dw_ref = refs
        else:
            x_ref, w_ref, dy_ref, dx_ref, dw_ref = refs
        xv = x_ref[...]
        dyv = dy_ref[...].astype(F32)
        rs = lax.rsqrt(jnp.mean(xv * xv, axis=-1, keepdims=True) + EPS)
        nv = xv * rs
        dyw = dyv * w_ref[...]
        dx = rs * (dyw - nv * jnp.mean(dyw * nv, axis=-1, keepdims=True))
        if has_res:
            dx = dx + res_ref[...]
        dx_ref[...] = dx

        @pl.when(pl.program_id(0) == 0)
        def _():
            dw_ref[...] = jnp.zeros_like(dw_ref)

        dw_ref[...] += jnp.sum(dyv * nv, axis=0, keepdims=True)

    row = BS((tr, d), lambda i: (i, 0))
    vec = BS((1, d), lambda i: (0, 0))
    ins = [x, w, dy] + ([res] if has_res else [])
    return pl.pallas_call(
        body, grid=(r // tr,), in_specs=[row, vec, row] + ([row] if has_res else []),
        out_specs=[row, vec], out_shape=[SDS((r, d), F32), SDS((1, d), F32)],
        compiler_params=_cp(), name=name)(*ins)


def _swiglu(gu, *, name):
    r, h2 = gu.shape
    hid = h2 // 2
    tr, tc = _tile(r, 512), _tile(hid, 1408)
    nb = hid // tc

    def body(g_ref, u_ref, o_ref):
        g = g_ref[...]
        o_ref[...] = (g * _sigmoid(g) * u_ref[...]).astype(BF16)

    return pl.pallas_call(
        body, grid=(r // tr, nb),
        in_specs=[BS((tr, tc), lambda i, j: (i, j)), BS((tr, tc), lambda i, j: (i, j + nb))],
        out_specs=BS((tr, tc), lambda i, j: (i, j)), out_shape=SDS((r, hid), BF16),
        compiler_params=_cp(), name=name)(gu, gu)


def _swiglu_bwd(gu, dact, *, name):
    r, h2 = gu.shape
    hid = h2 // 2
    tr, tc = _tile(r, 512), _tile(hid, 1408)
    nb = hid // tc

    def body(g_ref, u_ref, d_ref, o_ref):
        g = g_ref[...]
        da = d_ref[...]
        sg = _sigmoid(g)
        dgate = da * u_ref[...] * sg * (1.0 + g * (1.0 - sg))
        dup = da * g * sg
        o_ref[...] = jnp.where(pl.program_id(1) < nb, dgate, dup).astype(BF16)

    return pl.pallas_call(
        body, grid=(r // tr, 2 * nb),
        in_specs=[BS((tr, tc), lambda i, j: (i, j % nb)), BS((tr, tc), lambda i, j: (i, j % nb + nb)),
                  BS((tr, tc), lambda i, j: (i, j % nb))],
        out_specs=BS((tr, tc), lambda i, j: (i, j)), out_shape=SDS((r, h2), BF16),
        compiler_params=_cp(), name=name)(gu, gu, dact)


def _loss_grad(h, target, *, name):
    r, d = h.shape
    tr = _tile(r, 512)

    def body(h_ref, t_ref, l_ref, g_ref):
        e = h_ref[...] - t_ref[...]
        g_ref[...] = e * (1.0 / d)

        @pl.when(pl.program_id(0) == 0)
        def _():
            l_ref[...] = jnp.zeros_like(l_ref)

        l_ref[...] += jnp.full((1, LANE), 0.5 / d, F32) * jnp.sum(e * e)

    row = BS((tr, d), lambda i: (i, 0))
    return pl.pallas_call(
        body, grid=(r // tr,), in_specs=[row, row],
        out_specs=[BS((1, LANE), lambda i: (0, 0)), row],
        out_shape=[SDS((1, LANE), F32), SDS((r, d), F32)], compiler_params=_cp(), name=name)(h, target)


def _adamw(w, g, m, v, *, name):
    r, c = w.shape
    tr = r if r <= 512 else _rows_tile(r)
    bc1 = 1.0 - ADAM_B1 ** ADAM_STEP
    bc2 = 1.0 - ADAM_B2 ** ADAM_STEP

    def body(w_ref, g_ref, m_ref, v_ref, d_ref, nm_ref, nv_ref):
        gv = g_ref[...]
        nm = ADAM_B1 * m_ref[...] + (1.0 - ADAM_B1) * gv
        nv = ADAM_B2 * v_ref[...] + (1.0 - ADAM_B2) * (gv * gv)
        d_ref[...] = -ADAM_LR * ((nm / bc1) / (jnp.sqrt(nv / bc2) + ADAM_EPS) + ADAM_WD * w_ref[...])
        nm_ref[...] = nm
        nv_ref[...] = nv

    blk = BS((tr, c), lambda i: (i, 0))
    return pl.pallas_call(
        body, grid=(r // tr,), in_specs=[blk] * 4, out_specs=[blk] * 3,
        out_shape=[SDS((r, c), F32)] * 3, compiler_params=_cp(), name=name)(w, g, m, v)


def _rows_tile(r):
    for t in (512, 256, 128, 64, 32, 16, 8):
        if r % t == 0:
            return t
    return r


def _rope_tables(pos, *, name):
    s = pos.shape[0]
    half = ROPE_DIM // 2

    def body(p_ref, c_ref, a_ref, b_ref):
        lane = lax.broadcasted_iota(jnp.int32, (s, ATTN_WIDTH), 1) & (ATTN_HEAD_DIM - 1)
        fi = (lane & (half - 1)).astype(F32)
        inv_freq = jnp.exp(fi * (-2.0 * math.log(ROPE_THETA) / ROPE_DIM))
        ang = p_ref[...].astype(F32) * inv_freq
        cs, sn = jnp.cos(ang), jnp.sin(ang)
        c_ref[...] = jnp.where(lane < ROPE_DIM, cs, 1.0)
        a_ref[...] = jnp.where(lane < half, -sn, 0.0)
        b_ref[...] = jnp.where((lane >= half) & (lane < ROPE_DIM), sn, 0.0)

    full = BS((s, ATTN_WIDTH), lambda i: (0, 0))
    return pl.pallas_call(
        body, grid=(1,), in_specs=[BS((s, 1), lambda i: (0, 0))], out_specs=[full] * 3,
        out_shape=[SDS((s, ATTN_WIDTH), F32)] * 3, compiler_params=_cp(), name=name)(pos)


def _rot(x, c, a, b):
    w = x.shape[1]
    return x * c + pltpu.roll(x, w - ROPE_DIM // 2, 1) * a + pltpu.roll(x, ROPE_DIM // 2, 1) * b


def _rot_t(dy, c, a, b):
    w = dy.shape[1]
    return dy * c + pltpu.roll(dy * a, ROPE_DIM // 2, 1) + pltpu.roll(dy * b, w - ROPE_DIM // 2, 1)


def _attn_count(q0, tq, s):
    dist = (lax.broadcasted_iota(jnp.int32, (tq, s), 0) + q0) - lax.broadcasted_iota(jnp.int32, (tq, s), 1)
    cnt = ((dist <= 128).astype(F32) + (((dist & 3) == 0) & (dist <= 512)).astype(F32)
           + ((dist & 15) == 0).astype(F32))
    return jnp.where(dist >= 0, cnt, 0.0)


def _attn_probs(qm, k, cnt):
    sc = _dot(qm, k, 1, 1)
    mx = jnp.max(jnp.where(cnt > 0.0, sc, -1e30), axis=-1, keepdims=True)
    p = cnt * jnp.exp(jnp.minimum(sc - mx, 0.0))
    return p / jnp.sum(p, axis=-1, keepdims=True)


ATTN_TQ = 256


def _attn_specs(s, tq):
    def qblk(col):
        return BS((tq, ATTN_WIDTH), lambda i: (i, col))

    def full(col):
        return BS((s, ATTN_WIDTH), lambda i: (0, col))

    return qblk, full


def _attn_fwd(proj, tabs, *, name):
    s = proj.shape[0]
    tq = ATTN_TQ
    qblk, full = _attn_specs(s, tq)
    scale = ATTN_HEAD_DIM ** -0.5

    def body(q_ref, k_ref, v_ref, cq, aq, bq, ck, ak, bk, o_ref):
        q = _rot(q_ref[...], cq[...], aq[...], bq[...]) * scale
        k = _rot(k_ref[...], ck[...], ak[...], bk[...]).astype(BF16)
        v = v_ref[...].astype(BF16)
        cnt = _attn_count(pl.program_id(0) * tq, tq, s)
        head = lax.broadcasted_iota(jnp.int32, (1, ATTN_WIDTH), 1) >> 6
        acc = jnp.zeros((tq, ATTN_WIDTH), F32)
        for h in range(ATTN_HEADS):
            hm = (head == h).astype(F32)
            p = _attn_probs(q * hm, k, cnt)
            acc = acc + _dot(p, v, 1, 0) * hm
        o_ref[...] = acc.astype(BF16)

    return pl.pallas_call(
        body, grid=(s // tq,),
        in_specs=[qblk(0), full(1), full(2), qblk(0), qblk(0), qblk(0), full(0), full(0), full(0)],
        out_specs=BS((tq, ATTN_WIDTH), lambda i: (i, 0)), out_shape=SDS((s, ATTN_WIDTH), BF16),
        compiler_params=_cp(), name=name)(proj, proj, proj, *tabs, *tabs)


def _attn_bwd(proj, tabs, dcat, *, name):
    s = proj.shape[0]
    tq = ATTN_TQ
    nq = s // tq
    qblk, full = _attn_specs(s, tq)
    scale = ATTN_HEAD_DIM ** -0.5

    def body(q_ref, k_ref, v_ref, cq, aq, bq, ck, ak, bk, dy_ref, dq_ref, dk_ref, dv_ref, dk_acc, dv_acc):
        i = pl.program_id(0)

        @pl.when(i == 0)
        def _():
            dk_acc[...] = jnp.zeros_like(dk_acc)
            dv_acc[...] = jnp.zeros_like(dv_acc)

        q = _rot(q_ref[...], cq[...], aq[...], bq[...]) * scale
        k = _rot(k_ref[...], ck[...], ak[...], bk[...]).astype(BF16)
        v = v_ref[...].astype(BF16)
        dy = dy_ref[...].astype(F32)
        cnt = _attn_count(i * tq, tq, s)
        head = lax.broadcasted_iota(jnp.int32, (1, ATTN_WIDTH), 1) >> 6
        dq = jnp.zeros((tq, ATTN_WIDTH), F32)
        for h in range(ATTN_HEADS):
            hm = (head == h).astype(F32)
            qm = q * hm
            dym = dy * hm
            p = _attn_probs(qm, k, cnt)
            dp = _dot(dym, v, 1, 1)
            ds = p * (dp - jnp.sum(p * dp, axis=-1, keepdims=True))
            dq = dq + _dot(ds, k, 1, 0) * hm
            dk_acc[...] += _dot(ds, qm, 0, 0)
            dv_acc[...] += _dot(p, dym, 0, 0)
        dq_ref[...] = _rot_t(dq * scale, cq[...], aq[...], bq[...])

        @pl.when(i == nq - 1)
        def _():
            dk_ref[...] = _rot_t(dk_acc[...], ck[...], ak[...], bk[...])
            dv_ref[...] = dv_acc[...]

    whole = BS((s, ATTN_WIDTH), lambda i: (0, 0))
    dq, dk, dv = pl.pallas_call(
        body, grid=(nq,),
        in_specs=[qblk(0), full(1), full(2), qblk(0), qblk(0), qblk(0), full(0), full(0), full(0), qblk(0)],
        out_specs=[BS((tq, ATTN_WIDTH), lambda i: (i, 0)), whole, whole],
        out_shape=[SDS((s, ATTN_WIDTH), F32)] * 3,
        scratch_shapes=[pltpu.VMEM((s, ATTN_WIDTH), F32), pltpu.VMEM((s, ATTN_WIDTH), F32)],
        compiler_params=_cp(), name=name)(proj, proj, proj, *tabs, *tabs, dcat)
    return dq, dk, dv


def _shift_down(x, n):
    if n == 0:
        return x
    rows = lax.broadcasted_iota(jnp.int32, x.shape, 0)
    return jnp.where(rows >= n, pltpu.roll(x, n, 0), 0.0)


def _shift_up(x, n):
    if n == 0:
        return x
    t = x.shape[0]
    rows = lax.broadcasted_iota(jnp.int32, x.shape, 0)
    return jnp.where(rows < t - n, pltpu.roll(x, t - n, 0), 0.0)


def _conv_fwd(z, w, kk):
    y = z * w[kk - 1:kk, :]
    for j in range(kk - 1):
        y = y + _shift_down(z, kk - 1 - j) * w[j:j + 1, :]
    return y


def _conv_bwd(z, dy, w, kk):
    dz = dy * w[kk - 1:kk, :]
    dws = []
    for j in range(kk - 1):
        dz = dz + _shift_up(dy, kk - 1 - j) * w[j:j + 1, :]
        dws.append(jnp.sum(dy * _shift_down(z, kk - 1 - j), axis=0, keepdims=True))
    dws.append(jnp.sum(dy * z, axis=0, keepdims=True))
    return dz, jnp.concatenate(dws, axis=0)


def _sconv_fwd(proj, w, *, name):
    s = proj.shape[0]

    def body(b_ref, c_ref, x_ref, w_ref, o_ref):
        y = _conv_fwd(c_ref[...] * x_ref[...], w_ref[...], CONV_K)
        o_ref[...] = (b_ref[...] * y).astype(BF16)

    def col(j):
        return BS((s, LANE), lambda i: (0, j + i))

    return pl.pallas_call(
        body, grid=(CONV_WIDTH // LANE,), in_specs=[col(6), col(8), col(10), BS((CONV_K, LANE), lambda i: (0, i))],
        out_specs=BS((s, LANE), lambda i: (0, i)), out_shape=SDS((s, CONV_WIDTH), BF16),
        compiler_params=_cp(), name=name)(proj, proj, proj, w)


def _sconv_bwd(proj, w, dcat, *, name):
    s = proj.shape[0]

    def body(b_ref, c_ref, x_ref, w_ref, dy_ref, db_ref, dc_ref, dx_ref, dw_ref):
        cv, xv, wv = c_ref[...], x_ref[...], w_ref[...]
        dy = dy_ref[...].astype(F32)
        z = cv * xv
        db_ref[...] = dy * _conv_fwd(z, wv, CONV_K)
        dz, dw = _conv_bwd(z, dy * b_ref[...], wv, CONV_K)
        dc_ref[...] = dz * xv
        dx_ref[...] = dz * cv
        dw_ref[...] = dw

    def col(j):
        return BS((s, LANE), lambda i: (0, j + i))

    out = BS((s, LANE), lambda i: (0, i))
    wspec = BS((CONV_K, LANE), lambda i: (0, i))
    return pl.pallas_call(
        body, grid=(CONV_WIDTH // LANE,), in_specs=[col(6), col(8), col(10), wspec, col(2)],
        out_specs=[out, out, out, wspec],
        out_shape=[SDS((s, CONV_WIDTH), F32)] * 3 + [SDS((CONV_K, CONV_WIDTH), F32)],
        compiler_params=_cp(), name=name)(proj, proj, proj, w, dcat)


def _l2n(y, scale):
    r = lax.rsqrt(jnp.sum(y * y, axis=-1, keepdims=True) + EPS)
    return y * r * scale, r


def _gdn_pre_fwd(proj, w, *, name):
    s = proj.shape[0]
    nh = GDN_HEADS

    def body(x_ref, w_ref, o_ref):
        j = pl.program_id(0)
        c = _conv_fwd(x_ref[...], w_ref[...], GDN_CONV_K)
        y = c * _sigmoid(c)
        scale = jnp.where(j < nh, GDN_HEAD_DIM ** -0.5, 1.0)
        n, _ = _l2n(y, scale)
        o_ref[...] = jnp.where(j < 2 * nh, n, y)

    return pl.pallas_call(
        body, grid=(3 * nh,),
        in_specs=[BS((s, LANE), lambda j: (0, COL_GDN // LANE + j)), BS((GDN_CONV_K, LANE), lambda j: (0, j))],
        out_specs=BS((s, LANE), lambda j: (0, j)), out_shape=SDS((s, 3 * GDN_WIDTH), F32),
        compiler_params=_cp(), name=name)(proj, w)


def _gdn_pre_bwd(proj, w, dqkv, *, name):
    s = proj.shape[0]
    nh = GDN_HEADS

    def body(x_ref, w_ref, d_ref, dx_ref, dw_ref):
        j = pl.program_id(0)
        xv, wv, dn = x_ref[...], w_ref[...], d_ref[...]
        c = _conv_fwd(xv, wv, GDN_CONV_K)
        sg = _sigmoid(c)
        y = c * sg
        scale = jnp.where(j < nh, GDN_HEAD_DIM ** -0.5, 1.0)
        n, r = _l2n(y, 1.0)
        dns = dn * scale
        dy_norm = r * (dns - n * jnp.sum(dns * n, axis=-1, keepdims=True))
        dy = jnp.where(j < 2 * nh, dy_norm, dn)
        dc = dy * sg * (1.0 + c * (1.0 - sg))
        dx, dw = _conv_bwd(xv, dc, wv, GDN_CONV_K)
        dx_ref[...] = dx
        dw_ref[...] = dw

    wspec = BS((GDN_CONV_K, LANE), lambda j: (0, j))
    blk = BS((s, LANE), lambda j: (0, j))
    return pl.pallas_call(
        body, grid=(3 * nh,),
        in_specs=[BS((s, LANE), lambda j: (0, COL_GDN // LANE + j)), wspec, blk],
        out_specs=[blk, wspec], out_shape=[SDS((s, 3 * GDN_WIDTH), F32), SDS((GDN_CONV_K, 3 * GDN_WIDTH), F32)],
        compiler_params=_cp(), name=name)(proj, w, dqkv)


def _softplus(x):
    return jnp.maximum(x, 0.0) + jnp.log(1.0 + jnp.exp(-jnp.abs(x)))


def _gdn_gates_fwd(proj, a_log, dt_bias, *, name):
    s = proj.shape[0]

    def body(x_ref, al_ref, dt_ref, o_ref):
        xv = x_ref[...]
        lane = lax.broadcasted_iota(jnp.int32, xv.shape, 1)
        g = -jnp.exp(al_ref[...]) * _softplus(xv + dt_ref[...])
        o_ref[...] = jnp.where(lane < GDN_HEADS, g, jnp.where(lane < 2 * GDN_HEADS, _sigmoid(xv), 0.0))

    vec = BS((1, LANE), lambda i: (0, 0))
    return pl.pallas_call(
        body, grid=(1,), in_specs=[BS((s, LANE), lambda i: (0, COL_AB // LANE)), vec, vec],
        out_specs=BS((s, LANE), lambda i: (0, 0)), out_shape=SDS((s, LANE), F32),
        compiler_params=_cp(), name=name)(proj, a_log, dt_bias)


def _gdn_gates_bwd(proj, a_log, dt_bias, dgb, *, name):
    s = proj.shape[0]

    def body(x_ref, al_ref, dt_ref, d_ref, dx_ref, dal_ref, ddt_ref):
        xv, dv = x_ref[...], d_ref[...]
        lane = lax.broadcasted_iota(jnp.int32, xv.shape, 1)
        is_g = lane < GDN_HEADS
        ea = -jnp.exp(al_ref[...])
        z = xv + dt_ref[...]
        da = jnp.where(is_g, dv * ea * _sigmoid(z), 0.0)
        beta = _sigmoid(xv)
        dx_ref[...] = jnp.where(is_g, da, jnp.where(lane < 2 * GDN_HEADS, dv * beta * (1.0 - beta), 0.0))
        dal_ref[...] = jnp.sum(jnp.where(is_g, dv * ea * _softplus(z), 0.0), axis=0, keepdims=True)
        ddt_ref[...] = jnp.sum(da, axis=0, keepdims=True)

    vec = BS((1, LANE), lambda i: (0, 0))
    blk = BS((s, LANE), lambda i: (0, 0))
    return pl.pallas_call(
        body, grid=(1,), in_specs=[BS((s, LANE), lambda i: (0, COL_AB // LANE)), vec, vec, blk],
        out_specs=[blk, vec, vec], out_shape=[SDS((s, LANE), F32), SDS((1, LANE), F32), SDS((1, LANE), F32)],
        compiler_params=_cp(), name=name)(proj, a_log, dt_bias, dgb)


def _col_to_row(col, eye):
    return jnp.sum(jnp.where(eye, col, 0.0), axis=0, keepdims=True)


def _row_to_col(row, eye):
    return jnp.sum(jnp.where(eye, row, 0.0), axis=1, keepdims=True)


def _tri_inv(a, eye_f):
    x = eye_f - a
    p = a
    for _ in range(5):
        p = _dot(p, p, 1, 0, precise=True)
        x = x + _dot(x, p, 1, 0, precise=True)
    return x


def _gdn_chunk(q, k, v, g, beta, st):
    c = q.shape[0]
    row = lax.broadcasted_iota(jnp.int32, (c, c), 0)
    col = lax.broadcasted_iota(jnp.int32, (c, c), 1)
    eye = row == col
    low = row >= col
    g_row = _col_to_row(g, eye)
    d_col = jnp.sum(jnp.where(low, g_row, 0.0), axis=1, keepdims=True)
    d_row = jnp.sum(jnp.where(row <= col, g, 0.0), axis=0, keepdims=True)
    rel = jnp.where(low, jnp.exp(jnp.minimum(d_col - d_row, 0.0)), 0.0)
    d_last = d_col[c - 1:c, :]
    e = jnp.exp(d_col)
    f = jnp.exp(d_last - d_col)
    cd = jnp.exp(d_last)
    kb = k * beta
    kk = _dot(kb, k, 1, 1)
    a = jnp.where(row > col, kk * rel, 0.0)
    t = _tri_inv(a, eye.astype(F32))
    vb = v * beta
    kbe = kb * e
    u = _dot(t, vb, 1, 0)
    w = _dot(t, kbe, 1, 0)
    qk = _dot(q, k, 1, 1)
    attn = qk * rel
    qd = q * e
    kd = k * f
    vn = u - _dot(w, st, 1, 0)
    o = _dot(qd, st, 1, 0) + _dot(attn, vn, 1, 0)
    st_new = cd * st + _dot(kd, vn, 0, 0)
    res = dict(row=row, col=col, eye=eye, low=low, rel=rel, e=e, f=f, cd=cd, kb=kb, kk=kk, t=t, vb=vb,
               kbe=kbe, u=u, w=w, qk=qk, attn=attn, qd=qd, kd=kd, vn=vn)
    return o, st_new, res


def _gdn_chunk_bwd(q, k, v, beta, st, r, do, dst):
    c = q.shape[0]
    row, col, eye, low, rel = r["row"], r["col"], r["eye"], r["low"], r["rel"]
    e, f, cd, kb, t, w, vn, attn = r["e"], r["f"], r["cd"], r["kb"], r["t"], r["w"], r["vn"], r["attn"]
    dvn = _dot(attn, do, 0, 0) + _dot(r["kd"], dst, 1, 0)
    dkd = _dot(vn, dst, 1, 1)
    dcd = jnp.sum(st * dst)
    dst_in = cd * dst + _dot(r["qd"], do, 0, 0) - _dot(w, dvn, 0, 0)
    dqd = _dot(do, st, 1, 1)
    dattn = jnp.where(low, _dot(do, vn, 1, 1), 0.0)
    du = dvn
    dw = -_dot(dvn, st, 1, 1)
    dvb = _dot(t, du, 0, 0)
    dkbe = _dot(t, dw, 0, 0)
    da = jnp.where(row > col, -(_dot(dvb, r["u"], 1, 1) + _dot(dkbe, w, 1, 1)), 0.0)
    dkk = da * rel
    dqk = dattn * rel
    grel = (da * r["kk"] + dattn * r["qk"]) * rel
    dkb = _dot(dkk, k, 1, 0) + dkbe * e
    dk = _dot(dkk, kb, 0, 0) + _dot(dqk, q, 0, 0) + dkd * f + dkb * beta
    dq = _dot(dqk, k, 1, 0) + dqd * e
    dv = dvb * beta
    dbeta = jnp.sum(dkb * k, axis=1, keepdims=True) + jnp.sum(dvb * v, axis=1, keepdims=True)
    de = jnp.sum(dqd * q, axis=1, keepdims=True) + jnp.sum(dkbe * kb, axis=1, keepdims=True)
    dff = jnp.sum(dkd * k, axis=1, keepdims=True) * f
    dd = de * e - dff + jnp.sum(grel, axis=1, keepdims=True) - _row_to_col(jnp.sum(grel, axis=0, keepdims=True), eye)
    rows1 = lax.broadcasted_iota(jnp.int32, (c, 1), 0)
    dd = dd + jnp.where(rows1 == c - 1, jnp.sum(dff) + dcd * cd, 0.0)
    dg = jnp.sum(jnp.where(col >= row, _col_to_row(dd, eye), 0.0), axis=1, keepdims=True)
    return dq, dk, dv, dg, dbeta, dst_in


def _gdn_specs(c):
    def qkv(j):
        return BS((c, GDN_WIDTH), lambda n: (n, j))

    return qkv


def _gdn_core_fwd(qkv, gbeta, proj, norm_w, *, name):
    s = qkv.shape[0]
    c, nh, hd = GDN_CHUNK, GDN_HEADS, GDN_HEAD_DIM
    n_chunks = s // c
    blk = _gdn_specs(c)

    def body(q_ref, k_ref, v_ref, gb_ref, gate_ref, nw_ref, y_ref, st_ref, state):
        @pl.when(pl.program_id(0) == 0)
        def _():
            state[...] = jnp.zeros_like(state)

        gb = gb_ref[...]
        for h in range(nh):
            ls = slice(h * hd, (h + 1) * hd)
            st = state[ls, :]
            st_ref[ls, :] = st
            o, st_new, _ = _gdn_chunk(q_ref[:, ls], k_ref[:, ls], v_ref[:, ls], gb[:, h:h + 1],
                                      gb[:, nh + h:nh + h + 1], st)
            state[ls, :] = st_new
            gate = gate_ref[:, ls]
            rs = lax.rsqrt(jnp.mean(o * o, axis=-1, keepdims=True) + EPS)
            y_ref[:, ls] = (o * rs * nw_ref[...] * (gate * _sigmoid(gate))).astype(BF16)

    return pl.pallas_call(
        body, grid=(n_chunks,),
        in_specs=[blk(0), blk(1), blk(2), BS((c, LANE), lambda n: (n, 0)),
                  BS((c, GDN_WIDTH), lambda n: (n, COL_GATE // GDN_WIDTH)), BS((1, hd), lambda n: (0, 0))],
        out_specs=[BS((c, GDN_WIDTH), lambda n: (n, 0)), BS((nh * hd, hd), lambda n: (n, 0))],
        out_shape=[SDS((s, GDN_WIDTH), BF16), SDS((n_chunks * nh * hd, hd), F32)],
        scratch_shapes=[pltpu.VMEM((nh * hd, hd), F32)],
        compiler_params=_cp(), name=name)(qkv, qkv, qkv, gbeta, proj, norm_w)


def _gdn_core_bwd(qkv, gbeta, proj, norm_w, states, dcat, *, name):
    s = qkv.shape[0]
    c, nh, hd = GDN_CHUNK, GDN_HEADS, GDN_HEAD_DIM
    n_chunks = s // c
    last = n_chunks - 1

    def rev(j, w):
        return BS((c, w), lambda n: (last - n, j))

    def body(q_ref, k_ref, v_ref, gb_ref, gate_ref, nw_ref, st_ref, dy_ref,
             dqkv_ref, dgb_ref, dgate_ref, dnw_ref, dstate):
        @pl.when(pl.program_id(0) == 0)
        def _():
            dstate[...] = jnp.zeros_like(dstate)
            dnw_ref[...] = jnp.zeros_like(dnw_ref)

        gb = gb_ref[...]
        nw = nw_ref[...]
        lane = lax.broadcasted_iota(jnp.int32, (c, LANE), 1)
        dgb = jnp.zeros((c, LANE), F32)
        for h in range(nh):
            ls = slice(h * hd, (h + 1) * hd)
            q, k, v = q_ref[:, ls], k_ref[:, ls], v_ref[:, ls]
            beta = gb[:, nh + h:nh + h + 1]
            st = st_ref[ls, :]
            o, _, res = _gdn_chunk(q, k, v, gb[:, h:h + 1], beta, st)
            gate = gate_ref[:, ls]
            dy = dy_ref[:, ls].astype(F32)
            sg = _sigmoid(gate)
            rs = lax.rsqrt(jnp.mean(o * o, axis=-1, keepdims=True) + EPS)
            nrm = o * rs
            dgate_ref[:, ls] = dy * nrm * nw * sg * (1.0 + gate * (1.0 - sg))
            dnv = dy * (gate * sg)
            dnw_ref[...] += jnp.sum(dnv * nrm, axis=0, keepdims=True)
            dno = dnv * nw
            do = rs * (dno - nrm * jnp.mean(dno * nrm, axis=-1, keepdims=True))
            dq, dk, dv, dg, dbeta, dst_in = _gdn_chunk_bwd(q, k, v, beta, st, res, do, dstate[ls, :])
            dstate[ls, :] = dst_in
            dqkv_ref[:, ls] = dq
            dqkv_ref[:, slice(GDN_WIDTH + h * hd, GDN_WIDTH + (h + 1) * hd)] = dk
            dqkv_ref[:, slice(2 * GDN_WIDTH + h * hd, 2 * GDN_WIDTH + (h + 1) * hd)] = dv
            dgb = jnp.where(lane == h, dg, jnp.where(lane == nh + h, dbeta, dgb))
        dgb_ref[...] = dgb

    return pl.pallas_call(
        body, grid=(n_chunks,),
        in_specs=[rev(0, GDN_WIDTH), rev(1, GDN_WIDTH), rev(2, GDN_WIDTH), rev(0, LANE),
                  rev(COL_GATE // GDN_WIDTH, GDN_WIDTH), BS((1, hd), lambda n: (0, 0)),
                  BS((nh * hd, hd), lambda n: (last - n, 0)), rev(1, GDN_WIDTH)],
        out_specs=[rev(0, 3 * GDN_WIDTH), rev(0, LANE), rev(0, GDN_WIDTH), BS((1, hd), lambda n: (0, 0))],
        out_shape=[SDS((s, 3 * GDN_WIDTH), F32), SDS((s, LANE), F32), SDS((s, GDN_WIDTH), F32), SDS((1, hd), F32)],
        scratch_shapes=[pltpu.VMEM((nh * hd, hd), F32)],
        compiler_params=_cp(), name=name)(qkv, qkv, qkv, gbeta, proj, norm_w, states, dcat)


XATTN_TQ = 512


def _xattn_probs(qh, kh):
    sc = _dot(qh, kh, 1, 1) * (XATTN_HEAD_DIM ** -0.5)
    p = jnp.exp(sc - jnp.max(sc, axis=-1, keepdims=True))
    return p / jnp.sum(p, axis=-1, keepdims=True)


def _xattn_fwd(q, kv, *, name):
    s, d = q.shape
    m = kv.shape[0]
    tq, hd = _tile(s, XATTN_TQ), XATTN_HEAD_DIM

    def body(q_ref, k_ref, v_ref, o_ref):
        for h in range(XATTN_HEADS):
            ls = slice(h * hd, (h + 1) * hd)
            p = _xattn_probs(q_ref[:, ls], k_ref[:, ls])
            o_ref[:, ls] = _dot(p, v_ref[:, ls], 1, 0).astype(BF16)

    return pl.pallas_call(
        body, grid=(s // tq,),
        in_specs=[BS((tq, d), lambda i: (i, 0)), BS((m, d), lambda i: (0, 0)), BS((m, d), lambda i: (0, 1))],
        out_specs=BS((tq, d), lambda i: (i, 0)), out_shape=SDS((s, d), BF16),
        compiler_params=_cp(), name=name)(q, kv, kv)


def _xattn_bwd(q, kv, do, *, name):
    s, d = q.shape
    m = kv.shape[0]
    tq, hd = _tile(s, XATTN_TQ), XATTN_HEAD_DIM
    scale = hd ** -0.5

    def body(q_ref, k_ref, v_ref, do_ref, dq_ref, dkv_ref):
        @pl.when(pl.program_id(0) == 0)
        def _():
            dkv_ref[...] = jnp.zeros_like(dkv_ref)

        for h in range(XATTN_HEADS):
            ls = slice(h * hd, (h + 1) * hd)
            vs = slice(d + h * hd, d + (h + 1) * hd)
            qh, kh, doh = q_ref[:, ls], k_ref[:, ls], do_ref[:, ls]
            p = _xattn_probs(qh, kh)
            dp = _dot(doh, v_ref[:, ls], 1, 1)
            ds = p * (dp - jnp.sum(p * dp, axis=-1, keepdims=True)) * scale
            dq_ref[:, ls] = _dot(ds, kh, 1, 0).astype(BF16)
            dkv_ref[:, ls] += _dot(ds, qh, 0, 0)
            dkv_ref[:, vs] += _dot(p, doh, 0, 0)

    row = BS((tq, d), lambda i: (i, 0))
    return pl.pallas_call(
        body, grid=(s // tq,),
        in_specs=[row, BS((m, d), lambda i: (0, 0)), BS((m, d), lambda i: (0, 1)), row],
        out_specs=[row, BS((m, 2 * d), lambda i: (0, 0))],
        out_shape=[SDS((s, d), BF16), SDS((m, 2 * d), F32)],
        compiler_params=_cp(), name=name)(q, kv, kv, do)


def _pad_lanes(vec4):
    return jnp.zeros((1, LANE), F32).at[0, :GDN_HEADS].set(vec4)


def _layer_fwd(h0, mem, tabs, p):
    sv = dict(h0=h0)
    hn1 = _rmsnorm(h0, p["norm_mix_pre"], name="norm_mix_pre")
    proj = _mm(hn1, p["w_in"], name="mm_in")
    ya = _attn_fwd(proj, tabs, name="attn_fwd")
    yc = _sconv_fwd(proj, p["conv_short"], name="sconv_fwd")
    qkv = _gdn_pre_fwd(proj, p["conv_gdn"], name="gdn_pre_fwd")
    gbeta = _gdn_gates_fwd(proj, p["gdn_a_log"], p["gdn_dt_bias"], name="gdn_gates_fwd")
    yg, states = _gdn_core_fwd(qkv, gbeta, proj, p["gdn_norm"], name="gdn_core_fwd")
    cat = jnp.concatenate([ya, yc, yg], axis=-1)
    mix = _mm(cat, p["w_out"], name="mm_out")
    h1 = _resnorm(h0, mix, p["norm_mix_post"], name="norm_mix_post")
    hn2 = _rmsnorm(h1, p["norm_xattn_pre"], name="norm_xattn_pre")
    memn = _rmsnorm(mem, p["norm_mem"], name="norm_mem")
    xq = _mm(hn2, p["w_xq"], out_dtype=BF16, name="mm_xq")
    kv = _mm(memn, p["w_xkv"], out_dtype=BF16, name="mm_xkv")
    xo = _xattn_fwd(xq, kv, name="xattn_fwd")
    xa = _mm(xo, p["w_xo"], name="mm_xo")
    h2 = _resnorm(h1, xa, p["norm_xattn_post"], name="norm_xattn_post")
    hn3 = _rmsnorm(h2, p["norm_ffn_pre"], name="norm_ffn_pre")
    gu = _mm(hn3, p["w_gate_up"], name="mm_gate_up")
    act = _swiglu(gu, name="swiglu_fwd")
    f = _mm(act, p["w_down"], name="mm_down")
    h3 = _resnorm(h2, f, p["norm_ffn_post"], name="norm_ffn_post")
    sv.update(hn1=hn1, proj=proj, qkv=qkv, gbeta=gbeta, states=states, cat=cat, mix=mix, h1=h1, hn2=hn2,
              memn=memn, xq=xq, kv=kv, xo=xo, xa=xa, h2=h2, hn3=hn3, gu=gu, act=act, f=f)
    return h3, sv


def _layer_bwd(dh3, mem, tabs, p, sv):
    g = {}
    df, g["norm_ffn_post"] = _rmsnorm_bwd(sv["f"], p["norm_ffn_post"], dh3, name="norm_ffn_post_bwd")
    dact = _mm(df, p["w_down"], tb=True, name="mm_down_da")
    g["w_down"] = _mm(sv["act"], df, ta=True, name="mm_down_dw")
    dgu = _swiglu_bwd(sv["gu"], dact, name="swiglu_bwd")
    dhn3 = _mm(dgu, p["w_gate_up"], tb=True, name="mm_gate_up_da")
    g["w_gate_up"] = _mm(sv["hn3"], dgu, ta=True, name="mm_gate_up_dw")
    dh2, g["norm_ffn_pre"] = _rmsnorm_bwd(sv["h2"], p["norm_ffn_pre"], dhn3, res=dh3, name="norm_ffn_pre_bwd")
    dxa, g["norm_xattn_post"] = _rmsnorm_bwd(sv["xa"], p["norm_xattn_post"], dh2, name="norm_xattn_post_bwd")
    dxo = _mm(dxa, p["w_xo"], tb=True, name="mm_xo_da")
    g["w_xo"] = _mm(sv["xo"], dxa, ta=True, name="mm_xo_dw")
    dxq, dkv = _xattn_bwd(sv["xq"], sv["kv"], dxo, name="xattn_bwd")
    dhn2 = _mm(dxq, p["w_xq"], tb=True, name="mm_xq_da")
    g["w_xq"] = _mm(sv["hn2"], dxq, ta=True, name="mm_xq_dw")
    dmemn = _mm(dkv, p["w_xkv"], tb=True, name="mm_xkv_da")
    g["w_xkv"] = _mm(sv["memn"], dkv, ta=True, name="mm_xkv_dw")
    _, g["norm_mem"] = _rmsnorm_bwd(mem, p["norm_mem"], dmemn, name="norm_mem_bwd")
    dh1, g["norm_xattn_pre"] = _rmsnorm_bwd(sv["h1"], p["norm_xattn_pre"], dhn2, res=dh2, name="norm_xattn_pre_bwd")
    dmix, g["norm_mix_post"] = _rmsnorm_bwd(sv["mix"], p["norm_mix_post"], dh1, name="norm_mix_post_bwd")
    dcat = _mm(dmix, p["w_out"], tb=True, name="mm_out_da")
    g["w_out"] = _mm(sv["cat"], dmix, ta=True, name="mm_out_dw")
    proj = sv["proj"]
    daq, dak, dav = _attn_bwd(proj, tabs, dcat, name="attn_bwd")
    dcb, dcc, dcx, g["conv_short"] = _sconv_bwd(proj, p["conv_short"], dcat, name="sconv_bwd")
    dqkv, dgbeta, dgate, g["gdn_norm"] = _gdn_core_bwd(sv["qkv"], sv["gbeta"], proj, p["gdn_norm"], sv["states"],
                                                        dcat, name="gdn_core_bwd")
    dgqkv, g["conv_gdn"] = _gdn_pre_bwd(proj, p["conv_gdn"], dqkv, name="gdn_pre_bwd")
    dab, g["gdn_a_log"], g["gdn_dt_bias"] = _gdn_gates_bwd(proj, p["gdn_a_log"], p["gdn_dt_bias"], dgbeta,
                                                          name="gdn_gates_bwd")
    s = proj.shape[0]
    dproj = jnp.concatenate([daq, dak, dav, dcb, dcc, dcx, dgqkv, dgate, dab,
                             jnp.zeros((s, IN_PAD - COL_AB - LANE), F32)], axis=-1)
    dhn1 = _mm(dproj, p["w_in"], tb=True, name="mm_in_da")
    g["w_in"] = _mm(sv["hn1"], dproj, ta=True, name="mm_in_dw")
    dh0, g["norm_mix_pre"] = _rmsnorm_bwd(sv["h0"], p["norm_mix_pre"], dhn1, res=dh1, name="norm_mix_pre_bwd")
    return dh0, g


MATRICES = ("w_in", "w_out", "w_xq", "w_xkv", "w_xo", "w_gate_up", "w_down")
VECTORS = ("norm_mix_pre", "norm_mix_post", "conv_short", "conv_gdn", "gdn_a_log", "gdn_dt_bias", "gdn_norm",
           "norm_mem", "norm_xattn_pre", "norm_xattn_post", "norm_ffn_pre", "norm_ffn_post")


def _w_in_to_padded(w):
    zeros = jnp.zeros(w.shape[:-1] + (IN_PAD - IN_WIDTH,), w.dtype)
    return jnp.concatenate([w[..., :COL_GATE], w[..., COL_GATE + 8:], w[..., COL_GATE:COL_GATE + 8], zeros], axis=-1)


def _w_in_from_padded(g):
    return jnp.concatenate([g[..., :COL_GATE], g[..., COL_AB:COL_AB + 8], g[..., COL_GATE:COL_AB]], axis=-1)


def _layer_params(full, l):
    p = {n: full[n][l] for n in MATRICES}
    for n in VECTORS:
        v = full[n][l]
        if n in ("gdn_a_log", "gdn_dt_bias"):
            p[n] = _pad_lanes(v)
        elif v.ndim == 1:
            p[n] = v.reshape(1, -1)
        else:
            p[n] = v
    return p


def _local_step(x, mem, pos, target, full):
    tabs = _rope_tables(pos, name="rope_tables")
    h = x
    saved, params = [], []
    for l in range(DEPTH):
        p = _layer_params(full, l)
        h, sv = _layer_fwd(h, mem, tabs, p)
        params.append(p)
        saved.append(sv)
    loss_row, dh = _loss_grad(h, target, name="loss_grad")
    grads = [None] * DEPTH
    for l in reversed(range(DEPTH)):
        dh, grads[l] = _layer_bwd(dh, mem, tabs, params[l], saved[l])
    return loss_row, dh, grads


ANY = pl.BlockSpec(memory_space=pl.ANY)
MESH = pl.DeviceIdType.MESH


def _flip(pos, mask):
    return tuple(1 - v if m else v for v, m in zip(pos, mask))


def _exchange(ins, out_shapes, remote, local, *, name):
    n_in = len(ins)
    n_out = len(out_shapes)

    def at(ref, idx):
        return ref.at[idx] if idx else ref

    def body(*refs):
        in_refs = refs[:n_in]
        out_refs = refs[n_in:n_in + n_out]
        send_sems, recv_sems, local_sems = refs[n_in + n_out:]
        me = (lax.axis_index("x"), lax.axis_index("y"), lax.axis_index("c"))
        waits = []
        for k, (ii, src_at, oi, dst_at, mask) in enumerate(remote):
            peer = _flip(me, mask)
            pltpu.make_async_remote_copy(
                src_ref=at(in_refs[ii], src_at(me, peer)), dst_ref=at(out_refs[oi], dst_at(me)),
                send_sem=send_sems.at[k], recv_sem=recv_sems.at[k], device_id=peer, device_id_type=MESH).start()
            waits.append(pltpu.make_async_remote_copy(
                src_ref=at(in_refs[ii], src_at(peer, me)), dst_ref=at(out_refs[oi], dst_at(peer)),
                send_sem=send_sems.at[k], recv_sem=recv_sems.at[k], device_id=peer, device_id_type=MESH))
        own = []
        for k, (ii, src_at, oi, dst_at) in enumerate(local):
            cp = pltpu.make_async_copy(at(in_refs[ii], src_at(me)), at(out_refs[oi], dst_at(me)), local_sems.at[k])
            cp.start()
            own.append(cp)
        for w in waits:
            w.wait_send()
            w.wait_recv()
        for cp in own:
            cp.wait()

    return pl.pallas_call(
        body, in_specs=[ANY] * n_in, out_specs=[ANY] * n_out, out_shape=list(out_shapes),
        scratch_shapes=[pltpu.SemaphoreType.DMA((len(remote),)), pltpu.SemaphoreType.DMA((len(remote),)),
                        pltpu.SemaphoreType.DMA((max(len(local), 1),))],
        name=name)(*ins)


def _chip(pos):
    return 2 * pos[0] + pos[1]


XY_MASKS = ((1, 0, 0), (0, 1, 0), (1, 1, 0))
SIBLING = (0, 0, 1)
ALL_MASKS = tuple((a, b, c) for a in (0, 1) for b in (0, 1) for c in (0, 1))[1:]


def _gather_xy(arrs, *, name):
    outs = [SDS((4,) + a.shape, a.dtype) for a in arrs]
    whole = lambda *_: ()
    slot = lambda pos: (_chip(pos),)
    remote = [(i, whole, i, slot, m) for i in range(len(arrs)) for m in XY_MASKS]
    local = [(i, whole, i, slot) for i in range(len(arrs))]
    return _exchange(arrs, outs, remote, local, name=name)


def _gather_all(arr, *, name):
    slot = lambda pos: (4 * pos[0] + 2 * pos[1] + pos[2],)
    whole = lambda *_: ()
    remote = [(0, whole, 0, slot, m) for m in ALL_MASKS]
    return _exchange([arr], [SDS((8,) + arr.shape, arr.dtype)], remote, [(0, whole, 0, slot)], name=name)[0]


def _swap_halves(arrs, *, name):
    outs = [SDS(a.shape[:2] + (a.shape[2] // 2, a.shape[3]), a.dtype) for a in arrs]

    def src(i):
        half = arrs[i].shape[2] // 2
        return lambda sender, receiver: (slice(None), slice(None), pl.ds(receiver[2] * half, half))

    whole = lambda *_: ()
    remote = [(i, src(i), i, whole, SIBLING) for i in range(len(arrs))]
    return _exchange(arrs, outs, remote, [], name=name)


def _scatter_xy(arrs, *, name):
    outs = [SDS(a.shape, a.dtype) for a in arrs]
    src = lambda sender, receiver: (_chip(receiver),)
    dst = lambda sender: (_chip(sender),)
    remote = [(i, src, i, dst, m) for i in range(len(arrs)) for m in XY_MASKS]
    local = [(i, lambda me: (_chip(me),), i, lambda me: (_chip(me),)) for i in range(len(arrs))]
    return _exchange(arrs, outs, remote, local, name=name)


def _join_halves(arrs, *, name):
    outs = [SDS((a.shape[0], 2) + a.shape[1:], a.dtype) for a in arrs]
    whole = lambda *_: ()
    dst = lambda sender: (slice(None), sender[2])
    remote = [(i, whole, i, dst, SIBLING) for i in range(len(arrs))]
    local = [(i, whole, i, dst) for i in range(len(arrs))]
    return _exchange(arrs, outs, remote, local, name=name)


def _add_half(g, other, core, *, name):
    n4, nl, r, c = g.shape
    half = r // 2
    g3 = g.reshape(n4 * nl, 2, half, c)
    o3 = other.reshape(n4 * nl, half, c)
    tr = _rows_tile(half) if half > 512 else half

    def body(core_ref, g_ref, o_ref, out_ref):
        out_ref[...] = (g_ref[...] + o_ref[...]).astype(BF16)

    return pl.pallas_call(
        body,
        grid_spec=pltpu.PrefetchScalarGridSpec(
            num_scalar_prefetch=1, grid=(n4 * nl, half // tr),
            in_specs=[BS((None, None, tr, c), lambda i, j, core_ref: (i, core_ref[0], j, 0)),
                      BS((None, tr, c), lambda i, j, core_ref: (i, j, 0))],
            out_specs=BS((None, tr, c), lambda i, j, core_ref: (i, j, 0))),
        out_shape=SDS((n4 * nl, half, c), BF16), compiler_params=_cp(), name=name)(core, g3, o3).reshape(n4, nl, half, c)


def _sum_chips(parts, *, name):
    n4, nl, h, c = parts.shape
    tr = _rows_tile(h) if h > 512 else h

    def body(p_ref, out_ref):
        acc = p_ref[0].astype(F32) + p_ref[1].astype(F32)
        out_ref[...] = (acc + p_ref[2].astype(F32)) + p_ref[3].astype(F32)

    return pl.pallas_call(
        body, grid=(nl, h // tr), in_specs=[BS((n4, None, tr, c), lambda i, j: (0, i, j, 0))],
        out_specs=BS((None, tr, c), lambda i, j: (i, j, 0)), out_shape=SDS((nl, h, c), F32),
        compiler_params=_cp(), name=name)(parts)


def _sum_devices(parts, *, name):
    n, r, c = parts.shape

    def body(p_ref, out_ref):
        acc = p_ref[0]
        for d in range(1, n):
            acc = acc + p_ref[d]
        out_ref[...] = acc

    return pl.pallas_call(
        body, grid=(1,), in_specs=[BS((n, r, c), lambda i: (0, 0, 0))], out_specs=BS((r, c), lambda i: (0, 0)),
        out_shape=SDS((r, c), F32), compiler_params=_cp(), name=name)(parts)


WEIGHTS = ("norm_mix_pre", "norm_mix_post", "w_in", "conv_short", "conv_gdn", "gdn_a_log", "gdn_dt_bias",
           "gdn_norm", "w_out", "norm_mem", "norm_xattn_pre", "norm_xattn_post", "w_xq", "w_xkv", "w_xo",
           "norm_ffn_pre", "norm_ffn_post", "w_gate_up", "w_down")
COL_SHARDED = ("w_in", "w_xkv", "w_gate_up", "conv_short", "conv_gdn")
ROW_SHARDED = ("w_out", "w_xq", "w_xo", "w_down")
SMALL_SHARDED = ("conv_short", "conv_gdn")
SMALL_ROW_PAD = 8


def _from_shards(n, g):
    if n in COL_SHARDED:
        t = jnp.moveaxis(g, 0, 2)
        return t.reshape(t.shape[0], t.shape[1], -1)
    t = jnp.moveaxis(g, 0, 1)
    return t.reshape(t.shape[0], -1, t.shape[-1])


def _to_shards(n, full):
    nl, r, c = full.shape
    if n in COL_SHARDED:
        return jnp.moveaxis(full.reshape(nl, r, 4, c // 4), 2, 0)
    return jnp.moveaxis(full.reshape(nl, 4, r // 4, c), 1, 0)


def _pack_small(grads):
    rows = []
    for g in grads:
        for n in WEIGHTS:
            if n not in MATRICES:
                rows.append(g[n].reshape(-1, LANE))
    n_rows = sum(r.shape[0] for r in rows)
    pad = -n_rows % SMALL_ROW_PAD
    if pad:
        rows.append(jnp.zeros((pad, LANE), F32))
    return jnp.concatenate(rows, axis=0)


def _unpack_small(packed, like):
    out, at = [], 0
    for _ in range(DEPTH):
        g = {}
        for n in WEIGHTS:
            if n not in MATRICES:
                shape = like[n].shape
                k = math.prod(shape) // LANE
                g[n] = packed[at:at + k].reshape(shape)
                at += k
        out.append(g)
    return out


def kernel(x, mem, positions, norm_mix_pre, norm_mix_post, w_in, conv_short, conv_gdn, gdn_a_log, gdn_dt_bias, gdn_norm, w_out, norm_mem, norm_xattn_pre, norm_xattn_post, w_xq, w_xkv, w_xo, norm_ffn_pre, norm_ffn_post, w_gate_up, w_down, loss_target, m_norm_mix_pre, m_norm_mix_post, m_w_in, m_conv_short, m_conv_gdn, m_gdn_a_log, m_gdn_dt_bias, m_gdn_norm, m_w_out, m_norm_mem, m_norm_xattn_pre, m_norm_xattn_post, m_w_xq, m_w_xkv, m_w_xo, m_norm_ffn_pre, m_norm_ffn_post, m_w_gate_up, m_w_down, v_norm_mix_pre, v_norm_mix_post, v_w_in, v_conv_short, v_conv_gdn, v_gdn_a_log, v_gdn_dt_bias, v_gdn_norm, v_w_out, v_norm_mem, v_norm_xattn_pre, v_norm_xattn_post, v_w_xq, v_w_xkv, v_w_xo, v_norm_ffn_pre, v_norm_ffn_post, v_w_gate_up, v_w_down):
    args = dict(locals())
    w = {n: args[n] for n in WEIGHTS}
    m = {n: args["m_" + n] for n in WEIGHTS}
    v = {n: args["v_" + n] for n in WEIGHTS}
    seq = x.shape[1]
    chip = 2 * lax.axis_index("x") + lax.axis_index("y")
    core = lax.axis_index("c").astype(jnp.int32).reshape(1)

    sharded = list(MATRICES) + list(SMALL_SHARDED)
    blocks = _gather_xy([w[n].astype(BF16) if n in MATRICES else w[n] for n in sharded], name="gather_weights")
    full = {n: _from_shards(n, b) for n, b in zip(sharded, blocks)}
    full["w_in"] = _w_in_to_padded(full["w_in"])
    for n in WEIGHTS:
        if n not in full:
            full[n] = w[n]

    loss_row, dx, grads = _local_step(x[0], mem[0], positions.reshape(seq, 1), loss_target[0], full)

    mine = []
    for n in MATRICES:
        stacked = jnp.stack([g[n] for g in grads])
        if n == "w_in":
            stacked = _w_in_from_padded(stacked)
        mine.append(_to_shards(n, stacked))
    theirs = _swap_halves(mine, name="grads_swap_halves")
    pair = [_add_half(a, b, core, name="grads_pair_sum") for a, b in zip(mine, theirs)]
    parts = _scatter_xy(pair, name="grads_scatter")
    reduced = [_sum_chips(p, name="grads_chip_sum") for p in parts]
    joined = _join_halves(reduced, name="grads_join_halves")
    grad = {n: j.reshape(w[n].shape) for n, j in zip(MATRICES, joined)}

    packed = _pack_small(grads)
    total = _sum_devices(_gather_all(packed, name="small_grads_gather"), name="small_grads_sum")
    small = _unpack_small(total, grads[0])
    for n in WEIGHTS:
        if n in MATRICES:
            continue
        g = jnp.stack([s[n] for s in small])
        if n in ("gdn_a_log", "gdn_dt_bias"):
            g = g[:, 0, :GDN_HEADS]
        elif n in SMALL_SHARDED:
            width = w[n].shape[-1]
            g = lax.dynamic_slice_in_dim(g, chip * width, width, axis=2)
        grad[n] = g.reshape(w[n].shape)

    delta, new_m, new_v = {}, {}, {}
    for n in WEIGHTS:
        shape = w[n].shape
        two_d = (-1, shape[-1])
        d, nm, nv = _adamw(w[n].reshape(two_d), grad[n].reshape(two_d), m[n].reshape(two_d), v[n].reshape(two_d),
                           name="adamw_" + n)
        delta[n], new_m[n], new_v[n] = d.reshape(shape), nm.reshape(shape), nv.reshape(shape)

    loss = lax.psum(loss_row[0, 0], ("x", "y", "c"))
    return (loss, dx.reshape(x.shape), *[grad[n] for n in WEIGHTS], *[delta[n] for n in WEIGHTS],
            *[new_m[n] for n in WEIGHTS], *[new_v[n] for n in WEIGHTS])
```

```python
import functools
import math

import jax
import jax.numpy as jnp
from jax import lax
from jax.experimental import pallas as pl
from jax.experimental.pallas import tpu as pltpu

F32 = jnp.float32
BF16 = jnp.bfloat16
BS = pl.BlockSpec
SDS = jax.ShapeDtypeStruct
PRECISE = lax.Precision.HIGH

D_MODEL = 1024
DEPTH = 4
EPS = 1e-6
ATTN_HEADS = 4
ATTN_HEAD_DIM = 64
ATTN_WIDTH = 256
ROPE_THETA = 500000.0
ROPE_DIM = 16
CONV_WIDTH = 256
CONV_K = 3
GDN_HEADS = 4
GDN_HEAD_DIM = 128
GDN_WIDTH = 512
GDN_CONV_K = 4
GDN_CHUNK = 64
IN_WIDTH = 3592
XATTN_HEADS = 4
XATTN_HEAD_DIM = 256
FFN_HIDDEN = 2816
ADAM_LR = 0.001
ADAM_B1 = 0.9
ADAM_B2 = 0.999
ADAM_EPS = 1e-08
ADAM_WD = 0.01
ADAM_STEP = 10

IN_PAD = 3840
COL_GDN = 1536
COL_GATE = 3072
COL_AB = 3584

VMEM_LIMIT_V7X = 56 * 1024 * 1024
LANE = 128


def _cp(**kw):
    return pltpu.CompilerParams(vmem_limit_bytes=VMEM_LIMIT_V7X, **kw)


def _tile(n, cap):
    if n <= cap:
        return n
    best = None
    for t in range(LANE, cap + 1, LANE):
        if n % t == 0:
            best = t
    assert best is not None, (n, cap)
    return best


def _dot(a, b, ca, cb, precise=False):
    dims = (((ca,), (cb,)), ((), ()))
    if precise:
        return lax.dot_general(a.astype(F32), b.astype(F32), dims, precision=PRECISE,
                               preferred_element_type=F32)
    return lax.dot_general(a.astype(BF16), b.astype(BF16), dims, preferred_element_type=F32)


def _sigmoid(x):
    return 1.0 / (1.0 + jnp.exp(-x))


def _mm(a, b, *, ta=False, tb=False, out_dtype=F32, name):
    m, k = (a.shape[1], a.shape[0]) if ta else a.shape
    n = b.shape[0] if tb else b.shape[1]
    assert (b.shape[1] if tb else b.shape[0]) == k
    tm = _tile(m, 512)
    tn = _tile(n, max(LANE, min(1024, (4 * 1024 * 1024) // (k * b.dtype.itemsize) // LANE * LANE)))
    ca, cb = (0 if ta else 1), (1 if tb else 0)

    def body(a_ref, b_ref, o_ref):
        o_ref[...] = _dot(a_ref[...], b_ref[...], ca, cb).astype(out_dtype)

    a_spec = BS((k, tm), lambda i, j: (0, i)) if ta else BS((tm, k), lambda i, j: (i, 0))
    b_spec = BS((tn, k), lambda i, j: (j, 0)) if tb else BS((k, tn), lambda i, j: (0, j))
    return pl.pallas_call(
        body, grid=(m // tm, n // tn), in_specs=[a_spec, b_spec],
        out_specs=BS((tm, tn), lambda i, j: (i, j)), out_shape=SDS((m, n), out_dtype),
        compiler_params=_cp(), name=name)(a, b)


def _rmsnorm(x, w, *, name):
    r, d = x.shape
    tr = _tile(r, 512)

    def body(x_ref, w_ref, o_ref):
        xv = x_ref[...]
        rs = lax.rsqrt(jnp.mean(xv * xv, axis=-1, keepdims=True) + EPS)
        o_ref[...] = (xv * rs * w_ref[...]).astype(BF16)

    return pl.pallas_call(
        body, grid=(r // tr,), in_specs=[BS((tr, d), lambda i: (i, 0)), BS((1, d), lambda i: (0, 0))],
        out_specs=BS((tr, d), lambda i: (i, 0)), out_shape=SDS((r, d), BF16),
        compiler_params=_cp(), name=name)(x, w)


def _resnorm(h, m, w, *, name):
    r, d = h.shape
    tr = _tile(r, 512)

    def body(h_ref, m_ref, w_ref, o_ref):
        mv = m_ref[...]
        rs = lax.rsqrt(jnp.mean(mv * mv, axis=-1, keepdims=True) + EPS)
        o_ref[...] = h_ref[...] + mv * rs * w_ref[...]

    row = BS((tr, d), lambda i: (i, 0))
    return pl.pallas_call(
        body, grid=(r // tr,), in_specs=[row, row, BS((1, d), lambda i: (0, 0))],
        out_specs=row, out_shape=SDS((r, d), F32), compiler_params=_cp(), name=name)(h, m, w)


def _rmsnorm_bwd(x, w, dy, res=None, *, name):
    r, d = x.shape
    tr = _tile(r, 512)
    has_res = res is not None

    def body(*refs):
        if has_res:
            x_ref, w_ref, dy_ref, res_ref, dx_ref, dw_ref = refs
        else:
            x_ref, w_ref, dy_ref, dx_ref, dw_ref = refs
        xv = x_ref[...]
        dyv = dy_ref[...].astype(F32)
        rs = lax.rsqrt(jnp.mean(xv * xv, axis=-1, keepdims=True) + EPS)
        nv = xv * rs
        dyw = dyv * w_ref[...]
        dx = rs * (dyw - nv * jnp.mean(dyw * nv, axis=-1, keepdims=True))
        if has_res:
            dx = dx + res_ref[...]
        dx_ref[...] = dx

        @pl.when(pl.program_id(0) == 0)
        def _():
            dw_ref[...] = jnp.zeros_like(dw_ref)

        dw_ref[...] += jnp.sum(dyv * nv, axis=0, keepdims=True)

    row = BS((tr, d), lambda i: (i, 0))
    vec = BS((1, d), lambda i: (0, 0))
    ins = [x, w, dy] + ([res] if has_res else [])
    return pl.pallas_call(
        body, grid=(r // tr,), in_specs=[row, vec, row] + ([row] if has_res else []),
        out_specs=[row, vec], out_shape=[SDS((r, d), F32), SDS((1, d), F32)],
        compiler_params=_cp(), name=name)(*ins)


def _swiglu(gu, *, name):
    r, h2 = gu.shape
    hid = h2 // 2
    tr, tc = _tile(r, 512), _tile(hid, 1408)
    nb = hid // tc

    def body(g_ref, u_ref, o_ref):
        g = g_ref[...]
        o_ref[...] = (g * _sigmoid(g) * u_ref[...]).astype(BF16)

    return pl.pallas_call(
        body, grid=(r // tr, nb),
        in_specs=[BS((tr, tc), lambda i, j: (i, j)), BS((tr, tc), lambda i, j: (i, j + nb))],
        out_specs=BS((tr, tc), lambda i, j: (i, j)), out_shape=SDS((r, hid), BF16),
        compiler_params=_cp(), name=name)(gu, gu)


def _swiglu_bwd(gu, dact, *, name):
    r, h2 = gu.shape
    hid = h2 // 2
    tr, tc = _tile(r, 512), _tile(hid, 1408)
    nb = hid // tc

    def body(g_ref, u_ref, d_ref, o_ref):
        g = g_ref[...]
        da = d_ref[...]
        sg = _sigmoid(g)
        dgate = da * u_ref[...] * sg * (1.0 + g * (1.0 - sg))
        dup = da * g * sg
        o_ref[...] = jnp.where(pl.program_id(1) < nb, dgate, dup).astype(BF16)

    return pl.pallas_call(
        body, grid=(r // tr, 2 * nb),
        in_specs=[BS((tr, tc), lambda i, j: (i, j % nb)), BS((tr, tc), lambda i, j: (i, j % nb + nb)),
                  BS((tr, tc), lambda i, j: (i, j % nb))],
        out_specs=BS((tr, tc), lambda i, j: (i, j)), out_shape=SDS((r, h2), BF16),
        compiler_params=_cp(), name=name)(gu, gu, dact)


def _loss_grad(h, target, *, name):
    r, d = h.shape
    tr = _tile(r, 512)

    def body(h_ref, t_ref, l_ref, g_ref):
        e = h_ref[...] - t_ref[...]
        g_ref[...] = e * (1.0 / d)

        @pl.when(pl.program_id(0) == 0)
        def _():
            l_ref[...] = jnp.zeros_like(l_ref)

        l_ref[...] += jnp.full((1, LANE), 0.5 / d, F32) * jnp.sum(e * e)

    row = BS((tr, d), lambda i: (i, 0))
    return pl.pallas_call(
        body, grid=(r // tr,), in_specs=[row, row],
        out_specs=[BS((1, LANE), lambda i: (0, 0)), row],
        out_shape=[SDS((1, LANE), F32), SDS((r, d), F32)], compiler_params=_cp(), name=name)(h, target)


def _adamw(w, g, m, v, *, name):
    r, c = w.shape
    tr = r if r <= 512 else _rows_tile(r)
    bc1 = 1.0 - ADAM_B1 ** ADAM_STEP
    bc2 = 1.0 - ADAM_B2 ** ADAM_STEP

    def body(w_ref, g_ref, m_ref, v_ref, d_ref, nm_ref, nv_ref):
        gv = g_ref[...]
        nm = ADAM_B1 * m_ref[...] + (1.0 - ADAM_B1) * gv
        nv = ADAM_B2 * v_ref[...] + (1.0 - ADAM_B2) * (gv * gv)
        d_ref[...] = -ADAM_LR * ((nm / bc1) / (jnp.sqrt(nv / bc2) + ADAM_EPS) + ADAM_WD * w_ref[...])
        nm_ref[...] = nm
        nv_ref[...] = nv

    blk = BS((tr, c), lambda i: (i, 0))
    return pl.pallas_call(
        body, grid=(r // tr,), in_specs=[blk] * 4, out_specs=[blk] * 3,
        out_shape=[SDS((r, c), F32)] * 3, compiler_params=_cp(), name=name)(w, g, m, v)


def _rows_tile(r):
    for t in (512, 256, 128, 64, 32, 16, 8):
        if r % t == 0:
            return t
    return r


def _rope_tables(pos, *, name):
    s = pos.shape[0]
    half = ROPE_DIM // 2

    def body(p_ref, c_ref, a_ref, b_ref):
        lane = lax.broadcasted_iota(jnp.int32, (s, ATTN_WIDTH), 1) & (ATTN_HEAD_DIM - 1)
        fi = (lane & (half - 1)).astype(F32)
        inv_freq = jnp.exp(fi * (-2.0 * math.log(ROPE_THETA) / ROPE_DIM))
        ang = p_ref[...].astype(F32) * inv_freq
        cs, sn = jnp.cos(ang), jnp.sin(ang)
        c_ref[...] = jnp.where(lane < ROPE_DIM, cs, 1.0)
        a_ref[...] = jnp.where(lane < half, -sn, 0.0)
        b_ref[...] = jnp.where((lane >= half) & (lane < ROPE_DIM), sn, 0.0)

    full = BS((s, ATTN_WIDTH), lambda i: (0, 0))
    return pl.pallas_call(
        body, grid=(1,), in_specs=[BS((s, 1), lambda i: (0, 0))], out_specs=[full] * 3,
        out_shape=[SDS((s, ATTN_WIDTH), F32)] * 3, compiler_params=_cp(), name=name)(pos)


def _rot(x, c, a, b):
    w = x.shape[1]
    return x * c + pltpu.roll(x, w - ROPE_DIM // 2, 1) * a + pltpu.roll(x, ROPE_DIM // 2, 1) * b


def _rot_t(dy, c, a, b):
    w = dy.shape[1]
    return dy * c + pltpu.roll(dy * a, ROPE_DIM // 2, 1) + pltpu.roll(dy * b, w - ROPE_DIM // 2, 1)


def _attn_count(q0, tq, s):
    dist = (lax.broadcasted_iota(jnp.int32, (tq, s), 0) + q0) - lax.broadcasted_iota(jnp.int32, (tq, s), 1)
    cnt = ((dist <= 128).astype(F32) + (((dist & 3) == 0) & (dist <= 512)).astype(F32)
           + ((dist & 15) == 0).astype(F32))
    return jnp.where(dist >= 0, cnt, 0.0)


def _attn_probs(qm, k, cnt):
    sc = _dot(qm, k, 1, 1)
    mx = jnp.max(jnp.where(cnt > 0.0, sc, -1e30), axis=-1, keepdims=True)
    p = cnt * jnp.exp(jnp.minimum(sc - mx, 0.0))
    return p / jnp.sum(p, axis=-1, keepdims=True)


ATTN_TQ = 256


def _attn_specs(s, tq):
    def qblk(col):
        return BS((tq, ATTN_WIDTH), lambda i: (i, col))

    def full(col):
        return BS((s, ATTN_WIDTH), lambda i: (0, col))

    return qblk, full


def _attn_fwd(proj, tabs, *, name):
    s = proj.shape[0]
    tq = ATTN_TQ
    qblk, full = _attn_specs(s, tq)
    scale = ATTN_HEAD_DIM ** -0.5

    def body(q_ref, k_ref, v_ref, cq, aq, bq, ck, ak, bk, o_ref):
        q = _rot(q_ref[...], cq[...], aq[...], bq[...]) * scale
        head = lax.broadcasted_iota(jnp.int32, (1, ATTN_WIDTH), 1) >> 6

        def block(i):
            n = (i + 1) * tq
            k = _rot(k_ref[:n, :], ck[:n, :], ak[:n, :], bk[:n, :]).astype(BF16)
            v = v_ref[:n, :].astype(BF16)
            cnt = _attn_count(i * tq, tq, n)
            acc = jnp.zeros((tq, ATTN_WIDTH), F32)
            for h in range(ATTN_HEADS):
                hm = (head == h).astype(F32)
                p = _attn_probs(q * hm, k, cnt)
                acc = acc + _dot(p, v, 1, 0) * hm
            o_ref[...] = acc.astype(BF16)

        for i in range(s // tq):
            pl.when(pl.program_id(0) == i)(functools.partial(block, i))

    return pl.pallas_call(
        body, grid=(s // tq,),
        in_specs=[qblk(0), full(1), full(2), qblk(0), qblk(0), qblk(0), full(0), full(0), full(0)],
        out_specs=BS((tq, ATTN_WIDTH), lambda i: (i, 0)), out_shape=SDS((s, ATTN_WIDTH), BF16),
        compiler_params=_cp(), name=name)(proj, proj, proj, *tabs, *tabs)


def _attn_bwd(proj, tabs, dcat, *, name):
    s = proj.shape[0]
    tq = ATTN_TQ
    nq = s // tq
    qblk, full = _attn_specs(s, tq)
    scale = ATTN_HEAD_DIM ** -0.5

    def body(q_ref, k_ref, v_ref, cq, aq, bq, ck, ak, bk, dy_ref, dq_ref, dk_ref, dv_ref, dk_acc, dv_acc):
        i = pl.program_id(0)

        @pl.when(i == 0)
        def _():
            dk_acc[...] = jnp.zeros_like(dk_acc)
            dv_acc[...] = jnp.zeros_like(dv_acc)

        q = _rot(q_ref[...], cq[...], aq[...], bq[...]) * scale
        dy = dy_ref[...].astype(F32)
        head = lax.broadcasted_iota(jnp.int32, (1, ATTN_WIDTH), 1) >> 6

        def block(j):
            n = (j + 1) * tq
            k = _rot(k_ref[:n, :], ck[:n, :], ak[:n, :], bk[:n, :]).astype(BF16)
            v = v_ref[:n, :].astype(BF16)
            cnt = _attn_count(j * tq, tq, n)
            dq = jnp.zeros((tq, ATTN_WIDTH), F32)
            for h in range(ATTN_HEADS):
                hm = (head == h).astype(F32)
                qm = q * hm
                dym = dy * hm
                p = _attn_probs(qm, k, cnt)
                dp = _dot(dym, v, 1, 1)
                ds = p * (dp - jnp.sum(p * dp, axis=-1, keepdims=True))
                dq = dq + _dot(ds, k, 1, 0) * hm
                dk_acc[:n, :] += _dot(ds, qm, 0, 0)
                dv_acc[:n, :] += _dot(p, dym, 0, 0)
            dq_ref[...] = _rot_t(dq * scale, cq[...], aq[...], bq[...])

        for j in range(nq):
            pl.when(i == j)(functools.partial(block, j))

        @pl.when(i == nq - 1)
        def _():
            dk_ref[...] = _rot_t(dk_acc[...], ck[...], ak[...], bk[...])
            dv_ref[...] = dv_acc[...]

    whole = BS((s, ATTN_WIDTH), lambda i: (0, 0))
    dq, dk, dv = pl.pallas_call(
        body, grid=(nq,),
        in_specs=[qblk(0), full(1), full(2), qblk(0), qblk(0), qblk(0), full(0), full(0), full(0), qblk(0)],
        out_specs=[BS((tq, ATTN_WIDTH), lambda i: (i, 0)), whole, whole],
        out_shape=[SDS((s, ATTN_WIDTH), F32)] * 3,
        scratch_shapes=[pltpu.VMEM((s, ATTN_WIDTH), F32), pltpu.VMEM((s, ATTN_WIDTH), F32)],
        compiler_params=_cp(), name=name)(proj, proj, proj, *tabs, *tabs, dcat)
    return dq, dk, dv


def _shift_down(x, n):
    if n == 0:
        return x
    rows = lax.broadcasted_iota(jnp.int32, x.shape, 0)
    return jnp.where(rows >= n, pltpu.roll(x, n, 0), 0.0)


def _shift_up(x, n):
    if n == 0:
        return x
    t = x.shape[0]
    rows = lax.broadcasted_iota(jnp.int32, x.shape, 0)
    return jnp.where(rows < t - n, pltpu.roll(x, t - n, 0), 0.0)


def _conv_fwd(z, w, kk):
    y = z * w[kk - 1:kk, :]
    for j in range(kk - 1):
        y = y + _shift_down(z, kk - 1 - j) * w[j:j + 1, :]
    return y


def _conv_bwd(z, dy, w, kk):
    dz = dy * w[kk - 1:kk, :]
    dws = []
    for j in range(kk - 1):
        dz = dz + _shift_up(dy, kk - 1 - j) * w[j:j + 1, :]
        dws.append(jnp.sum(dy * _shift_down(z, kk - 1 - j), axis=0, keepdims=True))
    dws.append(jnp.sum(dy * z, axis=0, keepdims=True))
    return dz, jnp.concatenate(dws, axis=0)


def _sconv_fwd(proj, w, *, name):
    s = proj.shape[0]

    def body(b_ref, c_ref, x_ref, w_ref, o_ref):
        y = _conv_fwd(c_ref[...] * x_ref[...], w_ref[...], CONV_K)
        o_ref[...] = (b_ref[...] * y).astype(BF16)

    def col(j):
        return BS((s, LANE), lambda i: (0, j + i))

    return pl.pallas_call(
        body, grid=(CONV_WIDTH // LANE,), in_specs=[col(6), col(8), col(10), BS((CONV_K, LANE), lambda i: (0, i))],
        out_specs=BS((s, LANE), lambda i: (0, i)), out_shape=SDS((s, CONV_WIDTH), BF16),
        compiler_params=_cp(), name=name)(proj, proj, proj, w)


def _sconv_bwd(proj, w, dcat, *, name):
    s = proj.shape[0]

    def body(b_ref, c_ref, x_ref, w_ref, dy_ref, db_ref, dc_ref, dx_ref, dw_ref):
        cv, xv, wv = c_ref[...], x_ref[...], w_ref[...]
        dy = dy_ref[...].astype(F32)
        z = cv * xv
        db_ref[...] = dy * _conv_fwd(z, wv, CONV_K)
        dz, dw = _conv_bwd(z, dy * b_ref[...], wv, CONV_K)
        dc_ref[...] = dz * xv
        dx_ref[...] = dz * cv
        dw_ref[...] = dw

    def col(j):
        return BS((s, LANE), lambda i: (0, j + i))

    out = BS((s, LANE), lambda i: (0, i))
    wspec = BS((CONV_K, LANE), lambda i: (0, i))
    return pl.pallas_call(
        body, grid=(CONV_WIDTH // LANE,), in_specs=[col(6), col(8), col(10), wspec, col(2)],
        out_specs=[out, out, out, wspec],
        out_shape=[SDS((s, CONV_WIDTH), F32)] * 3 + [SDS((CONV_K, CONV_WIDTH), F32)],
        compiler_params=_cp(), name=name)(proj, proj, proj, w, dcat)


def _l2n(y, scale):
    r = lax.rsqrt(jnp.sum(y * y, axis=-1, keepdims=True) + EPS)
    return y * r * scale, r


def _gdn_pre_fwd(proj, w, *, name):
    s = proj.shape[0]
    nh = GDN_HEADS

    def body(x_ref, w_ref, o_ref):
        j = pl.program_id(0)
        c = _conv_fwd(x_ref[...], w_ref[...], GDN_CONV_K)
        y = c * _sigmoid(c)
        scale = jnp.where(j < nh, GDN_HEAD_DIM ** -0.5, 1.0)
        n, _ = _l2n(y, scale)
        o_ref[...] = jnp.where(j < 2 * nh, n, y)

    return pl.pallas_call(
        body, grid=(3 * nh,),
        in_specs=[BS((s, LANE), lambda j: (0, COL_GDN // LANE + j)), BS((GDN_CONV_K, LANE), lambda j: (0, j))],
        out_specs=BS((s, LANE), lambda j: (0, j)), out_shape=SDS((s, 3 * GDN_WIDTH), F32),
        compiler_params=_cp(), name=name)(proj, w)


def _gdn_pre_bwd(proj, w, dqkv, *, name):
    s = proj.shape[0]
    nh = GDN_HEADS

    def body(x_ref, w_ref, d_ref, dx_ref, dw_ref):
        j = pl.program_id(0)
        xv, wv, dn = x_ref[...], w_ref[...], d_ref[...]
        c = _conv_fwd(xv, wv, GDN_CONV_K)
        sg = _sigmoid(c)
        y = c * sg
        scale = jnp.where(j < nh, GDN_HEAD_DIM ** -0.5, 1.0)
        n, r = _l2n(y, 1.0)
        dns = dn * scale
        dy_norm = r * (dns - n * jnp.sum(dns * n, axis=-1, keepdims=True))
        dy = jnp.where(j < 2 * nh, dy_norm, dn)
        dc = dy * sg * (1.0 + c * (1.0 - sg))
        dx, dw = _conv_bwd(xv, dc, wv, GDN_CONV_K)
        dx_ref[...] = dx
        dw_ref[...] = dw

    wspec = BS((GDN_CONV_K, LANE), lambda j: (0, j))
    blk = BS((s, LANE), lambda j: (0, j))
    return pl.pallas_call(
        body, grid=(3 * nh,),
        in_specs=[BS((s, LANE), lambda j: (0, COL_GDN // LANE + j)), wspec, blk],
        out_specs=[blk, wspec], out_shape=[SDS((s, 3 * GDN_WIDTH), F32), SDS((GDN_CONV_K, 3 * GDN_WIDTH), F32)],
        compiler_params=_cp(), name=name)(proj, w, dqkv)


def _softplus(x):
    return jnp.maximum(x, 0.0) + jnp.log(1.0 + jnp.exp(-jnp.abs(x)))


def _gdn_gates_fwd(proj, a_log, dt_bias, *, name):
    s = proj.shape[0]

    def body(x_ref, al_ref, dt_ref, o_ref):
        xv = x_ref[...]
        lane = lax.broadcasted_iota(jnp.int32, xv.shape, 1)
        g = -jnp.exp(al_ref[...]) * _softplus(xv + dt_ref[...])
        o_ref[...] = jnp.where(lane < GDN_HEADS, g, jnp.where(lane < 2 * GDN_HEADS, _sigmoid(xv), 0.0))

    vec = BS((1, LANE), lambda i: (0, 0))
    return pl.pallas_call(
        body, grid=(1,), in_specs=[BS((s, LANE), lambda i: (0, COL_AB // LANE)), vec, vec],
        out_specs=BS((s, LANE), lambda i: (0, 0)), out_shape=SDS((s, LANE), F32),
        compiler_params=_cp(), name=name)(proj, a_log, dt_bias)


def _gdn_gates_bwd(proj, a_log, dt_bias, dgb, *, name):
    s = proj.shape[0]

    def body(x_ref, al_ref, dt_ref, d_ref, dx_ref, dal_ref, ddt_ref):
        xv, dv = x_ref[...], d_ref[...]
        lane = lax.broadcasted_iota(jnp.int32, xv.shape, 1)
        is_g = lane < GDN_HEADS
        ea = -jnp.exp(al_ref[...])
        z = xv + dt_ref[...]
        da = jnp.where(is_g, dv * ea * _sigmoid(z), 0.0)
        beta = _sigmoid(xv)
        dx_ref[...] = jnp.where(is_g, da, jnp.where(lane < 2 * GDN_HEADS, dv * beta * (1.0 - beta), 0.0))
        dal_ref[...] = jnp.sum(jnp.where(is_g, dv * ea * _softplus(z), 0.0), axis=0, keepdims=True)
        ddt_ref[...] = jnp.sum(da, axis=0, keepdims=True)

    vec = BS((1, LANE), lambda i: (0, 0))
    blk = BS((s, LANE), lambda i: (0, 0))
    return pl.pallas_call(
        body, grid=(1,), in_specs=[BS((s, LANE), lambda i: (0, COL_AB // LANE)), vec, vec, blk],
        out_specs=[blk, vec, vec], out_shape=[SDS((s, LANE), F32), SDS((1, LANE), F32), SDS((1, LANE), F32)],
        compiler_params=_cp(), name=name)(proj, a_log, dt_bias, dgb)


def _col_to_row(col, eye):
    return jnp.sum(jnp.where(eye, col, 0.0), axis=0, keepdims=True)


def _row_to_col(row, eye):
    return jnp.sum(jnp.where(eye, row, 0.0), axis=1, keepdims=True)


GDN_GROUP = 4
TRI_BLOCK_SHIFT = 4


def _tri_inv(a, eye_f, row, col):
    on_diag = (row >> TRI_BLOCK_SHIFT) == (col >> TRI_BLOCK_SHIFT)
    d = jnp.where(on_diag, a, 0.0)
    x = eye_f - d
    p = d
    for _ in range(3):
        p = _dot(p, p, 1, 0, precise=True)
        x = x + _dot(x, p, 1, 0, precise=True)
    m = _dot(x, a - d, 1, 0, precise=True)
    y = eye_f - m
    y = y + _dot(y, _dot(m, m, 1, 0, precise=True), 1, 0, precise=True)
    return _dot(y, x, 1, 0, precise=True)


def _gdn_chunk(q, k, v, g, beta, st):
    c = q.shape[0]
    row = lax.broadcasted_iota(jnp.int32, (c, c), 0)
    col = lax.broadcasted_iota(jnp.int32, (c, c), 1)
    eye = row == col
    low = row >= col
    g_row = _col_to_row(g, eye)
    d_col = jnp.sum(jnp.where(low, g_row, 0.0), axis=1, keepdims=True)
    d_row = jnp.sum(jnp.where(row <= col, g, 0.0), axis=0, keepdims=True)
    rel = jnp.where(low, jnp.exp(jnp.minimum(d_col - d_row, 0.0)), 0.0)
    d_last = d_col[c - 1:c, :]
    e = jnp.exp(d_col)
    f = jnp.exp(d_last - d_col)
    cd = jnp.exp(d_last)
    kb = k * beta
    kk = _dot(kb, k, 1, 1)
    a = jnp.where(row > col, kk * rel, 0.0)
    t = _tri_inv(a, eye.astype(F32), row, col)
    vb = v * beta
    kbe = kb * e
    u = _dot(t, vb, 1, 0)
    w = _dot(t, kbe, 1, 0)
    qk = _dot(q, k, 1, 1)
    attn = qk * rel
    qd = q * e
    kd = k * f
    vn = u - _dot(w, st, 1, 0)
    o = _dot(qd, st, 1, 0) + _dot(attn, vn, 1, 0)
    st_new = cd * st + _dot(kd, vn, 0, 0)
    res = dict(row=row, col=col, eye=eye, low=low, rel=rel, e=e, f=f, cd=cd, kb=kb, kk=kk, t=t, vb=vb,
               kbe=kbe, u=u, w=w, qk=qk, attn=attn, qd=qd, kd=kd, vn=vn)
    return o, st_new, res


def _gdn_chunk_bwd(q, k, v, beta, st, r, do, dst):
    c = q.shape[0]
    row, col, eye, low, rel = r["row"], r["col"], r["eye"], r["low"], r["rel"]
    e, f, cd, kb, t, w, vn, attn = r["e"], r["f"], r["cd"], r["kb"], r["t"], r["w"], r["vn"], r["attn"]
    dvn = _dot(attn, do, 0, 0) + _dot(r["kd"], dst, 1, 0)
    dkd = _dot(vn, dst, 1, 1)
    dcd = jnp.sum(st * dst)
    dst_in = cd * dst + _dot(r["qd"], do, 0, 0) - _dot(w, dvn, 0, 0)
    dqd = _dot(do, st, 1, 1)
    dattn = jnp.where(low, _dot(do, vn, 1, 1), 0.0)
    du = dvn
    dw = -_dot(dvn, st, 1, 1)
    dvb = _dot(t, du, 0, 0)
    dkbe = _dot(t, dw, 0, 0)
    da = jnp.where(row > col, -(_dot(dvb, r["u"], 1, 1) + _dot(dkbe, w, 1, 1)), 0.0)
    dkk = da * rel
    dqk = dattn * rel
    grel = (da * r["kk"] + dattn * r["qk"]) * rel
    dkb = _dot(dkk, k, 1, 0) + dkbe * e
    dk = _dot(dkk, kb, 0, 0) + _dot(dqk, q, 0, 0) + dkd * f + dkb * beta
    dq = _dot(dqk, k, 1, 0) + dqd * e
    dv = dvb * beta
    dbeta = jnp.sum(dkb * k, axis=1, keepdims=True) + jnp.sum(dvb * v, axis=1, keepdims=True)
    de = jnp.sum(dqd * q, axis=1, keepdims=True) + jnp.sum(dkbe * kb, axis=1, keepdims=True)
    dff = jnp.sum(dkd * k, axis=1, keepdims=True) * f
    dd = de * e - dff + jnp.sum(grel, axis=1, keepdims=True) - _row_to_col(jnp.sum(grel, axis=0, keepdims=True), eye)
    rows1 = lax.broadcasted_iota(jnp.int32, (c, 1), 0)
    dd = dd + jnp.where(rows1 == c - 1, jnp.sum(dff) + dcd * cd, 0.0)
    dg = jnp.sum(jnp.where(col >= row, _col_to_row(dd, eye), 0.0), axis=1, keepdims=True)
    return dq, dk, dv, dg, dbeta, dst_in


def _gdn_specs(c):
    def qkv(j):
        return BS((c, GDN_WIDTH), lambda n: (n, j))

    return qkv


def _gdn_core_fwd(qkv, gbeta, proj, norm_w, *, name):
    s = qkv.shape[0]
    c, nh, hd, grp = GDN_CHUNK, GDN_HEADS, GDN_HEAD_DIM, GDN_GROUP
    n_chunks = s // c
    blk = _gdn_specs(grp * c)

    def body(q_ref, k_ref, v_ref, gb_ref, gate_ref, nw_ref, y_ref, st_ref, state):
        @pl.when(pl.program_id(0) == 0)
        def _():
            state[...] = jnp.zeros_like(state)

        nw = nw_ref[...]
        for h in range(nh):
            ls = slice(h * hd, (h + 1) * hd)
            st = state[ls, :]
            for sub in range(grp):
                rs_ = slice(sub * c, (sub + 1) * c)
                st_ref[pl.ds((sub * nh + h) * hd, hd), :] = st
                gb = gb_ref[rs_, :]
                o, st, _ = _gdn_chunk(q_ref[rs_, ls], k_ref[rs_, ls], v_ref[rs_, ls], gb[:, h:h + 1],
                                      gb[:, nh + h:nh + h + 1], st)
                gate = gate_ref[rs_, ls]
                rs = lax.rsqrt(jnp.mean(o * o, axis=-1, keepdims=True) + EPS)
                y_ref[rs_, ls] = (o * rs * nw * (gate * _sigmoid(gate))).astype(BF16)
            state[ls, :] = st

    return pl.pallas_call(
        body, grid=(n_chunks // grp,),
        in_specs=[blk(0), blk(1), blk(2), BS((grp * c, LANE), lambda n: (n, 0)),
                  BS((grp * c, GDN_WIDTH), lambda n: (n, COL_GATE // GDN_WIDTH)), BS((1, hd), lambda n: (0, 0))],
        out_specs=[BS((grp * c, GDN_WIDTH), lambda n: (n, 0)), BS((grp * nh * hd, hd), lambda n: (n, 0))],
        out_shape=[SDS((s, GDN_WIDTH), BF16), SDS((n_chunks * nh * hd, hd), F32)],
        scratch_shapes=[pltpu.VMEM((nh * hd, hd), F32)],
        compiler_params=_cp(), name=name)(qkv, qkv, qkv, gbeta, proj, norm_w)


def _gdn_core_bwd(qkv, gbeta, proj, norm_w, states, dcat, *, name):
    s = qkv.shape[0]
    c, nh, hd, grp = GDN_CHUNK, GDN_HEADS, GDN_HEAD_DIM, GDN_GROUP
    n_chunks = s // c
    last = n_chunks // grp - 1

    def rev(j, w):
        return BS((grp * c, w), lambda n: (last - n, j))

    def body(q_ref, k_ref, v_ref, gb_ref, gate_ref, nw_ref, st_ref, dy_ref,
             dqkv_ref, dgb_ref, dgate_ref, dnw_ref, dstate):
        @pl.when(pl.program_id(0) == 0)
        def _():
            dstate[...] = jnp.zeros_like(dstate)
            dnw_ref[...] = jnp.zeros_like(dnw_ref)

        nw = nw_ref[...]
        lane = lax.broadcasted_iota(jnp.int32, (c, LANE), 1)
        dgb = [jnp.zeros((c, LANE), F32) for _ in range(grp)]
        dnw = jnp.zeros((1, hd), F32)
        for h in range(nh):
            ls = slice(h * hd, (h + 1) * hd)
            dst = dstate[ls, :]
            for sub in reversed(range(grp)):
                rs_ = slice(sub * c, (sub + 1) * c)
                q, k, v = q_ref[rs_, ls], k_ref[rs_, ls], v_ref[rs_, ls]
                gb = gb_ref[rs_, :]
                beta = gb[:, nh + h:nh + h + 1]
                st = st_ref[pl.ds((sub * nh + h) * hd, hd), :]
                o, _, res = _gdn_chunk(q, k, v, gb[:, h:h + 1], beta, st)
                gate = gate_ref[rs_, ls]
                dy = dy_ref[rs_, ls].astype(F32)
                sg = _sigmoid(gate)
                rs = lax.rsqrt(jnp.mean(o * o, axis=-1, keepdims=True) + EPS)
                nrm = o * rs
                dgate_ref[rs_, ls] = dy * nrm * nw * sg * (1.0 + gate * (1.0 - sg))
                dnv = dy * (gate * sg)
                dnw = dnw + jnp.sum(dnv * nrm, axis=0, keepdims=True)
                dno = dnv * nw
                do = rs * (dno - nrm * jnp.mean(dno * nrm, axis=-1, keepdims=True))
                dq, dk, dv, dg, dbeta, dst = _gdn_chunk_bwd(q, k, v, beta, st, res, do, dst)
                dqkv_ref[rs_, ls] = dq
                dqkv_ref[rs_, slice(GDN_WIDTH + h * hd, GDN_WIDTH + (h + 1) * hd)] = dk
                dqkv_ref[rs_, slice(2 * GDN_WIDTH + h * hd, 2 * GDN_WIDTH + (h + 1) * hd)] = dv
                dgb[sub] = jnp.where(lane == h, dg, jnp.where(lane == nh + h, dbeta, dgb[sub]))
            dstate[ls, :] = dst
        for sub in range(grp):
            dgb_ref[slice(sub * c, (sub + 1) * c), :] = dgb[sub]
        dnw_ref[...] += dnw

    return pl.pallas_call(
        body, grid=(n_chunks // grp,),
        in_specs=[rev(0, GDN_WIDTH), rev(1, GDN_WIDTH), rev(2, GDN_WIDTH), rev(0, LANE),
                  rev(COL_GATE // GDN_WIDTH, GDN_WIDTH), BS((1, hd), lambda n: (0, 0)),
                  BS((grp * nh * hd, hd), lambda n: (last - n, 0)), rev(1, GDN_WIDTH)],
        out_specs=[rev(0, 3 * GDN_WIDTH), rev(0, LANE), rev(0, GDN_WIDTH), BS((1, hd), lambda n: (0, 0))],
        out_shape=[SDS((s, 3 * GDN_WIDTH), F32), SDS((s, LANE), F32), SDS((s, GDN_WIDTH), F32), SDS((1, hd), F32)],
        scratch_shapes=[pltpu.VMEM((nh * hd, hd), F32)],
        compiler_params=_cp(), name=name)(qkv, qkv, qkv, gbeta, proj, norm_w, states, dcat)


XATTN_TQ = 512


def _xattn_probs(qh, kh):
    sc = _dot(qh, kh, 1, 1) * (XATTN_HEAD_DIM ** -0.5)
    p = jnp.exp(sc - jnp.max(sc, axis=-1, keepdims=True))
    return p / jnp.sum(p, axis=-1, keepdims=True)


def _xattn_fwd(q, kv, *, name):
    s, d = q.shape
    m = kv.shape[0]
    tq, hd = _tile(s, XATTN_TQ), XATTN_HEAD_DIM

    def body(q_ref, k_ref, v_ref, o_ref):
        for h in range(XATTN_HEADS):
            ls = slice(h * hd, (h + 1) * hd)
            p = _xattn_probs(q_ref[:, ls], k_ref[:, ls])
            o_ref[:, ls] = _dot(p, v_ref[:, ls], 1, 0).astype(BF16)

    return pl.pallas_call(
        body, grid=(s // tq,),
        in_specs=[BS((tq, d), lambda i: (i, 0)), BS((m, d), lambda i: (0, 0)), BS((m, d), lambda i: (0, 1))],
        out_specs=BS((tq, d), lambda i: (i, 0)), out_shape=SDS((s, d), BF16),
        compiler_params=_cp(), name=name)(q, kv, kv)


def _xattn_bwd(q, kv, do, *, name):
    s, d = q.shape
    m = kv.shape[0]
    tq, hd = _tile(s, XATTN_TQ), XATTN_HEAD_DIM
    scale = hd ** -0.5

    def body(q_ref, k_ref, v_ref, do_ref, dq_ref, dkv_ref):
        @pl.when(pl.program_id(0) == 0)
        def _():
            dkv_ref[...] = jnp.zeros_like(dkv_ref)

        for h in range(XATTN_HEADS):
            ls = slice(h * hd, (h + 1) * hd)
            vs = slice(d + h * hd, d + (h + 1) * hd)
            qh, kh, doh = q_ref[:, ls], k_ref[:, ls], do_ref[:, ls]
            p = _xattn_probs(qh, kh)
            dp = _dot(doh, v_ref[:, ls], 1, 1)
            ds = p * (dp - jnp.sum(p * dp, axis=-1, keepdims=True)) * scale
            dq_ref[:, ls] = _dot(ds, kh, 1, 0).astype(BF16)
            dkv_ref[:, ls] += _dot(ds, qh, 0, 0)
            dkv_ref[:, vs] += _dot(p, doh, 0, 0)

    row = BS((tq, d), lambda i: (i, 0))
    return pl.pallas_call(
        body, grid=(s // tq,),
        in_specs=[row, BS((m, d), lambda i: (0, 0)), BS((m, d), lambda i: (0, 1)), row],
        out_specs=[row, BS((m, 2 * d), lambda i: (0, 0))],
        out_shape=[SDS((s, d), BF16), SDS((m, 2 * d), F32)],
        compiler_params=_cp(), name=name)(q, kv, kv, do)


def _pad_lanes(vec4):
    return jnp.zeros((1, LANE), F32).at[0, :GDN_HEADS].set(vec4)


def _layer_fwd(h0, mem, tabs, p):
    sv = dict(h0=h0)
    hn1 = _rmsnorm(h0, p["norm_mix_pre"], name="norm_mix_pre")
    proj = _mm(hn1, p["w_in"], name="mm_in")
    ya = _attn_fwd(proj, tabs, name="attn_fwd")
    yc = _sconv_fwd(proj, p["conv_short"], name="sconv_fwd")
    qkv = _gdn_pre_fwd(proj, p["conv_gdn"], name="gdn_pre_fwd")
    gbeta = _gdn_gates_fwd(proj, p["gdn_a_log"], p["gdn_dt_bias"], name="gdn_gates_fwd")
    yg, states = _gdn_core_fwd(qkv, gbeta, proj, p["gdn_norm"], name="gdn_core_fwd")
    cat = jnp.concatenate([ya, yc, yg], axis=-1)
    mix = _mm(cat, p["w_out"], name="mm_out")
    h1 = _resnorm(h0, mix, p["norm_mix_post"], name="norm_mix_post")
    hn2 = _rmsnorm(h1, p["norm_xattn_pre"], name="norm_xattn_pre")
    memn = _rmsnorm(mem, p["norm_mem"], name="norm_mem")
    xq = _mm(hn2, p["w_xq"], out_dtype=BF16, name="mm_xq")
    kv = _mm(memn, p["w_xkv"], out_dtype=BF16, name="mm_xkv")
    xo = _xattn_fwd(xq, kv, name="xattn_fwd")
    xa = _mm(xo, p["w_xo"], name="mm_xo")
    h2 = _resnorm(h1, xa, p["norm_xattn_post"], name="norm_xattn_post")
    hn3 = _rmsnorm(h2, p["norm_ffn_pre"], name="norm_ffn_pre")
    gu = _mm(hn3, p["w_gate_up"], name="mm_gate_up")
    act = _swiglu(gu, name="swiglu_fwd")
    f = _mm(act, p["w_down"], name="mm_down")
    h3 = _resnorm(h2, f, p["norm_ffn_post"], name="norm_ffn_post")
    sv.update(hn1=hn1, proj=proj, qkv=qkv, gbeta=gbeta, states=states, cat=cat, mix=mix, h1=h1, hn2=hn2,
              memn=memn, xq=xq, kv=kv, xo=xo, xa=xa, h2=h2, hn3=hn3, gu=gu, act=act, f=f)
    return h3, sv


def _layer_bwd(dh3, mem, tabs, p, sv):
    g = {}
    df, g["norm_ffn_post"] = _rmsnorm_bwd(sv["f"], p["norm_ffn_post"], dh3, name="norm_ffn_post_bwd")
    dact = _mm(df, p["w_down"], tb=True, name="mm_down_da")
    g["w_down"] = _mm(sv["act"], df, ta=True, name="mm_down_dw")
    dgu = _swiglu_bwd(sv["gu"], dact, name="swiglu_bwd")
    dhn3 = _mm(dgu, p["w_gate_up"], tb=True, name="mm_gate_up_da")
    g["w_gate_up"] = _mm(sv["hn3"], dgu, ta=True, name="mm_gate_up_dw")
    dh2, g["norm_ffn_pre"] = _rmsnorm_bwd(sv["h2"], p["norm_ffn_pre"], dhn3, res=dh3, name="norm_ffn_pre_bwd")
    dxa, g["norm_xattn_post"] = _rmsnorm_bwd(sv["xa"], p["norm_xattn_post"], dh2, name="norm_xattn_post_bwd")
    dxo = _mm(dxa, p["w_xo"], tb=True, name="mm_xo_da")
    g["w_xo"] = _mm(sv["xo"], dxa, ta=True, name="mm_xo_dw")
    dxq, dkv = _xattn_bwd(sv["xq"], sv["kv"], dxo, name="xattn_bwd")
    dhn2 = _mm(dxq, p["w_xq"], tb=True, name="mm_xq_da")
    g["w_xq"] = _mm(sv["hn2"], dxq, ta=True, name="mm_xq_dw")
    dmemn = _mm(dkv, p["w_xkv"], tb=True, name="mm_xkv_da")
    g["w_xkv"] = _mm(sv["memn"], dkv, ta=True, name="mm_xkv_dw")
    _, g["norm_mem"] = _rmsnorm_bwd(mem, p["norm_mem"], dmemn, name="norm_mem_bwd")
    dh1, g["norm_xattn_pre"] = _rmsnorm_bwd(sv["h1"], p["norm_xattn_pre"], dhn2, res=dh2, name="norm_xattn_pre_bwd")
    dmix, g["norm_mix_post"] = _rmsnorm_bwd(sv["mix"], p["norm_mix_post"], dh1, name="norm_mix_post_bwd")
    dcat = _mm(dmix, p["w_out"], tb=True, name="mm_out_da")
    g["w_out"] = _mm(sv["cat"], dmix, ta=True, name="mm_out_dw")
    proj = sv["proj"]
    daq, dak, dav = _attn_bwd(proj, tabs, dcat, name="attn_bwd")
    dcb, dcc, dcx, g["conv_short"] = _sconv_bwd(proj, p["conv_short"], dcat, name="sconv_bwd")
    dqkv, dgbeta, dgate, g["gdn_norm"] = _gdn_core_bwd(sv["qkv"], sv["gbeta"], proj, p["gdn_norm"], sv["states"],
                                                        dcat, name="gdn_core_bwd")
    dgqkv, g["conv_gdn"] = _gdn_pre_bwd(proj, p["conv_gdn"], dqkv, name="gdn_pre_bwd")
    dab, g["gdn_a_log"], g["gdn_dt_bias"] = _gdn_gates_bwd(proj, p["gdn_a_log"], p["gdn_dt_bias"], dgbeta,
                                                          name="gdn_gates_bwd")
    s = proj.shape[0]
    dproj = jnp.concatenate([daq, dak, dav, dcb, dcc, dcx, dgqkv, dgate, dab,
                             jnp.zeros((s, IN_PAD - COL_AB - LANE), F32)], axis=-1)
    dhn1 = _mm(dproj, p["w_in"], tb=True, name="mm_in_da")
    g["w_in"] = _mm(sv["hn1"], dproj, ta=True, name="mm_in_dw")
    dh0, g["norm_mix_pre"] = _rmsnorm_bwd(sv["h0"], p["norm_mix_pre"], dhn1, res=dh1, name="norm_mix_pre_bwd")
    return dh0, g


MATRICES = ("w_in", "w_out", "w_xq", "w_xkv", "w_xo", "w_gate_up", "w_down")
VECTORS = ("norm_mix_pre", "norm_mix_post", "conv_short", "conv_gdn", "gdn_a_log", "gdn_dt_bias", "gdn_norm",
           "norm_mem", "norm_xattn_pre", "norm_xattn_post", "norm_ffn_pre", "norm_ffn_post")


def _w_in_to_padded(w):
    zeros = jnp.zeros(w.shape[:-1] + (IN_PAD - IN_WIDTH,), w.dtype)
    return jnp.concatenate([w[..., :COL_GATE], w[..., COL_GATE + 8:], w[..., COL_GATE:COL_GATE + 8], zeros], axis=-1)


def _w_in_from_padded(g):
    return jnp.concatenate([g[..., :COL_GATE], g[..., COL_AB:COL_AB + 8], g[..., COL_GATE:COL_AB]], axis=-1)


def _layer_params(full, l):
    p = {n: full[n][l] for n in MATRICES}
    for n in VECTORS:
        v = full[n][l]
        if n in ("gdn_a_log", "gdn_dt_bias"):
            p[n] = _pad_lanes(v)
        elif v.ndim == 1:
            p[n] = v.reshape(1, -1)
        else:
            p[n] = v
    return p


def _local_step(x, mem, pos, target, full):
    tabs = _rope_tables(pos, name="rope_tables")
    h = x
    saved, params = [], []
    for l in range(DEPTH):
        p = _layer_params(full, l)
        h, sv = _layer_fwd(h, mem, tabs, p)
        params.append(p)
        saved.append(sv)
    loss_row, dh = _loss_grad(h, target, name="loss_grad")
    grads = [None] * DEPTH
    for l in reversed(range(DEPTH)):
        dh, grads[l] = _layer_bwd(dh, mem, tabs, params[l], saved[l])
    return loss_row, dh, grads


ANY = pl.BlockSpec(memory_space=pl.ANY)
MESH = pl.DeviceIdType.MESH


def _flip(pos, mask):
    return tuple(1 - v if m else v for v, m in zip(pos, mask))


def _exchange(ins, out_shapes, remote, local, *, name):
    n_in = len(ins)
    n_out = len(out_shapes)

    def at(ref, idx):
        return ref.at[idx] if idx else ref

    def body(*refs):
        in_refs = refs[:n_in]
        out_refs = refs[n_in:n_in + n_out]
        send_sems, recv_sems, local_sems = refs[n_in + n_out:]
        me = (lax.axis_index("x"), lax.axis_index("y"), lax.axis_index("c"))
        waits = []
        for k, (ii, src_at, oi, dst_at, mask) in enumerate(remote):
            peer = _flip(me, mask)
            pltpu.make_async_remote_copy(
                src_ref=at(in_refs[ii], src_at(me, peer)), dst_ref=at(out_refs[oi], dst_at(me)),
                send_sem=send_sems.at[k], recv_sem=recv_sems.at[k], device_id=peer, device_id_type=MESH).start()
            waits.append(pltpu.make_async_remote_copy(
                src_ref=at(in_refs[ii], src_at(peer, me)), dst_ref=at(out_refs[oi], dst_at(peer)),
                send_sem=send_sems.at[k], recv_sem=recv_sems.at[k], device_id=peer, device_id_type=MESH))
        own = []
        for k, (ii, src_at, oi, dst_at) in enumerate(local):
            cp = pltpu.make_async_copy(at(in_refs[ii], src_at(me)), at(out_refs[oi], dst_at(me)), local_sems.at[k])
            cp.start()
            own.append(cp)
        for w in waits:
            w.wait_send()
            w.wait_recv()
        for cp in own:
            cp.wait()

    return pl.pallas_call(
        body, in_specs=[ANY] * n_in, out_specs=[ANY] * n_out, out_shape=list(out_shapes),
        scratch_shapes=[pltpu.SemaphoreType.DMA((len(remote),)), pltpu.SemaphoreType.DMA((len(remote),)),
                        pltpu.SemaphoreType.DMA((max(len(local), 1),))],
        name=name)(*ins)


def _chip(pos):
    return 2 * pos[0] + pos[1]


XY_MASKS = ((1, 0, 0), (0, 1, 0), (1, 1, 0))
SIBLING = (0, 0, 1)
ALL_MASKS = tuple((a, b, c) for a in (0, 1) for b in (0, 1) for c in (0, 1))[1:]


def _gather_xy(arrs, *, name):
    n = len(arrs)
    outs = [SDS((4,) + a.shape, a.dtype) for a in arrs]
    halves = [a.shape[0] // 2 for a in arrs]

    def body(*refs):
        in_refs, out_refs = refs[:n], refs[n:2 * n]
        ici_send, ici_recv, d2d_send, d2d_recv, local_sems = refs[2 * n:]
        me = (lax.axis_index("x"), lax.axis_index("y"), lax.axis_index("c"))
        sibling = _flip(me, SIBLING)
        own = []
        for i in range(n):
            cp = pltpu.make_async_copy(in_refs[i], out_refs[i].at[_chip(me)], local_sems.at[i])
            cp.start()
            own.append(cp)
        flows = []
        for i in range(n):
            mine = pl.ds(me[2] * halves[i], halves[i])
            other = pl.ds(sibling[2] * halves[i], halves[i])
            for m in XY_MASKS:
                k = len(flows)
                peer = _flip(me, m)

                def remote(src, dst, sems, to, k=k):
                    return pltpu.make_async_remote_copy(src_ref=src, dst_ref=dst, send_sem=sems[0].at[k],
                                                        recv_sem=sems[1].at[k], device_id=to, device_id_type=MESH)

                landed = out_refs[i].at[_chip(peer), mine]
                send = remote(in_refs[i].at[mine], out_refs[i].at[_chip(me), mine], (ici_send, ici_recv), peer)
                send.start()
                arrive = remote(in_refs[i].at[mine], landed, (ici_send, ici_recv), peer)
                forward = remote(landed, landed, (d2d_send, d2d_recv), sibling)
                handed = remote(out_refs[i].at[_chip(peer), other], out_refs[i].at[_chip(peer), other],
                                (d2d_send, d2d_recv), sibling)
                flows.append((send, arrive, forward, handed))
        for _, arrive, forward, _ in flows:
            arrive.wait_recv()
            forward.start()
        for send, _, forward, handed in flows:
            handed.wait_recv()
            send.wait_send()
            forward.wait_send()
        for cp in own:
            cp.wait()

    n_flows = 3 * n
    return pl.pallas_call(
        body, in_specs=[ANY] * n, out_specs=[ANY] * n, out_shape=outs,
        scratch_shapes=[pltpu.SemaphoreType.DMA((n_flows,))] * 4 + [pltpu.SemaphoreType.DMA((n,))],
        name=name)(*arrs)


def _gather_all(arr, *, name):
    slot = lambda pos: (4 * pos[0] + 2 * pos[1] + pos[2],)
    whole = lambda *_: ()
    remote = [(0, whole, 0, slot, m) for m in ALL_MASKS]
    return _exchange([arr], [SDS((8,) + arr.shape, arr.dtype)], remote, [(0, whole, 0, slot)], name=name)[0]


def _swap_halves(arrs, *, name):
    outs = [SDS(a.shape[:2] + (a.shape[2] // 2, a.shape[3]), a.dtype) for a in arrs]

    def src(i):
        half = arrs[i].shape[2] // 2
        return lambda sender, receiver: (slice(None), slice(None), pl.ds(receiver[2] * half, half))

    whole = lambda *_: ()
    remote = [(i, src(i), i, whole, SIBLING) for i in range(len(arrs))]
    return _exchange(arrs, outs, remote, [], name=name)


def _scatter_xy(arrs, *, name):
    outs = [SDS(a.shape, a.dtype) for a in arrs]
    src = lambda sender, receiver: (_chip(receiver),)
    dst = lambda sender: (_chip(sender),)
    remote = [(i, src, i, dst, m) for i in range(len(arrs)) for m in XY_MASKS]
    local = [(i, lambda me: (_chip(me),), i, lambda me: (_chip(me),)) for i in range(len(arrs))]
    return _exchange(arrs, outs, remote, local, name=name)


def _join_halves(arrs, *, name):
    outs = [SDS((2,) + a.shape, a.dtype) for a in arrs]
    whole = lambda *_: ()
    dst = lambda sender: (sender[2],)
    remote = [(i, whole, i, dst, SIBLING) for i in range(len(arrs))]
    local = [(i, whole, i, dst) for i in range(len(arrs))]
    return _exchange(arrs, outs, remote, local, name=name)


def _add_half(g, other, core, *, name):
    n4, nl, r, c = g.shape
    half = r // 2
    g3 = g.reshape(n4 * nl, 2, half, c)
    o3 = other.reshape(n4 * nl, half, c)
    tr = _rows_tile(half) if half > 512 else half

    def body(core_ref, g_ref, o_ref, out_ref):
        out_ref[...] = (g_ref[...] + o_ref[...]).astype(BF16)

    return pl.pallas_call(
        body,
        grid_spec=pltpu.PrefetchScalarGridSpec(
            num_scalar_prefetch=1, grid=(n4 * nl, half // tr),
            in_specs=[BS((None, None, tr, c), lambda i, j, core_ref: (i, core_ref[0], j, 0)),
                      BS((None, tr, c), lambda i, j, core_ref: (i, j, 0))],
            out_specs=BS((None, tr, c), lambda i, j, core_ref: (i, j, 0))),
        out_shape=SDS((n4 * nl, half, c), BF16), compiler_params=_cp(), name=name)(core, g3, o3).reshape(n4, nl, half, c)


def _sum_chips(parts, *, name):
    n4, nl, h, c = parts.shape
    tr = _rows_tile(h) if h > 512 else h

    def body(p_ref, out_ref):
        acc = p_ref[0].astype(F32) + p_ref[1].astype(F32)
        out_ref[...] = (acc + p_ref[2].astype(F32)) + p_ref[3].astype(F32)

    return pl.pallas_call(
        body, grid=(nl, h // tr), in_specs=[BS((n4, None, tr, c), lambda i, j: (0, i, j, 0))],
        out_specs=BS((None, tr, c), lambda i, j: (i, j, 0)), out_shape=SDS((nl, h, c), F32),
        compiler_params=_cp(), name=name)(parts)


def _sum_devices(parts, *, name):
    n, r, c = parts.shape

    def body(p_ref, out_ref):
        acc = p_ref[0]
        for d in range(1, n):
            acc = acc + p_ref[d]
        out_ref[...] = acc

    return pl.pallas_call(
        body, grid=(1,), in_specs=[BS((n, r, c), lambda i: (0, 0, 0))], out_specs=BS((r, c), lambda i: (0, 0)),
        out_shape=SDS((r, c), F32), compiler_params=_cp(), name=name)(parts)


WEIGHTS = ("norm_mix_pre", "norm_mix_post", "w_in", "conv_short", "conv_gdn", "gdn_a_log", "gdn_dt_bias",
           "gdn_norm", "w_out", "norm_mem", "norm_xattn_pre", "norm_xattn_post", "w_xq", "w_xkv", "w_xo",
           "norm_ffn_pre", "norm_ffn_post", "w_gate_up", "w_down")
COL_SHARDED = ("w_in", "w_xkv", "w_gate_up", "conv_short", "conv_gdn")
ROW_SHARDED = ("w_out", "w_xq", "w_xo", "w_down")
SMALL_SHARDED = ("conv_short", "conv_gdn")
SMALL_ROW_PAD = 8


def _from_shards(n, g):
    if n in COL_SHARDED:
        t = jnp.moveaxis(g, 0, 2)
        return t.reshape(t.shape[0], t.shape[1], -1)
    t = jnp.moveaxis(g, 0, 1)
    return t.reshape(t.shape[0], -1, t.shape[-1])


def _to_shards(n, full):
    nl, r, c = full.shape
    if n in COL_SHARDED:
        return jnp.moveaxis(full.reshape(nl, r, 4, c // 4), 2, 0)
    return jnp.moveaxis(full.reshape(nl, 4, r // 4, c), 1, 0)


def _pack_small(grads):
    rows = []
    for g in grads:
        for n in WEIGHTS:
            if n not in MATRICES:
                rows.append(g[n].reshape(-1, LANE))
    n_rows = sum(r.shape[0] for r in rows)
    pad = -n_rows % SMALL_ROW_PAD
    if pad:
        rows.append(jnp.zeros((pad, LANE), F32))
    return jnp.concatenate(rows, axis=0)


def _unpack_small(packed, like):
    out, at = [], 0
    for _ in range(DEPTH):
        g = {}
        for n in WEIGHTS:
            if n not in MATRICES:
                shape = like[n].shape
                k = math.prod(shape) // LANE
                g[n] = packed[at:at + k].reshape(shape)
                at += k
        out.append(g)
    return out


def kernel(x, mem, positions, norm_mix_pre, norm_mix_post, w_in, conv_short, conv_gdn, gdn_a_log, gdn_dt_bias, gdn_norm, w_out, norm_mem, norm_xattn_pre, norm_xattn_post, w_xq, w_xkv, w_xo, norm_ffn_pre, norm_ffn_post, w_gate_up, w_down, loss_target, m_norm_mix_pre, m_norm_mix_post, m_w_in, m_conv_short, m_conv_gdn, m_gdn_a_log, m_gdn_dt_bias, m_gdn_norm, m_w_out, m_norm_mem, m_norm_xattn_pre, m_norm_xattn_post, m_w_xq, m_w_xkv, m_w_xo, m_norm_ffn_pre, m_norm_ffn_post, m_w_gate_up, m_w_down, v_norm_mix_pre, v_norm_mix_post, v_w_in, v_conv_short, v_conv_gdn, v_gdn_a_log, v_gdn_dt_bias, v_gdn_norm, v_w_out, v_norm_mem, v_norm_xattn_pre, v_norm_xattn_post, v_w_xq, v_w_xkv, v_w_xo, v_norm_ffn_pre, v_norm_ffn_post, v_w_gate_up, v_w_down):
    args = dict(locals())
    w = {n: args[n] for n in WEIGHTS}
    m = {n: args["m_" + n] for n in WEIGHTS}
    v = {n: args["v_" + n] for n in WEIGHTS}
    seq = x.shape[1]
    chip = 2 * lax.axis_index("x") + lax.axis_index("y")
    core = lax.axis_index("c").astype(jnp.int32).reshape(1)

    sharded = list(MATRICES) + list(SMALL_SHARDED)
    blocks = _gather_xy([w[n].astype(BF16) if n in MATRICES else w[n] for n in sharded], name="gather_weights")
    full = {n: _from_shards(n, b) for n, b in zip(sharded, blocks)}
    full["w_in"] = _w_in_to_padded(full["w_in"])
    for n in WEIGHTS:
        if n not in full:
            full[n] = w[n]

    loss_row, dx, grads = _local_step(x[0], mem[0], positions.reshape(seq, 1), loss_target[0], full)

    mine = []
    for n in MATRICES:
        stacked = jnp.stack([g[n] for g in grads])
        if n == "w_in":
            stacked = _w_in_from_padded(stacked)
        mine.append(_to_shards(n, stacked))
    theirs = _swap_halves(mine, name="grads_swap_halves")
    pair = [_add_half(a, b, core, name="grads_pair_sum") for a, b in zip(mine, theirs)]
    parts = _scatter_xy(pair, name="grads_scatter")
    reduced = [_sum_chips(p, name="grads_chip_sum") for p in parts]
    joined = _join_halves(reduced, name="grads_join_halves")
    grad = {n: jnp.moveaxis(j, 0, 1).reshape(w[n].shape) for n, j in zip(MATRICES, joined)}

    packed = _pack_small(grads)
    total = _sum_devices(_gather_all(packed, name="small_grads_gather"), name="small_grads_sum")
    small = _unpack_small(total, grads[0])
    for n in WEIGHTS:
        if n in MATRICES:
            continue
        g = jnp.stack([s[n] for s in small])
        if n in ("gdn_a_log", "gdn_dt_bias"):
            g = g[:, 0, :GDN_HEADS]
        elif n in SMALL_SHARDED:
            width = w[n].shape[-1]
            g = lax.dynamic_slice_in_dim(g, chip * width, width, axis=2)
        grad[n] = g.reshape(w[n].shape)

    delta, new_m, new_v = {}, {}, {}
    for n in WEIGHTS:
        shape = w[n].shape
        two_d = (-1, shape[-1])
        d, nm, nv = _adamw(w[n].reshape(two_d), grad[n].reshape(two_d), m[n].reshape(two_d), v[n].reshape(two_d),
                           name="adamw_" + n)
        delta[n], new_m[n], new_v[n] = d.reshape(shape), nm.reshape(shape), nv.reshape(shape)

    loss = lax.psum(loss_row[0, 0], ("x", "y", "c"))
    return (loss, dx.reshape(x.shape), *[grad[n] for n in WEIGHTS], *[delta[n] for n in WEIGHTS],
            *[new_m[n] for n in WEIGHTS], *[new_v[n] for n in WEIGHTS])
```

```python
import functools
import math

import jax
import jax.numpy as jnp
from jax import lax
from jax.experimental import pallas as pl
from jax.experimental.pallas import tpu as pltpu

F32 = jnp.float32
BF16 = jnp.bfloat16
BS = pl.BlockSpec
SDS = jax.ShapeDtypeStruct
PRECISE = lax.Precision.HIGH

D_MODEL = 1024
DEPTH = 4
EPS = 1e-6
ATTN_HEADS = 4
ATTN_HEAD_DIM = 64
ATTN_WIDTH = 256
ROPE_THETA = 500000.0
ROPE_DIM = 16
CONV_WIDTH = 256
CONV_K = 3
GDN_HEADS = 4
GDN_HEAD_DIM = 128
GDN_WIDTH = 512
GDN_CONV_K = 4
GDN_CHUNK = 64
IN_WIDTH = 3592
XATTN_HEADS = 4
XATTN_HEAD_DIM = 256
FFN_HIDDEN = 2816
ADAM_LR = 0.001
ADAM_B1 = 0.9
ADAM_B2 = 0.999
ADAM_EPS = 1e-08
ADAM_WD = 0.01
ADAM_STEP = 10

IN_PAD = 3840
COL_GDN = 1536
COL_GATE = 3072
COL_AB = 3584

VMEM_LIMIT_V7X = 56 * 1024 * 1024
LANE = 128


def _cp(**kw):
    return pltpu.CompilerParams(vmem_limit_bytes=VMEM_LIMIT_V7X, **kw)


def _tile(n, cap):
    if n <= cap:
        return n
    best = None
    for t in range(LANE, cap + 1, LANE):
        if n % t == 0:
            best = t
    assert best is not None, (n, cap)
    return best


def _dot(a, b, ca, cb, precise=False):
    dims = (((ca,), (cb,)), ((), ()))
    if precise:
        return lax.dot_general(a.astype(F32), b.astype(F32), dims, precision=PRECISE,
                               preferred_element_type=F32)
    return lax.dot_general(a.astype(BF16), b.astype(BF16), dims, preferred_element_type=F32)


def _sigmoid(x):
    return 1.0 / (1.0 + jnp.exp(-x))


def _mm(a, b, *, ta=False, tb=False, out_dtype=F32, name):
    m, k = (a.shape[1], a.shape[0]) if ta else a.shape
    n = b.shape[0] if tb else b.shape[1]
    assert (b.shape[1] if tb else b.shape[0]) == k
    tm = _tile(m, 512)
    tn = _tile(n, max(LANE, min(1024, (4 * 1024 * 1024) // (k * b.dtype.itemsize) // LANE * LANE)))
    ca, cb = (0 if ta else 1), (1 if tb else 0)

    def body(a_ref, b_ref, o_ref):
        o_ref[...] = _dot(a_ref[...], b_ref[...], ca, cb).astype(out_dtype)

    a_spec = BS((k, tm), lambda i, j: (0, i)) if ta else BS((tm, k), lambda i, j: (i, 0))
    b_spec = BS((tn, k), lambda i, j: (j, 0)) if tb else BS((k, tn), lambda i, j: (0, j))
    return pl.pallas_call(
        body, grid=(m // tm, n // tn), in_specs=[a_spec, b_spec],
        out_specs=BS((tm, tn), lambda i, j: (i, j)), out_shape=SDS((m, n), out_dtype),
        compiler_params=_cp(), name=name)(a, b)


def _rmsnorm(x, w, *, name):
    r, d = x.shape
    tr = _tile(r, 512)

    def body(x_ref, w_ref, o_ref):
        xv = x_ref[...]
        rs = lax.rsqrt(jnp.mean(xv * xv, axis=-1, keepdims=True) + EPS)
        o_ref[...] = (xv * rs * w_ref[...]).astype(BF16)

    return pl.pallas_call(
        body, grid=(r // tr,), in_specs=[BS((tr, d), lambda i: (i, 0)), BS((1, d), lambda i: (0, 0))],
        out_specs=BS((tr, d), lambda i: (i, 0)), out_shape=SDS((r, d), BF16),
        compiler_params=_cp(), name=name)(x, w)


def _resnorm(h, m, w, *, name):
    r, d = h.shape
    tr = _tile(r, 512)

    def body(h_ref, m_ref, w_ref, o_ref):
        mv = m_ref[...]
        rs = lax.rsqrt(jnp.mean(mv * mv, axis=-1, keepdims=True) + EPS)
        o_ref[...] = h_ref[...] + mv * rs * w_ref[...]

    row = BS((tr, d), lambda i: (i, 0))
    return pl.pallas_call(
        body, grid=(r // tr,), in_specs=[row, row, BS((1, d), lambda i: (0, 0))],
        out_specs=row, out_shape=SDS((r, d), F32), compiler_params=_cp(), name=name)(h, m, w)


def _rmsnorm_bwd(x, w, dy, res=None, *, name):
    r, d = x.shape
    tr = _tile(r, 512)
    has_res = res is not None

    def body(*refs):
        if has_res:
            x_ref, w_ref, dy_ref, res_ref, dx_ref, dw_ref = refs
        else:
            x_ref, w_ref, dy_ref, dx_ref, dw_ref = refs
        xv = x_ref[...]
        dyv = dy_ref[...].astype(F32)
        rs = lax.rsqrt(jnp.mean(xv * xv, axis=-1, keepdims=True) + EPS)
        nv = xv * rs
        dyw = dyv * w_ref[...]
        dx = rs * (dyw - nv * jnp.mean(dyw * nv, axis=-1, keepdims=True))
        if has_res:
            dx = dx + res_ref[...]
        dx_ref[...] = dx

        @pl.when(pl.program_id(0) == 0)
        def _():
            dw_ref[...] = jnp.zeros_like(dw_ref)

        dw_ref[...] += jnp.sum(dyv * nv, axis=0, keepdims=True)

    row = BS((tr, d), lambda i: (i, 0))
    vec = BS((1, d), lambda i: (0, 0))
    ins = [x, w, dy] + ([res] if has_res else [])
    return pl.pallas_call(
        body, grid=(r // tr,), in_specs=[row, vec, row] + ([row] if has_res else []),
        out_specs=[row, vec], out_shape=[SDS((r, d), F32), SDS((1, d), F32)],
        compiler_params=_cp(), name=name)(*ins)


def _swiglu(gu, *, name):
    r, h2 = gu.shape
    hid = h2 // 2
    tr, tc = _tile(r, 512), _tile(hid, 1408)
    nb = hid // tc

    def body(g_ref, u_ref, o_ref):
        g = g_ref[...]
        o_ref[...] = (g * _sigmoid(g) * u_ref[...]).astype(BF16)

    return pl.pallas_call(
        body, grid=(r // tr, nb),
        in_specs=[BS((tr, tc), lambda i, j: (i, j)), BS((tr, tc), lambda i, j: (i, j + nb))],
        out_specs=BS((tr, tc), lambda i, j: (i, j)), out_shape=SDS((r, hid), BF16),
        compiler_params=_cp(), name=name)(gu, gu)


def _swiglu_bwd(gu, dact, *, name):
    r, h2 = gu.shape
    hid = h2 // 2
    tr, tc = _tile(r, 512), _tile(hid, 1408)
    nb = hid // tc

    def body(g_ref, u_ref, d_ref, o_ref):
        g = g_ref[...]
        da = d_ref[...]
        sg = _sigmoid(g)
        dgate = da * u_ref[...] * sg * (1.0 + g * (1.0 - sg))
        dup = da * g * sg
        o_ref[...] = jnp.where(pl.program_id(1) < nb, dgate, dup).astype(BF16)

    return pl.pallas_call(
        body, grid=(r // tr, 2 * nb),
        in_specs=[BS((tr, tc), lambda i, j: (i, j % nb)), BS((tr, tc), lambda i, j: (i, j % nb + nb)),
                  BS((tr, tc), lambda i, j: (i, j % nb))],
        out_specs=BS((tr, tc), lambda i, j: (i, j)), out_shape=SDS((r, h2), BF16),
        compiler_params=_cp(), name=name)(gu, gu, dact)


def _loss_grad(h, target, *, name):
    r, d = h.shape
    tr = _tile(r, 512)

    def body(h_ref, t_ref, l_ref, g_ref):
        e = h_ref[...] - t_ref[...]
        g_ref[...] = e * (1.0 / d)

        @pl.when(pl.program_id(0) == 0)
        def _():
            l_ref[...] = jnp.zeros_like(l_ref)

        l_ref[...] += jnp.full((1, LANE), 0.5 / d, F32) * jnp.sum(e * e)

    row = BS((tr, d), lambda i: (i, 0))
    return pl.pallas_call(
        body, grid=(r // tr,), in_specs=[row, row],
        out_specs=[BS((1, LANE), lambda i: (0, 0)), row],
        out_shape=[SDS((1, LANE), F32), SDS((r, d), F32)], compiler_params=_cp(), name=name)(h, target)


def _adamw(w, g, m, v, *, name):
    r, c = w.shape
    tr = r if r <= 512 else _rows_tile(r)
    bc1 = 1.0 - ADAM_B1 ** ADAM_STEP
    bc2 = 1.0 - ADAM_B2 ** ADAM_STEP

    def body(w_ref, g_ref, m_ref, v_ref, d_ref, nm_ref, nv_ref):
        gv = g_ref[...]
        nm = ADAM_B1 * m_ref[...] + (1.0 - ADAM_B1) * gv
        nv = ADAM_B2 * v_ref[...] + (1.0 - ADAM_B2) * (gv * gv)
        d_ref[...] = -ADAM_LR * ((nm / bc1) / (jnp.sqrt(nv / bc2) + ADAM_EPS) + ADAM_WD * w_ref[...])
        nm_ref[...] = nm
        nv_ref[...] = nv

    blk = BS((tr, c), lambda i: (i, 0))
    return pl.pallas_call(
        body, grid=(r // tr,), in_specs=[blk] * 4, out_specs=[blk] * 3,
        out_shape=[SDS((r, c), F32)] * 3, compiler_params=_cp(), name=name)(w, g, m, v)


def _rows_tile(r):
    for t in (512, 256, 128, 64, 32, 16, 8):
        if r % t == 0:
            return t
    return r


def _rope_tables(pos, *, name):
    s = pos.shape[0]
    half = ROPE_DIM // 2

    def body(p_ref, c_ref, a_ref, b_ref):
        lane = lax.broadcasted_iota(jnp.int32, (s, ATTN_WIDTH), 1) & (ATTN_HEAD_DIM - 1)
        fi = (lane & (half - 1)).astype(F32)
        inv_freq = jnp.exp(fi * (-2.0 * math.log(ROPE_THETA) / ROPE_DIM))
        ang = p_ref[...].astype(F32) * inv_freq
        cs, sn = jnp.cos(ang), jnp.sin(ang)
        c_ref[...] = jnp.where(lane < ROPE_DIM, cs, 1.0)
        a_ref[...] = jnp.where(lane < half, -sn, 0.0)
        b_ref[...] = jnp.where((lane >= half) & (lane < ROPE_DIM), sn, 0.0)

    full = BS((s, ATTN_WIDTH), lambda i: (0, 0))
    return pl.pallas_call(
        body, grid=(1,), in_specs=[BS((s, 1), lambda i: (0, 0))], out_specs=[full] * 3,
        out_shape=[SDS((s, ATTN_WIDTH), F32)] * 3, compiler_params=_cp(), name=name)(pos)


def _rot(x, c, a, b):
    w = x.shape[1]
    return x * c + pltpu.roll(x, w - ROPE_DIM // 2, 1) * a + pltpu.roll(x, ROPE_DIM // 2, 1) * b


def _rot_t(dy, c, a, b):
    w = dy.shape[1]
    return dy * c + pltpu.roll(dy * a, ROPE_DIM // 2, 1) + pltpu.roll(dy * b, w - ROPE_DIM // 2, 1)


def _attn_count(q0, tq, s):
    dist = (lax.broadcasted_iota(jnp.int32, (tq, s), 0) + q0) - lax.broadcasted_iota(jnp.int32, (tq, s), 1)
    cnt = ((dist <= 128).astype(F32) + (((dist & 3) == 0) & (dist <= 512)).astype(F32)
           + ((dist & 15) == 0).astype(F32))
    return jnp.where(dist >= 0, cnt, 0.0)


def _attn_probs(qm, k, cnt):
    sc = _dot(qm, k, 1, 1)
    mx = jnp.max(jnp.where(cnt > 0.0, sc, -1e30), axis=-1, keepdims=True)
    p = cnt * jnp.exp(jnp.minimum(sc - mx, 0.0))
    return p / jnp.sum(p, axis=-1, keepdims=True)


ATTN_TQ = 256


def _attn_specs(s, tq):
    def qblk(col):
        return BS((tq, ATTN_WIDTH), lambda i: (i, col))

    def full(col):
        return BS((s, ATTN_WIDTH), lambda i: (0, col))

    return qblk, full


def _attn_fwd(proj, tabs, *, name):
    s = proj.shape[0]
    tq = ATTN_TQ
    qblk, full = _attn_specs(s, tq)
    scale = ATTN_HEAD_DIM ** -0.5

    def body(q_ref, k_ref, v_ref, cq, aq, bq, ck, ak, bk, o_ref):
        q = _rot(q_ref[...], cq[...], aq[...], bq[...]) * scale
        head = lax.broadcasted_iota(jnp.int32, (1, ATTN_WIDTH), 1) >> 6

        def block(i):
            n = (i + 1) * tq
            k = _rot(k_ref[:n, :], ck[:n, :], ak[:n, :], bk[:n, :]).astype(BF16)
            v = v_ref[:n, :].astype(BF16)
            cnt = _attn_count(i * tq, tq, n)
            acc = jnp.zeros((tq, ATTN_WIDTH), F32)
            for h in range(ATTN_HEADS):
                hm = (head == h).astype(F32)
                p = _attn_probs(q * hm, k, cnt)
                acc = acc + _dot(p, v, 1, 0) * hm
            o_ref[...] = acc.astype(BF16)

        for i in range(s // tq):
            pl.when(pl.program_id(0) == i)(functools.partial(block, i))

    return pl.pallas_call(
        body, grid=(s // tq,),
        in_specs=[qblk(0), full(1), full(2), qblk(0), qblk(0), qblk(0), full(0), full(0), full(0)],
        out_specs=BS((tq, ATTN_WIDTH), lambda i: (i, 0)), out_shape=SDS((s, ATTN_WIDTH), BF16),
        compiler_params=_cp(), name=name)(proj, proj, proj, *tabs, *tabs)


def _attn_bwd(proj, tabs, dcat, *, name):
    s = proj.shape[0]
    tq = ATTN_TQ
    nq = s // tq
    qblk, full = _attn_specs(s, tq)
    scale = ATTN_HEAD_DIM ** -0.5

    def body(q_ref, k_ref, v_ref, cq, aq, bq, ck, ak, bk, dy_ref, dq_ref, dk_ref, dv_ref, dk_acc, dv_acc):
        i = pl.program_id(0)

        @pl.when(i == 0)
        def _():
            dk_acc[...] = jnp.zeros_like(dk_acc)
            dv_acc[...] = jnp.zeros_like(dv_acc)

        q = _rot(q_ref[...], cq[...], aq[...], bq[...]) * scale
        k = _rot(k_ref[...], ck[...], ak[...], bk[...]).astype(BF16)
        v = v_ref[...].astype(BF16)
        dy = dy_ref[...].astype(F32)
        cnt = _attn_count(i * tq, tq, s)
        head = lax.broadcasted_iota(jnp.int32, (1, ATTN_WIDTH), 1) >> 6
        dq = jnp.zeros((tq, ATTN_WIDTH), F32)
        for h in range(ATTN_HEADS):
            hm = (head == h).astype(F32)
            qm = q * hm
            dym = dy * hm
            p = _attn_probs(qm, k, cnt)
            dp = _dot(dym, v, 1, 1)
            ds = p * (dp - jnp.sum(p * dp, axis=-1, keepdims=True))
            dq = dq + _dot(ds, k, 1, 0) * hm
            dk_acc[...] += _dot(ds, qm, 0, 0)
            dv_acc[...] += _dot(p, dym, 0, 0)
        dq_ref[...] = _rot_t(dq * scale, cq[...], aq[...], bq[...])

        @pl.when(i == nq - 1)
        def _():
            dk_ref[...] = _rot_t(dk_acc[...], ck[...], ak[...], bk[...])
            dv_ref[...] = dv_acc[...]

    whole = BS((s, ATTN_WIDTH), lambda i: (0, 0))
    dq, dk, dv = pl.pallas_call(
        body, grid=(nq,),
        in_specs=[qblk(0), full(1), full(2), qblk(0), qblk(0), qblk(0), full(0), full(0), full(0), qblk(0)],
        out_specs=[BS((tq, ATTN_WIDTH), lambda i: (i, 0)), whole, whole],
        out_shape=[SDS((s, ATTN_WIDTH), F32)] * 3,
        scratch_shapes=[pltpu.VMEM((s, ATTN_WIDTH), F32), pltpu.VMEM((s, ATTN_WIDTH), F32)],
        compiler_params=_cp(), name=name)(proj, proj, proj, *tabs, *tabs, dcat)
    return dq, dk, dv


def _shift_down(x, n):
    if n == 0:
        return x
    rows = lax.broadcasted_iota(jnp.int32, x.shape, 0)
    return jnp.where(rows >= n, pltpu.roll(x, n, 0), 0.0)


def _shift_up(x, n):
    if n == 0:
        return x
    t = x.shape[0]
    rows = lax.broadcasted_iota(jnp.int32, x.shape, 0)
    return jnp.where(rows < t - n, pltpu.roll(x, t - n, 0), 0.0)


def _conv_fwd(z, w, kk):
    y = z * w[kk - 1:kk, :]
    for j in range(kk - 1):
        y = y + _shift_down(z, kk - 1 - j) * w[j:j + 1, :]
    return y


def _conv_bwd(z, dy, w, kk):
    dz = dy * w[kk - 1:kk, :]
    dws = []
    for j in range(kk - 1):
        dz = dz + _shift_up(dy, kk - 1 - j) * w[j:j + 1, :]
        dws.append(jnp.sum(dy * _shift_down(z, kk - 1 - j), axis=0, keepdims=True))
    dws.append(jnp.sum(dy * z, axis=0, keepdims=True))
    return dz, jnp.concatenate(dws, axis=0)


def _sconv_fwd(proj, w, *, name):
    s = proj.shape[0]

    def body(b_ref, c_ref, x_ref, w_ref, o_ref):
        y = _conv_fwd(c_ref[...] * x_ref[...], w_ref[...], CONV_K)
        o_ref[...] = (b_ref[...] * y).astype(BF16)

    def col(j):
        return BS((s, LANE), lambda i: (0, j + i))

    return pl.pallas_call(
        body, grid=(CONV_WIDTH // LANE,), in_specs=[col(6), col(8), col(10), BS((CONV_K, LANE), lambda i: (0, i))],
        out_specs=BS((s, LANE), lambda i: (0, i)), out_shape=SDS((s, CONV_WIDTH), BF16),
        compiler_params=_cp(), name=name)(proj, proj, proj, w)


def _sconv_bwd(proj, w, dcat, *, name):
    s = proj.shape[0]

    def body(b_ref, c_ref, x_ref, w_ref, dy_ref, db_ref, dc_ref, dx_ref, dw_ref):
        cv, xv, wv = c_ref[...], x_ref[...], w_ref[...]
        dy = dy_ref[...].astype(F32)
        z = cv * xv
        db_ref[...] = dy * _conv_fwd(z, wv, CONV_K)
        dz, dw = _conv_bwd(z, dy * b_ref[...], wv, CONV_K)
        dc_ref[...] = dz * xv
        dx_ref[...] = dz * cv
        dw_ref[...] = dw

    def col(j):
        return BS((s, LANE), lambda i: (0, j + i))

    out = BS((s, LANE), lambda i: (0, i))
    wspec = BS((CONV_K, LANE), lambda i: (0, i))
    return pl.pallas_call(
        body, grid=(CONV_WIDTH // LANE,), in_specs=[col(6), col(8), col(10), wspec, col(2)],
        out_specs=[out, out, out, wspec],
        out_shape=[SDS((s, CONV_WIDTH), F32)] * 3 + [SDS((CONV_K, CONV_WIDTH), F32)],
        compiler_params=_cp(), name=name)(proj, proj, proj, w, dcat)


def _l2n(y, scale):
    r = lax.rsqrt(jnp.sum(y * y, axis=-1, keepdims=True) + EPS)
    return y * r * scale, r


def _gdn_pre_fwd(proj, w, *, name):
    s = proj.shape[0]
    nh = GDN_HEADS

    def body(x_ref, w_ref, o_ref):
        j = pl.program_id(0)
        c = _conv_fwd(x_ref[...], w_ref[...], GDN_CONV_K)
        y = c * _sigmoid(c)
        scale = jnp.where(j < nh, GDN_HEAD_DIM ** -0.5, 1.0)
        n, _ = _l2n(y, scale)
        o_ref[...] = jnp.where(j < 2 * nh, n, y)

    return pl.pallas_call(
        body, grid=(3 * nh,),
        in_specs=[BS((s, LANE), lambda j: (0, COL_GDN // LANE + j)), BS((GDN_CONV_K, LANE), lambda j: (0, j))],
        out_specs=BS((s, LANE), lambda j: (0, j)), out_shape=SDS((s, 3 * GDN_WIDTH), F32),
        compiler_params=_cp(), name=name)(proj, w)


def _gdn_pre_bwd(proj, w, dqkv, *, name):
    s = proj.shape[0]
    nh = GDN_HEADS

    def body(x_ref, w_ref, d_ref, dx_ref, dw_ref):
        j = pl.program_id(0)
        xv, wv, dn = x_ref[...], w_ref[...], d_ref[...]
        c = _conv_fwd(xv, wv, GDN_CONV_K)
        sg = _sigmoid(c)
        y = c * sg
        scale = jnp.where(j < nh, GDN_HEAD_DIM ** -0.5, 1.0)
        n, r = _l2n(y, 1.0)
        dns = dn * scale
        dy_norm = r * (dns - n * jnp.sum(dns * n, axis=-1, keepdims=True))
        dy = jnp.where(j < 2 * nh, dy_norm, dn)
        dc = dy * sg * (1.0 + c * (1.0 - sg))
        dx, dw = _conv_bwd(xv, dc, wv, GDN_CONV_K)
        dx_ref[...] = dx
        dw_ref[...] = dw

    wspec = BS((GDN_CONV_K, LANE), lambda j: (0, j))
    blk = BS((s, LANE), lambda j: (0, j))
    return pl.pallas_call(
        body, grid=(3 * nh,),
        in_specs=[BS((s, LANE), lambda j: (0, COL_GDN // LANE + j)), wspec, blk],
        out_specs=[blk, wspec], out_shape=[SDS((s, 3 * GDN_WIDTH), F32), SDS((GDN_CONV_K, 3 * GDN_WIDTH), F32)],
        compiler_params=_cp(), name=name)(proj, w, dqkv)


def _softplus(x):
    return jnp.maximum(x, 0.0) + jnp.log(1.0 + jnp.exp(-jnp.abs(x)))


def _gdn_gates_fwd(proj, a_log, dt_bias, *, name):
    s = proj.shape[0]

    def body(x_ref, al_ref, dt_ref, o_ref):
        xv = x_ref[...]
        lane = lax.broadcasted_iota(jnp.int32, xv.shape, 1)
        g = -jnp.exp(al_ref[...]) * _softplus(xv + dt_ref[...])
        o_ref[...] = jnp.where(lane < GDN_HEADS, g, jnp.where(lane < 2 * GDN_HEADS, _sigmoid(xv), 0.0))

    vec = BS((1, LANE), lambda i: (0, 0))
    return pl.pallas_call(
        body, grid=(1,), in_specs=[BS((s, LANE), lambda i: (0, COL_AB // LANE)), vec, vec],
        out_specs=BS((s, LANE), lambda i: (0, 0)), out_shape=SDS((s, LANE), F32),
        compiler_params=_cp(), name=name)(proj, a_log, dt_bias)


def _gdn_gates_bwd(proj, a_log, dt_bias, dgb, *, name):
    s = proj.shape[0]

    def body(x_ref, al_ref, dt_ref, d_ref, dx_ref, dal_ref, ddt_ref):
        xv, dv = x_ref[...], d_ref[...]
        lane = lax.broadcasted_iota(jnp.int32, xv.shape, 1)
        is_g = lane < GDN_HEADS
        ea = -jnp.exp(al_ref[...])
        z = xv + dt_ref[...]
        da = jnp.where(is_g, dv * ea * _sigmoid(z), 0.0)
        beta = _sigmoid(xv)
        dx_ref[...] = jnp.where(is_g, da, jnp.where(lane < 2 * GDN_HEADS, dv * beta * (1.0 - beta), 0.0))
        dal_ref[...] = jnp.sum(jnp.where(is_g, dv * ea * _softplus(z), 0.0), axis=0, keepdims=True)
        ddt_ref[...] = jnp.sum(da, axis=0, keepdims=True)

    vec = BS((1, LANE), lambda i: (0, 0))
    blk = BS((s, LANE), lambda i: (0, 0))
    return pl.pallas_call(
        body, grid=(1,), in_specs=[BS((s, LANE), lambda i: (0, COL_AB // LANE)), vec, vec, blk],
        out_specs=[blk, vec, vec], out_shape=[SDS((s, LANE), F32), SDS((1, LANE), F32), SDS((1, LANE), F32)],
        compiler_params=_cp(), name=name)(proj, a_log, dt_bias, dgb)


def _col_to_row(col, eye):
    return jnp.sum(jnp.where(eye, col, 0.0), axis=0, keepdims=True)


def _row_to_col(row, eye):
    return jnp.sum(jnp.where(eye, row, 0.0), axis=1, keepdims=True)


GDN_GROUP = 4
TRI_BLOCK_SHIFT = 4


def _gdn_masks(c):
    row = lax.broadcasted_iota(jnp.int32, (c, c), 0)
    col = lax.broadcasted_iota(jnp.int32, (c, c), 1)
    return dict(row=row, col=col, eye=row == col, low=row >= col, strict=row > col, upper=row <= col,
                on_diag=(row >> TRI_BLOCK_SHIFT) == (col >> TRI_BLOCK_SHIFT))


def _tri_inv(a_list, mk):
    eye_f = mk["eye"].astype(F32)
    ds = [jnp.where(mk["on_diag"], a, 0.0) for a in a_list]
    xs = [eye_f - d for d in ds]
    ps = ds
    for _ in range(3):
        ps = [_dot(p, p, 1, 0, precise=True) for p in ps]
        xs = [x + _dot(x, p, 1, 0, precise=True) for x, p in zip(xs, ps)]
    ms = [_dot(x, a - d, 1, 0, precise=True) for x, a, d in zip(xs, a_list, ds)]
    m2s = [_dot(m, m, 1, 0, precise=True) for m in ms]
    ys = [eye_f - m for m in ms]
    ys = [y + _dot(y, m2, 1, 0, precise=True) for y, m2 in zip(ys, m2s)]
    return [_dot(y, x, 1, 0, precise=True) for y, x in zip(ys, xs)]


def _gdn_pre(qs, ks, vs, gs, betas, mk, ts=None):
    c, hd = qs[0].shape
    eye, low = mk["eye"], mk["low"]
    g_rows = [_col_to_row(g, eye) for g in gs]
    d_cols = [jnp.sum(jnp.where(low, gr, 0.0), axis=1, keepdims=True) for gr in g_rows]
    d_rows = [jnp.sum(jnp.where(mk["upper"], g, 0.0), axis=0, keepdims=True) for g in gs]
    rels = [jnp.where(low, jnp.exp(jnp.minimum(dc - dr, 0.0)), 0.0) for dc, dr in zip(d_cols, d_rows)]
    d_lasts = [dc[c - 1:c, :] for dc in d_cols]
    es = [jnp.exp(dc) for dc in d_cols]
    fs = [jnp.exp(dl - dc) for dl, dc in zip(d_lasts, d_cols)]
    cds = [jnp.exp(dl) for dl in d_lasts]
    kbs = [k * b for k, b in zip(ks, betas)]
    kbqs = [jnp.concatenate([kb, q], axis=0) for kb, q in zip(kbs, qs)]
    kqk = [_dot(kbq, k, 1, 1) for kbq, k in zip(kbqs, ks)]
    kks = [x[:c, :] for x in kqk]
    qks = [x[c:, :] for x in kqk]
    if ts is None:
        ts = _tri_inv([jnp.where(mk["strict"], kk * rel, 0.0) for kk, rel in zip(kks, rels)], mk)
    vbs = [v * b for v, b in zip(vs, betas)]
    kbes = [kb * e for kb, e in zip(kbs, es)]
    uws = [_dot(t, jnp.concatenate([vb, kbe], axis=1), 1, 0) for t, vb, kbe in zip(ts, vbs, kbes)]
    out = []
    for i in range(len(qs)):
        out.append(dict(rel=rels[i], e=es[i], f=fs[i], cd=cds[i], kb=kbs[i], kbq=kbqs[i], kk=kks[i], qk=qks[i],
                        t=ts[i], u=uws[i][:, :hd], w=uws[i][:, hd:], uw=uws[i], attn=qks[i] * rels[i],
                        qd=qs[i] * es[i], kd=ks[i] * fs[i]))
    return out


def _gdn_apply(pres, sts, leaving=True):
    c = pres[0]["u"].shape[0]
    wqs = [_dot(jnp.concatenate([p["w"], p["qd"]], axis=0), st, 1, 0) for p, st in zip(pres, sts)]
    vns = [p["u"] - x[:c, :] for p, x in zip(pres, wqs)]
    os_ = [x[c:, :] + _dot(p["attn"], vn, 1, 0) for p, x, vn in zip(pres, wqs, vns)]
    if not leaving:
        return vns, os_, None
    new = [p["cd"] * st + _dot(p["kd"], vn, 0, 0) for p, st, vn in zip(pres, sts, vns)]
    return vns, os_, new


def _gdn_bwd_rest(qs, ks, vs, betas, sts, pres, vns, dos, dvns, dsts, mk):
    c, hd = qs[0].shape
    eye = mk["eye"]
    n = range(len(qs))
    dkds = [_dot(vns[i], dsts[i], 1, 1) for i in n]
    dcds = [jnp.sum(sts[i] * dsts[i]) for i in n]
    dattns = [jnp.where(mk["low"], _dot(dos[i], vns[i], 1, 1), 0.0) for i in n]
    dqdws = [_dot(jnp.concatenate([dos[i], -dvns[i]], axis=0), sts[i], 1, 1) for i in n]
    dqds = [x[:c, :] for x in dqdws]
    dws = [x[c:, :] for x in dqdws]
    dvks = [_dot(pres[i]["t"], jnp.concatenate([dvns[i], dws[i]], axis=1), 0, 0) for i in n]
    das = [jnp.where(mk["strict"], -_dot(dvks[i], pres[i]["uw"], 1, 1), 0.0) for i in n]
    dkqs = [jnp.concatenate([das[i] * pres[i]["rel"], dattns[i] * pres[i]["rel"]], axis=0) for i in n]
    dkbdq = [_dot(dkqs[i], ks[i], 1, 0) for i in n]
    dk0 = [_dot(dkqs[i], pres[i]["kbq"], 0, 0) for i in n]
    out = []
    rows1 = lax.broadcasted_iota(jnp.int32, (c, 1), 0)
    for i in n:
        p = pres[i]
        dvb, dkbe = dvks[i][:, :hd], dvks[i][:, hd:]
        grel = (das[i] * p["kk"] + dattns[i] * p["qk"]) * p["rel"]
        dkb = dkbdq[i][:c, :] + dkbe * p["e"]
        dk = dk0[i] + dkds[i] * p["f"] + dkb * betas[i]
        dq = dkbdq[i][c:, :] + dqds[i] * p["e"]
        dv = dvb * betas[i]
        dbeta = jnp.sum(dkb * ks[i], axis=1, keepdims=True) + jnp.sum(dvb * vs[i], axis=1, keepdims=True)
        de = jnp.sum(dqds[i] * qs[i], axis=1, keepdims=True) + jnp.sum(dkbe * p["kb"], axis=1, keepdims=True)
        dff = jnp.sum(dkds[i] * ks[i], axis=1, keepdims=True) * p["f"]
        dd = (de * p["e"] - dff + jnp.sum(grel, axis=1, keepdims=True)
              - _row_to_col(jnp.sum(grel, axis=0, keepdims=True), eye))
        dd = dd + jnp.where(rows1 == c - 1, jnp.sum(dff) + dcds[i] * p["cd"], 0.0)
        dg = jnp.sum(jnp.where(mk["upper"], _col_to_row(dd, eye), 0.0), axis=1, keepdims=True)
        out.append((dq, dk, dv, dg, dbeta))
    return out


def _gdn_specs(c):
    def qkv(j):
        return BS((c, GDN_WIDTH), lambda n: (n, j))

    return qkv


def _gdn_core_fwd(qkv, gbeta, proj, norm_w, *, name):
    s = qkv.shape[0]
    c, nh, hd, grp = GDN_CHUNK, GDN_HEADS, GDN_HEAD_DIM, GDN_GROUP
    n_chunks = s // c
    blk = _gdn_specs(grp * c)
    inst = [(sub, h) for sub in range(grp) for h in range(nh)]

    def body(q_ref, k_ref, v_ref, gb_ref, gate_ref, nw_ref, y_ref, st_ref, t_ref, state):
        @pl.when(pl.program_id(0) == 0)
        def _():
            state[...] = jnp.zeros_like(state)

        mk = _gdn_masks(c)
        rows = [slice(sub * c, (sub + 1) * c) for sub in range(grp)]
        lanes = [slice(h * hd, (h + 1) * hd) for h in range(nh)]
        gbs = [gb_ref[r, :] for r in rows]
        pres = _gdn_pre([q_ref[rows[sub], lanes[h]] for sub, h in inst], [k_ref[rows[sub], lanes[h]] for sub, h in inst],
                        [v_ref[rows[sub], lanes[h]] for sub, h in inst], [gbs[sub][:, h:h + 1] for sub, h in inst],
                        [gbs[sub][:, nh + h:nh + h + 1] for sub, h in inst], mk)
        sts = [state[ls, :] for ls in lanes]
        outs = []
        for sub in range(grp):
            for h in range(nh):
                st_ref[pl.ds((sub * nh + h) * hd, hd), :] = sts[h]
            _, os_, sts = _gdn_apply(pres[sub * nh:(sub + 1) * nh], sts)
            outs += os_
        for h in range(nh):
            state[lanes[h], :] = sts[h]
        nw = nw_ref[...]
        for i, (sub, h) in enumerate(inst):
            t_ref[pl.ds(i * c, c), :] = pres[i]["t"]
            o = outs[i]
            gate = gate_ref[rows[sub], lanes[h]]
            rs = lax.rsqrt(jnp.mean(o * o, axis=-1, keepdims=True) + EPS)
            y_ref[rows[sub], lanes[h]] = (o * rs * nw * (gate * _sigmoid(gate))).astype(BF16)

    return pl.pallas_call(
        body, grid=(n_chunks // grp,),
        in_specs=[blk(0), blk(1), blk(2), BS((grp * c, LANE), lambda n: (n, 0)),
                  BS((grp * c, GDN_WIDTH), lambda n: (n, COL_GATE // GDN_WIDTH)), BS((1, hd), lambda n: (0, 0))],
        out_specs=[BS((grp * c, GDN_WIDTH), lambda n: (n, 0)), BS((grp * nh * hd, hd), lambda n: (n, 0)),
                   BS((grp * nh * c, c), lambda n: (n, 0))],
        out_shape=[SDS((s, GDN_WIDTH), BF16), SDS((n_chunks * nh * hd, hd), F32), SDS((n_chunks * nh * c, c), F32)],
        scratch_shapes=[pltpu.VMEM((nh * hd, hd), F32)],
        compiler_params=_cp(), name=name)(qkv, qkv, qkv, gbeta, proj, norm_w)


def _gdn_core_bwd(qkv, gbeta, proj, norm_w, states, tinv, dcat, *, name):
    s = qkv.shape[0]
    c, nh, hd, grp = GDN_CHUNK, GDN_HEADS, GDN_HEAD_DIM, GDN_GROUP
    n_chunks = s // c
    last = n_chunks // grp - 1
    inst = [(sub, h) for sub in range(grp) for h in range(nh)]

    def rev(j, w):
        return BS((grp * c, w), lambda n: (last - n, j))

    def body(q_ref, k_ref, v_ref, gb_ref, gate_ref, nw_ref, st_ref, t_ref, dy_ref,
             dqkv_ref, dgb_ref, dgate_ref, dnw_ref, dstate):
        @pl.when(pl.program_id(0) == 0)
        def _():
            dstate[...] = jnp.zeros_like(dstate)
            dnw_ref[...] = jnp.zeros_like(dnw_ref)

        mk = _gdn_masks(c)
        rows = [slice(sub * c, (sub + 1) * c) for sub in range(grp)]
        lanes = [slice(h * hd, (h + 1) * hd) for h in range(nh)]
        gbs = [gb_ref[r, :] for r in rows]
        qs = [q_ref[rows[sub], lanes[h]] for sub, h in inst]
        ks = [k_ref[rows[sub], lanes[h]] for sub, h in inst]
        vs = [v_ref[rows[sub], lanes[h]] for sub, h in inst]
        betas = [gbs[sub][:, nh + h:nh + h + 1] for sub, h in inst]
        sts = [st_ref[pl.ds(i * hd, hd), :] for i in range(len(inst))]
        pres = _gdn_pre(qs, ks, vs, [gbs[sub][:, h:h + 1] for sub, h in inst], betas, mk,
                        ts=[t_ref[pl.ds(i * c, c), :] for i in range(len(inst))])
        vns, outs, _ = _gdn_apply(pres, sts, leaving=False)

        nw = nw_ref[...]
        dnw = jnp.zeros((1, hd), F32)
        dos = []
        for i, (sub, h) in enumerate(inst):
            o = outs[i]
            gate = gate_ref[rows[sub], lanes[h]]
            dy = dy_ref[rows[sub], lanes[h]].astype(F32)
            sg = _sigmoid(gate)
            rs = lax.rsqrt(jnp.mean(o * o, axis=-1, keepdims=True) + EPS)
            nrm = o * rs
            dgate_ref[rows[sub], lanes[h]] = dy * nrm * nw * sg * (1.0 + gate * (1.0 - sg))
            dnv = dy * (gate * sg)
            dnw = dnw + jnp.sum(dnv * nrm, axis=0, keepdims=True)
            dno = dnv * nw
            dos.append(rs * (dno - nrm * jnp.mean(dno * nrm, axis=-1, keepdims=True)))
        dnw_ref[...] += dnw

        from_o = [_dot(p["attn"], do, 0, 0) for p, do in zip(pres, dos)]
        to_st = [_dot(p["qd"], do, 0, 0) for p, do in zip(pres, dos)]
        dst = [dstate[ls, :] for ls in lanes]
        dsts = [None] * len(inst)
        dvns = [None] * len(inst)
        for sub in reversed(range(grp)):
            idx = [sub * nh + h for h in range(nh)]
            for h, i in enumerate(idx):
                dsts[i] = dst[h]
                dvns[i] = from_o[i] + _dot(pres[i]["kd"], dst[h], 1, 0)
            dst = [pres[i]["cd"] * dst[h] + to_st[i] - _dot(pres[i]["w"], dvns[i], 0, 0) for h, i in enumerate(idx)]
        for h in range(nh):
            dstate[lanes[h], :] = dst[h]

        grads = _gdn_bwd_rest(qs, ks, vs, betas, sts, pres, vns, dos, dvns, dsts, mk)
        lane = lax.broadcasted_iota(jnp.int32, (c, LANE), 1)
        dgb = [jnp.zeros((c, LANE), F32) for _ in range(grp)]
        for (sub, h), (dq, dk, dv, dg, dbeta) in zip(inst, grads):
            dqkv_ref[rows[sub], lanes[h]] = dq
            dqkv_ref[rows[sub], slice(GDN_WIDTH + h * hd, GDN_WIDTH + (h + 1) * hd)] = dk
            dqkv_ref[rows[sub], slice(2 * GDN_WIDTH + h * hd, 2 * GDN_WIDTH + (h + 1) * hd)] = dv
            dgb[sub] = jnp.where(lane == h, dg, jnp.where(lane == nh + h, dbeta, dgb[sub]))
        for sub in range(grp):
            dgb_ref[rows[sub], :] = dgb[sub]

    return pl.pallas_call(
        body, grid=(n_chunks // grp,),
        in_specs=[rev(0, GDN_WIDTH), rev(1, GDN_WIDTH), rev(2, GDN_WIDTH), rev(0, LANE),
                  rev(COL_GATE // GDN_WIDTH, GDN_WIDTH), BS((1, hd), lambda n: (0, 0)),
                  BS((grp * nh * hd, hd), lambda n: (last - n, 0)), BS((grp * nh * c, c), lambda n: (last - n, 0)),
                  rev(1, GDN_WIDTH)],
        out_specs=[rev(0, 3 * GDN_WIDTH), rev(0, LANE), rev(0, GDN_WIDTH), BS((1, hd), lambda n: (0, 0))],
        out_shape=[SDS((s, 3 * GDN_WIDTH), F32), SDS((s, LANE), F32), SDS((s, GDN_WIDTH), F32), SDS((1, hd), F32)],
        scratch_shapes=[pltpu.VMEM((nh * hd, hd), F32)],
        compiler_params=_cp(), name=name)(qkv, qkv, qkv, gbeta, proj, norm_w, states, tinv, dcat)


XATTN_TQ = 512


def _xattn_probs(qh, kh):
    sc = _dot(qh, kh, 1, 1) * (XATTN_HEAD_DIM ** -0.5)
    p = jnp.exp(sc - jnp.max(sc, axis=-1, keepdims=True))
    return p / jnp.sum(p, axis=-1, keepdims=True)


def _xattn_fwd(q, kv, *, name):
    s, d = q.shape
    m = kv.shape[0]
    tq, hd = _tile(s, XATTN_TQ), XATTN_HEAD_DIM

    def body(q_ref, k_ref, v_ref, o_ref):
        for h in range(XATTN_HEADS):
            ls = slice(h * hd, (h + 1) * hd)
            p = _xattn_probs(q_ref[:, ls], k_ref[:, ls])
            o_ref[:, ls] = _dot(p, v_ref[:, ls], 1, 0).astype(BF16)

    return pl.pallas_call(
        body, grid=(s // tq,),
        in_specs=[BS((tq, d), lambda i: (i, 0)), BS((m, d), lambda i: (0, 0)), BS((m, d), lambda i: (0, 1))],
        out_specs=BS((tq, d), lambda i: (i, 0)), out_shape=SDS((s, d), BF16),
        compiler_params=_cp(), name=name)(q, kv, kv)


def _xattn_bwd(q, kv, do, *, name):
    s, d = q.shape
    m = kv.shape[0]
    tq, hd = _tile(s, XATTN_TQ), XATTN_HEAD_DIM
    scale = hd ** -0.5

    def body(q_ref, k_ref, v_ref, do_ref, dq_ref, dkv_ref):
        @pl.when(pl.program_id(0) == 0)
        def _():
            dkv_ref[...] = jnp.zeros_like(dkv_ref)

        for h in range(XATTN_HEADS):
            ls = slice(h * hd, (h + 1) * hd)
            vs = slice(d + h * hd, d + (h + 1) * hd)
            qh, kh, doh = q_ref[:, ls], k_ref[:, ls], do_ref[:, ls]
            p = _xattn_probs(qh, kh)
            dp = _dot(doh, v_ref[:, ls], 1, 1)
            ds = p * (dp - jnp.sum(p * dp, axis=-1, keepdims=True)) * scale
            dq_ref[:, ls] = _dot(ds, kh, 1, 0).astype(BF16)
            dkv_ref[:, ls] += _dot(ds, qh, 0, 0)
            dkv_ref[:, vs] += _dot(p, doh, 0, 0)

    row = BS((tq, d), lambda i: (i, 0))
    return pl.pallas_call(
        body, grid=(s // tq,),
        in_specs=[row, BS((m, d), lambda i: (0, 0)), BS((m, d), lambda i: (0, 1)), row],
        out_specs=[row, BS((m, 2 * d), lambda i: (0, 0))],
        out_shape=[SDS((s, d), BF16), SDS((m, 2 * d), F32)],
        compiler_params=_cp(), name=name)(q, kv, kv, do)


def _pad_lanes(vec4):
    return jnp.zeros((1, LANE), F32).at[0, :GDN_HEADS].set(vec4)


def _layer_fwd(h0, mem, tabs, p):
    sv = dict(h0=h0)
    hn1 = _rmsnorm(h0, p["norm_mix_pre"], name="norm_mix_pre")
    proj = _mm(hn1, p["w_in"], name="mm_in")
    ya = _attn_fwd(proj, tabs, name="attn_fwd")
    yc = _sconv_fwd(proj, p["conv_short"], name="sconv_fwd")
    qkv = _gdn_pre_fwd(proj, p["conv_gdn"], name="gdn_pre_fwd")
    gbeta = _gdn_gates_fwd(proj, p["gdn_a_log"], p["gdn_dt_bias"], name="gdn_gates_fwd")
    yg, states, tinv = _gdn_core_fwd(qkv, gbeta, proj, p["gdn_norm"], name="gdn_core_fwd")
    cat = jnp.concatenate([ya, yc, yg], axis=-1)
    mix = _mm(cat, p["w_out"], name="mm_out")
    h1 = _resnorm(h0, mix, p["norm_mix_post"], name="norm_mix_post")
    hn2 = _rmsnorm(h1, p["norm_xattn_pre"], name="norm_xattn_pre")
    memn = _rmsnorm(mem, p["norm_mem"], name="norm_mem")
    xq = _mm(hn2, p["w_xq"], out_dtype=BF16, name="mm_xq")
    kv = _mm(memn, p["w_xkv"], out_dtype=BF16, name="mm_xkv")
    xo = _xattn_fwd(xq, kv, name="xattn_fwd")
    xa = _mm(xo, p["w_xo"], name="mm_xo")
    h2 = _resnorm(h1, xa, p["norm_xattn_post"], name="norm_xattn_post")
    hn3 = _rmsnorm(h2, p["norm_ffn_pre"], name="norm_ffn_pre")
    gu = _mm(hn3, p["w_gate_up"], name="mm_gate_up")
    act = _swiglu(gu, name="swiglu_fwd")
    f = _mm(act, p["w_down"], name="mm_down")
    h3 = _resnorm(h2, f, p["norm_ffn_post"], name="norm_ffn_post")
    sv.update(hn1=hn1, proj=proj, qkv=qkv, gbeta=gbeta, states=states, tinv=tinv, cat=cat, mix=mix, h1=h1, hn2=hn2,
              memn=memn, xq=xq, kv=kv, xo=xo, xa=xa, h2=h2, hn3=hn3, gu=gu, act=act, f=f)
    return h3, sv


def _layer_bwd(dh3, mem, tabs, p, sv):
    g = {}
    df, g["norm_ffn_post"] = _rmsnorm_bwd(sv["f"], p["norm_ffn_post"], dh3, name="norm_ffn_post_bwd")
    dact = _mm(df, p["w_down"], tb=True, name="mm_down_da")
    g["w_down"] = _mm(sv["act"], df, ta=True, name="mm_down_dw")
    dgu = _swiglu_bwd(sv["gu"], dact, name="swiglu_bwd")
    dhn3 = _mm(dgu, p["w_gate_up"], tb=True, name="mm_gate_up_da")
    g["w_gate_up"] = _mm(sv["hn3"], dgu, ta=True, name="mm_gate_up_dw")
    dh2, g["norm_ffn_pre"] = _rmsnorm_bwd(sv["h2"], p["norm_ffn_pre"], dhn3, res=dh3, name="norm_ffn_pre_bwd")
    dxa, g["norm_xattn_post"] = _rmsnorm_bwd(sv["xa"], p["norm_xattn_post"], dh2, name="norm_xattn_post_bwd")
    dxo = _mm(dxa, p["w_xo"], tb=True, name="mm_xo_da")
    g["w_xo"] = _mm(sv["xo"], dxa, ta=True, name="mm_xo_dw")
    dxq, dkv = _xattn_bwd(sv["xq"], sv["kv"], dxo, name="xattn_bwd")
    dhn2 = _mm(dxq, p["w_xq"], tb=True, name="mm_xq_da")
    g["w_xq"] = _mm(sv["hn2"], dxq, ta=True, name="mm_xq_dw")
    dmemn = _mm(dkv, p["w_xkv"], tb=True, name="mm_xkv_da")
    g["w_xkv"] = _mm(sv["memn"], dkv, ta=True, name="mm_xkv_dw")
    _, g["norm_mem"] = _rmsnorm_bwd(mem, p["norm_mem"], dmemn, name="norm_mem_bwd")
    dh1, g["norm_xattn_pre"] = _rmsnorm_bwd(sv["h1"], p["norm_xattn_pre"], dhn2, res=dh2, name="norm_xattn_pre_bwd")
    dmix, g["norm_mix_post"] = _rmsnorm_bwd(sv["mix"], p["norm_mix_post"], dh1, name="norm_mix_post_bwd")
    dcat = _mm(dmix, p["w_out"], tb=True, name="mm_out_da")
    g["w_out"] = _mm(sv["cat"], dmix, ta=True, name="mm_out_dw")
    proj = sv["proj"]
    daq, dak, dav = _attn_bwd(proj, tabs, dcat, name="attn_bwd")
    dcb, dcc, dcx, g["conv_short"] = _sconv_bwd(proj, p["conv_short"], dcat, name="sconv_bwd")
    dqkv, dgbeta, dgate, g["gdn_norm"] = _gdn_core_bwd(sv["qkv"], sv["gbeta"], proj, p["gdn_norm"], sv["states"], sv["tinv"],
                                                        dcat, name="gdn_core_bwd")
    dgqkv, g["conv_gdn"] = _gdn_pre_bwd(proj, p["conv_gdn"], dqkv, name="gdn_pre_bwd")
    dab, g["gdn_a_log"], g["gdn_dt_bias"] = _gdn_gates_bwd(proj, p["gdn_a_log"], p["gdn_dt_bias"], dgbeta,
                                                          name="gdn_gates_bwd")
    s = proj.shape[0]
    dproj = jnp.concatenate([daq, dak, dav, dcb, dcc, dcx, dgqkv, dgate, dab,
                             jnp.zeros((s, IN_PAD - COL_AB - LANE), F32)], axis=-1)
    dhn1 = _mm(dproj, p["w_in"], tb=True, name="mm_in_da")
    g["w_in"] = _mm(sv["hn1"], dproj, ta=True, name="mm_in_dw")
    dh0, g["norm_mix_pre"] = _rmsnorm_bwd(sv["h0"], p["norm_mix_pre"], dhn1, res=dh1, name="norm_mix_pre_bwd")
    return dh0, g


MATRICES = ("w_in", "w_out", "w_xq", "w_xkv", "w_xo", "w_gate_up", "w_down")
VECTORS = ("norm_mix_pre", "norm_mix_post", "conv_short", "conv_gdn", "gdn_a_log", "gdn_dt_bias", "gdn_norm",
           "norm_mem", "norm_xattn_pre", "norm_xattn_post", "norm_ffn_pre", "norm_ffn_post")


def _w_in_to_padded(w):
    zeros = jnp.zeros(w.shape[:-1] + (IN_PAD - IN_WIDTH,), w.dtype)
    return jnp.concatenate([w[..., :COL_GATE], w[..., COL_GATE + 8:], w[..., COL_GATE:COL_GATE + 8], zeros], axis=-1)


def _w_in_from_padded(g):
    return jnp.concatenate([g[..., :COL_GATE], g[..., COL_AB:COL_AB + 8], g[..., COL_GATE:COL_AB]], axis=-1)


def _layer_params(full, l):
    p = {n: full[n][l] for n in MATRICES}
    for n in VECTORS:
        v = full[n][l]
        if n in ("gdn_a_log", "gdn_dt_bias"):
            p[n] = _pad_lanes(v)
        elif v.ndim == 1:
            p[n] = v.reshape(1, -1)
        else:
            p[n] = v
    return p


def _local_step(x, mem, pos, target, full):
    tabs = _rope_tables(pos, name="rope_tables")
    h = x
    saved, params = [], []
    for l in range(DEPTH):
        p = _layer_params(full, l)
        h, sv = _layer_fwd(h, mem, tabs, p)
        params.append(p)
        saved.append(sv)
    loss_row, dh = _loss_grad(h, target, name="loss_grad")
    grads = [None] * DEPTH
    for l in reversed(range(DEPTH)):
        dh, grads[l] = _layer_bwd(dh, mem, tabs, params[l], saved[l])
    return loss_row, dh, grads


ANY = pl.BlockSpec(memory_space=pl.ANY)
MESH = pl.DeviceIdType.MESH


def _flip(pos, mask):
    return tuple(1 - v if m else v for v, m in zip(pos, mask))


def _exchange(ins, out_shapes, remote, local, *, name):
    n_in = len(ins)
    n_out = len(out_shapes)

    def at(ref, idx):
        return ref.at[idx] if idx else ref

    def body(*refs):
        in_refs = refs[:n_in]
        out_refs = refs[n_in:n_in + n_out]
        send_sems, recv_sems, local_sems = refs[n_in + n_out:]
        me = (lax.axis_index("x"), lax.axis_index("y"), lax.axis_index("c"))
        waits = []
        for k, (ii, src_at, oi, dst_at, mask) in enumerate(remote):
            peer = _flip(me, mask)
            pltpu.make_async_remote_copy(
                src_ref=at(in_refs[ii], src_at(me, peer)), dst_ref=at(out_refs[oi], dst_at(me)),
                send_sem=send_sems.at[k], recv_sem=recv_sems.at[k], device_id=peer, device_id_type=MESH).start()
            waits.append(pltpu.make_async_remote_copy(
                src_ref=at(in_refs[ii], src_at(peer, me)), dst_ref=at(out_refs[oi], dst_at(peer)),
                send_sem=send_sems.at[k], recv_sem=recv_sems.at[k], device_id=peer, device_id_type=MESH))
        own = []
        for k, (ii, src_at, oi, dst_at) in enumerate(local):
            cp = pltpu.make_async_copy(at(in_refs[ii], src_at(me)), at(out_refs[oi], dst_at(me)), local_sems.at[k])
            cp.start()
            own.append(cp)
        for w in waits:
            w.wait_send()
            w.wait_recv()
        for cp in own:
            cp.wait()

    return pl.pallas_call(
        body, in_specs=[ANY] * n_in, out_specs=[ANY] * n_out, out_shape=list(out_shapes),
        scratch_shapes=[pltpu.SemaphoreType.DMA((len(remote),)), pltpu.SemaphoreType.DMA((len(remote),)),
                        pltpu.SemaphoreType.DMA((max(len(local), 1),))],
        name=name)(*ins)


def _chip(pos):
    return 2 * pos[0] + pos[1]


XY_MASKS = ((1, 0, 0), (0, 1, 0), (1, 1, 0))
SIBLING = (0, 0, 1)
ALL_MASKS = tuple((a, b, c) for a in (0, 1) for b in (0, 1) for c in (0, 1))[1:]


def _gather_xy(arrs, *, name):
    n = len(arrs)
    outs = [SDS((4,) + a.shape, a.dtype) for a in arrs]
    halves = [a.shape[0] // 2 for a in arrs]

    def body(*refs):
        in_refs, out_refs = refs[:n], refs[n:2 * n]
        ici_send, ici_recv, d2d_send, d2d_recv = refs[2 * n:]
        me = (lax.axis_index("x"), lax.axis_index("y"), lax.axis_index("c"))
        sibling = _flip(me, SIBLING)
        flows = []
        for i in range(n):
            mine = pl.ds(me[2] * halves[i], halves[i])
            other = pl.ds(sibling[2] * halves[i], halves[i])
            for m in XY_MASKS:
                k = len(flows)
                peer = _flip(me, m)

                def remote(src, dst, sems, to, k=k):
                    return pltpu.make_async_remote_copy(src_ref=src, dst_ref=dst, send_sem=sems[0].at[k],
                                                        recv_sem=sems[1].at[k], device_id=to, device_id_type=MESH)

                landed = out_refs[i].at[_chip(peer), mine]
                send = remote(in_refs[i].at[mine], out_refs[i].at[_chip(me), mine], (ici_send, ici_recv), peer)
                send.start()
                arrive = remote(in_refs[i].at[mine], landed, (ici_send, ici_recv), peer)
                forward = remote(landed, landed, (d2d_send, d2d_recv), sibling)
                handed = remote(out_refs[i].at[_chip(peer), other], out_refs[i].at[_chip(peer), other],
                                (d2d_send, d2d_recv), sibling)
                flows.append((send, arrive, forward, handed))
        for _, arrive, forward, _ in flows:
            arrive.wait_recv()
            forward.start()
        for send, _, forward, handed in flows:
            handed.wait_recv()
            send.wait_send()
            forward.wait_send()

    n_flows = 3 * n
    return pl.pallas_call(
        body, in_specs=[ANY] * n, out_specs=[ANY] * n, out_shape=outs,
        scratch_shapes=[pltpu.SemaphoreType.DMA((n_flows,))] * 4, name=name)(*arrs)


def _gather_all(arr, *, name):
    slot = lambda pos: (4 * pos[0] + 2 * pos[1] + pos[2],)
    whole = lambda *_: ()
    remote = [(0, whole, 0, slot, m) for m in ALL_MASKS]
    return _exchange([arr], [SDS((8,) + arr.shape, arr.dtype)], remote, [(0, whole, 0, slot)], name=name)[0]


def _swap_halves(arrs, *, name):
    outs = [SDS(a.shape[:2] + (a.shape[2] // 2, a.shape[3]), a.dtype) for a in arrs]

    def src(i):
        half = arrs[i].shape[2] // 2
        return lambda sender, receiver: (slice(None), slice(None), pl.ds(receiver[2] * half, half))

    whole = lambda *_: ()
    remote = [(i, src(i), i, whole, SIBLING) for i in range(len(arrs))]
    return _exchange(arrs, outs, remote, [], name=name)


def _scatter_xy(arrs, *, name):
    outs = [SDS((len(XY_MASKS),) + a.shape[1:], a.dtype) for a in arrs]
    src = lambda sender, receiver: (_chip(receiver),)
    remote = [(i, src, i, (lambda sender, j=j: (j,)), m) for i in range(len(arrs)) for j, m in enumerate(XY_MASKS)]
    return _exchange(arrs, outs, remote, [], name=name)


def _send_to_sibling(arrs, *, name):
    outs = [SDS(a.shape, a.dtype) for a in arrs]
    whole = lambda *_: ()
    remote = [(i, whole, i, whole, SIBLING) for i in range(len(arrs))]
    return _exchange(arrs, outs, remote, [], name=name)


def _add_half(g, other, core, *, name):
    n4, nl, r, c = g.shape
    half = r // 2
    g3 = g.reshape(n4 * nl, 2, half, c)
    o3 = other.reshape(n4 * nl, half, c)
    tr = _rows_tile(half) if half > 512 else half

    def body(core_ref, g_ref, o_ref, out_ref):
        out_ref[...] = (g_ref[...] + o_ref[...]).astype(BF16)

    return pl.pallas_call(
        body,
        grid_spec=pltpu.PrefetchScalarGridSpec(
            num_scalar_prefetch=1, grid=(n4 * nl, half // tr),
            in_specs=[BS((None, None, tr, c), lambda i, j, core_ref: (i, core_ref[0], j, 0)),
                      BS((None, tr, c), lambda i, j, core_ref: (i, j, 0))],
            out_specs=BS((None, tr, c), lambda i, j, core_ref: (i, j, 0))),
        out_shape=SDS((n4 * nl, half, c), BF16), compiler_params=_cp(), name=name)(core, g3, o3).reshape(n4, nl, half, c)


def _sum_chips(parts, mine, chip, *, name):
    n4, nl, h, c = mine.shape
    tr = _rows_tile(h) if h > 512 else h

    def body(chip_ref, p_ref, own_ref, out_ref):
        me = chip_ref[0]
        own = own_ref[...].astype(F32)
        across = [p_ref[j].astype(F32) for j in range(len(XY_MASKS))]
        t = []
        for s in range(n4):
            rel = s ^ me
            t.append(jnp.where(rel == 0, own, jnp.where(rel == 2, across[0], jnp.where(rel == 1, across[1], across[2]))))
        out_ref[...] = ((t[0] + t[1]) + t[2]) + t[3]

    return pl.pallas_call(
        body,
        grid_spec=pltpu.PrefetchScalarGridSpec(
            num_scalar_prefetch=1, grid=(nl, h // tr),
            in_specs=[BS((len(XY_MASKS), None, tr, c), lambda i, j, chip_ref: (0, i, j, 0)),
                      BS((None, None, tr, c), lambda i, j, chip_ref: (chip_ref[0], i, j, 0))],
            out_specs=BS((None, tr, c), lambda i, j, chip_ref: (i, j, 0))),
        out_shape=SDS((nl, h, c), F32), compiler_params=_cp(), name=name)(chip, parts, mine)


def _sum_devices(parts, *, name):
    n, r, c = parts.shape

    def body(p_ref, out_ref):
        acc = p_ref[0]
        for d in range(1, n):
            acc = acc + p_ref[d]
        out_ref[...] = acc

    return pl.pallas_call(
        body, grid=(1,), in_specs=[BS((n, r, c), lambda i: (0, 0, 0))], out_specs=BS((r, c), lambda i: (0, 0)),
        out_shape=SDS((r, c), F32), compiler_params=_cp(), name=name)(parts)


WEIGHTS = ("norm_mix_pre", "norm_mix_post", "w_in", "conv_short", "conv_gdn", "gdn_a_log", "gdn_dt_bias",
           "gdn_norm", "w_out", "norm_mem", "norm_xattn_pre", "norm_xattn_post", "w_xq", "w_xkv", "w_xo",
           "norm_ffn_pre", "norm_ffn_post", "w_gate_up", "w_down")
COL_SHARDED = ("w_in", "w_xkv", "w_gate_up", "conv_short", "conv_gdn")
ROW_SHARDED = ("w_out", "w_xq", "w_xo", "w_down")
SMALL_SHARDED = ("conv_short", "conv_gdn")
SMALL_ROW_PAD = 8


def _from_shards(n, g):
    if n in COL_SHARDED:
        t = jnp.moveaxis(g, 0, 2)
        return t.reshape(t.shape[0], t.shape[1], -1)
    t = jnp.moveaxis(g, 0, 1)
    return t.reshape(t.shape[0], -1, t.shape[-1])


def _to_shards(n, full):
    nl, r, c = full.shape
    if n in COL_SHARDED:
        return jnp.moveaxis(full.reshape(nl, r, 4, c // 4), 2, 0)
    return jnp.moveaxis(full.reshape(nl, 4, r // 4, c), 1, 0)


def _pack_small(grads):
    rows = []
    for g in grads:
        for n in WEIGHTS:
            if n not in MATRICES:
                rows.append(g[n].reshape(-1, LANE))
    n_rows = sum(r.shape[0] for r in rows)
    pad = -n_rows % SMALL_ROW_PAD
    if pad:
        rows.append(jnp.zeros((pad, LANE), F32))
    return jnp.concatenate(rows, axis=0)


def _unpack_small(packed, like):
    out, at = [], 0
    for _ in range(DEPTH):
        g = {}
        for n in WEIGHTS:
            if n not in MATRICES:
                shape = like[n].shape
                k = math.prod(shape) // LANE
                g[n] = packed[at:at + k].reshape(shape)
                at += k
        out.append(g)
    return out


def kernel(x, mem, positions, norm_mix_pre, norm_mix_post, w_in, conv_short, conv_gdn, gdn_a_log, gdn_dt_bias, gdn_norm, w_out, norm_mem, norm_xattn_pre, norm_xattn_post, w_xq, w_xkv, w_xo, norm_ffn_pre, norm_ffn_post, w_gate_up, w_down, loss_target, m_norm_mix_pre, m_norm_mix_post, m_w_in, m_conv_short, m_conv_gdn, m_gdn_a_log, m_gdn_dt_bias, m_gdn_norm, m_w_out, m_norm_mem, m_norm_xattn_pre, m_norm_xattn_post, m_w_xq, m_w_xkv, m_w_xo, m_norm_ffn_pre, m_norm_ffn_post, m_w_gate_up, m_w_down, v_norm_mix_pre, v_norm_mix_post, v_w_in, v_conv_short, v_conv_gdn, v_gdn_a_log, v_gdn_dt_bias, v_gdn_norm, v_w_out, v_norm_mem, v_norm_xattn_pre, v_norm_xattn_post, v_w_xq, v_w_xkv, v_w_xo, v_norm_ffn_pre, v_norm_ffn_post, v_w_gate_up, v_w_down):
    args = dict(locals())
    w = {n: args[n] for n in WEIGHTS}
    m = {n: args["m_" + n] for n in WEIGHTS}
    v = {n: args["v_" + n] for n in WEIGHTS}
    seq = x.shape[1]
    chip = 2 * lax.axis_index("x") + lax.axis_index("y")
    core = lax.axis_index("c").astype(jnp.int32).reshape(1)

    sharded = list(MATRICES) + list(SMALL_SHARDED)
    own = [w[n].astype(BF16) if n in MATRICES else w[n] for n in sharded]
    blocks = _gather_xy(own, name="gather_weights")
    blocks = [lax.dynamic_update_index_in_dim(b, o, chip, axis=0) for b, o in zip(blocks, own)]
    full = {n: _from_shards(n, b) for n, b in zip(sharded, blocks)}
    full["w_in"] = _w_in_to_padded(full["w_in"])
    for n in WEIGHTS:
        if n not in full:
            full[n] = w[n]

    loss_row, dx, grads = _local_step(x[0], mem[0], positions.reshape(seq, 1), loss_target[0], full)

    mine = []
    for n in MATRICES:
        stacked = jnp.stack([g[n] for g in grads])
        if n == "w_in":
            stacked = _w_in_from_padded(stacked)
        mine.append(_to_shards(n, stacked))
    theirs = _swap_halves(mine, name="grads_swap_halves")
    pair = [_add_half(a, b, core, name="grads_pair_sum") for a, b in zip(mine, theirs)]
    parts = _scatter_xy(pair, name="grads_scatter")
    chip1 = chip.astype(jnp.int32).reshape(1)
    reduced = [_sum_chips(p, pr, chip1, name="grads_chip_sum") for p, pr in zip(parts, pair)]
    others = _send_to_sibling(reduced, name="grads_share_halves")
    south = lax.axis_index("c") == 0
    grad = {n: jnp.concatenate([jnp.where(south, a, b), jnp.where(south, b, a)], axis=1).reshape(w[n].shape)
            for n, a, b in zip(MATRICES, reduced, others)}

    packed = _pack_small(grads)
    total = _sum_devices(_gather_all(packed, name="small_grads_gather"), name="small_grads_sum")
    small = _unpack_small(total, grads[0])
    for n in WEIGHTS:
        if n in MATRICES:
            continue
        g = jnp.stack([s[n] for s in small])
        if n in ("gdn_a_log", "gdn_dt_bias"):
            g = g[:, 0, :GDN_HEADS]
        elif n in SMALL_SHARDED:
            width = w[n].shape[-1]
            g = lax.dynamic_slice_in_dim(g, chip * width, width, axis=2)
        grad[n] = g.reshape(w[n].shape)

    delta, new_m, new_v = {}, {}, {}
    for n in WEIGHTS:
        shape = w[n].shape
        two_d = (-1, shape[-1])
        d, nm, nv = _adamw(w[n].reshape(two_d), grad[n].reshape(two_d), m[n].reshape(two_d), v[n].reshape(two_d),
                           name="adamw_" + n)
        delta[n], new_m[n], new_v[n] = d.reshape(shape), nm.reshape(shape), nv.reshape(shape)

    loss = lax.psum(loss_row[0, 0], ("x", "y", "c"))
    return (loss, dx.reshape(x.shape), *[grad[n] for n in WEIGHTS], *[delta[n] for n in WEIGHTS],
            *[new_m[n] for n in WEIGHTS], *[new_v[n] for n in WEIGHTS])
```

```python
import functools
import math

import jax
import jax.numpy as jnp
from jax import lax
from jax.experimental import pallas as pl
from jax.experimental.pallas import tpu as pltpu

F32 = jnp.float32
BF16 = jnp.bfloat16
BS = pl.BlockSpec
SDS = jax.ShapeDtypeStruct
PRECISE = lax.Precision.HIGH

D_MODEL = 1024
DEPTH = 4
EPS = 1e-6
ATTN_HEADS = 4
ATTN_HEAD_DIM = 64
ATTN_WIDTH = 256
ROPE_THETA = 500000.0
ROPE_DIM = 16
CONV_WIDTH = 256
CONV_K = 3
GDN_HEADS = 4
GDN_HEAD_DIM = 128
GDN_WIDTH = 512
GDN_CONV_K = 4
GDN_CHUNK = 64
IN_WIDTH = 3592
XATTN_HEADS = 4
XATTN_HEAD_DIM = 256
FFN_HIDDEN = 2816
ADAM_LR = 0.001
ADAM_B1 = 0.9
ADAM_B2 = 0.999
ADAM_EPS = 1e-08
ADAM_WD = 0.01
ADAM_STEP = 10

IN_PAD = 3840
COL_GDN = 1536
COL_GATE = 3072
COL_AB = 3584

VMEM_LIMIT_V7X = 56 * 1024 * 1024
LANE = 128


def _cp(**kw):
    return pltpu.CompilerParams(vmem_limit_bytes=VMEM_LIMIT_V7X, **kw)


def _tile(n, cap):
    if n <= cap:
        return n
    best = None
    for t in range(LANE, cap + 1, LANE):
        if n % t == 0:
            best = t
    assert best is not None, (n, cap)
    return best


def _dot(a, b, ca, cb, precise=False):
    dims = (((ca,), (cb,)), ((), ()))
    if precise:
        return lax.dot_general(a.astype(F32), b.astype(F32), dims, precision=PRECISE,
                               preferred_element_type=F32)
    return lax.dot_general(a.astype(BF16), b.astype(BF16), dims, preferred_element_type=F32)


def _sigmoid(x):
    return 1.0 / (1.0 + jnp.exp(-x))


MM_ROWS = 1024
MM_BLOCK_BYTES = 6 * 1024 * 1024
MM_A_BYTES = 8 * 1024 * 1024


def _mm_tn(width, k, itemsize):
    if k * width * itemsize <= MM_BLOCK_BYTES:
        return width
    return _tile(width, max(LANE, min(1024, MM_BLOCK_BYTES // (k * itemsize) // LANE * LANE)))


def _mm(a, b, *, ta=False, tb=False, out_dtype=F32, b_shards=False, out_shards=False, name):
    m, k = (a.shape[1], a.shape[0]) if ta else a.shape
    tm = _tile(m, MM_ROWS)
    ca = 0 if ta else 1

    if b_shards and tb:
        ns, n, c = b.shape
        assert k == ns * c and not ta
        tn = _tile(n, max(LANE, min(1024, MM_BLOCK_BYTES // (k * b.dtype.itemsize) // LANE * LANE)))

        def body(a_ref, b_ref, o_ref):
            acc = _dot(a_ref[:, :c], b_ref[0], 1, 1)
            for s in range(1, ns):
                acc = acc + _dot(a_ref[:, s * c:(s + 1) * c], b_ref[s], 1, 1)
            o_ref[...] = acc.astype(out_dtype)

        b_spec = BS((ns, tn, c), lambda i, j: (0, j, 0))
    else:
        if b_shards:
            ns, kb, c = b.shape
            n = ns * c
            tn = _mm_tn(c, k, b.dtype.itemsize)
            nb = c // tn
            b_spec = BS((None, k, tn), lambda i, j: (j // nb, 0, j % nb))
        else:
            kb, n = (b.shape[1], b.shape[0]) if tb else b.shape
            c = n // 4 if out_shards else n
            tn = _mm_tn(c, k, b.dtype.itemsize)
            nb = c // tn
            b_spec = BS((tn, k), lambda i, j: (j, 0)) if tb else BS((k, tn), lambda i, j: (0, j))
        assert kb == k
        cb = 1 if tb else 0

        def body(a_ref, b_ref, o_ref):
            o_ref[...] = _dot(a_ref[...], b_ref[...], ca, cb).astype(out_dtype)

    out_bytes = jnp.dtype(out_dtype).itemsize
    while tm > 256 and (tm * k * a.dtype.itemsize > MM_A_BYTES or tm * tn * out_bytes > MM_BLOCK_BYTES):
        tm //= 2
    a_spec = BS((k, tm), lambda i, j: (0, i)) if ta else BS((tm, k), lambda i, j: (i, 0))
    if out_shards:
        out_spec = BS((None, tm, tn), lambda i, j: (j // nb, i, j % nb))
        out_shape = SDS((4, m, n // 4), out_dtype)
    else:
        out_spec = BS((tm, tn), lambda i, j: (i, j))
        out_shape = SDS((m, n), out_dtype)
    return pl.pallas_call(
        body, grid=(m // tm, n // tn), in_specs=[a_spec, b_spec], out_specs=out_spec, out_shape=out_shape,
        compiler_params=_cp(), name=name)(a, b)


def _rmsnorm(x, w, *, name):
    r, d = x.shape
    tr = _tile(r, 512)

    def body(x_ref, w_ref, o_ref):
        xv = x_ref[...]
        rs = lax.rsqrt(jnp.mean(xv * xv, axis=-1, keepdims=True) + EPS)
        o_ref[...] = (xv * rs * w_ref[...]).astype(BF16)

    return pl.pallas_call(
        body, grid=(r // tr,), in_specs=[BS((tr, d), lambda i: (i, 0)), BS((1, d), lambda i: (0, 0))],
        out_specs=BS((tr, d), lambda i: (i, 0)), out_shape=SDS((r, d), BF16),
        compiler_params=_cp(), name=name)(x, w)


def _resnorm(h, m, w, *, name):
    r, d = h.shape
    tr = _tile(r, 512)

    def body(h_ref, m_ref, w_ref, o_ref):
        mv = m_ref[...]
        rs = lax.rsqrt(jnp.mean(mv * mv, axis=-1, keepdims=True) + EPS)
        o_ref[...] = h_ref[...] + mv * rs * w_ref[...]

    row = BS((tr, d), lambda i: (i, 0))
    return pl.pallas_call(
        body, grid=(r // tr,), in_specs=[row, row, BS((1, d), lambda i: (0, 0))],
        out_specs=row, out_shape=SDS((r, d), F32), compiler_params=_cp(), name=name)(h, m, w)


def _rmsnorm_bwd(x, w, dy, res=None, *, name):
    r, d = x.shape
    tr = _tile(r, 512)
    has_res = res is not None

    def body(*refs):
        if has_res:
            x_ref, w_ref, dy_ref, res_ref, dx_ref, dw_ref = refs
        else:
            x_ref, w_ref, dy_ref, dx_ref, dw_ref = refs
        xv = x_ref[...]
        dyv = dy_ref[...].astype(F32)
        rs = lax.rsqrt(jnp.mean(xv * xv, axis=-1, keepdims=True) + EPS)
        nv = xv * rs
        dyw = dyv * w_ref[...]
        dx = rs * (dyw - nv * jnp.mean(dyw * nv, axis=-1, keepdims=True))
        if has_res:
            dx = dx + res_ref[...]
        dx_ref[...] = dx

        @pl.when(pl.program_id(0) == 0)
        def _():
            dw_ref[...] = jnp.zeros_like(dw_ref)

        dw_ref[...] += jnp.sum(dyv * nv, axis=0, keepdims=True)

    row = BS((tr, d), lambda i: (i, 0))
    vec = BS((1, d), lambda i: (0, 0))
    ins = [x, w, dy] + ([res] if has_res else [])
    return pl.pallas_call(
        body, grid=(r // tr,), in_specs=[row, vec, row] + ([row] if has_res else []),
        out_specs=[row, vec], out_shape=[SDS((r, d), F32), SDS((1, d), F32)],
        compiler_params=_cp(), name=name)(*ins)


def _swiglu(gu, *, name):
    r, h2 = gu.shape
    hid = h2 // 2
    tr, tc = _tile(r, 512), _tile(hid, 1408)
    nb = hid // tc

    def body(g_ref, u_ref, o_ref):
        g = g_ref[...]
        o_ref[...] = (g * _sigmoid(g) * u_ref[...]).astype(BF16)

    return pl.pallas_call(
        body, grid=(r // tr, nb),
        in_specs=[BS((tr, tc), lambda i, j: (i, j)), BS((tr, tc), lambda i, j: (i, j + nb))],
        out_specs=BS((tr, tc), lambda i, j: (i, j)), out_shape=SDS((r, hid), BF16),
        compiler_params=_cp(), name=name)(gu, gu)


def _swiglu_bwd(gu, dact, *, name):
    r, h2 = gu.shape
    hid = h2 // 2
    tr, tc = _tile(r, 512), _tile(hid, 1408)
    nb = hid // tc

    def body(g_ref, u_ref, d_ref, o_ref):
        g = g_ref[...]
        da = d_ref[...]
        sg = _sigmoid(g)
        dgate = da * u_ref[...] * sg * (1.0 + g * (1.0 - sg))
        dup = da * g * sg
        o_ref[...] = jnp.where(pl.program_id(1) < nb, dgate, dup).astype(BF16)

    return pl.pallas_call(
        body, grid=(r // tr, 2 * nb),
        in_specs=[BS((tr, tc), lambda i, j: (i, j % nb)), BS((tr, tc), lambda i, j: (i, j % nb + nb)),
                  BS((tr, tc), lambda i, j: (i, j % nb))],
        out_specs=BS((tr, tc), lambda i, j: (i, j)), out_shape=SDS((r, h2), BF16),
        compiler_params=_cp(), name=name)(gu, gu, dact)


def _loss_grad(h, target, *, name):
    r, d = h.shape
    tr = _tile(r, 512)

    def body(h_ref, t_ref, l_ref, g_ref):
        e = h_ref[...] - t_ref[...]
        g_ref[...] = e * (1.0 / d)

        @pl.when(pl.program_id(0) == 0)
        def _():
            l_ref[...] = jnp.zeros_like(l_ref)

        l_ref[...] += jnp.full((1, LANE), 0.5 / d, F32) * jnp.sum(e * e)

    row = BS((tr, d), lambda i: (i, 0))
    return pl.pallas_call(
        body, grid=(r // tr,), in_specs=[row, row],
        out_specs=[BS((1, LANE), lambda i: (0, 0)), row],
        out_shape=[SDS((1, LANE), F32), SDS((r, d), F32)], compiler_params=_cp(), name=name)(h, target)


def _adamw(w, g, m, v, *, name):
    r, c = w.shape
    tr = r if r <= 512 else _rows_tile(r)
    bc1 = 1.0 - ADAM_B1 ** ADAM_STEP
    bc2 = 1.0 - ADAM_B2 ** ADAM_STEP

    def body(w_ref, g_ref, m_ref, v_ref, d_ref, nm_ref, nv_ref):
        gv = g_ref[...]
        nm = ADAM_B1 * m_ref[...] + (1.0 - ADAM_B1) * gv
        nv = ADAM_B2 * v_ref[...] + (1.0 - ADAM_B2) * (gv * gv)
        d_ref[...] = -ADAM_LR * ((nm / bc1) / (jnp.sqrt(nv / bc2) + ADAM_EPS) + ADAM_WD * w_ref[...])
        nm_ref[...] = nm
        nv_ref[...] = nv

    blk = BS((tr, c), lambda i: (i, 0))
    return pl.pallas_call(
        body, grid=(r // tr,), in_specs=[blk] * 4, out_specs=[blk] * 3,
        out_shape=[SDS((r, c), F32)] * 3, compiler_params=_cp(), name=name)(w, g, m, v)


def _rows_tile(r):
    for t in (512, 256, 128, 64, 32, 16, 8):
        if r % t == 0:
            return t
    return r


def _rope_tables(pos, *, name):
    s = pos.shape[0]
    half = ROPE_DIM // 2

    def body(p_ref, c_ref, a_ref, b_ref):
        lane = lax.broadcasted_iota(jnp.int32, (s, ATTN_WIDTH), 1) & (ATTN_HEAD_DIM - 1)
        fi = (lane & (half - 1)).astype(F32)
        inv_freq = jnp.exp(fi * (-2.0 * math.log(ROPE_THETA) / ROPE_DIM))
        ang = p_ref[...].astype(F32) * inv_freq
        cs, sn = jnp.cos(ang), jnp.sin(ang)
        c_ref[...] = jnp.where(lane < ROPE_DIM, cs, 1.0)
        a_ref[...] = jnp.where(lane < half, -sn, 0.0)
        b_ref[...] = jnp.where((lane >= half) & (lane < ROPE_DIM), sn, 0.0)

    full = BS((s, ATTN_WIDTH), lambda i: (0, 0))
    return pl.pallas_call(
        body, grid=(1,), in_specs=[BS((s, 1), lambda i: (0, 0))], out_specs=[full] * 3,
        out_shape=[SDS((s, ATTN_WIDTH), F32)] * 3, compiler_params=_cp(), name=name)(pos)


def _rot(x, c, a, b):
    w = x.shape[1]
    return x * c + pltpu.roll(x, w - ROPE_DIM // 2, 1) * a + pltpu.roll(x, ROPE_DIM // 2, 1) * b


def _rot_t(dy, c, a, b):
    w = dy.shape[1]
    return dy * c + pltpu.roll(dy * a, ROPE_DIM // 2, 1) + pltpu.roll(dy * b, w - ROPE_DIM // 2, 1)


def _attn_count(q0, tq, s):
    dist = (lax.broadcasted_iota(jnp.int32, (tq, s), 0) + q0) - lax.broadcasted_iota(jnp.int32, (tq, s), 1)
    cnt = ((dist <= 128).astype(F32) + (((dist & 3) == 0) & (dist <= 512)).astype(F32)
           + ((dist & 15) == 0).astype(F32))
    return jnp.where(dist >= 0, cnt, 0.0)


def _attn_probs(qm, k, cnt):
    sc = _dot(qm, k, 1, 1)
    mx = jnp.max(jnp.where(cnt > 0.0, sc, -1e30), axis=-1, keepdims=True)
    p = cnt * jnp.exp(jnp.minimum(sc - mx, 0.0))
    return p / jnp.sum(p, axis=-1, keepdims=True)


ATTN_TQ = 256


def _attn_specs(s, tq):
    def qblk(col):
        return BS((tq, ATTN_WIDTH), lambda i: (i, col))

    def full(col):
        return BS((s, ATTN_WIDTH), lambda i: (0, col))

    return qblk, full


def _attn_fwd(proj, tabs, *, name):
    s = proj.shape[0]
    tq = ATTN_TQ
    qblk, full = _attn_specs(s, tq)
    scale = ATTN_HEAD_DIM ** -0.5

    def body(q_ref, k_ref, v_ref, cq, aq, bq, ck, ak, bk, o_ref):
        q = _rot(q_ref[...], cq[...], aq[...], bq[...]) * scale
        head = lax.broadcasted_iota(jnp.int32, (1, ATTN_WIDTH), 1) >> 6

        def block(i):
            n = (i + 1) * tq
            k = _rot(k_ref[:n, :], ck[:n, :], ak[:n, :], bk[:n, :]).astype(BF16)
            v = v_ref[:n, :].astype(BF16)
            cnt = _attn_count(i * tq, tq, n)
            acc = jnp.zeros((tq, ATTN_WIDTH), F32)
            for h in range(ATTN_HEADS):
                hm = (head == h).astype(F32)
                p = _attn_probs(q * hm, k, cnt)
                acc = acc + _dot(p, v, 1, 0) * hm
            o_ref[...] = acc.astype(BF16)

        for i in range(s // tq):
            pl.when(pl.program_id(0) == i)(functools.partial(block, i))

    return pl.pallas_call(
        body, grid=(s // tq,),
        in_specs=[qblk(0), full(1), full(2), qblk(0), qblk(0), qblk(0), full(0), full(0), full(0)],
        out_specs=BS((tq, ATTN_WIDTH), lambda i: (i, 0)), out_shape=SDS((s, ATTN_WIDTH), BF16),
        compiler_params=_cp(), name=name)(proj, proj, proj, *tabs, *tabs)


def _attn_bwd(proj, tabs, dcat, *, name):
    s = proj.shape[0]
    tq = ATTN_TQ
    nq = s // tq
    qblk, full = _attn_specs(s, tq)
    scale = ATTN_HEAD_DIM ** -0.5

    def body(q_ref, k_ref, v_ref, cq, aq, bq, ck, ak, bk, dy_ref, dq_ref, dk_ref, dv_ref, dk_acc, dv_acc):
        i = pl.program_id(0)

        @pl.when(i == 0)
        def _():
            dk_acc[...] = jnp.zeros_like(dk_acc)
            dv_acc[...] = jnp.zeros_like(dv_acc)

        q = _rot(q_ref[...], cq[...], aq[...], bq[...]) * scale
        k = _rot(k_ref[...], ck[...], ak[...], bk[...]).astype(BF16)
        v = v_ref[...].astype(BF16)
        dy = dy_ref[...].astype(F32)
        cnt = _attn_count(i * tq, tq, s)
        head = lax.broadcasted_iota(jnp.int32, (1, ATTN_WIDTH), 1) >> 6
        dq = jnp.zeros((tq, ATTN_WIDTH), F32)
        for h in range(ATTN_HEADS):
            hm = (head == h).astype(F32)
            qm = q * hm
            dym = dy * hm
            p = _attn_probs(qm, k, cnt)
            dp = _dot(dym, v, 1, 1)
            ds = p * (dp - jnp.sum(p * dp, axis=-1, keepdims=True))
            dq = dq + _dot(ds, k, 1, 0) * hm
            dk_acc[...] += _dot(ds, qm, 0, 0)
            dv_acc[...] += _dot(p, dym, 0, 0)
        dq_ref[...] = _rot_t(dq * scale, cq[...], aq[...], bq[...])

        @pl.when(i == nq - 1)
        def _():
            dk_ref[...] = _rot_t(dk_acc[...], ck[...], ak[...], bk[...])
            dv_ref[...] = dv_acc[...]

    whole = BS((s, ATTN_WIDTH), lambda i: (0, 0))
    dq, dk, dv = pl.pallas_call(
        body, grid=(nq,),
        in_specs=[qblk(0), full(1), full(2), qblk(0), qblk(0), qblk(0), full(0), full(0), full(0), qblk(0)],
        out_specs=[BS((tq, ATTN_WIDTH), lambda i: (i, 0)), whole, whole],
        out_shape=[SDS((s, ATTN_WIDTH), F32)] * 3,
        scratch_shapes=[pltpu.VMEM((s, ATTN_WIDTH), F32), pltpu.VMEM((s, ATTN_WIDTH), F32)],
        compiler_params=_cp(), name=name)(proj, proj, proj, *tabs, *tabs, dcat)
    return dq, dk, dv


def _shift_down(x, n):
    if n == 0:
        return x
    rows = lax.broadcasted_iota(jnp.int32, x.shape, 0)
    return jnp.where(rows >= n, pltpu.roll(x, n, 0), 0.0)


def _shift_up(x, n):
    if n == 0:
        return x
    t = x.shape[0]
    rows = lax.broadcasted_iota(jnp.int32, x.shape, 0)
    return jnp.where(rows < t - n, pltpu.roll(x, t - n, 0), 0.0)


def _conv_fwd(z, w, kk):
    y = z * w[kk - 1:kk, :]
    for j in range(kk - 1):
        y = y + _shift_down(z, kk - 1 - j) * w[j:j + 1, :]
    return y


def _conv_bwd(z, dy, w, kk):
    dz = dy * w[kk - 1:kk, :]
    dws = []
    for j in range(kk - 1):
        dz = dz + _shift_up(dy, kk - 1 - j) * w[j:j + 1, :]
        dws.append(jnp.sum(dy * _shift_down(z, kk - 1 - j), axis=0, keepdims=True))
    dws.append(jnp.sum(dy * z, axis=0, keepdims=True))
    return dz, jnp.concatenate(dws, axis=0)


def _sconv_fwd(proj, w, *, name):
    s = proj.shape[0]

    def body(b_ref, c_ref, x_ref, w_ref, o_ref):
        y = _conv_fwd(c_ref[...] * x_ref[...], w_ref[...], CONV_K)
        o_ref[...] = (b_ref[...] * y).astype(BF16)

    def col(j):
        return BS((s, LANE), lambda i: (0, j + i))

    return pl.pallas_call(
        body, grid=(CONV_WIDTH // LANE,), in_specs=[col(6), col(8), col(10), BS((CONV_K, LANE), lambda i: (0, i))],
        out_specs=BS((s, LANE), lambda i: (0, i)), out_shape=SDS((s, CONV_WIDTH), BF16),
        compiler_params=_cp(), name=name)(proj, proj, proj, w)


def _sconv_bwd(proj, w, dcat, *, name):
    s = proj.shape[0]

    def body(b_ref, c_ref, x_ref, w_ref, dy_ref, db_ref, dc_ref, dx_ref, dw_ref):
        cv, xv, wv = c_ref[...], x_ref[...], w_ref[...]
        dy = dy_ref[...].astype(F32)
        z = cv * xv
        db_ref[...] = dy * _conv_fwd(z, wv, CONV_K)
        dz, dw = _conv_bwd(z, dy * b_ref[...], wv, CONV_K)
        dc_ref[...] = dz * xv
        dx_ref[...] = dz * cv
        dw_ref[...] = dw

    def col(j):
        return BS((s, LANE), lambda i: (0, j + i))

    out = BS((s, LANE), lambda i: (0, i))
    wspec = BS((CONV_K, LANE), lambda i: (0, i))
    return pl.pallas_call(
        body, grid=(CONV_WIDTH // LANE,), in_specs=[col(6), col(8), col(10), wspec, col(2)],
        out_specs=[out, out, out, wspec],
        out_shape=[SDS((s, CONV_WIDTH), F32)] * 3 + [SDS((CONV_K, CONV_WIDTH), F32)],
        compiler_params=_cp(), name=name)(proj, proj, proj, w, dcat)


def _l2n(y, scale):
    r = lax.rsqrt(jnp.sum(y * y, axis=-1, keepdims=True) + EPS)
    return y * r * scale, r


def _gdn_pre_fwd(proj, w, *, name):
    s = proj.shape[0]
    nh = GDN_HEADS

    def body(x_ref, w_ref, o_ref):
        j = pl.program_id(0)
        c = _conv_fwd(x_ref[...], w_ref[...], GDN_CONV_K)
        y = c * _sigmoid(c)
        scale = jnp.where(j < nh, GDN_HEAD_DIM ** -0.5, 1.0)
        n, _ = _l2n(y, scale)
        o_ref[...] = jnp.where(j < 2 * nh, n, y)

    return pl.pallas_call(
        body, grid=(3 * nh,),
        in_specs=[BS((s, LANE), lambda j: (0, COL_GDN // LANE + j)), BS((GDN_CONV_K, LANE), lambda j: (0, j))],
        out_specs=BS((s, LANE), lambda j: (0, j)), out_shape=SDS((s, 3 * GDN_WIDTH), F32),
        compiler_params=_cp(), name=name)(proj, w)


def _gdn_pre_bwd(proj, w, dqkv, *, name):
    s = proj.shape[0]
    nh = GDN_HEADS

    def body(x_ref, w_ref, d_ref, dx_ref, dw_ref):
        j = pl.program_id(0)
        xv, wv, dn = x_ref[...], w_ref[...], d_ref[...]
        c = _conv_fwd(xv, wv, GDN_CONV_K)
        sg = _sigmoid(c)
        y = c * sg
        scale = jnp.where(j < nh, GDN_HEAD_DIM ** -0.5, 1.0)
        n, r = _l2n(y, 1.0)
        dns = dn * scale
        dy_norm = r * (dns - n * jnp.sum(dns * n, axis=-1, keepdims=True))
        dy = jnp.where(j < 2 * nh, dy_norm, dn)
        dc = dy * sg * (1.0 + c * (1.0 - sg))
        dx, dw = _conv_bwd(xv, dc, wv, GDN_CONV_K)
        dx_ref[...] = dx
        dw_ref[...] = dw

    wspec = BS((GDN_CONV_K, LANE), lambda j: (0, j))
    blk = BS((s, LANE), lambda j: (0, j))
    return pl.pallas_call(
        body, grid=(3 * nh,),
        in_specs=[BS((s, LANE), lambda j: (0, COL_GDN // LANE + j)), wspec, blk],
        out_specs=[blk, wspec], out_shape=[SDS((s, 3 * GDN_WIDTH), F32), SDS((GDN_CONV_K, 3 * GDN_WIDTH), F32)],
        compiler_params=_cp(), name=name)(proj, w, dqkv)


def _softplus(x):
    return jnp.maximum(x, 0.0) + jnp.log(1.0 + jnp.exp(-jnp.abs(x)))


def _gdn_gates_fwd(proj, a_log, dt_bias, *, name):
    s = proj.shape[0]

    def body(x_ref, al_ref, dt_ref, o_ref):
        xv = x_ref[...]
        lane = lax.broadcasted_iota(jnp.int32, xv.shape, 1)
        g = -jnp.exp(al_ref[...]) * _softplus(xv + dt_ref[...])
        o_ref[...] = jnp.where(lane < GDN_HEADS, g, jnp.where(lane < 2 * GDN_HEADS, _sigmoid(xv), 0.0))

    vec = BS((1, LANE), lambda i: (0, 0))
    return pl.pallas_call(
        body, grid=(1,), in_specs=[BS((s, LANE), lambda i: (0, COL_AB // LANE)), vec, vec],
        out_specs=BS((s, LANE), lambda i: (0, 0)), out_shape=SDS((s, LANE), F32),
        compiler_params=_cp(), name=name)(proj, a_log, dt_bias)


def _gdn_gates_bwd(proj, a_log, dt_bias, dgb, *, name):
    s = proj.shape[0]

    def body(x_ref, al_ref, dt_ref, d_ref, dx_ref, dal_ref, ddt_ref):
        xv, dv = x_ref[...], d_ref[...]
        lane = lax.broadcasted_iota(jnp.int32, xv.shape, 1)
        is_g = lane < GDN_HEADS
        ea = -jnp.exp(al_ref[...])
        z = xv + dt_ref[...]
        da = jnp.where(is_g, dv * ea * _sigmoid(z), 0.0)
        beta = _sigmoid(xv)
        dx_ref[...] = jnp.where(is_g, da, jnp.where(lane < 2 * GDN_HEADS, dv * beta * (1.0 - beta), 0.0))
        dal_ref[...] = jnp.sum(jnp.where(is_g, dv * ea * _softplus(z), 0.0), axis=0, keepdims=True)
        ddt_ref[...] = jnp.sum(da, axis=0, keepdims=True)

    vec = BS((1, LANE), lambda i: (0, 0))
    blk = BS((s, LANE), lambda i: (0, 0))
    return pl.pallas_call(
        body, grid=(1,), in_specs=[BS((s, LANE), lambda i: (0, COL_AB // LANE)), vec, vec, blk],
        out_specs=[blk, vec, vec], out_shape=[SDS((s, LANE), F32), SDS((1, LANE), F32), SDS((1, LANE), F32)],
        compiler_params=_cp(), name=name)(proj, a_log, dt_bias, dgb)


def _col_to_row(col, eye):
    return jnp.sum(jnp.where(eye, col, 0.0), axis=0, keepdims=True)


def _row_to_col(row, eye):
    return jnp.sum(jnp.where(eye, row, 0.0), axis=1, keepdims=True)


GDN_GROUP = 4
TRI_BLOCK_SHIFT = 4


def _gdn_masks(c):
    row = lax.broadcasted_iota(jnp.int32, (c, c), 0)
    col = lax.broadcasted_iota(jnp.int32, (c, c), 1)
    return dict(row=row, col=col, eye=row == col, low=row >= col, strict=row > col, upper=row <= col,
                on_diag=(row >> TRI_BLOCK_SHIFT) == (col >> TRI_BLOCK_SHIFT))


def _tri_inv(a_list, mk):
    eye_f = mk["eye"].astype(F32)
    ds = [jnp.where(mk["on_diag"], a, 0.0) for a in a_list]
    xs = [eye_f - d for d in ds]
    ps = ds
    for _ in range(3):
        ps = [_dot(p, p, 1, 0, precise=True) for p in ps]
        xs = [x + _dot(x, p, 1, 0, precise=True) for x, p in zip(xs, ps)]
    ms = [_dot(x, a - d, 1, 0, precise=True) for x, a, d in zip(xs, a_list, ds)]
    m2s = [_dot(m, m, 1, 0, precise=True) for m in ms]
    ys = [eye_f - m for m in ms]
    ys = [y + _dot(y, m2, 1, 0, precise=True) for y, m2 in zip(ys, m2s)]
    return [_dot(y, x, 1, 0, precise=True) for y, x in zip(ys, xs)]


def _gdn_pre(qs, ks, vs, gs, betas, mk, ts=None):
    c, hd = qs[0].shape
    eye, low = mk["eye"], mk["low"]
    g_rows = [_col_to_row(g, eye) for g in gs]
    d_cols = [jnp.sum(jnp.where(low, gr, 0.0), axis=1, keepdims=True) for gr in g_rows]
    d_rows = [jnp.sum(jnp.where(mk["upper"], g, 0.0), axis=0, keepdims=True) for g in gs]
    rels = [jnp.where(low, jnp.exp(jnp.minimum(dc - dr, 0.0)), 0.0) for dc, dr in zip(d_cols, d_rows)]
    d_lasts = [dc[c - 1:c, :] for dc in d_cols]
    es = [jnp.exp(dc) for dc in d_cols]
    fs = [jnp.exp(dl - dc) for dl, dc in zip(d_lasts, d_cols)]
    cds = [jnp.exp(dl) for dl in d_lasts]
    kbs = [k * b for k, b in zip(ks, betas)]
    kbqs = [jnp.concatenate([kb, q], axis=0) for kb, q in zip(kbs, qs)]
    kqk = [_dot(kbq, k, 1, 1) for kbq, k in zip(kbqs, ks)]
    kks = [x[:c, :] for x in kqk]
    qks = [x[c:, :] for x in kqk]
    if ts is None:
        ts = _tri_inv([jnp.where(mk["strict"], kk * rel, 0.0) for kk, rel in zip(kks, rels)], mk)
    vbs = [v * b for v, b in zip(vs, betas)]
    kbes = [kb * e for kb, e in zip(kbs, es)]
    uws = [_dot(t, jnp.concatenate([vb, kbe], axis=1), 1, 0) for t, vb, kbe in zip(ts, vbs, kbes)]
    out = []
    for i in range(len(qs)):
        out.append(dict(rel=rels[i], e=es[i], f=fs[i], cd=cds[i], kb=kbs[i], kbq=kbqs[i], kk=kks[i], qk=qks[i],
                        t=ts[i], u=uws[i][:, :hd], w=uws[i][:, hd:], uw=uws[i], attn=qks[i] * rels[i],
                        qd=qs[i] * es[i], kd=ks[i] * fs[i]))
    return out


def _gdn_apply(pres, sts, leaving=True):
    c = pres[0]["u"].shape[0]
    wqs = [_dot(jnp.concatenate([p["w"], p["qd"]], axis=0), st, 1, 0) for p, st in zip(pres, sts)]
    vns = [p["u"] - x[:c, :] for p, x in zip(pres, wqs)]
    os_ = [x[c:, :] + _dot(p["attn"], vn, 1, 0) for p, x, vn in zip(pres, wqs, vns)]
    if not leaving:
        return vns, os_, None
    new = [p["cd"] * st + _dot(p["kd"], vn, 0, 0) for p, st, vn in zip(pres, sts, vns)]
    return vns, os_, new


def _gdn_bwd_rest(qs, ks, vs, betas, sts, pres, vns, dos, dvns, dsts, mk):
    c, hd = qs[0].shape
    eye = mk["eye"]
    n = range(len(qs))
    dkds = [_dot(vns[i], dsts[i], 1, 1) for i in n]
    dcds = [jnp.sum(sts[i] * dsts[i]) for i in n]
    dattns = [jnp.where(mk["low"], _dot(dos[i], vns[i], 1, 1), 0.0) for i in n]
    dqdws = [_dot(jnp.concatenate([dos[i], -dvns[i]], axis=0), sts[i], 1, 1) for i in n]
    dqds = [x[:c, :] for x in dqdws]
    dws = [x[c:, :] for x in dqdws]
    dvks = [_dot(pres[i]["t"], jnp.concatenate([dvns[i], dws[i]], axis=1), 0, 0) for i in n]
    das = [jnp.where(mk["strict"], -_dot(dvks[i], pres[i]["uw"], 1, 1), 0.0) for i in n]
    dkqs = [jnp.concatenate([das[i] * pres[i]["rel"], dattns[i] * pres[i]["rel"]], axis=0) for i in n]
    dkbdq = [_dot(dkqs[i], ks[i], 1, 0) for i in n]
    dk0 = [_dot(dkqs[i], pres[i]["kbq"], 0, 0) for i in n]
    out = []
    rows1 = lax.broadcasted_iota(jnp.int32, (c, 1), 0)
    for i in n:
        p = pres[i]
        dvb, dkbe = dvks[i][:, :hd], dvks[i][:, hd:]
        grel = (das[i] * p["kk"] + dattns[i] * p["qk"]) * p["rel"]
        dkb = dkbdq[i][:c, :] + dkbe * p["e"]
        dk = dk0[i] + dkds[i] * p["f"] + dkb * betas[i]
        dq = dkbdq[i][c:, :] + dqds[i] * p["e"]
        dv = dvb * betas[i]
        dbeta = jnp.sum(dkb * ks[i], axis=1, keepdims=True) + jnp.sum(dvb * vs[i], axis=1, keepdims=True)
        de = jnp.sum(dqds[i] * qs[i], axis=1, keepdims=True) + jnp.sum(dkbe * p["kb"], axis=1, keepdims=True)
        dff = jnp.sum(dkds[i] * ks[i], axis=1, keepdims=True) * p["f"]
        dd = (de * p["e"] - dff + jnp.sum(grel, axis=1, keepdims=True)
              - _row_to_col(jnp.sum(grel, axis=0, keepdims=True), eye))
        dd = dd + jnp.where(rows1 == c - 1, jnp.sum(dff) + dcds[i] * p["cd"], 0.0)
        dg = jnp.sum(jnp.where(mk["upper"], _col_to_row(dd, eye), 0.0), axis=1, keepdims=True)
        out.append((dq, dk, dv, dg, dbeta))
    return out


def _gdn_specs(c):
    def qkv(j):
        return BS((c, GDN_WIDTH), lambda n: (n, j))

    return qkv


def _gdn_core_fwd(qkv, gbeta, proj, norm_w, *, name):
    s = qkv.shape[0]
    c, nh, hd, grp = GDN_CHUNK, GDN_HEADS, GDN_HEAD_DIM, GDN_GROUP
    n_chunks = s // c
    blk = _gdn_specs(grp * c)
    inst = [(sub, h) for sub in range(grp) for h in range(nh)]

    def body(q_ref, k_ref, v_ref, gb_ref, gate_ref, nw_ref, y_ref, st_ref, t_ref, state):
        @pl.when(pl.program_id(0) == 0)
        def _():
            state[...] = jnp.zeros_like(state)

        mk = _gdn_masks(c)
        rows = [slice(sub * c, (sub + 1) * c) for sub in range(grp)]
        lanes = [slice(h * hd, (h + 1) * hd) for h in range(nh)]
        gbs = [gb_ref[r, :] for r in rows]
        pres = _gdn_pre([q_ref[rows[sub], lanes[h]] for sub, h in inst], [k_ref[rows[sub], lanes[h]] for sub, h in inst],
                        [v_ref[rows[sub], lanes[h]] for sub, h in inst], [gbs[sub][:, h:h + 1] for sub, h in inst],
                        [gbs[sub][:, nh + h:nh + h + 1] for sub, h in inst], mk)
        sts = [state[ls, :] for ls in lanes]
        outs = []
        for sub in range(grp):
            for h in range(nh):
                st_ref[pl.ds((sub * nh + h) * hd, hd), :] = sts[h]
            _, os_, sts = _gdn_apply(pres[sub * nh:(sub + 1) * nh], sts)
            outs += os_
        for h in range(nh):
            state[lanes[h], :] = sts[h]
        nw = nw_ref[...]
        for i, (sub, h) in enumerate(inst):
            t_ref[pl.ds(i * c, c), :] = pres[i]["t"]
            o = outs[i]
            gate = gate_ref[rows[sub], lanes[h]]
            rs = lax.rsqrt(jnp.mean(o * o, axis=-1, keepdims=True) + EPS)
            y_ref[rows[sub], lanes[h]] = (o * rs * nw * (gate * _sigmoid(gate))).astype(BF16)

    return pl.pallas_call(
        body, grid=(n_chunks // grp,),
        in_specs=[blk(0), blk(1), blk(2), BS((grp * c, LANE), lambda n: (n, 0)),
                  BS((grp * c, GDN_WIDTH), lambda n: (n, COL_GATE // GDN_WIDTH)), BS((1, hd), lambda n: (0, 0))],
        out_specs=[BS((grp * c, GDN_WIDTH), lambda n: (n, 0)), BS((grp * nh * hd, hd), lambda n: (n, 0)),
                   BS((grp * nh * c, c), lambda n: (n, 0))],
        out_shape=[SDS((s, GDN_WIDTH), BF16), SDS((n_chunks * nh * hd, hd), F32), SDS((n_chunks * nh * c, c), F32)],
        scratch_shapes=[pltpu.VMEM((nh * hd, hd), F32)],
        compiler_params=_cp(), name=name)(qkv, qkv, qkv, gbeta, proj, norm_w)


def _gdn_core_bwd(qkv, gbeta, proj, norm_w, states, tinv, dcat, *, name):
    s = qkv.shape[0]
    c, nh, hd, grp = GDN_CHUNK, GDN_HEADS, GDN_HEAD_DIM, GDN_GROUP
    n_chunks = s // c
    last = n_chunks // grp - 1
    inst = [(sub, h) for sub in range(grp) for h in range(nh)]

    def rev(j, w):
        return BS((grp * c, w), lambda n: (last - n, j))

    def body(q_ref, k_ref, v_ref, gb_ref, gate_ref, nw_ref, st_ref, t_ref, dy_ref,
             dqkv_ref, dgb_ref, dgate_ref, dnw_ref, dstate):
        @pl.when(pl.program_id(0) == 0)
        def _():
            dstate[...] = jnp.zeros_like(dstate)
            dnw_ref[...] = jnp.zeros_like(dnw_ref)

        mk = _gdn_masks(c)
        rows = [slice(sub * c, (sub + 1) * c) for sub in range(grp)]
        lanes = [slice(h * hd, (h + 1) * hd) for h in range(nh)]
        gbs = [gb_ref[r, :] for r in rows]
        qs = [q_ref[rows[sub], lanes[h]] for sub, h in inst]
        ks = [k_ref[rows[sub], lanes[h]] for sub, h in inst]
        vs = [v_ref[rows[sub], lanes[h]] for sub, h in inst]
        betas = [gbs[sub][:, nh + h:nh + h + 1] for sub, h in inst]
        sts = [st_ref[pl.ds(i * hd, hd), :] for i in range(len(inst))]
        pres = _gdn_pre(qs, ks, vs, [gbs[sub][:, h:h + 1] for sub, h in inst], betas, mk,
                        ts=[t_ref[pl.ds(i * c, c), :] for i in range(len(inst))])
        vns, outs, _ = _gdn_apply(pres, sts, leaving=False)

        nw = nw_ref[...]
        dnw = jnp.zeros((1, hd), F32)
        dos = []
        for i, (sub, h) in enumerate(inst):
            o = outs[i]
            gate = gate_ref[rows[sub], lanes[h]]
            dy = dy_ref[rows[sub], lanes[h]].astype(F32)
            sg = _sigmoid(gate)
            rs = lax.rsqrt(jnp.mean(o * o, axis=-1, keepdims=True) + EPS)
            nrm = o * rs
            dgate_ref[rows[sub], lanes[h]] = dy * nrm * nw * sg * (1.0 + gate * (1.0 - sg))
            dnv = dy * (gate * sg)
            dnw = dnw + jnp.sum(dnv * nrm, axis=0, keepdims=True)
            dno = dnv * nw
            dos.append(rs * (dno - nrm * jnp.mean(dno * nrm, axis=-1, keepdims=True)))
        dnw_ref[...] += dnw

        from_o = [_dot(p["attn"], do, 0, 0) for p, do in zip(pres, dos)]
        to_st = [_dot(p["qd"], do, 0, 0) for p, do in zip(pres, dos)]
        dst = [dstate[ls, :] for ls in lanes]
        dsts = [None] * len(inst)
        dvns = [None] * len(inst)
        for sub in reversed(range(grp)):
            idx = [sub * nh + h for h in range(nh)]
            for h, i in enumerate(idx):
                dsts[i] = dst[h]
                dvns[i] = from_o[i] + _dot(pres[i]["kd"], dst[h], 1, 0)
            dst = [pres[i]["cd"] * dst[h] + to_st[i] - _dot(pres[i]["w"], dvns[i], 0, 0) for h, i in enumerate(idx)]
        for h in range(nh):
            dstate[lanes[h], :] = dst[h]

        grads = _gdn_bwd_rest(qs, ks, vs, betas, sts, pres, vns, dos, dvns, dsts, mk)
        lane = lax.broadcasted_iota(jnp.int32, (c, LANE), 1)
        dgb = [jnp.zeros((c, LANE), F32) for _ in range(grp)]
        for (sub, h), (dq, dk, dv, dg, dbeta) in zip(inst, grads):
            dqkv_ref[rows[sub], lanes[h]] = dq
            dqkv_ref[rows[sub], slice(GDN_WIDTH + h * hd, GDN_WIDTH + (h + 1) * hd)] = dk
            dqkv_ref[rows[sub], slice(2 * GDN_WIDTH + h * hd, 2 * GDN_WIDTH + (h + 1) * hd)] = dv
            dgb[sub] = jnp.where(lane == h, dg, jnp.where(lane == nh + h, dbeta, dgb[sub]))
        for sub in range(grp):
            dgb_ref[rows[sub], :] = dgb[sub]

    return pl.pallas_call(
        body, grid=(n_chunks // grp,),
        in_specs=[rev(0, GDN_WIDTH), rev(1, GDN_WIDTH), rev(2, GDN_WIDTH), rev(0, LANE),
                  rev(COL_GATE // GDN_WIDTH, GDN_WIDTH), BS((1, hd), lambda n: (0, 0)),
                  BS((grp * nh * hd, hd), lambda n: (last - n, 0)), BS((grp * nh * c, c), lambda n: (last - n, 0)),
                  rev(1, GDN_WIDTH)],
        out_specs=[rev(0, 3 * GDN_WIDTH), rev(0, LANE), rev(0, GDN_WIDTH), BS((1, hd), lambda n: (0, 0))],
        out_shape=[SDS((s, 3 * GDN_WIDTH), F32), SDS((s, LANE), F32), SDS((s, GDN_WIDTH), F32), SDS((1, hd), F32)],
        scratch_shapes=[pltpu.VMEM((nh * hd, hd), F32)],
        compiler_params=_cp(), name=name)(qkv, qkv, qkv, gbeta, proj, norm_w, states, tinv, dcat)


XATTN_TQ = 512


def _xattn_probs(qh, kh):
    sc = _dot(qh, kh, 1, 1) * (XATTN_HEAD_DIM ** -0.5)
    p = jnp.exp(sc - jnp.max(sc, axis=-1, keepdims=True))
    return p / jnp.sum(p, axis=-1, keepdims=True)


def _xattn_fwd(q, kv, *, name):
    s, d = q.shape
    m = kv.shape[0]
    tq, hd = _tile(s, XATTN_TQ), XATTN_HEAD_DIM

    def body(q_ref, k_ref, v_ref, o_ref):
        for h in range(XATTN_HEADS):
            ls = slice(h * hd, (h + 1) * hd)
            p = _xattn_probs(q_ref[:, ls], k_ref[:, ls])
            o_ref[:, ls] = _dot(p, v_ref[:, ls], 1, 0).astype(BF16)

    return pl.pallas_call(
        body, grid=(s // tq,),
        in_specs=[BS((tq, d), lambda i: (i, 0)), BS((m, d), lambda i: (0, 0)), BS((m, d), lambda i: (0, 1))],
        out_specs=BS((tq, d), lambda i: (i, 0)), out_shape=SDS((s, d), BF16),
        compiler_params=_cp(), name=name)(q, kv, kv)


def _xattn_bwd(q, kv, do, *, name):
    s, d = q.shape
    m = kv.shape[0]
    tq, hd = _tile(s, XATTN_TQ), XATTN_HEAD_DIM
    scale = hd ** -0.5

    def body(q_ref, k_ref, v_ref, do_ref, dq_ref, dkv_ref):
        @pl.when(pl.program_id(0) == 0)
        def _():
            dkv_ref[...] = jnp.zeros_like(dkv_ref)

        for h in range(XATTN_HEADS):
            ls = slice(h * hd, (h + 1) * hd)
            vs = slice(d + h * hd, d + (h + 1) * hd)
            qh, kh, doh = q_ref[:, ls], k_ref[:, ls], do_ref[:, ls]
            p = _xattn_probs(qh, kh)
            dp = _dot(doh, v_ref[:, ls], 1, 1)
            ds = p * (dp - jnp.sum(p * dp, axis=-1, keepdims=True)) * scale
            dq_ref[:, ls] = _dot(ds, kh, 1, 0).astype(BF16)
            dkv_ref[:, ls] += _dot(ds, qh, 0, 0)
            dkv_ref[:, vs] += _dot(p, doh, 0, 0)

    row = BS((tq, d), lambda i: (i, 0))
    return pl.pallas_call(
        body, grid=(s // tq,),
        in_specs=[row, BS((m, d), lambda i: (0, 0)), BS((m, d), lambda i: (0, 1)), row],
        out_specs=[row, BS((m, 2 * d), lambda i: (0, 0))],
        out_shape=[SDS((s, d), BF16), SDS((m, 2 * d), F32)],
        compiler_params=_cp(), name=name)(q, kv, kv, do)


def _pad_lanes(vec4):
    return jnp.zeros((1, LANE), F32).at[0, :GDN_HEADS].set(vec4)


def _layer_fwd(h0, mem, tabs, p):
    sv = dict(h0=h0)
    hn1 = _rmsnorm(h0, p["norm_mix_pre"], name="norm_mix_pre")
    proj = _mm(hn1, p["w_in"], name="mm_in")
    ya = _attn_fwd(proj, tabs, name="attn_fwd")
    yc = _sconv_fwd(proj, p["conv_short"], name="sconv_fwd")
    qkv = _gdn_pre_fwd(proj, p["conv_gdn"], name="gdn_pre_fwd")
    gbeta = _gdn_gates_fwd(proj, p["gdn_a_log"], p["gdn_dt_bias"], name="gdn_gates_fwd")
    yg, states, tinv = _gdn_core_fwd(qkv, gbeta, proj, p["gdn_norm"], name="gdn_core_fwd")
    cat = jnp.concatenate([ya, yc, yg], axis=-1)
    mix = _mm(cat, p["w_out"], name="mm_out")
    h1 = _resnorm(h0, mix, p["norm_mix_post"], name="norm_mix_post")
    hn2 = _rmsnorm(h1, p["norm_xattn_pre"], name="norm_xattn_pre")
    memn = _rmsnorm(mem, p["norm_mem"], name="norm_mem")
    xq = _mm(hn2, p["w_xq"], out_dtype=BF16, name="mm_xq")
    kv = _mm(memn, p["w_xkv"], out_dtype=BF16, b_shards=True, name="mm_xkv")
    xo = _xattn_fwd(xq, kv, name="xattn_fwd")
    xa = _mm(xo, p["w_xo"], name="mm_xo")
    h2 = _resnorm(h1, xa, p["norm_xattn_post"], name="norm_xattn_post")
    hn3 = _rmsnorm(h2, p["norm_ffn_pre"], name="norm_ffn_pre")
    gu = _mm(hn3, p["w_gate_up"], b_shards=True, name="mm_gate_up")
    act = _swiglu(gu, name="swiglu_fwd")
    f = _mm(act, p["w_down"], name="mm_down")
    h3 = _resnorm(h2, f, p["norm_ffn_post"], name="norm_ffn_post")
    sv.update(hn1=hn1, proj=proj, qkv=qkv, gbeta=gbeta, states=states, tinv=tinv, cat=cat, mix=mix, h1=h1, hn2=hn2,
              memn=memn, xq=xq, kv=kv, xo=xo, xa=xa, h2=h2, hn3=hn3, gu=gu, act=act, f=f)
    return h3, sv


def _layer_bwd(dh3, mem, tabs, p, sv):
    g = {}
    df, g["norm_ffn_post"] = _rmsnorm_bwd(sv["f"], p["norm_ffn_post"], dh3, name="norm_ffn_post_bwd")
    dact = _mm(df, p["w_down"], tb=True, name="mm_down_da")
    g["w_down"] = _mm(sv["act"], df, ta=True, name="mm_down_dw")
    dgu = _swiglu_bwd(sv["gu"], dact, name="swiglu_bwd")
    dhn3 = _mm(dgu, p["w_gate_up"], tb=True, b_shards=True, name="mm_gate_up_da")
    g["w_gate_up"] = _mm(sv["hn3"], dgu, ta=True, out_shards=True, name="mm_gate_up_dw")
    dh2, g["norm_ffn_pre"] = _rmsnorm_bwd(sv["h2"], p["norm_ffn_pre"], dhn3, res=dh3, name="norm_ffn_pre_bwd")
    dxa, g["norm_xattn_post"] = _rmsnorm_bwd(sv["xa"], p["norm_xattn_post"], dh2, name="norm_xattn_post_bwd")
    dxo = _mm(dxa, p["w_xo"], tb=True, name="mm_xo_da")
    g["w_xo"] = _mm(sv["xo"], dxa, ta=True, name="mm_xo_dw")
    dxq, dkv = _xattn_bwd(sv["xq"], sv["kv"], dxo, name="xattn_bwd")
    dhn2 = _mm(dxq, p["w_xq"], tb=True, name="mm_xq_da")
    g["w_xq"] = _mm(sv["hn2"], dxq, ta=True, name="mm_xq_dw")
    dmemn = _mm(dkv, p["w_xkv"], tb=True, b_shards=True, name="mm_xkv_da")
    g["w_xkv"] = _mm(sv["memn"], dkv, ta=True, out_shards=True, name="mm_xkv_dw")
    _, g["norm_mem"] = _rmsnorm_bwd(mem, p["norm_mem"], dmemn, name="norm_mem_bwd")
    dh1, g["norm_xattn_pre"] = _rmsnorm_bwd(sv["h1"], p["norm_xattn_pre"], dhn2, res=dh2, name="norm_xattn_pre_bwd")
    dmix, g["norm_mix_post"] = _rmsnorm_bwd(sv["mix"], p["norm_mix_post"], dh1, name="norm_mix_post_bwd")
    dcat = _mm(dmix, p["w_out"], tb=True, name="mm_out_da")
    g["w_out"] = _mm(sv["cat"], dmix, ta=True, name="mm_out_dw")
    proj = sv["proj"]
    daq, dak, dav = _attn_bwd(proj, tabs, dcat, name="attn_bwd")
    dcb, dcc, dcx, g["conv_short"] = _sconv_bwd(proj, p["conv_short"], dcat, name="sconv_bwd")
    dqkv, dgbeta, dgate, g["gdn_norm"] = _gdn_core_bwd(sv["qkv"], sv["gbeta"], proj, p["gdn_norm"], sv["states"], sv["tinv"],
                                                        dcat, name="gdn_core_bwd")
    dgqkv, g["conv_gdn"] = _gdn_pre_bwd(proj, p["conv_gdn"], dqkv, name="gdn_pre_bwd")
    dab, g["gdn_a_log"], g["gdn_dt_bias"] = _gdn_gates_bwd(proj, p["gdn_a_log"], p["gdn_dt_bias"], dgbeta,
                                                          name="gdn_gates_bwd")
    s = proj.shape[0]
    dproj = jnp.concatenate([daq, dak, dav, dcb, dcc, dcx, dgqkv, dgate, dab,
                             jnp.zeros((s, IN_PAD - COL_AB - LANE), F32)], axis=-1)
    dhn1 = _mm(dproj, p["w_in"], tb=True, name="mm_in_da")
    g["w_in"] = _mm(sv["hn1"], dproj, ta=True, name="mm_in_dw")
    dh0, g["norm_mix_pre"] = _rmsnorm_bwd(sv["h0"], p["norm_mix_pre"], dhn1, res=dh1, name="norm_mix_pre_bwd")
    return dh0, g


MATRICES = ("w_in", "w_out", "w_xq", "w_xkv", "w_xo", "w_gate_up", "w_down")
VECTORS = ("norm_mix_pre", "norm_mix_post", "conv_short", "conv_gdn", "gdn_a_log", "gdn_dt_bias", "gdn_norm",
           "norm_mem", "norm_xattn_pre", "norm_xattn_post", "norm_ffn_pre", "norm_ffn_post")


def _w_in_to_padded(w):
    zeros = jnp.zeros(w.shape[:-1] + (IN_PAD - IN_WIDTH,), w.dtype)
    return jnp.concatenate([w[..., :COL_GATE], w[..., COL_GATE + 8:], w[..., COL_GATE:COL_GATE + 8], zeros], axis=-1)


def _w_in_from_padded(g):
    return jnp.concatenate([g[..., :COL_GATE], g[..., COL_AB:COL_AB + 8], g[..., COL_GATE:COL_AB]], axis=-1)


def _layer_params(full, l):
    p = {n: full[n][l] for n in MATRICES}
    for n in VECTORS:
        v = full[n][l]
        if n in ("gdn_a_log", "gdn_dt_bias"):
            p[n] = _pad_lanes(v)
        elif v.ndim == 1:
            p[n] = v.reshape(1, -1)
        else:
            p[n] = v
    return p


def _local_step(x, mem, pos, target, full):
    tabs = _rope_tables(pos, name="rope_tables")
    h = x
    saved, params = [], []
    for l in range(DEPTH):
        p = _layer_params(full, l)
        h, sv = _layer_fwd(h, mem, tabs, p)
        params.append(p)
        saved.append(sv)
    loss_row, dh = _loss_grad(h, target, name="loss_grad")
    grads = [None] * DEPTH
    for l in reversed(range(DEPTH)):
        dh, grads[l] = _layer_bwd(dh, mem, tabs, params[l], saved[l])
    return loss_row, dh, grads


ANY = pl.BlockSpec(memory_space=pl.ANY)
MESH = pl.DeviceIdType.MESH


def _flip(pos, mask):
    return tuple(1 - v if m else v for v, m in zip(pos, mask))


def _exchange(ins, out_shapes, remote, local, *, name):
    n_in = len(ins)
    n_out = len(out_shapes)

    def at(ref, idx):
        return ref.at[idx] if idx else ref

    def body(*refs):
        in_refs = refs[:n_in]
        out_refs = refs[n_in:n_in + n_out]
        send_sems, recv_sems, local_sems = refs[n_in + n_out:]
        me = (lax.axis_index("x"), lax.axis_index("y"), lax.axis_index("c"))
        waits = []
        for k, (ii, src_at, oi, dst_at, mask) in enumerate(remote):
            peer = _flip(me, mask)
            pltpu.make_async_remote_copy(
                src_ref=at(in_refs[ii], src_at(me, peer)), dst_ref=at(out_refs[oi], dst_at(me)),
                send_sem=send_sems.at[k], recv_sem=recv_sems.at[k], device_id=peer, device_id_type=MESH).start()
            waits.append(pltpu.make_async_remote_copy(
                src_ref=at(in_refs[ii], src_at(peer, me)), dst_ref=at(out_refs[oi], dst_at(peer)),
                send_sem=send_sems.at[k], recv_sem=recv_sems.at[k], device_id=peer, device_id_type=MESH))
        own = []
        for k, (ii, src_at, oi, dst_at) in enumerate(local):
            cp = pltpu.make_async_copy(at(in_refs[ii], src_at(me)), at(out_refs[oi], dst_at(me)), local_sems.at[k])
            cp.start()
            own.append(cp)
        for w in waits:
            w.wait_send()
            w.wait_recv()
        for cp in own:
            cp.wait()

    return pl.pallas_call(
        body, in_specs=[ANY] * n_in, out_specs=[ANY] * n_out, out_shape=list(out_shapes),
        scratch_shapes=[pltpu.SemaphoreType.DMA((len(remote),)), pltpu.SemaphoreType.DMA((len(remote),)),
                        pltpu.SemaphoreType.DMA((max(len(local), 1),))],
        name=name)(*ins)


def _chip(pos):
    return 2 * pos[0] + pos[1]


XY_MASKS = ((1, 0, 0), (0, 1, 0), (1, 1, 0))
SIBLING = (0, 0, 1)
ALL_MASKS = tuple((a, b, c) for a in (0, 1) for b in (0, 1) for c in (0, 1))[1:]


def _gather_xy(arrs, *, name):
    n = len(arrs)
    outs = [SDS((4,) + a.shape, a.dtype) for a in arrs]
    halves = [a.shape[0] // 2 for a in arrs]

    def body(*refs):
        in_refs, out_refs = refs[:n], refs[n:2 * n]
        ici_send, ici_recv, d2d_send, d2d_recv = refs[2 * n:]
        me = (lax.axis_index("x"), lax.axis_index("y"), lax.axis_index("c"))
        sibling = _flip(me, SIBLING)
        flows = []
        for i in range(n):
            mine = pl.ds(me[2] * halves[i], halves[i])
            other = pl.ds(sibling[2] * halves[i], halves[i])
            for m in XY_MASKS:
                k = len(flows)
                peer = _flip(me, m)

                def remote(src, dst, sems, to, k=k):
                    return pltpu.make_async_remote_copy(src_ref=src, dst_ref=dst, send_sem=sems[0].at[k],
                                                        recv_sem=sems[1].at[k], device_id=to, device_id_type=MESH)

                landed = out_refs[i].at[_chip(peer), mine]
                send = remote(in_refs[i].at[mine], out_refs[i].at[_chip(me), mine], (ici_send, ici_recv), peer)
                send.start()
                arrive = remote(in_refs[i].at[mine], landed, (ici_send, ici_recv), peer)
                forward = remote(landed, landed, (d2d_send, d2d_recv), sibling)
                handed = remote(out_refs[i].at[_chip(peer), other], out_refs[i].at[_chip(peer), other],
                                (d2d_send, d2d_recv), sibling)
                flows.append((send, arrive, forward, handed))
        for _, arrive, forward, _ in flows:
            arrive.wait_recv()
            forward.start()
        for send, _, forward, handed in flows:
            handed.wait_recv()
            send.wait_send()
            forward.wait_send()

    n_flows = 3 * n
    return pl.pallas_call(
        body, in_specs=[ANY] * n, out_specs=[ANY] * n, out_shape=outs,
        scratch_shapes=[pltpu.SemaphoreType.DMA((n_flows,))] * 4, name=name)(*arrs)


def _gather_all(arr, *, name):
    slot = lambda pos: (4 * pos[0] + 2 * pos[1] + pos[2],)
    whole = lambda *_: ()
    remote = [(0, whole, 0, slot, m) for m in ALL_MASKS]
    return _exchange([arr], [SDS((8,) + arr.shape, arr.dtype)], remote, [(0, whole, 0, slot)], name=name)[0]


def _swap_halves(arrs, *, name):
    outs = [SDS(a.shape[:2] + (a.shape[2] // 2, a.shape[3]), a.dtype) for a in arrs]

    def src(i):
        half = arrs[i].shape[2] // 2
        return lambda sender, receiver: (slice(None), slice(None), pl.ds(receiver[2] * half, half))

    whole = lambda *_: ()
    remote = [(i, src(i), i, whole, SIBLING) for i in range(len(arrs))]
    return _exchange(arrs, outs, remote, [], name=name)


def _scatter_xy(arrs, *, name):
    outs = [SDS((len(XY_MASKS),) + a.shape[1:], a.dtype) for a in arrs]
    src = lambda sender, receiver: (_chip(receiver),)
    remote = [(i, src, i, (lambda sender, j=j: (j,)), m) for i in range(len(arrs)) for j, m in enumerate(XY_MASKS)]
    return _exchange(arrs, outs, remote, [], name=name)


def _send_to_sibling(arrs, *, name):
    outs = [SDS(a.shape, a.dtype) for a in arrs]
    whole = lambda *_: ()
    remote = [(i, whole, i, whole, SIBLING) for i in range(len(arrs))]
    return _exchange(arrs, outs, remote, [], name=name)


def _add_half(g, other, core, *, name):
    n4, nl, r, c = g.shape
    half = r // 2
    g3 = g.reshape(n4 * nl, 2, half, c)
    o3 = other.reshape(n4 * nl, half, c)
    tr = _rows_tile(half) if half > 512 else half

    def body(core_ref, g_ref, o_ref, out_ref):
        out_ref[...] = (g_ref[...] + o_ref[...]).astype(BF16)

    return pl.pallas_call(
        body,
        grid_spec=pltpu.PrefetchScalarGridSpec(
            num_scalar_prefetch=1, grid=(n4 * nl, half // tr),
            in_specs=[BS((None, None, tr, c), lambda i, j, core_ref: (i, core_ref[0], j, 0)),
                      BS((None, tr, c), lambda i, j, core_ref: (i, j, 0))],
            out_specs=BS((None, tr, c), lambda i, j, core_ref: (i, j, 0))),
        out_shape=SDS((n4 * nl, half, c), BF16), compiler_params=_cp(), name=name)(core, g3, o3).reshape(n4, nl, half, c)


def _sum_chips(parts, mine, chip, *, name):
    n4, nl, h, c = mine.shape
    tr = _rows_tile(h) if h > 512 else h

    def body(chip_ref, p_ref, own_ref, out_ref):
        me = chip_ref[0]
        own = own_ref[...].astype(F32)
        across = [p_ref[j].astype(F32) for j in range(len(XY_MASKS))]
        t = []
        for s in range(n4):
            rel = s ^ me
            t.append(jnp.where(rel == 0, own, jnp.where(rel == 2, across[0], jnp.where(rel == 1, across[1], across[2]))))
        out_ref[...] = ((t[0] + t[1]) + t[2]) + t[3]

    return pl.pallas_call(
        body,
        grid_spec=pltpu.PrefetchScalarGridSpec(
            num_scalar_prefetch=1, grid=(nl, h // tr),
            in_specs=[BS((len(XY_MASKS), None, tr, c), lambda i, j, chip_ref: (0, i, j, 0)),
                      BS((None, None, tr, c), lambda i, j, chip_ref: (chip_ref[0], i, j, 0))],
            out_specs=BS((None, tr, c), lambda i, j, chip_ref: (i, j, 0))),
        out_shape=SDS((nl, h, c), F32), compiler_params=_cp(), name=name)(chip, parts, mine)


def _sum_devices(parts, *, name):
    n, r, c = parts.shape

    def body(p_ref, out_ref):
        acc = p_ref[0]
        for d in range(1, n):
            acc = acc + p_ref[d]
        out_ref[...] = acc

    return pl.pallas_call(
        body, grid=(1,), in_specs=[BS((n, r, c), lambda i: (0, 0, 0))], out_specs=BS((r, c), lambda i: (0, 0)),
        out_shape=SDS((r, c), F32), compiler_params=_cp(), name=name)(parts)


WEIGHTS = ("norm_mix_pre", "norm_mix_post", "w_in", "conv_short", "conv_gdn", "gdn_a_log", "gdn_dt_bias",
           "gdn_norm", "w_out", "norm_mem", "norm_xattn_pre", "norm_xattn_post", "w_xq", "w_xkv", "w_xo",
           "norm_ffn_pre", "norm_ffn_post", "w_gate_up", "w_down")
COL_SHARDED = ("w_in", "w_xkv", "w_gate_up", "conv_short", "conv_gdn")
ROW_SHARDED = ("w_out", "w_xq", "w_xo", "w_down")
SMALL_SHARDED = ("conv_short", "conv_gdn")
SMALL_ROW_PAD = 8


KEPT_AS_SHARDS = ("w_xkv", "w_gate_up")


def _from_shards(n, g):
    if n in KEPT_AS_SHARDS:
        return g
    if n in COL_SHARDED:
        t = jnp.moveaxis(g, 0, -2)
        t = t.reshape(t.shape[:-2] + (-1,))
        return _w_in_to_padded(t) if n == "w_in" else t
    return g.reshape(-1, g.shape[-1])


def _to_shards(n, g):
    if n in KEPT_AS_SHARDS:
        return g
    if n == "w_in":
        g = _w_in_from_padded(g)
        return jnp.moveaxis(g.reshape(g.shape[0], 4, -1), 1, 0)
    return g.reshape(4, -1, g.shape[-1])


def _pack_small(grads):
    rows = []
    for g in grads:
        for n in WEIGHTS:
            if n not in MATRICES:
                rows.append(g[n].reshape(-1, LANE))
    n_rows = sum(r.shape[0] for r in rows)
    pad = -n_rows % SMALL_ROW_PAD
    if pad:
        rows.append(jnp.zeros((pad, LANE), F32))
    return jnp.concatenate(rows, axis=0)


def _unpack_small(packed, like):
    out, at = [], 0
    for _ in range(DEPTH):
        g = {}
        for n in WEIGHTS:
            if n not in MATRICES:
                shape = like[n].shape
                k = math.prod(shape) // LANE
                g[n] = packed[at:at + k].reshape(shape)
                at += k
        out.append(g)
    return out


def kernel(x, mem, positions, norm_mix_pre, norm_mix_post, w_in, conv_short, conv_gdn, gdn_a_log, gdn_dt_bias, gdn_norm, w_out, norm_mem, norm_xattn_pre, norm_xattn_post, w_xq, w_xkv, w_xo, norm_ffn_pre, norm_ffn_post, w_gate_up, w_down, loss_target, m_norm_mix_pre, m_norm_mix_post, m_w_in, m_conv_short, m_conv_gdn, m_gdn_a_log, m_gdn_dt_bias, m_gdn_norm, m_w_out, m_norm_mem, m_norm_xattn_pre, m_norm_xattn_post, m_w_xq, m_w_xkv, m_w_xo, m_norm_ffn_pre, m_norm_ffn_post, m_w_gate_up, m_w_down, v_norm_mix_pre, v_norm_mix_post, v_w_in, v_conv_short, v_conv_gdn, v_gdn_a_log, v_gdn_dt_bias, v_gdn_norm, v_w_out, v_norm_mem, v_norm_xattn_pre, v_norm_xattn_post, v_w_xq, v_w_xkv, v_w_xo, v_norm_ffn_pre, v_norm_ffn_post, v_w_gate_up, v_w_down):
    args = dict(locals())
    w = {n: args[n] for n in WEIGHTS}
    m = {n: args["m_" + n] for n in WEIGHTS}
    v = {n: args["v_" + n] for n in WEIGHTS}
    seq = x.shape[1]
    chip = 2 * lax.axis_index("x") + lax.axis_index("y")
    core = lax.axis_index("c").astype(jnp.int32).reshape(1)

    keys = [(n, l) for n in MATRICES for l in range(DEPTH)] + [(n, None) for n in SMALL_SHARDED]
    own = [w[n][l].astype(BF16) if l is not None else w[n] for n, l in keys]
    blocks = _gather_xy(own, name="gather_weights")
    blocks = [lax.dynamic_update_index_in_dim(b, o, chip, axis=0) for b, o in zip(blocks, own)]
    full = {n: [None] * DEPTH for n in MATRICES}
    for (n, l), b in zip(keys, blocks):
        if l is None:
            full[n] = _from_shards(n, b)
        else:
            full[n][l] = _from_shards(n, b)
    for n in WEIGHTS:
        if n not in full:
            full[n] = w[n]

    loss_row, dx, grads = _local_step(x[0], mem[0], positions.reshape(seq, 1), loss_target[0], full)

    keys = [(n, l) for n in MATRICES for l in range(DEPTH)]
    mine = [_to_shards(n, grads[l][n])[:, None] for n, l in keys]
    theirs = _swap_halves(mine, name="grads_swap_halves")
    pair = [_add_half(a, b, core, name="grads_pair_sum") for a, b in zip(mine, theirs)]
    parts = _scatter_xy(pair, name="grads_scatter")
    chip1 = chip.astype(jnp.int32).reshape(1)
    reduced = [_sum_chips(p, pr, chip1, name="grads_chip_sum") for p, pr in zip(parts, pair)]
    others = _send_to_sibling(reduced, name="grads_share_halves")
    south = lax.axis_index("c") == 0
    grad = {}
    for n in MATRICES:
        layers = []
        for (n2, _), a, b in zip(keys, reduced, others):
            if n2 == n:
                layers += [jnp.where(south, a, b), jnp.where(south, b, a)]
        grad[n] = jnp.concatenate(layers, axis=0).reshape(w[n].shape)

    packed = _pack_small(grads)
    total = _sum_devices(_gather_all(packed, name="small_grads_gather"), name="small_grads_sum")
    small = _unpack_small(total, grads[0])
    for n in WEIGHTS:
        if n in MATRICES:
            continue
        g = jnp.stack([s[n] for s in small])
        if n in ("gdn_a_log", "gdn_dt_bias"):
            g = g[:, 0, :GDN_HEADS]
        elif n in SMALL_SHARDED:
            width = w[n].shape[-1]
            g = lax.dynamic_slice_in_dim(g, chip * width, width, axis=2)
        grad[n] = g.reshape(w[n].shape)

    delta, new_m, new_v = {}, {}, {}
    for n in WEIGHTS:
        shape = w[n].shape
        two_d = (-1, shape[-1])
        d, nm, nv = _adamw(w[n].reshape(two_d), grad[n].reshape(two_d), m[n].reshape(two_d), v[n].reshape(two_d),
                           name="adamw_" + n)
        delta[n], new_m[n], new_v[n] = d.reshape(shape), nm.reshape(shape), nv.reshape(shape)

    loss = lax.psum(loss_row[0, 0], ("x", "y", "c"))
    return (loss, dx.reshape(x.shape), *[grad[n] for n in WEIGHTS], *[delta[n] for n in WEIGHTS],
            *[new_m[n] for n in WEIGHTS], *[new_v[n] for n in WEIGHTS])
```

```python
import functools
import math

import jax
import jax.numpy as jnp
from jax import lax
from jax.experimental import pallas as pl
from jax.experimental.pallas import tpu as pltpu

F32 = jnp.float32
BF16 = jnp.bfloat16
BS = pl.BlockSpec
SDS = jax.ShapeDtypeStruct
PRECISE = lax.Precision.HIGH

D_MODEL = 1024
DEPTH = 4
EPS = 1e-6
ATTN_HEADS = 4
ATTN_HEAD_DIM = 64
ATTN_WIDTH = 256
ROPE_THETA = 500000.0
ROPE_DIM = 16
CONV_WIDTH = 256
CONV_K = 3
GDN_HEADS = 4
GDN_HEAD_DIM = 128
GDN_WIDTH = 512
GDN_CONV_K = 4
GDN_CHUNK = 64
IN_WIDTH = 3592
XATTN_HEADS = 4
XATTN_HEAD_DIM = 256
FFN_HIDDEN = 2816
ADAM_LR = 0.001
ADAM_B1 = 0.9
ADAM_B2 = 0.999
ADAM_EPS = 1e-08
ADAM_WD = 0.01
ADAM_STEP = 10

IN_PAD = 3840
COL_GDN = 1536
COL_GATE = 3072
COL_AB = 3584

VMEM_LIMIT_V7X = 56 * 1024 * 1024
LANE = 128


def _cp(**kw):
    return pltpu.CompilerParams(vmem_limit_bytes=VMEM_LIMIT_V7X, **kw)


def _tile(n, cap):
    if n <= cap:
        return n
    best = None
    for t in range(LANE, cap + 1, LANE):
        if n % t == 0:
            best = t
    assert best is not None, (n, cap)
    return best


def _dot(a, b, ca, cb, precise=False):
    dims = (((ca,), (cb,)), ((), ()))
    if precise:
        return lax.dot_general(a.astype(F32), b.astype(F32), dims, precision=PRECISE,
                               preferred_element_type=F32)
    return lax.dot_general(a.astype(BF16), b.astype(BF16), dims, preferred_element_type=F32)


def _sigmoid(x):
    return 1.0 / (1.0 + jnp.exp(-x))


MM_ROWS = 1024
MM_BLOCK_BYTES = 6 * 1024 * 1024
MM_A_BYTES = 8 * 1024 * 1024


def _mm_tn(width, k, itemsize):
    if k * width * itemsize <= MM_BLOCK_BYTES:
        return width
    return _tile(width, max(LANE, min(1024, MM_BLOCK_BYTES // (k * itemsize) // LANE * LANE)))


def _mm(a, b, *, ta=False, tb=False, out_dtype=F32, b_shards=False, out_shards=False, name):
    m, k = (a.shape[1], a.shape[0]) if ta else a.shape
    tm = _tile(m, MM_ROWS)
    ca = 0 if ta else 1

    if b_shards and tb:
        ns, n, c = b.shape
        assert k == ns * c and not ta
        tn = _tile(n, max(LANE, min(1024, MM_BLOCK_BYTES // (k * b.dtype.itemsize) // LANE * LANE)))

        def body(a_ref, b_ref, o_ref):
            acc = _dot(a_ref[:, :c], b_ref[0], 1, 1)
            for s in range(1, ns):
                acc = acc + _dot(a_ref[:, s * c:(s + 1) * c], b_ref[s], 1, 1)
            o_ref[...] = acc.astype(out_dtype)

        b_spec = BS((ns, tn, c), lambda i, j: (0, j, 0))
    else:
        if b_shards:
            ns, kb, c = b.shape
            n = ns * c
            tn = _mm_tn(c, k, b.dtype.itemsize)
            nb = c // tn
            b_spec = BS((None, k, tn), lambda i, j: (j // nb, 0, j % nb))
        else:
            kb, n = (b.shape[1], b.shape[0]) if tb else b.shape
            c = n // 4 if out_shards else n
            tn = _mm_tn(c, k, b.dtype.itemsize)
            nb = c // tn
            b_spec = BS((tn, k), lambda i, j: (j, 0)) if tb else BS((k, tn), lambda i, j: (0, j))
        assert kb == k
        cb = 1 if tb else 0

        def body(a_ref, b_ref, o_ref):
            o_ref[...] = _dot(a_ref[...], b_ref[...], ca, cb).astype(out_dtype)

    out_bytes = jnp.dtype(out_dtype).itemsize
    while tm > 256 and (tm * k * a.dtype.itemsize > MM_A_BYTES or tm * tn * out_bytes > MM_BLOCK_BYTES):
        tm //= 2
    a_spec = BS((k, tm), lambda i, j: (0, i)) if ta else BS((tm, k), lambda i, j: (i, 0))
    if out_shards:
        out_spec = BS((None, tm, tn), lambda i, j: (j // nb, i, j % nb))
        out_shape = SDS((4, m, n // 4), out_dtype)
    else:
        out_spec = BS((tm, tn), lambda i, j: (i, j))
        out_shape = SDS((m, n), out_dtype)
    return pl.pallas_call(
        body, grid=(m // tm, n // tn), in_specs=[a_spec, b_spec], out_specs=out_spec, out_shape=out_shape,
        compiler_params=_cp(), name=name)(a, b)


def _rmsnorm(x, w, *, name):
    r, d = x.shape
    tr = _tile(r, 512)

    def body(x_ref, w_ref, o_ref):
        xv = x_ref[...]
        rs = lax.rsqrt(jnp.mean(xv * xv, axis=-1, keepdims=True) + EPS)
        o_ref[...] = (xv * rs * w_ref[...]).astype(BF16)

    return pl.pallas_call(
        body, grid=(r // tr,), in_specs=[BS((tr, d), lambda i: (i, 0)), BS((1, d), lambda i: (0, 0))],
        out_specs=BS((tr, d), lambda i: (i, 0)), out_shape=SDS((r, d), BF16),
        compiler_params=_cp(), name=name)(x, w)


def _resnorm(h, m, w, *, name):
    r, d = h.shape
    tr = _tile(r, 512)

    def body(h_ref, m_ref, w_ref, o_ref):
        mv = m_ref[...]
        rs = lax.rsqrt(jnp.mean(mv * mv, axis=-1, keepdims=True) + EPS)
        o_ref[...] = h_ref[...] + mv * rs * w_ref[...]

    row = BS((tr, d), lambda i: (i, 0))
    return pl.pallas_call(
        body, grid=(r // tr,), in_specs=[row, row, BS((1, d), lambda i: (0, 0))],
        out_specs=row, out_shape=SDS((r, d), F32), compiler_params=_cp(), name=name)(h, m, w)


def _rmsnorm_bwd(x, w, dy, res=None, *, name):
    r, d = x.shape
    tr = _tile(r, 512)
    has_res = res is not None

    def body(*refs):
        if has_res:
            x_ref, w_ref, dy_ref, res_ref, dx_ref, dw_ref = refs
        else:
            x_ref, w_ref, dy_ref, dx_ref, dw_ref = refs
        xv = x_ref[...]
        dyv = dy_ref[...].astype(F32)
        rs = lax.rsqrt(jnp.mean(xv * xv, axis=-1, keepdims=True) + EPS)
        nv = xv * rs
        dyw = dyv * w_ref[...]
        dx = rs * (dyw - nv * jnp.mean(dyw * nv, axis=-1, keepdims=True))
        if has_res:
            dx = dx + res_ref[...]
        dx_ref[...] = dx

        @pl.when(pl.program_id(0) == 0)
        def _():
            dw_ref[...] = jnp.zeros_like(dw_ref)

        dw_ref[...] += jnp.sum(dyv * nv, axis=0, keepdims=True)

    row = BS((tr, d), lambda i: (i, 0))
    vec = BS((1, d), lambda i: (0, 0))
    ins = [x, w, dy] + ([res] if has_res else [])
    return pl.pallas_call(
        body, grid=(r // tr,), in_specs=[row, vec, row] + ([row] if has_res else []),
        out_specs=[row, vec], out_shape=[SDS((r, d), F32), SDS((1, d), F32)],
        compiler_params=_cp(), name=name)(*ins)


def _swiglu(gu, *, name):
    r, h2 = gu.shape
    hid = h2 // 2
    tr, tc = _tile(r, 512), _tile(hid, 1408)
    nb = hid // tc

    def body(g_ref, u_ref, o_ref):
        g = g_ref[...]
        o_ref[...] = (g * _sigmoid(g) * u_ref[...]).astype(BF16)

    return pl.pallas_call(
        body, grid=(r // tr, nb),
        in_specs=[BS((tr, tc), lambda i, j: (i, j)), BS((tr, tc), lambda i, j: (i, j + nb))],
        out_specs=BS((tr, tc), lambda i, j: (i, j)), out_shape=SDS((r, hid), BF16),
        compiler_params=_cp(), name=name)(gu, gu)


def _swiglu_bwd(gu, dact, *, name):
    r, h2 = gu.shape
    hid = h2 // 2
    tr, tc = _tile(r, 512), _tile(hid, 1408)
    nb = hid // tc

    def body(g_ref, u_ref, d_ref, o_ref):
        g = g_ref[...]
        da = d_ref[...]
        sg = _sigmoid(g)
        dgate = da * u_ref[...] * sg * (1.0 + g * (1.0 - sg))
        dup = da * g * sg
        o_ref[...] = jnp.where(pl.program_id(1) < nb, dgate, dup).astype(BF16)

    return pl.pallas_call(
        body, grid=(r // tr, 2 * nb),
        in_specs=[BS((tr, tc), lambda i, j: (i, j % nb)), BS((tr, tc), lambda i, j: (i, j % nb + nb)),
                  BS((tr, tc), lambda i, j: (i, j % nb))],
        out_specs=BS((tr, tc), lambda i, j: (i, j)), out_shape=SDS((r, h2), BF16),
        compiler_params=_cp(), name=name)(gu, gu, dact)


def _loss_grad(h, target, *, name):
    r, d = h.shape
    tr = _tile(r, 512)

    def body(h_ref, t_ref, l_ref, g_ref):
        e = h_ref[...] - t_ref[...]
        g_ref[...] = e * (1.0 / d)

        @pl.when(pl.program_id(0) == 0)
        def _():
            l_ref[...] = jnp.zeros_like(l_ref)

        l_ref[...] += jnp.full((1, LANE), 0.5 / d, F32) * jnp.sum(e * e)

    row = BS((tr, d), lambda i: (i, 0))
    return pl.pallas_call(
        body, grid=(r // tr,), in_specs=[row, row],
        out_specs=[BS((1, LANE), lambda i: (0, 0)), row],
        out_shape=[SDS((1, LANE), F32), SDS((r, d), F32)], compiler_params=_cp(), name=name)(h, target)


def _adamw(w, g, m, v, *, name):
    r, c = w.shape
    tr = r if r <= 512 else _rows_tile(r)
    bc1 = 1.0 - ADAM_B1 ** ADAM_STEP
    bc2 = 1.0 - ADAM_B2 ** ADAM_STEP

    def body(w_ref, g_ref, m_ref, v_ref, d_ref, nm_ref, nv_ref):
        gv = g_ref[...]
        nm = ADAM_B1 * m_ref[...] + (1.0 - ADAM_B1) * gv
        nv = ADAM_B2 * v_ref[...] + (1.0 - ADAM_B2) * (gv * gv)
        d_ref[...] = -ADAM_LR * ((nm / bc1) / (jnp.sqrt(nv / bc2) + ADAM_EPS) + ADAM_WD * w_ref[...])
        nm_ref[...] = nm
        nv_ref[...] = nv

    blk = BS((tr, c), lambda i: (i, 0))
    return pl.pallas_call(
        body, grid=(r // tr,), in_specs=[blk] * 4, out_specs=[blk] * 3,
        out_shape=[SDS((r, c), F32)] * 3, compiler_params=_cp(), name=name)(w, g, m, v)


def _rows_tile(r):
    for t in (512, 256, 128, 64, 32, 16, 8):
        if r % t == 0:
            return t
    return r


def _rope_tables(pos, *, name):
    s = pos.shape[0]
    half = ROPE_DIM // 2

    def body(p_ref, c_ref, a_ref, b_ref):
        lane = lax.broadcasted_iota(jnp.int32, (s, ATTN_WIDTH), 1) & (ATTN_HEAD_DIM - 1)
        fi = (lane & (half - 1)).astype(F32)
        inv_freq = jnp.exp(fi * (-2.0 * math.log(ROPE_THETA) / ROPE_DIM))
        ang = p_ref[...].astype(F32) * inv_freq
        cs, sn = jnp.cos(ang), jnp.sin(ang)
        c_ref[...] = jnp.where(lane < ROPE_DIM, cs, 1.0)
        a_ref[...] = jnp.where(lane < half, -sn, 0.0)
        b_ref[...] = jnp.where((lane >= half) & (lane < ROPE_DIM), sn, 0.0)

    full = BS((s, ATTN_WIDTH), lambda i: (0, 0))
    return pl.pallas_call(
        body, grid=(1,), in_specs=[BS((s, 1), lambda i: (0, 0))], out_specs=[full] * 3,
        out_shape=[SDS((s, ATTN_WIDTH), F32)] * 3, compiler_params=_cp(), name=name)(pos)


def _rot(x, c, a, b):
    w = x.shape[1]
    return x * c + pltpu.roll(x, w - ROPE_DIM // 2, 1) * a + pltpu.roll(x, ROPE_DIM // 2, 1) * b


def _rot_t(dy, c, a, b):
    w = dy.shape[1]
    return dy * c + pltpu.roll(dy * a, ROPE_DIM // 2, 1) + pltpu.roll(dy * b, w - ROPE_DIM // 2, 1)


def _attn_count(q0, tq, s):
    dist = (lax.broadcasted_iota(jnp.int32, (tq, s), 0) + q0) - lax.broadcasted_iota(jnp.int32, (tq, s), 1)
    cnt = ((dist <= 128).astype(F32) + (((dist & 3) == 0) & (dist <= 512)).astype(F32)
           + ((dist & 15) == 0).astype(F32))
    return jnp.where(dist >= 0, cnt, 0.0)


def _attn_probs(qm, k, cnt):
    sc = _dot(qm, k, 1, 1)
    mx = jnp.max(jnp.where(cnt > 0.0, sc, -1e30), axis=-1, keepdims=True)
    p = cnt * jnp.exp(jnp.minimum(sc - mx, 0.0))
    return p / jnp.sum(p, axis=-1, keepdims=True)


ATTN_TQ = 256


def _attn_specs(s, tq):
    def qblk(col):
        return BS((tq, ATTN_WIDTH), lambda i: (i, col))

    def full(col):
        return BS((s, ATTN_WIDTH), lambda i: (0, col))

    return qblk, full


def _attn_fwd(proj, tabs, *, name):
    s = proj.shape[0]
    tq = ATTN_TQ
    qblk, full = _attn_specs(s, tq)
    scale = ATTN_HEAD_DIM ** -0.5

    def body(q_ref, k_ref, v_ref, cq, aq, bq, ck, ak, bk, o_ref):
        q = _rot(q_ref[...], cq[...], aq[...], bq[...]) * scale
        head = lax.broadcasted_iota(jnp.int32, (1, ATTN_WIDTH), 1) >> 6

        def block(i):
            n = (i + 1) * tq
            k = _rot(k_ref[:n, :], ck[:n, :], ak[:n, :], bk[:n, :]).astype(BF16)
            v = v_ref[:n, :].astype(BF16)
            cnt = _attn_count(i * tq, tq, n)
            acc = jnp.zeros((tq, ATTN_WIDTH), F32)
            for h in range(ATTN_HEADS):
                hm = (head == h).astype(F32)
                p = _attn_probs(q * hm, k, cnt)
                acc = acc + _dot(p, v, 1, 0) * hm
            o_ref[...] = acc.astype(BF16)

        for i in range(s // tq):
            pl.when(pl.program_id(0) == i)(functools.partial(block, i))

    return pl.pallas_call(
        body, grid=(s // tq,),
        in_specs=[qblk(0), full(1), full(2), qblk(0), qblk(0), qblk(0), full(0), full(0), full(0)],
        out_specs=BS((tq, ATTN_WIDTH), lambda i: (i, 0)), out_shape=SDS((s, ATTN_WIDTH), BF16),
        compiler_params=_cp(), name=name)(proj, proj, proj, *tabs, *tabs)


def _attn_bwd(proj, tabs, dcat, *, name):
    s = proj.shape[0]
    tq = ATTN_TQ
    nq = s // tq
    qblk, full = _attn_specs(s, tq)
    scale = ATTN_HEAD_DIM ** -0.5

    def body(q_ref, k_ref, v_ref, cq, aq, bq, ck, ak, bk, dy_ref, dq_ref, dk_ref, dv_ref, dk_acc, dv_acc):
        i = pl.program_id(0)

        @pl.when(i == 0)
        def _():
            dk_acc[...] = jnp.zeros_like(dk_acc)
            dv_acc[...] = jnp.zeros_like(dv_acc)

        q = _rot(q_ref[...], cq[...], aq[...], bq[...]) * scale
        k = _rot(k_ref[...], ck[...], ak[...], bk[...]).astype(BF16)
        v = v_ref[...].astype(BF16)
        dy = dy_ref[...].astype(F32)
        cnt = _attn_count(i * tq, tq, s)
        head = lax.broadcasted_iota(jnp.int32, (1, ATTN_WIDTH), 1) >> 6
        dq = jnp.zeros((tq, ATTN_WIDTH), F32)
        for h in range(ATTN_HEADS):
            hm = (head == h).astype(F32)
            qm = q * hm
            dym = dy * hm
            p = _attn_probs(qm, k, cnt)
            dp = _dot(dym, v, 1, 1)
            ds = p * (dp - jnp.sum(p * dp, axis=-1, keepdims=True))
            dq = dq + _dot(ds, k, 1, 0) * hm
            dk_acc[...] += _dot(ds, qm, 0, 0)
            dv_acc[...] += _dot(p, dym, 0, 0)
        dq_ref[...] = _rot_t(dq * scale, cq[...], aq[...], bq[...])

        @pl.when(i == nq - 1)
        def _():
            dk_ref[...] = _rot_t(dk_acc[...], ck[...], ak[...], bk[...])
            dv_ref[...] = dv_acc[...]

    whole = BS((s, ATTN_WIDTH), lambda i: (0, 0))
    dq, dk, dv = pl.pallas_call(
        body, grid=(nq,),
        in_specs=[qblk(0), full(1), full(2), qblk(0), qblk(0), qblk(0), full(0), full(0), full(0), qblk(0)],
        out_specs=[BS((tq, ATTN_WIDTH), lambda i: (i, 0)), whole, whole],
        out_shape=[SDS((s, ATTN_WIDTH), F32)] * 3,
        scratch_shapes=[pltpu.VMEM((s, ATTN_WIDTH), F32), pltpu.VMEM((s, ATTN_WIDTH), F32)],
        compiler_params=_cp(), name=name)(proj, proj, proj, *tabs, *tabs, dcat)
    return dq, dk, dv


def _shift_down(x, n):
    if n == 0:
        return x
    rows = lax.broadcasted_iota(jnp.int32, x.shape, 0)
    return jnp.where(rows >= n, pltpu.roll(x, n, 0), 0.0)


def _shift_up(x, n):
    if n == 0:
        return x
    t = x.shape[0]
    rows = lax.broadcasted_iota(jnp.int32, x.shape, 0)
    return jnp.where(rows < t - n, pltpu.roll(x, t - n, 0), 0.0)


def _conv_fwd(z, w, kk):
    y = z * w[kk - 1:kk, :]
    for j in range(kk - 1):
        y = y + _shift_down(z, kk - 1 - j) * w[j:j + 1, :]
    return y


def _conv_bwd(z, dy, w, kk):
    dz = dy * w[kk - 1:kk, :]
    dws = []
    for j in range(kk - 1):
        dz = dz + _shift_up(dy, kk - 1 - j) * w[j:j + 1, :]
        dws.append(jnp.sum(dy * _shift_down(z, kk - 1 - j), axis=0, keepdims=True))
    dws.append(jnp.sum(dy * z, axis=0, keepdims=True))
    return dz, jnp.concatenate(dws, axis=0)


def _sconv_fwd(proj, w, *, name):
    s = proj.shape[0]

    def body(b_ref, c_ref, x_ref, w_ref, o_ref):
        y = _conv_fwd(c_ref[...] * x_ref[...], w_ref[...], CONV_K)
        o_ref[...] = (b_ref[...] * y).astype(BF16)

    def col(j):
        return BS((s, LANE), lambda i: (0, j + i))

    return pl.pallas_call(
        body, grid=(CONV_WIDTH // LANE,), in_specs=[col(6), col(8), col(10), BS((CONV_K, LANE), lambda i: (0, i))],
        out_specs=BS((s, LANE), lambda i: (0, i)), out_shape=SDS((s, CONV_WIDTH), BF16),
        compiler_params=_cp(), name=name)(proj, proj, proj, w)


def _sconv_bwd(proj, w, dcat, *, name):
    s = proj.shape[0]

    def body(b_ref, c_ref, x_ref, w_ref, dy_ref, db_ref, dc_ref, dx_ref, dw_ref):
        cv, xv, wv = c_ref[...], x_ref[...], w_ref[...]
        dy = dy_ref[...].astype(F32)
        z = cv * xv
        db_ref[...] = dy * _conv_fwd(z, wv, CONV_K)
        dz, dw = _conv_bwd(z, dy * b_ref[...], wv, CONV_K)
        dc_ref[...] = dz * xv
        dx_ref[...] = dz * cv
        dw_ref[...] = dw

    def col(j):
        return BS((s, LANE), lambda i: (0, j + i))

    out = BS((s, LANE), lambda i: (0, i))
    wspec = BS((CONV_K, LANE), lambda i: (0, i))
    return pl.pallas_call(
        body, grid=(CONV_WIDTH // LANE,), in_specs=[col(6), col(8), col(10), wspec, col(2)],
        out_specs=[out, out, out, wspec],
        out_shape=[SDS((s, CONV_WIDTH), F32)] * 3 + [SDS((CONV_K, CONV_WIDTH), F32)],
        compiler_params=_cp(), name=name)(proj, proj, proj, w, dcat)


def _l2n(y, scale):
    r = lax.rsqrt(jnp.sum(y * y, axis=-1, keepdims=True) + EPS)
    return y * r * scale, r


def _gdn_pre_fwd(proj, w, *, name):
    s = proj.shape[0]
    nh = GDN_HEADS

    def body(x_ref, w_ref, o_ref):
        j = pl.program_id(0)
        c = _conv_fwd(x_ref[...], w_ref[...], GDN_CONV_K)
        y = c * _sigmoid(c)
        scale = jnp.where(j < nh, GDN_HEAD_DIM ** -0.5, 1.0)
        n, _ = _l2n(y, scale)
        o_ref[...] = jnp.where(j < 2 * nh, n, y)

    return pl.pallas_call(
        body, grid=(3 * nh,),
        in_specs=[BS((s, LANE), lambda j: (0, COL_GDN // LANE + j)), BS((GDN_CONV_K, LANE), lambda j: (0, j))],
        out_specs=BS((s, LANE), lambda j: (0, j)), out_shape=SDS((s, 3 * GDN_WIDTH), F32),
        compiler_params=_cp(), name=name)(proj, w)


def _gdn_pre_bwd(proj, w, dqkv, *, name):
    s = proj.shape[0]
    nh = GDN_HEADS

    def body(x_ref, w_ref, d_ref, dx_ref, dw_ref):
        j = pl.program_id(0)
        xv, wv, dn = x_ref[...], w_ref[...], d_ref[...]
        c = _conv_fwd(xv, wv, GDN_CONV_K)
        sg = _sigmoid(c)
        y = c * sg
        scale = jnp.where(j < nh, GDN_HEAD_DIM ** -0.5, 1.0)
        n, r = _l2n(y, 1.0)
        dns = dn * scale
        dy_norm = r * (dns - n * jnp.sum(dns * n, axis=-1, keepdims=True))
        dy = jnp.where(j < 2 * nh, dy_norm, dn)
        dc = dy * sg * (1.0 + c * (1.0 - sg))
        dx, dw = _conv_bwd(xv, dc, wv, GDN_CONV_K)
        dx_ref[...] = dx
        dw_ref[...] = dw

    wspec = BS((GDN_CONV_K, LANE), lambda j: (0, j))
    blk = BS((s, LANE), lambda j: (0, j))
    return pl.pallas_call(
        body, grid=(3 * nh,),
        in_specs=[BS((s, LANE), lambda j: (0, COL_GDN // LANE + j)), wspec, blk],
        out_specs=[blk, wspec], out_shape=[SDS((s, 3 * GDN_WIDTH), F32), SDS((GDN_CONV_K, 3 * GDN_WIDTH), F32)],
        compiler_params=_cp(), name=name)(proj, w, dqkv)


def _softplus(x):
    return jnp.maximum(x, 0.0) + jnp.log(1.0 + jnp.exp(-jnp.abs(x)))


def _gdn_gates_fwd(proj, a_log, dt_bias, *, name):
    s = proj.shape[0]

    def body(x_ref, al_ref, dt_ref, o_ref):
        xv = x_ref[...]
        lane = lax.broadcasted_iota(jnp.int32, xv.shape, 1)
        g = -jnp.exp(al_ref[...]) * _softplus(xv + dt_ref[...])
        o_ref[...] = jnp.where(lane < GDN_HEADS, g, jnp.where(lane < 2 * GDN_HEADS, _sigmoid(xv), 0.0))

    vec = BS((1, LANE), lambda i: (0, 0))
    return pl.pallas_call(
        body, grid=(1,), in_specs=[BS((s, LANE), lambda i: (0, COL_AB // LANE)), vec, vec],
        out_specs=BS((s, LANE), lambda i: (0, 0)), out_shape=SDS((s, LANE), F32),
        compiler_params=_cp(), name=name)(proj, a_log, dt_bias)


def _gdn_gates_bwd(proj, a_log, dt_bias, dgb, *, name):
    s = proj.shape[0]

    def body(x_ref, al_ref, dt_ref, d_ref, dx_ref, dal_ref, ddt_ref):
        xv, dv = x_ref[...], d_ref[...]
        lane = lax.broadcasted_iota(jnp.int32, xv.shape, 1)
        is_g = lane < GDN_HEADS
        ea = -jnp.exp(al_ref[...])
        z = xv + dt_ref[...]
        da = jnp.where(is_g, dv * ea * _sigmoid(z), 0.0)
        beta = _sigmoid(xv)
        dx_ref[...] = jnp.where(is_g, da, jnp.where(lane < 2 * GDN_HEADS, dv * beta * (1.0 - beta), 0.0))
        dal_ref[...] = jnp.sum(jnp.where(is_g, dv * ea * _softplus(z), 0.0), axis=0, keepdims=True)
        ddt_ref[...] = jnp.sum(da, axis=0, keepdims=True)

    vec = BS((1, LANE), lambda i: (0, 0))
    blk = BS((s, LANE), lambda i: (0, 0))
    return pl.pallas_call(
        body, grid=(1,), in_specs=[BS((s, LANE), lambda i: (0, COL_AB // LANE)), vec, vec, blk],
        out_specs=[blk, vec, vec], out_shape=[SDS((s, LANE), F32), SDS((1, LANE), F32), SDS((1, LANE), F32)],
        compiler_params=_cp(), name=name)(proj, a_log, dt_bias, dgb)


def _col_to_row(col, eye):
    return jnp.sum(jnp.where(eye, col, 0.0), axis=0, keepdims=True)


def _row_to_col(row, eye):
    return jnp.sum(jnp.where(eye, row, 0.0), axis=1, keepdims=True)


GDN_GROUP = 4
TRI_BLOCK_SHIFT = 4


def _gdn_masks(c):
    row = lax.broadcasted_iota(jnp.int32, (c, c), 0)
    col = lax.broadcasted_iota(jnp.int32, (c, c), 1)
    return dict(row=row, col=col, eye=row == col, low=row >= col, strict=row > col, upper=row <= col,
                on_diag=(row >> TRI_BLOCK_SHIFT) == (col >> TRI_BLOCK_SHIFT))


def _tri_inv(a_list, mk):
    eye_f = mk["eye"].astype(F32)
    ds = [jnp.where(mk["on_diag"], a, 0.0) for a in a_list]
    xs = [eye_f - d for d in ds]
    ps = ds
    for _ in range(3):
        ps = [_dot(p, p, 1, 0, precise=True) for p in ps]
        xs = [x + _dot(x, p, 1, 0, precise=True) for x, p in zip(xs, ps)]
    ms = [_dot(x, a - d, 1, 0, precise=True) for x, a, d in zip(xs, a_list, ds)]
    m2s = [_dot(m, m, 1, 0, precise=True) for m in ms]
    ys = [eye_f - m for m in ms]
    ys = [y + _dot(y, m2, 1, 0, precise=True) for y, m2 in zip(ys, m2s)]
    return [_dot(y, x, 1, 0, precise=True) for y, x in zip(ys, xs)]


def _gdn_pre(qs, ks, vs, gs, betas, mk, ts=None):
    c, hd = qs[0].shape
    eye, low = mk["eye"], mk["low"]
    g_rows = [_col_to_row(g, eye) for g in gs]
    d_cols = [jnp.sum(jnp.where(low, gr, 0.0), axis=1, keepdims=True) for gr in g_rows]
    d_rows = [jnp.sum(jnp.where(mk["upper"], g, 0.0), axis=0, keepdims=True) for g in gs]
    rels = [jnp.where(low, jnp.exp(jnp.minimum(dc - dr, 0.0)), 0.0) for dc, dr in zip(d_cols, d_rows)]
    d_lasts = [dc[c - 1:c, :] for dc in d_cols]
    es = [jnp.exp(dc) for dc in d_cols]
    fs = [jnp.exp(dl - dc) for dl, dc in zip(d_lasts, d_cols)]
    cds = [jnp.exp(dl) for dl in d_lasts]
    kbs = [k * b for k, b in zip(ks, betas)]
    kbqs = [jnp.concatenate([kb, q], axis=0) for kb, q in zip(kbs, qs)]
    kqk = [_dot(kbq, k, 1, 1) for kbq, k in zip(kbqs, ks)]
    kks = [x[:c, :] for x in kqk]
    qks = [x[c:, :] for x in kqk]
    if ts is None:
        ts = _tri_inv([jnp.where(mk["strict"], kk * rel, 0.0) for kk, rel in zip(kks, rels)], mk)
    vbs = [v * b for v, b in zip(vs, betas)]
    kbes = [kb * e for kb, e in zip(kbs, es)]
    uws = [_dot(t, jnp.concatenate([vb, kbe], axis=1), 1, 0) for t, vb, kbe in zip(ts, vbs, kbes)]
    out = []
    for i in range(len(qs)):
        out.append(dict(rel=rels[i], e=es[i], f=fs[i], cd=cds[i], kb=kbs[i], kbq=kbqs[i], kk=kks[i], qk=qks[i],
                        t=ts[i], u=uws[i][:, :hd], w=uws[i][:, hd:], uw=uws[i], attn=qks[i] * rels[i],
                        qd=qs[i] * es[i], kd=ks[i] * fs[i]))
    return out


def _gdn_apply(pres, sts, leaving=True):
    c = pres[0]["u"].shape[0]
    wqs = [_dot(jnp.concatenate([p["w"], p["qd"]], axis=0), st, 1, 0) for p, st in zip(pres, sts)]
    vns = [p["u"] - x[:c, :] for p, x in zip(pres, wqs)]
    os_ = [x[c:, :] + _dot(p["attn"], vn, 1, 0) for p, x, vn in zip(pres, wqs, vns)]
    if not leaving:
        return vns, os_, None
    new = [p["cd"] * st + _dot(p["kd"], vn, 0, 0) for p, st, vn in zip(pres, sts, vns)]
    return vns, os_, new


def _gdn_bwd_rest(qs, ks, vs, betas, sts, pres, vns, dos, dvns, dsts, mk):
    c, hd = qs[0].shape
    eye = mk["eye"]
    n = range(len(qs))
    dkds = [_dot(vns[i], dsts[i], 1, 1) for i in n]
    dcds = [jnp.sum(sts[i] * dsts[i]) for i in n]
    dattns = [jnp.where(mk["low"], _dot(dos[i], vns[i], 1, 1), 0.0) for i in n]
    dqdws = [_dot(jnp.concatenate([dos[i], -dvns[i]], axis=0), sts[i], 1, 1) for i in n]
    dqds = [x[:c, :] for x in dqdws]
    dws = [x[c:, :] for x in dqdws]
    dvks = [_dot(pres[i]["t"], jnp.concatenate([dvns[i], dws[i]], axis=1), 0, 0) for i in n]
    das = [jnp.where(mk["strict"], -_dot(dvks[i], pres[i]["uw"], 1, 1), 0.0) for i in n]
    dkqs = [jnp.concatenate([das[i] * pres[i]["rel"], dattns[i] * pres[i]["rel"]], axis=0) for i in n]
    dkbdq = [_dot(dkqs[i], ks[i], 1, 0) for i in n]
    dk0 = [_dot(dkqs[i], pres[i]["kbq"], 0, 0) for i in n]
    out = []
    rows1 = lax.broadcasted_iota(jnp.int32, (c, 1), 0)
    for i in n:
        p = pres[i]
        dvb, dkbe = dvks[i][:, :hd], dvks[i][:, hd:]
        grel = (das[i] * p["kk"] + dattns[i] * p["qk"]) * p["rel"]
        dkb = dkbdq[i][:c, :] + dkbe * p["e"]
        dk = dk0[i] + dkds[i] * p["f"] + dkb * betas[i]
        dq = dkbdq[i][c:, :] + dqds[i] * p["e"]
        dv = dvb * betas[i]
        dbeta = jnp.sum(dkb * ks[i], axis=1, keepdims=True) + jnp.sum(dvb * vs[i], axis=1, keepdims=True)
        de = jnp.sum(dqds[i] * qs[i], axis=1, keepdims=True) + jnp.sum(dkbe * p["kb"], axis=1, keepdims=True)
        dff = jnp.sum(dkds[i] * ks[i], axis=1, keepdims=True) * p["f"]
        dd = (de * p["e"] - dff + jnp.sum(grel, axis=1, keepdims=True)
              - _row_to_col(jnp.sum(grel, axis=0, keepdims=True), eye))
        dd = dd + jnp.where(rows1 == c - 1, jnp.sum(dff) + dcds[i] * p["cd"], 0.0)
        dg = jnp.sum(jnp.where(mk["upper"], _col_to_row(dd, eye), 0.0), axis=1, keepdims=True)
        out.append((dq, dk, dv, dg, dbeta))
    return out


def _gdn_specs(c):
    def qkv(j):
        return BS((c, GDN_WIDTH), lambda n: (n, j))

    return qkv


def _gdn_core_fwd(qkv, gbeta, proj, norm_w, *, name):
    s = qkv.shape[0]
    c, nh, hd, grp = GDN_CHUNK, GDN_HEADS, GDN_HEAD_DIM, GDN_GROUP
    n_chunks = s // c
    blk = _gdn_specs(grp * c)
    inst = [(sub, h) for sub in range(grp) for h in range(nh)]

    def body(q_ref, k_ref, v_ref, gb_ref, gate_ref, nw_ref, y_ref, st_ref, t_ref, state):
        @pl.when(pl.program_id(0) == 0)
        def _():
            state[...] = jnp.zeros_like(state)

        mk = _gdn_masks(c)
        rows = [slice(sub * c, (sub + 1) * c) for sub in range(grp)]
        lanes = [slice(h * hd, (h + 1) * hd) for h in range(nh)]
        gbs = [gb_ref[r, :] for r in rows]
        pres = _gdn_pre([q_ref[rows[sub], lanes[h]] for sub, h in inst], [k_ref[rows[sub], lanes[h]] for sub, h in inst],
                        [v_ref[rows[sub], lanes[h]] for sub, h in inst], [gbs[sub][:, h:h + 1] for sub, h in inst],
                        [gbs[sub][:, nh + h:nh + h + 1] for sub, h in inst], mk)
        sts = [state[ls, :] for ls in lanes]
        outs = []
        for sub in range(grp):
            for h in range(nh):
                st_ref[pl.ds((sub * nh + h) * hd, hd), :] = sts[h]
            _, os_, sts = _gdn_apply(pres[sub * nh:(sub + 1) * nh], sts)
            outs += os_
        for h in range(nh):
            state[lanes[h], :] = sts[h]
        nw = nw_ref[...]
        for i, (sub, h) in enumerate(inst):
            t_ref[pl.ds(i * c, c), :] = pres[i]["t"]
            o = outs[i]
            gate = gate_ref[rows[sub], lanes[h]]
            rs = lax.rsqrt(jnp.mean(o * o, axis=-1, keepdims=True) + EPS)
            y_ref[rows[sub], lanes[h]] = (o * rs * nw * (gate * _sigmoid(gate))).astype(BF16)

    return pl.pallas_call(
        body, grid=(n_chunks // grp,),
        in_specs=[blk(0), blk(1), blk(2), BS((grp * c, LANE), lambda n: (n, 0)),
                  BS((grp * c, GDN_WIDTH), lambda n: (n, COL_GATE // GDN_WIDTH)), BS((1, hd), lambda n: (0, 0))],
        out_specs=[BS((grp * c, GDN_WIDTH), lambda n: (n, 0)), BS((grp * nh * hd, hd), lambda n: (n, 0)),
                   BS((grp * nh * c, c), lambda n: (n, 0))],
        out_shape=[SDS((s, GDN_WIDTH), BF16), SDS((n_chunks * nh * hd, hd), F32), SDS((n_chunks * nh * c, c), F32)],
        scratch_shapes=[pltpu.VMEM((nh * hd, hd), F32)],
        compiler_params=_cp(), name=name)(qkv, qkv, qkv, gbeta, proj, norm_w)


def _gdn_core_bwd(qkv, gbeta, proj, norm_w, states, tinv, dcat, *, name):
    s = qkv.shape[0]
    c, nh, hd, grp = GDN_CHUNK, GDN_HEADS, GDN_HEAD_DIM, GDN_GROUP
    n_chunks = s // c
    last = n_chunks // grp - 1
    inst = [(sub, h) for sub in range(grp) for h in range(nh)]

    def rev(j, w):
        return BS((grp * c, w), lambda n: (last - n, j))

    def body(q_ref, k_ref, v_ref, gb_ref, gate_ref, nw_ref, st_ref, t_ref, dy_ref,
             dqkv_ref, dgb_ref, dgate_ref, dnw_ref, dstate):
        @pl.when(pl.program_id(0) == 0)
        def _():
            dstate[...] = jnp.zeros_like(dstate)
            dnw_ref[...] = jnp.zeros_like(dnw_ref)

        mk = _gdn_masks(c)
        rows = [slice(sub * c, (sub + 1) * c) for sub in range(grp)]
        lanes = [slice(h * hd, (h + 1) * hd) for h in range(nh)]
        gbs = [gb_ref[r, :] for r in rows]
        qs = [q_ref[rows[sub], lanes[h]] for sub, h in inst]
        ks = [k_ref[rows[sub], lanes[h]] for sub, h in inst]
        vs = [v_ref[rows[sub], lanes[h]] for sub, h in inst]
        betas = [gbs[sub][:, nh + h:nh + h + 1] for sub, h in inst]
        sts = [st_ref[pl.ds(i * hd, hd), :] for i in range(len(inst))]
        pres = _gdn_pre(qs, ks, vs, [gbs[sub][:, h:h + 1] for sub, h in inst], betas, mk,
                        ts=[t_ref[pl.ds(i * c, c), :] for i in range(len(inst))])
        vns, outs, _ = _gdn_apply(pres, sts, leaving=False)

        nw = nw_ref[...]
        dnw = jnp.zeros((1, hd), F32)
        dos = []
        for i, (sub, h) in enumerate(inst):
            o = outs[i]
            gate = gate_ref[rows[sub], lanes[h]]
            dy = dy_ref[rows[sub], lanes[h]].astype(F32)
            sg = _sigmoid(gate)
            rs = lax.rsqrt(jnp.mean(o * o, axis=-1, keepdims=True) + EPS)
            nrm = o * rs
            dgate_ref[rows[sub], lanes[h]] = dy * nrm * nw * sg * (1.0 + gate * (1.0 - sg))
            dnv = dy * (gate * sg)
            dnw = dnw + jnp.sum(dnv * nrm, axis=0, keepdims=True)
            dno = dnv * nw
            dos.append(rs * (dno - nrm * jnp.mean(dno * nrm, axis=-1, keepdims=True)))
        dnw_ref[...] += dnw

        from_o = [_dot(p["attn"], do, 0, 0) for p, do in zip(pres, dos)]
        to_st = [_dot(p["qd"], do, 0, 0) for p, do in zip(pres, dos)]
        dst = [dstate[ls, :] for ls in lanes]
        dsts = [None] * len(inst)
        dvns = [None] * len(inst)
        for sub in reversed(range(grp)):
            idx = [sub * nh + h for h in range(nh)]
            for h, i in enumerate(idx):
                dsts[i] = dst[h]
                dvns[i] = from_o[i] + _dot(pres[i]["kd"], dst[h], 1, 0)
            dst = [pres[i]["cd"] * dst[h] + to_st[i] - _dot(pres[i]["w"], dvns[i], 0, 0) for h, i in enumerate(idx)]
        for h in range(nh):
            dstate[lanes[h], :] = dst[h]

        grads = _gdn_bwd_rest(qs, ks, vs, betas, sts, pres, vns, dos, dvns, dsts, mk)
        lane = lax.broadcasted_iota(jnp.int32, (c, LANE), 1)
        dgb = [jnp.zeros((c, LANE), F32) for _ in range(grp)]
        for (sub, h), (dq, dk, dv, dg, dbeta) in zip(inst, grads):
            dqkv_ref[rows[sub], lanes[h]] = dq
            dqkv_ref[rows[sub], slice(GDN_WIDTH + h * hd, GDN_WIDTH + (h + 1) * hd)] = dk
            dqkv_ref[rows[sub], slice(2 * GDN_WIDTH + h * hd, 2 * GDN_WIDTH + (h + 1) * hd)] = dv
            dgb[sub] = jnp.where(lane == h, dg, jnp.where(lane == nh + h, dbeta, dgb[sub]))
        for sub in range(grp):
            dgb_ref[rows[sub], :] = dgb[sub]

    return pl.pallas_call(
        body, grid=(n_chunks // grp,),
        in_specs=[rev(0, GDN_WIDTH), rev(1, GDN_WIDTH), rev(2, GDN_WIDTH), rev(0, LANE),
                  rev(COL_GATE // GDN_WIDTH, GDN_WIDTH), BS((1, hd), lambda n: (0, 0)),
                  BS((grp * nh * hd, hd), lambda n: (last - n, 0)), BS((grp * nh * c, c), lambda n: (last - n, 0)),
                  rev(1, GDN_WIDTH)],
        out_specs=[rev(0, 3 * GDN_WIDTH), rev(0, LANE), rev(0, GDN_WIDTH), BS((1, hd), lambda n: (0, 0))],
        out_shape=[SDS((s, 3 * GDN_WIDTH), F32), SDS((s, LANE), F32), SDS((s, GDN_WIDTH), F32), SDS((1, hd), F32)],
        scratch_shapes=[pltpu.VMEM((nh * hd, hd), F32)],
        compiler_params=_cp(), name=name)(qkv, qkv, qkv, gbeta, proj, norm_w, states, tinv, dcat)


XATTN_TQ = 512


def _xattn_probs(qh, kh):
    sc = _dot(qh, kh, 1, 1) * (XATTN_HEAD_DIM ** -0.5)
    p = jnp.exp(sc - jnp.max(sc, axis=-1, keepdims=True))
    return p / jnp.sum(p, axis=-1, keepdims=True)


def _xattn_fwd(q, kv, *, name):
    s, d = q.shape
    m = kv.shape[0]
    tq, hd = _tile(s, XATTN_TQ), XATTN_HEAD_DIM

    def body(q_ref, k_ref, v_ref, o_ref):
        for h in range(XATTN_HEADS):
            ls = slice(h * hd, (h + 1) * hd)
            p = _xattn_probs(q_ref[:, ls], k_ref[:, ls])
            o_ref[:, ls] = _dot(p, v_ref[:, ls], 1, 0).astype(BF16)

    return pl.pallas_call(
        body, grid=(s // tq,),
        in_specs=[BS((tq, d), lambda i: (i, 0)), BS((m, d), lambda i: (0, 0)), BS((m, d), lambda i: (0, 1))],
        out_specs=BS((tq, d), lambda i: (i, 0)), out_shape=SDS((s, d), BF16),
        compiler_params=_cp(), name=name)(q, kv, kv)


def _xattn_bwd(q, kv, do, *, name):
    s, d = q.shape
    m = kv.shape[0]
    tq, hd = _tile(s, XATTN_TQ), XATTN_HEAD_DIM
    scale = hd ** -0.5

    def body(q_ref, k_ref, v_ref, do_ref, dq_ref, dkv_ref):
        @pl.when(pl.program_id(0) == 0)
        def _():
            dkv_ref[...] = jnp.zeros_like(dkv_ref)

        for h in range(XATTN_HEADS):
            ls = slice(h * hd, (h + 1) * hd)
            vs = slice(d + h * hd, d + (h + 1) * hd)
            qh, kh, doh = q_ref[:, ls], k_ref[:, ls], do_ref[:, ls]
            p = _xattn_probs(qh, kh)
            dp = _dot(doh, v_ref[:, ls], 1, 1)
            ds = p * (dp - jnp.sum(p * dp, axis=-1, keepdims=True)) * scale
            dq_ref[:, ls] = _dot(ds, kh, 1, 0).astype(BF16)
            dkv_ref[:, ls] += _dot(ds, qh, 0, 0)
            dkv_ref[:, vs] += _dot(p, doh, 0, 0)

    row = BS((tq, d), lambda i: (i, 0))
    return pl.pallas_call(
        body, grid=(s // tq,),
        in_specs=[row, BS((m, d), lambda i: (0, 0)), BS((m, d), lambda i: (0, 1)), row],
        out_specs=[row, BS((m, 2 * d), lambda i: (0, 0))],
        out_shape=[SDS((s, d), BF16), SDS((m, 2 * d), F32)],
        compiler_params=_cp(), name=name)(q, kv, kv, do)


def _pad_lanes(vec4):
    return jnp.zeros((1, LANE), F32).at[0, :GDN_HEADS].set(vec4)


def _layer_fwd(h0, mem, tabs, p):
    sv = dict(h0=h0)
    hn1 = _rmsnorm(h0, p["norm_mix_pre"], name="norm_mix_pre")
    proj = _mm(hn1, p["w_in"], name="mm_in")
    ya = _attn_fwd(proj, tabs, name="attn_fwd")
    yc = _sconv_fwd(proj, p["conv_short"], name="sconv_fwd")
    qkv = _gdn_pre_fwd(proj, p["conv_gdn"], name="gdn_pre_fwd")
    gbeta = _gdn_gates_fwd(proj, p["gdn_a_log"], p["gdn_dt_bias"], name="gdn_gates_fwd")
    yg, states, tinv = _gdn_core_fwd(qkv, gbeta, proj, p["gdn_norm"], name="gdn_core_fwd")
    cat = jnp.concatenate([ya, yc, yg], axis=-1)
    mix = _mm(cat, p["w_out"], name="mm_out")
    h1 = _resnorm(h0, mix, p["norm_mix_post"], name="norm_mix_post")
    hn2 = _rmsnorm(h1, p["norm_xattn_pre"], name="norm_xattn_pre")
    memn = _rmsnorm(mem, p["norm_mem"], name="norm_mem")
    xq = _mm(hn2, p["w_xq"], out_dtype=BF16, name="mm_xq")
    kv = _mm(memn, p["w_xkv"], out_dtype=BF16, b_shards=True, name="mm_xkv")
    xo = _xattn_fwd(xq, kv, name="xattn_fwd")
    xa = _mm(xo, p["w_xo"], name="mm_xo")
    h2 = _resnorm(h1, xa, p["norm_xattn_post"], name="norm_xattn_post")
    hn3 = _rmsnorm(h2, p["norm_ffn_pre"], name="norm_ffn_pre")
    gu = _mm(hn3, p["w_gate_up"], b_shards=True, name="mm_gate_up")
    act = _swiglu(gu, name="swiglu_fwd")
    f = _mm(act, p["w_down"], name="mm_down")
    h3 = _resnorm(h2, f, p["norm_ffn_post"], name="norm_ffn_post")
    sv.update(hn1=hn1, proj=proj, qkv=qkv, gbeta=gbeta, states=states, tinv=tinv, cat=cat, mix=mix, h1=h1, hn2=hn2,
              memn=memn, xq=xq, kv=kv, xo=xo, xa=xa, h2=h2, hn3=hn3, gu=gu, act=act, f=f)
    return h3, sv


def _layer_bwd(dh3, mem, tabs, p, sv):
    g = {}
    df, g["norm_ffn_post"] = _rmsnorm_bwd(sv["f"], p["norm_ffn_post"], dh3, name="norm_ffn_post_bwd")
    dact = _mm(df, p["w_down"], tb=True, name="mm_down_da")
    g["w_down"] = _mm(sv["act"], df, ta=True, name="mm_down_dw")
    dgu = _swiglu_bwd(sv["gu"], dact, name="swiglu_bwd")
    dhn3 = _mm(dgu, p["w_gate_up"], tb=True, b_shards=True, name="mm_gate_up_da")
    g["w_gate_up"] = _mm(sv["hn3"], dgu, ta=True, out_shards=True, name="mm_gate_up_dw")
    dh2, g["norm_ffn_pre"] = _rmsnorm_bwd(sv["h2"], p["norm_ffn_pre"], dhn3, res=dh3, name="norm_ffn_pre_bwd")
    dxa, g["norm_xattn_post"] = _rmsnorm_bwd(sv["xa"], p["norm_xattn_post"], dh2, name="norm_xattn_post_bwd")
    dxo = _mm(dxa, p["w_xo"], tb=True, name="mm_xo_da")
    g["w_xo"] = _mm(sv["xo"], dxa, ta=True, name="mm_xo_dw")
    dxq, dkv = _xattn_bwd(sv["xq"], sv["kv"], dxo, name="xattn_bwd")
    dhn2 = _mm(dxq, p["w_xq"], tb=True, name="mm_xq_da")
    g["w_xq"] = _mm(sv["hn2"], dxq, ta=True, name="mm_xq_dw")
    dmemn = _mm(dkv, p["w_xkv"], tb=True, b_shards=True, name="mm_xkv_da")
    g["w_xkv"] = _mm(sv["memn"], dkv, ta=True, out_shards=True, name="mm_xkv_dw")
    _, g["norm_mem"] = _rmsnorm_bwd(mem, p["norm_mem"], dmemn, name="norm_mem_bwd")
    dh1, g["norm_xattn_pre"] = _rmsnorm_bwd(sv["h1"], p["norm_xattn_pre"], dhn2, res=dh2, name="norm_xattn_pre_bwd")
    dmix, g["norm_mix_post"] = _rmsnorm_bwd(sv["mix"], p["norm_mix_post"], dh1, name="norm_mix_post_bwd")
    dcat = _mm(dmix, p["w_out"], tb=True, name="mm_out_da")
    g["w_out"] = _mm(sv["cat"], dmix, ta=True, name="mm_out_dw")
    proj = sv["proj"]
    daq, dak, dav = _attn_bwd(proj, tabs, dcat, name="attn_bwd")
    dcb, dcc, dcx, g["conv_short"] = _sconv_bwd(proj, p["conv_short"], dcat, name="sconv_bwd")
    dqkv, dgbeta, dgate, g["gdn_norm"] = _gdn_core_bwd(sv["qkv"], sv["gbeta"], proj, p["gdn_norm"], sv["states"], sv["tinv"],
                                                        dcat, name="gdn_core_bwd")
    dgqkv, g["conv_gdn"] = _gdn_pre_bwd(proj, p["conv_gdn"], dqkv, name="gdn_pre_bwd")
    dab, g["gdn_a_log"], g["gdn_dt_bias"] = _gdn_gates_bwd(proj, p["gdn_a_log"], p["gdn_dt_bias"], dgbeta,
                                                          name="gdn_gates_bwd")
    s = proj.shape[0]
    dproj = jnp.concatenate([daq, dak, dav, dcb, dcc, dcx, dgqkv, dgate, dab,
                             jnp.zeros((s, IN_PAD - COL_AB - LANE), F32)], axis=-1)
    dhn1 = _mm(dproj, p["w_in"], tb=True, name="mm_in_da")
    g["w_in"] = _mm(sv["hn1"], dproj, ta=True, name="mm_in_dw")
    dh0, g["norm_mix_pre"] = _rmsnorm_bwd(sv["h0"], p["norm_mix_pre"], dhn1, res=dh1, name="norm_mix_pre_bwd")
    return dh0, g


MATRICES = ("w_in", "w_out", "w_xq", "w_xkv", "w_xo", "w_gate_up", "w_down")
VECTORS = ("norm_mix_pre", "norm_mix_post", "conv_short", "conv_gdn", "gdn_a_log", "gdn_dt_bias", "gdn_norm",
           "norm_mem", "norm_xattn_pre", "norm_xattn_post", "norm_ffn_pre", "norm_ffn_post")


def _w_in_to_padded(w):
    zeros = jnp.zeros(w.shape[:-1] + (IN_PAD - IN_WIDTH,), w.dtype)
    return jnp.concatenate([w[..., :COL_GATE], w[..., COL_GATE + 8:], w[..., COL_GATE:COL_GATE + 8], zeros], axis=-1)


def _w_in_from_padded(g):
    return jnp.concatenate([g[..., :COL_GATE], g[..., COL_AB:COL_AB + 8], g[..., COL_GATE:COL_AB]], axis=-1)


def _layer_params(full, l):
    p = {n: full[n][l] for n in MATRICES}
    for n in VECTORS:
        v = full[n][l]
        if n in ("gdn_a_log", "gdn_dt_bias"):
            p[n] = _pad_lanes(v)
        elif v.ndim == 1:
            p[n] = v.reshape(1, -1)
        else:
            p[n] = v
    return p


def _local_step(x, mem, pos, target, full, matrices_for=None, on_grads=None):
    tabs = _rope_tables(pos, name="rope_tables")
    h = x
    saved, params = [], []
    for l in range(DEPTH):
        if matrices_for is not None:
            full = {**full, **{n: {l: v} for n, v in matrices_for(l, h).items()}}
        p = _layer_params(full, l)
        h, sv = _layer_fwd(h, mem, tabs, p)
        params.append(p)
        saved.append(sv)
    loss_row, dh = _loss_grad(h, target, name="loss_grad")
    grads = [None] * DEPTH
    token = None
    for l in reversed(range(DEPTH)):
        p = params[l]
        if token is not None:
            p = {**p, "norm_ffn_post": p["norm_ffn_post"] + token[:1, :1]}
        dh, grads[l] = _layer_bwd(dh, mem, tabs, p, saved[l])
        if on_grads is not None:
            token = on_grads(l, grads[l], dh)
    return loss_row, dh, grads


ANY = pl.BlockSpec(memory_space=pl.ANY)
MESH = pl.DeviceIdType.MESH


def _flip(pos, mask):
    return tuple(1 - v if m else v for v, m in zip(pos, mask))


def _exchange(ins, out_shapes, remote, local, *, name):
    n_in = len(ins)
    n_out = len(out_shapes)

    def at(ref, idx):
        return ref.at[idx] if idx else ref

    def body(*refs):
        in_refs = refs[:n_in]
        out_refs = refs[n_in:n_in + n_out]
        send_sems, recv_sems, local_sems = refs[n_in + n_out:]
        me = (lax.axis_index("x"), lax.axis_index("y"), lax.axis_index("c"))
        waits = []
        for k, (ii, src_at, oi, dst_at, mask) in enumerate(remote):
            peer = _flip(me, mask)
            pltpu.make_async_remote_copy(
                src_ref=at(in_refs[ii], src_at(me, peer)), dst_ref=at(out_refs[oi], dst_at(me)),
                send_sem=send_sems.at[k], recv_sem=recv_sems.at[k], device_id=peer, device_id_type=MESH).start()
            waits.append(pltpu.make_async_remote_copy(
                src_ref=at(in_refs[ii], src_at(peer, me)), dst_ref=at(out_refs[oi], dst_at(peer)),
                send_sem=send_sems.at[k], recv_sem=recv_sems.at[k], device_id=peer, device_id_type=MESH))
        own = []
        for k, (ii, src_at, oi, dst_at) in enumerate(local):
            cp = pltpu.make_async_copy(at(in_refs[ii], src_at(me)), at(out_refs[oi], dst_at(me)), local_sems.at[k])
            cp.start()
            own.append(cp)
        for w in waits:
            w.wait_send()
            w.wait_recv()
        for cp in own:
            cp.wait()

    return pl.pallas_call(
        body, in_specs=[ANY] * n_in, out_specs=[ANY] * n_out, out_shape=list(out_shapes),
        scratch_shapes=[pltpu.SemaphoreType.DMA((len(remote),)), pltpu.SemaphoreType.DMA((len(remote),)),
                        pltpu.SemaphoreType.DMA((max(len(local), 1),))],
        name=name)(*ins)


HBM = pl.BlockSpec(memory_space=pltpu.HBM)
SEM = pl.BlockSpec(memory_space=pltpu.SEMAPHORE)
SPLIT_EFFECT = pltpu.SideEffectType.DATAFLOW_SIDE_EFFECTING


def _exchange_start(ins, land_shapes, remote, *, name):
    n_in, n_land, n_cp = len(ins), len(land_shapes), len(remote)

    def body(*refs):
        in_refs, land_refs = refs[:n_in], refs[n_in:n_in + n_land]
        send_sems, recv_sems = refs[n_in + n_land:n_in + n_land + 2]
        token = refs[-1]
        me = (lax.axis_index("x"), lax.axis_index("y"), lax.axis_index("c"))
        for k, (ii, src_at, oi, dst_at, mask) in enumerate(remote):
            peer = _flip(me, mask)
            idx_s, idx_d = src_at(me, peer), dst_at(me)
            pltpu.make_async_remote_copy(
                src_ref=in_refs[ii].at[idx_s] if idx_s else in_refs[ii],
                dst_ref=land_refs[oi].at[idx_d] if idx_d else land_refs[oi],
                send_sem=send_sems.at[k], recv_sem=recv_sems.at[k], device_id=peer, device_id_type=MESH).start()
        token[...] = jnp.zeros_like(token)

    buffers = [pltpu.with_memory_space_constraint(a, pltpu.HBM) for a in ins]
    buffers += [pltpu.with_memory_space_constraint(lax.empty(s.shape, s.dtype), pltpu.HBM) for s in land_shapes]
    out = pl.pallas_call(
        body, name=name,
        out_shape=(pltpu.SemaphoreType.DMA((n_cp,)), pltpu.SemaphoreType.DMA((n_cp,)),
                   *[pltpu.HBM(b.shape, b.dtype) for b in buffers], SDS((8, LANE), F32)),
        in_specs=[HBM] * len(buffers),
        out_specs=(SEM, SEM, *[HBM] * len(buffers), pl.BlockSpec(memory_space=pltpu.VMEM)),
        input_output_aliases={i: 2 + i for i in range(len(buffers))},
        compiler_params=pltpu.CompilerParams(has_side_effects=SPLIT_EFFECT))(*buffers)
    return out[0], out[1], list(out[2:2 + n_in]), list(out[2 + n_in:2 + n_in + n_land]), out[-1]


def _exchange_wait(send_sems, recv_sems, ins, lands, remote, after, *, name):
    n_in, n_land = len(ins), len(lands)

    def body(*refs):
        in_refs, land_refs = refs[:n_in], refs[n_in:n_in + n_land]
        send_sems_, recv_sems_ = refs[n_in + n_land:n_in + n_land + 2]
        me = (lax.axis_index("x"), lax.axis_index("y"), lax.axis_index("c"))
        for k, (ii, src_at, oi, dst_at, mask) in enumerate(remote):
            peer = _flip(me, mask)
            idx_s, idx_d = src_at(peer, me), dst_at(peer)
            cp = pltpu.make_async_remote_copy(
                src_ref=in_refs[ii].at[idx_s] if idx_s else in_refs[ii],
                dst_ref=land_refs[oi].at[idx_d] if idx_d else land_refs[oi],
                send_sem=send_sems_.at[k], recv_sem=recv_sems_.at[k], device_id=peer, device_id_type=MESH)
            cp.wait_send()
            cp.wait_recv()

    buffers = list(ins) + list(lands)
    out = pl.pallas_call(
        body, name=name, out_shape=tuple(pltpu.HBM(b.shape, b.dtype) for b in buffers),
        in_specs=[HBM] * len(buffers) + [SEM, SEM, ANY], out_specs=tuple([HBM] * len(buffers)),
        input_output_aliases={i: i for i in range(len(buffers))},
        compiler_params=pltpu.CompilerParams(has_side_effects=SPLIT_EFFECT))(*buffers, send_sems, recv_sems, after)
    return list(out[:n_in]), list(out[n_in:])


def _chip(pos):
    return 2 * pos[0] + pos[1]


XY_MASKS = ((1, 0, 0), (0, 1, 0), (1, 1, 0))
SIBLING = (0, 0, 1)
ALL_MASKS = tuple((a, b, c) for a in (0, 1) for b in (0, 1) for c in (0, 1))[1:]


def _gather_xy(arrs, *, name):
    n = len(arrs)
    outs = [SDS((4,) + a.shape, a.dtype) for a in arrs]
    halves = [a.shape[0] // 2 for a in arrs]

    def body(*refs):
        in_refs, out_refs = refs[:n], refs[n:2 * n]
        ici_send, ici_recv, d2d_send, d2d_recv = refs[2 * n:]
        me = (lax.axis_index("x"), lax.axis_index("y"), lax.axis_index("c"))
        sibling = _flip(me, SIBLING)
        flows = []
        for i in range(n):
            mine = pl.ds(me[2] * halves[i], halves[i])
            other = pl.ds(sibling[2] * halves[i], halves[i])
            for m in XY_MASKS:
                k = len(flows)
                peer = _flip(me, m)

                def remote(src, dst, sems, to, k=k):
                    return pltpu.make_async_remote_copy(src_ref=src, dst_ref=dst, send_sem=sems[0].at[k],
                                                        recv_sem=sems[1].at[k], device_id=to, device_id_type=MESH)

                landed = out_refs[i].at[_chip(peer), mine]
                send = remote(in_refs[i].at[mine], out_refs[i].at[_chip(me), mine], (ici_send, ici_recv), peer)
                send.start()
                arrive = remote(in_refs[i].at[mine], landed, (ici_send, ici_recv), peer)
                forward = remote(landed, landed, (d2d_send, d2d_recv), sibling)
                handed = remote(out_refs[i].at[_chip(peer), other], out_refs[i].at[_chip(peer), other],
                                (d2d_send, d2d_recv), sibling)
                flows.append((send, arrive, forward, handed))
        for _, arrive, forward, _ in flows:
            arrive.wait_recv()
            forward.start()
        for send, _, forward, handed in flows:
            handed.wait_recv()
            send.wait_send()
            forward.wait_send()

    n_flows = 3 * n
    return pl.pallas_call(
        body, in_specs=[ANY] * n, out_specs=[ANY] * n, out_shape=outs,
        scratch_shapes=[pltpu.SemaphoreType.DMA((n_flows,))] * 4, name=name)(*arrs)


def _gather_all(arr, *, name):
    slot = lambda pos: (4 * pos[0] + 2 * pos[1] + pos[2],)
    whole = lambda *_: ()
    remote = [(0, whole, 0, slot, m) for m in ALL_MASKS]
    return _exchange([arr], [SDS((8,) + arr.shape, arr.dtype)], remote, [(0, whole, 0, slot)], name=name)[0]


def _swap_halves(arrs, *, name):
    outs = [SDS(a.shape[:2] + (a.shape[2] // 2, a.shape[3]), a.dtype) for a in arrs]

    def src(i):
        half = arrs[i].shape[2] // 2
        return lambda sender, receiver: (slice(None), slice(None), pl.ds(receiver[2] * half, half))

    whole = lambda *_: ()
    remote = [(i, src(i), i, whole, SIBLING) for i in range(len(arrs))]
    return _exchange(arrs, outs, remote, [], name=name)


def _scatter_xy(arrs, *, name):
    outs = [SDS((len(XY_MASKS),) + a.shape[1:], a.dtype) for a in arrs]
    src = lambda sender, receiver: (_chip(receiver),)
    remote = [(i, src, i, (lambda sender, j=j: (j,)), m) for i in range(len(arrs)) for j, m in enumerate(XY_MASKS)]
    return _exchange(arrs, outs, remote, [], name=name)


def _send_to_sibling(arrs, *, name):
    outs = [SDS(a.shape, a.dtype) for a in arrs]
    whole = lambda *_: ()
    remote = [(i, whole, i, whole, SIBLING) for i in range(len(arrs))]
    return _exchange(arrs, outs, remote, [], name=name)


def _add_half(g, other, core, *, name):
    n4, nl, r, c = g.shape
    half = r // 2
    g3 = g.reshape(n4 * nl, 2, half, c)
    o3 = other.reshape(n4 * nl, half, c)
    tr = _rows_tile(half) if half > 512 else half

    def body(core_ref, g_ref, o_ref, out_ref):
        out_ref[...] = (g_ref[...] + o_ref[...]).astype(BF16)

    return pl.pallas_call(
        body,
        grid_spec=pltpu.PrefetchScalarGridSpec(
            num_scalar_prefetch=1, grid=(n4 * nl, half // tr),
            in_specs=[BS((None, None, tr, c), lambda i, j, core_ref: (i, core_ref[0], j, 0)),
                      BS((None, tr, c), lambda i, j, core_ref: (i, j, 0))],
            out_specs=BS((None, tr, c), lambda i, j, core_ref: (i, j, 0))),
        out_shape=SDS((n4 * nl, half, c), BF16), compiler_params=_cp(), name=name)(core, g3, o3).reshape(n4, nl, half, c)


def _sum_chips(parts, mine, chip, *, name):
    n4, nl, h, c = mine.shape
    tr = _rows_tile(h) if h > 512 else h

    def body(chip_ref, p_ref, own_ref, out_ref):
        me = chip_ref[0]
        own = own_ref[...].astype(F32)
        across = [p_ref[j].astype(F32) for j in range(len(XY_MASKS))]
        t = []
        for s in range(n4):
            rel = s ^ me
            t.append(jnp.where(rel == 0, own, jnp.where(rel == 2, across[0], jnp.where(rel == 1, across[1], across[2]))))
        out_ref[...] = ((t[0] + t[1]) + t[2]) + t[3]

    return pl.pallas_call(
        body,
        grid_spec=pltpu.PrefetchScalarGridSpec(
            num_scalar_prefetch=1, grid=(nl, h // tr),
            in_specs=[BS((len(XY_MASKS), None, tr, c), lambda i, j, chip_ref: (0, i, j, 0)),
                      BS((None, None, tr, c), lambda i, j, chip_ref: (chip_ref[0], i, j, 0))],
            out_specs=BS((None, tr, c), lambda i, j, chip_ref: (i, j, 0))),
        out_shape=SDS((nl, h, c), F32), compiler_params=_cp(), name=name)(chip, parts, mine)


def _sum_devices(parts, *, name):
    n, r, c = parts.shape

    def body(p_ref, out_ref):
        acc = p_ref[0]
        for d in range(1, n):
            acc = acc + p_ref[d]
        out_ref[...] = acc

    return pl.pallas_call(
        body, grid=(1,), in_specs=[BS((n, r, c), lambda i: (0, 0, 0))], out_specs=BS((r, c), lambda i: (0, 0)),
        out_shape=SDS((r, c), F32), compiler_params=_cp(), name=name)(parts)


WEIGHTS = ("norm_mix_pre", "norm_mix_post", "w_in", "conv_short", "conv_gdn", "gdn_a_log", "gdn_dt_bias",
           "gdn_norm", "w_out", "norm_mem", "norm_xattn_pre", "norm_xattn_post", "w_xq", "w_xkv", "w_xo",
           "norm_ffn_pre", "norm_ffn_post", "w_gate_up", "w_down")
COL_SHARDED = ("w_in", "w_xkv", "w_gate_up", "conv_short", "conv_gdn")
ROW_SHARDED = ("w_out", "w_xq", "w_xo", "w_down")
SMALL_SHARDED = ("conv_short", "conv_gdn")
SMALL_ROW_PAD = 8


KEPT_AS_SHARDS = ("w_xkv", "w_gate_up")


def _from_shards(n, g):
    if n in KEPT_AS_SHARDS:
        return g
    if n in COL_SHARDED:
        t = jnp.moveaxis(g, 0, -2)
        t = t.reshape(t.shape[:-2] + (-1,))
        return _w_in_to_padded(t) if n == "w_in" else t
    return g.reshape(-1, g.shape[-1])


def _to_shards(n, g):
    if n in KEPT_AS_SHARDS:
        return g
    if n == "w_in":
        g = _w_in_from_padded(g)
        return jnp.moveaxis(g.reshape(g.shape[0], 4, -1), 1, 0)
    return g.reshape(4, -1, g.shape[-1])


def _pack_small(grads):
    rows = []
    for g in grads:
        for n in WEIGHTS:
            if n not in MATRICES:
                part = g[n].reshape(-1, LANE)
                rows.append(jnp.pad(part, ((0, -part.shape[0] % SMALL_ROW_PAD), (0, 0))))
    return jnp.concatenate(rows, axis=0)


def _unpack_small(packed, like):
    out, at = [], 0
    for _ in range(DEPTH):
        g = {}
        for n in WEIGHTS:
            if n not in MATRICES:
                shape = like[n].shape
                k = math.prod(shape) // LANE
                g[n] = packed[at:at + k].reshape(shape)
                at += k + (-k % SMALL_ROW_PAD)
        out.append(g)
    return out


def kernel(x, mem, positions, norm_mix_pre, norm_mix_post, w_in, conv_short, conv_gdn, gdn_a_log, gdn_dt_bias, gdn_norm, w_out, norm_mem, norm_xattn_pre, norm_xattn_post, w_xq, w_xkv, w_xo, norm_ffn_pre, norm_ffn_post, w_gate_up, w_down, loss_target, m_norm_mix_pre, m_norm_mix_post, m_w_in, m_conv_short, m_conv_gdn, m_gdn_a_log, m_gdn_dt_bias, m_gdn_norm, m_w_out, m_norm_mem, m_norm_xattn_pre, m_norm_xattn_post, m_w_xq, m_w_xkv, m_w_xo, m_norm_ffn_pre, m_norm_ffn_post, m_w_gate_up, m_w_down, v_norm_mix_pre, v_norm_mix_post, v_w_in, v_conv_short, v_conv_gdn, v_gdn_a_log, v_gdn_dt_bias, v_gdn_norm, v_w_out, v_norm_mem, v_norm_xattn_pre, v_norm_xattn_post, v_w_xq, v_w_xkv, v_w_xo, v_norm_ffn_pre, v_norm_ffn_post, v_w_gate_up, v_w_down):
    args = dict(locals())
    w = {n: args[n] for n in WEIGHTS}
    m = {n: args["m_" + n] for n in WEIGHTS}
    v = {n: args["v_" + n] for n in WEIGHTS}
    seq = x.shape[1]
    chip = 2 * lax.axis_index("x") + lax.axis_index("y")
    core = lax.axis_index("c").astype(jnp.int32).reshape(1)

    def cast(n, l):
        return w[n][l].astype(BF16)

    first = [cast(n, 0) for n in MATRICES] + [w[n] for n in SMALL_SHARDED]
    blocks = _gather_xy(first, name="gather_weights")
    blocks = [lax.dynamic_update_index_in_dim(b, o, chip, axis=0) for b, o in zip(blocks, first)]
    full = {n: _from_shards(n, b) for n, b in zip(list(MATRICES) + list(SMALL_SHARDED), blocks)}
    layer0 = {n: full.pop(n) for n in MATRICES}
    for n in WEIGHTS:
        if n not in full and n not in MATRICES:
            full[n] = w[n]
    whole = lambda *_: ()
    gather_plan = [(i, whole, i, (lambda sender: (_chip(sender),)), mask)
                   for i in range(len(MATRICES)) for mask in XY_MASKS]
    in_flight = {}
    for l in range(1, DEPTH):
        own = [cast(n, l) for n in MATRICES]
        in_flight[l] = (own,) + _exchange_start(own, [SDS((4,) + o.shape, o.dtype) for o in own], gather_plan,
                                                name=f"gather_start_{l}")[:4]

    def matrices_for(l, h):
        if l == 0:
            return layer0
        own, send_sems, recv_sems, thru, lands = in_flight.pop(l)
        _, lands = _exchange_wait(send_sems, recv_sems, thru, lands, gather_plan, h, name=f"gather_wait_{l}")
        lands = [lax.dynamic_update_index_in_dim(b, o, chip, axis=0) for b, o in zip(lands, own)]
        return {n: _from_shards(n, b) for n, b in zip(MATRICES, lands)}

    chip1 = chip.astype(jnp.int32).reshape(1)
    scatter_plan = [(i, (lambda sender, receiver: (_chip(receiver),)), i, (lambda sender, j=j: (j,)), mask)
                    for i in range(len(MATRICES)) for j, mask in enumerate(XY_MASKS)]
    reduced = {}
    sending = []

    def finish_scatter(after):
        l, send_sems, recv_sems, pair, lands = sending.pop()
        pair, parts = _exchange_wait(send_sems, recv_sems, pair, lands, scatter_plan, after, name=f"scatter_wait_{l}")
        reduced[l] = [_sum_chips(p, pr, chip1, name="grads_chip_sum") for p, pr in zip(parts, pair)]

    def on_grads(l, g, dh):
        if sending:
            finish_scatter(dh)
        mine = [_to_shards(n, g[n])[:, None] for n in MATRICES]
        theirs = _swap_halves(mine, name="grads_swap_halves")
        pair = [_add_half(a, b, core, name="grads_pair_sum") for a, b in zip(mine, theirs)]
        lands = [SDS((len(XY_MASKS),) + p.shape[1:], p.dtype) for p in pair]
        send_sems, recv_sems, pair, lands, token = _exchange_start(pair, lands, scatter_plan, name=f"scatter_start_{l}")
        sending.append((l, send_sems, recv_sems, pair, lands))
        return token

    loss_row, dx, grads = _local_step(x[0], mem[0], positions.reshape(seq, 1), loss_target[0], full,
                                      matrices_for=matrices_for, on_grads=on_grads)
    finish_scatter(dx)
    keys = [(n, l) for l in range(DEPTH) for n in MATRICES]
    halves = [r for l in range(DEPTH) for r in reduced[l]]
    others = _send_to_sibling(halves, name="grads_share_halves")
    south = lax.axis_index("c") == 0
    grad = {}
    for n in MATRICES:
        layers = []
        for (n2, _), a, b in zip(keys, halves, others):
            if n2 == n:
                layers += [jnp.where(south, a, b), jnp.where(south, b, a)]
        grad[n] = jnp.concatenate(layers, axis=0).reshape(w[n].shape)

    packed = _pack_small(grads)
    total = _sum_devices(_gather_all(packed, name="small_grads_gather"), name="small_grads_sum")
    small = _unpack_small(total, grads[0])
    for n in WEIGHTS:
        if n in MATRICES:
            continue
        g = jnp.stack([s[n] for s in small])
        if n in ("gdn_a_log", "gdn_dt_bias"):
            g = g[:, 0, :GDN_HEADS]
        elif n in SMALL_SHARDED:
            width = w[n].shape[-1]
            g = lax.dynamic_slice_in_dim(g, chip * width, width, axis=2)
        grad[n] = g.reshape(w[n].shape)

    delta, new_m, new_v = {}, {}, {}
    for n in WEIGHTS:
        shape = w[n].shape
        two_d = (-1, shape[-1])
        d, nm, nv = _adamw(w[n].reshape(two_d), grad[n].reshape(two_d), m[n].reshape(two_d), v[n].reshape(two_d),
                           name="adamw_" + n)
        delta[n], new_m[n], new_v[n] = d.reshape(shape), nm.reshape(shape), nv.reshape(shape)

    loss = lax.psum(loss_row[0, 0], ("x", "y", "c"))
    return (loss, dx.reshape(x.shape), *[grad[n] for n in WEIGHTS], *[delta[n] for n in WEIGHTS],
            *[new_m[n] for n in WEIGHTS], *[new_v[n] for n in WEIGHTS])
```

```python
import functools
import math

import jax
import jax.numpy as jnp
from jax import lax
from jax.experimental import pallas as pl
from jax.experimental.pallas import tpu as pltpu

F32 = jnp.float32
BF16 = jnp.bfloat16
BS = pl.BlockSpec
SDS = jax.ShapeDtypeStruct
PRECISE = lax.Precision.HIGH

D_MODEL = 1024
DEPTH = 4
EPS = 1e-6
ATTN_HEADS = 4
ATTN_HEAD_DIM = 64
ATTN_WIDTH = 256
ROPE_THETA = 500000.0
ROPE_DIM = 16
CONV_WIDTH = 256
CONV_K = 3
GDN_HEADS = 4
GDN_HEAD_DIM = 128
GDN_WIDTH = 512
GDN_CONV_K = 4
GDN_CHUNK = 64
IN_WIDTH = 3592
XATTN_HEADS = 4
XATTN_HEAD_DIM = 256
FFN_HIDDEN = 2816
ADAM_LR = 0.001
ADAM_B1 = 0.9
ADAM_B2 = 0.999
ADAM_EPS = 1e-08
ADAM_WD = 0.01
ADAM_STEP = 10

IN_PAD = 3840
COL_GDN = 1536
COL_GATE = 3072
COL_AB = 3584

VMEM_LIMIT_V7X = 56 * 1024 * 1024
LANE = 128


def _cp(**kw):
    return pltpu.CompilerParams(vmem_limit_bytes=VMEM_LIMIT_V7X, **kw)


def _tile(n, cap):
    if n <= cap:
        return n
    best = None
    for t in range(LANE, cap + 1, LANE):
        if n % t == 0:
            best = t
    assert best is not None, (n, cap)
    return best


def _dot(a, b, ca, cb, precise=False):
    dims = (((ca,), (cb,)), ((), ()))
    if precise:
        return lax.dot_general(a.astype(F32), b.astype(F32), dims, precision=PRECISE,
                               preferred_element_type=F32)
    return lax.dot_general(a.astype(BF16), b.astype(BF16), dims, preferred_element_type=F32)


def _sigmoid(x):
    return 1.0 / (1.0 + jnp.exp(-x))


MM_ROWS = 1408
MM_BLOCK_BYTES = 6 * 1024 * 1024
MM_A_BYTES = 8 * 1024 * 1024


def _mm_tn(width, k, itemsize):
    if k * width * itemsize <= MM_BLOCK_BYTES:
        return width
    return _tile(width, max(LANE, min(1024, MM_BLOCK_BYTES // (k * itemsize) // LANE * LANE)))


def _mm(a, b, *, ta=False, tb=False, out_dtype=F32, b_shards=False, out_shards=False, halves=False, name):
    if halves and tb:
        m, k = a.shape[1], 2 * a.shape[2]
    else:
        m, k = (a.shape[1], a.shape[0]) if ta else a.shape
    tm = _tile(m, MM_ROWS)
    ca = 0 if ta else 1

    if b_shards and tb:
        ns, n, c = b.shape
        assert k == ns * c and not ta
        tn = _tile(n, max(LANE, min(1024, MM_BLOCK_BYTES // (k * b.dtype.itemsize) // LANE * LANE)))

        def a_block(a_ref, s):
            if halves:
                per = ns // 2
                return a_ref[s // per, :, (s % per) * c:(s % per + 1) * c]
            return a_ref[:, s * c:(s + 1) * c]

        def body(a_ref, b_ref, o_ref):
            acc = _dot(a_block(a_ref, 0), b_ref[0], 1, 1)
            for s in range(1, ns):
                acc = acc + _dot(a_block(a_ref, s), b_ref[s], 1, 1)
            o_ref[...] = acc.astype(out_dtype)

        b_spec = BS((ns, tn, c), lambda i, j: (0, j, 0))
    else:
        if b_shards:
            ns, kb, c = b.shape
            n = ns * c
            tn = _mm_tn(c, k, b.dtype.itemsize)
            nb = c // tn
            b_spec = BS((None, k, tn), lambda i, j: (j // nb, 0, j % nb))
        elif halves:
            kb, n = b.shape[1], 2 * b.shape[2]
            c = n // 4
            tn = _mm_tn(c, k, b.dtype.itemsize)
            nb = c // tn
            b_spec = BS((None, k, tn), lambda i, j: (j // (2 * nb), 0, j % (2 * nb)))
        else:
            kb, n = (b.shape[1], b.shape[0]) if tb else b.shape
            c = n // 4 if out_shards else n
            tn = _mm_tn(c, k, b.dtype.itemsize)
            nb = c // tn
            b_spec = BS((tn, k), lambda i, j: (j, 0)) if tb else BS((k, tn), lambda i, j: (0, j))
        assert kb == k
        cb = 1 if tb else 0

        def body(a_ref, b_ref, o_ref):
            o_ref[...] = _dot(a_ref[...], b_ref[...], ca, cb).astype(out_dtype)

    out_bytes = jnp.dtype(out_dtype).itemsize
    while (tm > 256 and tm % 256 == 0 and
           (tm * k * a.dtype.itemsize > MM_A_BYTES or tm * tn * out_bytes > MM_BLOCK_BYTES)):
        tm //= 2
    if halves and tb:
        a_spec = BS((2, tm, k // 2), lambda i, j: (0, i, 0))
    else:
        a_spec = BS((k, tm), lambda i, j: (0, i)) if ta else BS((tm, k), lambda i, j: (i, 0))
    if out_shards:
        out_spec = BS((None, tm, tn), lambda i, j: (j // nb, i, j % nb))
        out_shape = SDS((4, m, n // 4), out_dtype)
    elif halves and not (ta or tb):
        out_spec = BS((None, tm, tn), lambda i, j: (j // (2 * nb), i, j % (2 * nb)))
        out_shape = SDS((2, m, n // 2), out_dtype)
    else:
        out_spec = BS((tm, tn), lambda i, j: (i, j))
        out_shape = SDS((m, n), out_dtype)
    return pl.pallas_call(
        body, grid=(m // tm, n // tn), in_specs=[a_spec, b_spec], out_specs=out_spec, out_shape=out_shape,
        compiler_params=_cp(), name=name)(a, b)


def _rmsnorm(x, w, *, name):
    r, d = x.shape
    tr = _tile(r, 512)

    def body(x_ref, w_ref, o_ref):
        xv = x_ref[...]
        rs = lax.rsqrt(jnp.mean(xv * xv, axis=-1, keepdims=True) + EPS)
        o_ref[...] = (xv * rs * w_ref[...]).astype(BF16)

    return pl.pallas_call(
        body, grid=(r // tr,), in_specs=[BS((tr, d), lambda i: (i, 0)), BS((1, d), lambda i: (0, 0))],
        out_specs=BS((tr, d), lambda i: (i, 0)), out_shape=SDS((r, d), BF16),
        compiler_params=_cp(), name=name)(x, w)


def _resnorm(h, m, w, *, name):
    r, d = h.shape
    tr = _tile(r, 512)

    def body(h_ref, m_ref, w_ref, o_ref):
        mv = m_ref[...]
        rs = lax.rsqrt(jnp.mean(mv * mv, axis=-1, keepdims=True) + EPS)
        o_ref[...] = h_ref[...] + mv * rs * w_ref[...]

    row = BS((tr, d), lambda i: (i, 0))
    return pl.pallas_call(
        body, grid=(r // tr,), in_specs=[row, row, BS((1, d), lambda i: (0, 0))],
        out_specs=row, out_shape=SDS((r, d), F32), compiler_params=_cp(), name=name)(h, m, w)


def _rmsnorm_bwd(x, w, dy, res=None, *, name):
    r, d = x.shape
    tr = _tile(r, 512)
    has_res = res is not None

    def body(*refs):
        if has_res:
            x_ref, w_ref, dy_ref, res_ref, dx_ref, dw_ref = refs
        else:
            x_ref, w_ref, dy_ref, dx_ref, dw_ref = refs
        xv = x_ref[...]
        dyv = dy_ref[...].astype(F32)
        rs = lax.rsqrt(jnp.mean(xv * xv, axis=-1, keepdims=True) + EPS)
        nv = xv * rs
        dyw = dyv * w_ref[...]
        dx = rs * (dyw - nv * jnp.mean(dyw * nv, axis=-1, keepdims=True))
        if has_res:
            dx = dx + res_ref[...]
        dx_ref[...] = dx

        @pl.when(pl.program_id(0) == 0)
        def _():
            dw_ref[...] = jnp.zeros_like(dw_ref)

        dw_ref[...] += jnp.sum(dyv * nv, axis=0, keepdims=True)

    row = BS((tr, d), lambda i: (i, 0))
    vec = BS((1, d), lambda i: (0, 0))
    ins = [x, w, dy] + ([res] if has_res else [])
    return pl.pallas_call(
        body, grid=(r // tr,), in_specs=[row, vec, row] + ([row] if has_res else []),
        out_specs=[row, vec], out_shape=[SDS((r, d), F32), SDS((1, d), F32)],
        compiler_params=_cp(), name=name)(*ins)


def _swiglu(gu, *, name):
    _, r, hid = gu.shape
    tr, tc = _tile(r, 512), _tile(hid, 1408)

    def body(gu_ref, o_ref):
        g = gu_ref[0]
        o_ref[...] = (g * _sigmoid(g) * gu_ref[1]).astype(BF16)

    return pl.pallas_call(
        body, grid=(r // tr, hid // tc), in_specs=[BS((2, tr, tc), lambda i, j: (0, i, j))],
        out_specs=BS((tr, tc), lambda i, j: (i, j)), out_shape=SDS((r, hid), BF16),
        compiler_params=_cp(), name=name)(gu)


def _swiglu_bwd(gu, dact, *, name):
    _, r, hid = gu.shape
    tr, tc = _tile(r, 512), _tile(hid, 1408)

    def body(gu_ref, d_ref, o_ref):
        g = gu_ref[0]
        da = d_ref[...]
        sg = _sigmoid(g)
        o_ref[0] = (da * gu_ref[1] * sg * (1.0 + g * (1.0 - sg))).astype(BF16)
        o_ref[1] = (da * g * sg).astype(BF16)

    blk = BS((2, tr, tc), lambda i, j: (0, i, j))
    return pl.pallas_call(
        body, grid=(r // tr, hid // tc), in_specs=[blk, BS((tr, tc), lambda i, j: (i, j))],
        out_specs=blk, out_shape=SDS((2, r, hid), BF16), compiler_params=_cp(), name=name)(gu, dact)


def _loss_grad(h, target, *, name):
    r, d = h.shape
    tr = _tile(r, 512)

    def body(h_ref, t_ref, l_ref, g_ref):
        e = h_ref[...] - t_ref[...]
        g_ref[...] = e * (1.0 / d)

        @pl.when(pl.program_id(0) == 0)
        def _():
            l_ref[...] = jnp.zeros_like(l_ref)

        l_ref[...] += jnp.full((1, LANE), 0.5 / d, F32) * jnp.sum(e * e)

    row = BS((tr, d), lambda i: (i, 0))
    return pl.pallas_call(
        body, grid=(r // tr,), in_specs=[row, row],
        out_specs=[BS((1, LANE), lambda i: (0, 0)), row],
        out_shape=[SDS((1, LANE), F32), SDS((r, d), F32)], compiler_params=_cp(), name=name)(h, target)


def _adamw(w, g, m, v, *, name):
    r, c = w.shape
    tr = r if r <= 512 else _rows_tile(r)
    bc1 = 1.0 - ADAM_B1 ** ADAM_STEP
    bc2 = 1.0 - ADAM_B2 ** ADAM_STEP

    def body(w_ref, g_ref, m_ref, v_ref, d_ref, nm_ref, nv_ref):
        gv = g_ref[...]
        nm = ADAM_B1 * m_ref[...] + (1.0 - ADAM_B1) * gv
        nv = ADAM_B2 * v_ref[...] + (1.0 - ADAM_B2) * (gv * gv)
        d_ref[...] = -ADAM_LR * ((nm / bc1) / (jnp.sqrt(nv / bc2) + ADAM_EPS) + ADAM_WD * w_ref[...])
        nm_ref[...] = nm
        nv_ref[...] = nv

    blk = BS((tr, c), lambda i: (i, 0))
    return pl.pallas_call(
        body, grid=(r // tr,), in_specs=[blk] * 4, out_specs=[blk] * 3,
        out_shape=[SDS((r, c), F32)] * 3, compiler_params=_cp(), name=name)(w, g, m, v)


def _rows_tile(r):
    for t in (512, 256, 128, 64, 32, 16, 8):
        if r % t == 0:
            return t
    return r


def _rope_tables(pos, *, name):
    s = pos.shape[0]
    half = ROPE_DIM // 2

    def body(p_ref, c_ref, a_ref, b_ref):
        lane = lax.broadcasted_iota(jnp.int32, (s, ATTN_WIDTH), 1) & (ATTN_HEAD_DIM - 1)
        fi = (lane & (half - 1)).astype(F32)
        inv_freq = jnp.exp(fi * (-2.0 * math.log(ROPE_THETA) / ROPE_DIM))
        ang = p_ref[...].astype(F32) * inv_freq
        cs, sn = jnp.cos(ang), jnp.sin(ang)
        c_ref[...] = jnp.where(lane < ROPE_DIM, cs, 1.0)
        a_ref[...] = jnp.where(lane < half, -sn, 0.0)
        b_ref[...] = jnp.where((lane >= half) & (lane < ROPE_DIM), sn, 0.0)

    full = BS((s, ATTN_WIDTH), lambda i: (0, 0))
    return pl.pallas_call(
        body, grid=(1,), in_specs=[BS((s, 1), lambda i: (0, 0))], out_specs=[full] * 3,
        out_shape=[SDS((s, ATTN_WIDTH), F32)] * 3, compiler_params=_cp(), name=name)(pos)


def _rot(x, c, a, b):
    w = x.shape[1]
    return x * c + pltpu.roll(x, w - ROPE_DIM // 2, 1) * a + pltpu.roll(x, ROPE_DIM // 2, 1) * b


def _rot_t(dy, c, a, b):
    w = dy.shape[1]
    return dy * c + pltpu.roll(dy * a, ROPE_DIM // 2, 1) + pltpu.roll(dy * b, w - ROPE_DIM // 2, 1)


def _attn_count(q0, tq, s):
    dist = (lax.broadcasted_iota(jnp.int32, (tq, s), 0) + q0) - lax.broadcasted_iota(jnp.int32, (tq, s), 1)
    cnt = ((dist <= 128).astype(F32) + (((dist & 3) == 0) & (dist <= 512)).astype(F32)
           + ((dist & 15) == 0).astype(F32))
    return jnp.where(dist >= 0, cnt, 0.0)


ATTN_TQ = 256


def _attn_specs(s, tq):
    def qblk(col):
        return BS((tq, ATTN_WIDTH), lambda i: (i, col))

    def full(col):
        return BS((s, ATTN_WIDTH), lambda i: (0, col))

    return qblk, full


ATTN_TK = 512


def _attn_chunk(i, c, tq, k_ref, v_ref, ck, ak, bk):
    ks = pl.ds(pl.multiple_of(c * ATTN_TK, ATTN_TK), ATTN_TK)
    k = _rot(k_ref[ks, :], ck[ks, :], ak[ks, :], bk[ks, :]).astype(BF16)
    v = v_ref[ks, :].astype(BF16)
    cnt = _attn_count(i * tq - c * ATTN_TK, tq, ATTN_TK)
    return ks, k, v, cnt


def _attn_flash_fwd(proj, tabs, *, name):
    s = proj.shape[0]
    tq = ATTN_TQ
    qblk, full = _attn_specs(s, tq)
    scale = ATTN_HEAD_DIM ** -0.5
    nh = ATTN_HEADS

    def body(q_ref, k_ref, v_ref, cq, aq, bq, ck, ak, bk, o_ref, lse_ref):
        i = pl.program_id(0)
        q = _rot(q_ref[...], cq[...], aq[...], bq[...]) * scale
        head = lax.broadcasted_iota(jnp.int32, (1, ATTN_WIDTH), 1) >> 6
        hms = [(head == h).astype(F32) for h in range(nh)]
        qms = [(q * hm).astype(BF16) for hm in hms]

        def step(c, carry):
            ms, ls, acc = carry
            _, k, v, cnt = _attn_chunk(i, c, tq, k_ref, v_ref, ck, ak, bk)
            valid = cnt > 0.0
            new_ms, new_ls = [], []
            scale_acc = jnp.zeros((tq, ATTN_WIDTH), F32)
            add = jnp.zeros((tq, ATTN_WIDTH), F32)
            for h in range(nh):
                sc = _dot(qms[h], k, 1, 1)
                m_new = jnp.maximum(ms[h], jnp.max(jnp.where(valid, sc, -1e30), axis=-1, keepdims=True))
                alpha = jnp.exp(ms[h] - m_new)
                p = cnt * jnp.exp(jnp.minimum(sc - m_new, 0.0))
                new_ms.append(m_new)
                new_ls.append(alpha * ls[h] + jnp.sum(p, axis=-1, keepdims=True))
                scale_acc = scale_acc + alpha * hms[h]
                add = add + _dot(p, v, 1, 0) * hms[h]
            return new_ms, new_ls, acc * scale_acc + add

        init = ([jnp.full((tq, 1), -1e30, F32)] * nh, [jnp.zeros((tq, 1), F32)] * nh,
                jnp.zeros((tq, ATTN_WIDTH), F32))
        ms, ls, acc = lax.fori_loop(0, i // (ATTN_TK // tq) + 1, step, init)
        inv = jnp.zeros((tq, ATTN_WIDTH), F32)
        lane = lax.broadcasted_iota(jnp.int32, (tq, LANE), 1)
        lse = jnp.zeros((tq, LANE), F32)
        for h in range(nh):
            inv = inv + (1.0 / ls[h]) * hms[h]
            lse = jnp.where(lane == h, ms[h] + jnp.log(ls[h]), lse)
        o_ref[...] = (acc * inv).astype(BF16)
        lse_ref[...] = lse

    return pl.pallas_call(
        body, grid=(s // tq,),
        in_specs=[qblk(0), full(1), full(2), qblk(0), qblk(0), qblk(0), full(0), full(0), full(0)],
        out_specs=[BS((tq, ATTN_WIDTH), lambda i: (i, 0)), BS((tq, LANE), lambda i: (i, 0))],
        out_shape=[SDS((s, ATTN_WIDTH), BF16), SDS((s, LANE), F32)],
        compiler_params=_cp(), name=name)(proj, proj, proj, *tabs, *tabs)


def _attn_flash_bwd(proj, tabs, cat, lse, dcat, *, name):
    s = proj.shape[0]
    tq = ATTN_TQ
    nq = s // tq
    qblk, full = _attn_specs(s, tq)
    scale = ATTN_HEAD_DIM ** -0.5
    nh = ATTN_HEADS

    def body(q_ref, k_ref, v_ref, cq, aq, bq, ck, ak, bk, y_ref, lse_ref, dy_ref,
             dq_ref, dk_ref, dv_ref, dk_acc, dv_acc):
        i = pl.program_id(0)

        @pl.when(i == 0)
        def _():
            dk_acc[...] = jnp.zeros_like(dk_acc)
            dv_acc[...] = jnp.zeros_like(dv_acc)

        q = _rot(q_ref[...], cq[...], aq[...], bq[...]) * scale
        dy = dy_ref[...].astype(F32)
        prod = dy * y_ref[...].astype(F32)
        lse_all = lse_ref[...]
        head = lax.broadcasted_iota(jnp.int32, (1, ATTN_WIDTH), 1) >> 6
        hms = [(head == h).astype(F32) for h in range(nh)]
        qms = [(q * hm).astype(BF16) for hm in hms]
        dyms = [(dy * hm).astype(BF16) for hm in hms]
        deltas = [jnp.sum(prod * hm, axis=-1, keepdims=True) for hm in hms]
        lses = [lse_all[:, h:h + 1] for h in range(nh)]

        def step(c, dq):
            ks, k, v, cnt = _attn_chunk(i, c, tq, k_ref, v_ref, ck, ak, bk)
            dk_c = jnp.zeros((ATTN_TK, ATTN_WIDTH), F32)
            dv_c = jnp.zeros((ATTN_TK, ATTN_WIDTH), F32)
            for h in range(nh):
                sc = _dot(qms[h], k, 1, 1)
                p = cnt * jnp.exp(jnp.minimum(sc - lses[h], 0.0))
                dp = _dot(dyms[h], v, 1, 1)
                ds = p * (dp - deltas[h])
                dq = dq + _dot(ds, k, 1, 0) * hms[h]
                dk_c = dk_c + _dot(ds, qms[h], 0, 0)
                dv_c = dv_c + _dot(p, dyms[h], 0, 0)
            dk_acc[ks, :] += dk_c
            dv_acc[ks, :] += dv_c
            return dq

        dq = lax.fori_loop(0, i // (ATTN_TK // tq) + 1, step, jnp.zeros((tq, ATTN_WIDTH), F32))
        dq_ref[...] = _rot_t(dq * scale, cq[...], aq[...], bq[...])

        @pl.when(i == nq - 1)
        def _():
            dk_ref[...] = _rot_t(dk_acc[...], ck[...], ak[...], bk[...])
            dv_ref[...] = dv_acc[...]

    whole = BS((s, ATTN_WIDTH), lambda i: (0, 0))
    return pl.pallas_call(
        body, grid=(nq,),
        in_specs=[qblk(0), full(1), full(2), qblk(0), qblk(0), qblk(0), full(0), full(0), full(0),
                  qblk(0), BS((tq, LANE), lambda i: (i, 0)), qblk(0)],
        out_specs=[BS((tq, ATTN_WIDTH), lambda i: (i, 0)), whole, whole],
        out_shape=[SDS((s, ATTN_WIDTH), F32)] * 3,
        scratch_shapes=[pltpu.VMEM((s, ATTN_WIDTH), F32), pltpu.VMEM((s, ATTN_WIDTH), F32)],
        compiler_params=_cp(), name=name)(proj, proj, proj, *tabs, *tabs, cat, lse, dcat)


def _shift_down(x, n):
    if n == 0:
        return x
    rows = lax.broadcasted_iota(jnp.int32, x.shape, 0)
    return jnp.where(rows >= n, pltpu.roll(x, n, 0), 0.0)


def _shift_up(x, n):
    if n == 0:
        return x
    t = x.shape[0]
    rows = lax.broadcasted_iota(jnp.int32, x.shape, 0)
    return jnp.where(rows < t - n, pltpu.roll(x, t - n, 0), 0.0)


def _conv_fwd(z, w, kk):
    y = z * w[kk - 1:kk, :]
    for j in range(kk - 1):
        y = y + _shift_down(z, kk - 1 - j) * w[j:j + 1, :]
    return y


def _conv_bwd(z, dy, w, kk):
    dz = dy * w[kk - 1:kk, :]
    dws = []
    for j in range(kk - 1):
        dz = dz + _shift_up(dy, kk - 1 - j) * w[j:j + 1, :]
        dws.append(jnp.sum(dy * _shift_down(z, kk - 1 - j), axis=0, keepdims=True))
    dws.append(jnp.sum(dy * z, axis=0, keepdims=True))
    return dz, jnp.concatenate(dws, axis=0)


def _sconv_fwd(proj, w, *, name):
    s = proj.shape[0]

    def body(b_ref, c_ref, x_ref, w_ref, o_ref):
        y = _conv_fwd(c_ref[...] * x_ref[...], w_ref[...], CONV_K)
        o_ref[...] = (b_ref[...] * y).astype(BF16)

    def col(j):
        return BS((s, LANE), lambda i: (0, j + i))

    return pl.pallas_call(
        body, grid=(CONV_WIDTH // LANE,), in_specs=[col(6), col(8), col(10), BS((CONV_K, LANE), lambda i: (0, i))],
        out_specs=BS((s, LANE), lambda i: (0, i)), out_shape=SDS((s, CONV_WIDTH), BF16),
        compiler_params=_cp(), name=name)(proj, proj, proj, w)


def _sconv_bwd(proj, w, dcat, *, name):
    s = proj.shape[0]

    def body(b_ref, c_ref, x_ref, w_ref, dy_ref, db_ref, dc_ref, dx_ref, dw_ref):
        cv, xv, wv = c_ref[...], x_ref[...], w_ref[...]
        dy = dy_ref[...].astype(F32)
        z = cv * xv
        db_ref[...] = dy * _conv_fwd(z, wv, CONV_K)
        dz, dw = _conv_bwd(z, dy * b_ref[...], wv, CONV_K)
        dc_ref[...] = dz * xv
        dx_ref[...] = dz * cv
        dw_ref[...] = dw

    def col(j):
        return BS((s, LANE), lambda i: (0, j + i))

    out = BS((s, LANE), lambda i: (0, i))
    wspec = BS((CONV_K, LANE), lambda i: (0, i))
    return pl.pallas_call(
        body, grid=(CONV_WIDTH // LANE,), in_specs=[col(6), col(8), col(10), wspec, col(2)],
        out_specs=[out, out, out, wspec],
        out_shape=[SDS((s, CONV_WIDTH), F32)] * 3 + [SDS((CONV_K, CONV_WIDTH), F32)],
        compiler_params=_cp(), name=name)(proj, proj, proj, w, dcat)


def _l2n(y, scale):
    r = lax.rsqrt(jnp.sum(y * y, axis=-1, keepdims=True) + EPS)
    return y * r * scale, r


def _gdn_pre_fwd(proj, w, *, name):
    s = proj.shape[0]
    nh = GDN_HEADS

    def body(x_ref, w_ref, o_ref):
        j = pl.program_id(0)
        c = _conv_fwd(x_ref[...], w_ref[...], GDN_CONV_K)
        y = c * _sigmoid(c)
        scale = jnp.where(j < nh, GDN_HEAD_DIM ** -0.5, 1.0)
        n, _ = _l2n(y, scale)
        o_ref[...] = jnp.where(j < 2 * nh, n, y)

    return pl.pallas_call(
        body, grid=(3 * nh,),
        in_specs=[BS((s, LANE), lambda j: (0, COL_GDN // LANE + j)), BS((GDN_CONV_K, LANE), lambda j: (0, j))],
        out_specs=BS((s, LANE), lambda j: (0, j)), out_shape=SDS((s, 3 * GDN_WIDTH), F32),
        compiler_params=_cp(), name=name)(proj, w)


def _gdn_pre_bwd(proj, w, dqkv, *, name):
    s = proj.shape[0]
    nh = GDN_HEADS

    def body(x_ref, w_ref, d_ref, dx_ref, dw_ref):
        j = pl.program_id(0)
        xv, wv, dn = x_ref[...], w_ref[...], d_ref[...]
        c = _conv_fwd(xv, wv, GDN_CONV_K)
        sg = _sigmoid(c)
        y = c * sg
        scale = jnp.where(j < nh, GDN_HEAD_DIM ** -0.5, 1.0)
        n, r = _l2n(y, 1.0)
        dns = dn * scale
        dy_norm = r * (dns - n * jnp.sum(dns * n, axis=-1, keepdims=True))
        dy = jnp.where(j < 2 * nh, dy_norm, dn)
        dc = dy * sg * (1.0 + c * (1.0 - sg))
        dx, dw = _conv_bwd(xv, dc, wv, GDN_CONV_K)
        dx_ref[...] = dx
        dw_ref[...] = dw

    wspec = BS((GDN_CONV_K, LANE), lambda j: (0, j))
    blk = BS((s, LANE), lambda j: (0, j))
    return pl.pallas_call(
        body, grid=(3 * nh,),
        in_specs=[BS((s, LANE), lambda j: (0, COL_GDN // LANE + j)), wspec, blk],
        out_specs=[blk, wspec], out_shape=[SDS((s, 3 * GDN_WIDTH), F32), SDS((GDN_CONV_K, 3 * GDN_WIDTH), F32)],
        compiler_params=_cp(), name=name)(proj, w, dqkv)


def _softplus(x):
    return jnp.maximum(x, 0.0) + jnp.log(1.0 + jnp.exp(-jnp.abs(x)))


def _gdn_gates_fwd(proj, a_log, dt_bias, *, name):
    s = proj.shape[0]

    def body(x_ref, al_ref, dt_ref, o_ref):
        xv = x_ref[...]
        lane = lax.broadcasted_iota(jnp.int32, xv.shape, 1)
        g = -jnp.exp(al_ref[...]) * _softplus(xv + dt_ref[...])
        o_ref[...] = jnp.where(lane < GDN_HEADS, g, jnp.where(lane < 2 * GDN_HEADS, _sigmoid(xv), 0.0))

    vec = BS((1, LANE), lambda i: (0, 0))
    return pl.pallas_call(
        body, grid=(1,), in_specs=[BS((s, LANE), lambda i: (0, COL_AB // LANE)), vec, vec],
        out_specs=BS((s, LANE), lambda i: (0, 0)), out_shape=SDS((s, LANE), F32),
        compiler_params=_cp(), name=name)(proj, a_log, dt_bias)


def _gdn_gates_bwd(proj, a_log, dt_bias, dgb, *, name):
    s = proj.shape[0]

    def body(x_ref, al_ref, dt_ref, d_ref, dx_ref, dal_ref, ddt_ref):
        xv, dv = x_ref[...], d_ref[...]
        lane = lax.broadcasted_iota(jnp.int32, xv.shape, 1)
        is_g = lane < GDN_HEADS
        ea = -jnp.exp(al_ref[...])
        z = xv + dt_ref[...]
        da = jnp.where(is_g, dv * ea * _sigmoid(z), 0.0)
        beta = _sigmoid(xv)
        dx_ref[...] = jnp.where(is_g, da, jnp.where(lane < 2 * GDN_HEADS, dv * beta * (1.0 - beta), 0.0))
        dal_ref[...] = jnp.sum(jnp.where(is_g, dv * ea * _softplus(z), 0.0), axis=0, keepdims=True)
        ddt_ref[...] = jnp.sum(da, axis=0, keepdims=True)

    vec = BS((1, LANE), lambda i: (0, 0))
    blk = BS((s, LANE), lambda i: (0, 0))
    return pl.pallas_call(
        body, grid=(1,), in_specs=[BS((s, LANE), lambda i: (0, COL_AB // LANE)), vec, vec, blk],
        out_specs=[blk, vec, vec], out_shape=[SDS((s, LANE), F32), SDS((1, LANE), F32), SDS((1, LANE), F32)],
        compiler_params=_cp(), name=name)(proj, a_log, dt_bias, dgb)


def _col_to_row(col, eye):
    return jnp.sum(jnp.where(eye, col, 0.0), axis=0, keepdims=True)


def _row_to_col(row, eye):
    return jnp.sum(jnp.where(eye, row, 0.0), axis=1, keepdims=True)


GDN_GROUP = 4
TRI_BLOCK_SHIFT = 4


def _gdn_masks(c):
    row = lax.broadcasted_iota(jnp.int32, (c, c), 0)
    col = lax.broadcasted_iota(jnp.int32, (c, c), 1)
    return dict(row=row, col=col, eye=row == col, low=row >= col, strict=row > col, upper=row <= col,
                on_diag=(row >> TRI_BLOCK_SHIFT) == (col >> TRI_BLOCK_SHIFT))


def _tri_inv(a_list, mk):
    eye_f = mk["eye"].astype(F32)
    ds = [jnp.where(mk["on_diag"], a, 0.0) for a in a_list]
    xs = [eye_f - d for d in ds]
    ps = ds
    for _ in range(3):
        ps = [_dot(p, p, 1, 0, precise=True) for p in ps]
        xs = [x + _dot(x, p, 1, 0, precise=True) for x, p in zip(xs, ps)]
    ms = [_dot(x, a - d, 1, 0, precise=True) for x, a, d in zip(xs, a_list, ds)]
    m2s = [_dot(m, m, 1, 0, precise=True) for m in ms]
    ys = [eye_f - m for m in ms]
    ys = [y + _dot(y, m2, 1, 0, precise=True) for y, m2 in zip(ys, m2s)]
    return [_dot(y, x, 1, 0, precise=True) for y, x in zip(ys, xs)]


def _gdn_pre(qs, ks, vs, gs, betas, mk, ts=None):
    c, hd = qs[0].shape
    eye, low = mk["eye"], mk["low"]
    g_rows = [_col_to_row(g, eye) for g in gs]
    d_cols = [jnp.sum(jnp.where(low, gr, 0.0), axis=1, keepdims=True) for gr in g_rows]
    d_rows = [jnp.sum(jnp.where(mk["upper"], g, 0.0), axis=0, keepdims=True) for g in gs]
    rels = [jnp.where(low, jnp.exp(jnp.minimum(dc - dr, 0.0)), 0.0) for dc, dr in zip(d_cols, d_rows)]
    d_lasts = [dc[c - 1:c, :] for dc in d_cols]
    es = [jnp.exp(dc) for dc in d_cols]
    fs = [jnp.exp(dl - dc) for dl, dc in zip(d_lasts, d_cols)]
    cds = [jnp.exp(dl) for dl in d_lasts]
    kbs = [k * b for k, b in zip(ks, betas)]
    kbqs = [jnp.concatenate([kb, q], axis=0) for kb, q in zip(kbs, qs)]
    kqk = [_dot(kbq, k, 1, 1) for kbq, k in zip(kbqs, ks)]
    kks = [x[:c, :] for x in kqk]
    qks = [x[c:, :] for x in kqk]
    if ts is None:
        ts = _tri_inv([jnp.where(mk["strict"], kk * rel, 0.0) for kk, rel in zip(kks, rels)], mk)
    vbs = [v * b for v, b in zip(vs, betas)]
    kbes = [kb * e for kb, e in zip(kbs, es)]
    uws = [_dot(t, jnp.concatenate([vb, kbe], axis=1), 1, 0) for t, vb, kbe in zip(ts, vbs, kbes)]
    out = []
    for i in range(len(qs)):
        out.append(dict(rel=rels[i], e=es[i], f=fs[i], cd=cds[i], kb=kbs[i], kbq=kbqs[i], kk=kks[i], qk=qks[i],
                        t=ts[i], u=uws[i][:, :hd], w=uws[i][:, hd:], uw=uws[i], attn=qks[i] * rels[i],
                        qd=qs[i] * es[i], kd=ks[i] * fs[i]))
    return out


def _gdn_apply(pres, sts, leaving=True):
    c = pres[0]["u"].shape[0]
    wqs = [_dot(jnp.concatenate([p["w"], p["qd"]], axis=0), st, 1, 0) for p, st in zip(pres, sts)]
    vns = [p["u"] - x[:c, :] for p, x in zip(pres, wqs)]
    os_ = [x[c:, :] + _dot(p["attn"], vn, 1, 0) for p, x, vn in zip(pres, wqs, vns)]
    if not leaving:
        return vns, os_, None
    new = [p["cd"] * st + _dot(p["kd"], vn, 0, 0) for p, st, vn in zip(pres, sts, vns)]
    return vns, os_, new


def _gdn_bwd_rest(qs, ks, vs, betas, sts, pres, vns, dos, dvns, dsts, mk):
    c, hd = qs[0].shape
    eye = mk["eye"]
    n = range(len(qs))
    dkds = [_dot(vns[i], dsts[i], 1, 1) for i in n]
    dcds = [jnp.sum(sts[i] * dsts[i]) for i in n]
    dattns = [jnp.where(mk["low"], _dot(dos[i], vns[i], 1, 1), 0.0) for i in n]
    dqdws = [_dot(jnp.concatenate([dos[i], -dvns[i]], axis=0), sts[i], 1, 1) for i in n]
    dqds = [x[:c, :] for x in dqdws]
    dws = [x[c:, :] for x in dqdws]
    dvks = [_dot(pres[i]["t"], jnp.concatenate([dvns[i], dws[i]], axis=1), 0, 0) for i in n]
    das = [jnp.where(mk["strict"], -_dot(dvks[i], pres[i]["uw"], 1, 1), 0.0) for i in n]
    dkqs = [jnp.concatenate([das[i] * pres[i]["rel"], dattns[i] * pres[i]["rel"]], axis=0) for i in n]
    dkbdq = [_dot(dkqs[i], ks[i], 1, 0) for i in n]
    dk0 = [_dot(dkqs[i], pres[i]["kbq"], 0, 0) for i in n]
    out = []
    rows1 = lax.broadcasted_iota(jnp.int32, (c, 1), 0)
    for i in n:
        p = pres[i]
        dvb, dkbe = dvks[i][:, :hd], dvks[i][:, hd:]
        grel = (das[i] * p["kk"] + dattns[i] * p["qk"]) * p["rel"]
        dkb = dkbdq[i][:c, :] + dkbe * p["e"]
        dk = dk0[i] + dkds[i] * p["f"] + dkb * betas[i]
        dq = dkbdq[i][c:, :] + dqds[i] * p["e"]
        dv = dvb * betas[i]
        dbeta = jnp.sum(dkb * ks[i], axis=1, keepdims=True) + jnp.sum(dvb * vs[i], axis=1, keepdims=True)
        de = jnp.sum(dqds[i] * qs[i], axis=1, keepdims=True) + jnp.sum(dkbe * p["kb"], axis=1, keepdims=True)
        dff = jnp.sum(dkds[i] * ks[i], axis=1, keepdims=True) * p["f"]
        dd = (de * p["e"] - dff + jnp.sum(grel, axis=1, keepdims=True)
              - _row_to_col(jnp.sum(grel, axis=0, keepdims=True), eye))
        dd = dd + jnp.where(rows1 == c - 1, jnp.sum(dff) + dcds[i] * p["cd"], 0.0)
        dg = jnp.sum(jnp.where(mk["upper"], _col_to_row(dd, eye), 0.0), axis=1, keepdims=True)
        out.append((dq, dk, dv, dg, dbeta))
    return out


def _gdn_specs(c):
    def qkv(j):
        return BS((c, GDN_WIDTH), lambda n: (n, j))

    return qkv


def _gdn_core_fwd(qkv, gbeta, proj, norm_w, *, name):
    s = qkv.shape[0]
    c, nh, hd, grp = GDN_CHUNK, GDN_HEADS, GDN_HEAD_DIM, GDN_GROUP
    n_chunks = s // c
    blk = _gdn_specs(grp * c)
    inst = [(sub, h) for sub in range(grp) for h in range(nh)]

    def body(q_ref, k_ref, v_ref, gb_ref, gate_ref, nw_ref, y_ref, st_ref, t_ref, state):
        @pl.when(pl.program_id(0) == 0)
        def _():
            state[...] = jnp.zeros_like(state)

        mk = _gdn_masks(c)
        rows = [slice(sub * c, (sub + 1) * c) for sub in range(grp)]
        lanes = [slice(h * hd, (h + 1) * hd) for h in range(nh)]
        gbs = [gb_ref[r, :] for r in rows]
        pres = _gdn_pre([q_ref[rows[sub], lanes[h]] for sub, h in inst], [k_ref[rows[sub], lanes[h]] for sub, h in inst],
                        [v_ref[rows[sub], lanes[h]] for sub, h in inst], [gbs[sub][:, h:h + 1] for sub, h in inst],
                        [gbs[sub][:, nh + h:nh + h + 1] for sub, h in inst], mk)
        sts = [state[ls, :] for ls in lanes]
        outs = []
        for sub in range(grp):
            for h in range(nh):
                st_ref[pl.ds((sub * nh + h) * hd, hd), :] = sts[h]
            _, os_, sts = _gdn_apply(pres[sub * nh:(sub + 1) * nh], sts)
            outs += os_
        for h in range(nh):
            state[lanes[h], :] = sts[h]
        nw = nw_ref[...]
        for i, (sub, h) in enumerate(inst):
            t_ref[pl.ds(i * c, c), :] = pres[i]["t"]
            o = outs[i]
            gate = gate_ref[rows[sub], lanes[h]]
            rs = lax.rsqrt(jnp.mean(o * o, axis=-1, keepdims=True) + EPS)
            y_ref[rows[sub], lanes[h]] = (o * rs * nw * (gate * _sigmoid(gate))).astype(BF16)

    return pl.pallas_call(
        body, grid=(n_chunks // grp,),
        in_specs=[blk(0), blk(1), blk(2), BS((grp * c, LANE), lambda n: (n, 0)),
                  BS((grp * c, GDN_WIDTH), lambda n: (n, COL_GATE // GDN_WIDTH)), BS((1, hd), lambda n: (0, 0))],
        out_specs=[BS((grp * c, GDN_WIDTH), lambda n: (n, 0)), BS((grp * nh * hd, hd), lambda n: (n, 0)),
                   BS((grp * nh * c, c), lambda n: (n, 0))],
        out_shape=[SDS((s, GDN_WIDTH), BF16), SDS((n_chunks * nh * hd, hd), F32), SDS((n_chunks * nh * c, c), F32)],
        scratch_shapes=[pltpu.VMEM((nh * hd, hd), F32)],
        compiler_params=_cp(), name=name)(qkv, qkv, qkv, gbeta, proj, norm_w)


def _gdn_core_bwd(qkv, gbeta, proj, norm_w, states, tinv, dcat, *, name):
    s = qkv.shape[0]
    c, nh, hd, grp = GDN_CHUNK, GDN_HEADS, GDN_HEAD_DIM, GDN_GROUP
    n_chunks = s // c
    last = n_chunks // grp - 1
    inst = [(sub, h) for sub in range(grp) for h in range(nh)]

    def rev(j, w):
        return BS((grp * c, w), lambda n: (last - n, j))

    def body(q_ref, k_ref, v_ref, gb_ref, gate_ref, nw_ref, st_ref, t_ref, dy_ref,
             dqkv_ref, dgb_ref, dgate_ref, dnw_ref, dstate):
        @pl.when(pl.program_id(0) == 0)
        def _():
            dstate[...] = jnp.zeros_like(dstate)
            dnw_ref[...] = jnp.zeros_like(dnw_ref)

        mk = _gdn_masks(c)
        rows = [slice(sub * c, (sub + 1) * c) for sub in range(grp)]
        lanes = [slice(h * hd, (h + 1) * hd) for h in range(nh)]
        gbs = [gb_ref[r, :] for r in rows]
        qs = [q_ref[rows[sub], lanes[h]] for sub, h in inst]
        ks = [k_ref[rows[sub], lanes[h]] for sub, h in inst]
        vs = [v_ref[rows[sub], lanes[h]] for sub, h in inst]
        betas = [gbs[sub][:, nh + h:nh + h + 1] for sub, h in inst]
        sts = [st_ref[pl.ds(i * hd, hd), :] for i in range(len(inst))]
        pres = _gdn_pre(qs, ks, vs, [gbs[sub][:, h:h + 1] for sub, h in inst], betas, mk,
                        ts=[t_ref[pl.ds(i * c, c), :] for i in range(len(inst))])
        vns, outs, _ = _gdn_apply(pres, sts, leaving=False)

        nw = nw_ref[...]
        dnw = jnp.zeros((1, hd), F32)
        dos = []
        for i, (sub, h) in enumerate(inst):
            o = outs[i]
            gate = gate_ref[rows[sub], lanes[h]]
            dy = dy_ref[rows[sub], lanes[h]].astype(F32)
            sg = _sigmoid(gate)
            rs = lax.rsqrt(jnp.mean(o * o, axis=-1, keepdims=True) + EPS)
            nrm = o * rs
            dgate_ref[rows[sub], lanes[h]] = dy * nrm * nw * sg * (1.0 + gate * (1.0 - sg))
            dnv = dy * (gate * sg)
            dnw = dnw + jnp.sum(dnv * nrm, axis=0, keepdims=True)
            dno = dnv * nw
            dos.append(rs * (dno - nrm * jnp.mean(dno * nrm, axis=-1, keepdims=True)))
        dnw_ref[...] += dnw

        from_o = [_dot(p["attn"], do, 0, 0) for p, do in zip(pres, dos)]
        to_st = [_dot(p["qd"], do, 0, 0) for p, do in zip(pres, dos)]
        dst = [dstate[ls, :] for ls in lanes]
        dsts = [None] * len(inst)
        dvns = [None] * len(inst)
        for sub in reversed(range(grp)):
            idx = [sub * nh + h for h in range(nh)]
            for h, i in enumerate(idx):
                dsts[i] = dst[h]
                dvns[i] = from_o[i] + _dot(pres[i]["kd"], dst[h], 1, 0)
            dst = [pres[i]["cd"] * dst[h] + to_st[i] - _dot(pres[i]["w"], dvns[i], 0, 0) for h, i in enumerate(idx)]
        for h in range(nh):
            dstate[lanes[h], :] = dst[h]

        grads = _gdn_bwd_rest(qs, ks, vs, betas, sts, pres, vns, dos, dvns, dsts, mk)
        lane = lax.broadcasted_iota(jnp.int32, (c, LANE), 1)
        dgb = [jnp.zeros((c, LANE), F32) for _ in range(grp)]
        for (sub, h), (dq, dk, dv, dg, dbeta) in zip(inst, grads):
            dqkv_ref[rows[sub], lanes[h]] = dq
            dqkv_ref[rows[sub], slice(GDN_WIDTH + h * hd, GDN_WIDTH + (h + 1) * hd)] = dk
            dqkv_ref[rows[sub], slice(2 * GDN_WIDTH + h * hd, 2 * GDN_WIDTH + (h + 1) * hd)] = dv
            dgb[sub] = jnp.where(lane == h, dg, jnp.where(lane == nh + h, dbeta, dgb[sub]))
        for sub in range(grp):
            dgb_ref[rows[sub], :] = dgb[sub]

    return pl.pallas_call(
        body, grid=(n_chunks // grp,),
        in_specs=[rev(0, GDN_WIDTH), rev(1, GDN_WIDTH), rev(2, GDN_WIDTH), rev(0, LANE),
                  rev(COL_GATE // GDN_WIDTH, GDN_WIDTH), BS((1, hd), lambda n: (0, 0)),
                  BS((grp * nh * hd, hd), lambda n: (last - n, 0)), BS((grp * nh * c, c), lambda n: (last - n, 0)),
                  rev(1, GDN_WIDTH)],
        out_specs=[rev(0, 3 * GDN_WIDTH), rev(0, LANE), rev(0, GDN_WIDTH), BS((1, hd), lambda n: (0, 0))],
        out_shape=[SDS((s, 3 * GDN_WIDTH), F32), SDS((s, LANE), F32), SDS((s, GDN_WIDTH), F32), SDS((1, hd), F32)],
        scratch_shapes=[pltpu.VMEM((nh * hd, hd), F32)],
        compiler_params=_cp(), name=name)(qkv, qkv, qkv, gbeta, proj, norm_w, states, tinv, dcat)


XATTN_TQ = 512


def _xattn_probs(qh, kh):
    sc = _dot(qh, kh, 1, 1) * (XATTN_HEAD_DIM ** -0.5)
    p = jnp.exp(sc - jnp.max(sc, axis=-1, keepdims=True))
    return p / jnp.sum(p, axis=-1, keepdims=True)


def _xattn_fwd(q, kv, *, name):
    s, d = q.shape
    m = kv.shape[0]
    tq, hd = _tile(s, XATTN_TQ), XATTN_HEAD_DIM

    def body(q_ref, k_ref, v_ref, o_ref):
        for h in range(XATTN_HEADS):
            ls = slice(h * hd, (h + 1) * hd)
            p = _xattn_probs(q_ref[:, ls], k_ref[:, ls])
            o_ref[:, ls] = _dot(p, v_ref[:, ls], 1, 0).astype(BF16)

    return pl.pallas_call(
        body, grid=(s // tq,),
        in_specs=[BS((tq, d), lambda i: (i, 0)), BS((m, d), lambda i: (0, 0)), BS((m, d), lambda i: (0, 1))],
        out_specs=BS((tq, d), lambda i: (i, 0)), out_shape=SDS((s, d), BF16),
        compiler_params=_cp(), name=name)(q, kv, kv)


def _xattn_bwd(q, kv, do, *, name):
    s, d = q.shape
    m = kv.shape[0]
    tq, hd = _tile(s, XATTN_TQ), XATTN_HEAD_DIM
    scale = hd ** -0.5

    def body(q_ref, k_ref, v_ref, do_ref, dq_ref, dkv_ref):
        @pl.when(pl.program_id(0) == 0)
        def _():
            dkv_ref[...] = jnp.zeros_like(dkv_ref)

        for h in range(XATTN_HEADS):
            ls = slice(h * hd, (h + 1) * hd)
            vs = slice(d + h * hd, d + (h + 1) * hd)
            qh, kh, doh = q_ref[:, ls], k_ref[:, ls], do_ref[:, ls]
            p = _xattn_probs(qh, kh)
            dp = _dot(doh, v_ref[:, ls], 1, 1)
            ds = p * (dp - jnp.sum(p * dp, axis=-1, keepdims=True)) * scale
            dq_ref[:, ls] = _dot(ds, kh, 1, 0).astype(BF16)
            dkv_ref[:, ls] += _dot(ds, qh, 0, 0)
            dkv_ref[:, vs] += _dot(p, doh, 0, 0)

    row = BS((tq, d), lambda i: (i, 0))
    return pl.pallas_call(
        body, grid=(s // tq,),
        in_specs=[row, BS((m, d), lambda i: (0, 0)), BS((m, d), lambda i: (0, 1)), row],
        out_specs=[row, BS((m, 2 * d), lambda i: (0, 0))],
        out_shape=[SDS((s, d), BF16), SDS((m, 2 * d), F32)],
        compiler_params=_cp(), name=name)(q, kv, kv, do)


def _pad_lanes(vec4):
    return jnp.zeros((1, LANE), F32).at[0, :GDN_HEADS].set(vec4)


def _layer_fwd(h0, mem, tabs, p):
    sv = dict(h0=h0)
    hn1 = _rmsnorm(h0, p["norm_mix_pre"], name="norm_mix_pre")
    proj = _mm(hn1, p["w_in"], name="mm_in")
    ya, lse = _attn_flash_fwd(proj, tabs, name="attn_fwd")
    yc = _sconv_fwd(proj, p["conv_short"], name="sconv_fwd")
    qkv = _gdn_pre_fwd(proj, p["conv_gdn"], name="gdn_pre_fwd")
    gbeta = _gdn_gates_fwd(proj, p["gdn_a_log"], p["gdn_dt_bias"], name="gdn_gates_fwd")
    yg, states, tinv = _gdn_core_fwd(qkv, gbeta, proj, p["gdn_norm"], name="gdn_core_fwd")
    cat = jnp.concatenate([ya, yc, yg], axis=-1)
    mix = _mm(cat, p["w_out"], name="mm_out")
    h1 = _resnorm(h0, mix, p["norm_mix_post"], name="norm_mix_post")
    hn2 = _rmsnorm(h1, p["norm_xattn_pre"], name="norm_xattn_pre")
    memn = _rmsnorm(mem, p["norm_mem"], name="norm_mem")
    xq = _mm(hn2, p["w_xq"], out_dtype=BF16, name="mm_xq")
    kv = _mm(memn, p["w_xkv"], out_dtype=BF16, b_shards=True, name="mm_xkv")
    xo = _xattn_fwd(xq, kv, name="xattn_fwd")
    xa = _mm(xo, p["w_xo"], name="mm_xo")
    h2 = _resnorm(h1, xa, p["norm_xattn_post"], name="norm_xattn_post")
    hn3 = _rmsnorm(h2, p["norm_ffn_pre"], name="norm_ffn_pre")
    gu = _mm(hn3, p["w_gate_up"], b_shards=True, halves=True, name="mm_gate_up")
    act = _swiglu(gu, name="swiglu_fwd")
    f = _mm(act, p["w_down"], name="mm_down")
    h3 = _resnorm(h2, f, p["norm_ffn_post"], name="norm_ffn_post")
    sv.update(hn1=hn1, proj=proj, lse=lse, qkv=qkv, gbeta=gbeta, states=states, tinv=tinv, cat=cat, mix=mix, h1=h1, hn2=hn2,
              memn=memn, xq=xq, kv=kv, xo=xo, xa=xa, h2=h2, hn3=hn3, gu=gu, act=act, f=f)
    return h3, sv


def _layer_bwd(dh3, mem, tabs, p, sv):
    g = {}
    df, g["norm_ffn_post"] = _rmsnorm_bwd(sv["f"], p["norm_ffn_post"], dh3, name="norm_ffn_post_bwd")
    dact = _mm(df, p["w_down"], tb=True, name="mm_down_da")
    g["w_down"] = _mm(sv["act"], df, ta=True, name="mm_down_dw")
    dgu = _swiglu_bwd(sv["gu"], dact, name="swiglu_bwd")
    dhn3 = _mm(dgu, p["w_gate_up"], tb=True, b_shards=True, halves=True, name="mm_gate_up_da")
    g["w_gate_up"] = _mm(sv["hn3"], dgu, ta=True, out_shards=True, halves=True, name="mm_gate_up_dw")
    dh2, g["norm_ffn_pre"] = _rmsnorm_bwd(sv["h2"], p["norm_ffn_pre"], dhn3, res=dh3, name="norm_ffn_pre_bwd")
    dxa, g["norm_xattn_post"] = _rmsnorm_bwd(sv["xa"], p["norm_xattn_post"], dh2, name="norm_xattn_post_bwd")
    dxo = _mm(dxa, p["w_xo"], tb=True, name="mm_xo_da")
    g["w_xo"] = _mm(sv["xo"], dxa, ta=True, name="mm_xo_dw")
    dxq, dkv = _xattn_bwd(sv["xq"], sv["kv"], dxo, name="xattn_bwd")
    dhn2 = _mm(dxq, p["w_xq"], tb=True, name="mm_xq_da")
    g["w_xq"] = _mm(sv["hn2"], dxq, ta=True, name="mm_xq_dw")
    dmemn = _mm(dkv, p["w_xkv"], tb=True, b_shards=True, name="mm_xkv_da")
    g["w_xkv"] = _mm(sv["memn"], dkv, ta=True, out_shards=True, name="mm_xkv_dw")
    _, g["norm_mem"] = _rmsnorm_bwd(mem, p["norm_mem"], dmemn, name="norm_mem_bwd")
    dh1, g["norm_xattn_pre"] = _rmsnorm_bwd(sv["h1"], p["norm_xattn_pre"], dhn2, res=dh2, name="norm_xattn_pre_bwd")
    dmix, g["norm_mix_post"] = _rmsnorm_bwd(sv["mix"], p["norm_mix_post"], dh1, name="norm_mix_post_bwd")
    dcat = _mm(dmix, p["w_out"], tb=True, name="mm_out_da")
    g["w_out"] = _mm(sv["cat"], dmix, ta=True, name="mm_out_dw")
    proj = sv["proj"]
    daq, dak, dav = _attn_flash_bwd(proj, tabs, sv["cat"], sv["lse"], dcat, name="attn_bwd")
    dcb, dcc, dcx, g["conv_short"] = _sconv_bwd(proj, p["conv_short"], dcat, name="sconv_bwd")
    dqkv, dgbeta, dgate, g["gdn_norm"] = _gdn_core_bwd(sv["qkv"], sv["gbeta"], proj, p["gdn_norm"], sv["states"], sv["tinv"],
                                                        dcat, name="gdn_core_bwd")
    dgqkv, g["conv_gdn"] = _gdn_pre_bwd(proj, p["conv_gdn"], dqkv, name="gdn_pre_bwd")
    dab, g["gdn_a_log"], g["gdn_dt_bias"] = _gdn_gates_bwd(proj, p["gdn_a_log"], p["gdn_dt_bias"], dgbeta,
                                                          name="gdn_gates_bwd")
    s = proj.shape[0]
    dproj = jnp.concatenate([daq, dak, dav, dcb, dcc, dcx, dgqkv, dgate, dab,
                             jnp.zeros((s, IN_PAD - COL_AB - LANE), F32)], axis=-1)
    dhn1 = _mm(dproj, p["w_in"], tb=True, name="mm_in_da")
    g["w_in"] = _mm(sv["hn1"], dproj, ta=True, name="mm_in_dw")
    dh0, g["norm_mix_pre"] = _rmsnorm_bwd(sv["h0"], p["norm_mix_pre"], dhn1, res=dh1, name="norm_mix_pre_bwd")
    return dh0, g


MATRICES = ("w_in", "w_out", "w_xq", "w_xkv", "w_xo", "w_gate_up", "w_down")
VECTORS = ("norm_mix_pre", "norm_mix_post", "conv_short", "conv_gdn", "gdn_a_log", "gdn_dt_bias", "gdn_norm",
           "norm_mem", "norm_xattn_pre", "norm_xattn_post", "norm_ffn_pre", "norm_ffn_post")


def _w_in_to_padded(w):
    zeros = jnp.zeros(w.shape[:-1] + (IN_PAD - IN_WIDTH,), w.dtype)
    return jnp.concatenate([w[..., :COL_GATE], w[..., COL_GATE + 8:], w[..., COL_GATE:COL_GATE + 8], zeros], axis=-1)


def _w_in_from_padded(g):
    return jnp.concatenate([g[..., :COL_GATE], g[..., COL_AB:COL_AB + 8], g[..., COL_GATE:COL_AB]], axis=-1)


def _layer_params(full, l):
    p = {n: full[n][l] for n in MATRICES}
    for n in VECTORS:
        v = full[n][l]
        if n in ("gdn_a_log", "gdn_dt_bias"):
            p[n] = _pad_lanes(v)
        elif v.ndim == 1:
            p[n] = v.reshape(1, -1)
        else:
            p[n] = v
    return p


def _local_step(x, mem, pos, target, full, matrices_for=None, on_grads=None):
    tabs = _rope_tables(pos, name="rope_tables")
    h = x
    saved, params = [], []
    for l in range(DEPTH):
        if matrices_for is not None:
            full = {**full, **{n: {l: v} for n, v in matrices_for(l, h).items()}}
        p = _layer_params(full, l)
        h, sv = _layer_fwd(h, mem, tabs, p)
        params.append(p)
        saved.append(sv)
    loss_row, dh = _loss_grad(h, target, name="loss_grad")
    grads = [None] * DEPTH
    token = None
    for l in reversed(range(DEPTH)):
        p = params[l]
        if token is not None:
            p = {**p, "norm_ffn_post": p["norm_ffn_post"] + token[:1, :1]}
        dh, grads[l] = _layer_bwd(dh, mem, tabs, p, saved[l])
        if on_grads is not None:
            token = on_grads(l, grads[l], dh)
    return loss_row, dh, grads


ANY = pl.BlockSpec(memory_space=pl.ANY)
MESH = pl.DeviceIdType.MESH


def _flip(pos, mask):
    return tuple(1 - v if m else v for v, m in zip(pos, mask))


def _exchange(ins, out_shapes, remote, local, *, name):
    n_in = len(ins)
    n_out = len(out_shapes)

    def at(ref, idx):
        return ref.at[idx] if idx else ref

    def body(*refs):
        in_refs = refs[:n_in]
        out_refs = refs[n_in:n_in + n_out]
        send_sems, recv_sems, local_sems = refs[n_in + n_out:]
        me = (lax.axis_index("x"), lax.axis_index("y"), lax.axis_index("c"))
        waits = []
        for k, (ii, src_at, oi, dst_at, mask) in enumerate(remote):
            peer = _flip(me, mask)
            pltpu.make_async_remote_copy(
                src_ref=at(in_refs[ii], src_at(me, peer)), dst_ref=at(out_refs[oi], dst_at(me)),
                send_sem=send_sems.at[k], recv_sem=recv_sems.at[k], device_id=peer, device_id_type=MESH).start()
            waits.append(pltpu.make_async_remote_copy(
                src_ref=at(in_refs[ii], src_at(peer, me)), dst_ref=at(out_refs[oi], dst_at(peer)),
                send_sem=send_sems.at[k], recv_sem=recv_sems.at[k], device_id=peer, device_id_type=MESH))
        own = []
        for k, (ii, src_at, oi, dst_at) in enumerate(local):
            cp = pltpu.make_async_copy(at(in_refs[ii], src_at(me)), at(out_refs[oi], dst_at(me)), local_sems.at[k])
            cp.start()
            own.append(cp)
        for w in waits:
            w.wait_send()
            w.wait_recv()
        for cp in own:
            cp.wait()

    return pl.pallas_call(
        body, in_specs=[ANY] * n_in, out_specs=[ANY] * n_out, out_shape=list(out_shapes),
        scratch_shapes=[pltpu.SemaphoreType.DMA((len(remote),)), pltpu.SemaphoreType.DMA((len(remote),)),
                        pltpu.SemaphoreType.DMA((max(len(local), 1),))],
        name=name)(*ins)


HBM = pl.BlockSpec(memory_space=pltpu.HBM)
SEM = pl.BlockSpec(memory_space=pltpu.SEMAPHORE)
SPLIT_EFFECT = pltpu.SideEffectType.DATAFLOW_SIDE_EFFECTING


def _exchange_start(ins, land_shapes, remote, *, name):
    n_in, n_land, n_cp = len(ins), len(land_shapes), len(remote)

    def body(*refs):
        in_refs, land_refs = refs[:n_in], refs[n_in:n_in + n_land]
        send_sems, recv_sems = refs[n_in + n_land:n_in + n_land + 2]
        token = refs[-1]
        me = (lax.axis_index("x"), lax.axis_index("y"), lax.axis_index("c"))
        for k, (ii, src_at, oi, dst_at, mask) in enumerate(remote):
            peer = _flip(me, mask)
            idx_s, idx_d = src_at(me, peer), dst_at(me)
            pltpu.make_async_remote_copy(
                src_ref=in_refs[ii].at[idx_s] if idx_s else in_refs[ii],
                dst_ref=land_refs[oi].at[idx_d] if idx_d else land_refs[oi],
                send_sem=send_sems.at[k], recv_sem=recv_sems.at[k], device_id=peer, device_id_type=MESH).start()
        token[...] = jnp.zeros_like(token)

    buffers = [pltpu.with_memory_space_constraint(a, pltpu.HBM) for a in ins]
    buffers += [pltpu.with_memory_space_constraint(lax.empty(s.shape, s.dtype), pltpu.HBM) for s in land_shapes]
    out = pl.pallas_call(
        body, name=name,
        out_shape=(pltpu.SemaphoreType.DMA((n_cp,)), pltpu.SemaphoreType.DMA((n_cp,)),
                   *[pltpu.HBM(b.shape, b.dtype) for b in buffers], SDS((8, LANE), F32)),
        in_specs=[HBM] * len(buffers),
        out_specs=(SEM, SEM, *[HBM] * len(buffers), pl.BlockSpec(memory_space=pltpu.VMEM)),
        input_output_aliases={i: 2 + i for i in range(len(buffers))},
        compiler_params=pltpu.CompilerParams(has_side_effects=SPLIT_EFFECT))(*buffers)
    return out[0], out[1], list(out[2:2 + n_in]), list(out[2 + n_in:2 + n_in + n_land]), out[-1]


def _exchange_wait(send_sems, recv_sems, ins, lands, remote, after, *, name):
    n_in, n_land = len(ins), len(lands)

    def body(*refs):
        in_refs, land_refs = refs[:n_in], refs[n_in:n_in + n_land]
        send_sems_, recv_sems_ = refs[n_in + n_land:n_in + n_land + 2]
        me = (lax.axis_index("x"), lax.axis_index("y"), lax.axis_index("c"))
        for k, (ii, src_at, oi, dst_at, mask) in enumerate(remote):
            peer = _flip(me, mask)
            idx_s, idx_d = src_at(peer, me), dst_at(peer)
            cp = pltpu.make_async_remote_copy(
                src_ref=in_refs[ii].at[idx_s] if idx_s else in_refs[ii],
                dst_ref=land_refs[oi].at[idx_d] if idx_d else land_refs[oi],
                send_sem=send_sems_.at[k], recv_sem=recv_sems_.at[k], device_id=peer, device_id_type=MESH)
            cp.wait_send()
            cp.wait_recv()

    buffers = list(ins) + list(lands)
    out = pl.pallas_call(
        body, name=name, out_shape=tuple(pltpu.HBM(b.shape, b.dtype) for b in buffers),
        in_specs=[HBM] * len(buffers) + [SEM, SEM, ANY], out_specs=tuple([HBM] * len(buffers)),
        input_output_aliases={i: i for i in range(len(buffers))},
        compiler_params=pltpu.CompilerParams(has_side_effects=SPLIT_EFFECT))(*buffers, send_sems, recv_sems, after)
    return list(out[:n_in]), list(out[n_in:])


def _chip(pos):
    return 2 * pos[0] + pos[1]


XY_MASKS = ((1, 0, 0), (0, 1, 0), (1, 1, 0))
SIBLING = (0, 0, 1)
ALL_MASKS = tuple((a, b, c) for a in (0, 1) for b in (0, 1) for c in (0, 1))[1:]


def _gather_xy(arrs, *, name):
    n = len(arrs)
    outs = [SDS((4,) + a.shape, a.dtype) for a in arrs]
    halves = [a.shape[0] // 2 for a in arrs]

    def body(*refs):
        in_refs, out_refs = refs[:n], refs[n:2 * n]
        ici_send, ici_recv, d2d_send, d2d_recv = refs[2 * n:]
        me = (lax.axis_index("x"), lax.axis_index("y"), lax.axis_index("c"))
        sibling = _flip(me, SIBLING)
        flows = []
        for i in range(n):
            mine = pl.ds(me[2] * halves[i], halves[i])
            other = pl.ds(sibling[2] * halves[i], halves[i])
            for m in XY_MASKS:
                k = len(flows)
                peer = _flip(me, m)

                def remote(src, dst, sems, to, k=k):
                    return pltpu.make_async_remote_copy(src_ref=src, dst_ref=dst, send_sem=sems[0].at[k],
                                                        recv_sem=sems[1].at[k], device_id=to, device_id_type=MESH)

                landed = out_refs[i].at[_chip(peer), mine]
                send = remote(in_refs[i].at[mine], out_refs[i].at[_chip(me), mine], (ici_send, ici_recv), peer)
                send.start()
                arrive = remote(in_refs[i].at[mine], landed, (ici_send, ici_recv), peer)
                forward = remote(landed, landed, (d2d_send, d2d_recv), sibling)
                handed = remote(out_refs[i].at[_chip(peer), other], out_refs[i].at[_chip(peer), other],
                                (d2d_send, d2d_recv), sibling)
                flows.append((send, arrive, forward, handed))
        for _, arrive, forward, _ in flows:
            arrive.wait_recv()
            forward.start()
        for send, _, forward, handed in flows:
            handed.wait_recv()
            send.wait_send()
            forward.wait_send()

    n_flows = 3 * n
    return pl.pallas_call(
        body, in_specs=[ANY] * n, out_specs=[ANY] * n, out_shape=outs,
        scratch_shapes=[pltpu.SemaphoreType.DMA((n_flows,))] * 4, name=name)(*arrs)


def _gather_all(arr, *, name):
    slot = lambda pos: (4 * pos[0] + 2 * pos[1] + pos[2],)
    whole = lambda *_: ()
    remote = [(0, whole, 0, slot, m) for m in ALL_MASKS]
    return _exchange([arr], [SDS((8,) + arr.shape, arr.dtype)], remote, [(0, whole, 0, slot)], name=name)[0]


def _swap_halves(arrs, *, name):
    outs = [SDS(a.shape[:2] + (a.shape[2] // 2, a.shape[3]), a.dtype) for a in arrs]

    def src(i):
        half = arrs[i].shape[2] // 2
        return lambda sender, receiver: (slice(None), slice(None), pl.ds(receiver[2] * half, half))

    whole = lambda *_: ()
    remote = [(i, src(i), i, whole, SIBLING) for i in range(len(arrs))]
    return _exchange(arrs, outs, remote, [], name=name)


def _scatter_xy(arrs, *, name):
    outs = [SDS((len(XY_MASKS),) + a.shape[1:], a.dtype) for a in arrs]
    src = lambda sender, receiver: (_chip(receiver),)
    remote = [(i, src, i, (lambda sender, j=j: (j,)), m) for i in range(len(arrs)) for j, m in enumerate(XY_MASKS)]
    return _exchange(arrs, outs, remote, [], name=name)


def _send_to_sibling(arrs, *, name):
    outs = [SDS(a.shape, a.dtype) for a in arrs]
    whole = lambda *_: ()
    remote = [(i, whole, i, whole, SIBLING) for i in range(len(arrs))]
    return _exchange(arrs, outs, remote, [], name=name)


def _add_half(g, other, core, *, name):
    n4, nl, r, c = g.shape
    half = r // 2
    g3 = g.reshape(n4 * nl, 2, half, c)
    o3 = other.reshape(n4 * nl, half, c)
    tr = _rows_tile(half) if half > 512 else half

    def body(core_ref, g_ref, o_ref, out_ref):
        out_ref[...] = (g_ref[...] + o_ref[...]).astype(BF16)

    return pl.pallas_call(
        body,
        grid_spec=pltpu.PrefetchScalarGridSpec(
            num_scalar_prefetch=1, grid=(n4 * nl, half // tr),
            in_specs=[BS((None, None, tr, c), lambda i, j, core_ref: (i, core_ref[0], j, 0)),
                      BS((None, tr, c), lambda i, j, core_ref: (i, j, 0))],
            out_specs=BS((None, tr, c), lambda i, j, core_ref: (i, j, 0))),
        out_shape=SDS((n4 * nl, half, c), BF16), compiler_params=_cp(), name=name)(core, g3, o3).reshape(n4, nl, half, c)


def _sum_chips(parts, mine, chip, *, name):
    n4, nl, h, c = mine.shape
    tr = _rows_tile(h) if h > 512 else h

    def body(chip_ref, p_ref, own_ref, out_ref):
        me = chip_ref[0]
        own = own_ref[...].astype(F32)
        across = [p_ref[j].astype(F32) for j in range(len(XY_MASKS))]
        t = []
        for s in range(n4):
            rel = s ^ me
            t.append(jnp.where(rel == 0, own, jnp.where(rel == 2, across[0], jnp.where(rel == 1, across[1], across[2]))))
        out_ref[...] = ((t[0] + t[1]) + t[2]) + t[3]

    return pl.pallas_call(
        body,
        grid_spec=pltpu.PrefetchScalarGridSpec(
            num_scalar_prefetch=1, grid=(nl, h // tr),
            in_specs=[BS((len(XY_MASKS), None, tr, c), lambda i, j, chip_ref: (0, i, j, 0)),
                      BS((None, None, tr, c), lambda i, j, chip_ref: (chip_ref[0], i, j, 0))],
            out_specs=BS((None, tr, c), lambda i, j, chip_ref: (i, j, 0))),
        out_shape=SDS((nl, h, c), F32), compiler_params=_cp(), name=name)(chip, parts, mine)


def _sum_devices(parts, *, name):
    n, r, c = parts.shape

    def body(p_ref, out_ref):
        acc = p_ref[0]
        for d in range(1, n):
            acc = acc + p_ref[d]
        out_ref[...] = acc

    return pl.pallas_call(
        body, grid=(1,), in_specs=[BS((n, r, c), lambda i: (0, 0, 0))], out_specs=BS((r, c), lambda i: (0, 0)),
        out_shape=SDS((r, c), F32), compiler_params=_cp(), name=name)(parts)


WEIGHTS = ("norm_mix_pre", "norm_mix_post", "w_in", "conv_short", "conv_gdn", "gdn_a_log", "gdn_dt_bias",
           "gdn_norm", "w_out", "norm_mem", "norm_xattn_pre", "norm_xattn_post", "w_xq", "w_xkv", "w_xo",
           "norm_ffn_pre", "norm_ffn_post", "w_gate_up", "w_down")
COL_SHARDED = ("w_in", "w_xkv", "w_gate_up", "conv_short", "conv_gdn")
ROW_SHARDED = ("w_out", "w_xq", "w_xo", "w_down")
SMALL_SHARDED = ("conv_short", "conv_gdn")
SMALL_ROW_PAD = 8


KEPT_AS_SHARDS = ("w_xkv", "w_gate_up")


def _from_shards(n, g):
    if n in KEPT_AS_SHARDS:
        return g
    if n in COL_SHARDED:
        t = jnp.moveaxis(g, 0, -2)
        t = t.reshape(t.shape[:-2] + (-1,))
        return _w_in_to_padded(t) if n == "w_in" else t
    return g.reshape(-1, g.shape[-1])


def _to_shards(n, g):
    if n in KEPT_AS_SHARDS:
        return g
    if n == "w_in":
        g = _w_in_from_padded(g)
        return jnp.moveaxis(g.reshape(g.shape[0], 4, -1), 1, 0)
    return g.reshape(4, -1, g.shape[-1])


def _pack_small(grads):
    rows = []
    for g in grads:
        for n in WEIGHTS:
            if n not in MATRICES:
                part = g[n].reshape(-1, LANE)
                rows.append(jnp.pad(part, ((0, -part.shape[0] % SMALL_ROW_PAD), (0, 0))))
    return jnp.concatenate(rows, axis=0)


def _unpack_small(packed, like):
    out, at = [], 0
    for _ in range(DEPTH):
        g = {}
        for n in WEIGHTS:
            if n not in MATRICES:
                shape = like[n].shape
                k = math.prod(shape) // LANE
                g[n] = packed[at:at + k].reshape(shape)
                at += k + (-k % SMALL_ROW_PAD)
        out.append(g)
    return out


def kernel(x, mem, positions, norm_mix_pre, norm_mix_post, w_in, conv_short, conv_gdn, gdn_a_log, gdn_dt_bias, gdn_norm, w_out, norm_mem, norm_xattn_pre, norm_xattn_post, w_xq, w_xkv, w_xo, norm_ffn_pre, norm_ffn_post, w_gate_up, w_down, loss_target, m_norm_mix_pre, m_norm_mix_post, m_w_in, m_conv_short, m_conv_gdn, m_gdn_a_log, m_gdn_dt_bias, m_gdn_norm, m_w_out, m_norm_mem, m_norm_xattn_pre, m_norm_xattn_post, m_w_xq, m_w_xkv, m_w_xo, m_norm_ffn_pre, m_norm_ffn_post, m_w_gate_up, m_w_down, v_norm_mix_pre, v_norm_mix_post, v_w_in, v_conv_short, v_conv_gdn, v_gdn_a_log, v_gdn_dt_bias, v_gdn_norm, v_w_out, v_norm_mem, v_norm_xattn_pre, v_norm_xattn_post, v_w_xq, v_w_xkv, v_w_xo, v_norm_ffn_pre, v_norm_ffn_post, v_w_gate_up, v_w_down):
    args = dict(locals())
    w = {n: args[n] for n in WEIGHTS}
    m = {n: args["m_" + n] for n in WEIGHTS}
    v = {n: args["v_" + n] for n in WEIGHTS}
    seq = x.shape[1]
    chip = 2 * lax.axis_index("x") + lax.axis_index("y")
    core = lax.axis_index("c").astype(jnp.int32).reshape(1)

    def cast(n, l):
        return w[n][l].astype(BF16)

    first = [cast(n, 0) for n in MATRICES] + [w[n] for n in SMALL_SHARDED]
    blocks = _gather_xy(first, name="gather_weights")
    blocks = [lax.dynamic_update_index_in_dim(b, o, chip, axis=0) for b, o in zip(blocks, first)]
    full = {n: _from_shards(n, b) for n, b in zip(list(MATRICES) + list(SMALL_SHARDED), blocks)}
    layer0 = {n: full.pop(n) for n in MATRICES}
    for n in WEIGHTS:
        if n not in full and n not in MATRICES:
            full[n] = w[n]
    whole = lambda *_: ()
    gather_plan = [(i, whole, i, (lambda sender: (_chip(sender),)), mask)
                   for i in range(len(MATRICES)) for mask in XY_MASKS]
    in_flight = {}
    for l in range(1, DEPTH):
        own = [cast(n, l) for n in MATRICES]
        in_flight[l] = (own,) + _exchange_start(own, [SDS((4,) + o.shape, o.dtype) for o in own], gather_plan,
                                                name=f"gather_start_{l}")[:4]

    def matrices_for(l, h):
        if l == 0:
            return layer0
        own, send_sems, recv_sems, thru, lands = in_flight.pop(l)
        _, lands = _exchange_wait(send_sems, recv_sems, thru, lands, gather_plan, h, name=f"gather_wait_{l}")
        lands = [lax.dynamic_update_index_in_dim(b, o, chip, axis=0) for b, o in zip(lands, own)]
        return {n: _from_shards(n, b) for n, b in zip(MATRICES, lands)}

    chip1 = chip.astype(jnp.int32).reshape(1)
    scatter_plan = [(i, (lambda sender, receiver: (_chip(receiver),)), i, (lambda sender, j=j: (j,)), mask)
                    for i in range(len(MATRICES)) for j, mask in enumerate(XY_MASKS)]
    reduced = {}
    sending = []

    def finish_scatter(after):
        l, send_sems, recv_sems, pair, lands = sending.pop()
        pair, parts = _exchange_wait(send_sems, recv_sems, pair, lands, scatter_plan, after, name=f"scatter_wait_{l}")
        reduced[l] = [_sum_chips(p, pr, chip1, name="grads_chip_sum") for p, pr in zip(parts, pair)]

    def on_grads(l, g, dh):
        if sending:
            finish_scatter(dh)
        mine = [_to_shards(n, g[n])[:, None] for n in MATRICES]
        theirs = _swap_halves(mine, name="grads_swap_halves")
        pair = [_add_half(a, b, core, name="grads_pair_sum") for a, b in zip(mine, theirs)]
        lands = [SDS((len(XY_MASKS),) + p.shape[1:], p.dtype) for p in pair]
        send_sems, recv_sems, pair, lands, token = _exchange_start(pair, lands, scatter_plan, name=f"scatter_start_{l}")
        sending.append((l, send_sems, recv_sems, pair, lands))
        return token

    loss_row, dx, grads = _local_step(x[0], mem[0], positions.reshape(seq, 1), loss_target[0], full,
                                      matrices_for=matrices_for, on_grads=on_grads)
    finish_scatter(dx)
    keys = [(n, l) for l in range(DEPTH) for n in MATRICES]
    halves = [r for l in range(DEPTH) for r in reduced[l]]
    others = _send_to_sibling(halves, name="grads_share_halves")
    south = lax.axis_index("c") == 0
    grad = {}
    for n in MATRICES:
        layers = []
        for (n2, _), a, b in zip(keys, halves, others):
            if n2 == n:
                layers += [jnp.where(south, a, b), jnp.where(south, b, a)]
        grad[n] = jnp.concatenate(layers, axis=0).reshape(w[n].shape)

    packed = _pack_small(grads)
    total = _sum_devices(_gather_all(packed, name="small_grads_gather"), name="small_grads_sum")
    small = _unpack_small(total, grads[0])
    for n in WEIGHTS:
        if n in MATRICES:
            continue
        g = jnp.stack([s[n] for s in small])
        if n in ("gdn_a_log", "gdn_dt_bias"):
            g = g[:, 0, :GDN_HEADS]
        elif n in SMALL_SHARDED:
            width = w[n].shape[-1]
            g = lax.dynamic_slice_in_dim(g, chip * width, width, axis=2)
        grad[n] = g.reshape(w[n].shape)

    delta, new_m, new_v = {}, {}, {}
    for n in WEIGHTS:
        shape = w[n].shape
        two_d = (-1, shape[-1])
        d, nm, nv = _adamw(w[n].reshape(two_d), grad[n].reshape(two_d), m[n].reshape(two_d), v[n].reshape(two_d),
                           name="adamw_" + n)
        delta[n], new_m[n], new_v[n] = d.reshape(shape), nm.reshape(shape), nv.reshape(shape)

    loss = lax.psum(loss_row[0, 0], ("x", "y", "c"))
    return (loss, dx.reshape(x.shape), *[grad[n] for n in WEIGHTS], *[delta[n] for n in WEIGHTS],
            *[new_m[n] for n in WEIGHTS], *[new_v[n] for n in WEIGHTS])
```

```python
import math

import jax
import jax.numpy as jnp
from jax import lax
from jax.experimental import pallas as pl
from jax.experimental.pallas import tpu as pltpu

F32 = jnp.float32
BF16 = jnp.bfloat16
BS = pl.BlockSpec
SDS = jax.ShapeDtypeStruct
PRECISE = lax.Precision.HIGH

D_MODEL = 1024
DEPTH = 4
EPS = 1e-6
ATTN_HEADS = 4
ATTN_HEAD_DIM = 64
ATTN_WIDTH = 256
ROPE_THETA = 500000.0
ROPE_DIM = 16
CONV_WIDTH = 256
CONV_K = 3
GDN_HEADS = 4
GDN_HEAD_DIM = 128
GDN_WIDTH = 512
GDN_CONV_K = 4
GDN_CHUNK = 64
IN_WIDTH = 3592
XATTN_HEADS = 4
XATTN_HEAD_DIM = 256
FFN_HIDDEN = 2816
ADAM_LR = 0.001
ADAM_B1 = 0.9
ADAM_B2 = 0.999
ADAM_EPS = 1e-08
ADAM_WD = 0.01
ADAM_STEP = 10

IN_PAD = 3840
COL_GDN = 1536
COL_GATE = 3072
COL_AB = 3584

VMEM_LIMIT_V7X = 56 * 1024 * 1024
LANE = 128


def _cp(**kw):
    return pltpu.CompilerParams(vmem_limit_bytes=VMEM_LIMIT_V7X, **kw)


def _tile(n, cap):
    if n <= cap:
        return n
    best = None
    for t in range(LANE, cap + 1, LANE):
        if n % t == 0:
            best = t
    assert best is not None, (n, cap)
    return best


def _dot(a, b, ca, cb, precise=False):
    dims = (((ca,), (cb,)), ((), ()))
    if precise:
        return lax.dot_general(a.astype(F32), b.astype(F32), dims, precision=PRECISE,
                               preferred_element_type=F32)
    return lax.dot_general(a.astype(BF16), b.astype(BF16), dims, preferred_element_type=F32)


def _sigmoid(x):
    return 1.0 / (1.0 + jnp.exp(-x))


MM_ROWS = 1408
MM_BLOCK_BYTES = 6 * 1024 * 1024
MM_A_BYTES = 8 * 1024 * 1024


def _mm_tn(width, k, itemsize):
    if k * width * itemsize <= MM_BLOCK_BYTES:
        return width
    return _tile(width, max(LANE, min(1024, MM_BLOCK_BYTES // (k * itemsize) // LANE * LANE)))


def _mm(a, b, *, ta=False, tb=False, out_dtype=F32, b_shards=False, out_shards=False, halves=False, name):
    if halves and tb:
        m, k = a.shape[1], 2 * a.shape[2]
    else:
        m, k = (a.shape[1], a.shape[0]) if ta else a.shape
    tm = _tile(m, MM_ROWS)
    ca = 0 if ta else 1

    if b_shards and tb:
        ns, n, c = b.shape
        assert k == ns * c and not ta
        tn = _tile(n, max(LANE, min(1024, MM_BLOCK_BYTES // (k * b.dtype.itemsize) // LANE * LANE)))

        def a_block(a_ref, s):
            if halves:
                per = ns // 2
                return a_ref[s // per, :, (s % per) * c:(s % per + 1) * c]
            return a_ref[:, s * c:(s + 1) * c]

        def body(a_ref, b_ref, o_ref):
            acc = _dot(a_block(a_ref, 0), b_ref[0], 1, 1)
            for s in range(1, ns):
                acc = acc + _dot(a_block(a_ref, s), b_ref[s], 1, 1)
            o_ref[...] = acc.astype(out_dtype)

        b_spec = BS((ns, tn, c), lambda i, j: (0, j, 0))
    else:
        if b_shards:
            ns, kb, c = b.shape
            n = ns * c
            tn = _mm_tn(c, k, b.dtype.itemsize)
            nb = c // tn
            b_spec = BS((None, k, tn), lambda i, j: (j // nb, 0, j % nb))
        elif halves:
            kb, n = b.shape[1], 2 * b.shape[2]
            c = n // 4
            tn = _mm_tn(c, k, b.dtype.itemsize)
            nb = c // tn
            b_spec = BS((None, k, tn), lambda i, j: (j // (2 * nb), 0, j % (2 * nb)))
        else:
            kb, n = (b.shape[1], b.shape[0]) if tb else b.shape
            c = n // 4 if out_shards else n
            tn = _mm_tn(c, k, b.dtype.itemsize)
            nb = c // tn
            b_spec = BS((tn, k), lambda i, j: (j, 0)) if tb else BS((k, tn), lambda i, j: (0, j))
        assert kb == k
        cb = 1 if tb else 0

        def body(a_ref, b_ref, o_ref):
            o_ref[...] = _dot(a_ref[...], b_ref[...], ca, cb).astype(out_dtype)

    out_bytes = jnp.dtype(out_dtype).itemsize
    while (tm > 256 and tm % 256 == 0 and
           (tm * k * a.dtype.itemsize > MM_A_BYTES or tm * tn * out_bytes > MM_BLOCK_BYTES)):
        tm //= 2
    if halves and tb:
        a_spec = BS((2, tm, k // 2), lambda i, j: (0, i, 0))
    else:
        a_spec = BS((k, tm), lambda i, j: (0, i)) if ta else BS((tm, k), lambda i, j: (i, 0))
    if out_shards:
        out_spec = BS((None, tm, tn), lambda i, j: (j // nb, i, j % nb))
        out_shape = SDS((4, m, n // 4), out_dtype)
    elif halves and not (ta or tb):
        out_spec = BS((None, tm, tn), lambda i, j: (j // (2 * nb), i, j % (2 * nb)))
        out_shape = SDS((2, m, n // 2), out_dtype)
    else:
        out_spec = BS((tm, tn), lambda i, j: (i, j))
        out_shape = SDS((m, n), out_dtype)
    return pl.pallas_call(
        body, grid=(m // tm, n // tn), in_specs=[a_spec, b_spec], out_specs=out_spec, out_shape=out_shape,
        compiler_params=_cp(), name=name)(a, b)


def _rmsnorm(x, w, *, name):
    r, d = x.shape
    tr = _tile(r, 512)

    def body(x_ref, w_ref, o_ref):
        xv = x_ref[...]
        rs = lax.rsqrt(jnp.mean(xv * xv, axis=-1, keepdims=True) + EPS)
        o_ref[...] = (xv * rs * w_ref[...]).astype(BF16)

    return pl.pallas_call(
        body, grid=(r // tr,), in_specs=[BS((tr, d), lambda i: (i, 0)), BS((1, d), lambda i: (0, 0))],
        out_specs=BS((tr, d), lambda i: (i, 0)), out_shape=SDS((r, d), BF16),
        compiler_params=_cp(), name=name)(x, w)


def _resnorm(h, m, w, *, name):
    r, d = h.shape
    tr = _tile(r, 512)

    def body(h_ref, m_ref, w_ref, o_ref):
        mv = m_ref[...]
        rs = lax.rsqrt(jnp.mean(mv * mv, axis=-1, keepdims=True) + EPS)
        o_ref[...] = h_ref[...] + mv * rs * w_ref[...]

    row = BS((tr, d), lambda i: (i, 0))
    return pl.pallas_call(
        body, grid=(r // tr,), in_specs=[row, row, BS((1, d), lambda i: (0, 0))],
        out_specs=row, out_shape=SDS((r, d), F32), compiler_params=_cp(), name=name)(h, m, w)


def _rmsnorm_bwd(x, w, dy, res=None, *, name):
    r, d = x.shape
    tr = _tile(r, 512)
    has_res = res is not None

    def body(*refs):
        if has_res:
            x_ref, w_ref, dy_ref, res_ref, dx_ref, dw_ref = refs
        else:
            x_ref, w_ref, dy_ref, dx_ref, dw_ref = refs
        xv = x_ref[...]
        dyv = dy_ref[...].astype(F32)
        rs = lax.rsqrt(jnp.mean(xv * xv, axis=-1, keepdims=True) + EPS)
        nv = xv * rs
        dyw = dyv * w_ref[...]
        dx = rs * (dyw - nv * jnp.mean(dyw * nv, axis=-1, keepdims=True))
        if has_res:
            dx = dx + res_ref[...]
        dx_ref[...] = dx

        @pl.when(pl.program_id(0) == 0)
        def _():
            dw_ref[...] = jnp.zeros_like(dw_ref)

        dw_ref[...] += jnp.sum(dyv * nv, axis=0, keepdims=True)

    row = BS((tr, d), lambda i: (i, 0))
    vec = BS((1, d), lambda i: (0, 0))
    ins = [x, w, dy] + ([res] if has_res else [])
    return pl.pallas_call(
        body, grid=(r // tr,), in_specs=[row, vec, row] + ([row] if has_res else []),
        out_specs=[row, vec], out_shape=[SDS((r, d), F32), SDS((1, d), F32)],
        compiler_params=_cp(), name=name)(*ins)


def _swiglu(gu, *, name):
    _, r, hid = gu.shape
    tr, tc = _tile(r, 512), _tile(hid, 1408)

    def body(gu_ref, o_ref):
        g = gu_ref[0]
        o_ref[...] = (g * _sigmoid(g) * gu_ref[1]).astype(BF16)

    return pl.pallas_call(
        body, grid=(r // tr, hid // tc), in_specs=[BS((2, tr, tc), lambda i, j: (0, i, j))],
        out_specs=BS((tr, tc), lambda i, j: (i, j)), out_shape=SDS((r, hid), BF16),
        compiler_params=_cp(), name=name)(gu)


def _swiglu_bwd(gu, dact, *, name):
    _, r, hid = gu.shape
    tr, tc = _tile(r, 512), _tile(hid, 1408)

    def body(gu_ref, d_ref, o_ref):
        g = gu_ref[0]
        da = d_ref[...]
        sg = _sigmoid(g)
        o_ref[0] = (da * gu_ref[1] * sg * (1.0 + g * (1.0 - sg))).astype(BF16)
        o_ref[1] = (da * g * sg).astype(BF16)

    blk = BS((2, tr, tc), lambda i, j: (0, i, j))
    return pl.pallas_call(
        body, grid=(r // tr, hid // tc), in_specs=[blk, BS((tr, tc), lambda i, j: (i, j))],
        out_specs=blk, out_shape=SDS((2, r, hid), BF16), compiler_params=_cp(), name=name)(gu, dact)


def _loss_grad(h, target, *, name):
    r, d = h.shape
    tr = _tile(r, 512)

    def body(h_ref, t_ref, l_ref, g_ref):
        e = h_ref[...] - t_ref[...]
        g_ref[...] = e * (1.0 / d)

        @pl.when(pl.program_id(0) == 0)
        def _():
            l_ref[...] = jnp.zeros_like(l_ref)

        l_ref[...] += jnp.full((1, LANE), 0.5 / d, F32) * jnp.sum(e * e)

    row = BS((tr, d), lambda i: (i, 0))
    return pl.pallas_call(
        body, grid=(r // tr,), in_specs=[row, row],
        out_specs=[BS((1, LANE), lambda i: (0, 0)), row],
        out_shape=[SDS((1, LANE), F32), SDS((r, d), F32)], compiler_params=_cp(), name=name)(h, target)


def _adamw(w, g, m, v, *, name):
    r, c = w.shape
    tr = r if r <= 512 else _rows_tile(r)
    bc1 = 1.0 - ADAM_B1 ** ADAM_STEP
    bc2 = 1.0 - ADAM_B2 ** ADAM_STEP

    def body(w_ref, g_ref, m_ref, v_ref, d_ref, nm_ref, nv_ref):
        gv = g_ref[...]
        nm = ADAM_B1 * m_ref[...] + (1.0 - ADAM_B1) * gv
        nv = ADAM_B2 * v_ref[...] + (1.0 - ADAM_B2) * (gv * gv)
        d_ref[...] = -ADAM_LR * ((nm / bc1) / (jnp.sqrt(nv / bc2) + ADAM_EPS) + ADAM_WD * w_ref[...])
        nm_ref[...] = nm
        nv_ref[...] = nv

    blk = BS((tr, c), lambda i: (i, 0))
    return pl.pallas_call(
        body, grid=(r // tr,), in_specs=[blk] * 4, out_specs=[blk] * 3,
        out_shape=[SDS((r, c), F32)] * 3, compiler_params=_cp(), name=name)(w, g, m, v)


def _rows_tile(r):
    for t in (512, 256, 128, 64, 32, 16, 8):
        if r % t == 0:
            return t
    return r


def _rope_tables(pos, *, name):
    s = pos.shape[0]
    half = ROPE_DIM // 2

    def body(p_ref, c_ref, a_ref, b_ref):
        lane = lax.broadcasted_iota(jnp.int32, (s, ATTN_WIDTH), 1) & (ATTN_HEAD_DIM - 1)
        fi = (lane & (half - 1)).astype(F32)
        inv_freq = jnp.exp(fi * (-2.0 * math.log(ROPE_THETA) / ROPE_DIM))
        ang = p_ref[...].astype(F32) * inv_freq
        cs, sn = jnp.cos(ang), jnp.sin(ang)
        c_ref[...] = jnp.where(lane < ROPE_DIM, cs, 1.0)
        a_ref[...] = jnp.where(lane < half, -sn, 0.0)
        b_ref[...] = jnp.where((lane >= half) & (lane < ROPE_DIM), sn, 0.0)

    full = BS((s, ATTN_WIDTH), lambda i: (0, 0))
    return pl.pallas_call(
        body, grid=(1,), in_specs=[BS((s, 1), lambda i: (0, 0))], out_specs=[full] * 3,
        out_shape=[SDS((s, ATTN_WIDTH), F32)] * 3, compiler_params=_cp(), name=name)(pos)


def _rot(x, c, a, b):
    w = x.shape[1]
    return x * c + pltpu.roll(x, w - ROPE_DIM // 2, 1) * a + pltpu.roll(x, ROPE_DIM // 2, 1) * b


def _rot_t(dy, c, a, b):
    w = dy.shape[1]
    return dy * c + pltpu.roll(dy * a, ROPE_DIM // 2, 1) + pltpu.roll(dy * b, w - ROPE_DIM // 2, 1)


def _attn_count(q0, tq, s):
    dist = (lax.broadcasted_iota(jnp.int32, (tq, s), 0) + q0) - lax.broadcasted_iota(jnp.int32, (tq, s), 1)
    cnt = ((dist <= 128).astype(F32) + (((dist & 3) == 0) & (dist <= 512)).astype(F32)
           + ((dist & 15) == 0).astype(F32))
    return jnp.where(dist >= 0, cnt, 0.0)


ATTN_TQ = 256


def _attn_specs(s, tq):
    def qblk(col):
        return BS((tq, ATTN_WIDTH), lambda i: (i, col))

    def full(col):
        return BS((s, ATTN_WIDTH), lambda i: (0, col))

    return qblk, full


ATTN_TK = 512


def _attn_chunk(i, c, tq, k_ref, v_ref, ck, ak, bk):
    ks = pl.ds(pl.multiple_of(c * ATTN_TK, ATTN_TK), ATTN_TK)
    k = _rot(k_ref[ks, :], ck[ks, :], ak[ks, :], bk[ks, :]).astype(BF16)
    v = v_ref[ks, :].astype(BF16)
    cnt = _attn_count(i * tq - c * ATTN_TK, tq, ATTN_TK)
    return ks, k, v, cnt


def _attn_flash_fwd(proj, tabs, *, name):
    s = proj.shape[0]
    tq = ATTN_TQ
    qblk, full = _attn_specs(s, tq)
    scale = ATTN_HEAD_DIM ** -0.5
    nh = ATTN_HEADS

    def body(q_ref, k_ref, v_ref, cq, aq, bq, ck, ak, bk, o_ref, lse_ref):
        i = pl.program_id(0)
        q = _rot(q_ref[...], cq[...], aq[...], bq[...]) * scale
        head = lax.broadcasted_iota(jnp.int32, (1, ATTN_WIDTH), 1) >> 6
        hms = [(head == h).astype(F32) for h in range(nh)]
        qms = [(q * hm).astype(BF16) for hm in hms]

        def step(c, carry):
            ms, ls, acc = carry
            _, k, v, cnt = _attn_chunk(i, c, tq, k_ref, v_ref, ck, ak, bk)
            valid = cnt > 0.0
            new_ms, new_ls = [], []
            scale_acc = jnp.zeros((tq, ATTN_WIDTH), F32)
            add = jnp.zeros((tq, ATTN_WIDTH), F32)
            for h in range(nh):
                sc = _dot(qms[h], k, 1, 1)
                m_new = jnp.maximum(ms[h], jnp.max(jnp.where(valid, sc, -1e30), axis=-1, keepdims=True))
                alpha = jnp.exp(ms[h] - m_new)
                p = cnt * jnp.exp(jnp.minimum(sc - m_new, 0.0))
                new_ms.append(m_new)
                new_ls.append(alpha * ls[h] + jnp.sum(p, axis=-1, keepdims=True))
                scale_acc = scale_acc + alpha * hms[h]
                add = add + _dot(p, v, 1, 0) * hms[h]
            return new_ms, new_ls, acc * scale_acc + add

        init = ([jnp.full((tq, 1), -1e30, F32)] * nh, [jnp.zeros((tq, 1), F32)] * nh,
                jnp.zeros((tq, ATTN_WIDTH), F32))
        ms, ls, acc = lax.fori_loop(0, i // (ATTN_TK // tq) + 1, step, init)
        inv = jnp.zeros((tq, ATTN_WIDTH), F32)
        lane = lax.broadcasted_iota(jnp.int32, (tq, LANE), 1)
        lse = jnp.zeros((tq, LANE), F32)
        for h in range(nh):
            inv = inv + (1.0 / ls[h]) * hms[h]
            lse = jnp.where(lane == h, ms[h] + jnp.log(ls[h]), lse)
        o_ref[...] = (acc * inv).astype(BF16)
        lse_ref[...] = lse

    return pl.pallas_call(
        body, grid=(s // tq,),
        in_specs=[qblk(0), full(1), full(2), qblk(0), qblk(0), qblk(0), full(0), full(0), full(0)],
        out_specs=[BS((tq, ATTN_WIDTH), lambda i: (i, 0)), BS((tq, LANE), lambda i: (i, 0))],
        out_shape=[SDS((s, ATTN_WIDTH), BF16), SDS((s, LANE), F32)],
        compiler_params=_cp(), name=name)(proj, proj, proj, *tabs, *tabs)


def _attn_flash_bwd(proj, tabs, cat, lse, dcat, *, name):
    s = proj.shape[0]
    tq = ATTN_TQ
    nq = s // tq
    qblk, full = _attn_specs(s, tq)
    scale = ATTN_HEAD_DIM ** -0.5
    nh = ATTN_HEADS

    def body(q_ref, k_ref, v_ref, cq, aq, bq, ck, ak, bk, y_ref, lse_ref, dy_ref,
             dq_ref, dk_ref, dv_ref, dk_acc, dv_acc):
        i = pl.program_id(0)

        @pl.when(i == 0)
        def _():
            dk_acc[...] = jnp.zeros_like(dk_acc)
            dv_acc[...] = jnp.zeros_like(dv_acc)

        q = _rot(q_ref[...], cq[...], aq[...], bq[...]) * scale
        dy = dy_ref[...].astype(F32)
        prod = dy * y_ref[...].astype(F32)
        lse_all = lse_ref[...]
        head = lax.broadcasted_iota(jnp.int32, (1, ATTN_WIDTH), 1) >> 6
        hms = [(head == h).astype(F32) for h in range(nh)]
        qms = [(q * hm).astype(BF16) for hm in hms]
        dyms = [(dy * hm).astype(BF16) for hm in hms]
        deltas = [jnp.sum(prod * hm, axis=-1, keepdims=True) for hm in hms]
        lses = [lse_all[:, h:h + 1] for h in range(nh)]

        def step(c, dq):
            ks, k, v, cnt = _attn_chunk(i, c, tq, k_ref, v_ref, ck, ak, bk)
            dk_c = jnp.zeros((ATTN_TK, ATTN_WIDTH), F32)
            dv_c = jnp.zeros((ATTN_TK, ATTN_WIDTH), F32)
            for h in range(nh):
                sc = _dot(qms[h], k, 1, 1)
                p = cnt * jnp.exp(jnp.minimum(sc - lses[h], 0.0))
                dp = _dot(dyms[h], v, 1, 1)
                ds = p * (dp - deltas[h])
                dq = dq + _dot(ds, k, 1, 0) * hms[h]
                dk_c = dk_c + _dot(ds, qms[h], 0, 0)
                dv_c = dv_c + _dot(p, dyms[h], 0, 0)
            dk_acc[ks, :] += dk_c
            dv_acc[ks, :] += dv_c
            return dq

        dq = lax.fori_loop(0, i // (ATTN_TK // tq) + 1, step, jnp.zeros((tq, ATTN_WIDTH), F32))
        dq_ref[...] = _rot_t(dq * scale, cq[...], aq[...], bq[...]).astype(BF16)

        @pl.when(i == nq - 1)
        def _():
            dk_ref[...] = _rot_t(dk_acc[...], ck[...], ak[...], bk[...]).astype(BF16)
            dv_ref[...] = dv_acc[...].astype(BF16)

    whole = BS((s, ATTN_WIDTH), lambda i: (0, 0))
    return pl.pallas_call(
        body, grid=(nq,),
        in_specs=[qblk(0), full(1), full(2), qblk(0), qblk(0), qblk(0), full(0), full(0), full(0),
                  qblk(0), BS((tq, LANE), lambda i: (i, 0)), qblk(0)],
        out_specs=[BS((tq, ATTN_WIDTH), lambda i: (i, 0)), whole, whole],
        out_shape=[SDS((s, ATTN_WIDTH), BF16)] * 3,
        scratch_shapes=[pltpu.VMEM((s, ATTN_WIDTH), F32), pltpu.VMEM((s, ATTN_WIDTH), F32)],
        compiler_params=_cp(), name=name)(proj, proj, proj, *tabs, *tabs, cat, lse, dcat)


def _shift_down(x, n):
    if n == 0:
        return x
    rows = lax.broadcasted_iota(jnp.int32, x.shape, 0)
    return jnp.where(rows >= n, pltpu.roll(x, n, 0), 0.0)


def _shift_up(x, n):
    if n == 0:
        return x
    t = x.shape[0]
    rows = lax.broadcasted_iota(jnp.int32, x.shape, 0)
    return jnp.where(rows < t - n, pltpu.roll(x, t - n, 0), 0.0)


def _conv_fwd(z, w, kk):
    y = z * w[kk - 1:kk, :]
    for j in range(kk - 1):
        y = y + _shift_down(z, kk - 1 - j) * w[j:j + 1, :]
    return y


def _conv_bwd(z, dy, w, kk):
    dz = dy * w[kk - 1:kk, :]
    dws = []
    for j in range(kk - 1):
        dz = dz + _shift_up(dy, kk - 1 - j) * w[j:j + 1, :]
        dws.append(jnp.sum(dy * _shift_down(z, kk - 1 - j), axis=0, keepdims=True))
    dws.append(jnp.sum(dy * z, axis=0, keepdims=True))
    return dz, jnp.concatenate(dws, axis=0)


def _sconv_fwd(proj, w, *, name):
    s = proj.shape[0]

    def body(b_ref, c_ref, x_ref, w_ref, o_ref):
        y = _conv_fwd(c_ref[...] * x_ref[...], w_ref[...], CONV_K)
        o_ref[...] = (b_ref[...] * y).astype(BF16)

    def col(j):
        return BS((s, LANE), lambda i: (0, j + i))

    return pl.pallas_call(
        body, grid=(CONV_WIDTH // LANE,), in_specs=[col(6), col(8), col(10), BS((CONV_K, LANE), lambda i: (0, i))],
        out_specs=BS((s, LANE), lambda i: (0, i)), out_shape=SDS((s, CONV_WIDTH), BF16),
        compiler_params=_cp(), name=name)(proj, proj, proj, w)


def _sconv_bwd(proj, w, dcat, *, name):
    s = proj.shape[0]

    def body(b_ref, c_ref, x_ref, w_ref, dy_ref, db_ref, dc_ref, dx_ref, dw_ref):
        cv, xv, wv = c_ref[...], x_ref[...], w_ref[...]
        dy = dy_ref[...].astype(F32)
        z = cv * xv
        db_ref[...] = (dy * _conv_fwd(z, wv, CONV_K)).astype(BF16)
        dz, dw = _conv_bwd(z, dy * b_ref[...], wv, CONV_K)
        dc_ref[...] = (dz * xv).astype(BF16)
        dx_ref[...] = (dz * cv).astype(BF16)
        dw_ref[...] = dw

    def col(j):
        return BS((s, LANE), lambda i: (0, j + i))

    out = BS((s, LANE), lambda i: (0, i))
    wspec = BS((CONV_K, LANE), lambda i: (0, i))
    return pl.pallas_call(
        body, grid=(CONV_WIDTH // LANE,), in_specs=[col(6), col(8), col(10), wspec, col(2)],
        out_specs=[out, out, out, wspec],
        out_shape=[SDS((s, CONV_WIDTH), BF16)] * 3 + [SDS((CONV_K, CONV_WIDTH), F32)],
        compiler_params=_cp(), name=name)(proj, proj, proj, w, dcat)


def _l2n(y, scale):
    r = lax.rsqrt(jnp.sum(y * y, axis=-1, keepdims=True) + EPS)
    return y * r * scale, r


def _gdn_pre_fwd(proj, w, *, name):
    s = proj.shape[0]
    nh = GDN_HEADS

    def body(x_ref, w_ref, o_ref):
        j = pl.program_id(0)
        c = _conv_fwd(x_ref[...], w_ref[...], GDN_CONV_K)
        y = c * _sigmoid(c)
        scale = jnp.where(j < nh, GDN_HEAD_DIM ** -0.5, 1.0)
        n, _ = _l2n(y, scale)
        o_ref[...] = jnp.where(j < 2 * nh, n, y)

    return pl.pallas_call(
        body, grid=(3 * nh,),
        in_specs=[BS((s, LANE), lambda j: (0, COL_GDN // LANE + j)), BS((GDN_CONV_K, LANE), lambda j: (0, j))],
        out_specs=BS((s, LANE), lambda j: (0, j)), out_shape=SDS((s, 3 * GDN_WIDTH), F32),
        compiler_params=_cp(), name=name)(proj, w)


def _gdn_pre_bwd(proj, w, dqkv, *, name):
    s = proj.shape[0]
    nh = GDN_HEADS

    def body(x_ref, w_ref, d_ref, dx_ref, dw_ref):
        j = pl.program_id(0)
        xv, wv, dn = x_ref[...], w_ref[...], d_ref[...]
        c = _conv_fwd(xv, wv, GDN_CONV_K)
        sg = _sigmoid(c)
        y = c * sg
        scale = jnp.where(j < nh, GDN_HEAD_DIM ** -0.5, 1.0)
        n, r = _l2n(y, 1.0)
        dns = dn * scale
        dy_norm = r * (dns - n * jnp.sum(dns * n, axis=-1, keepdims=True))
        dy = jnp.where(j < 2 * nh, dy_norm, dn)
        dc = dy * sg * (1.0 + c * (1.0 - sg))
        dx, dw = _conv_bwd(xv, dc, wv, GDN_CONV_K)
        dx_ref[...] = dx.astype(BF16)
        dw_ref[...] = dw

    wspec = BS((GDN_CONV_K, LANE), lambda j: (0, j))
    blk = BS((s, LANE), lambda j: (0, j))
    return pl.pallas_call(
        body, grid=(3 * nh,),
        in_specs=[BS((s, LANE), lambda j: (0, COL_GDN // LANE + j)), wspec, blk],
        out_specs=[blk, wspec], out_shape=[SDS((s, 3 * GDN_WIDTH), BF16), SDS((GDN_CONV_K, 3 * GDN_WIDTH), F32)],
        compiler_params=_cp(), name=name)(proj, w, dqkv)


def _softplus(x):
    return jnp.maximum(x, 0.0) + jnp.log(1.0 + jnp.exp(-jnp.abs(x)))


def _gdn_gates_fwd(proj, a_log, dt_bias, *, name):
    s = proj.shape[0]

    def body(x_ref, al_ref, dt_ref, o_ref):
        xv = x_ref[...]
        lane = lax.broadcasted_iota(jnp.int32, xv.shape, 1)
        g = -jnp.exp(al_ref[...]) * _softplus(xv + dt_ref[...])
        o_ref[...] = jnp.where(lane < GDN_HEADS, g, jnp.where(lane < 2 * GDN_HEADS, _sigmoid(xv), 0.0))

    vec = BS((1, LANE), lambda i: (0, 0))
    return pl.pallas_call(
        body, grid=(1,), in_specs=[BS((s, LANE), lambda i: (0, COL_AB // LANE)), vec, vec],
        out_specs=BS((s, LANE), lambda i: (0, 0)), out_shape=SDS((s, LANE), F32),
        compiler_params=_cp(), name=name)(proj, a_log, dt_bias)


def _gdn_gates_bwd(proj, a_log, dt_bias, dgb, *, name):
    s = proj.shape[0]

    def body(x_ref, al_ref, dt_ref, d_ref, dx_ref, dal_ref, ddt_ref):
        xv, dv = x_ref[...], d_ref[...]
        lane = lax.broadcasted_iota(jnp.int32, xv.shape, 1)
        is_g = lane < GDN_HEADS
        ea = -jnp.exp(al_ref[...])
        z = xv + dt_ref[...]
        da = jnp.where(is_g, dv * ea * _sigmoid(z), 0.0)
        beta = _sigmoid(xv)
        dx_ref[...] = jnp.where(is_g, da, jnp.where(lane < 2 * GDN_HEADS, dv * beta * (1.0 - beta), 0.0)).astype(BF16)
        dal_ref[...] = jnp.sum(jnp.where(is_g, dv * ea * _softplus(z), 0.0), axis=0, keepdims=True)
        ddt_ref[...] = jnp.sum(da, axis=0, keepdims=True)

    vec = BS((1, LANE), lambda i: (0, 0))
    blk = BS((s, LANE), lambda i: (0, 0))
    return pl.pallas_call(
        body, grid=(1,), in_specs=[BS((s, LANE), lambda i: (0, COL_AB // LANE)), vec, vec, blk],
        out_specs=[blk, vec, vec], out_shape=[SDS((s, LANE), BF16), SDS((1, LANE), F32), SDS((1, LANE), F32)],
        compiler_params=_cp(), name=name)(proj, a_log, dt_bias, dgb)


def _col_to_row(col, eye):
    return jnp.sum(jnp.where(eye, col, 0.0), axis=0, keepdims=True)


def _row_to_col(row, eye):
    return jnp.sum(jnp.where(eye, row, 0.0), axis=1, keepdims=True)


GDN_GROUP = 4
TRI_BLOCK_SHIFT = 4


def _gdn_masks(c):
    row = lax.broadcasted_iota(jnp.int32, (c, c), 0)
    col = lax.broadcasted_iota(jnp.int32, (c, c), 1)
    return dict(row=row, col=col, eye=row == col, low=row >= col, strict=row > col, upper=row <= col,
                on_diag=(row >> TRI_BLOCK_SHIFT) == (col >> TRI_BLOCK_SHIFT))


def _tri_inv(a_list, mk):
    eye_f = mk["eye"].astype(F32)
    ds = [jnp.where(mk["on_diag"], a, 0.0) for a in a_list]
    xs = [eye_f - d for d in ds]
    ps = ds
    for _ in range(3):
        ps = [_dot(p, p, 1, 0, precise=True) for p in ps]
        xs = [x + _dot(x, p, 1, 0, precise=True) for x, p in zip(xs, ps)]
    ms = [_dot(x, a - d, 1, 0, precise=True) for x, a, d in zip(xs, a_list, ds)]
    m2s = [_dot(m, m, 1, 0, precise=True) for m in ms]
    ys = [eye_f - m for m in ms]
    ys = [y + _dot(y, m2, 1, 0, precise=True) for y, m2 in zip(ys, m2s)]
    return [_dot(y, x, 1, 0, precise=True) for y, x in zip(ys, xs)]


def _gdn_pre(qs, ks, vs, gs, betas, mk, ts=None):
    c, hd = qs[0].shape
    eye, low = mk["eye"], mk["low"]
    g_rows = [_col_to_row(g, eye) for g in gs]
    d_cols = [jnp.sum(jnp.where(low, gr, 0.0), axis=1, keepdims=True) for gr in g_rows]
    d_rows = [jnp.sum(jnp.where(mk["upper"], g, 0.0), axis=0, keepdims=True) for g in gs]
    rels = [jnp.where(low, jnp.exp(jnp.minimum(dc - dr, 0.0)), 0.0) for dc, dr in zip(d_cols, d_rows)]
    d_lasts = [dc[c - 1:c, :] for dc in d_cols]
    es = [jnp.exp(dc) for dc in d_cols]
    fs = [jnp.exp(dl - dc) for dl, dc in zip(d_lasts, d_cols)]
    cds = [jnp.exp(dl) for dl in d_lasts]
    kbs = [k * b for k, b in zip(ks, betas)]
    kbqs = [jnp.concatenate([kb, q], axis=0) for kb, q in zip(kbs, qs)]
    kqk = [_dot(kbq, k, 1, 1) for kbq, k in zip(kbqs, ks)]
    kks = [x[:c, :] for x in kqk]
    qks = [x[c:, :] for x in kqk]
    if ts is None:
        ts = _tri_inv([jnp.where(mk["strict"], kk * rel, 0.0) for kk, rel in zip(kks, rels)], mk)
    vbs = [v * b for v, b in zip(vs, betas)]
    kbes = [kb * e for kb, e in zip(kbs, es)]
    uws = [_dot(t, jnp.concatenate([vb, kbe], axis=1), 1, 0) for t, vb, kbe in zip(ts, vbs, kbes)]
    out = []
    for i in range(len(qs)):
        out.append(dict(rel=rels[i], e=es[i], f=fs[i], cd=cds[i], kb=kbs[i], kbq=kbqs[i], kk=kks[i], qk=qks[i],
                        t=ts[i], u=uws[i][:, :hd], w=uws[i][:, hd:], uw=uws[i], attn=qks[i] * rels[i],
                        qd=qs[i] * es[i], kd=ks[i] * fs[i]))
    return out


def _gdn_apply(pres, sts, leaving=True):
    c = pres[0]["u"].shape[0]
    wqs = [_dot(jnp.concatenate([p["w"], p["qd"]], axis=0), st, 1, 0) for p, st in zip(pres, sts)]
    vns = [p["u"] - x[:c, :] for p, x in zip(pres, wqs)]
    os_ = [x[c:, :] + _dot(p["attn"], vn, 1, 0) for p, x, vn in zip(pres, wqs, vns)]
    if not leaving:
        return vns, os_, None
    new = [p["cd"] * st + _dot(p["kd"], vn, 0, 0) for p, st, vn in zip(pres, sts, vns)]
    return vns, os_, new


def _gdn_bwd_rest(qs, ks, vs, betas, sts, pres, vns, dos, dvns, dsts, mk):
    c, hd = qs[0].shape
    eye = mk["eye"]
    n = range(len(qs))
    dkds = [_dot(vns[i], dsts[i], 1, 1) for i in n]
    dcds = [jnp.sum(sts[i] * dsts[i]) for i in n]
    dattns = [jnp.where(mk["low"], _dot(dos[i], vns[i], 1, 1), 0.0) for i in n]
    dqdws = [_dot(jnp.concatenate([dos[i], -dvns[i]], axis=0), sts[i], 1, 1) for i in n]
    dqds = [x[:c, :] for x in dqdws]
    dws = [x[c:, :] for x in dqdws]
    dvks = [_dot(pres[i]["t"], jnp.concatenate([dvns[i], dws[i]], axis=1), 0, 0) for i in n]
    das = [jnp.where(mk["strict"], -_dot(dvks[i], pres[i]["uw"], 1, 1), 0.0) for i in n]
    dkqs = [jnp.concatenate([das[i] * pres[i]["rel"], dattns[i] * pres[i]["rel"]], axis=0) for i in n]
    dkbdq = [_dot(dkqs[i], ks[i], 1, 0) for i in n]
    dk0 = [_dot(dkqs[i], pres[i]["kbq"], 0, 0) for i in n]
    out = []
    rows1 = lax.broadcasted_iota(jnp.int32, (c, 1), 0)
    for i in n:
        p = pres[i]
        dvb, dkbe = dvks[i][:, :hd], dvks[i][:, hd:]
        grel = (das[i] * p["kk"] + dattns[i] * p["qk"]) * p["rel"]
        dkb = dkbdq[i][:c, :] + dkbe * p["e"]
        dk = dk0[i] + dkds[i] * p["f"] + dkb * betas[i]
        dq = dkbdq[i][c:, :] + dqds[i] * p["e"]
        dv = dvb * betas[i]
        dbeta = jnp.sum(dkb * ks[i], axis=1, keepdims=True) + jnp.sum(dvb * vs[i], axis=1, keepdims=True)
        de = jnp.sum(dqds[i] * qs[i], axis=1, keepdims=True) + jnp.sum(dkbe * p["kb"], axis=1, keepdims=True)
        dff = jnp.sum(dkds[i] * ks[i], axis=1, keepdims=True) * p["f"]
        dd = (de * p["e"] - dff + jnp.sum(grel, axis=1, keepdims=True)
              - _row_to_col(jnp.sum(grel, axis=0, keepdims=True), eye))
        dd = dd + jnp.where(rows1 == c - 1, jnp.sum(dff) + dcds[i] * p["cd"], 0.0)
        dg = jnp.sum(jnp.where(mk["upper"], _col_to_row(dd, eye), 0.0), axis=1, keepdims=True)
        out.append((dq, dk, dv, dg, dbeta))
    return out


def _gdn_specs(c):
    def qkv(j):
        return BS((c, GDN_WIDTH), lambda n: (n, j))

    return qkv


def _gdn_core_fwd(qkv, gbeta, proj, norm_w, *, name):
    s = qkv.shape[0]
    c, nh, hd, grp = GDN_CHUNK, GDN_HEADS, GDN_HEAD_DIM, GDN_GROUP
    n_chunks = s // c
    blk = _gdn_specs(grp * c)
    inst = [(sub, h) for sub in range(grp) for h in range(nh)]

    def body(q_ref, k_ref, v_ref, gb_ref, gate_ref, nw_ref, y_ref, st_ref, t_ref, state):
        @pl.when(pl.program_id(0) == 0)
        def _():
            state[...] = jnp.zeros_like(state)

        mk = _gdn_masks(c)
        rows = [slice(sub * c, (sub + 1) * c) for sub in range(grp)]
        lanes = [slice(h * hd, (h + 1) * hd) for h in range(nh)]
        gbs = [gb_ref[r, :] for r in rows]
        pres = _gdn_pre([q_ref[rows[sub], lanes[h]] for sub, h in inst], [k_ref[rows[sub], lanes[h]] for sub, h in inst],
                        [v_ref[rows[sub], lanes[h]] for sub, h in inst], [gbs[sub][:, h:h + 1] for sub, h in inst],
                        [gbs[sub][:, nh + h:nh + h + 1] for sub, h in inst], mk)
        sts = [state[ls, :] for ls in lanes]
        outs = []
        for sub in range(grp):
            for h in range(nh):
                st_ref[pl.ds((sub * nh + h) * hd, hd), :] = sts[h]
            _, os_, sts = _gdn_apply(pres[sub * nh:(sub + 1) * nh], sts)
            outs += os_
        for h in range(nh):
            state[lanes[h], :] = sts[h]
        nw = nw_ref[...]
        for i, (sub, h) in enumerate(inst):
            t_ref[pl.ds(i * c, c), :] = pres[i]["t"]
            o = outs[i]
            gate = gate_ref[rows[sub], lanes[h]]
            rs = lax.rsqrt(jnp.mean(o * o, axis=-1, keepdims=True) + EPS)
            y_ref[rows[sub], lanes[h]] = (o * rs * nw * (gate * _sigmoid(gate))).astype(BF16)

    return pl.pallas_call(
        body, grid=(n_chunks // grp,),
        in_specs=[blk(0), blk(1), blk(2), BS((grp * c, LANE), lambda n: (n, 0)),
                  BS((grp * c, GDN_WIDTH), lambda n: (n, COL_GATE // GDN_WIDTH)), BS((1, hd), lambda n: (0, 0))],
        out_specs=[BS((grp * c, GDN_WIDTH), lambda n: (n, 0)), BS((grp * nh * hd, hd), lambda n: (n, 0)),
                   BS((grp * nh * c, c), lambda n: (n, 0))],
        out_shape=[SDS((s, GDN_WIDTH), BF16), SDS((n_chunks * nh * hd, hd), F32), SDS((n_chunks * nh * c, c), F32)],
        scratch_shapes=[pltpu.VMEM((nh * hd, hd), F32)],
        compiler_params=_cp(), name=name)(qkv, qkv, qkv, gbeta, proj, norm_w)


def _gdn_core_bwd(qkv, gbeta, proj, norm_w, states, tinv, dcat, *, name):
    s = qkv.shape[0]
    c, nh, hd, grp = GDN_CHUNK, GDN_HEADS, GDN_HEAD_DIM, GDN_GROUP
    n_chunks = s // c
    last = n_chunks // grp - 1
    inst = [(sub, h) for sub in range(grp) for h in range(nh)]

    def rev(j, w):
        return BS((grp * c, w), lambda n: (last - n, j))

    def body(q_ref, k_ref, v_ref, gb_ref, gate_ref, nw_ref, st_ref, t_ref, dy_ref,
             dqkv_ref, dgb_ref, dgate_ref, dnw_ref, dstate):
        @pl.when(pl.program_id(0) == 0)
        def _():
            dstate[...] = jnp.zeros_like(dstate)
            dnw_ref[...] = jnp.zeros_like(dnw_ref)

        mk = _gdn_masks(c)
        rows = [slice(sub * c, (sub + 1) * c) for sub in range(grp)]
        lanes = [slice(h * hd, (h + 1) * hd) for h in range(nh)]
        gbs = [gb_ref[r, :] for r in rows]
        qs = [q_ref[rows[sub], lanes[h]] for sub, h in inst]
        ks = [k_ref[rows[sub], lanes[h]] for sub, h in inst]
        vs = [v_ref[rows[sub], lanes[h]] for sub, h in inst]
        betas = [gbs[sub][:, nh + h:nh + h + 1] for sub, h in inst]
        sts = [st_ref[pl.ds(i * hd, hd), :] for i in range(len(inst))]
        pres = _gdn_pre(qs, ks, vs, [gbs[sub][:, h:h + 1] for sub, h in inst], betas, mk,
                        ts=[t_ref[pl.ds(i * c, c), :] for i in range(len(inst))])
        vns, outs, _ = _gdn_apply(pres, sts, leaving=False)

        nw = nw_ref[...]
        dnw = jnp.zeros((1, hd), F32)
        dos = []
        for i, (sub, h) in enumerate(inst):
            o = outs[i]
            gate = gate_ref[rows[sub], lanes[h]]
            dy = dy_ref[rows[sub], lanes[h]].astype(F32)
            sg = _sigmoid(gate)
            rs = lax.rsqrt(jnp.mean(o * o, axis=-1, keepdims=True) + EPS)
            nrm = o * rs
            dgate_ref[rows[sub], lanes[h]] = (dy * nrm * nw * sg * (1.0 + gate * (1.0 - sg))).astype(BF16)
            dnv = dy * (gate * sg)
            dnw = dnw + jnp.sum(dnv * nrm, axis=0, keepdims=True)
            dno = dnv * nw
            dos.append(rs * (dno - nrm * jnp.mean(dno * nrm, axis=-1, keepdims=True)))
        dnw_ref[...] += dnw

        from_o = [_dot(p["attn"], do, 0, 0) for p, do in zip(pres, dos)]
        to_st = [_dot(p["qd"], do, 0, 0) for p, do in zip(pres, dos)]
        dst = [dstate[ls, :] for ls in lanes]
        dsts = [None] * len(inst)
        dvns = [None] * len(inst)
        for sub in reversed(range(grp)):
            idx = [sub * nh + h for h in range(nh)]
            for h, i in enumerate(idx):
                dsts[i] = dst[h]
                dvns[i] = from_o[i] + _dot(pres[i]["kd"], dst[h], 1, 0)
            dst = [pres[i]["cd"] * dst[h] + to_st[i] - _dot(pres[i]["w"], dvns[i], 0, 0) for h, i in enumerate(idx)]
        for h in range(nh):
            dstate[lanes[h], :] = dst[h]

        grads = _gdn_bwd_rest(qs, ks, vs, betas, sts, pres, vns, dos, dvns, dsts, mk)
        lane = lax.broadcasted_iota(jnp.int32, (c, LANE), 1)
        dgb = [jnp.zeros((c, LANE), F32) for _ in range(grp)]
        for (sub, h), (dq, dk, dv, dg, dbeta) in zip(inst, grads):
            dqkv_ref[rows[sub], lanes[h]] = dq
            dqkv_ref[rows[sub], slice(GDN_WIDTH + h * hd, GDN_WIDTH + (h + 1) * hd)] = dk
            dqkv_ref[rows[sub], slice(2 * GDN_WIDTH + h * hd, 2 * GDN_WIDTH + (h + 1) * hd)] = dv
            dgb[sub] = jnp.where(lane == h, dg, jnp.where(lane == nh + h, dbeta, dgb[sub]))
        for sub in range(grp):
            dgb_ref[rows[sub], :] = dgb[sub]

    return pl.pallas_call(
        body, grid=(n_chunks // grp,),
        in_specs=[rev(0, GDN_WIDTH), rev(1, GDN_WIDTH), rev(2, GDN_WIDTH), rev(0, LANE),
                  rev(COL_GATE // GDN_WIDTH, GDN_WIDTH), BS((1, hd), lambda n: (0, 0)),
                  BS((grp * nh * hd, hd), lambda n: (last - n, 0)), BS((grp * nh * c, c), lambda n: (last - n, 0)),
                  rev(1, GDN_WIDTH)],
        out_specs=[rev(0, 3 * GDN_WIDTH), rev(0, LANE), rev(0, GDN_WIDTH), BS((1, hd), lambda n: (0, 0))],
        out_shape=[SDS((s, 3 * GDN_WIDTH), F32), SDS((s, LANE), F32), SDS((s, GDN_WIDTH), BF16), SDS((1, hd), F32)],
        scratch_shapes=[pltpu.VMEM((nh * hd, hd), F32)],
        compiler_params=_cp(), name=name)(qkv, qkv, qkv, gbeta, proj, norm_w, states, tinv, dcat)


XATTN_TQ = 512


def _xattn_probs(qh, kh):
    sc = _dot(qh, kh, 1, 1) * (XATTN_HEAD_DIM ** -0.5)
    p = jnp.exp(sc - jnp.max(sc, axis=-1, keepdims=True))
    return p / jnp.sum(p, axis=-1, keepdims=True)


def _xattn_fwd(q, kv, *, name):
    s, d = q.shape
    m = kv.shape[0]
    tq, hd = _tile(s, XATTN_TQ), XATTN_HEAD_DIM

    def body(q_ref, k_ref, v_ref, o_ref):
        for h in range(XATTN_HEADS):
            ls = slice(h * hd, (h + 1) * hd)
            p = _xattn_probs(q_ref[:, ls], k_ref[:, ls])
            o_ref[:, ls] = _dot(p, v_ref[:, ls], 1, 0).astype(BF16)

    return pl.pallas_call(
        body, grid=(s // tq,),
        in_specs=[BS((tq, d), lambda i: (i, 0)), BS((m, d), lambda i: (0, 0)), BS((m, d), lambda i: (0, 1))],
        out_specs=BS((tq, d), lambda i: (i, 0)), out_shape=SDS((s, d), BF16),
        compiler_params=_cp(), name=name)(q, kv, kv)


def _xattn_bwd(q, kv, do, *, name):
    s, d = q.shape
    m = kv.shape[0]
    tq, hd = _tile(s, XATTN_TQ), XATTN_HEAD_DIM
    scale = hd ** -0.5

    def body(q_ref, k_ref, v_ref, do_ref, dq_ref, dkv_ref):
        @pl.when(pl.program_id(0) == 0)
        def _():
            dkv_ref[...] = jnp.zeros_like(dkv_ref)

        for h in range(XATTN_HEADS):
            ls = slice(h * hd, (h + 1) * hd)
            vs = slice(d + h * hd, d + (h + 1) * hd)
            qh, kh, doh = q_ref[:, ls], k_ref[:, ls], do_ref[:, ls]
            p = _xattn_probs(qh, kh)
            dp = _dot(doh, v_ref[:, ls], 1, 1)
            ds = p * (dp - jnp.sum(p * dp, axis=-1, keepdims=True)) * scale
            dq_ref[:, ls] = _dot(ds, kh, 1, 0).astype(BF16)
            dkv_ref[:, ls] += _dot(ds, qh, 0, 0)
            dkv_ref[:, vs] += _dot(p, doh, 0, 0)

    row = BS((tq, d), lambda i: (i, 0))
    return pl.pallas_call(
        body, grid=(s // tq,),
        in_specs=[row, BS((m, d), lambda i: (0, 0)), BS((m, d), lambda i: (0, 1)), row],
        out_specs=[row, BS((m, 2 * d), lambda i: (0, 0))],
        out_shape=[SDS((s, d), BF16), SDS((m, 2 * d), F32)],
        compiler_params=_cp(), name=name)(q, kv, kv, do)


def _pad_lanes(vec4):
    return jnp.zeros((1, LANE), F32).at[0, :GDN_HEADS].set(vec4)


def _layer_fwd(h0, mem, tabs, p):
    sv = dict(h0=h0)
    hn1 = _rmsnorm(h0, p["norm_mix_pre"], name="norm_mix_pre")
    proj = _mm(hn1, p["w_in"], name="mm_in")
    ya, lse = _attn_flash_fwd(proj, tabs, name="attn_fwd")
    yc = _sconv_fwd(proj, p["conv_short"], name="sconv_fwd")
    qkv = _gdn_pre_fwd(proj, p["conv_gdn"], name="gdn_pre_fwd")
    gbeta = _gdn_gates_fwd(proj, p["gdn_a_log"], p["gdn_dt_bias"], name="gdn_gates_fwd")
    yg, states, tinv = _gdn_core_fwd(qkv, gbeta, proj, p["gdn_norm"], name="gdn_core_fwd")
    cat = jnp.concatenate([ya, yc, yg], axis=-1)
    mix = _mm(cat, p["w_out"], name="mm_out")
    h1 = _resnorm(h0, mix, p["norm_mix_post"], name="norm_mix_post")
    hn2 = _rmsnorm(h1, p["norm_xattn_pre"], name="norm_xattn_pre")
    memn = _rmsnorm(mem, p["norm_mem"], name="norm_mem")
    xq = _mm(hn2, p["w_xq"], out_dtype=BF16, name="mm_xq")
    kv = _mm(memn, p["w_xkv"], out_dtype=BF16, b_shards=True, name="mm_xkv")
    xo = _xattn_fwd(xq, kv, name="xattn_fwd")
    xa = _mm(xo, p["w_xo"], name="mm_xo")
    h2 = _resnorm(h1, xa, p["norm_xattn_post"], name="norm_xattn_post")
    hn3 = _rmsnorm(h2, p["norm_ffn_pre"], name="norm_ffn_pre")
    gu = _mm(hn3, p["w_gate_up"], b_shards=True, halves=True, name="mm_gate_up")
    act = _swiglu(gu, name="swiglu_fwd")
    f = _mm(act, p["w_down"], name="mm_down")
    h3 = _resnorm(h2, f, p["norm_ffn_post"], name="norm_ffn_post")
    sv.update(hn1=hn1, proj=proj, lse=lse, qkv=qkv, gbeta=gbeta, states=states, tinv=tinv, cat=cat, mix=mix, h1=h1, hn2=hn2,
              memn=memn, xq=xq, kv=kv, xo=xo, xa=xa, h2=h2, hn3=hn3, gu=gu, act=act, f=f)
    return h3, sv


def _layer_bwd(dh3, mem, tabs, p, sv, after_ffn=None):
    g = {}
    df, g["norm_ffn_post"] = _rmsnorm_bwd(sv["f"], p["norm_ffn_post"], dh3, name="norm_ffn_post_bwd")
    dact = _mm(df, p["w_down"], tb=True, name="mm_down_da")
    g["w_down"] = _mm(sv["act"], df, ta=True, name="mm_down_dw")
    dgu = _swiglu_bwd(sv["gu"], dact, name="swiglu_bwd")
    dhn3 = _mm(dgu, p["w_gate_up"], tb=True, b_shards=True, halves=True, name="mm_gate_up_da")
    g["w_gate_up"] = _mm(sv["hn3"], dgu, ta=True, out_shards=True, halves=True, name="mm_gate_up_dw")
    dh2, g["norm_ffn_pre"] = _rmsnorm_bwd(sv["h2"], p["norm_ffn_pre"], dhn3, res=dh3, name="norm_ffn_pre_bwd")
    token = after_ffn(dh2) if after_ffn is not None else None
    w_post = p["norm_xattn_post"] if token is None else p["norm_xattn_post"] + token[:1, :1]
    dxa, g["norm_xattn_post"] = _rmsnorm_bwd(sv["xa"], w_post, dh2, name="norm_xattn_post_bwd")
    dxo = _mm(dxa, p["w_xo"], tb=True, name="mm_xo_da")
    g["w_xo"] = _mm(sv["xo"], dxa, ta=True, name="mm_xo_dw")
    dxq, dkv = _xattn_bwd(sv["xq"], sv["kv"], dxo, name="xattn_bwd")
    dhn2 = _mm(dxq, p["w_xq"], tb=True, name="mm_xq_da")
    g["w_xq"] = _mm(sv["hn2"], dxq, ta=True, name="mm_xq_dw")
    dmemn = _mm(dkv, p["w_xkv"], tb=True, b_shards=True, name="mm_xkv_da")
    g["w_xkv"] = _mm(sv["memn"], dkv, ta=True, out_shards=True, name="mm_xkv_dw")
    _, g["norm_mem"] = _rmsnorm_bwd(mem, p["norm_mem"], dmemn, name="norm_mem_bwd")
    dh1, g["norm_xattn_pre"] = _rmsnorm_bwd(sv["h1"], p["norm_xattn_pre"], dhn2, res=dh2, name="norm_xattn_pre_bwd")
    dmix, g["norm_mix_post"] = _rmsnorm_bwd(sv["mix"], p["norm_mix_post"], dh1, name="norm_mix_post_bwd")
    dcat = _mm(dmix, p["w_out"], tb=True, name="mm_out_da")
    g["w_out"] = _mm(sv["cat"], dmix, ta=True, name="mm_out_dw")
    proj = sv["proj"]
    daq, dak, dav = _attn_flash_bwd(proj, tabs, sv["cat"], sv["lse"], dcat, name="attn_bwd")
    dcb, dcc, dcx, g["conv_short"] = _sconv_bwd(proj, p["conv_short"], dcat, name="sconv_bwd")
    dqkv, dgbeta, dgate, g["gdn_norm"] = _gdn_core_bwd(sv["qkv"], sv["gbeta"], proj, p["gdn_norm"], sv["states"], sv["tinv"],
                                                        dcat, name="gdn_core_bwd")
    dgqkv, g["conv_gdn"] = _gdn_pre_bwd(proj, p["conv_gdn"], dqkv, name="gdn_pre_bwd")
    dab, g["gdn_a_log"], g["gdn_dt_bias"] = _gdn_gates_bwd(proj, p["gdn_a_log"], p["gdn_dt_bias"], dgbeta,
                                                          name="gdn_gates_bwd")
    s = proj.shape[0]
    dproj = jnp.concatenate([daq, dak, dav, dcb, dcc, dcx, dgqkv, dgate, dab,
                             jnp.zeros((s, IN_PAD - COL_AB - LANE), BF16)], axis=-1)
    dhn1 = _mm(dproj, p["w_in"], tb=True, name="mm_in_da")
    g["w_in"] = _mm(sv["hn1"], dproj, ta=True, name="mm_in_dw")
    dh0, g["norm_mix_pre"] = _rmsnorm_bwd(sv["h0"], p["norm_mix_pre"], dhn1, res=dh1, name="norm_mix_pre_bwd")
    return dh0, g


MATRICES = ("w_in", "w_out", "w_xq", "w_xkv", "w_xo", "w_gate_up", "w_down")
VECTORS = ("norm_mix_pre", "norm_mix_post", "conv_short", "conv_gdn", "gdn_a_log", "gdn_dt_bias", "gdn_norm",
           "norm_mem", "norm_xattn_pre", "norm_xattn_post", "norm_ffn_pre", "norm_ffn_post")


def _w_in_to_padded(w):
    zeros = jnp.zeros(w.shape[:-1] + (IN_PAD - IN_WIDTH,), w.dtype)
    return jnp.concatenate([w[..., :COL_GATE], w[..., COL_GATE + 8:], w[..., COL_GATE:COL_GATE + 8], zeros], axis=-1)


def _w_in_from_padded(g):
    return jnp.concatenate([g[..., :COL_GATE], g[..., COL_AB:COL_AB + 8], g[..., COL_GATE:COL_AB]], axis=-1)


def _layer_params(full, l):
    p = {n: full[n][l] for n in MATRICES}
    for n in VECTORS:
        v = full[n][l]
        if n in ("gdn_a_log", "gdn_dt_bias"):
            p[n] = _pad_lanes(v)
        elif v.ndim == 1:
            p[n] = v.reshape(1, -1)
        else:
            p[n] = v
    return p


def _local_step(x, mem, pos, target, full, matrices_for=None, on_grads=None, mid_backward=None):
    tabs = _rope_tables(pos, name="rope_tables")
    h = x
    saved, params = [], []
    for l in range(DEPTH):
        if matrices_for is not None:
            full = {**full, **{n: {l: v} for n, v in matrices_for(l, h).items()}}
        p = _layer_params(full, l)
        h, sv = _layer_fwd(h, mem, tabs, p)
        params.append(p)
        saved.append(sv)
    loss_row, dh = _loss_grad(h, target, name="loss_grad")
    grads = [None] * DEPTH
    token = None
    for l in reversed(range(DEPTH)):
        p = params[l]
        if token is not None:
            p = {**p, "norm_ffn_post": p["norm_ffn_post"] + token[:1, :1]}
        dh, grads[l] = _layer_bwd(dh, mem, tabs, p, saved[l], after_ffn=mid_backward)
        if on_grads is not None:
            token = on_grads(l, grads[l], dh)
    return loss_row, dh, grads


ANY = pl.BlockSpec(memory_space=pl.ANY)
MESH = pl.DeviceIdType.MESH


def _flip(pos, mask):
    return tuple(1 - v if m else v for v, m in zip(pos, mask))


def _exchange(ins, out_shapes, remote, local, *, name):
    n_in = len(ins)
    n_out = len(out_shapes)

    def at(ref, idx):
        return ref.at[idx] if idx else ref

    def body(*refs):
        in_refs = refs[:n_in]
        out_refs = refs[n_in:n_in + n_out]
        send_sems, recv_sems, local_sems = refs[n_in + n_out:]
        me = (lax.axis_index("x"), lax.axis_index("y"), lax.axis_index("c"))
        waits = []
        for k, (ii, src_at, oi, dst_at, mask) in enumerate(remote):
            peer = _flip(me, mask)
            pltpu.make_async_remote_copy(
                src_ref=at(in_refs[ii], src_at(me, peer)), dst_ref=at(out_refs[oi], dst_at(me)),
                send_sem=send_sems.at[k], recv_sem=recv_sems.at[k], device_id=peer, device_id_type=MESH).start()
            waits.append(pltpu.make_async_remote_copy(
                src_ref=at(in_refs[ii], src_at(peer, me)), dst_ref=at(out_refs[oi], dst_at(peer)),
                send_sem=send_sems.at[k], recv_sem=recv_sems.at[k], device_id=peer, device_id_type=MESH))
        own = []
        for k, (ii, src_at, oi, dst_at) in enumerate(local):
            cp = pltpu.make_async_copy(at(in_refs[ii], src_at(me)), at(out_refs[oi], dst_at(me)), local_sems.at[k])
            cp.start()
            own.append(cp)
        for w in waits:
            w.wait_send()
            w.wait_recv()
        for cp in own:
            cp.wait()

    return pl.pallas_call(
        body, in_specs=[ANY] * n_in, out_specs=[ANY] * n_out, out_shape=list(out_shapes),
        scratch_shapes=[pltpu.SemaphoreType.DMA((len(remote),)), pltpu.SemaphoreType.DMA((len(remote),)),
                        pltpu.SemaphoreType.DMA((max(len(local), 1),))],
        name=name)(*ins)


HBM = pl.BlockSpec(memory_space=pltpu.HBM)
SEM = pl.BlockSpec(memory_space=pltpu.SEMAPHORE)
SPLIT_EFFECT = pltpu.SideEffectType.DATAFLOW_SIDE_EFFECTING


def _exchange_start(ins, land_shapes, remote, *, name):
    n_in, n_land, n_cp = len(ins), len(land_shapes), len(remote)

    def body(*refs):
        in_refs, land_refs = refs[:n_in], refs[n_in:n_in + n_land]
        send_sems, recv_sems = refs[n_in + n_land:n_in + n_land + 2]
        token = refs[-1]
        me = (lax.axis_index("x"), lax.axis_index("y"), lax.axis_index("c"))
        for k, (ii, src_at, oi, dst_at, mask) in enumerate(remote):
            peer = _flip(me, mask)
            idx_s, idx_d = src_at(me, peer), dst_at(me)
            pltpu.make_async_remote_copy(
                src_ref=in_refs[ii].at[idx_s] if idx_s else in_refs[ii],
                dst_ref=land_refs[oi].at[idx_d] if idx_d else land_refs[oi],
                send_sem=send_sems.at[k], recv_sem=recv_sems.at[k], device_id=peer, device_id_type=MESH).start()
        token[...] = jnp.zeros_like(token)

    buffers = [pltpu.with_memory_space_constraint(a, pltpu.HBM) for a in ins]
    buffers += [pltpu.with_memory_space_constraint(lax.empty(s.shape, s.dtype), pltpu.HBM) for s in land_shapes]
    out = pl.pallas_call(
        body, name=name,
        out_shape=(pltpu.SemaphoreType.DMA((n_cp,)), pltpu.SemaphoreType.DMA((n_cp,)),
                   *[pltpu.HBM(b.shape, b.dtype) for b in buffers], SDS((8, LANE), F32)),
        in_specs=[HBM] * len(buffers),
        out_specs=(SEM, SEM, *[HBM] * len(buffers), pl.BlockSpec(memory_space=pltpu.VMEM)),
        input_output_aliases={i: 2 + i for i in range(len(buffers))},
        compiler_params=pltpu.CompilerParams(has_side_effects=SPLIT_EFFECT))(*buffers)
    return out[0], out[1], list(out[2:2 + n_in]), list(out[2 + n_in:2 + n_in + n_land]), out[-1]


def _exchange_wait(send_sems, recv_sems, ins, lands, remote, after, *, name):
    n_in, n_land = len(ins), len(lands)

    def body(*refs):
        in_refs, land_refs = refs[:n_in], refs[n_in:n_in + n_land]
        send_sems_, recv_sems_ = refs[n_in + n_land:n_in + n_land + 2]
        me = (lax.axis_index("x"), lax.axis_index("y"), lax.axis_index("c"))
        for k, (ii, src_at, oi, dst_at, mask) in enumerate(remote):
            peer = _flip(me, mask)
            idx_s, idx_d = src_at(peer, me), dst_at(peer)
            cp = pltpu.make_async_remote_copy(
                src_ref=in_refs[ii].at[idx_s] if idx_s else in_refs[ii],
                dst_ref=land_refs[oi].at[idx_d] if idx_d else land_refs[oi],
                send_sem=send_sems_.at[k], recv_sem=recv_sems_.at[k], device_id=peer, device_id_type=MESH)
            cp.wait_send()
            cp.wait_recv()

    buffers = list(ins) + list(lands)
    out = pl.pallas_call(
        body, name=name, out_shape=tuple(pltpu.HBM(b.shape, b.dtype) for b in buffers),
        in_specs=[HBM] * len(buffers) + [SEM, SEM, ANY], out_specs=tuple([HBM] * len(buffers)),
        input_output_aliases={i: i for i in range(len(buffers))},
        compiler_params=pltpu.CompilerParams(has_side_effects=SPLIT_EFFECT))(*buffers, send_sems, recv_sems, after)
    return list(out[:n_in]), list(out[n_in:])


def _chip(pos):
    return 2 * pos[0] + pos[1]


XY_MASKS = ((1, 0, 0), (0, 1, 0), (1, 1, 0))
SIBLING = (0, 0, 1)
ALL_MASKS = tuple((a, b, c) for a in (0, 1) for b in (0, 1) for c in (0, 1))[1:]


def _gather_xy(arrs, *, name):
    n = len(arrs)
    outs = [SDS((4,) + a.shape, a.dtype) for a in arrs]
    halves = [a.shape[0] // 2 for a in arrs]

    def body(*refs):
        in_refs, out_refs = refs[:n], refs[n:2 * n]
        ici_send, ici_recv, d2d_send, d2d_recv = refs[2 * n:]
        me = (lax.axis_index("x"), lax.axis_index("y"), lax.axis_index("c"))
        sibling = _flip(me, SIBLING)
        flows = []
        for i in range(n):
            mine = pl.ds(me[2] * halves[i], halves[i])
            other = pl.ds(sibling[2] * halves[i], halves[i])
            for m in XY_MASKS:
                k = len(flows)
                peer = _flip(me, m)

                def remote(src, dst, sems, to, k=k):
                    return pltpu.make_async_remote_copy(src_ref=src, dst_ref=dst, send_sem=sems[0].at[k],
                                                        recv_sem=sems[1].at[k], device_id=to, device_id_type=MESH)

                landed = out_refs[i].at[_chip(peer), mine]
                send = remote(in_refs[i].at[mine], out_refs[i].at[_chip(me), mine], (ici_send, ici_recv), peer)
                send.start()
                arrive = remote(in_refs[i].at[mine], landed, (ici_send, ici_recv), peer)
                forward = remote(landed, landed, (d2d_send, d2d_recv), sibling)
                handed = remote(out_refs[i].at[_chip(peer), other], out_refs[i].at[_chip(peer), other],
                                (d2d_send, d2d_recv), sibling)
                flows.append((send, arrive, forward, handed))
        for _, arrive, forward, _ in flows:
            arrive.wait_recv()
            forward.start()
        for send, _, forward, handed in flows:
            handed.wait_recv()
            send.wait_send()
            forward.wait_send()

    n_flows = 3 * n
    return pl.pallas_call(
        body, in_specs=[ANY] * n, out_specs=[ANY] * n, out_shape=outs,
        scratch_shapes=[pltpu.SemaphoreType.DMA((n_flows,))] * 4, name=name)(*arrs)


def _gather_all(arr, *, name):
    slot = lambda pos: (4 * pos[0] + 2 * pos[1] + pos[2],)
    whole = lambda *_: ()
    remote = [(0, whole, 0, slot, m) for m in ALL_MASKS]
    return _exchange([arr], [SDS((8,) + arr.shape, arr.dtype)], remote, [(0, whole, 0, slot)], name=name)[0]


def _send_to_sibling(arrs, *, name):
    outs = [SDS(a.shape, a.dtype) for a in arrs]
    whole = lambda *_: ()
    remote = [(i, whole, i, whole, SIBLING) for i in range(len(arrs))]
    return _exchange(arrs, outs, remote, [], name=name)


def _add_half(g, other, core, *, name):
    n4, nl, r, c = g.shape
    half = r // 2
    g3 = g.reshape(n4 * nl, 2, half, c)
    o3 = other.reshape(n4 * nl, half, c)
    tr = _rows_tile(half) if half > 512 else half

    def body(core_ref, g_ref, o_ref, out_ref):
        out_ref[...] = (g_ref[...] + o_ref[...]).astype(BF16)

    return pl.pallas_call(
        body,
        grid_spec=pltpu.PrefetchScalarGridSpec(
            num_scalar_prefetch=1, grid=(n4 * nl, half // tr),
            in_specs=[BS((None, None, tr, c), lambda i, j, core_ref: (i, core_ref[0], j, 0)),
                      BS((None, tr, c), lambda i, j, core_ref: (i, j, 0))],
            out_specs=BS((None, tr, c), lambda i, j, core_ref: (i, j, 0))),
        out_shape=SDS((n4 * nl, half, c), BF16), compiler_params=_cp(), name=name)(core, g3, o3).reshape(n4, nl, half, c)


def _sum_chips(parts, mine, chip, *, name):
    n4, nl, h, c = mine.shape
    tr = _rows_tile(h) if h > 512 else h

    def body(chip_ref, p_ref, own_ref, out_ref):
        me = chip_ref[0]
        own = own_ref[...].astype(F32)
        across = [p_ref[j].astype(F32) for j in range(len(XY_MASKS))]
        t = []
        for s in range(n4):
            rel = s ^ me
            t.append(jnp.where(rel == 0, own, jnp.where(rel == 2, across[0], jnp.where(rel == 1, across[1], across[2]))))
        out_ref[...] = ((t[0] + t[1]) + t[2]) + t[3]

    return pl.pallas_call(
        body,
        grid_spec=pltpu.PrefetchScalarGridSpec(
            num_scalar_prefetch=1, grid=(nl, h // tr),
            in_specs=[BS((len(XY_MASKS), None, tr, c), lambda i, j, chip_ref: (0, i, j, 0)),
                      BS((None, None, tr, c), lambda i, j, chip_ref: (chip_ref[0], i, j, 0))],
            out_specs=BS((None, tr, c), lambda i, j, chip_ref: (i, j, 0))),
        out_shape=SDS((nl, h, c), F32), compiler_params=_cp(), name=name)(chip, parts, mine)


def _sum_devices(parts, *, name):
    n, r, c = parts.shape

    def body(p_ref, out_ref):
        acc = p_ref[0]
        for d in range(1, n):
            acc = acc + p_ref[d]
        out_ref[...] = acc

    return pl.pallas_call(
        body, grid=(1,), in_specs=[BS((n, r, c), lambda i: (0, 0, 0))], out_specs=BS((r, c), lambda i: (0, 0)),
        out_shape=SDS((r, c), F32), compiler_params=_cp(), name=name)(parts)


WEIGHTS = ("norm_mix_pre", "norm_mix_post", "w_in", "conv_short", "conv_gdn", "gdn_a_log", "gdn_dt_bias",
           "gdn_norm", "w_out", "norm_mem", "norm_xattn_pre", "norm_xattn_post", "w_xq", "w_xkv", "w_xo",
           "norm_ffn_pre", "norm_ffn_post", "w_gate_up", "w_down")
COL_SHARDED = ("w_in", "w_xkv", "w_gate_up", "conv_short", "conv_gdn")
ROW_SHARDED = ("w_out", "w_xq", "w_xo", "w_down")
SMALL_SHARDED = ("conv_short", "conv_gdn")
SMALL_ROW_PAD = 8


KEPT_AS_SHARDS = ("w_xkv", "w_gate_up")


def _from_shards(n, g):
    if n in KEPT_AS_SHARDS:
        return g
    if n in COL_SHARDED:
        t = jnp.moveaxis(g, 0, -2)
        t = t.reshape(t.shape[:-2] + (-1,))
        return _w_in_to_padded(t) if n == "w_in" else t
    return g.reshape(-1, g.shape[-1])


def _to_shards(n, g):
    if n in KEPT_AS_SHARDS:
        return g
    if n == "w_in":
        g = _w_in_from_padded(g)
        return jnp.moveaxis(g.reshape(g.shape[0], 4, -1), 1, 0)
    return g.reshape(4, -1, g.shape[-1])


def _pack_small(grads):
    rows = []
    for g in grads:
        for n in WEIGHTS:
            if n not in MATRICES:
                part = g[n].reshape(-1, LANE)
                rows.append(jnp.pad(part, ((0, -part.shape[0] % SMALL_ROW_PAD), (0, 0))))
    return jnp.concatenate(rows, axis=0)


def _unpack_small(packed, like):
    out, at = [], 0
    for _ in range(DEPTH):
        g = {}
        for n in WEIGHTS:
            if n not in MATRICES:
                shape = like[n].shape
                k = math.prod(shape) // LANE
                g[n] = packed[at:at + k].reshape(shape)
                at += k + (-k % SMALL_ROW_PAD)
        out.append(g)
    return out


def kernel(x, mem, positions, norm_mix_pre, norm_mix_post, w_in, conv_short, conv_gdn, gdn_a_log, gdn_dt_bias, gdn_norm, w_out, norm_mem, norm_xattn_pre, norm_xattn_post, w_xq, w_xkv, w_xo, norm_ffn_pre, norm_ffn_post, w_gate_up, w_down, loss_target, m_norm_mix_pre, m_norm_mix_post, m_w_in, m_conv_short, m_conv_gdn, m_gdn_a_log, m_gdn_dt_bias, m_gdn_norm, m_w_out, m_norm_mem, m_norm_xattn_pre, m_norm_xattn_post, m_w_xq, m_w_xkv, m_w_xo, m_norm_ffn_pre, m_norm_ffn_post, m_w_gate_up, m_w_down, v_norm_mix_pre, v_norm_mix_post, v_w_in, v_conv_short, v_conv_gdn, v_gdn_a_log, v_gdn_dt_bias, v_gdn_norm, v_w_out, v_norm_mem, v_norm_xattn_pre, v_norm_xattn_post, v_w_xq, v_w_xkv, v_w_xo, v_norm_ffn_pre, v_norm_ffn_post, v_w_gate_up, v_w_down):
    args = dict(locals())
    w = {n: args[n] for n in WEIGHTS}
    m = {n: args["m_" + n] for n in WEIGHTS}
    v = {n: args["v_" + n] for n in WEIGHTS}
    seq = x.shape[1]
    chip = 2 * lax.axis_index("x") + lax.axis_index("y")
    core = lax.axis_index("c").astype(jnp.int32).reshape(1)

    def cast(n, l):
        return w[n][l].astype(BF16)

    first = [cast(n, 0) for n in MATRICES] + [w[n] for n in SMALL_SHARDED]
    blocks = _gather_xy(first, name="gather_weights")
    blocks = [lax.dynamic_update_index_in_dim(b, o, chip, axis=0) for b, o in zip(blocks, first)]
    full = {n: _from_shards(n, b) for n, b in zip(list(MATRICES) + list(SMALL_SHARDED), blocks)}
    layer0 = {n: full.pop(n) for n in MATRICES}
    for n in WEIGHTS:
        if n not in full and n not in MATRICES:
            full[n] = w[n]
    whole = lambda *_: ()
    gather_plan = [(i, whole, i, (lambda sender: (_chip(sender),)), mask)
                   for i in range(len(MATRICES)) for mask in XY_MASKS]
    in_flight = {}
    for l in range(1, DEPTH):
        own = [cast(n, l) for n in MATRICES]
        in_flight[l] = (own,) + _exchange_start(own, [SDS((4,) + o.shape, o.dtype) for o in own], gather_plan,
                                                name=f"gather_start_{l}")[:4]

    def matrices_for(l, h):
        if l == 0:
            return layer0
        own, send_sems, recv_sems, thru, lands = in_flight.pop(l)
        _, lands = _exchange_wait(send_sems, recv_sems, thru, lands, gather_plan, h, name=f"gather_wait_{l}")
        lands = [lax.dynamic_update_index_in_dim(b, o, chip, axis=0) for b, o in zip(lands, own)]
        return {n: _from_shards(n, b) for n, b in zip(MATRICES, lands)}

    chip1 = chip.astype(jnp.int32).reshape(1)
    scatter_plan = [(i, (lambda sender, receiver: (_chip(receiver),)), i, (lambda sender, j=j: (j,)), mask)
                    for i in range(len(MATRICES)) for j, mask in enumerate(XY_MASKS)]
    reduced = {}
    swapping, sending = [], []

    def half_rows(i, shape):
        half = shape[2] // 2
        return lambda sender, receiver: (slice(None), slice(None), pl.ds(receiver[2] * half, half))

    def finish_scatter(after):
        l, send_sems, recv_sems, pair, lands = sending.pop()
        pair, parts = _exchange_wait(send_sems, recv_sems, pair, lands, scatter_plan, after, name=f"scatter_wait_{l}")
        reduced[l] = [_sum_chips(p, pr, chip1, name="grads_chip_sum") for p, pr in zip(parts, pair)]

    def on_grads(l, g, dh):
        if sending:
            finish_scatter(dh)
        mine = [_to_shards(n, g[n])[:, None] for n in MATRICES]
        plan = [(i, half_rows(i, a.shape), i, whole, SIBLING) for i, a in enumerate(mine)]
        lands = [SDS(a.shape[:2] + (a.shape[2] // 2, a.shape[3]), a.dtype) for a in mine]
        send_sems, recv_sems, mine, lands, token = _exchange_start(mine, lands, plan, name=f"swap_start_{l}")
        swapping.append((l, send_sems, recv_sems, mine, lands, plan))
        return token

    def mid_backward(after):
        if not swapping:
            return None
        l, send_sems, recv_sems, mine, lands, plan = swapping.pop()
        mine, theirs = _exchange_wait(send_sems, recv_sems, mine, lands, plan, after, name=f"swap_wait_{l}")
        pair = [_add_half(a, b, core, name="grads_pair_sum") for a, b in zip(mine, theirs)]
        lands = [SDS((len(XY_MASKS),) + p.shape[1:], p.dtype) for p in pair]
        send_sems, recv_sems, pair, lands, token = _exchange_start(pair, lands, scatter_plan, name=f"scatter_start_{l}")
        sending.append((l, send_sems, recv_sems, pair, lands))
        return token

    loss_row, dx, grads = _local_step(x[0], mem[0], positions.reshape(seq, 1), loss_target[0], full,
                                      matrices_for=matrices_for, on_grads=on_grads, mid_backward=mid_backward)
    mid_backward(dx)
    finish_scatter(dx)
    keys = [(n, l) for l in range(DEPTH) for n in MATRICES]
    halves = [r for l in range(DEPTH) for r in reduced[l]]
    others = _send_to_sibling(halves, name="grads_share_halves")
    south = lax.axis_index("c") == 0
    grad = {}
    for n in MATRICES:
        layers = []
        for (n2, _), a, b in zip(keys, halves, others):
            if n2 == n:
                layers += [jnp.where(south, a, b), jnp.where(south, b, a)]
        grad[n] = jnp.concatenate(layers, axis=0).reshape(w[n].shape)

    packed = _pack_small(grads)
    total = _sum_devices(_gather_all(packed, name="small_grads_gather"), name="small_grads_sum")
    small = _unpack_small(total, grads[0])
    for n in WEIGHTS:
        if n in MATRICES:
            continue
        g = jnp.stack([s[n] for s in small])
        if n in ("gdn_a_log", "gdn_dt_bias"):
            g = g[:, 0, :GDN_HEADS]
        elif n in SMALL_SHARDED:
            width = w[n].shape[-1]
            g = lax.dynamic_slice_in_dim(g, chip * width, width, axis=2)
        grad[n] = g.reshape(w[n].shape)

    delta, new_m, new_v = {}, {}, {}
    for n in WEIGHTS:
        shape = w[n].shape
        two_d = (-1, shape[-1])
        d, nm, nv = _adamw(w[n].reshape(two_d), grad[n].reshape(two_d), m[n].reshape(two_d), v[n].reshape(two_d),
                           name="adamw_" + n)
        delta[n], new_m[n], new_v[n] = d.reshape(shape), nm.reshape(shape), nv.reshape(shape)

    loss = lax.psum(loss_row[0, 0], ("x", "y", "c"))
    return (loss, dx.reshape(x.shape), *[grad[n] for n in WEIGHTS], *[delta[n] for n in WEIGHTS],
            *[new_m[n] for n in WEIGHTS], *[new_v[n] for n in WEIGHTS])
```

```python
import math

import jax
import jax.numpy as jnp
from jax import lax
from jax.experimental import pallas as pl
from jax.experimental.pallas import tpu as pltpu

F32 = jnp.float32
BF16 = jnp.bfloat16
BS = pl.BlockSpec
SDS = jax.ShapeDtypeStruct
PRECISE = lax.Precision.HIGH

D_MODEL = 1024
DEPTH = 4
EPS = 1e-6
ATTN_HEADS = 4
ATTN_HEAD_DIM = 64
ATTN_WIDTH = 256
ROPE_THETA = 500000.0
ROPE_DIM = 16
CONV_WIDTH = 256
CONV_K = 3
GDN_HEADS = 4
GDN_HEAD_DIM = 128
GDN_WIDTH = 512
GDN_CONV_K = 4
GDN_CHUNK = 64
IN_WIDTH = 3592
XATTN_HEADS = 4
XATTN_HEAD_DIM = 256
FFN_HIDDEN = 2816
ADAM_LR = 0.001
ADAM_B1 = 0.9
ADAM_B2 = 0.999
ADAM_EPS = 1e-08
ADAM_WD = 0.01
ADAM_STEP = 10

IN_PAD = 3840
COL_GDN = 1536
COL_GATE = 3072
COL_AB = 3584

VMEM_LIMIT_V7X = 56 * 1024 * 1024
LANE = 128


def _cp(**kw):
    return pltpu.CompilerParams(vmem_limit_bytes=VMEM_LIMIT_V7X, **kw)


def _tile(n, cap):
    if n <= cap:
        return n
    best = None
    for t in range(LANE, cap + 1, LANE):
        if n % t == 0:
            best = t
    assert best is not None, (n, cap)
    return best


def _dot(a, b, ca, cb, precise=False):
    dims = (((ca,), (cb,)), ((), ()))
    if precise:
        return lax.dot_general(a.astype(F32), b.astype(F32), dims, precision=PRECISE,
                               preferred_element_type=F32)
    return lax.dot_general(a.astype(BF16), b.astype(BF16), dims, preferred_element_type=F32)


def _sigmoid(x):
    return 1.0 / (1.0 + jnp.exp(-x))


MM_ROWS = 1408
MM_BLOCK_BYTES = 6 * 1024 * 1024
MM_A_BYTES = 8 * 1024 * 1024


def _mm_tn(width, k, itemsize):
    if k * width * itemsize <= MM_BLOCK_BYTES:
        return width
    return _tile(width, max(LANE, min(1024, MM_BLOCK_BYTES // (k * itemsize) // LANE * LANE)))


def _mm(a, b, *, ta=False, tb=False, out_dtype=F32, b_shards=False, out_shards=False, halves=False, name):
    if halves and tb:
        m, k = a.shape[1], 2 * a.shape[2]
    else:
        m, k = (a.shape[1], a.shape[0]) if ta else a.shape
    tm = _tile(m, MM_ROWS)
    ca = 0 if ta else 1

    if b_shards and tb:
        ns, n, c = b.shape
        assert k == ns * c and not ta
        tn = _tile(n, max(LANE, min(1024, MM_BLOCK_BYTES // (k * b.dtype.itemsize) // LANE * LANE)))

        def a_block(a_ref, s):
            if halves:
                per = ns // 2
                return a_ref[s // per, :, (s % per) * c:(s % per + 1) * c]
            return a_ref[:, s * c:(s + 1) * c]

        def body(a_ref, b_ref, o_ref):
            acc = _dot(a_block(a_ref, 0), b_ref[0], 1, 1)
            for s in range(1, ns):
                acc = acc + _dot(a_block(a_ref, s), b_ref[s], 1, 1)
            o_ref[...] = acc.astype(out_dtype)

        b_spec = BS((ns, tn, c), lambda i, j: (0, j, 0))
    else:
        if b_shards:
            ns, kb, c = b.shape
            n = ns * c
            tn = _mm_tn(c, k, b.dtype.itemsize)
            nb = c // tn
            b_spec = BS((None, k, tn), lambda i, j: (j // nb, 0, j % nb))
        elif halves:
            kb, n = b.shape[1], 2 * b.shape[2]
            c = n // 4
            tn = _mm_tn(c, k, b.dtype.itemsize)
            nb = c // tn
            b_spec = BS((None, k, tn), lambda i, j: (j // (2 * nb), 0, j % (2 * nb)))
        else:
            kb, n = (b.shape[1], b.shape[0]) if tb else b.shape
            c = n // 4 if out_shards else n
            tn = _mm_tn(c, k, b.dtype.itemsize)
            nb = c // tn
            b_spec = BS((tn, k), lambda i, j: (j, 0)) if tb else BS((k, tn), lambda i, j: (0, j))
        assert kb == k
        cb = 1 if tb else 0

        def body(a_ref, b_ref, o_ref):
            o_ref[...] = _dot(a_ref[...], b_ref[...], ca, cb).astype(out_dtype)

    out_bytes = jnp.dtype(out_dtype).itemsize
    while (tm > 256 and tm % 256 == 0 and
           (tm * k * a.dtype.itemsize > MM_A_BYTES or tm * tn * out_bytes > MM_BLOCK_BYTES)):
        tm //= 2
    if halves and tb:
        a_spec = BS((2, tm, k // 2), lambda i, j: (0, i, 0))
    else:
        a_spec = BS((k, tm), lambda i, j: (0, i)) if ta else BS((tm, k), lambda i, j: (i, 0))
    if out_shards:
        out_spec = BS((None, tm, tn), lambda i, j: (j // nb, i, j % nb))
        out_shape = SDS((4, m, n // 4), out_dtype)
    elif halves and not (ta or tb):
        out_spec = BS((None, tm, tn), lambda i, j: (j // (2 * nb), i, j % (2 * nb)))
        out_shape = SDS((2, m, n // 2), out_dtype)
    else:
        out_spec = BS((tm, tn), lambda i, j: (i, j))
        out_shape = SDS((m, n), out_dtype)
    return pl.pallas_call(
        body, grid=(m // tm, n // tn), in_specs=[a_spec, b_spec], out_specs=out_spec, out_shape=out_shape,
        compiler_params=_cp(), name=name)(a, b)


def _rmsnorm(x, w, *, name):
    r, d = x.shape
    tr = _tile(r, 512)

    def body(x_ref, w_ref, o_ref):
        xv = x_ref[...]
        rs = lax.rsqrt(jnp.mean(xv * xv, axis=-1, keepdims=True) + EPS)
        o_ref[...] = (xv * rs * w_ref[...]).astype(BF16)

    return pl.pallas_call(
        body, grid=(r // tr,), in_specs=[BS((tr, d), lambda i: (i, 0)), BS((1, d), lambda i: (0, 0))],
        out_specs=BS((tr, d), lambda i: (i, 0)), out_shape=SDS((r, d), BF16),
        compiler_params=_cp(), name=name)(x, w)


def _resnorm(h, m, w, *, name):
    r, d = h.shape
    tr = _tile(r, 512)

    def body(h_ref, m_ref, w_ref, o_ref):
        mv = m_ref[...]
        rs = lax.rsqrt(jnp.mean(mv * mv, axis=-1, keepdims=True) + EPS)
        o_ref[...] = h_ref[...] + mv * rs * w_ref[...]

    row = BS((tr, d), lambda i: (i, 0))
    return pl.pallas_call(
        body, grid=(r // tr,), in_specs=[row, row, BS((1, d), lambda i: (0, 0))],
        out_specs=row, out_shape=SDS((r, d), F32), compiler_params=_cp(), name=name)(h, m, w)


def _rmsnorm_bwd(x, w, dy, res=None, *, name):
    r, d = x.shape
    tr = _tile(r, 512)
    has_res = res is not None

    def body(*refs):
        if has_res:
            x_ref, w_ref, dy_ref, res_ref, dx_ref, dw_ref = refs
        else:
            x_ref, w_ref, dy_ref, dx_ref, dw_ref = refs
        xv = x_ref[...]
        dyv = dy_ref[...].astype(F32)
        rs = lax.rsqrt(jnp.mean(xv * xv, axis=-1, keepdims=True) + EPS)
        nv = xv * rs
        dyw = dyv * w_ref[...]
        dx = rs * (dyw - nv * jnp.mean(dyw * nv, axis=-1, keepdims=True))
        if has_res:
            dx = dx + res_ref[...]
        dx_ref[...] = dx

        @pl.when(pl.program_id(0) == 0)
        def _():
            dw_ref[...] = jnp.zeros_like(dw_ref)

        dw_ref[...] += jnp.sum(dyv * nv, axis=0, keepdims=True)

    row = BS((tr, d), lambda i: (i, 0))
    vec = BS((1, d), lambda i: (0, 0))
    ins = [x, w, dy] + ([res] if has_res else [])
    return pl.pallas_call(
        body, grid=(r // tr,), in_specs=[row, vec, row] + ([row] if has_res else []),
        out_specs=[row, vec], out_shape=[SDS((r, d), F32), SDS((1, d), F32)],
        compiler_params=_cp(), name=name)(*ins)


def _swiglu(gu, *, name):
    _, r, hid = gu.shape
    tr, tc = _tile(r, 512), _tile(hid, 1408)

    def body(gu_ref, o_ref):
        g = gu_ref[0]
        o_ref[...] = (g * _sigmoid(g) * gu_ref[1]).astype(BF16)

    return pl.pallas_call(
        body, grid=(r // tr, hid // tc), in_specs=[BS((2, tr, tc), lambda i, j: (0, i, j))],
        out_specs=BS((tr, tc), lambda i, j: (i, j)), out_shape=SDS((r, hid), BF16),
        compiler_params=_cp(), name=name)(gu)


def _swiglu_bwd(gu, dact, *, name):
    _, r, hid = gu.shape
    tr, tc = _tile(r, 512), _tile(hid, 1408)

    def body(gu_ref, d_ref, o_ref):
        g = gu_ref[0]
        da = d_ref[...]
        sg = _sigmoid(g)
        o_ref[0] = (da * gu_ref[1] * sg * (1.0 + g * (1.0 - sg))).astype(BF16)
        o_ref[1] = (da * g * sg).astype(BF16)

    blk = BS((2, tr, tc), lambda i, j: (0, i, j))
    return pl.pallas_call(
        body, grid=(r // tr, hid // tc), in_specs=[blk, BS((tr, tc), lambda i, j: (i, j))],
        out_specs=blk, out_shape=SDS((2, r, hid), BF16), compiler_params=_cp(), name=name)(gu, dact)


def _loss_grad(h, target, *, name):
    r, d = h.shape
    tr = _tile(r, 512)

    def body(h_ref, t_ref, l_ref, g_ref):
        e = h_ref[...] - t_ref[...]
        g_ref[...] = e * (1.0 / d)

        @pl.when(pl.program_id(0) == 0)
        def _():
            l_ref[...] = jnp.zeros_like(l_ref)

        l_ref[...] += jnp.full((1, LANE), 0.5 / d, F32) * jnp.sum(e * e)

    row = BS((tr, d), lambda i: (i, 0))
    return pl.pallas_call(
        body, grid=(r // tr,), in_specs=[row, row],
        out_specs=[BS((1, LANE), lambda i: (0, 0)), row],
        out_shape=[SDS((1, LANE), F32), SDS((r, d), F32)], compiler_params=_cp(), name=name)(h, target)


def _adamw(w, g, m, v, *, name):
    r, c = w.shape
    tr = r if r <= 512 else _rows_tile(r)
    bc1 = 1.0 - ADAM_B1 ** ADAM_STEP
    bc2 = 1.0 - ADAM_B2 ** ADAM_STEP

    def body(w_ref, g_ref, m_ref, v_ref, d_ref, nm_ref, nv_ref):
        gv = g_ref[...]
        nm = ADAM_B1 * m_ref[...] + (1.0 - ADAM_B1) * gv
        nv = ADAM_B2 * v_ref[...] + (1.0 - ADAM_B2) * (gv * gv)
        d_ref[...] = -ADAM_LR * ((nm / bc1) / (jnp.sqrt(nv / bc2) + ADAM_EPS) + ADAM_WD * w_ref[...])
        nm_ref[...] = nm
        nv_ref[...] = nv

    blk = BS((tr, c), lambda i: (i, 0))
    return pl.pallas_call(
        body, grid=(r // tr,), in_specs=[blk] * 4, out_specs=[blk] * 3,
        out_shape=[SDS((r, c), F32)] * 3, compiler_params=_cp(), name=name)(w, g, m, v)


def _rows_tile(r):
    for t in (512, 256, 128, 64, 32, 16, 8):
        if r % t == 0:
            return t
    return r


def _rope_tables(pos, *, name):
    s = pos.shape[0]
    half = ROPE_DIM // 2

    def body(p_ref, c_ref, a_ref, b_ref):
        lane = lax.broadcasted_iota(jnp.int32, (s, ATTN_WIDTH), 1) & (ATTN_HEAD_DIM - 1)
        fi = (lane & (half - 1)).astype(F32)
        inv_freq = jnp.exp(fi * (-2.0 * math.log(ROPE_THETA) / ROPE_DIM))
        ang = p_ref[...].astype(F32) * inv_freq
        cs, sn = jnp.cos(ang), jnp.sin(ang)
        c_ref[...] = jnp.where(lane < ROPE_DIM, cs, 1.0)
        a_ref[...] = jnp.where(lane < half, -sn, 0.0)
        b_ref[...] = jnp.where((lane >= half) & (lane < ROPE_DIM), sn, 0.0)

    full = BS((s, ATTN_WIDTH), lambda i: (0, 0))
    return pl.pallas_call(
        body, grid=(1,), in_specs=[BS((s, 1), lambda i: (0, 0))], out_specs=[full] * 3,
        out_shape=[SDS((s, ATTN_WIDTH), F32)] * 3, compiler_params=_cp(), name=name)(pos)


def _rot(x, c, a, b):
    w = x.shape[1]
    return x * c + pltpu.roll(x, w - ROPE_DIM // 2, 1) * a + pltpu.roll(x, ROPE_DIM // 2, 1) * b


def _rot_t(dy, c, a, b):
    w = dy.shape[1]
    return dy * c + pltpu.roll(dy * a, ROPE_DIM // 2, 1) + pltpu.roll(dy * b, w - ROPE_DIM // 2, 1)


def _attn_count(q0, tq, s):
    dist = (lax.broadcasted_iota(jnp.int32, (tq, s), 0) + q0) - lax.broadcasted_iota(jnp.int32, (tq, s), 1)
    cnt = ((dist <= 128).astype(F32) + (((dist & 3) == 0) & (dist <= 512)).astype(F32)
           + ((dist & 15) == 0).astype(F32))
    return jnp.where(dist >= 0, cnt, 0.0)


ATTN_TQ = 256


def _attn_specs(s, tq):
    def qblk(col):
        return BS((tq, ATTN_WIDTH), lambda i: (i, col))

    def full(col):
        return BS((s, ATTN_WIDTH), lambda i: (0, col))

    return qblk, full


ATTN_TK = 512


def _attn_chunk(i, c, tq, k_ref, v_ref, ck, ak, bk):
    ks = pl.ds(pl.multiple_of(c * ATTN_TK, ATTN_TK), ATTN_TK)
    k = _rot(k_ref[ks, :], ck[ks, :], ak[ks, :], bk[ks, :]).astype(BF16)
    v = v_ref[ks, :].astype(BF16)
    cnt = _attn_count(i * tq - c * ATTN_TK, tq, ATTN_TK)
    return ks, k, v, cnt


def _attn_flash_fwd(proj, tabs, *, name):
    s = proj.shape[0]
    tq = ATTN_TQ
    qblk, full = _attn_specs(s, tq)
    scale = ATTN_HEAD_DIM ** -0.5
    nh = ATTN_HEADS

    def body(q_ref, k_ref, v_ref, cq, aq, bq, ck, ak, bk, o_ref, lse_ref):
        i = pl.program_id(0)
        q = _rot(q_ref[...], cq[...], aq[...], bq[...]) * scale
        head = lax.broadcasted_iota(jnp.int32, (1, ATTN_WIDTH), 1) >> 6
        hms = [(head == h).astype(F32) for h in range(nh)]
        qms = [(q * hm).astype(BF16) for hm in hms]

        def step(c, carry):
            ms, ls, acc = carry
            _, k, v, cnt = _attn_chunk(i, c, tq, k_ref, v_ref, ck, ak, bk)
            valid = cnt > 0.0
            new_ms, new_ls = [], []
            scale_acc = jnp.zeros((tq, ATTN_WIDTH), F32)
            add = jnp.zeros((tq, ATTN_WIDTH), F32)
            for h in range(nh):
                sc = _dot(qms[h], k, 1, 1)
                m_new = jnp.maximum(ms[h], jnp.max(jnp.where(valid, sc, -1e30), axis=-1, keepdims=True))
                alpha = jnp.exp(ms[h] - m_new)
                p = cnt * jnp.exp(jnp.minimum(sc - m_new, 0.0))
                new_ms.append(m_new)
                new_ls.append(alpha * ls[h] + jnp.sum(p, axis=-1, keepdims=True))
                scale_acc = scale_acc + alpha * hms[h]
                add = add + _dot(p, v, 1, 0) * hms[h]
            return new_ms, new_ls, acc * scale_acc + add

        init = ([jnp.full((tq, 1), -1e30, F32)] * nh, [jnp.zeros((tq, 1), F32)] * nh,
                jnp.zeros((tq, ATTN_WIDTH), F32))
        ms, ls, acc = lax.fori_loop(0, i // (ATTN_TK // tq) + 1, step, init)
        inv = jnp.zeros((tq, ATTN_WIDTH), F32)
        lane = lax.broadcasted_iota(jnp.int32, (tq, LANE), 1)
        lse = jnp.zeros((tq, LANE), F32)
        for h in range(nh):
            inv = inv + (1.0 / ls[h]) * hms[h]
            lse = jnp.where(lane == h, ms[h] + jnp.log(ls[h]), lse)
        o_ref[...] = (acc * inv).astype(BF16)
        lse_ref[...] = lse

    return pl.pallas_call(
        body, grid=(s // tq,),
        in_specs=[qblk(0), full(1), full(2), qblk(0), qblk(0), qblk(0), full(0), full(0), full(0)],
        out_specs=[BS((tq, ATTN_WIDTH), lambda i: (i, 0)), BS((tq, LANE), lambda i: (i, 0))],
        out_shape=[SDS((s, ATTN_WIDTH), BF16), SDS((s, LANE), F32)],
        compiler_params=_cp(), name=name)(proj, proj, proj, *tabs, *tabs)


def _attn_flash_bwd(proj, tabs, cat, lse, dcat, *, name):
    s = proj.shape[0]
    tq = ATTN_TQ
    nq = s // tq
    qblk, full = _attn_specs(s, tq)
    scale = ATTN_HEAD_DIM ** -0.5
    nh = ATTN_HEADS

    def body(q_ref, k_ref, v_ref, cq, aq, bq, ck, ak, bk, y_ref, lse_ref, dy_ref,
             dq_ref, dk_ref, dv_ref, dk_acc, dv_acc):
        i = pl.program_id(0)

        @pl.when(i == 0)
        def _():
            dk_acc[...] = jnp.zeros_like(dk_acc)
            dv_acc[...] = jnp.zeros_like(dv_acc)

        q = _rot(q_ref[...], cq[...], aq[...], bq[...]) * scale
        dy = dy_ref[...].astype(F32)
        prod = dy * y_ref[...].astype(F32)
        lse_all = lse_ref[...]
        head = lax.broadcasted_iota(jnp.int32, (1, ATTN_WIDTH), 1) >> 6
        hms = [(head == h).astype(F32) for h in range(nh)]
        qms = [(q * hm).astype(BF16) for hm in hms]
        dyms = [(dy * hm).astype(BF16) for hm in hms]
        deltas = [jnp.sum(prod * hm, axis=-1, keepdims=True) for hm in hms]
        lses = [lse_all[:, h:h + 1] for h in range(nh)]

        def step(c, dq):
            ks, k, v, cnt = _attn_chunk(i, c, tq, k_ref, v_ref, ck, ak, bk)
            dk_c = jnp.zeros((ATTN_TK, ATTN_WIDTH), F32)
            dv_c = jnp.zeros((ATTN_TK, ATTN_WIDTH), F32)
            for h in range(nh):
                sc = _dot(qms[h], k, 1, 1)
                p = cnt * jnp.exp(jnp.minimum(sc - lses[h], 0.0))
                dp = _dot(dyms[h], v, 1, 1)
                ds = p * (dp - deltas[h])
                dq = dq + _dot(ds, k, 1, 0) * hms[h]
                dk_c = dk_c + _dot(ds, qms[h], 0, 0)
                dv_c = dv_c + _dot(p, dyms[h], 0, 0)
            dk_acc[ks, :] += dk_c
            dv_acc[ks, :] += dv_c
            return dq

        dq = lax.fori_loop(0, i // (ATTN_TK // tq) + 1, step, jnp.zeros((tq, ATTN_WIDTH), F32))
        dq_ref[...] = _rot_t(dq * scale, cq[...], aq[...], bq[...]).astype(BF16)

        @pl.when(i == nq - 1)
        def _():
            dk_ref[...] = _rot_t(dk_acc[...], ck[...], ak[...], bk[...]).astype(BF16)
            dv_ref[...] = dv_acc[...].astype(BF16)

    whole = BS((s, ATTN_WIDTH), lambda i: (0, 0))
    return pl.pallas_call(
        body, grid=(nq,),
        in_specs=[qblk(0), full(1), full(2), qblk(0), qblk(0), qblk(0), full(0), full(0), full(0),
                  qblk(0), BS((tq, LANE), lambda i: (i, 0)), qblk(0)],
        out_specs=[BS((tq, ATTN_WIDTH), lambda i: (i, 0)), whole, whole],
        out_shape=[SDS((s, ATTN_WIDTH), BF16)] * 3,
        scratch_shapes=[pltpu.VMEM((s, ATTN_WIDTH), F32), pltpu.VMEM((s, ATTN_WIDTH), F32)],
        compiler_params=_cp(), name=name)(proj, proj, proj, *tabs, *tabs, cat, lse, dcat)


def _shift_down(x, n):
    if n == 0:
        return x
    rows = lax.broadcasted_iota(jnp.int32, x.shape, 0)
    return jnp.where(rows >= n, pltpu.roll(x, n, 0), 0.0)


def _shift_up(x, n):
    if n == 0:
        return x
    t = x.shape[0]
    rows = lax.broadcasted_iota(jnp.int32, x.shape, 0)
    return jnp.where(rows < t - n, pltpu.roll(x, t - n, 0), 0.0)


def _conv_fwd(z, w, kk):
    y = z * w[kk - 1:kk, :]
    for j in range(kk - 1):
        y = y + _shift_down(z, kk - 1 - j) * w[j:j + 1, :]
    return y


def _conv_bwd(z, dy, w, kk):
    dz = dy * w[kk - 1:kk, :]
    dws = []
    for j in range(kk - 1):
        dz = dz + _shift_up(dy, kk - 1 - j) * w[j:j + 1, :]
        dws.append(jnp.sum(dy * _shift_down(z, kk - 1 - j), axis=0, keepdims=True))
    dws.append(jnp.sum(dy * z, axis=0, keepdims=True))
    return dz, jnp.concatenate(dws, axis=0)


def _sconv_fwd(proj, w, *, name):
    s = proj.shape[0]

    def body(b_ref, c_ref, x_ref, w_ref, o_ref):
        y = _conv_fwd(c_ref[...] * x_ref[...], w_ref[...], CONV_K)
        o_ref[...] = (b_ref[...] * y).astype(BF16)

    def col(j):
        return BS((s, LANE), lambda i: (0, j + i))

    return pl.pallas_call(
        body, grid=(CONV_WIDTH // LANE,), in_specs=[col(6), col(8), col(10), BS((CONV_K, LANE), lambda i: (0, i))],
        out_specs=BS((s, LANE), lambda i: (0, i)), out_shape=SDS((s, CONV_WIDTH), BF16),
        compiler_params=_cp(), name=name)(proj, proj, proj, w)


def _sconv_bwd(proj, w, dcat, *, name):
    s = proj.shape[0]

    def body(b_ref, c_ref, x_ref, w_ref, dy_ref, db_ref, dc_ref, dx_ref, dw_ref):
        cv, xv, wv = c_ref[...], x_ref[...], w_ref[...]
        dy = dy_ref[...].astype(F32)
        z = cv * xv
        db_ref[...] = (dy * _conv_fwd(z, wv, CONV_K)).astype(BF16)
        dz, dw = _conv_bwd(z, dy * b_ref[...], wv, CONV_K)
        dc_ref[...] = (dz * xv).astype(BF16)
        dx_ref[...] = (dz * cv).astype(BF16)
        dw_ref[...] = dw

    def col(j):
        return BS((s, LANE), lambda i: (0, j + i))

    out = BS((s, LANE), lambda i: (0, i))
    wspec = BS((CONV_K, LANE), lambda i: (0, i))
    return pl.pallas_call(
        body, grid=(CONV_WIDTH // LANE,), in_specs=[col(6), col(8), col(10), wspec, col(2)],
        out_specs=[out, out, out, wspec],
        out_shape=[SDS((s, CONV_WIDTH), BF16)] * 3 + [SDS((CONV_K, CONV_WIDTH), F32)],
        compiler_params=_cp(), name=name)(proj, proj, proj, w, dcat)


def _l2n(y, scale):
    r = lax.rsqrt(jnp.sum(y * y, axis=-1, keepdims=True) + EPS)
    return y * r * scale, r


def _gdn_pre_fwd(proj, w, *, name):
    s = proj.shape[0]
    nh = GDN_HEADS

    def body(x_ref, w_ref, o_ref):
        j = pl.program_id(0)
        c = _conv_fwd(x_ref[...], w_ref[...], GDN_CONV_K)
        y = c * _sigmoid(c)
        scale = jnp.where(j < nh, GDN_HEAD_DIM ** -0.5, 1.0)
        n, _ = _l2n(y, scale)
        o_ref[...] = jnp.where(j < 2 * nh, n, y)

    return pl.pallas_call(
        body, grid=(3 * nh,),
        in_specs=[BS((s, LANE), lambda j: (0, COL_GDN // LANE + j)), BS((GDN_CONV_K, LANE), lambda j: (0, j))],
        out_specs=BS((s, LANE), lambda j: (0, j)), out_shape=SDS((s, 3 * GDN_WIDTH), F32),
        compiler_params=_cp(), name=name)(proj, w)


def _gdn_pre_bwd(proj, w, dqkv, *, name):
    s = proj.shape[0]
    nh = GDN_HEADS

    def body(x_ref, w_ref, d_ref, dx_ref, dw_ref):
        j = pl.program_id(0)
        xv, wv, dn = x_ref[...], w_ref[...], d_ref[...]
        c = _conv_fwd(xv, wv, GDN_CONV_K)
        sg = _sigmoid(c)
        y = c * sg
        scale = jnp.where(j < nh, GDN_HEAD_DIM ** -0.5, 1.0)
        n, r = _l2n(y, 1.0)
        dns = dn * scale
        dy_norm = r * (dns - n * jnp.sum(dns * n, axis=-1, keepdims=True))
        dy = jnp.where(j < 2 * nh, dy_norm, dn)
        dc = dy * sg * (1.0 + c * (1.0 - sg))
        dx, dw = _conv_bwd(xv, dc, wv, GDN_CONV_K)
        dx_ref[...] = dx.astype(BF16)
        dw_ref[...] = dw

    wspec = BS((GDN_CONV_K, LANE), lambda j: (0, j))
    blk = BS((s, LANE), lambda j: (0, j))
    return pl.pallas_call(
        body, grid=(3 * nh,),
        in_specs=[BS((s, LANE), lambda j: (0, COL_GDN // LANE + j)), wspec, blk],
        out_specs=[blk, wspec], out_shape=[SDS((s, 3 * GDN_WIDTH), BF16), SDS((GDN_CONV_K, 3 * GDN_WIDTH), F32)],
        compiler_params=_cp(), name=name)(proj, w, dqkv)


def _softplus(x):
    return jnp.maximum(x, 0.0) + jnp.log(1.0 + jnp.exp(-jnp.abs(x)))


def _gdn_gates_fwd(proj, a_log, dt_bias, *, name):
    s = proj.shape[0]

    def body(x_ref, al_ref, dt_ref, o_ref):
        xv = x_ref[...]
        lane = lax.broadcasted_iota(jnp.int32, xv.shape, 1)
        g = -jnp.exp(al_ref[...]) * _softplus(xv + dt_ref[...])
        o_ref[...] = jnp.where(lane < GDN_HEADS, g, jnp.where(lane < 2 * GDN_HEADS, _sigmoid(xv), 0.0))

    vec = BS((1, LANE), lambda i: (0, 0))
    return pl.pallas_call(
        body, grid=(1,), in_specs=[BS((s, LANE), lambda i: (0, COL_AB // LANE)), vec, vec],
        out_specs=BS((s, LANE), lambda i: (0, 0)), out_shape=SDS((s, LANE), F32),
        compiler_params=_cp(), name=name)(proj, a_log, dt_bias)


def _gdn_gates_bwd(proj, a_log, dt_bias, dgb, *, name):
    s = proj.shape[0]

    def body(x_ref, al_ref, dt_ref, d_ref, dx_ref, dal_ref, ddt_ref):
        xv, dv = x_ref[...], d_ref[...]
        lane = lax.broadcasted_iota(jnp.int32, xv.shape, 1)
        is_g = lane < GDN_HEADS
        ea = -jnp.exp(al_ref[...])
        z = xv + dt_ref[...]
        da = jnp.where(is_g, dv * ea * _sigmoid(z), 0.0)
        beta = _sigmoid(xv)
        dx_ref[...] = jnp.where(is_g, da, jnp.where(lane < 2 * GDN_HEADS, dv * beta * (1.0 - beta), 0.0)).astype(BF16)
        dal_ref[...] = jnp.sum(jnp.where(is_g, dv * ea * _softplus(z), 0.0), axis=0, keepdims=True)
        ddt_ref[...] = jnp.sum(da, axis=0, keepdims=True)

    vec = BS((1, LANE), lambda i: (0, 0))
    blk = BS((s, LANE), lambda i: (0, 0))
    return pl.pallas_call(
        body, grid=(1,), in_specs=[BS((s, LANE), lambda i: (0, COL_AB // LANE)), vec, vec, blk],
        out_specs=[blk, vec, vec], out_shape=[SDS((s, LANE), BF16), SDS((1, LANE), F32), SDS((1, LANE), F32)],
        compiler_params=_cp(), name=name)(proj, a_log, dt_bias, dgb)


def _col_to_row(col, eye):
    return jnp.sum(jnp.where(eye, col, 0.0), axis=0, keepdims=True)


def _row_to_col(row, eye):
    return jnp.sum(jnp.where(eye, row, 0.0), axis=1, keepdims=True)


GDN_GROUP = 4
TRI_BLOCK_SHIFT = 4


def _gdn_masks(c):
    row = lax.broadcasted_iota(jnp.int32, (c, c), 0)
    col = lax.broadcasted_iota(jnp.int32, (c, c), 1)
    return dict(row=row, col=col, eye=row == col, low=row >= col, strict=row > col, upper=row <= col,
                on_diag=(row >> TRI_BLOCK_SHIFT) == (col >> TRI_BLOCK_SHIFT))


def _tri_inv(a_list, mk):
    eye_f = mk["eye"].astype(F32)
    ds = [jnp.where(mk["on_diag"], a, 0.0) for a in a_list]
    xs = [eye_f - d for d in ds]
    ps = ds
    for _ in range(3):
        ps = [_dot(p, p, 1, 0, precise=True) for p in ps]
        xs = [x + _dot(x, p, 1, 0, precise=True) for x, p in zip(xs, ps)]
    ms = [_dot(x, a - d, 1, 0, precise=True) for x, a, d in zip(xs, a_list, ds)]
    m2s = [_dot(m, m, 1, 0, precise=True) for m in ms]
    ys = [eye_f - m for m in ms]
    ys = [y + _dot(y, m2, 1, 0, precise=True) for y, m2 in zip(ys, m2s)]
    return [_dot(y, x, 1, 0, precise=True) for y, x in zip(ys, xs)]


def _gdn_pre(qs, ks, vs, gs, betas, mk, ts=None):
    c, hd = qs[0].shape
    eye, low = mk["eye"], mk["low"]
    g_rows = [_col_to_row(g, eye) for g in gs]
    d_cols = [jnp.sum(jnp.where(low, gr, 0.0), axis=1, keepdims=True) for gr in g_rows]
    d_rows = [jnp.sum(jnp.where(mk["upper"], g, 0.0), axis=0, keepdims=True) for g in gs]
    rels = [jnp.where(low, jnp.exp(jnp.minimum(dc - dr, 0.0)), 0.0) for dc, dr in zip(d_cols, d_rows)]
    d_lasts = [dc[c - 1:c, :] for dc in d_cols]
    es = [jnp.exp(dc) for dc in d_cols]
    fs = [jnp.exp(dl - dc) for dl, dc in zip(d_lasts, d_cols)]
    cds = [jnp.exp(dl) for dl in d_lasts]
    kbs = [k * b for k, b in zip(ks, betas)]
    kbqs = [jnp.concatenate([kb, q], axis=0) for kb, q in zip(kbs, qs)]
    kqk = [_dot(kbq, k, 1, 1) for kbq, k in zip(kbqs, ks)]
    kks = [x[:c, :] for x in kqk]
    qks = [x[c:, :] for x in kqk]
    if ts is None:
        ts = _tri_inv([jnp.where(mk["strict"], kk * rel, 0.0) for kk, rel in zip(kks, rels)], mk)
    vbs = [v * b for v, b in zip(vs, betas)]
    kbes = [kb * e for kb, e in zip(kbs, es)]
    uws = [_dot(t, jnp.concatenate([vb, kbe], axis=1), 1, 0) for t, vb, kbe in zip(ts, vbs, kbes)]
    out = []
    for i in range(len(qs)):
        out.append(dict(rel=rels[i], e=es[i], f=fs[i], cd=cds[i], kb=kbs[i], kbq=kbqs[i], kk=kks[i], qk=qks[i],
                        t=ts[i], u=uws[i][:, :hd], w=uws[i][:, hd:], uw=uws[i], attn=qks[i] * rels[i],
                        qd=qs[i] * es[i], kd=ks[i] * fs[i]))
    return out


def _gdn_apply(pres, sts, leaving=True):
    c = pres[0]["u"].shape[0]
    wqs = [_dot(jnp.concatenate([p["w"], p["qd"]], axis=0), st, 1, 0) for p, st in zip(pres, sts)]
    vns = [p["u"] - x[:c, :] for p, x in zip(pres, wqs)]
    os_ = [x[c:, :] + _dot(p["attn"], vn, 1, 0) for p, x, vn in zip(pres, wqs, vns)]
    if not leaving:
        return vns, os_, None
    new = [p["cd"] * st + _dot(p["kd"], vn, 0, 0) for p, st, vn in zip(pres, sts, vns)]
    return vns, os_, new


def _gdn_bwd_rest(qs, ks, vs, betas, sts, pres, vns, dos, dvns, dsts, mk):
    c, hd = qs[0].shape
    eye = mk["eye"]
    n = range(len(qs))
    dkds = [_dot(vns[i], dsts[i], 1, 1) for i in n]
    dcds = [jnp.sum(sts[i] * dsts[i]) for i in n]
    dattns = [jnp.where(mk["low"], _dot(dos[i], vns[i], 1, 1), 0.0) for i in n]
    dqdws = [_dot(jnp.concatenate([dos[i], -dvns[i]], axis=0), sts[i], 1, 1) for i in n]
    dqds = [x[:c, :] for x in dqdws]
    dws = [x[c:, :] for x in dqdws]
    dvks = [_dot(pres[i]["t"], jnp.concatenate([dvns[i], dws[i]], axis=1), 0, 0) for i in n]
    das = [jnp.where(mk["strict"], -_dot(dvks[i], pres[i]["uw"], 1, 1), 0.0) for i in n]
    dkqs = [jnp.concatenate([das[i] * pres[i]["rel"], dattns[i] * pres[i]["rel"]], axis=0) for i in n]
    dkbdq = [_dot(dkqs[i], ks[i], 1, 0) for i in n]
    dk0 = [_dot(dkqs[i], pres[i]["kbq"], 0, 0) for i in n]
    out = []
    rows1 = lax.broadcasted_iota(jnp.int32, (c, 1), 0)
    for i in n:
        p = pres[i]
        dvb, dkbe = dvks[i][:, :hd], dvks[i][:, hd:]
        grel = (das[i] * p["kk"] + dattns[i] * p["qk"]) * p["rel"]
        dkb = dkbdq[i][:c, :] + dkbe * p["e"]
        dk = dk0[i] + dkds[i] * p["f"] + dkb * betas[i]
        dq = dkbdq[i][c:, :] + dqds[i] * p["e"]
        dv = dvb * betas[i]
        dbeta = jnp.sum(dkb * ks[i], axis=1, keepdims=True) + jnp.sum(dvb * vs[i], axis=1, keepdims=True)
        de = jnp.sum(dqds[i] * qs[i], axis=1, keepdims=True) + jnp.sum(dkbe * p["kb"], axis=1, keepdims=True)
        dff = jnp.sum(dkds[i] * ks[i], axis=1, keepdims=True) * p["f"]
        dd = (de * p["e"] - dff + jnp.sum(grel, axis=1, keepdims=True)
              - _row_to_col(jnp.sum(grel, axis=0, keepdims=True), eye))
        dd = dd + jnp.where(rows1 == c - 1, jnp.sum(dff) + dcds[i] * p["cd"], 0.0)
        dg = jnp.sum(jnp.where(mk["upper"], _col_to_row(dd, eye), 0.0), axis=1, keepdims=True)
        out.append((dq, dk, dv, dg, dbeta))
    return out


def _gdn_specs(c):
    def qkv(j):
        return BS((c, GDN_WIDTH), lambda n: (n, j))

    return qkv


def _gdn_core_fwd(qkv, gbeta, proj, norm_w, *, name):
    s = qkv.shape[0]
    c, nh, hd, grp = GDN_CHUNK, GDN_HEADS, GDN_HEAD_DIM, GDN_GROUP
    n_chunks = s // c
    blk = _gdn_specs(grp * c)
    inst = [(sub, h) for sub in range(grp) for h in range(nh)]

    def body(q_ref, k_ref, v_ref, gb_ref, gate_ref, nw_ref, y_ref, st_ref, t_ref, state):
        @pl.when(pl.program_id(0) == 0)
        def _():
            state[...] = jnp.zeros_like(state)

        mk = _gdn_masks(c)
        rows = [slice(sub * c, (sub + 1) * c) for sub in range(grp)]
        lanes = [slice(h * hd, (h + 1) * hd) for h in range(nh)]
        gbs = [gb_ref[r, :] for r in rows]
        pres = _gdn_pre([q_ref[rows[sub], lanes[h]] for sub, h in inst], [k_ref[rows[sub], lanes[h]] for sub, h in inst],
                        [v_ref[rows[sub], lanes[h]] for sub, h in inst], [gbs[sub][:, h:h + 1] for sub, h in inst],
                        [gbs[sub][:, nh + h:nh + h + 1] for sub, h in inst], mk)
        sts = [state[ls, :] for ls in lanes]
        outs = []
        for sub in range(grp):
            for h in range(nh):
                st_ref[pl.ds((sub * nh + h) * hd, hd), :] = sts[h]
            _, os_, sts = _gdn_apply(pres[sub * nh:(sub + 1) * nh], sts)
            outs += os_
        for h in range(nh):
            state[lanes[h], :] = sts[h]
        nw = nw_ref[...]
        for i, (sub, h) in enumerate(inst):
            t_ref[pl.ds(i * c, c), :] = pres[i]["t"]
            o = outs[i]
            gate = gate_ref[rows[sub], lanes[h]]
            rs = lax.rsqrt(jnp.mean(o * o, axis=-1, keepdims=True) + EPS)
            y_ref[rows[sub], lanes[h]] = (o * rs * nw * (gate * _sigmoid(gate))).astype(BF16)

    return pl.pallas_call(
        body, grid=(n_chunks // grp,),
        in_specs=[blk(0), blk(1), blk(2), BS((grp * c, LANE), lambda n: (n, 0)),
                  BS((grp * c, GDN_WIDTH), lambda n: (n, COL_GATE // GDN_WIDTH)), BS((1, hd), lambda n: (0, 0))],
        out_specs=[BS((grp * c, GDN_WIDTH), lambda n: (n, 0)), BS((grp * nh * hd, hd), lambda n: (n, 0)),
                   BS((grp * nh * c, c), lambda n: (n, 0))],
        out_shape=[SDS((s, GDN_WIDTH), BF16), SDS((n_chunks * nh * hd, hd), F32), SDS((n_chunks * nh * c, c), F32)],
        scratch_shapes=[pltpu.VMEM((nh * hd, hd), F32)],
        compiler_params=_cp(), name=name)(qkv, qkv, qkv, gbeta, proj, norm_w)


def _gdn_core_bwd(qkv, gbeta, proj, norm_w, states, tinv, dcat, *, name):
    s = qkv.shape[0]
    c, nh, hd, grp = GDN_CHUNK, GDN_HEADS, GDN_HEAD_DIM, GDN_GROUP
    n_chunks = s // c
    last = n_chunks // grp - 1
    inst = [(sub, h) for sub in range(grp) for h in range(nh)]

    def rev(j, w):
        return BS((grp * c, w), lambda n: (last - n, j))

    def body(q_ref, k_ref, v_ref, gb_ref, gate_ref, nw_ref, st_ref, t_ref, dy_ref,
             dqkv_ref, dgb_ref, dgate_ref, dnw_ref, dstate):
        @pl.when(pl.program_id(0) == 0)
        def _():
            dstate[...] = jnp.zeros_like(dstate)
            dnw_ref[...] = jnp.zeros_like(dnw_ref)

        mk = _gdn_masks(c)
        rows = [slice(sub * c, (sub + 1) * c) for sub in range(grp)]
        lanes = [slice(h * hd, (h + 1) * hd) for h in range(nh)]
        gbs = [gb_ref[r, :] for r in rows]
        qs = [q_ref[rows[sub], lanes[h]] for sub, h in inst]
        ks = [k_ref[rows[sub], lanes[h]] for sub, h in inst]
        vs = [v_ref[rows[sub], lanes[h]] for sub, h in inst]
        betas = [gbs[sub][:, nh + h:nh + h + 1] for sub, h in inst]
        sts = [st_ref[pl.ds(i * hd, hd), :] for i in range(len(inst))]
        pres = _gdn_pre(qs, ks, vs, [gbs[sub][:, h:h + 1] for sub, h in inst], betas, mk,
                        ts=[t_ref[pl.ds(i * c, c), :] for i in range(len(inst))])
        vns, outs, _ = _gdn_apply(pres, sts, leaving=False)

        nw = nw_ref[...]
        dnw = jnp.zeros((1, hd), F32)
        dos = []
        for i, (sub, h) in enumerate(inst):
            o = outs[i]
            gate = gate_ref[rows[sub], lanes[h]]
            dy = dy_ref[rows[sub], lanes[h]].astype(F32)
            sg = _sigmoid(gate)
            rs = lax.rsqrt(jnp.mean(o * o, axis=-1, keepdims=True) + EPS)
            nrm = o * rs
            dgate_ref[rows[sub], lanes[h]] = (dy * nrm * nw * sg * (1.0 + gate * (1.0 - sg))).astype(BF16)
            dnv = dy * (gate * sg)
            dnw = dnw + jnp.sum(dnv * nrm, axis=0, keepdims=True)
            dno = dnv * nw
            dos.append(rs * (dno - nrm * jnp.mean(dno * nrm, axis=-1, keepdims=True)))
        dnw_ref[...] += dnw

        from_o = [_dot(p["attn"], do, 0, 0) for p, do in zip(pres, dos)]
        to_st = [_dot(p["qd"], do, 0, 0) for p, do in zip(pres, dos)]
        dst = [dstate[ls, :] for ls in lanes]
        dsts = [None] * len(inst)
        dvns = [None] * len(inst)
        for sub in reversed(range(grp)):
            idx = [sub * nh + h for h in range(nh)]
            for h, i in enumerate(idx):
                dsts[i] = dst[h]
                dvns[i] = from_o[i] + _dot(pres[i]["kd"], dst[h], 1, 0)
            dst = [pres[i]["cd"] * dst[h] + to_st[i] - _dot(pres[i]["w"], dvns[i], 0, 0) for h, i in enumerate(idx)]
        for h in range(nh):
            dstate[lanes[h], :] = dst[h]

        grads = _gdn_bwd_rest(qs, ks, vs, betas, sts, pres, vns, dos, dvns, dsts, mk)
        lane = lax.broadcasted_iota(jnp.int32, (c, LANE), 1)
        dgb = [jnp.zeros((c, LANE), F32) for _ in range(grp)]
        for (sub, h), (dq, dk, dv, dg, dbeta) in zip(inst, grads):
            dqkv_ref[rows[sub], lanes[h]] = dq
            dqkv_ref[rows[sub], slice(GDN_WIDTH + h * hd, GDN_WIDTH + (h + 1) * hd)] = dk
            dqkv_ref[rows[sub], slice(2 * GDN_WIDTH + h * hd, 2 * GDN_WIDTH + (h + 1) * hd)] = dv
            dgb[sub] = jnp.where(lane == h, dg, jnp.where(lane == nh + h, dbeta, dgb[sub]))
        for sub in range(grp):
            dgb_ref[rows[sub], :] = dgb[sub]

    return pl.pallas_call(
        body, grid=(n_chunks // grp,),
        in_specs=[rev(0, GDN_WIDTH), rev(1, GDN_WIDTH), rev(2, GDN_WIDTH), rev(0, LANE),
                  rev(COL_GATE // GDN_WIDTH, GDN_WIDTH), BS((1, hd), lambda n: (0, 0)),
                  BS((grp * nh * hd, hd), lambda n: (last - n, 0)), BS((grp * nh * c, c), lambda n: (last - n, 0)),
                  rev(1, GDN_WIDTH)],
        out_specs=[rev(0, 3 * GDN_WIDTH), rev(0, LANE), rev(0, GDN_WIDTH), BS((1, hd), lambda n: (0, 0))],
        out_shape=[SDS((s, 3 * GDN_WIDTH), F32), SDS((s, LANE), F32), SDS((s, GDN_WIDTH), BF16), SDS((1, hd), F32)],
        scratch_shapes=[pltpu.VMEM((nh * hd, hd), F32)],
        compiler_params=_cp(), name=name)(qkv, qkv, qkv, gbeta, proj, norm_w, states, tinv, dcat)


XATTN_TQ = 512


def _xattn_probs(qh, kh):
    sc = _dot(qh, kh, 1, 1) * (XATTN_HEAD_DIM ** -0.5)
    p = jnp.exp(sc - jnp.max(sc, axis=-1, keepdims=True))
    return p / jnp.sum(p, axis=-1, keepdims=True)


def _xattn_fwd(q, kv, *, name):
    s, d = q.shape
    m = kv.shape[0]
    tq, hd = _tile(s, XATTN_TQ), XATTN_HEAD_DIM

    def body(q_ref, k_ref, v_ref, o_ref):
        for h in range(XATTN_HEADS):
            ls = slice(h * hd, (h + 1) * hd)
            p = _xattn_probs(q_ref[:, ls], k_ref[:, ls])
            o_ref[:, ls] = _dot(p, v_ref[:, ls], 1, 0).astype(BF16)

    return pl.pallas_call(
        body, grid=(s // tq,),
        in_specs=[BS((tq, d), lambda i: (i, 0)), BS((m, d), lambda i: (0, 0)), BS((m, d), lambda i: (0, 1))],
        out_specs=BS((tq, d), lambda i: (i, 0)), out_shape=SDS((s, d), BF16),
        compiler_params=_cp(), name=name)(q, kv, kv)


def _xattn_bwd(q, kv, do, *, name):
    s, d = q.shape
    m = kv.shape[0]
    tq, hd = _tile(s, XATTN_TQ), XATTN_HEAD_DIM
    scale = hd ** -0.5

    def body(q_ref, k_ref, v_ref, do_ref, dq_ref, dkv_ref):
        @pl.when(pl.program_id(0) == 0)
        def _():
            dkv_ref[...] = jnp.zeros_like(dkv_ref)

        for h in range(XATTN_HEADS):
            ls = slice(h * hd, (h + 1) * hd)
            vs = slice(d + h * hd, d + (h + 1) * hd)
            qh, kh, doh = q_ref[:, ls], k_ref[:, ls], do_ref[:, ls]
            p = _xattn_probs(qh, kh)
            dp = _dot(doh, v_ref[:, ls], 1, 1)
            ds = p * (dp - jnp.sum(p * dp, axis=-1, keepdims=True)) * scale
            dq_ref[:, ls] = _dot(ds, kh, 1, 0).astype(BF16)
            dkv_ref[:, ls] += _dot(ds, qh, 0, 0)
            dkv_ref[:, vs] += _dot(p, doh, 0, 0)

    row = BS((tq, d), lambda i: (i, 0))
    return pl.pallas_call(
        body, grid=(s // tq,),
        in_specs=[row, BS((m, d), lambda i: (0, 0)), BS((m, d), lambda i: (0, 1)), row],
        out_specs=[row, BS((m, 2 * d), lambda i: (0, 0))],
        out_shape=[SDS((s, d), BF16), SDS((m, 2 * d), F32)],
        compiler_params=_cp(), name=name)(q, kv, kv, do)


def _pad_lanes(vec4):
    return jnp.zeros((1, LANE), F32).at[0, :GDN_HEADS].set(vec4)


def _layer_fwd(h0, mem, tabs, p):
    sv = dict(h0=h0)
    hn1 = _rmsnorm(h0, p["norm_mix_pre"], name="norm_mix_pre")
    proj = _mm(hn1, p["w_in"], name="mm_in")
    ya, lse = _attn_flash_fwd(proj, tabs, name="attn_fwd")
    yc = _sconv_fwd(proj, p["conv_short"], name="sconv_fwd")
    qkv = _gdn_pre_fwd(proj, p["conv_gdn"], name="gdn_pre_fwd")
    gbeta = _gdn_gates_fwd(proj, p["gdn_a_log"], p["gdn_dt_bias"], name="gdn_gates_fwd")
    yg, states, tinv = _gdn_core_fwd(qkv, gbeta, proj, p["gdn_norm"], name="gdn_core_fwd")
    cat = jnp.concatenate([ya, yc, yg], axis=-1)
    mix = _mm(cat, p["w_out"], name="mm_out")
    h1 = _resnorm(h0, mix, p["norm_mix_post"], name="norm_mix_post")
    hn2 = _rmsnorm(h1, p["norm_xattn_pre"], name="norm_xattn_pre")
    memn = _rmsnorm(mem, p["norm_mem"], name="norm_mem")
    xq = _mm(hn2, p["w_xq"], out_dtype=BF16, name="mm_xq")
    kv = _mm(memn, p["w_xkv"], out_dtype=BF16, b_shards=True, name="mm_xkv")
    xo = _xattn_fwd(xq, kv, name="xattn_fwd")
    xa = _mm(xo, p["w_xo"], name="mm_xo")
    h2 = _resnorm(h1, xa, p["norm_xattn_post"], name="norm_xattn_post")
    hn3 = _rmsnorm(h2, p["norm_ffn_pre"], name="norm_ffn_pre")
    gu = _mm(hn3, p["w_gate_up"], b_shards=True, halves=True, name="mm_gate_up")
    act = _swiglu(gu, name="swiglu_fwd")
    f = _mm(act, p["w_down"], name="mm_down")
    h3 = _resnorm(h2, f, p["norm_ffn_post"], name="norm_ffn_post")
    sv.update(hn1=hn1, proj=proj, lse=lse, qkv=qkv, gbeta=gbeta, states=states, tinv=tinv, cat=cat, mix=mix, h1=h1, hn2=hn2,
              memn=memn, xq=xq, kv=kv, xo=xo, xa=xa, h2=h2, hn3=hn3, gu=gu, act=act, f=f)
    return h3, sv


def _layer_bwd(dh3, mem, tabs, p, sv, after_ffn=None, after_xattn=None):
    g = {}
    df, g["norm_ffn_post"] = _rmsnorm_bwd(sv["f"], p["norm_ffn_post"], dh3, name="norm_ffn_post_bwd")
    dact = _mm(df, p["w_down"], tb=True, name="mm_down_da")
    g["w_down"] = _mm(sv["act"], df, ta=True, name="mm_down_dw")
    dgu = _swiglu_bwd(sv["gu"], dact, name="swiglu_bwd")
    dhn3 = _mm(dgu, p["w_gate_up"], tb=True, b_shards=True, halves=True, name="mm_gate_up_da")
    g["w_gate_up"] = _mm(sv["hn3"], dgu, ta=True, out_shards=True, halves=True, name="mm_gate_up_dw")
    dh2, g["norm_ffn_pre"] = _rmsnorm_bwd(sv["h2"], p["norm_ffn_pre"], dhn3, res=dh3, name="norm_ffn_pre_bwd")
    token = after_ffn(dh2) if after_ffn is not None else None
    w_post = p["norm_xattn_post"] if token is None else p["norm_xattn_post"] + token[:1, :1]
    dxa, g["norm_xattn_post"] = _rmsnorm_bwd(sv["xa"], w_post, dh2, name="norm_xattn_post_bwd")
    dxo = _mm(dxa, p["w_xo"], tb=True, name="mm_xo_da")
    g["w_xo"] = _mm(sv["xo"], dxa, ta=True, name="mm_xo_dw")
    dxq, dkv = _xattn_bwd(sv["xq"], sv["kv"], dxo, name="xattn_bwd")
    dhn2 = _mm(dxq, p["w_xq"], tb=True, name="mm_xq_da")
    g["w_xq"] = _mm(sv["hn2"], dxq, ta=True, name="mm_xq_dw")
    dmemn = _mm(dkv, p["w_xkv"], tb=True, b_shards=True, name="mm_xkv_da")
    g["w_xkv"] = _mm(sv["memn"], dkv, ta=True, out_shards=True, name="mm_xkv_dw")
    _, g["norm_mem"] = _rmsnorm_bwd(mem, p["norm_mem"], dmemn, name="norm_mem_bwd")
    dh1, g["norm_xattn_pre"] = _rmsnorm_bwd(sv["h1"], p["norm_xattn_pre"], dhn2, res=dh2, name="norm_xattn_pre_bwd")
    token = after_xattn(g, dh1) if after_xattn is not None else None
    w_post = p["norm_mix_post"] if token is None else p["norm_mix_post"] + token[:1, :1]
    dmix, g["norm_mix_post"] = _rmsnorm_bwd(sv["mix"], w_post, dh1, name="norm_mix_post_bwd")
    dcat = _mm(dmix, p["w_out"], tb=True, name="mm_out_da")
    g["w_out"] = _mm(sv["cat"], dmix, ta=True, name="mm_out_dw")
    proj = sv["proj"]
    daq, dak, dav = _attn_flash_bwd(proj, tabs, sv["cat"], sv["lse"], dcat, name="attn_bwd")
    dcb, dcc, dcx, g["conv_short"] = _sconv_bwd(proj, p["conv_short"], dcat, name="sconv_bwd")
    dqkv, dgbeta, dgate, g["gdn_norm"] = _gdn_core_bwd(sv["qkv"], sv["gbeta"], proj, p["gdn_norm"], sv["states"], sv["tinv"],
                                                        dcat, name="gdn_core_bwd")
    dgqkv, g["conv_gdn"] = _gdn_pre_bwd(proj, p["conv_gdn"], dqkv, name="gdn_pre_bwd")
    dab, g["gdn_a_log"], g["gdn_dt_bias"] = _gdn_gates_bwd(proj, p["gdn_a_log"], p["gdn_dt_bias"], dgbeta,
                                                          name="gdn_gates_bwd")
    s = proj.shape[0]
    dproj = jnp.concatenate([daq, dak, dav, dcb, dcc, dcx, dgqkv, dgate, dab,
                             jnp.zeros((s, IN_PAD - COL_AB - LANE), BF16)], axis=-1)
    dhn1 = _mm(dproj, p["w_in"], tb=True, name="mm_in_da")
    g["w_in"] = _mm(sv["hn1"], dproj, ta=True, name="mm_in_dw")
    dh0, g["norm_mix_pre"] = _rmsnorm_bwd(sv["h0"], p["norm_mix_pre"], dhn1, res=dh1, name="norm_mix_pre_bwd")
    return dh0, g


MATRICES = ("w_in", "w_out", "w_xq", "w_xkv", "w_xo", "w_gate_up", "w_down")
VECTORS = ("norm_mix_pre", "norm_mix_post", "conv_short", "conv_gdn", "gdn_a_log", "gdn_dt_bias", "gdn_norm",
           "norm_mem", "norm_xattn_pre", "norm_xattn_post", "norm_ffn_pre", "norm_ffn_post")


def _w_in_segments(n_shards=4):
    c = IN_WIDTH // n_shards
    moves = ((0, COL_GATE, 0), (COL_GATE, COL_GATE + 8, COL_AB), (COL_GATE + 8, IN_WIDTH, COL_GATE))
    segs = []
    for s in range(n_shards):
        for lo, hi, dst in moves:
            a, b = max(lo, s * c), min(hi, (s + 1) * c)
            if a < b:
                segs.append((s, a - s * c, dst + a - lo, b - a))
    return segs


def _w_in_pack(g, *, name):
    ns, r, c = g.shape
    tr = _tile(r, 256)

    def body(g_ref, o_ref):
        o_ref[:, IN_WIDTH:] = jnp.zeros((tr, IN_PAD - IN_WIDTH), o_ref.dtype)
        for s, src, dst, width in _w_in_segments(ns):
            o_ref[:, dst:dst + width] = g_ref[s, :, src:src + width]

    return pl.pallas_call(
        body, grid=(r // tr,), in_specs=[BS((ns, tr, c), lambda i: (0, i, 0))],
        out_specs=BS((tr, IN_PAD), lambda i: (i, 0)), out_shape=SDS((r, IN_PAD), g.dtype),
        compiler_params=_cp(), name=name)(g)


def _w_in_unpack(dw, *, name):
    r = dw.shape[0]
    ns, c = 4, IN_WIDTH // 4
    tr = _tile(r, 256)

    def body(d_ref, o_ref):
        for s, src, dst, width in _w_in_segments(ns):
            o_ref[s, :, src:src + width] = d_ref[:, dst:dst + width]

    return pl.pallas_call(
        body, grid=(r // tr,), in_specs=[BS((tr, IN_PAD), lambda i: (i, 0))],
        out_specs=BS((ns, tr, c), lambda i: (0, i, 0)), out_shape=SDS((ns, r, c), dw.dtype),
        compiler_params=_cp(), name=name)(dw)


def _layer_params(full, l):
    p = {n: full[n][l] for n in MATRICES}
    for n in VECTORS:
        v = full[n][l]
        if n in ("gdn_a_log", "gdn_dt_bias"):
            p[n] = _pad_lanes(v)
        elif v.ndim == 1:
            p[n] = v.reshape(1, -1)
        else:
            p[n] = v
    return p


def _local_step(x, mem, pos, target, full, matrices_for=None, on_grads=None, mid_backward=None,
                after_xattn=None):
    tabs = _rope_tables(pos, name="rope_tables")
    h = x
    saved, params = [], []
    for l in range(DEPTH):
        if matrices_for is not None:
            full = {**full, **{n: {l: v} for n, v in matrices_for(l, h).items()}}
        p = _layer_params(full, l)
        h, sv = _layer_fwd(h, mem, tabs, p)
        params.append(p)
        saved.append(sv)
    loss_row, dh = _loss_grad(h, target, name="loss_grad")
    grads = [None] * DEPTH
    token = None
    for l in reversed(range(DEPTH)):
        p = params[l]
        if token is not None:
            p = {**p, "norm_ffn_post": p["norm_ffn_post"] + token[:1, :1]}
        late = None if after_xattn is None else (lambda g, dh1, l=l: after_xattn(l, g, dh1))
        dh, grads[l] = _layer_bwd(dh, mem, tabs, p, saved[l], after_ffn=mid_backward, after_xattn=late)
        if on_grads is not None:
            token = on_grads(l, grads[l], dh)
    return loss_row, dh, grads


ANY = pl.BlockSpec(memory_space=pl.ANY)
MESH = pl.DeviceIdType.MESH


def _flip(pos, mask):
    return tuple(1 - v if m else v for v, m in zip(pos, mask))


def _exchange(ins, out_shapes, remote, local, *, name):
    n_in = len(ins)
    n_out = len(out_shapes)

    def at(ref, idx):
        return ref.at[idx] if idx else ref

    def body(*refs):
        in_refs = refs[:n_in]
        out_refs = refs[n_in:n_in + n_out]
        send_sems, recv_sems, local_sems = refs[n_in + n_out:]
        me = (lax.axis_index("x"), lax.axis_index("y"), lax.axis_index("c"))
        waits = []
        for k, (ii, src_at, oi, dst_at, mask) in enumerate(remote):
            peer = _flip(me, mask)
            pltpu.make_async_remote_copy(
                src_ref=at(in_refs[ii], src_at(me, peer)), dst_ref=at(out_refs[oi], dst_at(me)),
                send_sem=send_sems.at[k], recv_sem=recv_sems.at[k], device_id=peer, device_id_type=MESH).start()
            waits.append(pltpu.make_async_remote_copy(
                src_ref=at(in_refs[ii], src_at(peer, me)), dst_ref=at(out_refs[oi], dst_at(peer)),
                send_sem=send_sems.at[k], recv_sem=recv_sems.at[k], device_id=peer, device_id_type=MESH))
        own = []
        for k, (ii, src_at, oi, dst_at) in enumerate(local):
            cp = pltpu.make_async_copy(at(in_refs[ii], src_at(me)), at(out_refs[oi], dst_at(me)), local_sems.at[k])
            cp.start()
            own.append(cp)
        for w in waits:
            w.wait_send()
            w.wait_recv()
        for cp in own:
            cp.wait()

    return pl.pallas_call(
        body, in_specs=[ANY] * n_in, out_specs=[ANY] * n_out, out_shape=list(out_shapes),
        scratch_shapes=[pltpu.SemaphoreType.DMA((len(remote),)), pltpu.SemaphoreType.DMA((len(remote),)),
                        pltpu.SemaphoreType.DMA((max(len(local), 1),))],
        name=name)(*ins)


HBM = pl.BlockSpec(memory_space=pltpu.HBM)
SEM = pl.BlockSpec(memory_space=pltpu.SEMAPHORE)
SPLIT_EFFECT = pltpu.SideEffectType.DATAFLOW_SIDE_EFFECTING


def _exchange_start(ins, land_shapes, remote, *, name):
    n_in, n_land, n_cp = len(ins), len(land_shapes), len(remote)

    def body(*refs):
        in_refs, land_refs = refs[:n_in], refs[n_in:n_in + n_land]
        send_sems, recv_sems = refs[n_in + n_land:n_in + n_land + 2]
        token = refs[-1]
        me = (lax.axis_index("x"), lax.axis_index("y"), lax.axis_index("c"))
        for k, (ii, src_at, oi, dst_at, mask) in enumerate(remote):
            peer = _flip(me, mask)
            idx_s, idx_d = src_at(me, peer), dst_at(me)
            pltpu.make_async_remote_copy(
                src_ref=in_refs[ii].at[idx_s] if idx_s else in_refs[ii],
                dst_ref=land_refs[oi].at[idx_d] if idx_d else land_refs[oi],
                send_sem=send_sems.at[k], recv_sem=recv_sems.at[k], device_id=peer, device_id_type=MESH).start()
        token[...] = jnp.zeros_like(token)

    buffers = [pltpu.with_memory_space_constraint(a, pltpu.HBM) for a in ins]
    buffers += [pltpu.with_memory_space_constraint(lax.empty(s.shape, s.dtype), pltpu.HBM) for s in land_shapes]
    out = pl.pallas_call(
        body, name=name,
        out_shape=(pltpu.SemaphoreType.DMA((n_cp,)), pltpu.SemaphoreType.DMA((n_cp,)),
                   *[pltpu.HBM(b.shape, b.dtype) for b in buffers], SDS((8, LANE), F32)),
        in_specs=[HBM] * len(buffers),
        out_specs=(SEM, SEM, *[HBM] * len(buffers), pl.BlockSpec(memory_space=pltpu.VMEM)),
        input_output_aliases={i: 2 + i for i in range(len(buffers))},
        compiler_params=pltpu.CompilerParams(has_side_effects=SPLIT_EFFECT))(*buffers)
    return out[0], out[1], list(out[2:2 + n_in]), list(out[2 + n_in:2 + n_in + n_land]), out[-1]


def _exchange_wait(send_sems, recv_sems, ins, lands, remote, after, *, name):
    n_in, n_land = len(ins), len(lands)

    def body(*refs):
        in_refs, land_refs = refs[:n_in], refs[n_in:n_in + n_land]
        send_sems_, recv_sems_ = refs[n_in + n_land:n_in + n_land + 2]
        me = (lax.axis_index("x"), lax.axis_index("y"), lax.axis_index("c"))
        for k, (ii, src_at, oi, dst_at, mask) in enumerate(remote):
            peer = _flip(me, mask)
            idx_s, idx_d = src_at(peer, me), dst_at(peer)
            cp = pltpu.make_async_remote_copy(
                src_ref=in_refs[ii].at[idx_s] if idx_s else in_refs[ii],
                dst_ref=land_refs[oi].at[idx_d] if idx_d else land_refs[oi],
                send_sem=send_sems_.at[k], recv_sem=recv_sems_.at[k], device_id=peer, device_id_type=MESH)
            cp.wait_send()
            cp.wait_recv()

    buffers = list(ins) + list(lands)
    out = pl.pallas_call(
        body, name=name, out_shape=tuple(pltpu.HBM(b.shape, b.dtype) for b in buffers),
        in_specs=[HBM] * len(buffers) + [SEM, SEM, ANY], out_specs=tuple([HBM] * len(buffers)),
        input_output_aliases={i: i for i in range(len(buffers))},
        compiler_params=pltpu.CompilerParams(has_side_effects=SPLIT_EFFECT))(*buffers, send_sems, recv_sems, after)
    return list(out[:n_in]), list(out[n_in:])


def _chip(pos):
    return 2 * pos[0] + pos[1]


XY_MASKS = ((1, 0, 0), (0, 1, 0), (1, 1, 0))
SIBLING = (0, 0, 1)
ALL_MASKS = tuple((a, b, c) for a in (0, 1) for b in (0, 1) for c in (0, 1))[1:]


def _gather_xy(arrs, *, name):
    n = len(arrs)
    outs = [SDS((4,) + a.shape, a.dtype) for a in arrs]
    halves = [a.shape[0] // 2 for a in arrs]

    def body(*refs):
        in_refs, out_refs = refs[:n], refs[n:2 * n]
        ici_send, ici_recv, d2d_send, d2d_recv = refs[2 * n:]
        me = (lax.axis_index("x"), lax.axis_index("y"), lax.axis_index("c"))
        sibling = _flip(me, SIBLING)
        flows = []
        for i in range(n):
            mine = pl.ds(me[2] * halves[i], halves[i])
            other = pl.ds(sibling[2] * halves[i], halves[i])
            for m in XY_MASKS:
                k = len(flows)
                peer = _flip(me, m)

                def remote(src, dst, sems, to, k=k):
                    return pltpu.make_async_remote_copy(src_ref=src, dst_ref=dst, send_sem=sems[0].at[k],
                                                        recv_sem=sems[1].at[k], device_id=to, device_id_type=MESH)

                landed = out_refs[i].at[_chip(peer), mine]
                send = remote(in_refs[i].at[mine], out_refs[i].at[_chip(me), mine], (ici_send, ici_recv), peer)
                send.start()
                arrive = remote(in_refs[i].at[mine], landed, (ici_send, ici_recv), peer)
                forward = remote(landed, landed, (d2d_send, d2d_recv), sibling)
                handed = remote(out_refs[i].at[_chip(peer), other], out_refs[i].at[_chip(peer), other],
                                (d2d_send, d2d_recv), sibling)
                flows.append((send, arrive, forward, handed))
        for _, arrive, forward, _ in flows:
            arrive.wait_recv()
            forward.start()
        for send, _, forward, handed in flows:
            handed.wait_recv()
            send.wait_send()
            forward.wait_send()

    n_flows = 3 * n
    return pl.pallas_call(
        body, in_specs=[ANY] * n, out_specs=[ANY] * n, out_shape=outs,
        scratch_shapes=[pltpu.SemaphoreType.DMA((n_flows,))] * 4, name=name)(*arrs)


def _gather_all(arr, *, name):
    slot = lambda pos: (4 * pos[0] + 2 * pos[1] + pos[2],)
    whole = lambda *_: ()
    remote = [(0, whole, 0, slot, m) for m in ALL_MASKS]
    return _exchange([arr], [SDS((8,) + arr.shape, arr.dtype)], remote, [(0, whole, 0, slot)], name=name)[0]


def _send_to_sibling(arrs, *, name):
    outs = [SDS(a.shape, a.dtype) for a in arrs]
    whole = lambda *_: ()
    remote = [(i, whole, i, whole, SIBLING) for i in range(len(arrs))]
    return _exchange(arrs, outs, remote, [], name=name)


def _add_half(g, other, core, *, name):
    n4, nl, r, c = g.shape
    half = r // 2
    g3 = g.reshape(n4 * nl, 2, half, c)
    o3 = other.reshape(n4 * nl, half, c)
    tr = _rows_tile(half) if half > 512 else half

    def body(core_ref, g_ref, o_ref, out_ref):
        out_ref[...] = (g_ref[...] + o_ref[...]).astype(BF16)

    return pl.pallas_call(
        body,
        grid_spec=pltpu.PrefetchScalarGridSpec(
            num_scalar_prefetch=1, grid=(n4 * nl, half // tr),
            in_specs=[BS((None, None, tr, c), lambda i, j, core_ref: (i, core_ref[0], j, 0)),
                      BS((None, tr, c), lambda i, j, core_ref: (i, j, 0))],
            out_specs=BS((None, tr, c), lambda i, j, core_ref: (i, j, 0))),
        out_shape=SDS((n4 * nl, half, c), BF16), compiler_params=_cp(), name=name)(core, g3, o3).reshape(n4, nl, half, c)


def _sum_chips(parts, mine, chip, *, name):
    n4, nl, h, c = mine.shape
    tr = _rows_tile(h) if h > 512 else h

    def body(chip_ref, p_ref, own_ref, out_ref):
        me = chip_ref[0]
        own = own_ref[...].astype(F32)
        across = [p_ref[j].astype(F32) for j in range(len(XY_MASKS))]
        t = []
        for s in range(n4):
            rel = s ^ me
            t.append(jnp.where(rel == 0, own, jnp.where(rel == 2, across[0], jnp.where(rel == 1, across[1], across[2]))))
        out_ref[...] = ((t[0] + t[1]) + t[2]) + t[3]

    return pl.pallas_call(
        body,
        grid_spec=pltpu.PrefetchScalarGridSpec(
            num_scalar_prefetch=1, grid=(nl, h // tr),
            in_specs=[BS((len(XY_MASKS), None, tr, c), lambda i, j, chip_ref: (0, i, j, 0)),
                      BS((None, None, tr, c), lambda i, j, chip_ref: (chip_ref[0], i, j, 0))],
            out_specs=BS((None, tr, c), lambda i, j, chip_ref: (i, j, 0))),
        out_shape=SDS((nl, h, c), F32), compiler_params=_cp(), name=name)(chip, parts, mine)


def _sum_devices(parts, *, name):
    n, r, c = parts.shape

    def body(p_ref, out_ref):
        acc = p_ref[0]
        for d in range(1, n):
            acc = acc + p_ref[d]
        out_ref[...] = acc

    return pl.pallas_call(
        body, grid=(1,), in_specs=[BS((n, r, c), lambda i: (0, 0, 0))], out_specs=BS((r, c), lambda i: (0, 0)),
        out_shape=SDS((r, c), F32), compiler_params=_cp(), name=name)(parts)


WEIGHTS = ("norm_mix_pre", "norm_mix_post", "w_in", "conv_short", "conv_gdn", "gdn_a_log", "gdn_dt_bias",
           "gdn_norm", "w_out", "norm_mem", "norm_xattn_pre", "norm_xattn_post", "w_xq", "w_xkv", "w_xo",
           "norm_ffn_pre", "norm_ffn_post", "w_gate_up", "w_down")
COL_SHARDED = ("w_in", "w_xkv", "w_gate_up", "conv_short", "conv_gdn")
ROW_SHARDED = ("w_out", "w_xq", "w_xo", "w_down")
SMALL_SHARDED = ("conv_short", "conv_gdn")
SMALL_ROW_PAD = 8


KEPT_AS_SHARDS = ("w_xkv", "w_gate_up")


def _from_shards(n, g):
    if n in KEPT_AS_SHARDS:
        return g
    if n == "w_in":
        return _w_in_pack(g, name="w_in_pack")
    if n in COL_SHARDED:
        t = jnp.moveaxis(g, 0, -2)
        return t.reshape(t.shape[:-2] + (-1,))
    return g.reshape(-1, g.shape[-1])


def _to_shards(n, g):
    if n in KEPT_AS_SHARDS:
        return g
    if n == "w_in":
        return _w_in_unpack(g, name="w_in_unpack")
    return g.reshape(4, -1, g.shape[-1])


def _pack_small(grads):
    rows = []
    for g in grads:
        for n in WEIGHTS:
            if n not in MATRICES:
                part = g[n].reshape(-1, LANE)
                rows.append(jnp.pad(part, ((0, -part.shape[0] % SMALL_ROW_PAD), (0, 0))))
    return jnp.concatenate(rows, axis=0)


def _unpack_small(packed, like):
    out, at = [], 0
    for _ in range(DEPTH):
        g = {}
        for n in WEIGHTS:
            if n not in MATRICES:
                shape = like[n].shape
                k = math.prod(shape) // LANE
                g[n] = packed[at:at + k].reshape(shape)
                at += k + (-k % SMALL_ROW_PAD)
        out.append(g)
    return out


def kernel(x, mem, positions, norm_mix_pre, norm_mix_post, w_in, conv_short, conv_gdn, gdn_a_log, gdn_dt_bias, gdn_norm, w_out, norm_mem, norm_xattn_pre, norm_xattn_post, w_xq, w_xkv, w_xo, norm_ffn_pre, norm_ffn_post, w_gate_up, w_down, loss_target, m_norm_mix_pre, m_norm_mix_post, m_w_in, m_conv_short, m_conv_gdn, m_gdn_a_log, m_gdn_dt_bias, m_gdn_norm, m_w_out, m_norm_mem, m_norm_xattn_pre, m_norm_xattn_post, m_w_xq, m_w_xkv, m_w_xo, m_norm_ffn_pre, m_norm_ffn_post, m_w_gate_up, m_w_down, v_norm_mix_pre, v_norm_mix_post, v_w_in, v_conv_short, v_conv_gdn, v_gdn_a_log, v_gdn_dt_bias, v_gdn_norm, v_w_out, v_norm_mem, v_norm_xattn_pre, v_norm_xattn_post, v_w_xq, v_w_xkv, v_w_xo, v_norm_ffn_pre, v_norm_ffn_post, v_w_gate_up, v_w_down):
    args = dict(locals())
    w = {n: args[n] for n in WEIGHTS}
    m = {n: args["m_" + n] for n in WEIGHTS}
    v = {n: args["v_" + n] for n in WEIGHTS}
    seq = x.shape[1]
    chip = 2 * lax.axis_index("x") + lax.axis_index("y")
    core = lax.axis_index("c").astype(jnp.int32).reshape(1)

    def cast(n, l):
        return w[n][l].astype(BF16)

    first = [cast(n, 0) for n in MATRICES] + [w[n] for n in SMALL_SHARDED]
    blocks = _gather_xy(first, name="gather_weights")
    blocks = [lax.dynamic_update_index_in_dim(b, o, chip, axis=0) for b, o in zip(blocks, first)]
    full = {n: _from_shards(n, b) for n, b in zip(list(MATRICES) + list(SMALL_SHARDED), blocks)}
    layer0 = {n: full.pop(n) for n in MATRICES}
    for n in WEIGHTS:
        if n not in full and n not in MATRICES:
            full[n] = w[n]
    whole = lambda *_: ()
    gather_plan = [(i, whole, i, (lambda sender: (_chip(sender),)), mask)
                   for i in range(len(MATRICES)) for mask in XY_MASKS]
    in_flight = {}
    for l in range(1, DEPTH):
        own = [cast(n, l) for n in MATRICES]
        in_flight[l] = (own,) + _exchange_start(own, [SDS((4,) + o.shape, o.dtype) for o in own], gather_plan,
                                                name=f"gather_start_{l}")[:4]

    def matrices_for(l, h):
        if l == 0:
            return layer0
        own, send_sems, recv_sems, thru, lands = in_flight.pop(l)
        _, lands = _exchange_wait(send_sems, recv_sems, thru, lands, gather_plan, h, name=f"gather_wait_{l}")
        lands = [lax.dynamic_update_index_in_dim(b, o, chip, axis=0) for b, o in zip(lands, own)]
        return {n: _from_shards(n, b) for n, b in zip(MATRICES, lands)}

    chip1 = chip.astype(jnp.int32).reshape(1)
    early = ("w_xq", "w_xkv", "w_xo", "w_gate_up", "w_down")
    reduced = {l: {} for l in range(DEPTH)}
    swapping, sending = [], []

    def scatter_plan(count):
        return [(i, (lambda sender, receiver: (_chip(receiver),)), i, (lambda sender, j=j: (j,)), mask)
                for i in range(count) for j, mask in enumerate(XY_MASKS)]

    def swap_plan(mine):
        def half_rows(shape):
            half = shape[2] // 2
            return lambda sender, receiver: (slice(None), slice(None), pl.ds(receiver[2] * half, half))

        plan = [(i, half_rows(a.shape), i, whole, SIBLING) for i, a in enumerate(mine)]
        return plan, [SDS(a.shape[:2] + (a.shape[2] // 2, a.shape[3]), a.dtype) for a in mine]

    def start_scatter(tag, names, mine, theirs):
        pair = [_add_half(a, b, core, name="grads_pair_sum") for a, b in zip(mine, theirs)]
        lands = [SDS((len(XY_MASKS),) + p.shape[1:], p.dtype) for p in pair]
        send_sems, recv_sems, pair, lands, token = _exchange_start(pair, lands, scatter_plan(len(names)),
                                                                   name=f"scatter_start_{tag}")
        sending.append((tag, names, send_sems, recv_sems, pair, lands))
        return token

    def finish_scatter(after):
        tag, names, send_sems, recv_sems, pair, lands = sending.pop(0)
        pair, parts = _exchange_wait(send_sems, recv_sems, pair, lands, scatter_plan(len(names)), after,
                                     name=f"scatter_wait_{tag}")
        for n, p, pr in zip(names, parts, pair):
            reduced[int(tag[0])][n] = _sum_chips(p, pr, chip1, name="grads_chip_sum")

    def on_grads(l, g, dh):
        while sending:
            finish_scatter(dh)
        names = [n for n in MATRICES if not (l == 0 and n in early)]
        mine = [_to_shards(n, g[n])[:, None] for n in names]
        plan, lands = swap_plan(mine)
        send_sems, recv_sems, mine, lands, token = _exchange_start(mine, lands, plan, name=f"swap_start_{l}")
        swapping.append((str(l), names, send_sems, recv_sems, mine, lands, plan))
        return token

    def mid_backward(after):
        if not swapping:
            return None
        tag, names, send_sems, recv_sems, mine, lands, plan = swapping.pop()
        mine, theirs = _exchange_wait(send_sems, recv_sems, mine, lands, plan, after, name=f"swap_wait_{tag}")
        return start_scatter(tag, names, mine, theirs)

    def after_xattn(l, g, dh1):
        if l != 0:
            return None
        mine = [_to_shards(n, g[n])[:, None] for n in early]
        plan, lands = swap_plan(mine)
        theirs = _exchange(mine, lands, plan, [], name="grads_swap_early")
        return start_scatter("0a", early, mine, theirs)

    loss_row, dx, grads = _local_step(x[0], mem[0], positions.reshape(seq, 1), loss_target[0], full,
                                      matrices_for=matrices_for, on_grads=on_grads, mid_backward=mid_backward,
                                      after_xattn=after_xattn)
    mid_backward(dx)
    while sending:
        finish_scatter(dx)
    keys = [(n, l) for l in range(DEPTH) for n in MATRICES]
    halves = [reduced[l][n] for l in range(DEPTH) for n in MATRICES]
    others = _send_to_sibling(halves, name="grads_share_halves")
    south = lax.axis_index("c") == 0
    grad = {}
    for n in MATRICES:
        layers = []
        for (n2, _), a, b in zip(keys, halves, others):
            if n2 == n:
                layers += [jnp.where(south, a, b), jnp.where(south, b, a)]
        grad[n] = jnp.concatenate(layers, axis=0).reshape(w[n].shape)

    packed = _pack_small(grads)
    total = _sum_devices(_gather_all(packed, name="small_grads_gather"), name="small_grads_sum")
    small = _unpack_small(total, grads[0])
    for n in WEIGHTS:
        if n in MATRICES:
            continue
        g = jnp.stack([s[n] for s in small])
        if n in ("gdn_a_log", "gdn_dt_bias"):
            g = g[:, 0, :GDN_HEADS]
        elif n in SMALL_SHARDED:
            width = w[n].shape[-1]
            g = lax.dynamic_slice_in_dim(g, chip * width, width, axis=2)
        grad[n] = g.reshape(w[n].shape)

    delta, new_m, new_v = {}, {}, {}
    for n in WEIGHTS:
        shape = w[n].shape
        two_d = (-1, shape[-1])
        d, nm, nv = _adamw(w[n].reshape(two_d), grad[n].reshape(two_d), m[n].reshape(two_d), v[n].reshape(two_d),
                           name="adamw_" + n)
        delta[n], new_m[n], new_v[n] = d.reshape(shape), nm.reshape(shape), nv.reshape(shape)

    loss = lax.psum(loss_row[0, 0], ("x", "y", "c"))
    return (loss, dx.reshape(x.shape), *[grad[n] for n in WEIGHTS], *[delta[n] for n in WEIGHTS],
            *[new_m[n] for n in WEIGHTS], *[new_v[n] for n in WEIGHTS])
```

```python
import math

import jax
import jax.numpy as jnp
from jax import lax
from jax.experimental import pallas as pl
from jax.experimental.pallas import tpu as pltpu

F32 = jnp.float32
BF16 = jnp.bfloat16
BS = pl.BlockSpec
SDS = jax.ShapeDtypeStruct
PRECISE = lax.Precision.HIGH

D_MODEL = 1024
DEPTH = 4
EPS = 1e-6
ATTN_HEADS = 4
ATTN_HEAD_DIM = 64
ATTN_WIDTH = 256
ROPE_THETA = 500000.0
ROPE_DIM = 16
CONV_WIDTH = 256
CONV_K = 3
GDN_HEADS = 4
GDN_HEAD_DIM = 128
GDN_WIDTH = 512
GDN_CONV_K = 4
GDN_CHUNK = 64
IN_WIDTH = 3592
XATTN_HEADS = 4
XATTN_HEAD_DIM = 256
FFN_HIDDEN = 2816
ADAM_LR = 0.001
ADAM_B1 = 0.9
ADAM_B2 = 0.999
ADAM_EPS = 1e-08
ADAM_WD = 0.01
ADAM_STEP = 10

IN_PAD = 3840
COL_GDN = 1536
COL_GATE = 3072
COL_AB = 3584

VMEM_LIMIT_V7X = 56 * 1024 * 1024
LANE = 128


def _cp(**kw):
    return pltpu.CompilerParams(vmem_limit_bytes=VMEM_LIMIT_V7X, **kw)


def _tile(n, cap):
    if n <= cap:
        return n
    best = None
    for t in range(LANE, cap + 1, LANE):
        if n % t == 0:
            best = t
    assert best is not None, (n, cap)
    return best


def _dot(a, b, ca, cb, precise=False):
    dims = (((ca,), (cb,)), ((), ()))
    if precise:
        return lax.dot_general(a.astype(F32), b.astype(F32), dims, precision=PRECISE,
                               preferred_element_type=F32)
    return lax.dot_general(a.astype(BF16), b.astype(BF16), dims, preferred_element_type=F32)


def _sigmoid(x):
    return 1.0 / (1.0 + jnp.exp(-x))


MM_ROWS = 1408
MM_BLOCK_BYTES = 6 * 1024 * 1024
MM_A_BYTES = 8 * 1024 * 1024


def _mm_tn(width, k, itemsize):
    if k * width * itemsize <= MM_BLOCK_BYTES:
        return width
    return _tile(width, max(LANE, min(1024, MM_BLOCK_BYTES // (k * itemsize) // LANE * LANE)))


def _mm(a, b, *, ta=False, tb=False, out_dtype=F32, b_shards=False, out_shards=False, halves=False, name):
    if halves and tb:
        m, k = a.shape[1], 2 * a.shape[2]
    else:
        m, k = (a.shape[1], a.shape[0]) if ta else a.shape
    tm = _tile(m, MM_ROWS)
    ca = 0 if ta else 1

    if b_shards and tb:
        ns, n, c = b.shape
        assert k == ns * c and not ta
        tn = _tile(n, max(LANE, min(1024, MM_BLOCK_BYTES // (k * b.dtype.itemsize) // LANE * LANE)))

        def a_block(a_ref, s):
            if halves:
                per = ns // 2
                return a_ref[s // per, :, (s % per) * c:(s % per + 1) * c]
            return a_ref[:, s * c:(s + 1) * c]

        def body(a_ref, b_ref, o_ref):
            acc = _dot(a_block(a_ref, 0), b_ref[0], 1, 1)
            for s in range(1, ns):
                acc = acc + _dot(a_block(a_ref, s), b_ref[s], 1, 1)
            o_ref[...] = acc.astype(out_dtype)

        b_spec = BS((ns, tn, c), lambda i, j: (0, j, 0))
    else:
        if b_shards:
            ns, kb, c = b.shape
            n = ns * c
            tn = _mm_tn(c, k, b.dtype.itemsize)
            nb = c // tn
            b_spec = BS((None, k, tn), lambda i, j: (j // nb, 0, j % nb))
        elif halves:
            kb, n = b.shape[1], 2 * b.shape[2]
            c = n // 4
            tn = _mm_tn(c, k, b.dtype.itemsize)
            nb = c // tn
            b_spec = BS((None, k, tn), lambda i, j: (j // (2 * nb), 0, j % (2 * nb)))
        else:
            kb, n = (b.shape[1], b.shape[0]) if tb else b.shape
            c = n // 4 if out_shards else n
            tn = _mm_tn(c, k, b.dtype.itemsize)
            nb = c // tn
            b_spec = BS((tn, k), lambda i, j: (j, 0)) if tb else BS((k, tn), lambda i, j: (0, j))
        assert kb == k
        cb = 1 if tb else 0

        def body(a_ref, b_ref, o_ref):
            o_ref[...] = _dot(a_ref[...], b_ref[...], ca, cb).astype(out_dtype)

    out_bytes = jnp.dtype(out_dtype).itemsize
    while (tm > 256 and tm % 256 == 0 and
           (tm * k * a.dtype.itemsize > MM_A_BYTES or tm * tn * out_bytes > MM_BLOCK_BYTES)):
        tm //= 2
    if halves and tb:
        a_spec = BS((2, tm, k // 2), lambda i, j: (0, i, 0))
    else:
        a_spec = BS((k, tm), lambda i, j: (0, i)) if ta else BS((tm, k), lambda i, j: (i, 0))
    if out_shards:
        out_spec = BS((None, tm, tn), lambda i, j: (j // nb, i, j % nb))
        out_shape = SDS((4, m, n // 4), out_dtype)
    elif halves and not (ta or tb):
        out_spec = BS((None, tm, tn), lambda i, j: (j // (2 * nb), i, j % (2 * nb)))
        out_shape = SDS((2, m, n // 2), out_dtype)
    else:
        out_spec = BS((tm, tn), lambda i, j: (i, j))
        out_shape = SDS((m, n), out_dtype)
    return pl.pallas_call(
        body, grid=(m // tm, n // tn), in_specs=[a_spec, b_spec], out_specs=out_spec, out_shape=out_shape,
        compiler_params=_cp(), name=name)(a, b)


def _rmsnorm(x, w, *, name):
    r, d = x.shape
    tr = _tile(r, 512)

    def body(x_ref, w_ref, o_ref):
        xv = x_ref[...]
        rs = lax.rsqrt(jnp.mean(xv * xv, axis=-1, keepdims=True) + EPS)
        o_ref[...] = (xv * rs * w_ref[...]).astype(BF16)

    return pl.pallas_call(
        body, grid=(r // tr,), in_specs=[BS((tr, d), lambda i: (i, 0)), BS((1, d), lambda i: (0, 0))],
        out_specs=BS((tr, d), lambda i: (i, 0)), out_shape=SDS((r, d), BF16),
        compiler_params=_cp(), name=name)(x, w)


def _resnorm(h, m, w, *, name):
    r, d = h.shape
    tr = _tile(r, 512)

    def body(h_ref, m_ref, w_ref, o_ref):
        mv = m_ref[...]
        rs = lax.rsqrt(jnp.mean(mv * mv, axis=-1, keepdims=True) + EPS)
        o_ref[...] = h_ref[...] + mv * rs * w_ref[...]

    row = BS((tr, d), lambda i: (i, 0))
    return pl.pallas_call(
        body, grid=(r // tr,), in_specs=[row, row, BS((1, d), lambda i: (0, 0))],
        out_specs=row, out_shape=SDS((r, d), F32), compiler_params=_cp(), name=name)(h, m, w)


def _rmsnorm_bwd(x, w, dy, res=None, *, name):
    r, d = x.shape
    tr = _tile(r, 512)
    has_res = res is not None

    def body(*refs):
        if has_res:
            x_ref, w_ref, dy_ref, res_ref, dx_ref, dw_ref = refs
        else:
            x_ref, w_ref, dy_ref, dx_ref, dw_ref = refs
        xv = x_ref[...]
        dyv = dy_ref[...].astype(F32)
        rs = lax.rsqrt(jnp.mean(xv * xv, axis=-1, keepdims=True) + EPS)
        nv = xv * rs
        dyw = dyv * w_ref[...]
        dx = rs * (dyw - nv * jnp.mean(dyw * nv, axis=-1, keepdims=True))
        if has_res:
            dx = dx + res_ref[...]
        dx_ref[...] = dx

        @pl.when(pl.program_id(0) == 0)
        def _():
            dw_ref[...] = jnp.zeros_like(dw_ref)

        dw_ref[...] += jnp.sum(dyv * nv, axis=0, keepdims=True)

    row = BS((tr, d), lambda i: (i, 0))
    vec = BS((1, d), lambda i: (0, 0))
    ins = [x, w, dy] + ([res] if has_res else [])
    return pl.pallas_call(
        body, grid=(r // tr,), in_specs=[row, vec, row] + ([row] if has_res else []),
        out_specs=[row, vec], out_shape=[SDS((r, d), F32), SDS((1, d), F32)],
        compiler_params=_cp(), name=name)(*ins)


def _swiglu(gu, *, name):
    _, r, hid = gu.shape
    tr, tc = _tile(r, 512), _tile(hid, 1408)

    def body(gu_ref, o_ref):
        g = gu_ref[0]
        o_ref[...] = (g * _sigmoid(g) * gu_ref[1]).astype(BF16)

    return pl.pallas_call(
        body, grid=(r // tr, hid // tc), in_specs=[BS((2, tr, tc), lambda i, j: (0, i, j))],
        out_specs=BS((tr, tc), lambda i, j: (i, j)), out_shape=SDS((r, hid), BF16),
        compiler_params=_cp(), name=name)(gu)


def _swiglu_bwd(gu, dact, *, name):
    _, r, hid = gu.shape
    tr, tc = _tile(r, 512), _tile(hid, 1408)

    def body(gu_ref, d_ref, o_ref):
        g = gu_ref[0]
        da = d_ref[...]
        sg = _sigmoid(g)
        o_ref[0] = (da * gu_ref[1] * sg * (1.0 + g * (1.0 - sg))).astype(BF16)
        o_ref[1] = (da * g * sg).astype(BF16)

    blk = BS((2, tr, tc), lambda i, j: (0, i, j))
    return pl.pallas_call(
        body, grid=(r // tr, hid // tc), in_specs=[blk, BS((tr, tc), lambda i, j: (i, j))],
        out_specs=blk, out_shape=SDS((2, r, hid), BF16), compiler_params=_cp(), name=name)(gu, dact)


def _loss_grad(h, target, *, name):
    r, d = h.shape
    tr = _tile(r, 512)

    def body(h_ref, t_ref, l_ref, g_ref):
        e = h_ref[...] - t_ref[...]
        g_ref[...] = e * (1.0 / d)

        @pl.when(pl.program_id(0) == 0)
        def _():
            l_ref[...] = jnp.zeros_like(l_ref)

        l_ref[...] += jnp.full((1, LANE), 0.5 / d, F32) * jnp.sum(e * e)

    row = BS((tr, d), lambda i: (i, 0))
    return pl.pallas_call(
        body, grid=(r // tr,), in_specs=[row, row],
        out_specs=[BS((1, LANE), lambda i: (0, 0)), row],
        out_shape=[SDS((1, LANE), F32), SDS((r, d), F32)], compiler_params=_cp(), name=name)(h, target)


def _adamw(w, g, m, v, *, name):
    r, c = w.shape
    tr = r if r <= 512 else _rows_tile(r)
    bc1 = 1.0 - ADAM_B1 ** ADAM_STEP
    bc2 = 1.0 - ADAM_B2 ** ADAM_STEP

    def body(w_ref, g_ref, m_ref, v_ref, d_ref, nm_ref, nv_ref):
        gv = g_ref[...]
        nm = ADAM_B1 * m_ref[...] + (1.0 - ADAM_B1) * gv
        nv = ADAM_B2 * v_ref[...] + (1.0 - ADAM_B2) * (gv * gv)
        d_ref[...] = -ADAM_LR * ((nm / bc1) / (jnp.sqrt(nv / bc2) + ADAM_EPS) + ADAM_WD * w_ref[...])
        nm_ref[...] = nm
        nv_ref[...] = nv

    blk = BS((tr, c), lambda i: (i, 0))
    return pl.pallas_call(
        body, grid=(r // tr,), in_specs=[blk] * 4, out_specs=[blk] * 3,
        out_shape=[SDS((r, c), F32)] * 3, compiler_params=_cp(), name=name)(w, g, m, v)


def _rows_tile(r):
    for t in (512, 256, 128, 64, 32, 16, 8):
        if r % t == 0:
            return t
    return r


def _rope_tables(pos, *, name):
    s = pos.shape[0]
    half = ROPE_DIM // 2

    def body(p_ref, c_ref, a_ref, b_ref):
        lane = lax.broadcasted_iota(jnp.int32, (s, ATTN_WIDTH), 1) & (ATTN_HEAD_DIM - 1)
        fi = (lane & (half - 1)).astype(F32)
        inv_freq = jnp.exp(fi * (-2.0 * math.log(ROPE_THETA) / ROPE_DIM))
        ang = p_ref[...].astype(F32) * inv_freq
        cs, sn = jnp.cos(ang), jnp.sin(ang)
        c_ref[...] = jnp.where(lane < ROPE_DIM, cs, 1.0)
        a_ref[...] = jnp.where(lane < half, -sn, 0.0)
        b_ref[...] = jnp.where((lane >= half) & (lane < ROPE_DIM), sn, 0.0)

    full = BS((s, ATTN_WIDTH), lambda i: (0, 0))
    return pl.pallas_call(
        body, grid=(1,), in_specs=[BS((s, 1), lambda i: (0, 0))], out_specs=[full] * 3,
        out_shape=[SDS((s, ATTN_WIDTH), F32)] * 3, compiler_params=_cp(), name=name)(pos)


def _rot(x, c, a, b):
    w = x.shape[1]
    return x * c + pltpu.roll(x, w - ROPE_DIM // 2, 1) * a + pltpu.roll(x, ROPE_DIM // 2, 1) * b


def _rot_t(dy, c, a, b):
    w = dy.shape[1]
    return dy * c + pltpu.roll(dy * a, ROPE_DIM // 2, 1) + pltpu.roll(dy * b, w - ROPE_DIM // 2, 1)


def _attn_count(q0, tq, s):
    dist = (lax.broadcasted_iota(jnp.int32, (tq, s), 0) + q0) - lax.broadcasted_iota(jnp.int32, (tq, s), 1)
    cnt = ((dist <= 128).astype(F32) + (((dist & 3) == 0) & (dist <= 512)).astype(F32)
           + ((dist & 15) == 0).astype(F32))
    return jnp.where(dist >= 0, cnt, 0.0)


ATTN_TQ = 256


def _attn_specs(s, tq):
    def qblk(col):
        return BS((tq, ATTN_WIDTH), lambda i: (i, col))

    def full(col):
        return BS((s, ATTN_WIDTH), lambda i: (0, col))

    return qblk, full


ATTN_TK = 512


def _attn_chunk(i, c, tq, k_ref, v_ref, ck, ak, bk):
    ks = pl.ds(pl.multiple_of(c * ATTN_TK, ATTN_TK), ATTN_TK)
    k = _rot(k_ref[ks, :], ck[ks, :], ak[ks, :], bk[ks, :]).astype(BF16)
    v = v_ref[ks, :].astype(BF16)
    cnt = _attn_count(i * tq - c * ATTN_TK, tq, ATTN_TK)
    return ks, k, v, cnt


def _attn_flash_fwd(proj, tabs, *, name):
    s = proj.shape[0]
    tq = ATTN_TQ
    qblk, full = _attn_specs(s, tq)
    scale = ATTN_HEAD_DIM ** -0.5
    nh = ATTN_HEADS

    def body(q_ref, k_ref, v_ref, cq, aq, bq, ck, ak, bk, o_ref, lse_ref):
        i = pl.program_id(0)
        q = _rot(q_ref[...], cq[...], aq[...], bq[...]) * scale
        head = lax.broadcasted_iota(jnp.int32, (1, ATTN_WIDTH), 1) >> 6
        hms = [(head == h).astype(F32) for h in range(nh)]
        qms = [(q * hm).astype(BF16) for hm in hms]

        def step(c, carry):
            ms, ls, acc = carry
            _, k, v, cnt = _attn_chunk(i, c, tq, k_ref, v_ref, ck, ak, bk)
            valid = cnt > 0.0
            new_ms, new_ls = [], []
            scale_acc = jnp.zeros((tq, ATTN_WIDTH), F32)
            add = jnp.zeros((tq, ATTN_WIDTH), F32)
            for h in range(nh):
                sc = _dot(qms[h], k, 1, 1)
                m_new = jnp.maximum(ms[h], jnp.max(jnp.where(valid, sc, -1e30), axis=-1, keepdims=True))
                alpha = jnp.exp(ms[h] - m_new)
                p = cnt * jnp.exp(jnp.minimum(sc - m_new, 0.0))
                new_ms.append(m_new)
                new_ls.append(alpha * ls[h] + jnp.sum(p, axis=-1, keepdims=True))
                scale_acc = scale_acc + alpha * hms[h]
                add = add + _dot(p, v, 1, 0) * hms[h]
            return new_ms, new_ls, acc * scale_acc + add

        init = ([jnp.full((tq, 1), -1e30, F32)] * nh, [jnp.zeros((tq, 1), F32)] * nh,
                jnp.zeros((tq, ATTN_WIDTH), F32))
        ms, ls, acc = lax.fori_loop(0, i // (ATTN_TK // tq) + 1, step, init)
        inv = jnp.zeros((tq, ATTN_WIDTH), F32)
        lane = lax.broadcasted_iota(jnp.int32, (tq, LANE), 1)
        lse = jnp.zeros((tq, LANE), F32)
        for h in range(nh):
            inv = inv + (1.0 / ls[h]) * hms[h]
            lse = jnp.where(lane == h, ms[h] + jnp.log(ls[h]), lse)
        o_ref[...] = (acc * inv).astype(BF16)
        lse_ref[...] = lse

    return pl.pallas_call(
        body, grid=(s // tq,),
        in_specs=[qblk(0), full(1), full(2), qblk(0), qblk(0), qblk(0), full(0), full(0), full(0)],
        out_specs=[BS((tq, ATTN_WIDTH), lambda i: (i, 0)), BS((tq, LANE), lambda i: (i, 0))],
        out_shape=[SDS((s, ATTN_WIDTH), BF16), SDS((s, LANE), F32)],
        compiler_params=_cp(), name=name)(proj, proj, proj, *tabs, *tabs)


def _attn_flash_bwd(proj, tabs, cat, lse, dcat, *, name):
    s = proj.shape[0]
    tq = ATTN_TQ
    nq = s // tq
    qblk, full = _attn_specs(s, tq)
    scale = ATTN_HEAD_DIM ** -0.5
    nh = ATTN_HEADS

    def body(q_ref, k_ref, v_ref, cq, aq, bq, ck, ak, bk, y_ref, lse_ref, dy_ref,
             dq_ref, dk_ref, dv_ref, dk_acc, dv_acc):
        i = pl.program_id(0)

        @pl.when(i == 0)
        def _():
            dk_acc[...] = jnp.zeros_like(dk_acc)
            dv_acc[...] = jnp.zeros_like(dv_acc)

        q = _rot(q_ref[...], cq[...], aq[...], bq[...]) * scale
        dy = dy_ref[...].astype(F32)
        prod = dy * y_ref[...].astype(F32)
        lse_all = lse_ref[...]
        head = lax.broadcasted_iota(jnp.int32, (1, ATTN_WIDTH), 1) >> 6
        hms = [(head == h).astype(F32) for h in range(nh)]
        qms = [(q * hm).astype(BF16) for hm in hms]
        dyms = [(dy * hm).astype(BF16) for hm in hms]
        deltas = [jnp.sum(prod * hm, axis=-1, keepdims=True) for hm in hms]
        lses = [lse_all[:, h:h + 1] for h in range(nh)]

        def step(c, dq):
            ks, k, v, cnt = _attn_chunk(i, c, tq, k_ref, v_ref, ck, ak, bk)
            dk_c = jnp.zeros((ATTN_TK, ATTN_WIDTH), F32)
            dv_c = jnp.zeros((ATTN_TK, ATTN_WIDTH), F32)
            for h in range(nh):
                sc = _dot(qms[h], k, 1, 1)
                p = cnt * jnp.exp(jnp.minimum(sc - lses[h], 0.0))
                dp = _dot(dyms[h], v, 1, 1)
                ds = p * (dp - deltas[h])
                dq = dq + _dot(ds, k, 1, 0) * hms[h]
                dk_c = dk_c + _dot(ds, qms[h], 0, 0)
                dv_c = dv_c + _dot(p, dyms[h], 0, 0)
            dk_acc[ks, :] += dk_c
            dv_acc[ks, :] += dv_c
            return dq

        dq = lax.fori_loop(0, i // (ATTN_TK // tq) + 1, step, jnp.zeros((tq, ATTN_WIDTH), F32))
        dq_ref[...] = _rot_t(dq * scale, cq[...], aq[...], bq[...]).astype(BF16)

        @pl.when(i == nq - 1)
        def _():
            dk_ref[...] = _rot_t(dk_acc[...], ck[...], ak[...], bk[...]).astype(BF16)
            dv_ref[...] = dv_acc[...].astype(BF16)

    whole = BS((s, ATTN_WIDTH), lambda i: (0, 0))
    return pl.pallas_call(
        body, grid=(nq,),
        in_specs=[qblk(0), full(1), full(2), qblk(0), qblk(0), qblk(0), full(0), full(0), full(0),
                  qblk(0), BS((tq, LANE), lambda i: (i, 0)), qblk(0)],
        out_specs=[BS((tq, ATTN_WIDTH), lambda i: (i, 0)), whole, whole],
        out_shape=[SDS((s, ATTN_WIDTH), BF16)] * 3,
        scratch_shapes=[pltpu.VMEM((s, ATTN_WIDTH), F32), pltpu.VMEM((s, ATTN_WIDTH), F32)],
        compiler_params=_cp(), name=name)(proj, proj, proj, *tabs, *tabs, cat, lse, dcat)


def _shift_down(x, n):
    if n == 0:
        return x
    rows = lax.broadcasted_iota(jnp.int32, x.shape, 0)
    return jnp.where(rows >= n, pltpu.roll(x, n, 0), 0.0)


def _shift_up(x, n):
    if n == 0:
        return x
    t = x.shape[0]
    rows = lax.broadcasted_iota(jnp.int32, x.shape, 0)
    return jnp.where(rows < t - n, pltpu.roll(x, t - n, 0), 0.0)


def _conv_fwd(z, w, kk):
    y = z * w[kk - 1:kk, :]
    for j in range(kk - 1):
        y = y + _shift_down(z, kk - 1 - j) * w[j:j + 1, :]
    return y


def _conv_bwd(z, dy, w, kk):
    dz = dy * w[kk - 1:kk, :]
    dws = []
    for j in range(kk - 1):
        dz = dz + _shift_up(dy, kk - 1 - j) * w[j:j + 1, :]
        dws.append(jnp.sum(dy * _shift_down(z, kk - 1 - j), axis=0, keepdims=True))
    dws.append(jnp.sum(dy * z, axis=0, keepdims=True))
    return dz, jnp.concatenate(dws, axis=0)


def _sconv_fwd(proj, w, *, name):
    s = proj.shape[0]

    def body(b_ref, c_ref, x_ref, w_ref, o_ref):
        y = _conv_fwd(c_ref[...] * x_ref[...], w_ref[...], CONV_K)
        o_ref[...] = (b_ref[...] * y).astype(BF16)

    def col(j):
        return BS((s, LANE), lambda i: (0, j + i))

    return pl.pallas_call(
        body, grid=(CONV_WIDTH // LANE,), in_specs=[col(6), col(8), col(10), BS((CONV_K, LANE), lambda i: (0, i))],
        out_specs=BS((s, LANE), lambda i: (0, i)), out_shape=SDS((s, CONV_WIDTH), BF16),
        compiler_params=_cp(), name=name)(proj, proj, proj, w)


def _sconv_bwd(proj, w, dcat, *, name):
    s = proj.shape[0]

    def body(b_ref, c_ref, x_ref, w_ref, dy_ref, db_ref, dc_ref, dx_ref, dw_ref):
        cv, xv, wv = c_ref[...], x_ref[...], w_ref[...]
        dy = dy_ref[...].astype(F32)
        z = cv * xv
        db_ref[...] = (dy * _conv_fwd(z, wv, CONV_K)).astype(BF16)
        dz, dw = _conv_bwd(z, dy * b_ref[...], wv, CONV_K)
        dc_ref[...] = (dz * xv).astype(BF16)
        dx_ref[...] = (dz * cv).astype(BF16)
        dw_ref[...] = dw

    def col(j):
        return BS((s, LANE), lambda i: (0, j + i))

    out = BS((s, LANE), lambda i: (0, i))
    wspec = BS((CONV_K, LANE), lambda i: (0, i))
    return pl.pallas_call(
        body, grid=(CONV_WIDTH // LANE,), in_specs=[col(6), col(8), col(10), wspec, col(2)],
        out_specs=[out, out, out, wspec],
        out_shape=[SDS((s, CONV_WIDTH), BF16)] * 3 + [SDS((CONV_K, CONV_WIDTH), F32)],
        compiler_params=_cp(), name=name)(proj, proj, proj, w, dcat)


def _l2n(y, scale):
    r = lax.rsqrt(jnp.sum(y * y, axis=-1, keepdims=True) + EPS)
    return y * r * scale, r


def _gdn_pre_fwd(proj, w, *, name):
    s = proj.shape[0]
    nh = GDN_HEADS

    def body(x_ref, w_ref, o_ref):
        j = pl.program_id(0)
        c = _conv_fwd(x_ref[...], w_ref[...], GDN_CONV_K)
        y = c * _sigmoid(c)
        scale = jnp.where(j < nh, GDN_HEAD_DIM ** -0.5, 1.0)
        n, _ = _l2n(y, scale)
        o_ref[...] = jnp.where(j < 2 * nh, n, y)

    return pl.pallas_call(
        body, grid=(3 * nh,),
        in_specs=[BS((s, LANE), lambda j: (0, COL_GDN // LANE + j)), BS((GDN_CONV_K, LANE), lambda j: (0, j))],
        out_specs=BS((s, LANE), lambda j: (0, j)), out_shape=SDS((s, 3 * GDN_WIDTH), F32),
        compiler_params=_cp(), name=name)(proj, w)


def _gdn_pre_bwd(proj, w, dqkv, *, name):
    s = proj.shape[0]
    nh = GDN_HEADS

    def body(x_ref, w_ref, d_ref, dx_ref, dw_ref):
        j = pl.program_id(0)
        xv, wv, dn = x_ref[...], w_ref[...], d_ref[...]
        c = _conv_fwd(xv, wv, GDN_CONV_K)
        sg = _sigmoid(c)
        y = c * sg
        scale = jnp.where(j < nh, GDN_HEAD_DIM ** -0.5, 1.0)
        n, r = _l2n(y, 1.0)
        dns = dn * scale
        dy_norm = r * (dns - n * jnp.sum(dns * n, axis=-1, keepdims=True))
        dy = jnp.where(j < 2 * nh, dy_norm, dn)
        dc = dy * sg * (1.0 + c * (1.0 - sg))
        dx, dw = _conv_bwd(xv, dc, wv, GDN_CONV_K)
        dx_ref[...] = dx.astype(BF16)
        dw_ref[...] = dw

    wspec = BS((GDN_CONV_K, LANE), lambda j: (0, j))
    blk = BS((s, LANE), lambda j: (0, j))
    return pl.pallas_call(
        body, grid=(3 * nh,),
        in_specs=[BS((s, LANE), lambda j: (0, COL_GDN // LANE + j)), wspec, blk],
        out_specs=[blk, wspec], out_shape=[SDS((s, 3 * GDN_WIDTH), BF16), SDS((GDN_CONV_K, 3 * GDN_WIDTH), F32)],
        compiler_params=_cp(), name=name)(proj, w, dqkv)


def _softplus(x):
    return jnp.maximum(x, 0.0) + jnp.log(1.0 + jnp.exp(-jnp.abs(x)))


def _gdn_gates_fwd(proj, a_log, dt_bias, *, name):
    s = proj.shape[0]

    def body(x_ref, al_ref, dt_ref, o_ref):
        xv = x_ref[...]
        lane = lax.broadcasted_iota(jnp.int32, xv.shape, 1)
        g = -jnp.exp(al_ref[...]) * _softplus(xv + dt_ref[...])
        o_ref[...] = jnp.where(lane < GDN_HEADS, g, jnp.where(lane < 2 * GDN_HEADS, _sigmoid(xv), 0.0))

    vec = BS((1, LANE), lambda i: (0, 0))
    return pl.pallas_call(
        body, grid=(1,), in_specs=[BS((s, LANE), lambda i: (0, COL_AB // LANE)), vec, vec],
        out_specs=BS((s, LANE), lambda i: (0, 0)), out_shape=SDS((s, LANE), F32),
        compiler_params=_cp(), name=name)(proj, a_log, dt_bias)


def _gdn_gates_bwd(proj, a_log, dt_bias, dgb, *, name):
    s = proj.shape[0]

    def body(x_ref, al_ref, dt_ref, d_ref, dx_ref, dal_ref, ddt_ref):
        xv, dv = x_ref[...], d_ref[...]
        lane = lax.broadcasted_iota(jnp.int32, xv.shape, 1)
        is_g = lane < GDN_HEADS
        ea = -jnp.exp(al_ref[...])
        z = xv + dt_ref[...]
        da = jnp.where(is_g, dv * ea * _sigmoid(z), 0.0)
        beta = _sigmoid(xv)
        dx_ref[...] = jnp.where(is_g, da, jnp.where(lane < 2 * GDN_HEADS, dv * beta * (1.0 - beta), 0.0)).astype(BF16)
        dal_ref[...] = jnp.sum(jnp.where(is_g, dv * ea * _softplus(z), 0.0), axis=0, keepdims=True)
        ddt_ref[...] = jnp.sum(da, axis=0, keepdims=True)

    vec = BS((1, LANE), lambda i: (0, 0))
    blk = BS((s, LANE), lambda i: (0, 0))
    return pl.pallas_call(
        body, grid=(1,), in_specs=[BS((s, LANE), lambda i: (0, COL_AB // LANE)), vec, vec, blk],
        out_specs=[blk, vec, vec], out_shape=[SDS((s, LANE), BF16), SDS((1, LANE), F32), SDS((1, LANE), F32)],
        compiler_params=_cp(), name=name)(proj, a_log, dt_bias, dgb)


def _col_to_row(col, eye):
    return jnp.sum(jnp.where(eye, col, 0.0), axis=0, keepdims=True)


def _row_to_col(row, eye):
    return jnp.sum(jnp.where(eye, row, 0.0), axis=1, keepdims=True)


GDN_GROUP = 4
TRI_BLOCK_SHIFT = 4


def _gdn_masks(c):
    row = lax.broadcasted_iota(jnp.int32, (c, c), 0)
    col = lax.broadcasted_iota(jnp.int32, (c, c), 1)
    return dict(row=row, col=col, eye=row == col, low=row >= col, strict=row > col, upper=row <= col,
                on_diag=(row >> TRI_BLOCK_SHIFT) == (col >> TRI_BLOCK_SHIFT))


def _tri_inv(a_list, mk):
    eye_f = mk["eye"].astype(F32)
    ds = [jnp.where(mk["on_diag"], a, 0.0) for a in a_list]
    xs = [eye_f - d for d in ds]
    ps = ds
    for _ in range(3):
        ps = [_dot(p, p, 1, 0, precise=True) for p in ps]
        xs = [x + _dot(x, p, 1, 0, precise=True) for x, p in zip(xs, ps)]
    ms = [_dot(x, a - d, 1, 0, precise=True) for x, a, d in zip(xs, a_list, ds)]
    m2s = [_dot(m, m, 1, 0, precise=True) for m in ms]
    ys = [eye_f - m for m in ms]
    ys = [y + _dot(y, m2, 1, 0, precise=True) for y, m2 in zip(ys, m2s)]
    return [_dot(y, x, 1, 0, precise=True) for y, x in zip(ys, xs)]


def _gdn_pre(qs, ks, vs, gs, betas, mk, ts=None):
    c, hd = qs[0].shape
    eye, low = mk["eye"], mk["low"]
    g_rows = [_col_to_row(g, eye) for g in gs]
    d_cols = [jnp.sum(jnp.where(low, gr, 0.0), axis=1, keepdims=True) for gr in g_rows]
    d_rows = [jnp.sum(jnp.where(mk["upper"], g, 0.0), axis=0, keepdims=True) for g in gs]
    rels = [jnp.where(low, jnp.exp(jnp.minimum(dc - dr, 0.0)), 0.0) for dc, dr in zip(d_cols, d_rows)]
    d_lasts = [dc[c - 1:c, :] for dc in d_cols]
    es = [jnp.exp(dc) for dc in d_cols]
    fs = [jnp.exp(dl - dc) for dl, dc in zip(d_lasts, d_cols)]
    cds = [jnp.exp(dl) for dl in d_lasts]
    kbs = [k * b for k, b in zip(ks, betas)]
    kbqs = [jnp.concatenate([kb, q], axis=0) for kb, q in zip(kbs, qs)]
    kqk = [_dot(kbq, k, 1, 1) for kbq, k in zip(kbqs, ks)]
    kks = [x[:c, :] for x in kqk]
    qks = [x[c:, :] for x in kqk]
    if ts is None:
        ts = _tri_inv([jnp.where(mk["strict"], kk * rel, 0.0) for kk, rel in zip(kks, rels)], mk)
    vbs = [v * b for v, b in zip(vs, betas)]
    kbes = [kb * e for kb, e in zip(kbs, es)]
    uws = [_dot(t, jnp.concatenate([vb, kbe], axis=1), 1, 0) for t, vb, kbe in zip(ts, vbs, kbes)]
    out = []
    for i in range(len(qs)):
        out.append(dict(rel=rels[i], e=es[i], f=fs[i], cd=cds[i], kb=kbs[i], kbq=kbqs[i], kk=kks[i], qk=qks[i],
                        t=ts[i], u=uws[i][:, :hd], w=uws[i][:, hd:], uw=uws[i], attn=qks[i] * rels[i],
                        qd=qs[i] * es[i], kd=ks[i] * fs[i]))
    return out


def _gdn_apply(pres, sts, leaving=True):
    c = pres[0]["u"].shape[0]
    wqs = [_dot(jnp.concatenate([p["w"], p["qd"]], axis=0), st, 1, 0) for p, st in zip(pres, sts)]
    vns = [p["u"] - x[:c, :] for p, x in zip(pres, wqs)]
    os_ = [x[c:, :] + _dot(p["attn"], vn, 1, 0) for p, x, vn in zip(pres, wqs, vns)]
    if not leaving:
        return vns, os_, None
    new = [p["cd"] * st + _dot(p["kd"], vn, 0, 0) for p, st, vn in zip(pres, sts, vns)]
    return vns, os_, new


def _gdn_bwd_rest(qs, ks, vs, betas, sts, pres, vns, dos, dvns, dsts, mk):
    c, hd = qs[0].shape
    eye = mk["eye"]
    n = range(len(qs))
    dkds = [_dot(vns[i], dsts[i], 1, 1) for i in n]
    dcds = [jnp.sum(sts[i] * dsts[i]) for i in n]
    dattns = [jnp.where(mk["low"], _dot(dos[i], vns[i], 1, 1), 0.0) for i in n]
    dqdws = [_dot(jnp.concatenate([dos[i], -dvns[i]], axis=0), sts[i], 1, 1) for i in n]
    dqds = [x[:c, :] for x in dqdws]
    dws = [x[c:, :] for x in dqdws]
    dvks = [_dot(pres[i]["t"], jnp.concatenate([dvns[i], dws[i]], axis=1), 0, 0) for i in n]
    das = [jnp.where(mk["strict"], -_dot(dvks[i], pres[i]["uw"], 1, 1), 0.0) for i in n]
    dkqs = [jnp.concatenate([das[i] * pres[i]["rel"], dattns[i] * pres[i]["rel"]], axis=0) for i in n]
    dkbdq = [_dot(dkqs[i], ks[i], 1, 0) for i in n]
    dk0 = [_dot(dkqs[i], pres[i]["kbq"], 0, 0) for i in n]
    out = []
    rows1 = lax.broadcasted_iota(jnp.int32, (c, 1), 0)
    for i in n:
        p = pres[i]
        dvb, dkbe = dvks[i][:, :hd], dvks[i][:, hd:]
        grel = (das[i] * p["kk"] + dattns[i] * p["qk"]) * p["rel"]
        dkb = dkbdq[i][:c, :] + dkbe * p["e"]
        dk = dk0[i] + dkds[i] * p["f"] + dkb * betas[i]
        dq = dkbdq[i][c:, :] + dqds[i] * p["e"]
        dv = dvb * betas[i]
        dbeta = jnp.sum(dkb * ks[i], axis=1, keepdims=True) + jnp.sum(dvb * vs[i], axis=1, keepdims=True)
        de = jnp.sum(dqds[i] * qs[i], axis=1, keepdims=True) + jnp.sum(dkbe * p["kb"], axis=1, keepdims=True)
        dff = jnp.sum(dkds[i] * ks[i], axis=1, keepdims=True) * p["f"]
        dd = (de * p["e"] - dff + jnp.sum(grel, axis=1, keepdims=True)
              - _row_to_col(jnp.sum(grel, axis=0, keepdims=True), eye))
        dd = dd + jnp.where(rows1 == c - 1, jnp.sum(dff) + dcds[i] * p["cd"], 0.0)
        dg = jnp.sum(jnp.where(mk["upper"], _col_to_row(dd, eye), 0.0), axis=1, keepdims=True)
        out.append((dq, dk, dv, dg, dbeta))
    return out


def _gdn_specs(c):
    def qkv(j):
        return BS((c, GDN_WIDTH), lambda n: (n, j))

    return qkv


def _gdn_core_fwd(qkv, gbeta, proj, norm_w, *, name):
    s = qkv.shape[0]
    c, nh, hd, grp = GDN_CHUNK, GDN_HEADS, GDN_HEAD_DIM, GDN_GROUP
    n_chunks = s // c
    blk = _gdn_specs(grp * c)
    inst = [(sub, h) for sub in range(grp) for h in range(nh)]

    def body(q_ref, k_ref, v_ref, gb_ref, gate_ref, nw_ref, y_ref, st_ref, t_ref, state):
        @pl.when(pl.program_id(0) == 0)
        def _():
            state[...] = jnp.zeros_like(state)

        mk = _gdn_masks(c)
        rows = [slice(sub * c, (sub + 1) * c) for sub in range(grp)]
        lanes = [slice(h * hd, (h + 1) * hd) for h in range(nh)]
        gbs = [gb_ref[r, :] for r in rows]
        pres = _gdn_pre([q_ref[rows[sub], lanes[h]] for sub, h in inst], [k_ref[rows[sub], lanes[h]] for sub, h in inst],
                        [v_ref[rows[sub], lanes[h]] for sub, h in inst], [gbs[sub][:, h:h + 1] for sub, h in inst],
                        [gbs[sub][:, nh + h:nh + h + 1] for sub, h in inst], mk)
        sts = [state[ls, :] for ls in lanes]
        outs = []
        for sub in range(grp):
            for h in range(nh):
                st_ref[pl.ds((sub * nh + h) * hd, hd), :] = sts[h]
            _, os_, sts = _gdn_apply(pres[sub * nh:(sub + 1) * nh], sts)
            outs += os_
        for h in range(nh):
            state[lanes[h], :] = sts[h]
        nw = nw_ref[...]
        for i, (sub, h) in enumerate(inst):
            t_ref[pl.ds(i * c, c), :] = pres[i]["t"]
            o = outs[i]
            gate = gate_ref[rows[sub], lanes[h]]
            rs = lax.rsqrt(jnp.mean(o * o, axis=-1, keepdims=True) + EPS)
            y_ref[rows[sub], lanes[h]] = (o * rs * nw * (gate * _sigmoid(gate))).astype(BF16)

    return pl.pallas_call(
        body, grid=(n_chunks // grp,),
        in_specs=[blk(0), blk(1), blk(2), BS((grp * c, LANE), lambda n: (n, 0)),
                  BS((grp * c, GDN_WIDTH), lambda n: (n, COL_GATE // GDN_WIDTH)), BS((1, hd), lambda n: (0, 0))],
        out_specs=[BS((grp * c, GDN_WIDTH), lambda n: (n, 0)), BS((grp * nh * hd, hd), lambda n: (n, 0)),
                   BS((grp * nh * c, c), lambda n: (n, 0))],
        out_shape=[SDS((s, GDN_WIDTH), BF16), SDS((n_chunks * nh * hd, hd), F32), SDS((n_chunks * nh * c, c), F32)],
        scratch_shapes=[pltpu.VMEM((nh * hd, hd), F32)],
        compiler_params=_cp(), name=name)(qkv, qkv, qkv, gbeta, proj, norm_w)


def _gdn_core_bwd(qkv, gbeta, proj, norm_w, states, tinv, dcat, *, name):
    s = qkv.shape[0]
    c, nh, hd, grp = GDN_CHUNK, GDN_HEADS, GDN_HEAD_DIM, GDN_GROUP
    n_chunks = s // c
    last = n_chunks // grp - 1
    inst = [(sub, h) for sub in range(grp) for h in range(nh)]

    def rev(j, w):
        return BS((grp * c, w), lambda n: (last - n, j))

    def body(q_ref, k_ref, v_ref, gb_ref, gate_ref, nw_ref, st_ref, t_ref, dy_ref,
             dqkv_ref, dgb_ref, dgate_ref, dnw_ref, dstate):
        @pl.when(pl.program_id(0) == 0)
        def _():
            dstate[...] = jnp.zeros_like(dstate)
            dnw_ref[...] = jnp.zeros_like(dnw_ref)

        mk = _gdn_masks(c)
        rows = [slice(sub * c, (sub + 1) * c) for sub in range(grp)]
        lanes = [slice(h * hd, (h + 1) * hd) for h in range(nh)]
        gbs = [gb_ref[r, :] for r in rows]
        qs = [q_ref[rows[sub], lanes[h]] for sub, h in inst]
        ks = [k_ref[rows[sub], lanes[h]] for sub, h in inst]
        vs = [v_ref[rows[sub], lanes[h]] for sub, h in inst]
        betas = [gbs[sub][:, nh + h:nh + h + 1] for sub, h in inst]
        sts = [st_ref[pl.ds(i * hd, hd), :] for i in range(len(inst))]
        pres = _gdn_pre(qs, ks, vs, [gbs[sub][:, h:h + 1] for sub, h in inst], betas, mk,
                        ts=[t_ref[pl.ds(i * c, c), :] for i in range(len(inst))])
        vns, outs, _ = _gdn_apply(pres, sts, leaving=False)

        nw = nw_ref[...]
        dnw = jnp.zeros((1, hd), F32)
        dos = []
        for i, (sub, h) in enumerate(inst):
            o = outs[i]
            gate = gate_ref[rows[sub], lanes[h]]
            dy = dy_ref[rows[sub], lanes[h]].astype(F32)
            sg = _sigmoid(gate)
            rs = lax.rsqrt(jnp.mean(o * o, axis=-1, keepdims=True) + EPS)
            nrm = o * rs
            dgate_ref[rows[sub], lanes[h]] = (dy * nrm * nw * sg * (1.0 + gate * (1.0 - sg))).astype(BF16)
            dnv = dy * (gate * sg)
            dnw = dnw + jnp.sum(dnv * nrm, axis=0, keepdims=True)
            dno = dnv * nw
            dos.append(rs * (dno - nrm * jnp.mean(dno * nrm, axis=-1, keepdims=True)))
        dnw_ref[...] += dnw

        from_o = [_dot(p["attn"], do, 0, 0) for p, do in zip(pres, dos)]
        to_st = [_dot(p["qd"], do, 0, 0) for p, do in zip(pres, dos)]
        dst = [dstate[ls, :] for ls in lanes]
        dsts = [None] * len(inst)
        dvns = [None] * len(inst)
        for sub in reversed(range(grp)):
            idx = [sub * nh + h for h in range(nh)]
            for h, i in enumerate(idx):
                dsts[i] = dst[h]
                dvns[i] = from_o[i] + _dot(pres[i]["kd"], dst[h], 1, 0)
            dst = [pres[i]["cd"] * dst[h] + to_st[i] - _dot(pres[i]["w"], dvns[i], 0, 0) for h, i in enumerate(idx)]
        for h in range(nh):
            dstate[lanes[h], :] = dst[h]

        grads = _gdn_bwd_rest(qs, ks, vs, betas, sts, pres, vns, dos, dvns, dsts, mk)
        lane = lax.broadcasted_iota(jnp.int32, (c, LANE), 1)
        dgb = [jnp.zeros((c, LANE), F32) for _ in range(grp)]
        for (sub, h), (dq, dk, dv, dg, dbeta) in zip(inst, grads):
            dqkv_ref[rows[sub], lanes[h]] = dq
            dqkv_ref[rows[sub], slice(GDN_WIDTH + h * hd, GDN_WIDTH + (h + 1) * hd)] = dk
            dqkv_ref[rows[sub], slice(2 * GDN_WIDTH + h * hd, 2 * GDN_WIDTH + (h + 1) * hd)] = dv
            dgb[sub] = jnp.where(lane == h, dg, jnp.where(lane == nh + h, dbeta, dgb[sub]))
        for sub in range(grp):
            dgb_ref[rows[sub], :] = dgb[sub]

    return pl.pallas_call(
        body, grid=(n_chunks // grp,),
        in_specs=[rev(0, GDN_WIDTH), rev(1, GDN_WIDTH), rev(2, GDN_WIDTH), rev(0, LANE),
                  rev(COL_GATE // GDN_WIDTH, GDN_WIDTH), BS((1, hd), lambda n: (0, 0)),
                  BS((grp * nh * hd, hd), lambda n: (last - n, 0)), BS((grp * nh * c, c), lambda n: (last - n, 0)),
                  rev(1, GDN_WIDTH)],
        out_specs=[rev(0, 3 * GDN_WIDTH), rev(0, LANE), rev(0, GDN_WIDTH), BS((1, hd), lambda n: (0, 0))],
        out_shape=[SDS((s, 3 * GDN_WIDTH), F32), SDS((s, LANE), F32), SDS((s, GDN_WIDTH), BF16), SDS((1, hd), F32)],
        scratch_shapes=[pltpu.VMEM((nh * hd, hd), F32)],
        compiler_params=_cp(), name=name)(qkv, qkv, qkv, gbeta, proj, norm_w, states, tinv, dcat)


XATTN_TQ = 512


def _xattn_probs(qh, kh):
    sc = _dot(qh, kh, 1, 1) * (XATTN_HEAD_DIM ** -0.5)
    p = jnp.exp(sc - jnp.max(sc, axis=-1, keepdims=True))
    return p / jnp.sum(p, axis=-1, keepdims=True)


def _xattn_fwd(q, kv, *, name):
    s, d = q.shape
    m = kv.shape[0]
    tq, hd = _tile(s, XATTN_TQ), XATTN_HEAD_DIM

    def body(q_ref, k_ref, v_ref, o_ref):
        for h in range(XATTN_HEADS):
            ls = slice(h * hd, (h + 1) * hd)
            p = _xattn_probs(q_ref[:, ls], k_ref[:, ls])
            o_ref[:, ls] = _dot(p, v_ref[:, ls], 1, 0).astype(BF16)

    return pl.pallas_call(
        body, grid=(s // tq,),
        in_specs=[BS((tq, d), lambda i: (i, 0)), BS((m, d), lambda i: (0, 0)), BS((m, d), lambda i: (0, 1))],
        out_specs=BS((tq, d), lambda i: (i, 0)), out_shape=SDS((s, d), BF16),
        compiler_params=_cp(), name=name)(q, kv, kv)


def _xattn_bwd(q, kv, do, *, name):
    s, d = q.shape
    m = kv.shape[0]
    tq, hd = _tile(s, XATTN_TQ), XATTN_HEAD_DIM
    scale = hd ** -0.5

    def body(q_ref, k_ref, v_ref, do_ref, dq_ref, dkv_ref):
        @pl.when(pl.program_id(0) == 0)
        def _():
            dkv_ref[...] = jnp.zeros_like(dkv_ref)

        for h in range(XATTN_HEADS):
            ls = slice(h * hd, (h + 1) * hd)
            vs = slice(d + h * hd, d + (h + 1) * hd)
            qh, kh, doh = q_ref[:, ls], k_ref[:, ls], do_ref[:, ls]
            p = _xattn_probs(qh, kh)
            dp = _dot(doh, v_ref[:, ls], 1, 1)
            ds = p * (dp - jnp.sum(p * dp, axis=-1, keepdims=True)) * scale
            dq_ref[:, ls] = _dot(ds, kh, 1, 0).astype(BF16)
            dkv_ref[:, ls] += _dot(ds, qh, 0, 0)
            dkv_ref[:, vs] += _dot(p, doh, 0, 0)

    row = BS((tq, d), lambda i: (i, 0))
    return pl.pallas_call(
        body, grid=(s // tq,),
        in_specs=[row, BS((m, d), lambda i: (0, 0)), BS((m, d), lambda i: (0, 1)), row],
        out_specs=[row, BS((m, 2 * d), lambda i: (0, 0))],
        out_shape=[SDS((s, d), BF16), SDS((m, 2 * d), F32)],
        compiler_params=_cp(), name=name)(q, kv, kv, do)


def _pad_lanes(vec4):
    return jnp.zeros((1, LANE), F32).at[0, :GDN_HEADS].set(vec4)


def _layer_fwd(h0, mem, tabs, p):
    sv = dict(h0=h0)
    hn1 = _rmsnorm(h0, p["norm_mix_pre"], name="norm_mix_pre")
    proj = _mm(hn1, p["w_in"], name="mm_in")
    ya, lse = _attn_flash_fwd(proj, tabs, name="attn_fwd")
    yc = _sconv_fwd(proj, p["conv_short"], name="sconv_fwd")
    qkv = _gdn_pre_fwd(proj, p["conv_gdn"], name="gdn_pre_fwd")
    gbeta = _gdn_gates_fwd(proj, p["gdn_a_log"], p["gdn_dt_bias"], name="gdn_gates_fwd")
    yg, states, tinv = _gdn_core_fwd(qkv, gbeta, proj, p["gdn_norm"], name="gdn_core_fwd")
    cat = jnp.concatenate([ya, yc, yg], axis=-1)
    mix = _mm(cat, p["w_out"], name="mm_out")
    h1 = _resnorm(h0, mix, p["norm_mix_post"], name="norm_mix_post")
    hn2 = _rmsnorm(h1, p["norm_xattn_pre"], name="norm_xattn_pre")
    memn = _rmsnorm(mem, p["norm_mem"], name="norm_mem")
    xq = _mm(hn2, p["w_xq"], out_dtype=BF16, name="mm_xq")
    kv = _mm(memn, p["w_xkv"], out_dtype=BF16, b_shards=True, name="mm_xkv")
    xo = _xattn_fwd(xq, kv, name="xattn_fwd")
    xa = _mm(xo, p["w_xo"], name="mm_xo")
    h2 = _resnorm(h1, xa, p["norm_xattn_post"], name="norm_xattn_post")
    hn3 = _rmsnorm(h2, p["norm_ffn_pre"], name="norm_ffn_pre")
    gu = _mm(hn3, p["w_gate_up"], b_shards=True, halves=True, name="mm_gate_up")
    act = _swiglu(gu, name="swiglu_fwd")
    f = _mm(act, p["w_down"], name="mm_down")
    h3 = _resnorm(h2, f, p["norm_ffn_post"], name="norm_ffn_post")
    sv.update(hn1=hn1, proj=proj, lse=lse, qkv=qkv, gbeta=gbeta, states=states, tinv=tinv, cat=cat, mix=mix, h1=h1, hn2=hn2,
              memn=memn, xq=xq, kv=kv, xo=xo, xa=xa, h2=h2, hn3=hn3, gu=gu, act=act, f=f)
    return h3, sv


def _layer_bwd(dh3, mem, tabs, p, sv, after_ffn=None, after_xattn=None):
    g = {}
    df, g["norm_ffn_post"] = _rmsnorm_bwd(sv["f"], p["norm_ffn_post"], dh3, name="norm_ffn_post_bwd")
    dact = _mm(df, p["w_down"], tb=True, name="mm_down_da")
    g["w_down"] = _mm(sv["act"], df, ta=True, name="mm_down_dw")
    dgu = _swiglu_bwd(sv["gu"], dact, name="swiglu_bwd")
    dhn3 = _mm(dgu, p["w_gate_up"], tb=True, b_shards=True, halves=True, name="mm_gate_up_da")
    g["w_gate_up"] = _mm(sv["hn3"], dgu, ta=True, out_shards=True, halves=True, name="mm_gate_up_dw")
    dh2, g["norm_ffn_pre"] = _rmsnorm_bwd(sv["h2"], p["norm_ffn_pre"], dhn3, res=dh3, name="norm_ffn_pre_bwd")
    token = after_ffn(dh2) if after_ffn is not None else None
    w_post = p["norm_xattn_post"] if token is None else p["norm_xattn_post"] + token[:1, :1]
    dxa, g["norm_xattn_post"] = _rmsnorm_bwd(sv["xa"], w_post, dh2, name="norm_xattn_post_bwd")
    dxo = _mm(dxa, p["w_xo"], tb=True, name="mm_xo_da")
    g["w_xo"] = _mm(sv["xo"], dxa, ta=True, name="mm_xo_dw")
    dxq, dkv = _xattn_bwd(sv["xq"], sv["kv"], dxo, name="xattn_bwd")
    dhn2 = _mm(dxq, p["w_xq"], tb=True, name="mm_xq_da")
    g["w_xq"] = _mm(sv["hn2"], dxq, ta=True, name="mm_xq_dw")
    dmemn = _mm(dkv, p["w_xkv"], tb=True, b_shards=True, name="mm_xkv_da")
    g["w_xkv"] = _mm(sv["memn"], dkv, ta=True, out_shards=True, name="mm_xkv_dw")
    _, g["norm_mem"] = _rmsnorm_bwd(mem, p["norm_mem"], dmemn, name="norm_mem_bwd")
    dh1, g["norm_xattn_pre"] = _rmsnorm_bwd(sv["h1"], p["norm_xattn_pre"], dhn2, res=dh2, name="norm_xattn_pre_bwd")
    token = after_xattn(g, dh1) if after_xattn is not None else None
    w_post = p["norm_mix_post"] if token is None else p["norm_mix_post"] + token[:1, :1]
    dmix, g["norm_mix_post"] = _rmsnorm_bwd(sv["mix"], w_post, dh1, name="norm_mix_post_bwd")
    dcat = _mm(dmix, p["w_out"], tb=True, name="mm_out_da")
    g["w_out"] = _mm(sv["cat"], dmix, ta=True, name="mm_out_dw")
    proj = sv["proj"]
    daq, dak, dav = _attn_flash_bwd(proj, tabs, sv["cat"], sv["lse"], dcat, name="attn_bwd")
    dcb, dcc, dcx, g["conv_short"] = _sconv_bwd(proj, p["conv_short"], dcat, name="sconv_bwd")
    dqkv, dgbeta, dgate, g["gdn_norm"] = _gdn_core_bwd(sv["qkv"], sv["gbeta"], proj, p["gdn_norm"], sv["states"], sv["tinv"],
                                                        dcat, name="gdn_core_bwd")
    dgqkv, g["conv_gdn"] = _gdn_pre_bwd(proj, p["conv_gdn"], dqkv, name="gdn_pre_bwd")
    dab, g["gdn_a_log"], g["gdn_dt_bias"] = _gdn_gates_bwd(proj, p["gdn_a_log"], p["gdn_dt_bias"], dgbeta,
                                                          name="gdn_gates_bwd")
    s = proj.shape[0]
    dproj = jnp.concatenate([daq, dak, dav, dcb, dcc, dcx, dgqkv, dgate, dab,
                             jnp.zeros((s, IN_PAD - COL_AB - LANE), BF16)], axis=-1)
    dhn1 = _mm(dproj, p["w_in"], tb=True, name="mm_in_da")
    g["w_in"] = _mm(sv["hn1"], dproj, ta=True, name="mm_in_dw")
    dh0, g["norm_mix_pre"] = _rmsnorm_bwd(sv["h0"], p["norm_mix_pre"], dhn1, res=dh1, name="norm_mix_pre_bwd")
    return dh0, g


MATRICES = ("w_in", "w_out", "w_xq", "w_xkv", "w_xo", "w_gate_up", "w_down")
VECTORS = ("norm_mix_pre", "norm_mix_post", "conv_short", "conv_gdn", "gdn_a_log", "gdn_dt_bias", "gdn_norm",
           "norm_mem", "norm_xattn_pre", "norm_xattn_post", "norm_ffn_pre", "norm_ffn_post")


def _w_in_segments(n_shards=4):
    c = IN_WIDTH // n_shards
    moves = ((0, COL_GATE, 0), (COL_GATE, COL_GATE + 8, COL_AB), (COL_GATE + 8, IN_WIDTH, COL_GATE))
    segs = []
    for s in range(n_shards):
        for lo, hi, dst in moves:
            a, b = max(lo, s * c), min(hi, (s + 1) * c)
            if a < b:
                segs.append((s, a - s * c, dst + a - lo, b - a))
    return segs


def _w_in_pack(g, *, name):
    ns, r, c = g.shape
    tr = _tile(r, 256)

    def body(g_ref, o_ref):
        o_ref[:, IN_WIDTH:] = jnp.zeros((tr, IN_PAD - IN_WIDTH), o_ref.dtype)
        for s, src, dst, width in _w_in_segments(ns):
            o_ref[:, dst:dst + width] = g_ref[s, :, src:src + width]

    return pl.pallas_call(
        body, grid=(r // tr,), in_specs=[BS((ns, tr, c), lambda i: (0, i, 0))],
        out_specs=BS((tr, IN_PAD), lambda i: (i, 0)), out_shape=SDS((r, IN_PAD), g.dtype),
        compiler_params=_cp(), name=name)(g)


def _w_in_unpack(dw, *, name):
    r = dw.shape[0]
    ns, c = 4, IN_WIDTH // 4
    tr = _tile(r, 256)

    def body(d_ref, o_ref):
        for s, src, dst, width in _w_in_segments(ns):
            o_ref[s, :, src:src + width] = d_ref[:, dst:dst + width]

    return pl.pallas_call(
        body, grid=(r // tr,), in_specs=[BS((tr, IN_PAD), lambda i: (i, 0))],
        out_specs=BS((ns, tr, c), lambda i: (0, i, 0)), out_shape=SDS((ns, r, c), dw.dtype),
        compiler_params=_cp(), name=name)(dw)


def _layer_params(full, l):
    p = {n: full[n][l] for n in MATRICES}
    for n in VECTORS:
        v = full[n][l]
        if n in ("gdn_a_log", "gdn_dt_bias"):
            p[n] = _pad_lanes(v)
        elif v.ndim == 1:
            p[n] = v.reshape(1, -1)
        else:
            p[n] = v
    return p


def _local_step(x, mem, pos, target, full, matrices_for=None, on_grads=None, mid_backward=None,
                after_xattn=None):
    tabs = _rope_tables(pos, name="rope_tables")
    h = x
    saved, params = [], []
    for l in range(DEPTH):
        if matrices_for is not None:
            full = {**full, **{n: {l: v} for n, v in matrices_for(l, h).items()}}
        p = _layer_params(full, l)
        h, sv = _layer_fwd(h, mem, tabs, p)
        params.append(p)
        saved.append(sv)
    loss_row, dh = _loss_grad(h, target, name="loss_grad")
    grads = [None] * DEPTH
    token = None
    for l in reversed(range(DEPTH)):
        p = params[l]
        if token is not None:
            p = {**p, "norm_ffn_post": p["norm_ffn_post"] + token[:1, :1]}
        late = None if after_xattn is None else (lambda g, dh1, l=l: after_xattn(l, g, dh1))
        dh, grads[l] = _layer_bwd(dh, mem, tabs, p, saved[l], after_ffn=mid_backward, after_xattn=late)
        if on_grads is not None:
            token = on_grads(l, grads[l], dh)
    return loss_row, dh, grads


ANY = pl.BlockSpec(memory_space=pl.ANY)
MESH = pl.DeviceIdType.MESH


def _flip(pos, mask):
    return tuple(1 - v if m else v for v, m in zip(pos, mask))


def _exchange(ins, out_shapes, remote, local, *, name):
    n_in = len(ins)
    n_out = len(out_shapes)

    def at(ref, idx):
        return ref.at[idx] if idx else ref

    def body(*refs):
        in_refs = refs[:n_in]
        out_refs = refs[n_in:n_in + n_out]
        send_sems, recv_sems, local_sems = refs[n_in + n_out:]
        me = (lax.axis_index("x"), lax.axis_index("y"), lax.axis_index("c"))
        waits = []
        for k, (ii, src_at, oi, dst_at, mask) in enumerate(remote):
            peer = _flip(me, mask)
            pltpu.make_async_remote_copy(
                src_ref=at(in_refs[ii], src_at(me, peer)), dst_ref=at(out_refs[oi], dst_at(me)),
                send_sem=send_sems.at[k], recv_sem=recv_sems.at[k], device_id=peer, device_id_type=MESH).start()
            waits.append(pltpu.make_async_remote_copy(
                src_ref=at(in_refs[ii], src_at(peer, me)), dst_ref=at(out_refs[oi], dst_at(peer)),
                send_sem=send_sems.at[k], recv_sem=recv_sems.at[k], device_id=peer, device_id_type=MESH))
        own = []
        for k, (ii, src_at, oi, dst_at) in enumerate(local):
            cp = pltpu.make_async_copy(at(in_refs[ii], src_at(me)), at(out_refs[oi], dst_at(me)), local_sems.at[k])
            cp.start()
            own.append(cp)
        for w in waits:
            w.wait_send()
            w.wait_recv()
        for cp in own:
            cp.wait()

    return pl.pallas_call(
        body, in_specs=[ANY] * n_in, out_specs=[ANY] * n_out, out_shape=list(out_shapes),
        scratch_shapes=[pltpu.SemaphoreType.DMA((len(remote),)), pltpu.SemaphoreType.DMA((len(remote),)),
                        pltpu.SemaphoreType.DMA((max(len(local), 1),))],
        name=name)(*ins)


HBM = pl.BlockSpec(memory_space=pltpu.HBM)
SEM = pl.BlockSpec(memory_space=pltpu.SEMAPHORE)
SPLIT_EFFECT = pltpu.SideEffectType.DATAFLOW_SIDE_EFFECTING


def _exchange_start(ins, land_shapes, remote, *, name):
    n_in, n_land, n_cp = len(ins), len(land_shapes), len(remote)

    def body(*refs):
        in_refs, land_refs = refs[:n_in], refs[n_in:n_in + n_land]
        send_sems, recv_sems = refs[n_in + n_land:n_in + n_land + 2]
        token = refs[-1]
        me = (lax.axis_index("x"), lax.axis_index("y"), lax.axis_index("c"))
        for k, (ii, src_at, oi, dst_at, mask) in enumerate(remote):
            peer = _flip(me, mask)
            idx_s, idx_d = src_at(me, peer), dst_at(me)
            pltpu.make_async_remote_copy(
                src_ref=in_refs[ii].at[idx_s] if idx_s else in_refs[ii],
                dst_ref=land_refs[oi].at[idx_d] if idx_d else land_refs[oi],
                send_sem=send_sems.at[k], recv_sem=recv_sems.at[k], device_id=peer, device_id_type=MESH).start()
        token[...] = jnp.zeros_like(token)

    buffers = [pltpu.with_memory_space_constraint(a, pltpu.HBM) for a in ins]
    buffers += [pltpu.with_memory_space_constraint(lax.empty(s.shape, s.dtype), pltpu.HBM) for s in land_shapes]
    out = pl.pallas_call(
        body, name=name,
        out_shape=(pltpu.SemaphoreType.DMA((n_cp,)), pltpu.SemaphoreType.DMA((n_cp,)),
                   *[pltpu.HBM(b.shape, b.dtype) for b in buffers], SDS((8, LANE), F32)),
        in_specs=[HBM] * len(buffers),
        out_specs=(SEM, SEM, *[HBM] * len(buffers), pl.BlockSpec(memory_space=pltpu.VMEM)),
        input_output_aliases={i: 2 + i for i in range(len(buffers))},
        compiler_params=pltpu.CompilerParams(has_side_effects=SPLIT_EFFECT))(*buffers)
    return out[0], out[1], list(out[2:2 + n_in]), list(out[2 + n_in:2 + n_in + n_land]), out[-1]


def _exchange_wait(send_sems, recv_sems, ins, lands, remote, after, *, name):
    n_in, n_land = len(ins), len(lands)

    def body(*refs):
        in_refs, land_refs = refs[:n_in], refs[n_in:n_in + n_land]
        send_sems_, recv_sems_ = refs[n_in + n_land:n_in + n_land + 2]
        me = (lax.axis_index("x"), lax.axis_index("y"), lax.axis_index("c"))
        for k, (ii, src_at, oi, dst_at, mask) in enumerate(remote):
            peer = _flip(me, mask)
            idx_s, idx_d = src_at(peer, me), dst_at(peer)
            cp = pltpu.make_async_remote_copy(
                src_ref=in_refs[ii].at[idx_s] if idx_s else in_refs[ii],
                dst_ref=land_refs[oi].at[idx_d] if idx_d else land_refs[oi],
                send_sem=send_sems_.at[k], recv_sem=recv_sems_.at[k], device_id=peer, device_id_type=MESH)
            cp.wait_send()
            cp.wait_recv()

    buffers = list(ins) + list(lands)
    out = pl.pallas_call(
        body, name=name, out_shape=tuple(pltpu.HBM(b.shape, b.dtype) for b in buffers),
        in_specs=[HBM] * len(buffers) + [SEM, SEM, ANY], out_specs=tuple([HBM] * len(buffers)),
        input_output_aliases={i: i for i in range(len(buffers))},
        compiler_params=pltpu.CompilerParams(has_side_effects=SPLIT_EFFECT))(*buffers, send_sems, recv_sems, after)
    return list(out[:n_in]), list(out[n_in:])


def _chip(pos):
    return 2 * pos[0] + pos[1]


XY_MASKS = ((1, 0, 0), (0, 1, 0), (1, 1, 0))
SIBLING = (0, 0, 1)
ALL_MASKS = tuple((a, b, c) for a in (0, 1) for b in (0, 1) for c in (0, 1))[1:]


def _gather_xy(arrs, *, name):
    n = len(arrs)
    outs = [SDS((4,) + a.shape, a.dtype) for a in arrs]
    halves = [a.shape[0] // 2 for a in arrs]

    def body(*refs):
        in_refs, out_refs = refs[:n], refs[n:2 * n]
        ici_send, ici_recv, d2d_send, d2d_recv = refs[2 * n:]
        me = (lax.axis_index("x"), lax.axis_index("y"), lax.axis_index("c"))
        sibling = _flip(me, SIBLING)
        flows = []
        for i in range(n):
            mine = pl.ds(me[2] * halves[i], halves[i])
            other = pl.ds(sibling[2] * halves[i], halves[i])
            for m in XY_MASKS:
                k = len(flows)
                peer = _flip(me, m)

                def remote(src, dst, sems, to, k=k):
                    return pltpu.make_async_remote_copy(src_ref=src, dst_ref=dst, send_sem=sems[0].at[k],
                                                        recv_sem=sems[1].at[k], device_id=to, device_id_type=MESH)

                landed = out_refs[i].at[_chip(peer), mine]
                send = remote(in_refs[i].at[mine], out_refs[i].at[_chip(me), mine], (ici_send, ici_recv), peer)
                send.start()
                arrive = remote(in_refs[i].at[mine], landed, (ici_send, ici_recv), peer)
                forward = remote(landed, landed, (d2d_send, d2d_recv), sibling)
                handed = remote(out_refs[i].at[_chip(peer), other], out_refs[i].at[_chip(peer), other],
                                (d2d_send, d2d_recv), sibling)
                flows.append((send, arrive, forward, handed))
        for _, arrive, forward, _ in flows:
            arrive.wait_recv()
            forward.start()
        for send, _, forward, handed in flows:
            handed.wait_recv()
            send.wait_send()
            forward.wait_send()

    n_flows = 3 * n
    return pl.pallas_call(
        body, in_specs=[ANY] * n, out_specs=[ANY] * n, out_shape=outs,
        scratch_shapes=[pltpu.SemaphoreType.DMA((n_flows,))] * 4, name=name)(*arrs)


def _gather_all(arr, *, name):
    slot = lambda pos: (4 * pos[0] + 2 * pos[1] + pos[2],)
    whole = lambda *_: ()
    remote = [(0, whole, 0, slot, m) for m in ALL_MASKS]
    return _exchange([arr], [SDS((8,) + arr.shape, arr.dtype)], remote, [(0, whole, 0, slot)], name=name)[0]


def _send_to_sibling(arrs, *, name):
    outs = [SDS(a.shape, a.dtype) for a in arrs]
    whole = lambda *_: ()
    remote = [(i, whole, i, whole, SIBLING) for i in range(len(arrs))]
    return _exchange(arrs, outs, remote, [], name=name)


def _add_half(g, other, core, *, name):
    n4, nl, r, c = g.shape
    half = r // 2
    g3 = g.reshape(n4 * nl, 2, half, c)
    o3 = other.reshape(n4 * nl, half, c)
    tr = _rows_tile(half) if half > 512 else half

    def body(core_ref, g_ref, o_ref, out_ref):
        out_ref[...] = (g_ref[...] + o_ref[...]).astype(BF16)

    return pl.pallas_call(
        body,
        grid_spec=pltpu.PrefetchScalarGridSpec(
            num_scalar_prefetch=1, grid=(n4 * nl, half // tr),
            in_specs=[BS((None, None, tr, c), lambda i, j, core_ref: (i, core_ref[0], j, 0)),
                      BS((None, tr, c), lambda i, j, core_ref: (i, j, 0))],
            out_specs=BS((None, tr, c), lambda i, j, core_ref: (i, j, 0))),
        out_shape=SDS((n4 * nl, half, c), BF16), compiler_params=_cp(), name=name)(core, g3, o3).reshape(n4, nl, half, c)


def _sum_chips(parts, mine, chip, *, name):
    n4, nl, h, c = mine.shape
    tr = _rows_tile(h) if h > 512 else h

    def body(chip_ref, p_ref, own_ref, out_ref):
        me = chip_ref[0]
        own = own_ref[...].astype(F32)
        across = [p_ref[j].astype(F32) for j in range(len(XY_MASKS))]
        t = []
        for s in range(n4):
            rel = s ^ me
            t.append(jnp.where(rel == 0, own, jnp.where(rel == 2, across[0], jnp.where(rel == 1, across[1], across[2]))))
        out_ref[...] = ((t[0] + t[1]) + t[2]) + t[3]

    return pl.pallas_call(
        body,
        grid_spec=pltpu.PrefetchScalarGridSpec(
            num_scalar_prefetch=1, grid=(nl, h // tr),
            in_specs=[BS((len(XY_MASKS), None, tr, c), lambda i, j, chip_ref: (0, i, j, 0)),
                      BS((None, None, tr, c), lambda i, j, chip_ref: (chip_ref[0], i, j, 0))],
            out_specs=BS((None, tr, c), lambda i, j, chip_ref: (i, j, 0))),
        out_shape=SDS((nl, h, c), F32), compiler_params=_cp(), name=name)(chip, parts, mine)


def _sum_devices(parts, *, name):
    n, r, c = parts.shape

    def body(p_ref, out_ref):
        acc = p_ref[0]
        for d in range(1, n):
            acc = acc + p_ref[d]
        out_ref[...] = acc

    return pl.pallas_call(
        body, grid=(1,), in_specs=[BS((n, r, c), lambda i: (0, 0, 0))], out_specs=BS((r, c), lambda i: (0, 0)),
        out_shape=SDS((r, c), F32), compiler_params=_cp(), name=name)(parts)


WEIGHTS = ("norm_mix_pre", "norm_mix_post", "w_in", "conv_short", "conv_gdn", "gdn_a_log", "gdn_dt_bias",
           "gdn_norm", "w_out", "norm_mem", "norm_xattn_pre", "norm_xattn_post", "w_xq", "w_xkv", "w_xo",
           "norm_ffn_pre", "norm_ffn_post", "w_gate_up", "w_down")
COL_SHARDED = ("w_in", "w_xkv", "w_gate_up", "conv_short", "conv_gdn")
ROW_SHARDED = ("w_out", "w_xq", "w_xo", "w_down")
SMALL_SHARDED = ("conv_short", "conv_gdn")
SMALL_ROW_PAD = 8


KEPT_AS_SHARDS = ("w_xkv", "w_gate_up")


def _from_shards(n, g):
    if n in KEPT_AS_SHARDS:
        return g
    if n == "w_in":
        return _w_in_pack(g, name="w_in_pack")
    if n in COL_SHARDED:
        t = jnp.moveaxis(g, 0, -2)
        return t.reshape(t.shape[:-2] + (-1,))
    return g.reshape(-1, g.shape[-1])


def _to_shards(n, g):
    if n in KEPT_AS_SHARDS:
        return g
    if n == "w_in":
        return _w_in_unpack(g, name="w_in_unpack")
    return g.reshape(4, -1, g.shape[-1])


def _pack_small(grads):
    rows = []
    for g in grads:
        for n in WEIGHTS:
            if n not in MATRICES:
                part = g[n].reshape(-1, LANE)
                rows.append(jnp.pad(part, ((0, -part.shape[0] % SMALL_ROW_PAD), (0, 0))))
    return jnp.concatenate(rows, axis=0)


def _unpack_small(packed, like):
    out, at = [], 0
    for _ in range(DEPTH):
        g = {}
        for n in WEIGHTS:
            if n not in MATRICES:
                shape = like[n].shape
                k = math.prod(shape) // LANE
                g[n] = packed[at:at + k].reshape(shape)
                at += k + (-k % SMALL_ROW_PAD)
        out.append(g)
    return out


def kernel(x, mem, positions, norm_mix_pre, norm_mix_post, w_in, conv_short, conv_gdn, gdn_a_log, gdn_dt_bias, gdn_norm, w_out, norm_mem, norm_xattn_pre, norm_xattn_post, w_xq, w_xkv, w_xo, norm_ffn_pre, norm_ffn_post, w_gate_up, w_down, loss_target, m_norm_mix_pre, m_norm_mix_post, m_w_in, m_conv_short, m_conv_gdn, m_gdn_a_log, m_gdn_dt_bias, m_gdn_norm, m_w_out, m_norm_mem, m_norm_xattn_pre, m_norm_xattn_post, m_w_xq, m_w_xkv, m_w_xo, m_norm_ffn_pre, m_norm_ffn_post, m_w_gate_up, m_w_down, v_norm_mix_pre, v_norm_mix_post, v_w_in, v_conv_short, v_conv_gdn, v_gdn_a_log, v_gdn_dt_bias, v_gdn_norm, v_w_out, v_norm_mem, v_norm_xattn_pre, v_norm_xattn_post, v_w_xq, v_w_xkv, v_w_xo, v_norm_ffn_pre, v_norm_ffn_post, v_w_gate_up, v_w_down):
    args = dict(locals())
    w = {n: args[n] for n in WEIGHTS}
    m = {n: args["m_" + n] for n in WEIGHTS}
    v = {n: args["v_" + n] for n in WEIGHTS}
    seq = x.shape[1]
    chip = 2 * lax.axis_index("x") + lax.axis_index("y")
    core = lax.axis_index("c").astype(jnp.int32).reshape(1)

    def cast(n, l):
        return w[n][l].astype(BF16)

    first = [cast(n, 0) for n in MATRICES] + [w[n] for n in SMALL_SHARDED]
    blocks = _gather_xy(first, name="gather_weights")
    blocks = [lax.dynamic_update_index_in_dim(b, o, chip, axis=0) for b, o in zip(blocks, first)]
    full = {n: _from_shards(n, b) for n, b in zip(list(MATRICES) + list(SMALL_SHARDED), blocks)}
    layer0 = {n: full.pop(n) for n in MATRICES}
    for n in WEIGHTS:
        if n not in full and n not in MATRICES:
            full[n] = w[n]
    whole = lambda *_: ()
    gather_plan = [(i, whole, i, (lambda sender: (_chip(sender),)), mask)
                   for i in range(len(MATRICES)) for mask in XY_MASKS]
    in_flight = {}
    started = jnp.zeros((1, 1), F32)
    for l in range(1, DEPTH):
        own = [cast(n, l) for n in MATRICES]
        out = _exchange_start(own, [SDS((4,) + o.shape, o.dtype) for o in own], gather_plan, name=f"gather_start_{l}")
        in_flight[l] = (own,) + out[:4]
        started = started + out[4][:1, :1]
    full["norm_mix_pre"] = full["norm_mix_pre"] + started

    def matrices_for(l, h):
        if l == 0:
            return layer0
        own, send_sems, recv_sems, thru, lands = in_flight.pop(l)
        _, lands = _exchange_wait(send_sems, recv_sems, thru, lands, gather_plan, h, name=f"gather_wait_{l}")
        lands = [lax.dynamic_update_index_in_dim(b, o, chip, axis=0) for b, o in zip(lands, own)]
        return {n: _from_shards(n, b) for n, b in zip(MATRICES, lands)}

    chip1 = chip.astype(jnp.int32).reshape(1)
    early = ("w_xq", "w_xkv", "w_xo", "w_gate_up", "w_down")
    reduced = {l: {} for l in range(DEPTH)}
    swapping, sending = [], []

    def scatter_plan(count):
        return [(i, (lambda sender, receiver: (_chip(receiver),)), i, (lambda sender, j=j: (j,)), mask)
                for i in range(count) for j, mask in enumerate(XY_MASKS)]

    def swap_plan(mine):
        def half_rows(shape):
            half = shape[2] // 2
            return lambda sender, receiver: (slice(None), slice(None), pl.ds(receiver[2] * half, half))

        plan = [(i, half_rows(a.shape), i, whole, SIBLING) for i, a in enumerate(mine)]
        return plan, [SDS(a.shape[:2] + (a.shape[2] // 2, a.shape[3]), a.dtype) for a in mine]

    def start_scatter(tag, names, mine, theirs):
        pair = [_add_half(a, b, core, name="grads_pair_sum") for a, b in zip(mine, theirs)]
        lands = [SDS((len(XY_MASKS),) + p.shape[1:], p.dtype) for p in pair]
        send_sems, recv_sems, pair, lands, token = _exchange_start(pair, lands, scatter_plan(len(names)),
                                                                   name=f"scatter_start_{tag}")
        sending.append((tag, names, send_sems, recv_sems, pair, lands))
        return token

    def finish_scatter(after):
        tag, names, send_sems, recv_sems, pair, lands = sending.pop(0)
        pair, parts = _exchange_wait(send_sems, recv_sems, pair, lands, scatter_plan(len(names)), after,
                                     name=f"scatter_wait_{tag}")
        for n, p, pr in zip(names, parts, pair):
            reduced[int(tag[0])][n] = _sum_chips(p, pr, chip1, name="grads_chip_sum")

    def on_grads(l, g, dh):
        while sending:
            finish_scatter(dh)
        names = [n for n in MATRICES if not (l == 0 and n in early)]
        mine = [_to_shards(n, g[n])[:, None] for n in names]
        plan, lands = swap_plan(mine)
        send_sems, recv_sems, mine, lands, token = _exchange_start(mine, lands, plan, name=f"swap_start_{l}")
        swapping.append((str(l), names, send_sems, recv_sems, mine, lands, plan))
        return token

    def mid_backward(after):
        if not swapping:
            return None
        tag, names, send_sems, recv_sems, mine, lands, plan = swapping.pop()
        mine, theirs = _exchange_wait(send_sems, recv_sems, mine, lands, plan, after, name=f"swap_wait_{tag}")
        return start_scatter(tag, names, mine, theirs)

    def after_xattn(l, g, dh1):
        if l != 0:
            return None
        mine = [_to_shards(n, g[n])[:, None] for n in early]
        plan, lands = swap_plan(mine)
        theirs = _exchange(mine, lands, plan, [], name="grads_swap_early")
        return start_scatter("0a", early, mine, theirs)

    loss_row, dx, grads = _local_step(x[0], mem[0], positions.reshape(seq, 1), loss_target[0], full,
                                      matrices_for=matrices_for, on_grads=on_grads, mid_backward=mid_backward,
                                      after_xattn=after_xattn)
    mid_backward(dx)
    while sending:
        finish_scatter(dx)
    keys = [(n, l) for l in range(DEPTH) for n in MATRICES]
    halves = [reduced[l][n] for l in range(DEPTH) for n in MATRICES]
    others = _send_to_sibling(halves, name="grads_share_halves")
    south = lax.axis_index("c") == 0
    grad = {}
    for n in MATRICES:
        layers = []
        for (n2, _), a, b in zip(keys, halves, others):
            if n2 == n:
                layers += [jnp.where(south, a, b), jnp.where(south, b, a)]
        grad[n] = jnp.concatenate(layers, axis=0).reshape(w[n].shape)

    packed = _pack_small(grads)
    total = _sum_devices(_gather_all(packed, name="small_grads_gather"), name="small_grads_sum")
    small = _unpack_small(total, grads[0])
    for n in WEIGHTS:
        if n in MATRICES:
            continue
        g = jnp.stack([s[n] for s in small])
        if n in ("gdn_a_log", "gdn_dt_bias"):
            g = g[:, 0, :GDN_HEADS]
        elif n in SMALL_SHARDED:
            width = w[n].shape[-1]
            g = lax.dynamic_slice_in_dim(g, chip * width, width, axis=2)
        grad[n] = g.reshape(w[n].shape)

    delta, new_m, new_v = {}, {}, {}
    for n in WEIGHTS:
        shape = w[n].shape
        two_d = (-1, shape[-1])
        d, nm, nv = _adamw(w[n].reshape(two_d), grad[n].reshape(two_d), m[n].reshape(two_d), v[n].reshape(two_d),
                           name="adamw_" + n)
        delta[n], new_m[n], new_v[n] = d.reshape(shape), nm.reshape(shape), nv.reshape(shape)

    loss = lax.psum(loss_row[0, 0], ("x", "y", "c"))
    return (loss, dx.reshape(x.shape), *[grad[n] for n in WEIGHTS], *[delta[n] for n in WEIGHTS],
            *[new_m[n] for n in WEIGHTS], *[new_v[n] for n in WEIGHTS])
```

```python
import math

import jax
import jax.numpy as jnp
from jax import lax
from jax.experimental import pallas as pl
from jax.experimental.pallas import tpu as pltpu

F32 = jnp.float32
BF16 = jnp.bfloat16
BS = pl.BlockSpec
SDS = jax.ShapeDtypeStruct
PRECISE = lax.Precision.HIGH

D_MODEL = 1024
DEPTH = 4
EPS = 1e-6
ATTN_HEADS = 4
ATTN_HEAD_DIM = 64
ATTN_WIDTH = 256
ROPE_THETA = 500000.0
ROPE_DIM = 16
CONV_WIDTH = 256
CONV_K = 3
GDN_HEADS = 4
GDN_HEAD_DIM = 128
GDN_WIDTH = 512
GDN_CONV_K = 4
GDN_CHUNK = 64
IN_WIDTH = 3592
XATTN_HEADS = 4
XATTN_HEAD_DIM = 256
FFN_HIDDEN = 2816
ADAM_LR = 0.001
ADAM_B1 = 0.9
ADAM_B2 = 0.999
ADAM_EPS = 1e-08
ADAM_WD = 0.01
ADAM_STEP = 10

IN_PAD = 3840
COL_GDN = 1536
COL_GATE = 3072
COL_AB = 3584

VMEM_LIMIT_V7X = 56 * 1024 * 1024
LANE = 128


def _cp(**kw):
    return pltpu.CompilerParams(vmem_limit_bytes=VMEM_LIMIT_V7X, **kw)


def _tile(n, cap):
    if n <= cap:
        return n
    best = None
    for t in range(LANE, cap + 1, LANE):
        if n % t == 0:
            best = t
    assert best is not None, (n, cap)
    return best


def _dot(a, b, ca, cb, precise=False):
    dims = (((ca,), (cb,)), ((), ()))
    if precise:
        return lax.dot_general(a.astype(F32), b.astype(F32), dims, precision=PRECISE,
                               preferred_element_type=F32)
    return lax.dot_general(a.astype(BF16), b.astype(BF16), dims, preferred_element_type=F32)


def _sigmoid(x):
    return 1.0 / (1.0 + jnp.exp(-x))


MM_ROWS = 1408
MM_BLOCK_BYTES = 6 * 1024 * 1024
MM_A_BYTES = 8 * 1024 * 1024


def _mm_tn(width, k, itemsize):
    if k * width * itemsize <= MM_BLOCK_BYTES:
        return width
    return _tile(width, max(LANE, min(1024, MM_BLOCK_BYTES // (k * itemsize) // LANE * LANE)))


def _mm(a, b, *, ta=False, tb=False, out_dtype=F32, b_shards=False, out_shards=False, halves=False, name):
    if halves and tb:
        m, k = a.shape[1], 2 * a.shape[2]
    else:
        m, k = (a.shape[1], a.shape[0]) if ta else a.shape
    tm = _tile(m, MM_ROWS)
    ca = 0 if ta else 1

    if b_shards and tb:
        ns, n, c = b.shape
        assert k == ns * c and not ta
        tn = _tile(n, max(LANE, min(1024, MM_BLOCK_BYTES // (k * b.dtype.itemsize) // LANE * LANE)))

        def a_block(a_ref, s):
            if halves:
                per = ns // 2
                return a_ref[s // per, :, (s % per) * c:(s % per + 1) * c]
            return a_ref[:, s * c:(s + 1) * c]

        def body(a_ref, b_ref, o_ref):
            acc = _dot(a_block(a_ref, 0), b_ref[0], 1, 1)
            for s in range(1, ns):
                acc = acc + _dot(a_block(a_ref, s), b_ref[s], 1, 1)
            o_ref[...] = acc.astype(out_dtype)

        b_spec = BS((ns, tn, c), lambda i, j: (0, j, 0))
    else:
        if b_shards:
            ns, kb, c = b.shape
            n = ns * c
            tn = _mm_tn(c, k, b.dtype.itemsize)
            nb = c // tn
            b_spec = BS((None, k, tn), lambda i, j: (j // nb, 0, j % nb))
        elif halves:
            kb, n = b.shape[1], 2 * b.shape[2]
            c = n // 4
            tn = _mm_tn(c, k, b.dtype.itemsize)
            nb = c // tn
            b_spec = BS((None, k, tn), lambda i, j: (j // (2 * nb), 0, j % (2 * nb)))
        else:
            kb, n = (b.shape[1], b.shape[0]) if tb else b.shape
            c = n // 4 if out_shards else n
            tn = _mm_tn(c, k, b.dtype.itemsize)
            nb = c // tn
            b_spec = BS((tn, k), lambda i, j: (j, 0)) if tb else BS((k, tn), lambda i, j: (0, j))
        assert kb == k
        cb = 1 if tb else 0

        def body(a_ref, b_ref, o_ref):
            o_ref[...] = _dot(a_ref[...], b_ref[...], ca, cb).astype(out_dtype)

    out_bytes = jnp.dtype(out_dtype).itemsize
    while (tm > 256 and tm % 256 == 0 and
           (tm * k * a.dtype.itemsize > MM_A_BYTES or tm * tn * out_bytes > MM_BLOCK_BYTES)):
        tm //= 2
    if halves and tb:
        a_spec = BS((2, tm, k // 2), lambda i, j: (0, i, 0))
    else:
        a_spec = BS((k, tm), lambda i, j: (0, i)) if ta else BS((tm, k), lambda i, j: (i, 0))
    if out_shards:
        out_spec = BS((None, tm, tn), lambda i, j: (j // nb, i, j % nb))
        out_shape = SDS((4, m, n // 4), out_dtype)
    elif halves and not (ta or tb):
        out_spec = BS((None, tm, tn), lambda i, j: (j // (2 * nb), i, j % (2 * nb)))
        out_shape = SDS((2, m, n // 2), out_dtype)
    else:
        out_spec = BS((tm, tn), lambda i, j: (i, j))
        out_shape = SDS((m, n), out_dtype)
    return pl.pallas_call(
        body, grid=(m // tm, n // tn), in_specs=[a_spec, b_spec], out_specs=out_spec, out_shape=out_shape,
        compiler_params=_cp(), name=name)(a, b)


def _rmsnorm(x, w, *, name):
    r, d = x.shape
    tr = _tile(r, 512)

    def body(x_ref, w_ref, o_ref):
        xv = x_ref[...]
        rs = lax.rsqrt(jnp.mean(xv * xv, axis=-1, keepdims=True) + EPS)
        o_ref[...] = (xv * rs * w_ref[...]).astype(BF16)

    return pl.pallas_call(
        body, grid=(r // tr,), in_specs=[BS((tr, d), lambda i: (i, 0)), BS((1, d), lambda i: (0, 0))],
        out_specs=BS((tr, d), lambda i: (i, 0)), out_shape=SDS((r, d), BF16),
        compiler_params=_cp(), name=name)(x, w)


def _resnorm(h, m, w, *, name):
    r, d = h.shape
    tr = _tile(r, 512)

    def body(h_ref, m_ref, w_ref, o_ref):
        mv = m_ref[...]
        rs = lax.rsqrt(jnp.mean(mv * mv, axis=-1, keepdims=True) + EPS)
        o_ref[...] = h_ref[...] + mv * rs * w_ref[...]

    row = BS((tr, d), lambda i: (i, 0))
    return pl.pallas_call(
        body, grid=(r // tr,), in_specs=[row, row, BS((1, d), lambda i: (0, 0))],
        out_specs=row, out_shape=SDS((r, d), F32), compiler_params=_cp(), name=name)(h, m, w)


def _rmsnorm_bwd(x, w, dy, res=None, *, name):
    r, d = x.shape
    tr = _tile(r, 512)
    has_res = res is not None

    def body(*refs):
        if has_res:
            x_ref, w_ref, dy_ref, res_ref, dx_ref, dw_ref = refs
        else:
            x_ref, w_ref, dy_ref, dx_ref, dw_ref = refs
        xv = x_ref[...]
        dyv = dy_ref[...].astype(F32)
        rs = lax.rsqrt(jnp.mean(xv * xv, axis=-1, keepdims=True) + EPS)
        nv = xv * rs
        dyw = dyv * w_ref[...]
        dx = rs * (dyw - nv * jnp.mean(dyw * nv, axis=-1, keepdims=True))
        if has_res:
            dx = dx + res_ref[...]
        dx_ref[...] = dx

        @pl.when(pl.program_id(0) == 0)
        def _():
            dw_ref[...] = jnp.zeros_like(dw_ref)

        dw_ref[...] += jnp.sum(dyv * nv, axis=0, keepdims=True)

    row = BS((tr, d), lambda i: (i, 0))
    vec = BS((1, d), lambda i: (0, 0))
    ins = [x, w, dy] + ([res] if has_res else [])
    return pl.pallas_call(
        body, grid=(r // tr,), in_specs=[row, vec, row] + ([row] if has_res else []),
        out_specs=[row, vec], out_shape=[SDS((r, d), F32), SDS((1, d), F32)],
        compiler_params=_cp(), name=name)(*ins)


def _swiglu(gu, *, name):
    _, r, hid = gu.shape
    tr, tc = _tile(r, 512), _tile(hid, 1408)

    def body(gu_ref, o_ref):
        g = gu_ref[0]
        o_ref[...] = (g * _sigmoid(g) * gu_ref[1]).astype(BF16)

    return pl.pallas_call(
        body, grid=(r // tr, hid // tc), in_specs=[BS((2, tr, tc), lambda i, j: (0, i, j))],
        out_specs=BS((tr, tc), lambda i, j: (i, j)), out_shape=SDS((r, hid), BF16),
        compiler_params=_cp(), name=name)(gu)


def _swiglu_bwd(gu, dact, *, name):
    _, r, hid = gu.shape
    tr, tc = _tile(r, 512), _tile(hid, 1408)

    def body(gu_ref, d_ref, o_ref):
        g = gu_ref[0]
        da = d_ref[...]
        sg = _sigmoid(g)
        o_ref[0] = (da * gu_ref[1] * sg * (1.0 + g * (1.0 - sg))).astype(BF16)
        o_ref[1] = (da * g * sg).astype(BF16)

    blk = BS((2, tr, tc), lambda i, j: (0, i, j))
    return pl.pallas_call(
        body, grid=(r // tr, hid // tc), in_specs=[blk, BS((tr, tc), lambda i, j: (i, j))],
        out_specs=blk, out_shape=SDS((2, r, hid), BF16), compiler_params=_cp(), name=name)(gu, dact)


def _loss_grad(h, target, *, name):
    r, d = h.shape
    tr = _tile(r, 512)

    def body(h_ref, t_ref, l_ref, g_ref):
        e = h_ref[...] - t_ref[...]
        g_ref[...] = e * (1.0 / d)

        @pl.when(pl.program_id(0) == 0)
        def _():
            l_ref[...] = jnp.zeros_like(l_ref)

        l_ref[...] += jnp.full((1, LANE), 0.5 / d, F32) * jnp.sum(e * e)

    row = BS((tr, d), lambda i: (i, 0))
    return pl.pallas_call(
        body, grid=(r // tr,), in_specs=[row, row],
        out_specs=[BS((1, LANE), lambda i: (0, 0)), row],
        out_shape=[SDS((1, LANE), F32), SDS((r, d), F32)], compiler_params=_cp(), name=name)(h, target)


def _adamw(w, g, m, v, *, name):
    r, c = w.shape
    tr = r if r <= 512 else _rows_tile(r)
    bc1 = 1.0 - ADAM_B1 ** ADAM_STEP
    bc2 = 1.0 - ADAM_B2 ** ADAM_STEP

    def body(w_ref, g_ref, m_ref, v_ref, d_ref, nm_ref, nv_ref):
        gv = g_ref[...]
        nm = ADAM_B1 * m_ref[...] + (1.0 - ADAM_B1) * gv
        nv = ADAM_B2 * v_ref[...] + (1.0 - ADAM_B2) * (gv * gv)
        d_ref[...] = -ADAM_LR * ((nm / bc1) / (jnp.sqrt(nv / bc2) + ADAM_EPS) + ADAM_WD * w_ref[...])
        nm_ref[...] = nm
        nv_ref[...] = nv

    blk = BS((tr, c), lambda i: (i, 0))
    return pl.pallas_call(
        body, grid=(r // tr,), in_specs=[blk] * 4, out_specs=[blk] * 3,
        out_shape=[SDS((r, c), F32)] * 3, compiler_params=_cp(), name=name)(w, g, m, v)


def _rows_tile(r):
    for t in (512, 256, 128, 64, 32, 16, 8):
        if r % t == 0:
            return t
    return r


def _rope_tables(pos, *, name):
    s = pos.shape[0]
    half = ROPE_DIM // 2

    def body(p_ref, c_ref, a_ref, b_ref):
        lane = lax.broadcasted_iota(jnp.int32, (s, ATTN_WIDTH), 1) & (ATTN_HEAD_DIM - 1)
        fi = (lane & (half - 1)).astype(F32)
        inv_freq = jnp.exp(fi * (-2.0 * math.log(ROPE_THETA) / ROPE_DIM))
        ang = p_ref[...].astype(F32) * inv_freq
        cs, sn = jnp.cos(ang), jnp.sin(ang)
        c_ref[...] = jnp.where(lane < ROPE_DIM, cs, 1.0)
        a_ref[...] = jnp.where(lane < half, -sn, 0.0)
        b_ref[...] = jnp.where((lane >= half) & (lane < ROPE_DIM), sn, 0.0)

    full = BS((s, ATTN_WIDTH), lambda i: (0, 0))
    return pl.pallas_call(
        body, grid=(1,), in_specs=[BS((s, 1), lambda i: (0, 0))], out_specs=[full] * 3,
        out_shape=[SDS((s, ATTN_WIDTH), F32)] * 3, compiler_params=_cp(), name=name)(pos)


def _rot(x, c, a, b):
    w = x.shape[1]
    return x * c + pltpu.roll(x, w - ROPE_DIM // 2, 1) * a + pltpu.roll(x, ROPE_DIM // 2, 1) * b


def _rot_t(dy, c, a, b):
    w = dy.shape[1]
    return dy * c + pltpu.roll(dy * a, ROPE_DIM // 2, 1) + pltpu.roll(dy * b, w - ROPE_DIM // 2, 1)


def _attn_count(q0, tq, s):
    dist = (lax.broadcasted_iota(jnp.int32, (tq, s), 0) + q0) - lax.broadcasted_iota(jnp.int32, (tq, s), 1)
    cnt = ((dist <= 128).astype(F32) + (((dist & 3) == 0) & (dist <= 512)).astype(F32)
           + ((dist & 15) == 0).astype(F32))
    return jnp.where(dist >= 0, cnt, 0.0)


ATTN_TQ = 512


def _attn_specs(s, tq):
    def qblk(col):
        return BS((tq, ATTN_WIDTH), lambda i: (i, col))

    def full(col):
        return BS((s, ATTN_WIDTH), lambda i: (0, col))

    return qblk, full


ATTN_TK = 512


def _attn_chunk(i, c, tq, k_ref, v_ref, ck, ak, bk):
    ks = pl.ds(pl.multiple_of(c * ATTN_TK, ATTN_TK), ATTN_TK)
    k = _rot(k_ref[ks, :], ck[ks, :], ak[ks, :], bk[ks, :]).astype(BF16)
    v = v_ref[ks, :].astype(BF16)
    cnt = _attn_count(i * tq - c * ATTN_TK, tq, ATTN_TK)
    return ks, k, v, cnt


def _attn_flash_fwd(proj, tabs, *, name):
    s = proj.shape[0]
    tq = ATTN_TQ
    qblk, full = _attn_specs(s, tq)
    scale = ATTN_HEAD_DIM ** -0.5
    nh = ATTN_HEADS

    def body(q_ref, k_ref, v_ref, cq, aq, bq, ck, ak, bk, o_ref, lse_ref):
        i = pl.program_id(0)
        q = _rot(q_ref[...], cq[...], aq[...], bq[...]) * scale
        head = lax.broadcasted_iota(jnp.int32, (1, ATTN_WIDTH), 1) >> 6
        hms = [(head == h).astype(F32) for h in range(nh)]
        qms = [(q * hm).astype(BF16) for hm in hms]

        def step(c, carry):
            ms, ls, acc = carry
            _, k, v, cnt = _attn_chunk(i, c, tq, k_ref, v_ref, ck, ak, bk)
            valid = cnt > 0.0
            new_ms, new_ls = [], []
            scale_acc = jnp.zeros((tq, ATTN_WIDTH), F32)
            add = jnp.zeros((tq, ATTN_WIDTH), F32)
            for h in range(nh):
                sc = _dot(qms[h], k, 1, 1)
                m_new = jnp.maximum(ms[h], jnp.max(jnp.where(valid, sc, -1e30), axis=-1, keepdims=True))
                alpha = jnp.exp(ms[h] - m_new)
                p = cnt * jnp.exp(jnp.minimum(sc - m_new, 0.0))
                new_ms.append(m_new)
                new_ls.append(alpha * ls[h] + jnp.sum(p, axis=-1, keepdims=True))
                scale_acc = scale_acc + alpha * hms[h]
                add = add + _dot(p, v, 1, 0) * hms[h]
            return new_ms, new_ls, acc * scale_acc + add

        init = ([jnp.full((tq, 1), -1e30, F32)] * nh, [jnp.zeros((tq, 1), F32)] * nh,
                jnp.zeros((tq, ATTN_WIDTH), F32))
        ms, ls, acc = lax.fori_loop(0, i // (ATTN_TK // tq) + 1, step, init)
        inv = jnp.zeros((tq, ATTN_WIDTH), F32)
        lane = lax.broadcasted_iota(jnp.int32, (tq, LANE), 1)
        lse = jnp.zeros((tq, LANE), F32)
        for h in range(nh):
            inv = inv + (1.0 / ls[h]) * hms[h]
            lse = jnp.where(lane == h, ms[h] + jnp.log(ls[h]), lse)
        o_ref[...] = (acc * inv).astype(BF16)
        lse_ref[...] = lse

    return pl.pallas_call(
        body, grid=(s // tq,),
        in_specs=[qblk(0), full(1), full(2), qblk(0), qblk(0), qblk(0), full(0), full(0), full(0)],
        out_specs=[BS((tq, ATTN_WIDTH), lambda i: (i, 0)), BS((tq, LANE), lambda i: (i, 0))],
        out_shape=[SDS((s, ATTN_WIDTH), BF16), SDS((s, LANE), F32)],
        compiler_params=_cp(), name=name)(proj, proj, proj, *tabs, *tabs)


def _attn_flash_bwd(proj, tabs, cat, lse, dcat, *, name):
    s = proj.shape[0]
    tq = ATTN_TQ
    nq = s // tq
    qblk, full = _attn_specs(s, tq)
    scale = ATTN_HEAD_DIM ** -0.5
    nh = ATTN_HEADS

    def body(q_ref, k_ref, v_ref, cq, aq, bq, ck, ak, bk, y_ref, lse_ref, dy_ref,
             dq_ref, dk_ref, dv_ref, dk_acc, dv_acc):
        i = pl.program_id(0)

        @pl.when(i == 0)
        def _():
            dk_acc[...] = jnp.zeros_like(dk_acc)
            dv_acc[...] = jnp.zeros_like(dv_acc)

        q = _rot(q_ref[...], cq[...], aq[...], bq[...]) * scale
        dy = dy_ref[...].astype(F32)
        prod = dy * y_ref[...].astype(F32)
        lse_all = lse_ref[...]
        head = lax.broadcasted_iota(jnp.int32, (1, ATTN_WIDTH), 1) >> 6
        hms = [(head == h).astype(F32) for h in range(nh)]
        qms = [(q * hm).astype(BF16) for hm in hms]
        dyms = [(dy * hm).astype(BF16) for hm in hms]
        deltas = [jnp.sum(prod * hm, axis=-1, keepdims=True) for hm in hms]
        lses = [lse_all[:, h:h + 1] for h in range(nh)]

        def step(c, dq):
            ks, k, v, cnt = _attn_chunk(i, c, tq, k_ref, v_ref, ck, ak, bk)
            dk_c = jnp.zeros((ATTN_TK, ATTN_WIDTH), F32)
            dv_c = jnp.zeros((ATTN_TK, ATTN_WIDTH), F32)
            for h in range(nh):
                sc = _dot(qms[h], k, 1, 1)
                p = cnt * jnp.exp(jnp.minimum(sc - lses[h], 0.0))
                dp = _dot(dyms[h], v, 1, 1)
                ds = p * (dp - deltas[h])
                dq = dq + _dot(ds, k, 1, 0) * hms[h]
                dk_c = dk_c + _dot(ds, qms[h], 0, 0)
                dv_c = dv_c + _dot(p, dyms[h], 0, 0)
            dk_acc[ks, :] += dk_c
            dv_acc[ks, :] += dv_c
            return dq

        dq = lax.fori_loop(0, i // (ATTN_TK // tq) + 1, step, jnp.zeros((tq, ATTN_WIDTH), F32))
        dq_ref[...] = _rot_t(dq * scale, cq[...], aq[...], bq[...]).astype(BF16)

        @pl.when(i == nq - 1)
        def _():
            dk_ref[...] = _rot_t(dk_acc[...], ck[...], ak[...], bk[...]).astype(BF16)
            dv_ref[...] = dv_acc[...].astype(BF16)

    whole = BS((s, ATTN_WIDTH), lambda i: (0, 0))
    return pl.pallas_call(
        body, grid=(nq,),
        in_specs=[qblk(0), full(1), full(2), qblk(0), qblk(0), qblk(0), full(0), full(0), full(0),
                  qblk(0), BS((tq, LANE), lambda i: (i, 0)), qblk(0)],
        out_specs=[BS((tq, ATTN_WIDTH), lambda i: (i, 0)), whole, whole],
        out_shape=[SDS((s, ATTN_WIDTH), BF16)] * 3,
        scratch_shapes=[pltpu.VMEM((s, ATTN_WIDTH), F32), pltpu.VMEM((s, ATTN_WIDTH), F32)],
        compiler_params=_cp(), name=name)(proj, proj, proj, *tabs, *tabs, cat, lse, dcat)


def _shift_down(x, n):
    if n == 0:
        return x
    rows = lax.broadcasted_iota(jnp.int32, x.shape, 0)
    return jnp.where(rows >= n, pltpu.roll(x, n, 0), 0.0)


def _shift_up(x, n):
    if n == 0:
        return x
    t = x.shape[0]
    rows = lax.broadcasted_iota(jnp.int32, x.shape, 0)
    return jnp.where(rows < t - n, pltpu.roll(x, t - n, 0), 0.0)


def _conv_fwd(z, w, kk):
    y = z * w[kk - 1:kk, :]
    for j in range(kk - 1):
        y = y + _shift_down(z, kk - 1 - j) * w[j:j + 1, :]
    return y


def _conv_bwd(z, dy, w, kk):
    dz = dy * w[kk - 1:kk, :]
    dws = []
    for j in range(kk - 1):
        dz = dz + _shift_up(dy, kk - 1 - j) * w[j:j + 1, :]
        dws.append(jnp.sum(dy * _shift_down(z, kk - 1 - j), axis=0, keepdims=True))
    dws.append(jnp.sum(dy * z, axis=0, keepdims=True))
    return dz, jnp.concatenate(dws, axis=0)


def _sconv_fwd(proj, w, *, name):
    s = proj.shape[0]

    def body(b_ref, c_ref, x_ref, w_ref, o_ref):
        y = _conv_fwd(c_ref[...] * x_ref[...], w_ref[...], CONV_K)
        o_ref[...] = (b_ref[...] * y).astype(BF16)

    def col(j):
        return BS((s, LANE), lambda i: (0, j + i))

    return pl.pallas_call(
        body, grid=(CONV_WIDTH // LANE,), in_specs=[col(6), col(8), col(10), BS((CONV_K, LANE), lambda i: (0, i))],
        out_specs=BS((s, LANE), lambda i: (0, i)), out_shape=SDS((s, CONV_WIDTH), BF16),
        compiler_params=_cp(), name=name)(proj, proj, proj, w)


def _sconv_bwd(proj, w, dcat, *, name):
    s = proj.shape[0]

    def body(b_ref, c_ref, x_ref, w_ref, dy_ref, db_ref, dc_ref, dx_ref, dw_ref):
        cv, xv, wv = c_ref[...], x_ref[...], w_ref[...]
        dy = dy_ref[...].astype(F32)
        z = cv * xv
        db_ref[...] = (dy * _conv_fwd(z, wv, CONV_K)).astype(BF16)
        dz, dw = _conv_bwd(z, dy * b_ref[...], wv, CONV_K)
        dc_ref[...] = (dz * xv).astype(BF16)
        dx_ref[...] = (dz * cv).astype(BF16)
        dw_ref[...] = dw

    def col(j):
        return BS((s, LANE), lambda i: (0, j + i))

    out = BS((s, LANE), lambda i: (0, i))
    wspec = BS((CONV_K, LANE), lambda i: (0, i))
    return pl.pallas_call(
        body, grid=(CONV_WIDTH // LANE,), in_specs=[col(6), col(8), col(10), wspec, col(2)],
        out_specs=[out, out, out, wspec],
        out_shape=[SDS((s, CONV_WIDTH), BF16)] * 3 + [SDS((CONV_K, CONV_WIDTH), F32)],
        compiler_params=_cp(), name=name)(proj, proj, proj, w, dcat)


def _l2n(y, scale):
    r = lax.rsqrt(jnp.sum(y * y, axis=-1, keepdims=True) + EPS)
    return y * r * scale, r


def _gdn_pre_fwd(proj, w, *, name):
    s = proj.shape[0]
    nh = GDN_HEADS

    def body(x_ref, w_ref, o_ref):
        j = pl.program_id(0)
        c = _conv_fwd(x_ref[...], w_ref[...], GDN_CONV_K)
        y = c * _sigmoid(c)
        scale = jnp.where(j < nh, GDN_HEAD_DIM ** -0.5, 1.0)
        n, _ = _l2n(y, scale)
        o_ref[...] = jnp.where(j < 2 * nh, n, y)

    return pl.pallas_call(
        body, grid=(3 * nh,),
        in_specs=[BS((s, LANE), lambda j: (0, COL_GDN // LANE + j)), BS((GDN_CONV_K, LANE), lambda j: (0, j))],
        out_specs=BS((s, LANE), lambda j: (0, j)), out_shape=SDS((s, 3 * GDN_WIDTH), F32),
        compiler_params=_cp(), name=name)(proj, w)


def _gdn_pre_bwd(proj, w, dqkv, *, name):
    s = proj.shape[0]
    nh = GDN_HEADS

    def body(x_ref, w_ref, d_ref, dx_ref, dw_ref):
        j = pl.program_id(0)
        xv, wv, dn = x_ref[...], w_ref[...], d_ref[...]
        c = _conv_fwd(xv, wv, GDN_CONV_K)
        sg = _sigmoid(c)
        y = c * sg
        scale = jnp.where(j < nh, GDN_HEAD_DIM ** -0.5, 1.0)
        n, r = _l2n(y, 1.0)
        dns = dn * scale
        dy_norm = r * (dns - n * jnp.sum(dns * n, axis=-1, keepdims=True))
        dy = jnp.where(j < 2 * nh, dy_norm, dn)
        dc = dy * sg * (1.0 + c * (1.0 - sg))
        dx, dw = _conv_bwd(xv, dc, wv, GDN_CONV_K)
        dx_ref[...] = dx.astype(BF16)
        dw_ref[...] = dw

    wspec = BS((GDN_CONV_K, LANE), lambda j: (0, j))
    blk = BS((s, LANE), lambda j: (0, j))
    return pl.pallas_call(
        body, grid=(3 * nh,),
        in_specs=[BS((s, LANE), lambda j: (0, COL_GDN // LANE + j)), wspec, blk],
        out_specs=[blk, wspec], out_shape=[SDS((s, 3 * GDN_WIDTH), BF16), SDS((GDN_CONV_K, 3 * GDN_WIDTH), F32)],
        compiler_params=_cp(), name=name)(proj, w, dqkv)


def _softplus(x):
    return jnp.maximum(x, 0.0) + jnp.log(1.0 + jnp.exp(-jnp.abs(x)))


def _gdn_gates_fwd(proj, a_log, dt_bias, *, name):
    s = proj.shape[0]

    def body(x_ref, al_ref, dt_ref, o_ref):
        xv = x_ref[...]
        lane = lax.broadcasted_iota(jnp.int32, xv.shape, 1)
        g = -jnp.exp(al_ref[...]) * _softplus(xv + dt_ref[...])
        o_ref[...] = jnp.where(lane < GDN_HEADS, g, jnp.where(lane < 2 * GDN_HEADS, _sigmoid(xv), 0.0))

    vec = BS((1, LANE), lambda i: (0, 0))
    return pl.pallas_call(
        body, grid=(1,), in_specs=[BS((s, LANE), lambda i: (0, COL_AB // LANE)), vec, vec],
        out_specs=BS((s, LANE), lambda i: (0, 0)), out_shape=SDS((s, LANE), F32),
        compiler_params=_cp(), name=name)(proj, a_log, dt_bias)


def _gdn_gates_bwd(proj, a_log, dt_bias, dgb, *, name):
    s = proj.shape[0]

    def body(x_ref, al_ref, dt_ref, d_ref, dx_ref, dal_ref, ddt_ref):
        xv, dv = x_ref[...], d_ref[...]
        lane = lax.broadcasted_iota(jnp.int32, xv.shape, 1)
        is_g = lane < GDN_HEADS
        ea = -jnp.exp(al_ref[...])
        z = xv + dt_ref[...]
        da = jnp.where(is_g, dv * ea * _sigmoid(z), 0.0)
        beta = _sigmoid(xv)
        dx_ref[...] = jnp.where(is_g, da, jnp.where(lane < 2 * GDN_HEADS, dv * beta * (1.0 - beta), 0.0)).astype(BF16)
        dal_ref[...] = jnp.sum(jnp.where(is_g, dv * ea * _softplus(z), 0.0), axis=0, keepdims=True)
        ddt_ref[...] = jnp.sum(da, axis=0, keepdims=True)

    vec = BS((1, LANE), lambda i: (0, 0))
    blk = BS((s, LANE), lambda i: (0, 0))
    return pl.pallas_call(
        body, grid=(1,), in_specs=[BS((s, LANE), lambda i: (0, COL_AB // LANE)), vec, vec, blk],
        out_specs=[blk, vec, vec], out_shape=[SDS((s, LANE), BF16), SDS((1, LANE), F32), SDS((1, LANE), F32)],
        compiler_params=_cp(), name=name)(proj, a_log, dt_bias, dgb)


def _col_to_row(col, eye):
    return jnp.sum(jnp.where(eye, col, 0.0), axis=0, keepdims=True)


def _row_to_col(row, eye):
    return jnp.sum(jnp.where(eye, row, 0.0), axis=1, keepdims=True)


GDN_GROUP = 4
TRI_BLOCK_SHIFT = 4


def _gdn_masks(c):
    row = lax.broadcasted_iota(jnp.int32, (c, c), 0)
    col = lax.broadcasted_iota(jnp.int32, (c, c), 1)
    return dict(row=row, col=col, eye=row == col, low=row >= col, strict=row > col, upper=row <= col,
                on_diag=(row >> TRI_BLOCK_SHIFT) == (col >> TRI_BLOCK_SHIFT))


def _tri_inv(a_list, mk):
    eye_f = mk["eye"].astype(F32)
    ds = [jnp.where(mk["on_diag"], a, 0.0) for a in a_list]
    xs = [eye_f - d for d in ds]
    ps = ds
    for _ in range(3):
        ps = [_dot(p, p, 1, 0, precise=True) for p in ps]
        xs = [x + _dot(x, p, 1, 0, precise=True) for x, p in zip(xs, ps)]
    ms = [_dot(x, a - d, 1, 0, precise=True) for x, a, d in zip(xs, a_list, ds)]
    m2s = [_dot(m, m, 1, 0, precise=True) for m in ms]
    ys = [eye_f - m for m in ms]
    ys = [y + _dot(y, m2, 1, 0, precise=True) for y, m2 in zip(ys, m2s)]
    return [_dot(y, x, 1, 0, precise=True) for y, x in zip(ys, xs)]


def _gdn_pre(qs, ks, vs, gs, betas, mk, ts=None):
    c, hd = qs[0].shape
    eye, low = mk["eye"], mk["low"]
    g_rows = [_col_to_row(g, eye) for g in gs]
    d_cols = [jnp.sum(jnp.where(low, gr, 0.0), axis=1, keepdims=True) for gr in g_rows]
    d_rows = [jnp.sum(jnp.where(mk["upper"], g, 0.0), axis=0, keepdims=True) for g in gs]
    rels = [jnp.where(low, jnp.exp(jnp.minimum(dc - dr, 0.0)), 0.0) for dc, dr in zip(d_cols, d_rows)]
    d_lasts = [dc[c - 1:c, :] for dc in d_cols]
    es = [jnp.exp(dc) for dc in d_cols]
    fs = [jnp.exp(dl - dc) for dl, dc in zip(d_lasts, d_cols)]
    cds = [jnp.exp(dl) for dl in d_lasts]
    kbs = [k * b for k, b in zip(ks, betas)]
    kbqs = [jnp.concatenate([kb, q], axis=0) for kb, q in zip(kbs, qs)]
    kqk = [_dot(kbq, k, 1, 1) for kbq, k in zip(kbqs, ks)]
    kks = [x[:c, :] for x in kqk]
    qks = [x[c:, :] for x in kqk]
    if ts is None:
        ts = _tri_inv([jnp.where(mk["strict"], kk * rel, 0.0) for kk, rel in zip(kks, rels)], mk)
    vbs = [v * b for v, b in zip(vs, betas)]
    kbes = [kb * e for kb, e in zip(kbs, es)]
    uws = [_dot(t, jnp.concatenate([vb, kbe], axis=1), 1, 0) for t, vb, kbe in zip(ts, vbs, kbes)]
    out = []
    for i in range(len(qs)):
        out.append(dict(rel=rels[i], e=es[i], f=fs[i], cd=cds[i], kb=kbs[i], kbq=kbqs[i], kk=kks[i], qk=qks[i],
                        t=ts[i], u=uws[i][:, :hd], w=uws[i][:, hd:], uw=uws[i], attn=qks[i] * rels[i],
                        qd=qs[i] * es[i], kd=ks[i] * fs[i]))
    return out


def _gdn_apply(pres, sts, leaving=True):
    c = pres[0]["u"].shape[0]
    wqs = [_dot(jnp.concatenate([p["w"], p["qd"]], axis=0), st, 1, 0) for p, st in zip(pres, sts)]
    vns = [p["u"] - x[:c, :] for p, x in zip(pres, wqs)]
    os_ = [x[c:, :] + _dot(p["attn"], vn, 1, 0) for p, x, vn in zip(pres, wqs, vns)]
    if not leaving:
        return vns, os_, None
    new = [p["cd"] * st + _dot(p["kd"], vn, 0, 0) for p, st, vn in zip(pres, sts, vns)]
    return vns, os_, new


def _gdn_bwd_rest(qs, ks, vs, betas, sts, pres, vns, dos, dvns, dsts, mk):
    c, hd = qs[0].shape
    eye = mk["eye"]
    n = range(len(qs))
    dkds = [_dot(vns[i], dsts[i], 1, 1) for i in n]
    dcds = [jnp.sum(sts[i] * dsts[i]) for i in n]
    dattns = [jnp.where(mk["low"], _dot(dos[i], vns[i], 1, 1), 0.0) for i in n]
    dqdws = [_dot(jnp.concatenate([dos[i], -dvns[i]], axis=0), sts[i], 1, 1) for i in n]
    dqds = [x[:c, :] for x in dqdws]
    dws = [x[c:, :] for x in dqdws]
    dvks = [_dot(pres[i]["t"], jnp.concatenate([dvns[i], dws[i]], axis=1), 0, 0) for i in n]
    das = [jnp.where(mk["strict"], -_dot(dvks[i], pres[i]["uw"], 1, 1), 0.0) for i in n]
    dkqs = [jnp.concatenate([das[i] * pres[i]["rel"], dattns[i] * pres[i]["rel"]], axis=0) for i in n]
    dkbdq = [_dot(dkqs[i], ks[i], 1, 0) for i in n]
    dk0 = [_dot(dkqs[i], pres[i]["kbq"], 0, 0) for i in n]
    out = []
    rows1 = lax.broadcasted_iota(jnp.int32, (c, 1), 0)
    for i in n:
        p = pres[i]
        dvb, dkbe = dvks[i][:, :hd], dvks[i][:, hd:]
        grel = (das[i] * p["kk"] + dattns[i] * p["qk"]) * p["rel"]
        dkb = dkbdq[i][:c, :] + dkbe * p["e"]
        dk = dk0[i] + dkds[i] * p["f"] + dkb * betas[i]
        dq = dkbdq[i][c:, :] + dqds[i] * p["e"]
        dv = dvb * betas[i]
        dbeta = jnp.sum(dkb * ks[i], axis=1, keepdims=True) + jnp.sum(dvb * vs[i], axis=1, keepdims=True)
        de = jnp.sum(dqds[i] * qs[i], axis=1, keepdims=True) + jnp.sum(dkbe * p["kb"], axis=1, keepdims=True)
        dff = jnp.sum(dkds[i] * ks[i], axis=1, keepdims=True) * p["f"]
        dd = (de * p["e"] - dff + jnp.sum(grel, axis=1, keepdims=True)
              - _row_to_col(jnp.sum(grel, axis=0, keepdims=True), eye))
        dd = dd + jnp.where(rows1 == c - 1, jnp.sum(dff) + dcds[i] * p["cd"], 0.0)
        dg = jnp.sum(jnp.where(mk["upper"], _col_to_row(dd, eye), 0.0), axis=1, keepdims=True)
        out.append((dq, dk, dv, dg, dbeta))
    return out


def _gdn_specs(c):
    def qkv(j):
        return BS((c, GDN_WIDTH), lambda n: (n, j))

    return qkv


def _gdn_core_fwd(qkv, gbeta, proj, norm_w, *, name):
    s = qkv.shape[0]
    c, nh, hd, grp = GDN_CHUNK, GDN_HEADS, GDN_HEAD_DIM, GDN_GROUP
    n_chunks = s // c
    blk = _gdn_specs(grp * c)
    inst = [(sub, h) for sub in range(grp) for h in range(nh)]

    def body(q_ref, k_ref, v_ref, gb_ref, gate_ref, nw_ref, y_ref, st_ref, t_ref, state):
        @pl.when(pl.program_id(0) == 0)
        def _():
            state[...] = jnp.zeros_like(state)

        mk = _gdn_masks(c)
        rows = [slice(sub * c, (sub + 1) * c) for sub in range(grp)]
        lanes = [slice(h * hd, (h + 1) * hd) for h in range(nh)]
        gbs = [gb_ref[r, :] for r in rows]
        pres = _gdn_pre([q_ref[rows[sub], lanes[h]] for sub, h in inst], [k_ref[rows[sub], lanes[h]] for sub, h in inst],
                        [v_ref[rows[sub], lanes[h]] for sub, h in inst], [gbs[sub][:, h:h + 1] for sub, h in inst],
                        [gbs[sub][:, nh + h:nh + h + 1] for sub, h in inst], mk)
        sts = [state[ls, :] for ls in lanes]
        outs = []
        for sub in range(grp):
            for h in range(nh):
                st_ref[pl.ds((sub * nh + h) * hd, hd), :] = sts[h]
            _, os_, sts = _gdn_apply(pres[sub * nh:(sub + 1) * nh], sts)
            outs += os_
        for h in range(nh):
            state[lanes[h], :] = sts[h]
        nw = nw_ref[...]
        for i, (sub, h) in enumerate(inst):
            t_ref[pl.ds(i * c, c), :] = pres[i]["t"]
            o = outs[i]
            gate = gate_ref[rows[sub], lanes[h]]
            rs = lax.rsqrt(jnp.mean(o * o, axis=-1, keepdims=True) + EPS)
            y_ref[rows[sub], lanes[h]] = (o * rs * nw * (gate * _sigmoid(gate))).astype(BF16)

    return pl.pallas_call(
        body, grid=(n_chunks // grp,),
        in_specs=[blk(0), blk(1), blk(2), BS((grp * c, LANE), lambda n: (n, 0)),
                  BS((grp * c, GDN_WIDTH), lambda n: (n, COL_GATE // GDN_WIDTH)), BS((1, hd), lambda n: (0, 0))],
        out_specs=[BS((grp * c, GDN_WIDTH), lambda n: (n, 0)), BS((grp * nh * hd, hd), lambda n: (n, 0)),
                   BS((grp * nh * c, c), lambda n: (n, 0))],
        out_shape=[SDS((s, GDN_WIDTH), BF16), SDS((n_chunks * nh * hd, hd), F32), SDS((n_chunks * nh * c, c), F32)],
        scratch_shapes=[pltpu.VMEM((nh * hd, hd), F32)],
        compiler_params=_cp(), name=name)(qkv, qkv, qkv, gbeta, proj, norm_w)


def _gdn_core_bwd(qkv, gbeta, proj, norm_w, states, tinv, dcat, *, name):
    s = qkv.shape[0]
    c, nh, hd, grp = GDN_CHUNK, GDN_HEADS, GDN_HEAD_DIM, GDN_GROUP
    n_chunks = s // c
    last = n_chunks // grp - 1
    inst = [(sub, h) for sub in range(grp) for h in range(nh)]

    def rev(j, w):
        return BS((grp * c, w), lambda n: (last - n, j))

    def body(q_ref, k_ref, v_ref, gb_ref, gate_ref, nw_ref, st_ref, t_ref, dy_ref,
             dqkv_ref, dgb_ref, dgate_ref, dnw_ref, dstate):
        @pl.when(pl.program_id(0) == 0)
        def _():
            dstate[...] = jnp.zeros_like(dstate)
            dnw_ref[...] = jnp.zeros_like(dnw_ref)

        mk = _gdn_masks(c)
        rows = [slice(sub * c, (sub + 1) * c) for sub in range(grp)]
        lanes = [slice(h * hd, (h + 1) * hd) for h in range(nh)]
        gbs = [gb_ref[r, :] for r in rows]
        qs = [q_ref[rows[sub], lanes[h]] for sub, h in inst]
        ks = [k_ref[rows[sub], lanes[h]] for sub, h in inst]
        vs = [v_ref[rows[sub], lanes[h]] for sub, h in inst]
        betas = [gbs[sub][:, nh + h:nh + h + 1] for sub, h in inst]
        sts = [st_ref[pl.ds(i * hd, hd), :] for i in range(len(inst))]
        pres = _gdn_pre(qs, ks, vs, [gbs[sub][:, h:h + 1] for sub, h in inst], betas, mk,
                        ts=[t_ref[pl.ds(i * c, c), :] for i in range(len(inst))])
        vns, outs, _ = _gdn_apply(pres, sts, leaving=False)

        nw = nw_ref[...]
        dnw = jnp.zeros((1, hd), F32)
        dos = []
        for i, (sub, h) in enumerate(inst):
            o = outs[i]
            gate = gate_ref[rows[sub], lanes[h]]
            dy = dy_ref[rows[sub], lanes[h]].astype(F32)
            sg = _sigmoid(gate)
            rs = lax.rsqrt(jnp.mean(o * o, axis=-1, keepdims=True) + EPS)
            nrm = o * rs
            dgate_ref[rows[sub], lanes[h]] = (dy * nrm * nw * sg * (1.0 + gate * (1.0 - sg))).astype(BF16)
            dnv = dy * (gate * sg)
            dnw = dnw + jnp.sum(dnv * nrm, axis=0, keepdims=True)
            dno = dnv * nw
            dos.append(rs * (dno - nrm * jnp.mean(dno * nrm, axis=-1, keepdims=True)))
        dnw_ref[...] += dnw

        from_o = [_dot(p["attn"], do, 0, 0) for p, do in zip(pres, dos)]
        to_st = [_dot(p["qd"], do, 0, 0) for p, do in zip(pres, dos)]
        dst = [dstate[ls, :] for ls in lanes]
        dsts = [None] * len(inst)
        dvns = [None] * len(inst)
        for sub in reversed(range(grp)):
            idx = [sub * nh + h for h in range(nh)]
            for h, i in enumerate(idx):
                dsts[i] = dst[h]
                dvns[i] = from_o[i] + _dot(pres[i]["kd"], dst[h], 1, 0)
            dst = [pres[i]["cd"] * dst[h] + to_st[i] - _dot(pres[i]["w"], dvns[i], 0, 0) for h, i in enumerate(idx)]
        for h in range(nh):
            dstate[lanes[h], :] = dst[h]

        grads = _gdn_bwd_rest(qs, ks, vs, betas, sts, pres, vns, dos, dvns, dsts, mk)
        lane = lax.broadcasted_iota(jnp.int32, (c, LANE), 1)
        dgb = [jnp.zeros((c, LANE), F32) for _ in range(grp)]
        for (sub, h), (dq, dk, dv, dg, dbeta) in zip(inst, grads):
            dqkv_ref[rows[sub], lanes[h]] = dq
            dqkv_ref[rows[sub], slice(GDN_WIDTH + h * hd, GDN_WIDTH + (h + 1) * hd)] = dk
            dqkv_ref[rows[sub], slice(2 * GDN_WIDTH + h * hd, 2 * GDN_WIDTH + (h + 1) * hd)] = dv
            dgb[sub] = jnp.where(lane == h, dg, jnp.where(lane == nh + h, dbeta, dgb[sub]))
        for sub in range(grp):
            dgb_ref[rows[sub], :] = dgb[sub]

    return pl.pallas_call(
        body, grid=(n_chunks // grp,),
        in_specs=[rev(0, GDN_WIDTH), rev(1, GDN_WIDTH), rev(2, GDN_WIDTH), rev(0, LANE),
                  rev(COL_GATE // GDN_WIDTH, GDN_WIDTH), BS((1, hd), lambda n: (0, 0)),
                  BS((grp * nh * hd, hd), lambda n: (last - n, 0)), BS((grp * nh * c, c), lambda n: (last - n, 0)),
                  rev(1, GDN_WIDTH)],
        out_specs=[rev(0, 3 * GDN_WIDTH), rev(0, LANE), rev(0, GDN_WIDTH), BS((1, hd), lambda n: (0, 0))],
        out_shape=[SDS((s, 3 * GDN_WIDTH), F32), SDS((s, LANE), F32), SDS((s, GDN_WIDTH), BF16), SDS((1, hd), F32)],
        scratch_shapes=[pltpu.VMEM((nh * hd, hd), F32)],
        compiler_params=_cp(), name=name)(qkv, qkv, qkv, gbeta, proj, norm_w, states, tinv, dcat)


XATTN_TQ = 512


def _xattn_probs(qh, kh):
    sc = _dot(qh, kh, 1, 1) * (XATTN_HEAD_DIM ** -0.5)
    p = jnp.exp(sc - jnp.max(sc, axis=-1, keepdims=True))
    return p / jnp.sum(p, axis=-1, keepdims=True)


def _xattn_fwd(q, kv, *, name):
    s, d = q.shape
    m = kv.shape[0]
    tq, hd = _tile(s, XATTN_TQ), XATTN_HEAD_DIM

    def body(q_ref, k_ref, v_ref, o_ref):
        for h in range(XATTN_HEADS):
            ls = slice(h * hd, (h + 1) * hd)
            p = _xattn_probs(q_ref[:, ls], k_ref[:, ls])
            o_ref[:, ls] = _dot(p, v_ref[:, ls], 1, 0).astype(BF16)

    return pl.pallas_call(
        body, grid=(s // tq,),
        in_specs=[BS((tq, d), lambda i: (i, 0)), BS((m, d), lambda i: (0, 0)), BS((m, d), lambda i: (0, 1))],
        out_specs=BS((tq, d), lambda i: (i, 0)), out_shape=SDS((s, d), BF16),
        compiler_params=_cp(), name=name)(q, kv, kv)


def _xattn_bwd(q, kv, do, *, name):
    s, d = q.shape
    m = kv.shape[0]
    tq, hd = _tile(s, XATTN_TQ), XATTN_HEAD_DIM
    scale = hd ** -0.5

    def body(q_ref, k_ref, v_ref, do_ref, dq_ref, dkv_ref):
        @pl.when(pl.program_id(0) == 0)
        def _():
            dkv_ref[...] = jnp.zeros_like(dkv_ref)

        for h in range(XATTN_HEADS):
            ls = slice(h * hd, (h + 1) * hd)
            vs = slice(d + h * hd, d + (h + 1) * hd)
            qh, kh, doh = q_ref[:, ls], k_ref[:, ls], do_ref[:, ls]
            p = _xattn_probs(qh, kh)
            dp = _dot(doh, v_ref[:, ls], 1, 1)
            ds = p * (dp - jnp.sum(p * dp, axis=-1, keepdims=True)) * scale
            dq_ref[:, ls] = _dot(ds, kh, 1, 0).astype(BF16)
            dkv_ref[:, ls] += _dot(ds, qh, 0, 0)
            dkv_ref[:, vs] += _dot(p, doh, 0, 0)

    row = BS((tq, d), lambda i: (i, 0))
    return pl.pallas_call(
        body, grid=(s // tq,),
        in_specs=[row, BS((m, d), lambda i: (0, 0)), BS((m, d), lambda i: (0, 1)), row],
        out_specs=[row, BS((m, 2 * d), lambda i: (0, 0))],
        out_shape=[SDS((s, d), BF16), SDS((m, 2 * d), F32)],
        compiler_params=_cp(), name=name)(q, kv, kv, do)


def _pad_lanes(vec4):
    return jnp.zeros((1, LANE), F32).at[0, :GDN_HEADS].set(vec4)


def _layer_fwd(h0, mem, tabs, p):
    sv = dict(h0=h0)
    hn1 = _rmsnorm(h0, p["norm_mix_pre"], name="norm_mix_pre")
    proj = _mm(hn1, p["w_in"], name="mm_in")
    ya, lse = _attn_flash_fwd(proj, tabs, name="attn_fwd")
    yc = _sconv_fwd(proj, p["conv_short"], name="sconv_fwd")
    qkv = _gdn_pre_fwd(proj, p["conv_gdn"], name="gdn_pre_fwd")
    gbeta = _gdn_gates_fwd(proj, p["gdn_a_log"], p["gdn_dt_bias"], name="gdn_gates_fwd")
    yg, states, tinv = _gdn_core_fwd(qkv, gbeta, proj, p["gdn_norm"], name="gdn_core_fwd")
    cat = jnp.concatenate([ya, yc, yg], axis=-1)
    mix = _mm(cat, p["w_out"], name="mm_out")
    h1 = _resnorm(h0, mix, p["norm_mix_post"], name="norm_mix_post")
    hn2 = _rmsnorm(h1, p["norm_xattn_pre"], name="norm_xattn_pre")
    memn = _rmsnorm(mem, p["norm_mem"], name="norm_mem")
    xq = _mm(hn2, p["w_xq"], out_dtype=BF16, name="mm_xq")
    kv = _mm(memn, p["w_xkv"], out_dtype=BF16, b_shards=True, name="mm_xkv")
    xo = _xattn_fwd(xq, kv, name="xattn_fwd")
    xa = _mm(xo, p["w_xo"], name="mm_xo")
    h2 = _resnorm(h1, xa, p["norm_xattn_post"], name="norm_xattn_post")
    hn3 = _rmsnorm(h2, p["norm_ffn_pre"], name="norm_ffn_pre")
    gu = _mm(hn3, p["w_gate_up"], b_shards=True, halves=True, name="mm_gate_up")
    act = _swiglu(gu, name="swiglu_fwd")
    f = _mm(act, p["w_down"], name="mm_down")
    h3 = _resnorm(h2, f, p["norm_ffn_post"], name="norm_ffn_post")
    sv.update(hn1=hn1, proj=proj, lse=lse, qkv=qkv, gbeta=gbeta, states=states, tinv=tinv, cat=cat, mix=mix, h1=h1, hn2=hn2,
              memn=memn, xq=xq, kv=kv, xo=xo, xa=xa, h2=h2, hn3=hn3, gu=gu, act=act, f=f)
    return h3, sv


def _layer_bwd(dh3, mem, tabs, p, sv, after_ffn=None, after_xattn=None):
    g = {}
    df, g["norm_ffn_post"] = _rmsnorm_bwd(sv["f"], p["norm_ffn_post"], dh3, name="norm_ffn_post_bwd")
    dact = _mm(df, p["w_down"], tb=True, name="mm_down_da")
    g["w_down"] = _mm(sv["act"], df, ta=True, name="mm_down_dw")
    dgu = _swiglu_bwd(sv["gu"], dact, name="swiglu_bwd")
    dhn3 = _mm(dgu, p["w_gate_up"], tb=True, b_shards=True, halves=True, name="mm_gate_up_da")
    g["w_gate_up"] = _mm(sv["hn3"], dgu, ta=True, out_shards=True, halves=True, name="mm_gate_up_dw")
    dh2, g["norm_ffn_pre"] = _rmsnorm_bwd(sv["h2"], p["norm_ffn_pre"], dhn3, res=dh3, name="norm_ffn_pre_bwd")
    token = after_ffn(dh2) if after_ffn is not None else None
    w_post = p["norm_xattn_post"] if token is None else p["norm_xattn_post"] + token[:1, :1]
    dxa, g["norm_xattn_post"] = _rmsnorm_bwd(sv["xa"], w_post, dh2, name="norm_xattn_post_bwd")
    dxo = _mm(dxa, p["w_xo"], tb=True, name="mm_xo_da")
    g["w_xo"] = _mm(sv["xo"], dxa, ta=True, name="mm_xo_dw")
    dxq, dkv = _xattn_bwd(sv["xq"], sv["kv"], dxo, name="xattn_bwd")
    dhn2 = _mm(dxq, p["w_xq"], tb=True, name="mm_xq_da")
    g["w_xq"] = _mm(sv["hn2"], dxq, ta=True, name="mm_xq_dw")
    dmemn = _mm(dkv, p["w_xkv"], tb=True, b_shards=True, name="mm_xkv_da")
    g["w_xkv"] = _mm(sv["memn"], dkv, ta=True, out_shards=True, name="mm_xkv_dw")
    _, g["norm_mem"] = _rmsnorm_bwd(mem, p["norm_mem"], dmemn, name="norm_mem_bwd")
    dh1, g["norm_xattn_pre"] = _rmsnorm_bwd(sv["h1"], p["norm_xattn_pre"], dhn2, res=dh2, name="norm_xattn_pre_bwd")
    token = after_xattn(g, dh1) if after_xattn is not None else None
    w_post = p["norm_mix_post"] if token is None else p["norm_mix_post"] + token[:1, :1]
    dmix, g["norm_mix_post"] = _rmsnorm_bwd(sv["mix"], w_post, dh1, name="norm_mix_post_bwd")
    dcat = _mm(dmix, p["w_out"], tb=True, name="mm_out_da")
    g["w_out"] = _mm(sv["cat"], dmix, ta=True, name="mm_out_dw")
    proj = sv["proj"]
    daq, dak, dav = _attn_flash_bwd(proj, tabs, sv["cat"], sv["lse"], dcat, name="attn_bwd")
    dcb, dcc, dcx, g["conv_short"] = _sconv_bwd(proj, p["conv_short"], dcat, name="sconv_bwd")
    dqkv, dgbeta, dgate, g["gdn_norm"] = _gdn_core_bwd(sv["qkv"], sv["gbeta"], proj, p["gdn_norm"], sv["states"], sv["tinv"],
                                                        dcat, name="gdn_core_bwd")
    dgqkv, g["conv_gdn"] = _gdn_pre_bwd(proj, p["conv_gdn"], dqkv, name="gdn_pre_bwd")
    dab, g["gdn_a_log"], g["gdn_dt_bias"] = _gdn_gates_bwd(proj, p["gdn_a_log"], p["gdn_dt_bias"], dgbeta,
                                                          name="gdn_gates_bwd")
    s = proj.shape[0]
    dproj = jnp.concatenate([daq, dak, dav, dcb, dcc, dcx, dgqkv, dgate, dab,
                             jnp.zeros((s, IN_PAD - COL_AB - LANE), BF16)], axis=-1)
    dhn1 = _mm(dproj, p["w_in"], tb=True, name="mm_in_da")
    g["w_in"] = _mm(sv["hn1"], dproj, ta=True, name="mm_in_dw")
    dh0, g["norm_mix_pre"] = _rmsnorm_bwd(sv["h0"], p["norm_mix_pre"], dhn1, res=dh1, name="norm_mix_pre_bwd")
    return dh0, g


MATRICES = ("w_in", "w_out", "w_xq", "w_xkv", "w_xo", "w_gate_up", "w_down")
VECTORS = ("norm_mix_pre", "norm_mix_post", "conv_short", "conv_gdn", "gdn_a_log", "gdn_dt_bias", "gdn_norm",
           "norm_mem", "norm_xattn_pre", "norm_xattn_post", "norm_ffn_pre", "norm_ffn_post")


def _w_in_segments(n_shards=4):
    c = IN_WIDTH // n_shards
    moves = ((0, COL_GATE, 0), (COL_GATE, COL_GATE + 8, COL_AB), (COL_GATE + 8, IN_WIDTH, COL_GATE))
    segs = []
    for s in range(n_shards):
        for lo, hi, dst in moves:
            a, b = max(lo, s * c), min(hi, (s + 1) * c)
            if a < b:
                segs.append((s, a - s * c, dst + a - lo, b - a))
    return segs


def _w_in_pack(g, *, name):
    ns, r, c = g.shape
    tr = _tile(r, 256)

    def body(g_ref, o_ref):
        o_ref[:, IN_WIDTH:] = jnp.zeros((tr, IN_PAD - IN_WIDTH), o_ref.dtype)
        for s, src, dst, width in _w_in_segments(ns):
            o_ref[:, dst:dst + width] = g_ref[s, :, src:src + width]

    return pl.pallas_call(
        body, grid=(r // tr,), in_specs=[BS((ns, tr, c), lambda i: (0, i, 0))],
        out_specs=BS((tr, IN_PAD), lambda i: (i, 0)), out_shape=SDS((r, IN_PAD), g.dtype),
        compiler_params=_cp(), name=name)(g)


def _w_in_unpack(dw, *, name):
    r = dw.shape[0]
    ns, c = 4, IN_WIDTH // 4
    tr = _tile(r, 256)

    def body(d_ref, o_ref):
        for s, src, dst, width in _w_in_segments(ns):
            o_ref[s, :, src:src + width] = d_ref[:, dst:dst + width]

    return pl.pallas_call(
        body, grid=(r // tr,), in_specs=[BS((tr, IN_PAD), lambda i: (i, 0))],
        out_specs=BS((ns, tr, c), lambda i: (0, i, 0)), out_shape=SDS((ns, r, c), dw.dtype),
        compiler_params=_cp(), name=name)(dw)


def _layer_params(full, l):
    p = {n: full[n][l] for n in MATRICES}
    for n in VECTORS:
        v = full[n][l]
        if n in ("gdn_a_log", "gdn_dt_bias"):
            p[n] = _pad_lanes(v)
        elif v.ndim == 1:
            p[n] = v.reshape(1, -1)
        else:
            p[n] = v
    return p


def _local_step(x, mem, pos, target, full, matrices_for=None, on_grads=None, mid_backward=None,
                after_xattn=None):
    tabs = _rope_tables(pos, name="rope_tables")
    h = x
    saved, params = [], []
    for l in range(DEPTH):
        if matrices_for is not None:
            full = {**full, **{n: {l: v} for n, v in matrices_for(l, h).items()}}
        p = _layer_params(full, l)
        h, sv = _layer_fwd(h, mem, tabs, p)
        params.append(p)
        saved.append(sv)
    loss_row, dh = _loss_grad(h, target, name="loss_grad")
    grads = [None] * DEPTH
    token = None
    for l in reversed(range(DEPTH)):
        p = params[l]
        if token is not None:
            p = {**p, "norm_ffn_post": p["norm_ffn_post"] + token[:1, :1]}
        late = None if after_xattn is None else (lambda g, dh1, l=l: after_xattn(l, g, dh1))
        dh, grads[l] = _layer_bwd(dh, mem, tabs, p, saved[l], after_ffn=mid_backward, after_xattn=late)
        if on_grads is not None:
            token = on_grads(l, grads[l], dh)
    return loss_row, dh, grads


ANY = pl.BlockSpec(memory_space=pl.ANY)
MESH = pl.DeviceIdType.MESH


def _flip(pos, mask):
    return tuple(1 - v if m else v for v, m in zip(pos, mask))


def _exchange(ins, out_shapes, remote, local, *, name):
    n_in = len(ins)
    n_out = len(out_shapes)

    def at(ref, idx):
        return ref.at[idx] if idx else ref

    def body(*refs):
        in_refs = refs[:n_in]
        out_refs = refs[n_in:n_in + n_out]
        send_sems, recv_sems, local_sems = refs[n_in + n_out:]
        me = (lax.axis_index("x"), lax.axis_index("y"), lax.axis_index("c"))
        waits = []
        for k, (ii, src_at, oi, dst_at, mask) in enumerate(remote):
            peer = _flip(me, mask)
            pltpu.make_async_remote_copy(
                src_ref=at(in_refs[ii], src_at(me, peer)), dst_ref=at(out_refs[oi], dst_at(me)),
                send_sem=send_sems.at[k], recv_sem=recv_sems.at[k], device_id=peer, device_id_type=MESH).start()
            waits.append(pltpu.make_async_remote_copy(
                src_ref=at(in_refs[ii], src_at(peer, me)), dst_ref=at(out_refs[oi], dst_at(peer)),
                send_sem=send_sems.at[k], recv_sem=recv_sems.at[k], device_id=peer, device_id_type=MESH))
        own = []
        for k, (ii, src_at, oi, dst_at) in enumerate(local):
            cp = pltpu.make_async_copy(at(in_refs[ii], src_at(me)), at(out_refs[oi], dst_at(me)), local_sems.at[k])
            cp.start()
            own.append(cp)
        for w in waits:
            w.wait_send()
            w.wait_recv()
        for cp in own:
            cp.wait()

    return pl.pallas_call(
        body, in_specs=[ANY] * n_in, out_specs=[ANY] * n_out, out_shape=list(out_shapes),
        scratch_shapes=[pltpu.SemaphoreType.DMA((len(remote),)), pltpu.SemaphoreType.DMA((len(remote),)),
                        pltpu.SemaphoreType.DMA((max(len(local), 1),))],
        name=name)(*ins)


HBM = pl.BlockSpec(memory_space=pltpu.HBM)
SEM = pl.BlockSpec(memory_space=pltpu.SEMAPHORE)
SPLIT_EFFECT = pltpu.SideEffectType.DATAFLOW_SIDE_EFFECTING


def _exchange_start(ins, land_shapes, remote, *, name):
    n_in, n_land, n_cp = len(ins), len(land_shapes), len(remote)

    def body(*refs):
        in_refs, land_refs = refs[:n_in], refs[n_in:n_in + n_land]
        send_sems, recv_sems = refs[n_in + n_land:n_in + n_land + 2]
        token = refs[-1]
        me = (lax.axis_index("x"), lax.axis_index("y"), lax.axis_index("c"))
        for k, (ii, src_at, oi, dst_at, mask) in enumerate(remote):
            peer = _flip(me, mask)
            idx_s, idx_d = src_at(me, peer), dst_at(me)
            pltpu.make_async_remote_copy(
                src_ref=in_refs[ii].at[idx_s] if idx_s else in_refs[ii],
                dst_ref=land_refs[oi].at[idx_d] if idx_d else land_refs[oi],
                send_sem=send_sems.at[k], recv_sem=recv_sems.at[k], device_id=peer, device_id_type=MESH).start()
        token[...] = jnp.zeros_like(token)

    buffers = [pltpu.with_memory_space_constraint(a, pltpu.HBM) for a in ins]
    buffers += [pltpu.with_memory_space_constraint(lax.empty(s.shape, s.dtype), pltpu.HBM) for s in land_shapes]
    out = pl.pallas_call(
        body, name=name,
        out_shape=(pltpu.SemaphoreType.DMA((n_cp,)), pltpu.SemaphoreType.DMA((n_cp,)),
                   *[pltpu.HBM(b.shape, b.dtype) for b in buffers], SDS((8, LANE), F32)),
        in_specs=[HBM] * len(buffers),
        out_specs=(SEM, SEM, *[HBM] * len(buffers), pl.BlockSpec(memory_space=pltpu.VMEM)),
        input_output_aliases={i: 2 + i for i in range(len(buffers))},
        compiler_params=pltpu.CompilerParams(has_side_effects=SPLIT_EFFECT))(*buffers)
    return out[0], out[1], list(out[2:2 + n_in]), list(out[2 + n_in:2 + n_in + n_land]), out[-1]


def _exchange_wait(send_sems, recv_sems, ins, lands, remote, after, *, name):
    n_in, n_land = len(ins), len(lands)

    def body(*refs):
        in_refs, land_refs = refs[:n_in], refs[n_in:n_in + n_land]
        send_sems_, recv_sems_ = refs[n_in + n_land:n_in + n_land + 2]
        me = (lax.axis_index("x"), lax.axis_index("y"), lax.axis_index("c"))
        for k, (ii, src_at, oi, dst_at, mask) in enumerate(remote):
            peer = _flip(me, mask)
            idx_s, idx_d = src_at(peer, me), dst_at(peer)
            cp = pltpu.make_async_remote_copy(
                src_ref=in_refs[ii].at[idx_s] if idx_s else in_refs[ii],
                dst_ref=land_refs[oi].at[idx_d] if idx_d else land_refs[oi],
                send_sem=send_sems_.at[k], recv_sem=recv_sems_.at[k], device_id=peer, device_id_type=MESH)
            cp.wait_send()
            cp.wait_recv()

    buffers = list(ins) + list(lands)
    out = pl.pallas_call(
        body, name=name, out_shape=tuple(pltpu.HBM(b.shape, b.dtype) for b in buffers),
        in_specs=[HBM] * len(buffers) + [SEM, SEM, ANY], out_specs=tuple([HBM] * len(buffers)),
        input_output_aliases={i: i for i in range(len(buffers))},
        compiler_params=pltpu.CompilerParams(has_side_effects=SPLIT_EFFECT))(*buffers, send_sems, recv_sems, after)
    return list(out[:n_in]), list(out[n_in:])


def _chip(pos):
    return 2 * pos[0] + pos[1]


XY_MASKS = ((1, 0, 0), (0, 1, 0), (1, 1, 0))
SIBLING = (0, 0, 1)
ALL_MASKS = tuple((a, b, c) for a in (0, 1) for b in (0, 1) for c in (0, 1))[1:]


def _gather_xy(arrs, *, name):
    n = len(arrs)
    outs = [SDS((4,) + a.shape, a.dtype) for a in arrs]
    halves = [a.shape[0] // 2 for a in arrs]

    def body(*refs):
        in_refs, out_refs = refs[:n], refs[n:2 * n]
        ici_send, ici_recv, d2d_send, d2d_recv = refs[2 * n:]
        me = (lax.axis_index("x"), lax.axis_index("y"), lax.axis_index("c"))
        sibling = _flip(me, SIBLING)
        flows = []
        for i in range(n):
            mine = pl.ds(me[2] * halves[i], halves[i])
            other = pl.ds(sibling[2] * halves[i], halves[i])
            for m in XY_MASKS:
                k = len(flows)
                peer = _flip(me, m)

                def remote(src, dst, sems, to, k=k):
                    return pltpu.make_async_remote_copy(src_ref=src, dst_ref=dst, send_sem=sems[0].at[k],
                                                        recv_sem=sems[1].at[k], device_id=to, device_id_type=MESH)

                landed = out_refs[i].at[_chip(peer), mine]
                send = remote(in_refs[i].at[mine], out_refs[i].at[_chip(me), mine], (ici_send, ici_recv), peer)
                send.start()
                arrive = remote(in_refs[i].at[mine], landed, (ici_send, ici_recv), peer)
                forward = remote(landed, landed, (d2d_send, d2d_recv), sibling)
                handed = remote(out_refs[i].at[_chip(peer), other], out_refs[i].at[_chip(peer), other],
                                (d2d_send, d2d_recv), sibling)
                flows.append((send, arrive, forward, handed))
        for _, arrive, forward, _ in flows:
            arrive.wait_recv()
            forward.start()
        for send, _, forward, handed in flows:
            handed.wait_recv()
            send.wait_send()
            forward.wait_send()

    n_flows = 3 * n
    return pl.pallas_call(
        body, in_specs=[ANY] * n, out_specs=[ANY] * n, out_shape=outs,
        scratch_shapes=[pltpu.SemaphoreType.DMA((n_flows,))] * 4, name=name)(*arrs)


def _gather_all(arr, *, name):
    slot = lambda pos: (4 * pos[0] + 2 * pos[1] + pos[2],)
    whole = lambda *_: ()
    remote = [(0, whole, 0, slot, m) for m in ALL_MASKS]
    return _exchange([arr], [SDS((8,) + arr.shape, arr.dtype)], remote, [(0, whole, 0, slot)], name=name)[0]


def _send_to_sibling(arrs, *, name):
    outs = [SDS(a.shape, a.dtype) for a in arrs]
    whole = lambda *_: ()
    remote = [(i, whole, i, whole, SIBLING) for i in range(len(arrs))]
    return _exchange(arrs, outs, remote, [], name=name)


def _add_half(g, other, core, *, name):
    n4, nl, r, c = g.shape
    half = r // 2
    g3 = g.reshape(n4 * nl, 2, half, c)
    o3 = other.reshape(n4 * nl, half, c)
    tr = _rows_tile(half) if half > 512 else half

    def body(core_ref, g_ref, o_ref, out_ref):
        out_ref[...] = (g_ref[...] + o_ref[...]).astype(BF16)

    return pl.pallas_call(
        body,
        grid_spec=pltpu.PrefetchScalarGridSpec(
            num_scalar_prefetch=1, grid=(n4 * nl, half // tr),
            in_specs=[BS((None, None, tr, c), lambda i, j, core_ref: (i, core_ref[0], j, 0)),
                      BS((None, tr, c), lambda i, j, core_ref: (i, j, 0))],
            out_specs=BS((None, tr, c), lambda i, j, core_ref: (i, j, 0))),
        out_shape=SDS((n4 * nl, half, c), BF16), compiler_params=_cp(), name=name)(core, g3, o3).reshape(n4, nl, half, c)


def _sum_chips(parts, mine, chip, *, name):
    n4, nl, h, c = mine.shape
    tr = _rows_tile(h) if h > 512 else h

    def body(chip_ref, p_ref, own_ref, out_ref):
        me = chip_ref[0]
        own = own_ref[...].astype(F32)
        across = [p_ref[j].astype(F32) for j in range(len(XY_MASKS))]
        t = []
        for s in range(n4):
            rel = s ^ me
            t.append(jnp.where(rel == 0, own, jnp.where(rel == 2, across[0], jnp.where(rel == 1, across[1], across[2]))))
        out_ref[...] = ((t[0] + t[1]) + t[2]) + t[3]

    return pl.pallas_call(
        body,
        grid_spec=pltpu.PrefetchScalarGridSpec(
            num_scalar_prefetch=1, grid=(nl, h // tr),
            in_specs=[BS((len(XY_MASKS), None, tr, c), lambda i, j, chip_ref: (0, i, j, 0)),
                      BS((None, None, tr, c), lambda i, j, chip_ref: (chip_ref[0], i, j, 0))],
            out_specs=BS((None, tr, c), lambda i, j, chip_ref: (i, j, 0))),
        out_shape=SDS((nl, h, c), F32), compiler_params=_cp(), name=name)(chip, parts, mine)


def _sum_devices(parts, *, name):
    n, r, c = parts.shape

    def body(p_ref, out_ref):
        acc = p_ref[0]
        for d in range(1, n):
            acc = acc + p_ref[d]
        out_ref[...] = acc

    return pl.pallas_call(
        body, grid=(1,), in_specs=[BS((n, r, c), lambda i: (0, 0, 0))], out_specs=BS((r, c), lambda i: (0, 0)),
        out_shape=SDS((r, c), F32), compiler_params=_cp(), name=name)(parts)


WEIGHTS = ("norm_mix_pre", "norm_mix_post", "w_in", "conv_short", "conv_gdn", "gdn_a_log", "gdn_dt_bias",
           "gdn_norm", "w_out", "norm_mem", "norm_xattn_pre", "norm_xattn_post", "w_xq", "w_xkv", "w_xo",
           "norm_ffn_pre", "norm_ffn_post", "w_gate_up", "w_down")
COL_SHARDED = ("w_in", "w_xkv", "w_gate_up", "conv_short", "conv_gdn")
ROW_SHARDED = ("w_out", "w_xq", "w_xo", "w_down")
SMALL_SHARDED = ("conv_short", "conv_gdn")
SMALL_ROW_PAD = 8


KEPT_AS_SHARDS = ("w_xkv", "w_gate_up")


def _from_shards(n, g):
    if n in KEPT_AS_SHARDS:
        return g
    if n == "w_in":
        return _w_in_pack(g, name="w_in_pack")
    if n in COL_SHARDED:
        t = jnp.moveaxis(g, 0, -2)
        return t.reshape(t.shape[:-2] + (-1,))
    return g.reshape(-1, g.shape[-1])


def _to_shards(n, g):
    if n in KEPT_AS_SHARDS:
        return g
    if n == "w_in":
        return _w_in_unpack(g, name="w_in_unpack")
    return g.reshape(4, -1, g.shape[-1])


def _pack_small(grads):
    rows = []
    for g in grads:
        for n in WEIGHTS:
            if n not in MATRICES:
                part = g[n].reshape(-1, LANE)
                rows.append(jnp.pad(part, ((0, -part.shape[0] % SMALL_ROW_PAD), (0, 0))))
    return jnp.concatenate(rows, axis=0)


def _unpack_small(packed, like):
    out, at = [], 0
    for _ in range(DEPTH):
        g = {}
        for n in WEIGHTS:
            if n not in MATRICES:
                shape = like[n].shape
                k = math.prod(shape) // LANE
                g[n] = packed[at:at + k].reshape(shape)
                at += k + (-k % SMALL_ROW_PAD)
        out.append(g)
    return out


def kernel(x, mem, positions, norm_mix_pre, norm_mix_post, w_in, conv_short, conv_gdn, gdn_a_log, gdn_dt_bias, gdn_norm, w_out, norm_mem, norm_xattn_pre, norm_xattn_post, w_xq, w_xkv, w_xo, norm_ffn_pre, norm_ffn_post, w_gate_up, w_down, loss_target, m_norm_mix_pre, m_norm_mix_post, m_w_in, m_conv_short, m_conv_gdn, m_gdn_a_log, m_gdn_dt_bias, m_gdn_norm, m_w_out, m_norm_mem, m_norm_xattn_pre, m_norm_xattn_post, m_w_xq, m_w_xkv, m_w_xo, m_norm_ffn_pre, m_norm_ffn_post, m_w_gate_up, m_w_down, v_norm_mix_pre, v_norm_mix_post, v_w_in, v_conv_short, v_conv_gdn, v_gdn_a_log, v_gdn_dt_bias, v_gdn_norm, v_w_out, v_norm_mem, v_norm_xattn_pre, v_norm_xattn_post, v_w_xq, v_w_xkv, v_w_xo, v_norm_ffn_pre, v_norm_ffn_post, v_w_gate_up, v_w_down):
    args = dict(locals())
    w = {n: args[n] for n in WEIGHTS}
    m = {n: args["m_" + n] for n in WEIGHTS}
    v = {n: args["v_" + n] for n in WEIGHTS}
    seq = x.shape[1]
    chip = 2 * lax.axis_index("x") + lax.axis_index("y")
    core = lax.axis_index("c").astype(jnp.int32).reshape(1)

    def cast(n, l):
        return w[n][l].astype(BF16)

    first = [cast(n, 0) for n in MATRICES] + [w[n] for n in SMALL_SHARDED]
    blocks = _gather_xy(first, name="gather_weights")
    blocks = [lax.dynamic_update_index_in_dim(b, o, chip, axis=0) for b, o in zip(blocks, first)]
    full = {n: _from_shards(n, b) for n, b in zip(list(MATRICES) + list(SMALL_SHARDED), blocks)}
    layer0 = {n: full.pop(n) for n in MATRICES}
    for n in WEIGHTS:
        if n not in full and n not in MATRICES:
            full[n] = w[n]
    whole = lambda *_: ()
    gather_plan = [(i, whole, i, (lambda sender: (_chip(sender),)), mask)
                   for i in range(len(MATRICES)) for mask in XY_MASKS]
    in_flight = {}
    started = jnp.zeros((1, 1), F32)
    for l in range(1, DEPTH):
        own = [cast(n, l) for n in MATRICES]
        out = _exchange_start(own, [SDS((4,) + o.shape, o.dtype) for o in own], gather_plan, name=f"gather_start_{l}")
        in_flight[l] = (own,) + out[:4]
        started = started + out[4][:1, :1]
    full["norm_mix_pre"] = full["norm_mix_pre"] + started

    def matrices_for(l, h):
        if l == 0:
            return layer0
        own, send_sems, recv_sems, thru, lands = in_flight.pop(l)
        _, lands = _exchange_wait(send_sems, recv_sems, thru, lands, gather_plan, h, name=f"gather_wait_{l}")
        lands = [lax.dynamic_update_index_in_dim(b, o, chip, axis=0) for b, o in zip(lands, own)]
        return {n: _from_shards(n, b) for n, b in zip(MATRICES, lands)}

    chip1 = chip.astype(jnp.int32).reshape(1)
    early = ("w_xq", "w_xkv", "w_xo", "w_gate_up", "w_down")
    reduced = {l: {} for l in range(DEPTH)}
    swapping, sending, sharing = [], [], []

    def scatter_plan(count):
        return [(i, (lambda sender, receiver: (_chip(receiver),)), i, (lambda sender, j=j: (j,)), mask)
                for i in range(count) for j, mask in enumerate(XY_MASKS)]

    def swap_plan(mine):
        def half_rows(shape):
            half = shape[2] // 2
            return lambda sender, receiver: (slice(None), slice(None), pl.ds(receiver[2] * half, half))

        plan = [(i, half_rows(a.shape), i, whole, SIBLING) for i, a in enumerate(mine)]
        return plan, [SDS(a.shape[:2] + (a.shape[2] // 2, a.shape[3]), a.dtype) for a in mine]

    def start_scatter(tag, names, mine, theirs):
        pair = [_add_half(a, b, core, name="grads_pair_sum") for a, b in zip(mine, theirs)]
        lands = [SDS((len(XY_MASKS),) + p.shape[1:], p.dtype) for p in pair]
        send_sems, recv_sems, pair, lands, token = _exchange_start(pair, lands, scatter_plan(len(names)),
                                                                   name=f"scatter_start_{tag}")
        sending.append((tag, names, send_sems, recv_sems, pair, lands))
        return token

    def finish_scatter(after):
        tag, names, send_sems, recv_sems, pair, lands = sending.pop(0)
        pair, parts = _exchange_wait(send_sems, recv_sems, pair, lands, scatter_plan(len(names)), after,
                                     name=f"scatter_wait_{tag}")
        layer = int(tag[0])
        sums = [_sum_chips(p, pr, chip1, name="grads_chip_sum") for p, pr in zip(parts, pair)]
        token = None
        if layer > 0:
            plan = [(i, whole, i, whole, SIBLING) for i in range(len(sums))]
            send_sems, recv_sems, sums, lands, token = _exchange_start(
                sums, [SDS(r.shape, r.dtype) for r in sums], plan, name=f"share_start_{tag}")
            sharing.append((layer, names, send_sems, recv_sems, sums, lands, plan, tag))
        reduced[layer].update(zip(names, sums))
        return token

    def on_grads(l, g, dh):
        tokens = []
        while sending:
            tokens.append(finish_scatter(dh))
        names = [n for n in MATRICES if not (l == 0 and n in early)]
        mine = [_to_shards(n, g[n])[:, None] for n in names]
        plan, lands = swap_plan(mine)
        send_sems, recv_sems, mine, lands, token = _exchange_start(mine, lands, plan, name=f"swap_start_{l}")
        swapping.append((str(l), names, send_sems, recv_sems, mine, lands, plan))
        for t in tokens:
            if t is not None:
                token = token + t
        return token

    def mid_backward(after):
        if not swapping:
            return None
        tag, names, send_sems, recv_sems, mine, lands, plan = swapping.pop()
        mine, theirs = _exchange_wait(send_sems, recv_sems, mine, lands, plan, after, name=f"swap_wait_{tag}")
        return start_scatter(tag, names, mine, theirs)

    def after_xattn(l, g, dh1):
        if l != 0:
            return None
        mine = [_to_shards(n, g[n])[:, None] for n in early]
        plan, lands = swap_plan(mine)
        theirs = _exchange(mine, lands, plan, [], name="grads_swap_early")
        return start_scatter("0a", early, mine, theirs)

    loss_row, dx, grads = _local_step(x[0], mem[0], positions.reshape(seq, 1), loss_target[0], full,
                                      matrices_for=matrices_for, on_grads=on_grads, mid_backward=mid_backward,
                                      after_xattn=after_xattn)
    mid_backward(dx)
    while sending:
        finish_scatter(dx)
    others = {l: {} for l in range(DEPTH)}
    for layer, names, send_sems, recv_sems, sums, lands, plan, tag in sharing:
        sums, lands = _exchange_wait(send_sems, recv_sems, sums, lands, plan, dx, name=f"share_wait_{tag}")
        reduced[layer].update(zip(names, sums))
        others[layer].update(zip(names, lands))
    last = [reduced[0][n] for n in MATRICES]
    others[0].update(zip(MATRICES, _send_to_sibling(last, name="grads_share_halves")))
    south = lax.axis_index("c") == 0
    grad = {}
    for n in MATRICES:
        layers = []
        for l in range(DEPTH):
            a, b = reduced[l][n], others[l][n]
            layers += [jnp.where(south, a, b), jnp.where(south, b, a)]
        grad[n] = jnp.concatenate(layers, axis=0).reshape(w[n].shape)

    packed = _pack_small(grads)
    total = _sum_devices(_gather_all(packed, name="small_grads_gather"), name="small_grads_sum")
    small = _unpack_small(total, grads[0])
    for n in WEIGHTS:
        if n in MATRICES:
            continue
        g = jnp.stack([s[n] for s in small])
        if n in ("gdn_a_log", "gdn_dt_bias"):
            g = g[:, 0, :GDN_HEADS]
        elif n in SMALL_SHARDED:
            width = w[n].shape[-1]
            g = lax.dynamic_slice_in_dim(g, chip * width, width, axis=2)
        grad[n] = g.reshape(w[n].shape)

    delta, new_m, new_v = {}, {}, {}
    for n in WEIGHTS:
        shape = w[n].shape
        two_d = (-1, shape[-1])
        d, nm, nv = _adamw(w[n].reshape(two_d), grad[n].reshape(two_d), m[n].reshape(two_d), v[n].reshape(two_d),
                           name="adamw_" + n)
        delta[n], new_m[n], new_v[n] = d.reshape(shape), nm.reshape(shape), nv.reshape(shape)

    loss = lax.psum(loss_row[0, 0], ("x", "y", "c"))
    return (loss, dx.reshape(x.shape), *[grad[n] for n in WEIGHTS], *[delta[n] for n in WEIGHTS],
            *[new_m[n] for n in WEIGHTS], *[new_v[n] for n in WEIGHTS])
```

```python
import math

import jax
import jax.numpy as jnp
from jax import lax
from jax.experimental import pallas as pl
from jax.experimental.pallas import tpu as pltpu

F32 = jnp.float32
BF16 = jnp.bfloat16
BS = pl.BlockSpec
SDS = jax.ShapeDtypeStruct
PRECISE = lax.Precision.HIGH

D_MODEL = 1024
DEPTH = 4
EPS = 1e-6
ATTN_HEADS = 4
ATTN_HEAD_DIM = 64
ATTN_WIDTH = 256
ROPE_THETA = 500000.0
ROPE_DIM = 16
CONV_WIDTH = 256
CONV_K = 3
GDN_HEADS = 4
GDN_HEAD_DIM = 128
GDN_WIDTH = 512
GDN_CONV_K = 4
GDN_CHUNK = 64
IN_WIDTH = 3592
XATTN_HEADS = 4
XATTN_HEAD_DIM = 256
FFN_HIDDEN = 2816
ADAM_LR = 0.001
ADAM_B1 = 0.9
ADAM_B2 = 0.999
ADAM_EPS = 1e-08
ADAM_WD = 0.01
ADAM_STEP = 10

IN_PAD = 3840
COL_GDN = 1536
COL_GATE = 3072
COL_AB = 3584

VMEM_LIMIT_V7X = 56 * 1024 * 1024
LANE = 128


def _cp(**kw):
    return pltpu.CompilerParams(vmem_limit_bytes=VMEM_LIMIT_V7X, **kw)


def _tile(n, cap):
    if n <= cap:
        return n
    best = None
    for t in range(LANE, cap + 1, LANE):
        if n % t == 0:
            best = t
    assert best is not None, (n, cap)
    return best


def _dot(a, b, ca, cb, precise=False):
    dims = (((ca,), (cb,)), ((), ()))
    if precise:
        return lax.dot_general(a.astype(F32), b.astype(F32), dims, precision=PRECISE,
                               preferred_element_type=F32)
    return lax.dot_general(a.astype(BF16), b.astype(BF16), dims, preferred_element_type=F32)


def _sigmoid(x):
    return 1.0 / (1.0 + jnp.exp(-x))


MM_ROWS = 1408
MM_BLOCK_BYTES = 6 * 1024 * 1024
MM_A_BYTES = 8 * 1024 * 1024


def _mm_tn(width, k, itemsize):
    if k * width * itemsize <= MM_BLOCK_BYTES:
        return width
    return _tile(width, max(LANE, min(1024, MM_BLOCK_BYTES // (k * itemsize) // LANE * LANE)))


def _mm(a, b, *, ta=False, tb=False, out_dtype=F32, b_shards=False, out_shards=False, halves=False, name):
    if halves and tb:
        m, k = a.shape[1], 2 * a.shape[2]
    else:
        m, k = (a.shape[1], a.shape[0]) if ta else a.shape
    tm = _tile(m, MM_ROWS)
    ca = 0 if ta else 1

    if b_shards and tb:
        ns, n, c = b.shape
        assert k == ns * c and not ta
        tn = _tile(n, max(LANE, min(1024, MM_BLOCK_BYTES // (k * b.dtype.itemsize) // LANE * LANE)))

        def a_block(a_ref, s):
            if halves:
                per = ns // 2
                return a_ref[s // per, :, (s % per) * c:(s % per + 1) * c]
            return a_ref[:, s * c:(s + 1) * c]

        def body(a_ref, b_ref, o_ref):
            acc = _dot(a_block(a_ref, 0), b_ref[0], 1, 1)
            for s in range(1, ns):
                acc = acc + _dot(a_block(a_ref, s), b_ref[s], 1, 1)
            o_ref[...] = acc.astype(out_dtype)

        b_spec = BS((ns, tn, c), lambda i, j: (0, j, 0))
    else:
        if b_shards:
            ns, kb, c = b.shape
            n = ns * c
            tn = _mm_tn(c, k, b.dtype.itemsize)
            nb = c // tn
            b_spec = BS((None, k, tn), lambda i, j: (j // nb, 0, j % nb))
        elif halves:
            kb, n = b.shape[1], 2 * b.shape[2]
            c = n // 4
            tn = _mm_tn(c, k, b.dtype.itemsize)
            nb = c // tn
            b_spec = BS((None, k, tn), lambda i, j: (j // (2 * nb), 0, j % (2 * nb)))
        else:
            kb, n = (b.shape[1], b.shape[0]) if tb else b.shape
            c = n // 4 if out_shards else n
            tn = _mm_tn(c, k, b.dtype.itemsize)
            nb = c // tn
            b_spec = BS((tn, k), lambda i, j: (j, 0)) if tb else BS((k, tn), lambda i, j: (0, j))
        assert kb == k
        cb = 1 if tb else 0

        def body(a_ref, b_ref, o_ref):
            o_ref[...] = _dot(a_ref[...], b_ref[...], ca, cb).astype(out_dtype)

    out_bytes = jnp.dtype(out_dtype).itemsize
    while (tm > 256 and tm % 256 == 0 and
           (tm * k * a.dtype.itemsize > MM_A_BYTES or tm * tn * out_bytes > MM_BLOCK_BYTES)):
        tm //= 2
    if halves and tb:
        a_spec = BS((2, tm, k // 2), lambda i, j: (0, i, 0))
    else:
        a_spec = BS((k, tm), lambda i, j: (0, i)) if ta else BS((tm, k), lambda i, j: (i, 0))
    if out_shards:
        out_spec = BS((None, tm, tn), lambda i, j: (j // nb, i, j % nb))
        out_shape = SDS((4, m, n // 4), out_dtype)
    elif halves and not (ta or tb):
        out_spec = BS((None, tm, tn), lambda i, j: (j // (2 * nb), i, j % (2 * nb)))
        out_shape = SDS((2, m, n // 2), out_dtype)
    else:
        out_spec = BS((tm, tn), lambda i, j: (i, j))
        out_shape = SDS((m, n), out_dtype)
    return pl.pallas_call(
        body, grid=(m // tm, n // tn), in_specs=[a_spec, b_spec], out_specs=out_spec, out_shape=out_shape,
        compiler_params=_cp(), name=name)(a, b)


def _rmsnorm(x, w, *, name):
    r, d = x.shape
    tr = _tile(r, 512)

    def body(x_ref, w_ref, o_ref):
        xv = x_ref[...]
        rs = lax.rsqrt(jnp.mean(xv * xv, axis=-1, keepdims=True) + EPS)
        o_ref[...] = (xv * rs * w_ref[...]).astype(BF16)

    return pl.pallas_call(
        body, grid=(r // tr,), in_specs=[BS((tr, d), lambda i: (i, 0)), BS((1, d), lambda i: (0, 0))],
        out_specs=BS((tr, d), lambda i: (i, 0)), out_shape=SDS((r, d), BF16),
        compiler_params=_cp(), name=name)(x, w)


def _resnorm(h, m, w, *, name):
    r, d = h.shape
    tr = _tile(r, 512)

    def body(h_ref, m_ref, w_ref, o_ref):
        mv = m_ref[...]
        rs = lax.rsqrt(jnp.mean(mv * mv, axis=-1, keepdims=True) + EPS)
        o_ref[...] = h_ref[...] + mv * rs * w_ref[...]

    row = BS((tr, d), lambda i: (i, 0))
    return pl.pallas_call(
        body, grid=(r // tr,), in_specs=[row, row, BS((1, d), lambda i: (0, 0))],
        out_specs=row, out_shape=SDS((r, d), F32), compiler_params=_cp(), name=name)(h, m, w)


def _rmsnorm_bwd(x, w, dy, res=None, *, name):
    r, d = x.shape
    tr = _tile(r, 512)
    has_res = res is not None

    def body(*refs):
        if has_res:
            x_ref, w_ref, dy_ref, res_ref, dx_ref, dw_ref = refs
        else:
            x_ref, w_ref, dy_ref, dx_ref, dw_ref = refs
        xv = x_ref[...]
        dyv = dy_ref[...].astype(F32)
        rs = lax.rsqrt(jnp.mean(xv * xv, axis=-1, keepdims=True) + EPS)
        nv = xv * rs
        dyw = dyv * w_ref[...]
        dx = rs * (dyw - nv * jnp.mean(dyw * nv, axis=-1, keepdims=True))
        if has_res:
            dx = dx + res_ref[...]
        dx_ref[...] = dx

        @pl.when(pl.program_id(0) == 0)
        def _():
            dw_ref[...] = jnp.zeros_like(dw_ref)

        dw_ref[...] += jnp.sum(dyv * nv, axis=0, keepdims=True)

    row = BS((tr, d), lambda i: (i, 0))
    vec = BS((1, d), lambda i: (0, 0))
    ins = [x, w, dy] + ([res] if has_res else [])
    return pl.pallas_call(
        body, grid=(r // tr,), in_specs=[row, vec, row] + ([row] if has_res else []),
        out_specs=[row, vec], out_shape=[SDS((r, d), F32), SDS((1, d), F32)],
        compiler_params=_cp(), name=name)(*ins)


def _gate_up_swiglu(a, w, *, name):
    m, k = a.shape
    ns, _, c = w.shape
    per = ns // 2
    tm = _tile(m, 512)

    def body(a_ref, wg_ref, wu_ref, gu_ref, act_ref):
        av = a_ref[...]
        g = _dot(av, wg_ref[...], 1, 0)
        u = _dot(av, wu_ref[...], 1, 0)
        gu_ref[0] = g
        gu_ref[1] = u
        act_ref[...] = (g * _sigmoid(g) * u).astype(BF16)

    return pl.pallas_call(
        body, grid=(m // tm, per),
        in_specs=[BS((tm, k), lambda i, j: (i, 0)), BS((None, k, c), lambda i, j: (j, 0, 0)),
                  BS((None, k, c), lambda i, j: (per + j, 0, 0))],
        out_specs=[BS((2, tm, c), lambda i, j: (0, i, j)), BS((tm, c), lambda i, j: (i, j))],
        out_shape=[SDS((2, m, per * c), F32), SDS((m, per * c), BF16)],
        compiler_params=_cp(), name=name)(a, w, w)


def _swiglu_bwd(gu, dact, *, name):
    _, r, hid = gu.shape
    tr, tc = _tile(r, 512), _tile(hid, 1408)

    def body(gu_ref, d_ref, o_ref):
        g = gu_ref[0]
        da = d_ref[...]
        sg = _sigmoid(g)
        o_ref[0] = (da * gu_ref[1] * sg * (1.0 + g * (1.0 - sg))).astype(BF16)
        o_ref[1] = (da * g * sg).astype(BF16)

    blk = BS((2, tr, tc), lambda i, j: (0, i, j))
    return pl.pallas_call(
        body, grid=(r // tr, hid // tc), in_specs=[blk, BS((tr, tc), lambda i, j: (i, j))],
        out_specs=blk, out_shape=SDS((2, r, hid), BF16), compiler_params=_cp(), name=name)(gu, dact)


def _loss_grad(h, target, *, name):
    r, d = h.shape
    tr = _tile(r, 512)

    def body(h_ref, t_ref, l_ref, g_ref):
        e = h_ref[...] - t_ref[...]
        g_ref[...] = e * (1.0 / d)

        @pl.when(pl.program_id(0) == 0)
        def _():
            l_ref[...] = jnp.zeros_like(l_ref)

        l_ref[...] += jnp.full((1, LANE), 0.5 / d, F32) * jnp.sum(e * e)

    row = BS((tr, d), lambda i: (i, 0))
    return pl.pallas_call(
        body, grid=(r // tr,), in_specs=[row, row],
        out_specs=[BS((1, LANE), lambda i: (0, 0)), row],
        out_shape=[SDS((1, LANE), F32), SDS((r, d), F32)], compiler_params=_cp(), name=name)(h, target)


def _adamw(w, g, m, v, *, name):
    r, c = w.shape
    tr = r if r <= 512 else _rows_tile(r)
    bc1 = 1.0 - ADAM_B1 ** ADAM_STEP
    bc2 = 1.0 - ADAM_B2 ** ADAM_STEP

    def body(w_ref, g_ref, m_ref, v_ref, d_ref, nm_ref, nv_ref):
        gv = g_ref[...]
        nm = ADAM_B1 * m_ref[...] + (1.0 - ADAM_B1) * gv
        nv = ADAM_B2 * v_ref[...] + (1.0 - ADAM_B2) * (gv * gv)
        d_ref[...] = -ADAM_LR * ((nm / bc1) / (jnp.sqrt(nv / bc2) + ADAM_EPS) + ADAM_WD * w_ref[...])
        nm_ref[...] = nm
        nv_ref[...] = nv

    blk = BS((tr, c), lambda i: (i, 0))
    return pl.pallas_call(
        body, grid=(r // tr,), in_specs=[blk] * 4, out_specs=[blk] * 3,
        out_shape=[SDS((r, c), F32)] * 3, compiler_params=_cp(), name=name)(w, g, m, v)


def _rows_tile(r):
    for t in (512, 256, 128, 64, 32, 16, 8):
        if r % t == 0:
            return t
    return r


def _rope_tables(pos, *, name):
    s = pos.shape[0]
    half = ROPE_DIM // 2

    def body(p_ref, c_ref, a_ref, b_ref):
        lane = lax.broadcasted_iota(jnp.int32, (s, ATTN_WIDTH), 1) & (ATTN_HEAD_DIM - 1)
        fi = (lane & (half - 1)).astype(F32)
        inv_freq = jnp.exp(fi * (-2.0 * math.log(ROPE_THETA) / ROPE_DIM))
        ang = p_ref[...].astype(F32) * inv_freq
        cs, sn = jnp.cos(ang), jnp.sin(ang)
        c_ref[...] = jnp.where(lane < ROPE_DIM, cs, 1.0)
        a_ref[...] = jnp.where(lane < half, -sn, 0.0)
        b_ref[...] = jnp.where((lane >= half) & (lane < ROPE_DIM), sn, 0.0)

    full = BS((s, ATTN_WIDTH), lambda i: (0, 0))
    return pl.pallas_call(
        body, grid=(1,), in_specs=[BS((s, 1), lambda i: (0, 0))], out_specs=[full] * 3,
        out_shape=[SDS((s, ATTN_WIDTH), F32)] * 3, compiler_params=_cp(), name=name)(pos)


def _rot(x, c, a, b):
    w = x.shape[1]
    return x * c + pltpu.roll(x, w - ROPE_DIM // 2, 1) * a + pltpu.roll(x, ROPE_DIM // 2, 1) * b


def _rot_t(dy, c, a, b):
    w = dy.shape[1]
    return dy * c + pltpu.roll(dy * a, ROPE_DIM // 2, 1) + pltpu.roll(dy * b, w - ROPE_DIM // 2, 1)


def _attn_count(q0, tq, s):
    dist = (lax.broadcasted_iota(jnp.int32, (tq, s), 0) + q0) - lax.broadcasted_iota(jnp.int32, (tq, s), 1)
    cnt = ((dist <= 128).astype(F32) + (((dist & 3) == 0) & (dist <= 512)).astype(F32)
           + ((dist & 15) == 0).astype(F32))
    return jnp.where(dist >= 0, cnt, 0.0)


ATTN_TQ = 512


def _attn_specs(s, tq):
    def qblk(col):
        return BS((tq, ATTN_WIDTH), lambda i: (i, col))

    def full(col):
        return BS((s, ATTN_WIDTH), lambda i: (0, col))

    return qblk, full


ATTN_TK = 512


def _attn_chunk(i, c, tq, k_ref, v_ref, ck, ak, bk):
    ks = pl.ds(pl.multiple_of(c * ATTN_TK, ATTN_TK), ATTN_TK)
    k = _rot(k_ref[ks, :], ck[ks, :], ak[ks, :], bk[ks, :]).astype(BF16)
    v = v_ref[ks, :].astype(BF16)
    cnt = _attn_count(i * tq - c * ATTN_TK, tq, ATTN_TK)
    return ks, k, v, cnt


def _attn_flash_fwd(proj, tabs, *, name):
    s = proj.shape[0]
    tq = ATTN_TQ
    qblk, full = _attn_specs(s, tq)
    scale = ATTN_HEAD_DIM ** -0.5
    nh = ATTN_HEADS

    def body(q_ref, k_ref, v_ref, cq, aq, bq, ck, ak, bk, o_ref, lse_ref):
        i = pl.program_id(0)
        q = _rot(q_ref[...], cq[...], aq[...], bq[...]) * scale
        head = lax.broadcasted_iota(jnp.int32, (1, ATTN_WIDTH), 1) >> 6
        hms = [(head == h).astype(F32) for h in range(nh)]
        qms = [(q * hm).astype(BF16) for hm in hms]

        def step(c, carry):
            ms, ls, acc = carry
            _, k, v, cnt = _attn_chunk(i, c, tq, k_ref, v_ref, ck, ak, bk)
            valid = cnt > 0.0
            new_ms, new_ls = [], []
            scale_acc = jnp.zeros((tq, ATTN_WIDTH), F32)
            add = jnp.zeros((tq, ATTN_WIDTH), F32)
            for h in range(nh):
                sc = _dot(qms[h], k, 1, 1)
                m_new = jnp.maximum(ms[h], jnp.max(jnp.where(valid, sc, -1e30), axis=-1, keepdims=True))
                alpha = jnp.exp(ms[h] - m_new)
                p = cnt * jnp.exp(jnp.minimum(sc - m_new, 0.0))
                new_ms.append(m_new)
                new_ls.append(alpha * ls[h] + jnp.sum(p, axis=-1, keepdims=True))
                scale_acc = scale_acc + alpha * hms[h]
                add = add + _dot(p, v, 1, 0) * hms[h]
            return new_ms, new_ls, acc * scale_acc + add

        init = ([jnp.full((tq, 1), -1e30, F32)] * nh, [jnp.zeros((tq, 1), F32)] * nh,
                jnp.zeros((tq, ATTN_WIDTH), F32))
        ms, ls, acc = lax.fori_loop(0, i // (ATTN_TK // tq) + 1, step, init)
        inv = jnp.zeros((tq, ATTN_WIDTH), F32)
        lane = lax.broadcasted_iota(jnp.int32, (tq, LANE), 1)
        lse = jnp.zeros((tq, LANE), F32)
        for h in range(nh):
            inv = inv + (1.0 / ls[h]) * hms[h]
            lse = jnp.where(lane == h, ms[h] + jnp.log(ls[h]), lse)
        o_ref[...] = (acc * inv).astype(BF16)
        lse_ref[...] = lse

    return pl.pallas_call(
        body, grid=(s // tq,),
        in_specs=[qblk(0), full(1), full(2), qblk(0), qblk(0), qblk(0), full(0), full(0), full(0)],
        out_specs=[BS((tq, ATTN_WIDTH), lambda i: (i, 0)), BS((tq, LANE), lambda i: (i, 0))],
        out_shape=[SDS((s, ATTN_WIDTH), BF16), SDS((s, LANE), F32)],
        compiler_params=_cp(), name=name)(proj, proj, proj, *tabs, *tabs)


def _attn_flash_bwd(proj, tabs, cat, lse, dcat, *, name):
    s = proj.shape[0]
    tq = ATTN_TQ
    nq = s // tq
    qblk, full = _attn_specs(s, tq)
    scale = ATTN_HEAD_DIM ** -0.5
    nh = ATTN_HEADS

    def body(q_ref, k_ref, v_ref, cq, aq, bq, ck, ak, bk, y_ref, lse_ref, dy_ref,
             dq_ref, dk_ref, dv_ref, dk_acc, dv_acc):
        i = pl.program_id(0)

        @pl.when(i == 0)
        def _():
            dk_acc[...] = jnp.zeros_like(dk_acc)
            dv_acc[...] = jnp.zeros_like(dv_acc)

        q = _rot(q_ref[...], cq[...], aq[...], bq[...]) * scale
        dy = dy_ref[...].astype(F32)
        prod = dy * y_ref[...].astype(F32)
        lse_all = lse_ref[...]
        head = lax.broadcasted_iota(jnp.int32, (1, ATTN_WIDTH), 1) >> 6
        hms = [(head == h).astype(F32) for h in range(nh)]
        qms = [(q * hm).astype(BF16) for hm in hms]
        dyms = [(dy * hm).astype(BF16) for hm in hms]
        deltas = [jnp.sum(prod * hm, axis=-1, keepdims=True) for hm in hms]
        lses = [lse_all[:, h:h + 1] for h in range(nh)]

        def step(c, dq):
            ks, k, v, cnt = _attn_chunk(i, c, tq, k_ref, v_ref, ck, ak, bk)
            dk_c = jnp.zeros((ATTN_TK, ATTN_WIDTH), F32)
            dv_c = jnp.zeros((ATTN_TK, ATTN_WIDTH), F32)
            for h in range(nh):
                sc = _dot(qms[h], k, 1, 1)
                p = cnt * jnp.exp(jnp.minimum(sc - lses[h], 0.0))
                dp = _dot(dyms[h], v, 1, 1)
                ds = p * (dp - deltas[h])
                dq = dq + _dot(ds, k, 1, 0) * hms[h]
                dk_c = dk_c + _dot(ds, qms[h], 0, 0)
                dv_c = dv_c + _dot(p, dyms[h], 0, 0)
            dk_acc[ks, :] += dk_c
            dv_acc[ks, :] += dv_c
            return dq

        dq = lax.fori_loop(0, i // (ATTN_TK // tq) + 1, step, jnp.zeros((tq, ATTN_WIDTH), F32))
        dq_ref[...] = _rot_t(dq * scale, cq[...], aq[...], bq[...]).astype(BF16)

        @pl.when(i == nq - 1)
        def _():
            dk_ref[...] = _rot_t(dk_acc[...], ck[...], ak[...], bk[...]).astype(BF16)
            dv_ref[...] = dv_acc[...].astype(BF16)

    whole = BS((s, ATTN_WIDTH), lambda i: (0, 0))
    return pl.pallas_call(
        body, grid=(nq,),
        in_specs=[qblk(0), full(1), full(2), qblk(0), qblk(0), qblk(0), full(0), full(0), full(0),
                  qblk(0), BS((tq, LANE), lambda i: (i, 0)), qblk(0)],
        out_specs=[BS((tq, ATTN_WIDTH), lambda i: (i, 0)), whole, whole],
        out_shape=[SDS((s, ATTN_WIDTH), BF16)] * 3,
        scratch_shapes=[pltpu.VMEM((s, ATTN_WIDTH), F32), pltpu.VMEM((s, ATTN_WIDTH), F32)],
        compiler_params=_cp(), name=name)(proj, proj, proj, *tabs, *tabs, cat, lse, dcat)


def _shift_down(x, n):
    if n == 0:
        return x
    rows = lax.broadcasted_iota(jnp.int32, x.shape, 0)
    return jnp.where(rows >= n, pltpu.roll(x, n, 0), 0.0)


def _shift_up(x, n):
    if n == 0:
        return x
    t = x.shape[0]
    rows = lax.broadcasted_iota(jnp.int32, x.shape, 0)
    return jnp.where(rows < t - n, pltpu.roll(x, t - n, 0), 0.0)


def _conv_fwd(z, w, kk):
    y = z * w[kk - 1:kk, :]
    for j in range(kk - 1):
        y = y + _shift_down(z, kk - 1 - j) * w[j:j + 1, :]
    return y


def _conv_bwd(z, dy, w, kk):
    dz = dy * w[kk - 1:kk, :]
    dws = []
    for j in range(kk - 1):
        dz = dz + _shift_up(dy, kk - 1 - j) * w[j:j + 1, :]
        dws.append(jnp.sum(dy * _shift_down(z, kk - 1 - j), axis=0, keepdims=True))
    dws.append(jnp.sum(dy * z, axis=0, keepdims=True))
    return dz, jnp.concatenate(dws, axis=0)


def _sconv_fwd(proj, w, *, name):
    s = proj.shape[0]

    def body(b_ref, c_ref, x_ref, w_ref, o_ref):
        y = _conv_fwd(c_ref[...] * x_ref[...], w_ref[...], CONV_K)
        o_ref[...] = (b_ref[...] * y).astype(BF16)

    def col(j):
        return BS((s, LANE), lambda i: (0, j + i))

    return pl.pallas_call(
        body, grid=(CONV_WIDTH // LANE,), in_specs=[col(6), col(8), col(10), BS((CONV_K, LANE), lambda i: (0, i))],
        out_specs=BS((s, LANE), lambda i: (0, i)), out_shape=SDS((s, CONV_WIDTH), BF16),
        compiler_params=_cp(), name=name)(proj, proj, proj, w)


def _sconv_bwd(proj, w, dcat, *, name):
    s = proj.shape[0]

    def body(b_ref, c_ref, x_ref, w_ref, dy_ref, db_ref, dc_ref, dx_ref, dw_ref):
        cv, xv, wv = c_ref[...], x_ref[...], w_ref[...]
        dy = dy_ref[...].astype(F32)
        z = cv * xv
        db_ref[...] = (dy * _conv_fwd(z, wv, CONV_K)).astype(BF16)
        dz, dw = _conv_bwd(z, dy * b_ref[...], wv, CONV_K)
        dc_ref[...] = (dz * xv).astype(BF16)
        dx_ref[...] = (dz * cv).astype(BF16)
        dw_ref[...] = dw

    def col(j):
        return BS((s, LANE), lambda i: (0, j + i))

    out = BS((s, LANE), lambda i: (0, i))
    wspec = BS((CONV_K, LANE), lambda i: (0, i))
    return pl.pallas_call(
        body, grid=(CONV_WIDTH // LANE,), in_specs=[col(6), col(8), col(10), wspec, col(2)],
        out_specs=[out, out, out, wspec],
        out_shape=[SDS((s, CONV_WIDTH), BF16)] * 3 + [SDS((CONV_K, CONV_WIDTH), F32)],
        compiler_params=_cp(), name=name)(proj, proj, proj, w, dcat)


def _l2n(y, scale):
    r = lax.rsqrt(jnp.sum(y * y, axis=-1, keepdims=True) + EPS)
    return y * r * scale, r


def _gdn_pre_fwd(proj, w, *, name):
    s = proj.shape[0]
    nh = GDN_HEADS

    def body(x_ref, w_ref, o_ref):
        j = pl.program_id(0)
        c = _conv_fwd(x_ref[...], w_ref[...], GDN_CONV_K)
        y = c * _sigmoid(c)
        scale = jnp.where(j < nh, GDN_HEAD_DIM ** -0.5, 1.0)
        n, _ = _l2n(y, scale)
        o_ref[...] = jnp.where(j < 2 * nh, n, y)

    return pl.pallas_call(
        body, grid=(3 * nh,),
        in_specs=[BS((s, LANE), lambda j: (0, COL_GDN // LANE + j)), BS((GDN_CONV_K, LANE), lambda j: (0, j))],
        out_specs=BS((s, LANE), lambda j: (0, j)), out_shape=SDS((s, 3 * GDN_WIDTH), F32),
        compiler_params=_cp(), name=name)(proj, w)


def _gdn_pre_bwd(proj, w, dqkv, *, name):
    s = proj.shape[0]
    nh = GDN_HEADS

    def body(x_ref, w_ref, d_ref, dx_ref, dw_ref):
        j = pl.program_id(0)
        xv, wv, dn = x_ref[...], w_ref[...], d_ref[...]
        c = _conv_fwd(xv, wv, GDN_CONV_K)
        sg = _sigmoid(c)
        y = c * sg
        scale = jnp.where(j < nh, GDN_HEAD_DIM ** -0.5, 1.0)
        n, r = _l2n(y, 1.0)
        dns = dn * scale
        dy_norm = r * (dns - n * jnp.sum(dns * n, axis=-1, keepdims=True))
        dy = jnp.where(j < 2 * nh, dy_norm, dn)
        dc = dy * sg * (1.0 + c * (1.0 - sg))
        dx, dw = _conv_bwd(xv, dc, wv, GDN_CONV_K)
        dx_ref[...] = dx.astype(BF16)
        dw_ref[...] = dw

    wspec = BS((GDN_CONV_K, LANE), lambda j: (0, j))
    blk = BS((s, LANE), lambda j: (0, j))
    return pl.pallas_call(
        body, grid=(3 * nh,),
        in_specs=[BS((s, LANE), lambda j: (0, COL_GDN // LANE + j)), wspec, blk],
        out_specs=[blk, wspec], out_shape=[SDS((s, 3 * GDN_WIDTH), BF16), SDS((GDN_CONV_K, 3 * GDN_WIDTH), F32)],
        compiler_params=_cp(), name=name)(proj, w, dqkv)


def _softplus(x):
    return jnp.maximum(x, 0.0) + jnp.log(1.0 + jnp.exp(-jnp.abs(x)))


def _gdn_gates_fwd(proj, a_log, dt_bias, *, name):
    s = proj.shape[0]

    def body(x_ref, al_ref, dt_ref, o_ref):
        xv = x_ref[...]
        lane = lax.broadcasted_iota(jnp.int32, xv.shape, 1)
        g = -jnp.exp(al_ref[...]) * _softplus(xv + dt_ref[...])
        o_ref[...] = jnp.where(lane < GDN_HEADS, g, jnp.where(lane < 2 * GDN_HEADS, _sigmoid(xv), 0.0))

    vec = BS((1, LANE), lambda i: (0, 0))
    return pl.pallas_call(
        body, grid=(1,), in_specs=[BS((s, LANE), lambda i: (0, COL_AB // LANE)), vec, vec],
        out_specs=BS((s, LANE), lambda i: (0, 0)), out_shape=SDS((s, LANE), F32),
        compiler_params=_cp(), name=name)(proj, a_log, dt_bias)


def _gdn_gates_bwd(proj, a_log, dt_bias, dgb, *, name):
    s = proj.shape[0]

    def body(x_ref, al_ref, dt_ref, d_ref, dx_ref, dal_ref, ddt_ref):
        xv, dv = x_ref[...], d_ref[...]
        lane = lax.broadcasted_iota(jnp.int32, xv.shape, 1)
        is_g = lane < GDN_HEADS
        ea = -jnp.exp(al_ref[...])
        z = xv + dt_ref[...]
        da = jnp.where(is_g, dv * ea * _sigmoid(z), 0.0)
        beta = _sigmoid(xv)
        dx_ref[...] = jnp.where(is_g, da, jnp.where(lane < 2 * GDN_HEADS, dv * beta * (1.0 - beta), 0.0)).astype(BF16)
        dal_ref[...] = jnp.sum(jnp.where(is_g, dv * ea * _softplus(z), 0.0), axis=0, keepdims=True)
        ddt_ref[...] = jnp.sum(da, axis=0, keepdims=True)

    vec = BS((1, LANE), lambda i: (0, 0))
    blk = BS((s, LANE), lambda i: (0, 0))
    return pl.pallas_call(
        body, grid=(1,), in_specs=[BS((s, LANE), lambda i: (0, COL_AB // LANE)), vec, vec, blk],
        out_specs=[blk, vec, vec], out_shape=[SDS((s, LANE), BF16), SDS((1, LANE), F32), SDS((1, LANE), F32)],
        compiler_params=_cp(), name=name)(proj, a_log, dt_bias, dgb)


def _col_to_row(col, eye):
    return jnp.sum(jnp.where(eye, col, 0.0), axis=0, keepdims=True)


def _row_to_col(row, eye):
    return jnp.sum(jnp.where(eye, row, 0.0), axis=1, keepdims=True)


GDN_GROUP = 4
TRI_BLOCK_SHIFT = 4


def _gdn_masks(c):
    row = lax.broadcasted_iota(jnp.int32, (c, c), 0)
    col = lax.broadcasted_iota(jnp.int32, (c, c), 1)
    return dict(row=row, col=col, eye=row == col, low=row >= col, strict=row > col, upper=row <= col,
                on_diag=(row >> TRI_BLOCK_SHIFT) == (col >> TRI_BLOCK_SHIFT))


def _tri_inv(a_list, mk):
    eye_f = mk["eye"].astype(F32)
    ds = [jnp.where(mk["on_diag"], a, 0.0) for a in a_list]
    xs = [eye_f - d for d in ds]
    ps = ds
    for _ in range(3):
        ps = [_dot(p, p, 1, 0, precise=True) for p in ps]
        xs = [x + _dot(x, p, 1, 0, precise=True) for x, p in zip(xs, ps)]
    ms = [_dot(x, a - d, 1, 0, precise=True) for x, a, d in zip(xs, a_list, ds)]
    m2s = [_dot(m, m, 1, 0, precise=True) for m in ms]
    ys = [eye_f - m for m in ms]
    ys = [y + _dot(y, m2, 1, 0, precise=True) for y, m2 in zip(ys, m2s)]
    return [_dot(y, x, 1, 0, precise=True) for y, x in zip(ys, xs)]


def _gdn_pre(qs, ks, vs, gs, betas, mk, ts=None):
    c, hd = qs[0].shape
    eye, low = mk["eye"], mk["low"]
    g_rows = [_col_to_row(g, eye) for g in gs]
    d_cols = [jnp.sum(jnp.where(low, gr, 0.0), axis=1, keepdims=True) for gr in g_rows]
    d_rows = [jnp.sum(jnp.where(mk["upper"], g, 0.0), axis=0, keepdims=True) for g in gs]
    rels = [jnp.where(low, jnp.exp(jnp.minimum(dc - dr, 0.0)), 0.0) for dc, dr in zip(d_cols, d_rows)]
    d_lasts = [dc[c - 1:c, :] for dc in d_cols]
    es = [jnp.exp(dc) for dc in d_cols]
    fs = [jnp.exp(dl - dc) for dl, dc in zip(d_lasts, d_cols)]
    cds = [jnp.exp(dl) for dl in d_lasts]
    kbs = [k * b for k, b in zip(ks, betas)]
    kbqs = [jnp.concatenate([kb, q], axis=0) for kb, q in zip(kbs, qs)]
    kqk = [_dot(kbq, k, 1, 1) for kbq, k in zip(kbqs, ks)]
    kks = [x[:c, :] for x in kqk]
    qks = [x[c:, :] for x in kqk]
    if ts is None:
        ts = _tri_inv([jnp.where(mk["strict"], kk * rel, 0.0) for kk, rel in zip(kks, rels)], mk)
    vbs = [v * b for v, b in zip(vs, betas)]
    kbes = [kb * e for kb, e in zip(kbs, es)]
    uws = [_dot(t, jnp.concatenate([vb, kbe], axis=1), 1, 0) for t, vb, kbe in zip(ts, vbs, kbes)]
    out = []
    for i in range(len(qs)):
        out.append(dict(rel=rels[i], e=es[i], f=fs[i], cd=cds[i], kb=kbs[i], kbq=kbqs[i], kk=kks[i], qk=qks[i],
                        t=ts[i], u=uws[i][:, :hd], w=uws[i][:, hd:], uw=uws[i], attn=qks[i] * rels[i],
                        qd=qs[i] * es[i], kd=ks[i] * fs[i]))
    return out


def _gdn_apply(pres, sts, leaving=True):
    c = pres[0]["u"].shape[0]
    wqs = [_dot(jnp.concatenate([p["w"], p["qd"]], axis=0), st, 1, 0) for p, st in zip(pres, sts)]
    vns = [p["u"] - x[:c, :] for p, x in zip(pres, wqs)]
    os_ = [x[c:, :] + _dot(p["attn"], vn, 1, 0) for p, x, vn in zip(pres, wqs, vns)]
    if not leaving:
        return vns, os_, None
    new = [p["cd"] * st + _dot(p["kd"], vn, 0, 0) for p, st, vn in zip(pres, sts, vns)]
    return vns, os_, new


def _gdn_bwd_rest(qs, ks, vs, betas, sts, pres, vns, dos, dvns, dsts, mk):
    c, hd = qs[0].shape
    eye = mk["eye"]
    n = range(len(qs))
    dkds = [_dot(vns[i], dsts[i], 1, 1) for i in n]
    dcds = [jnp.sum(sts[i] * dsts[i]) for i in n]
    dattns = [jnp.where(mk["low"], _dot(dos[i], vns[i], 1, 1), 0.0) for i in n]
    dqdws = [_dot(jnp.concatenate([dos[i], -dvns[i]], axis=0), sts[i], 1, 1) for i in n]
    dqds = [x[:c, :] for x in dqdws]
    dws = [x[c:, :] for x in dqdws]
    dvks = [_dot(pres[i]["t"], jnp.concatenate([dvns[i], dws[i]], axis=1), 0, 0) for i in n]
    das = [jnp.where(mk["strict"], -_dot(dvks[i], pres[i]["uw"], 1, 1), 0.0) for i in n]
    dkqs = [jnp.concatenate([das[i] * pres[i]["rel"], dattns[i] * pres[i]["rel"]], axis=0) for i in n]
    dkbdq = [_dot(dkqs[i], ks[i], 1, 0) for i in n]
    dk0 = [_dot(dkqs[i], pres[i]["kbq"], 0, 0) for i in n]
    out = []
    rows1 = lax.broadcasted_iota(jnp.int32, (c, 1), 0)
    for i in n:
        p = pres[i]
        dvb, dkbe = dvks[i][:, :hd], dvks[i][:, hd:]
        grel = (das[i] * p["kk"] + dattns[i] * p["qk"]) * p["rel"]
        dkb = dkbdq[i][:c, :] + dkbe * p["e"]
        dk = dk0[i] + dkds[i] * p["f"] + dkb * betas[i]
        dq = dkbdq[i][c:, :] + dqds[i] * p["e"]
        dv = dvb * betas[i]
        dbeta = jnp.sum(dkb * ks[i], axis=1, keepdims=True) + jnp.sum(dvb * vs[i], axis=1, keepdims=True)
        de = jnp.sum(dqds[i] * qs[i], axis=1, keepdims=True) + jnp.sum(dkbe * p["kb"], axis=1, keepdims=True)
        dff = jnp.sum(dkds[i] * ks[i], axis=1, keepdims=True) * p["f"]
        dd = (de * p["e"] - dff + jnp.sum(grel, axis=1, keepdims=True)
              - _row_to_col(jnp.sum(grel, axis=0, keepdims=True), eye))
        dd = dd + jnp.where(rows1 == c - 1, jnp.sum(dff) + dcds[i] * p["cd"], 0.0)
        dg = jnp.sum(jnp.where(mk["upper"], _col_to_row(dd, eye), 0.0), axis=1, keepdims=True)
        out.append((dq, dk, dv, dg, dbeta))
    return out


def _gdn_specs(c):
    def qkv(j):
        return BS((c, GDN_WIDTH), lambda n: (n, j))

    return qkv


def _gdn_core_fwd(qkv, gbeta, proj, norm_w, *, name):
    s = qkv.shape[0]
    c, nh, hd, grp = GDN_CHUNK, GDN_HEADS, GDN_HEAD_DIM, GDN_GROUP
    n_chunks = s // c
    blk = _gdn_specs(grp * c)
    inst = [(sub, h) for sub in range(grp) for h in range(nh)]

    def body(q_ref, k_ref, v_ref, gb_ref, gate_ref, nw_ref, y_ref, st_ref, t_ref, state):
        @pl.when(pl.program_id(0) == 0)
        def _():
            state[...] = jnp.zeros_like(state)

        mk = _gdn_masks(c)
        rows = [slice(sub * c, (sub + 1) * c) for sub in range(grp)]
        lanes = [slice(h * hd, (h + 1) * hd) for h in range(nh)]
        gbs = [gb_ref[r, :] for r in rows]
        pres = _gdn_pre([q_ref[rows[sub], lanes[h]] for sub, h in inst], [k_ref[rows[sub], lanes[h]] for sub, h in inst],
                        [v_ref[rows[sub], lanes[h]] for sub, h in inst], [gbs[sub][:, h:h + 1] for sub, h in inst],
                        [gbs[sub][:, nh + h:nh + h + 1] for sub, h in inst], mk)
        sts = [state[ls, :] for ls in lanes]
        outs = []
        for sub in range(grp):
            for h in range(nh):
                st_ref[pl.ds((sub * nh + h) * hd, hd), :] = sts[h]
            _, os_, sts = _gdn_apply(pres[sub * nh:(sub + 1) * nh], sts)
            outs += os_
        for h in range(nh):
            state[lanes[h], :] = sts[h]
        nw = nw_ref[...]
        for i, (sub, h) in enumerate(inst):
            t_ref[pl.ds(i * c, c), :] = pres[i]["t"]
            o = outs[i]
            gate = gate_ref[rows[sub], lanes[h]]
            rs = lax.rsqrt(jnp.mean(o * o, axis=-1, keepdims=True) + EPS)
            y_ref[rows[sub], lanes[h]] = (o * rs * nw * (gate * _sigmoid(gate))).astype(BF16)

    return pl.pallas_call(
        body, grid=(n_chunks // grp,),
        in_specs=[blk(0), blk(1), blk(2), BS((grp * c, LANE), lambda n: (n, 0)),
                  BS((grp * c, GDN_WIDTH), lambda n: (n, COL_GATE // GDN_WIDTH)), BS((1, hd), lambda n: (0, 0))],
        out_specs=[BS((grp * c, GDN_WIDTH), lambda n: (n, 0)), BS((grp * nh * hd, hd), lambda n: (n, 0)),
                   BS((grp * nh * c, c), lambda n: (n, 0))],
        out_shape=[SDS((s, GDN_WIDTH), BF16), SDS((n_chunks * nh * hd, hd), F32), SDS((n_chunks * nh * c, c), F32)],
        scratch_shapes=[pltpu.VMEM((nh * hd, hd), F32)],
        compiler_params=_cp(), name=name)(qkv, qkv, qkv, gbeta, proj, norm_w)


def _gdn_core_bwd(qkv, gbeta, proj, norm_w, states, tinv, dcat, *, name):
    s = qkv.shape[0]
    c, nh, hd, grp = GDN_CHUNK, GDN_HEADS, GDN_HEAD_DIM, GDN_GROUP
    n_chunks = s // c
    last = n_chunks // grp - 1
    inst = [(sub, h) for sub in range(grp) for h in range(nh)]

    def rev(j, w):
        return BS((grp * c, w), lambda n: (last - n, j))

    def body(q_ref, k_ref, v_ref, gb_ref, gate_ref, nw_ref, st_ref, t_ref, dy_ref,
             dqkv_ref, dgb_ref, dgate_ref, dnw_ref, dstate):
        @pl.when(pl.program_id(0) == 0)
        def _():
            dstate[...] = jnp.zeros_like(dstate)
            dnw_ref[...] = jnp.zeros_like(dnw_ref)

        mk = _gdn_masks(c)
        rows = [slice(sub * c, (sub + 1) * c) for sub in range(grp)]
        lanes = [slice(h * hd, (h + 1) * hd) for h in range(nh)]
        gbs = [gb_ref[r, :] for r in rows]
        qs = [q_ref[rows[sub], lanes[h]] for sub, h in inst]
        ks = [k_ref[rows[sub], lanes[h]] for sub, h in inst]
        vs = [v_ref[rows[sub], lanes[h]] for sub, h in inst]
        betas = [gbs[sub][:, nh + h:nh + h + 1] for sub, h in inst]
        sts = [st_ref[pl.ds(i * hd, hd), :] for i in range(len(inst))]
        pres = _gdn_pre(qs, ks, vs, [gbs[sub][:, h:h + 1] for sub, h in inst], betas, mk,
                        ts=[t_ref[pl.ds(i * c, c), :] for i in range(len(inst))])
        vns, outs, _ = _gdn_apply(pres, sts, leaving=False)

        nw = nw_ref[...]
        dnw = jnp.zeros((1, hd), F32)
        dos = []
        for i, (sub, h) in enumerate(inst):
            o = outs[i]
            gate = gate_ref[rows[sub], lanes[h]]
            dy = dy_ref[rows[sub], lanes[h]].astype(F32)
            sg = _sigmoid(gate)
            rs = lax.rsqrt(jnp.mean(o * o, axis=-1, keepdims=True) + EPS)
            nrm = o * rs
            dgate_ref[rows[sub], lanes[h]] = (dy * nrm * nw * sg * (1.0 + gate * (1.0 - sg))).astype(BF16)
            dnv = dy * (gate * sg)
            dnw = dnw + jnp.sum(dnv * nrm, axis=0, keepdims=True)
            dno = dnv * nw
            dos.append(rs * (dno - nrm * jnp.mean(dno * nrm, axis=-1, keepdims=True)))
        dnw_ref[...] += dnw

        from_o = [_dot(p["attn"], do, 0, 0) for p, do in zip(pres, dos)]
        to_st = [_dot(p["qd"], do, 0, 0) for p, do in zip(pres, dos)]
        dst = [dstate[ls, :] for ls in lanes]
        dsts = [None] * len(inst)
        dvns = [None] * len(inst)
        for sub in reversed(range(grp)):
            idx = [sub * nh + h for h in range(nh)]
            for h, i in enumerate(idx):
                dsts[i] = dst[h]
                dvns[i] = from_o[i] + _dot(pres[i]["kd"], dst[h], 1, 0)
            dst = [pres[i]["cd"] * dst[h] + to_st[i] - _dot(pres[i]["w"], dvns[i], 0, 0) for h, i in enumerate(idx)]
        for h in range(nh):
            dstate[lanes[h], :] = dst[h]

        grads = _gdn_bwd_rest(qs, ks, vs, betas, sts, pres, vns, dos, dvns, dsts, mk)
        lane = lax.broadcasted_iota(jnp.int32, (c, LANE), 1)
        dgb = [jnp.zeros((c, LANE), F32) for _ in range(grp)]
        for (sub, h), (dq, dk, dv, dg, dbeta) in zip(inst, grads):
            dqkv_ref[rows[sub], lanes[h]] = dq
            dqkv_ref[rows[sub], slice(GDN_WIDTH + h * hd, GDN_WIDTH + (h + 1) * hd)] = dk
            dqkv_ref[rows[sub], slice(2 * GDN_WIDTH + h * hd, 2 * GDN_WIDTH + (h + 1) * hd)] = dv
            dgb[sub] = jnp.where(lane == h, dg, jnp.where(lane == nh + h, dbeta, dgb[sub]))
        for sub in range(grp):
            dgb_ref[rows[sub], :] = dgb[sub]

    return pl.pallas_call(
        body, grid=(n_chunks // grp,),
        in_specs=[rev(0, GDN_WIDTH), rev(1, GDN_WIDTH), rev(2, GDN_WIDTH), rev(0, LANE),
                  rev(COL_GATE // GDN_WIDTH, GDN_WIDTH), BS((1, hd), lambda n: (0, 0)),
                  BS((grp * nh * hd, hd), lambda n: (last - n, 0)), BS((grp * nh * c, c), lambda n: (last - n, 0)),
                  rev(1, GDN_WIDTH)],
        out_specs=[rev(0, 3 * GDN_WIDTH), rev(0, LANE), rev(0, GDN_WIDTH), BS((1, hd), lambda n: (0, 0))],
        out_shape=[SDS((s, 3 * GDN_WIDTH), F32), SDS((s, LANE), F32), SDS((s, GDN_WIDTH), BF16), SDS((1, hd), F32)],
        scratch_shapes=[pltpu.VMEM((nh * hd, hd), F32)],
        compiler_params=_cp(), name=name)(qkv, qkv, qkv, gbeta, proj, norm_w, states, tinv, dcat)


XATTN_TQ = 512


def _xattn_probs(qh, kh):
    sc = _dot(qh, kh, 1, 1) * (XATTN_HEAD_DIM ** -0.5)
    p = jnp.exp(sc - jnp.max(sc, axis=-1, keepdims=True))
    return p / jnp.sum(p, axis=-1, keepdims=True)


def _xattn_fwd(q, kv, *, name):
    s, d = q.shape
    m = kv.shape[0]
    tq, hd = _tile(s, XATTN_TQ), XATTN_HEAD_DIM

    def body(q_ref, k_ref, v_ref, o_ref):
        for h in range(XATTN_HEADS):
            ls = slice(h * hd, (h + 1) * hd)
            p = _xattn_probs(q_ref[:, ls], k_ref[:, ls])
            o_ref[:, ls] = _dot(p, v_ref[:, ls], 1, 0).astype(BF16)

    return pl.pallas_call(
        body, grid=(s // tq,),
        in_specs=[BS((tq, d), lambda i: (i, 0)), BS((m, d), lambda i: (0, 0)), BS((m, d), lambda i: (0, 1))],
        out_specs=BS((tq, d), lambda i: (i, 0)), out_shape=SDS((s, d), BF16),
        compiler_params=_cp(), name=name)(q, kv, kv)


def _xattn_bwd(q, kv, do, *, name):
    s, d = q.shape
    m = kv.shape[0]
    tq, hd = _tile(s, XATTN_TQ), XATTN_HEAD_DIM
    scale = hd ** -0.5

    def body(q_ref, k_ref, v_ref, do_ref, dq_ref, dkv_ref):
        @pl.when(pl.program_id(0) == 0)
        def _():
            dkv_ref[...] = jnp.zeros_like(dkv_ref)

        for h in range(XATTN_HEADS):
            ls = slice(h * hd, (h + 1) * hd)
            vs = slice(d + h * hd, d + (h + 1) * hd)
            qh, kh, doh = q_ref[:, ls], k_ref[:, ls], do_ref[:, ls]
            p = _xattn_probs(qh, kh)
            dp = _dot(doh, v_ref[:, ls], 1, 1)
            ds = p * (dp - jnp.sum(p * dp, axis=-1, keepdims=True)) * scale
            dq_ref[:, ls] = _dot(ds, kh, 1, 0).astype(BF16)
            dkv_ref[:, ls] += _dot(ds, qh, 0, 0)
            dkv_ref[:, vs] += _dot(p, doh, 0, 0)

    row = BS((tq, d), lambda i: (i, 0))
    return pl.pallas_call(
        body, grid=(s // tq,),
        in_specs=[row, BS((m, d), lambda i: (0, 0)), BS((m, d), lambda i: (0, 1)), row],
        out_specs=[row, BS((m, 2 * d), lambda i: (0, 0))],
        out_shape=[SDS((s, d), BF16), SDS((m, 2 * d), F32)],
        compiler_params=_cp(), name=name)(q, kv, kv, do)


def _pad_lanes(vec4):
    return jnp.zeros((1, LANE), F32).at[0, :GDN_HEADS].set(vec4)


def _layer_fwd(h0, mem, tabs, p):
    sv = dict(h0=h0)
    hn1 = _rmsnorm(h0, p["norm_mix_pre"], name="norm_mix_pre")
    proj = _mm(hn1, p["w_in"], name="mm_in")
    ya, lse = _attn_flash_fwd(proj, tabs, name="attn_fwd")
    yc = _sconv_fwd(proj, p["conv_short"], name="sconv_fwd")
    qkv = _gdn_pre_fwd(proj, p["conv_gdn"], name="gdn_pre_fwd")
    gbeta = _gdn_gates_fwd(proj, p["gdn_a_log"], p["gdn_dt_bias"], name="gdn_gates_fwd")
    yg, states, tinv = _gdn_core_fwd(qkv, gbeta, proj, p["gdn_norm"], name="gdn_core_fwd")
    cat = jnp.concatenate([ya, yc, yg], axis=-1)
    mix = _mm(cat, p["w_out"], name="mm_out")
    h1 = _resnorm(h0, mix, p["norm_mix_post"], name="norm_mix_post")
    hn2 = _rmsnorm(h1, p["norm_xattn_pre"], name="norm_xattn_pre")
    memn = _rmsnorm(mem, p["norm_mem"], name="norm_mem")
    xq = _mm(hn2, p["w_xq"], out_dtype=BF16, name="mm_xq")
    kv = _mm(memn, p["w_xkv"], out_dtype=BF16, b_shards=True, name="mm_xkv")
    xo = _xattn_fwd(xq, kv, name="xattn_fwd")
    xa = _mm(xo, p["w_xo"], name="mm_xo")
    h2 = _resnorm(h1, xa, p["norm_xattn_post"], name="norm_xattn_post")
    hn3 = _rmsnorm(h2, p["norm_ffn_pre"], name="norm_ffn_pre")
    gu, act = _gate_up_swiglu(hn3, p["w_gate_up"], name="mm_gate_up")
    f = _mm(act, p["w_down"], name="mm_down")
    h3 = _resnorm(h2, f, p["norm_ffn_post"], name="norm_ffn_post")
    sv.update(hn1=hn1, proj=proj, lse=lse, qkv=qkv, gbeta=gbeta, states=states, tinv=tinv, cat=cat, mix=mix, h1=h1, hn2=hn2,
              memn=memn, xq=xq, kv=kv, xo=xo, xa=xa, h2=h2, hn3=hn3, gu=gu, act=act, f=f)
    return h3, sv


def _layer_bwd(dh3, mem, tabs, p, sv, after_ffn=None, after_xattn=None):
    g = {}
    df, g["norm_ffn_post"] = _rmsnorm_bwd(sv["f"], p["norm_ffn_post"], dh3, name="norm_ffn_post_bwd")
    dact = _mm(df, p["w_down"], tb=True, name="mm_down_da")
    g["w_down"] = _mm(sv["act"], df, ta=True, name="mm_down_dw")
    dgu = _swiglu_bwd(sv["gu"], dact, name="swiglu_bwd")
    dhn3 = _mm(dgu, p["w_gate_up"], tb=True, b_shards=True, halves=True, name="mm_gate_up_da")
    g["w_gate_up"] = _mm(sv["hn3"], dgu, ta=True, out_shards=True, halves=True, name="mm_gate_up_dw")
    dh2, g["norm_ffn_pre"] = _rmsnorm_bwd(sv["h2"], p["norm_ffn_pre"], dhn3, res=dh3, name="norm_ffn_pre_bwd")
    token = after_ffn(dh2) if after_ffn is not None else None
    w_post = p["norm_xattn_post"] if token is None else p["norm_xattn_post"] + token[:1, :1]
    dxa, g["norm_xattn_post"] = _rmsnorm_bwd(sv["xa"], w_post, dh2, name="norm_xattn_post_bwd")
    dxo = _mm(dxa, p["w_xo"], tb=True, name="mm_xo_da")
    g["w_xo"] = _mm(sv["xo"], dxa, ta=True, name="mm_xo_dw")
    dxq, dkv = _xattn_bwd(sv["xq"], sv["kv"], dxo, name="xattn_bwd")
    dhn2 = _mm(dxq, p["w_xq"], tb=True, name="mm_xq_da")
    g["w_xq"] = _mm(sv["hn2"], dxq, ta=True, name="mm_xq_dw")
    dmemn = _mm(dkv, p["w_xkv"], tb=True, b_shards=True, name="mm_xkv_da")
    g["w_xkv"] = _mm(sv["memn"], dkv, ta=True, out_shards=True, name="mm_xkv_dw")
    _, g["norm_mem"] = _rmsnorm_bwd(mem, p["norm_mem"], dmemn, name="norm_mem_bwd")
    dh1, g["norm_xattn_pre"] = _rmsnorm_bwd(sv["h1"], p["norm_xattn_pre"], dhn2, res=dh2, name="norm_xattn_pre_bwd")
    token = after_xattn(g, dh1) if after_xattn is not None else None
    w_post = p["norm_mix_post"] if token is None else p["norm_mix_post"] + token[:1, :1]
    dmix, g["norm_mix_post"] = _rmsnorm_bwd(sv["mix"], w_post, dh1, name="norm_mix_post_bwd")
    dcat = _mm(dmix, p["w_out"], tb=True, name="mm_out_da")
    g["w_out"] = _mm(sv["cat"], dmix, ta=True, name="mm_out_dw")
    proj = sv["proj"]
    daq, dak, dav = _attn_flash_bwd(proj, tabs, sv["cat"], sv["lse"], dcat, name="attn_bwd")
    dcb, dcc, dcx, g["conv_short"] = _sconv_bwd(proj, p["conv_short"], dcat, name="sconv_bwd")
    dqkv, dgbeta, dgate, g["gdn_norm"] = _gdn_core_bwd(sv["qkv"], sv["gbeta"], proj, p["gdn_norm"], sv["states"], sv["tinv"],
                                                        dcat, name="gdn_core_bwd")
    dgqkv, g["conv_gdn"] = _gdn_pre_bwd(proj, p["conv_gdn"], dqkv, name="gdn_pre_bwd")
    dab, g["gdn_a_log"], g["gdn_dt_bias"] = _gdn_gates_bwd(proj, p["gdn_a_log"], p["gdn_dt_bias"], dgbeta,
                                                          name="gdn_gates_bwd")
    s = proj.shape[0]
    dproj = jnp.concatenate([daq, dak, dav, dcb, dcc, dcx, dgqkv, dgate, dab,
                             jnp.zeros((s, IN_PAD - COL_AB - LANE), BF16)], axis=-1)
    dhn1 = _mm(dproj, p["w_in"], tb=True, name="mm_in_da")
    g["w_in"] = _mm(sv["hn1"], dproj, ta=True, name="mm_in_dw")
    dh0, g["norm_mix_pre"] = _rmsnorm_bwd(sv["h0"], p["norm_mix_pre"], dhn1, res=dh1, name="norm_mix_pre_bwd")
    return dh0, g


MATRICES = ("w_in", "w_out", "w_xq", "w_xkv", "w_xo", "w_gate_up", "w_down")
VECTORS = ("norm_mix_pre", "norm_mix_post", "conv_short", "conv_gdn", "gdn_a_log", "gdn_dt_bias", "gdn_norm",
           "norm_mem", "norm_xattn_pre", "norm_xattn_post", "norm_ffn_pre", "norm_ffn_post")


def _w_in_segments(n_shards=4):
    c = IN_WIDTH // n_shards
    moves = ((0, COL_GATE, 0), (COL_GATE, COL_GATE + 8, COL_AB), (COL_GATE + 8, IN_WIDTH, COL_GATE))
    segs = []
    for s in range(n_shards):
        for lo, hi, dst in moves:
            a, b = max(lo, s * c), min(hi, (s + 1) * c)
            if a < b:
                segs.append((s, a - s * c, dst + a - lo, b - a))
    return segs


def _w_in_pack(g, *, name):
    ns, r, c = g.shape
    tr = _tile(r, 256)

    def body(g_ref, o_ref):
        o_ref[:, IN_WIDTH:] = jnp.zeros((tr, IN_PAD - IN_WIDTH), o_ref.dtype)
        for s, src, dst, width in _w_in_segments(ns):
            o_ref[:, dst:dst + width] = g_ref[s, :, src:src + width]

    return pl.pallas_call(
        body, grid=(r // tr,), in_specs=[BS((ns, tr, c), lambda i: (0, i, 0))],
        out_specs=BS((tr, IN_PAD), lambda i: (i, 0)), out_shape=SDS((r, IN_PAD), g.dtype),
        compiler_params=_cp(), name=name)(g)


def _w_in_unpack(dw, *, name):
    r = dw.shape[0]
    ns, c = 4, IN_WIDTH // 4
    tr = _tile(r, 256)

    def body(d_ref, o_ref):
        for s, src, dst, width in _w_in_segments(ns):
            o_ref[s, :, src:src + width] = d_ref[:, dst:dst + width]

    return pl.pallas_call(
        body, grid=(r // tr,), in_specs=[BS((tr, IN_PAD), lambda i: (i, 0))],
        out_specs=BS((ns, tr, c), lambda i: (0, i, 0)), out_shape=SDS((ns, r, c), dw.dtype),
        compiler_params=_cp(), name=name)(dw)


def _layer_params(full, l):
    p = {n: full[n][l] for n in MATRICES}
    for n in VECTORS:
        v = full[n][l]
        if n in ("gdn_a_log", "gdn_dt_bias"):
            p[n] = _pad_lanes(v)
        elif v.ndim == 1:
            p[n] = v.reshape(1, -1)
        else:
            p[n] = v
    return p


def _local_step(x, mem, pos, target, full, matrices_for=None, on_grads=None, mid_backward=None,
                after_xattn=None):
    tabs = _rope_tables(pos, name="rope_tables")
    h = x
    saved, params = [], []
    for l in range(DEPTH):
        if matrices_for is not None:
            full = {**full, **{n: {l: v} for n, v in matrices_for(l, h).items()}}
        p = _layer_params(full, l)
        h, sv = _layer_fwd(h, mem, tabs, p)
        params.append(p)
        saved.append(sv)
    loss_row, dh = _loss_grad(h, target, name="loss_grad")
    grads = [None] * DEPTH
    token = None
    for l in reversed(range(DEPTH)):
        p = params[l]
        if token is not None:
            p = {**p, "norm_ffn_post": p["norm_ffn_post"] + token[:1, :1]}
        late = None if after_xattn is None else (lambda g, dh1, l=l: after_xattn(l, g, dh1))
        dh, grads[l] = _layer_bwd(dh, mem, tabs, p, saved[l], after_ffn=mid_backward, after_xattn=late)
        if on_grads is not None:
            token = on_grads(l, grads[l], dh)
    return loss_row, dh, grads


ANY = pl.BlockSpec(memory_space=pl.ANY)
MESH = pl.DeviceIdType.MESH


def _flip(pos, mask):
    return tuple(1 - v if m else v for v, m in zip(pos, mask))


def _exchange(ins, out_shapes, remote, local, *, name):
    n_in = len(ins)
    n_out = len(out_shapes)

    def at(ref, idx):
        return ref.at[idx] if idx else ref

    def body(*refs):
        in_refs = refs[:n_in]
        out_refs = refs[n_in:n_in + n_out]
        send_sems, recv_sems, local_sems = refs[n_in + n_out:]
        me = (lax.axis_index("x"), lax.axis_index("y"), lax.axis_index("c"))
        waits = []
        for k, (ii, src_at, oi, dst_at, mask) in enumerate(remote):
            peer = _flip(me, mask)
            pltpu.make_async_remote_copy(
                src_ref=at(in_refs[ii], src_at(me, peer)), dst_ref=at(out_refs[oi], dst_at(me)),
                send_sem=send_sems.at[k], recv_sem=recv_sems.at[k], device_id=peer, device_id_type=MESH).start()
            waits.append(pltpu.make_async_remote_copy(
                src_ref=at(in_refs[ii], src_at(peer, me)), dst_ref=at(out_refs[oi], dst_at(peer)),
                send_sem=send_sems.at[k], recv_sem=recv_sems.at[k], device_id=peer, device_id_type=MESH))
        own = []
        for k, (ii, src_at, oi, dst_at) in enumerate(local):
            cp = pltpu.make_async_copy(at(in_refs[ii], src_at(me)), at(out_refs[oi], dst_at(me)), local_sems.at[k])
            cp.start()
            own.append(cp)
        for w in waits:
            w.wait_send()
            w.wait_recv()
        for cp in own:
            cp.wait()

    return pl.pallas_call(
        body, in_specs=[ANY] * n_in, out_specs=[ANY] * n_out, out_shape=list(out_shapes),
        scratch_shapes=[pltpu.SemaphoreType.DMA((len(remote),)), pltpu.SemaphoreType.DMA((len(remote),)),
                        pltpu.SemaphoreType.DMA((max(len(local), 1),))],
        name=name)(*ins)


HBM = pl.BlockSpec(memory_space=pltpu.HBM)
SEM = pl.BlockSpec(memory_space=pltpu.SEMAPHORE)
SPLIT_EFFECT = pltpu.SideEffectType.DATAFLOW_SIDE_EFFECTING


def _exchange_start(ins, land_shapes, remote, *, name):
    n_in, n_land, n_cp = len(ins), len(land_shapes), len(remote)

    def body(*refs):
        in_refs, land_refs = refs[:n_in], refs[n_in:n_in + n_land]
        send_sems, recv_sems = refs[n_in + n_land:n_in + n_land + 2]
        token = refs[-1]
        me = (lax.axis_index("x"), lax.axis_index("y"), lax.axis_index("c"))
        for k, (ii, src_at, oi, dst_at, mask) in enumerate(remote):
            peer = _flip(me, mask)
            idx_s, idx_d = src_at(me, peer), dst_at(me)
            pltpu.make_async_remote_copy(
                src_ref=in_refs[ii].at[idx_s] if idx_s else in_refs[ii],
                dst_ref=land_refs[oi].at[idx_d] if idx_d else land_refs[oi],
                send_sem=send_sems.at[k], recv_sem=recv_sems.at[k], device_id=peer, device_id_type=MESH).start()
        token[...] = jnp.zeros_like(token)

    buffers = [pltpu.with_memory_space_constraint(a, pltpu.HBM) for a in ins]
    buffers += [pltpu.with_memory_space_constraint(lax.empty(s.shape, s.dtype), pltpu.HBM) for s in land_shapes]
    out = pl.pallas_call(
        body, name=name,
        out_shape=(pltpu.SemaphoreType.DMA((n_cp,)), pltpu.SemaphoreType.DMA((n_cp,)),
                   *[pltpu.HBM(b.shape, b.dtype) for b in buffers], SDS((8, LANE), F32)),
        in_specs=[HBM] * len(buffers),
        out_specs=(SEM, SEM, *[HBM] * len(buffers), pl.BlockSpec(memory_space=pltpu.VMEM)),
        input_output_aliases={i: 2 + i for i in range(len(buffers))},
        compiler_params=pltpu.CompilerParams(has_side_effects=SPLIT_EFFECT))(*buffers)
    return out[0], out[1], list(out[2:2 + n_in]), list(out[2 + n_in:2 + n_in + n_land]), out[-1]


def _exchange_wait(send_sems, recv_sems, ins, lands, remote, after, *, name):
    n_in, n_land = len(ins), len(lands)

    def body(*refs):
        in_refs, land_refs = refs[:n_in], refs[n_in:n_in + n_land]
        send_sems_, recv_sems_ = refs[n_in + n_land:n_in + n_land + 2]
        me = (lax.axis_index("x"), lax.axis_index("y"), lax.axis_index("c"))
        for k, (ii, src_at, oi, dst_at, mask) in enumerate(remote):
            peer = _flip(me, mask)
            idx_s, idx_d = src_at(peer, me), dst_at(peer)
            cp = pltpu.make_async_remote_copy(
                src_ref=in_refs[ii].at[idx_s] if idx_s else in_refs[ii],
                dst_ref=land_refs[oi].at[idx_d] if idx_d else land_refs[oi],
                send_sem=send_sems_.at[k], recv_sem=recv_sems_.at[k], device_id=peer, device_id_type=MESH)
            cp.wait_send()
            cp.wait_recv()

    buffers = list(ins) + list(lands)
    out = pl.pallas_call(
        body, name=name, out_shape=tuple(pltpu.HBM(b.shape, b.dtype) for b in buffers),
        in_specs=[HBM] * len(buffers) + [SEM, SEM, ANY], out_specs=tuple([HBM] * len(buffers)),
        input_output_aliases={i: i for i in range(len(buffers))},
        compiler_params=pltpu.CompilerParams(has_side_effects=SPLIT_EFFECT))(*buffers, send_sems, recv_sems, after)
    return list(out[:n_in]), list(out[n_in:])


def _chip(pos):
    return 2 * pos[0] + pos[1]


XY_MASKS = ((1, 0, 0), (0, 1, 0), (1, 1, 0))
SIBLING = (0, 0, 1)
ALL_MASKS = tuple((a, b, c) for a in (0, 1) for b in (0, 1) for c in (0, 1))[1:]


def _gather_xy(arrs, *, name):
    n = len(arrs)
    outs = [SDS((4,) + a.shape, a.dtype) for a in arrs]
    halves = [a.shape[0] // 2 for a in arrs]

    def body(*refs):
        in_refs, out_refs = refs[:n], refs[n:2 * n]
        ici_send, ici_recv, d2d_send, d2d_recv = refs[2 * n:]
        me = (lax.axis_index("x"), lax.axis_index("y"), lax.axis_index("c"))
        sibling = _flip(me, SIBLING)
        flows = []
        for i in range(n):
            mine = pl.ds(me[2] * halves[i], halves[i])
            other = pl.ds(sibling[2] * halves[i], halves[i])
            for m in XY_MASKS:
                k = len(flows)
                peer = _flip(me, m)

                def remote(src, dst, sems, to, k=k):
                    return pltpu.make_async_remote_copy(src_ref=src, dst_ref=dst, send_sem=sems[0].at[k],
                                                        recv_sem=sems[1].at[k], device_id=to, device_id_type=MESH)

                landed = out_refs[i].at[_chip(peer), mine]
                send = remote(in_refs[i].at[mine], out_refs[i].at[_chip(me), mine], (ici_send, ici_recv), peer)
                send.start()
                arrive = remote(in_refs[i].at[mine], landed, (ici_send, ici_recv), peer)
                forward = remote(landed, landed, (d2d_send, d2d_recv), sibling)
                handed = remote(out_refs[i].at[_chip(peer), other], out_refs[i].at[_chip(peer), other],
                                (d2d_send, d2d_recv), sibling)
                flows.append((send, arrive, forward, handed))
        for _, arrive, forward, _ in flows:
            arrive.wait_recv()
            forward.start()
        for send, _, forward, handed in flows:
            handed.wait_recv()
            send.wait_send()
            forward.wait_send()

    n_flows = 3 * n
    return pl.pallas_call(
        body, in_specs=[ANY] * n, out_specs=[ANY] * n, out_shape=outs,
        scratch_shapes=[pltpu.SemaphoreType.DMA((n_flows,))] * 4, name=name)(*arrs)


def _gather_all(arr, *, name):
    slot = lambda pos: (4 * pos[0] + 2 * pos[1] + pos[2],)
    whole = lambda *_: ()
    remote = [(0, whole, 0, slot, m) for m in ALL_MASKS]
    return _exchange([arr], [SDS((8,) + arr.shape, arr.dtype)], remote, [(0, whole, 0, slot)], name=name)[0]


def _send_to_sibling(arrs, *, name):
    outs = [SDS(a.shape, a.dtype) for a in arrs]
    whole = lambda *_: ()
    remote = [(i, whole, i, whole, SIBLING) for i in range(len(arrs))]
    return _exchange(arrs, outs, remote, [], name=name)


def _add_half(g, other, core, *, name):
    n4, nl, r, c = g.shape
    half = r // 2
    g3 = g.reshape(n4 * nl, 2, half, c)
    o3 = other.reshape(n4 * nl, half, c)
    tr = _rows_tile(half) if half > 512 else half

    def body(core_ref, g_ref, o_ref, out_ref):
        out_ref[...] = (g_ref[...] + o_ref[...]).astype(BF16)

    return pl.pallas_call(
        body,
        grid_spec=pltpu.PrefetchScalarGridSpec(
            num_scalar_prefetch=1, grid=(n4 * nl, half // tr),
            in_specs=[BS((None, None, tr, c), lambda i, j, core_ref: (i, core_ref[0], j, 0)),
                      BS((None, tr, c), lambda i, j, core_ref: (i, j, 0))],
            out_specs=BS((None, tr, c), lambda i, j, core_ref: (i, j, 0))),
        out_shape=SDS((n4 * nl, half, c), BF16), compiler_params=_cp(), name=name)(core, g3, o3).reshape(n4, nl, half, c)


def _sum_chips(parts, mine, chip, *, name):
    n4, nl, h, c = mine.shape
    tr = _rows_tile(h) if h > 512 else h

    def body(chip_ref, p_ref, own_ref, out_ref):
        me = chip_ref[0]
        own = own_ref[...].astype(F32)
        across = [p_ref[j].astype(F32) for j in range(len(XY_MASKS))]
        t = []
        for s in range(n4):
            rel = s ^ me
            t.append(jnp.where(rel == 0, own, jnp.where(rel == 2, across[0], jnp.where(rel == 1, across[1], across[2]))))
        out_ref[...] = ((t[0] + t[1]) + t[2]) + t[3]

    return pl.pallas_call(
        body,
        grid_spec=pltpu.PrefetchScalarGridSpec(
            num_scalar_prefetch=1, grid=(nl, h // tr),
            in_specs=[BS((len(XY_MASKS), None, tr, c), lambda i, j, chip_ref: (0, i, j, 0)),
                      BS((None, None, tr, c), lambda i, j, chip_ref: (chip_ref[0], i, j, 0))],
            out_specs=BS((None, tr, c), lambda i, j, chip_ref: (i, j, 0))),
        out_shape=SDS((nl, h, c), F32), compiler_params=_cp(), name=name)(chip, parts, mine)


def _sum_devices(parts, *, name):
    n, r, c = parts.shape

    def body(p_ref, out_ref):
        acc = p_ref[0]
        for d in range(1, n):
            acc = acc + p_ref[d]
        out_ref[...] = acc

    return pl.pallas_call(
        body, grid=(1,), in_specs=[BS((n, r, c), lambda i: (0, 0, 0))], out_specs=BS((r, c), lambda i: (0, 0)),
        out_shape=SDS((r, c), F32), compiler_params=_cp(), name=name)(parts)


WEIGHTS = ("norm_mix_pre", "norm_mix_post", "w_in", "conv_short", "conv_gdn", "gdn_a_log", "gdn_dt_bias",
           "gdn_norm", "w_out", "norm_mem", "norm_xattn_pre", "norm_xattn_post", "w_xq", "w_xkv", "w_xo",
           "norm_ffn_pre", "norm_ffn_post", "w_gate_up", "w_down")
COL_SHARDED = ("w_in", "w_xkv", "w_gate_up", "conv_short", "conv_gdn")
ROW_SHARDED = ("w_out", "w_xq", "w_xo", "w_down")
SMALL_SHARDED = ("conv_short", "conv_gdn")
SMALL_ROW_PAD = 8


KEPT_AS_SHARDS = ("w_xkv", "w_gate_up")


def _from_shards(n, g):
    if n in KEPT_AS_SHARDS:
        return g
    if n == "w_in":
        return _w_in_pack(g, name="w_in_pack")
    if n in COL_SHARDED:
        t = jnp.moveaxis(g, 0, -2)
        return t.reshape(t.shape[:-2] + (-1,))
    return g.reshape(-1, g.shape[-1])


def _to_shards(n, g):
    if n in KEPT_AS_SHARDS:
        return g
    if n == "w_in":
        return _w_in_unpack(g, name="w_in_unpack")
    return g.reshape(4, -1, g.shape[-1])


def _pack_small(grads):
    rows = []
    for g in grads:
        for n in WEIGHTS:
            if n not in MATRICES:
                part = g[n].reshape(-1, LANE)
                rows.append(jnp.pad(part, ((0, -part.shape[0] % SMALL_ROW_PAD), (0, 0))))
    return jnp.concatenate(rows, axis=0)


def _unpack_small(packed, like):
    out, at = [], 0
    for _ in range(DEPTH):
        g = {}
        for n in WEIGHTS:
            if n not in MATRICES:
                shape = like[n].shape
                k = math.prod(shape) // LANE
                g[n] = packed[at:at + k].reshape(shape)
                at += k + (-k % SMALL_ROW_PAD)
        out.append(g)
    return out


def kernel(x, mem, positions, norm_mix_pre, norm_mix_post, w_in, conv_short, conv_gdn, gdn_a_log, gdn_dt_bias, gdn_norm, w_out, norm_mem, norm_xattn_pre, norm_xattn_post, w_xq, w_xkv, w_xo, norm_ffn_pre, norm_ffn_post, w_gate_up, w_down, loss_target, m_norm_mix_pre, m_norm_mix_post, m_w_in, m_conv_short, m_conv_gdn, m_gdn_a_log, m_gdn_dt_bias, m_gdn_norm, m_w_out, m_norm_mem, m_norm_xattn_pre, m_norm_xattn_post, m_w_xq, m_w_xkv, m_w_xo, m_norm_ffn_pre, m_norm_ffn_post, m_w_gate_up, m_w_down, v_norm_mix_pre, v_norm_mix_post, v_w_in, v_conv_short, v_conv_gdn, v_gdn_a_log, v_gdn_dt_bias, v_gdn_norm, v_w_out, v_norm_mem, v_norm_xattn_pre, v_norm_xattn_post, v_w_xq, v_w_xkv, v_w_xo, v_norm_ffn_pre, v_norm_ffn_post, v_w_gate_up, v_w_down):
    args = dict(locals())
    w = {n: args[n] for n in WEIGHTS}
    m = {n: args["m_" + n] for n in WEIGHTS}
    v = {n: args["v_" + n] for n in WEIGHTS}
    seq = x.shape[1]
    chip = 2 * lax.axis_index("x") + lax.axis_index("y")
    core = lax.axis_index("c").astype(jnp.int32).reshape(1)

    def cast(n, l):
        return w[n][l].astype(BF16)

    first = [cast(n, 0) for n in MATRICES] + [w[n] for n in SMALL_SHARDED]
    blocks = _gather_xy(first, name="gather_weights")
    blocks = [lax.dynamic_update_index_in_dim(b, o, chip, axis=0) for b, o in zip(blocks, first)]
    full = {n: _from_shards(n, b) for n, b in zip(list(MATRICES) + list(SMALL_SHARDED), blocks)}
    layer0 = {n: full.pop(n) for n in MATRICES}
    for n in WEIGHTS:
        if n not in full and n not in MATRICES:
            full[n] = w[n]
    whole = lambda *_: ()
    gather_plan = [(i, whole, i, (lambda sender: (_chip(sender),)), mask)
                   for i in range(len(MATRICES)) for mask in XY_MASKS]
    in_flight = {}
    started = jnp.zeros((1, 1), F32)
    for l in range(1, DEPTH):
        own = [cast(n, l) for n in MATRICES]
        out = _exchange_start(own, [SDS((4,) + o.shape, o.dtype) for o in own], gather_plan, name=f"gather_start_{l}")
        in_flight[l] = (own,) + out[:4]
        started = started + out[4][:1, :1]
    full["norm_mix_pre"] = full["norm_mix_pre"] + started

    def matrices_for(l, h):
        if l == 0:
            return layer0
        own, send_sems, recv_sems, thru, lands = in_flight.pop(l)
        _, lands = _exchange_wait(send_sems, recv_sems, thru, lands, gather_plan, h, name=f"gather_wait_{l}")
        lands = [lax.dynamic_update_index_in_dim(b, o, chip, axis=0) for b, o in zip(lands, own)]
        return {n: _from_shards(n, b) for n, b in zip(MATRICES, lands)}

    chip1 = chip.astype(jnp.int32).reshape(1)
    early = ("w_xq", "w_xkv", "w_xo", "w_gate_up", "w_down")
    reduced = {l: {} for l in range(DEPTH)}
    swapping, sending, sharing = [], [], []

    def scatter_plan(count):
        return [(i, (lambda sender, receiver: (_chip(receiver),)), i, (lambda sender, j=j: (j,)), mask)
                for i in range(count) for j, mask in enumerate(XY_MASKS)]

    def swap_plan(mine):
        def half_rows(shape):
            half = shape[2] // 2
            return lambda sender, receiver: (slice(None), slice(None), pl.ds(receiver[2] * half, half))

        plan = [(i, half_rows(a.shape), i, whole, SIBLING) for i, a in enumerate(mine)]
        return plan, [SDS(a.shape[:2] + (a.shape[2] // 2, a.shape[3]), a.dtype) for a in mine]

    def start_scatter(tag, names, mine, theirs):
        pair = [_add_half(a, b, core, name="grads_pair_sum") for a, b in zip(mine, theirs)]
        lands = [SDS((len(XY_MASKS),) + p.shape[1:], p.dtype) for p in pair]
        send_sems, recv_sems, pair, lands, token = _exchange_start(pair, lands, scatter_plan(len(names)),
                                                                   name=f"scatter_start_{tag}")
        sending.append((tag, names, send_sems, recv_sems, pair, lands))
        return token

    def finish_scatter(after):
        tag, names, send_sems, recv_sems, pair, lands = sending.pop(0)
        pair, parts = _exchange_wait(send_sems, recv_sems, pair, lands, scatter_plan(len(names)), after,
                                     name=f"scatter_wait_{tag}")
        layer = int(tag[0])
        sums = [_sum_chips(p, pr, chip1, name="grads_chip_sum") for p, pr in zip(parts, pair)]
        token = None
        if layer > 0:
            plan = [(i, whole, i, whole, SIBLING) for i in range(len(sums))]
            send_sems, recv_sems, sums, lands, token = _exchange_start(
                sums, [SDS(r.shape, r.dtype) for r in sums], plan, name=f"share_start_{tag}")
            sharing.append((layer, names, send_sems, recv_sems, sums, lands, plan, tag))
        reduced[layer].update(zip(names, sums))
        return token

    def on_grads(l, g, dh):
        tokens = []
        while sending:
            tokens.append(finish_scatter(dh))
        names = [n for n in MATRICES if not (l == 0 and n in early)]
        mine = [_to_shards(n, g[n])[:, None] for n in names]
        plan, lands = swap_plan(mine)
        send_sems, recv_sems, mine, lands, token = _exchange_start(mine, lands, plan, name=f"swap_start_{l}")
        swapping.append((str(l), names, send_sems, recv_sems, mine, lands, plan))
        for t in tokens:
            if t is not None:
                token = token + t
        return token

    def mid_backward(after):
        if not swapping:
            return None
        tag, names, send_sems, recv_sems, mine, lands, plan = swapping.pop()
        mine, theirs = _exchange_wait(send_sems, recv_sems, mine, lands, plan, after, name=f"swap_wait_{tag}")
        return start_scatter(tag, names, mine, theirs)

    def after_xattn(l, g, dh1):
        if l != 0:
            return None
        mine = [_to_shards(n, g[n])[:, None] for n in early]
        plan, lands = swap_plan(mine)
        theirs = _exchange(mine, lands, plan, [], name="grads_swap_early")
        return start_scatter("0a", early, mine, theirs)

    loss_row, dx, grads = _local_step(x[0], mem[0], positions.reshape(seq, 1), loss_target[0], full,
                                      matrices_for=matrices_for, on_grads=on_grads, mid_backward=mid_backward,
                                      after_xattn=after_xattn)
    mid_backward(dx)
    while sending:
        finish_scatter(dx)
    others = {l: {} for l in range(DEPTH)}
    for layer, names, send_sems, recv_sems, sums, lands, plan, tag in sharing:
        sums, lands = _exchange_wait(send_sems, recv_sems, sums, lands, plan, dx, name=f"share_wait_{tag}")
        reduced[layer].update(zip(names, sums))
        others[layer].update(zip(names, lands))
    last = [reduced[0][n] for n in MATRICES]
    others[0].update(zip(MATRICES, _send_to_sibling(last, name="grads_share_halves")))
    south = lax.axis_index("c") == 0
    grad = {}
    for n in MATRICES:
        layers = []
        for l in range(DEPTH):
            a, b = reduced[l][n], others[l][n]
            layers += [jnp.where(south, a, b), jnp.where(south, b, a)]
        grad[n] = jnp.concatenate(layers, axis=0).reshape(w[n].shape)

    packed = _pack_small(grads)
    total = _sum_devices(_gather_all(packed, name="small_grads_gather"), name="small_grads_sum")
    small = _unpack_small(total, grads[0])
    for n in WEIGHTS:
        if n in MATRICES:
            continue
        g = jnp.stack([s[n] for s in small])
        if n in ("gdn_a_log", "gdn_dt_bias"):
            g = g[:, 0, :GDN_HEADS]
        elif n in SMALL_SHARDED:
            width = w[n].shape[-1]
            g = lax.dynamic_slice_in_dim(g, chip * width, width, axis=2)
        grad[n] = g.reshape(w[n].shape)

    delta, new_m, new_v = {}, {}, {}
    for n in WEIGHTS:
        shape = w[n].shape
        two_d = (-1, shape[-1])
        d, nm, nv = _adamw(w[n].reshape(two_d), grad[n].reshape(two_d), m[n].reshape(two_d), v[n].reshape(two_d),
                           name="adamw_" + n)
        delta[n], new_m[n], new_v[n] = d.reshape(shape), nm.reshape(shape), nv.reshape(shape)

    loss = lax.psum(loss_row[0, 0], ("x", "y", "c"))
    return (loss, dx.reshape(x.shape), *[grad[n] for n in WEIGHTS], *[delta[n] for n in WEIGHTS],
            *[new_m[n] for n in WEIGHTS], *[new_v[n] for n in WEIGHTS])
```

```python
import math

import jax
import jax.numpy as jnp
from jax import lax
from jax.experimental import pallas as pl
from jax.experimental.pallas import tpu as pltpu

F32 = jnp.float32
BF16 = jnp.bfloat16
BS = pl.BlockSpec
SDS = jax.ShapeDtypeStruct
PRECISE = lax.Precision.HIGH

D_MODEL = 1024
DEPTH = 4
EPS = 1e-6
ATTN_HEADS = 4
ATTN_HEAD_DIM = 64
ATTN_WIDTH = 256
ROPE_THETA = 500000.0
ROPE_DIM = 16
CONV_WIDTH = 256
CONV_K = 3
GDN_HEADS = 4
GDN_HEAD_DIM = 128
GDN_WIDTH = 512
GDN_CONV_K = 4
GDN_CHUNK = 64
IN_WIDTH = 3592
XATTN_HEADS = 4
XATTN_HEAD_DIM = 256
FFN_HIDDEN = 2816
ADAM_LR = 0.001
ADAM_B1 = 0.9
ADAM_B2 = 0.999
ADAM_EPS = 1e-08
ADAM_WD = 0.01
ADAM_STEP = 10

IN_PAD = 3840
COL_GDN = 1536
COL_GATE = 3072
COL_AB = 3584

VMEM_LIMIT_V7X = 56 * 1024 * 1024
LANE = 128


def _cp(**kw):
    return pltpu.CompilerParams(vmem_limit_bytes=VMEM_LIMIT_V7X, **kw)


def _tile(n, cap):
    if n <= cap:
        return n
    best = None
    for t in range(LANE, cap + 1, LANE):
        if n % t == 0:
            best = t
    assert best is not None, (n, cap)
    return best


def _dot(a, b, ca, cb, precise=False):
    dims = (((ca,), (cb,)), ((), ()))
    if precise:
        return lax.dot_general(a.astype(F32), b.astype(F32), dims, precision=PRECISE,
                               preferred_element_type=F32)
    return lax.dot_general(a.astype(BF16), b.astype(BF16), dims, preferred_element_type=F32)


def _sigmoid(x):
    return 1.0 / (1.0 + jnp.exp(-x))


ROW_TILE = 1024
MM_ROWS = 1408
MM_BLOCK_BYTES = 6 * 1024 * 1024
MM_A_BYTES = 8 * 1024 * 1024


def _mm_tn(width, k, itemsize):
    if k * width * itemsize <= MM_BLOCK_BYTES:
        return width
    return _tile(width, max(LANE, min(1024, MM_BLOCK_BYTES // (k * itemsize) // LANE * LANE)))


def _mm(a, b, *, ta=False, tb=False, out_dtype=F32, b_shards=False, out_shards=False, halves=False, name):
    if halves and tb:
        m, k = a.shape[1], 2 * a.shape[2]
    else:
        m, k = (a.shape[1], a.shape[0]) if ta else a.shape
    tm = _tile(m, MM_ROWS)
    ca = 0 if ta else 1

    if b_shards and tb:
        ns, n, c = b.shape
        assert k == ns * c and not ta
        tn = _tile(n, max(LANE, min(1024, MM_BLOCK_BYTES // (k * b.dtype.itemsize) // LANE * LANE)))

        def a_block(a_ref, s):
            if halves:
                per = ns // 2
                return a_ref[s // per, :, (s % per) * c:(s % per + 1) * c]
            return a_ref[:, s * c:(s + 1) * c]

        def body(a_ref, b_ref, o_ref):
            acc = _dot(a_block(a_ref, 0), b_ref[0], 1, 1)
            for s in range(1, ns):
                acc = acc + _dot(a_block(a_ref, s), b_ref[s], 1, 1)
            o_ref[...] = acc.astype(out_dtype)

        b_spec = BS((ns, tn, c), lambda i, j: (0, j, 0))
    else:
        if b_shards:
            ns, kb, c = b.shape
            n = ns * c
            tn = _mm_tn(c, k, b.dtype.itemsize)
            nb = c // tn
            b_spec = BS((None, k, tn), lambda i, j: (j // nb, 0, j % nb))
        elif halves:
            kb, n = b.shape[1], 2 * b.shape[2]
            c = n // 4
            tn = _mm_tn(c, k, b.dtype.itemsize)
            nb = c // tn
            b_spec = BS((None, k, tn), lambda i, j: (j // (2 * nb), 0, j % (2 * nb)))
        else:
            kb, n = (b.shape[1], b.shape[0]) if tb else b.shape
            c = n // 4 if out_shards else n
            tn = _mm_tn(c, k, b.dtype.itemsize)
            nb = c // tn
            b_spec = BS((tn, k), lambda i, j: (j, 0)) if tb else BS((k, tn), lambda i, j: (0, j))
        assert kb == k
        cb = 1 if tb else 0

        def body(a_ref, b_ref, o_ref):
            o_ref[...] = _dot(a_ref[...], b_ref[...], ca, cb).astype(out_dtype)

    out_bytes = jnp.dtype(out_dtype).itemsize
    while (tm > 256 and tm % 256 == 0 and
           (tm * k * a.dtype.itemsize > MM_A_BYTES or tm * tn * out_bytes > MM_BLOCK_BYTES)):
        tm //= 2
    if halves and tb:
        a_spec = BS((2, tm, k // 2), lambda i, j: (0, i, 0))
    else:
        a_spec = BS((k, tm), lambda i, j: (0, i)) if ta else BS((tm, k), lambda i, j: (i, 0))
    if out_shards:
        out_spec = BS((None, tm, tn), lambda i, j: (j // nb, i, j % nb))
        out_shape = SDS((4, m, n // 4), out_dtype)
    elif halves and not (ta or tb):
        out_spec = BS((None, tm, tn), lambda i, j: (j // (2 * nb), i, j % (2 * nb)))
        out_shape = SDS((2, m, n // 2), out_dtype)
    else:
        out_spec = BS((tm, tn), lambda i, j: (i, j))
        out_shape = SDS((m, n), out_dtype)
    return pl.pallas_call(
        body, grid=(m // tm, n // tn), in_specs=[a_spec, b_spec], out_specs=out_spec, out_shape=out_shape,
        compiler_params=_cp(), name=name)(a, b)


def _rmsnorm(x, w, *, name):
    r, d = x.shape
    tr = _tile(r, ROW_TILE)

    def body(x_ref, w_ref, o_ref):
        xv = x_ref[...]
        rs = lax.rsqrt(jnp.mean(xv * xv, axis=-1, keepdims=True) + EPS)
        o_ref[...] = (xv * rs * w_ref[...]).astype(BF16)

    return pl.pallas_call(
        body, grid=(r // tr,), in_specs=[BS((tr, d), lambda i: (i, 0)), BS((1, d), lambda i: (0, 0))],
        out_specs=BS((tr, d), lambda i: (i, 0)), out_shape=SDS((r, d), BF16),
        compiler_params=_cp(), name=name)(x, w)


def _resnorm(h, m, w, *, name):
    r, d = h.shape
    tr = _tile(r, ROW_TILE)

    def body(h_ref, m_ref, w_ref, o_ref):
        mv = m_ref[...]
        rs = lax.rsqrt(jnp.mean(mv * mv, axis=-1, keepdims=True) + EPS)
        o_ref[...] = h_ref[...] + mv * rs * w_ref[...]

    row = BS((tr, d), lambda i: (i, 0))
    return pl.pallas_call(
        body, grid=(r // tr,), in_specs=[row, row, BS((1, d), lambda i: (0, 0))],
        out_specs=row, out_shape=SDS((r, d), F32), compiler_params=_cp(), name=name)(h, m, w)


def _rmsnorm_bwd(x, w, dy, res=None, *, name):
    r, d = x.shape
    tr = _tile(r, ROW_TILE)
    has_res = res is not None

    def body(*refs):
        if has_res:
            x_ref, w_ref, dy_ref, res_ref, dx_ref, dw_ref = refs
        else:
            x_ref, w_ref, dy_ref, dx_ref, dw_ref = refs
        xv = x_ref[...]
        dyv = dy_ref[...].astype(F32)
        rs = lax.rsqrt(jnp.mean(xv * xv, axis=-1, keepdims=True) + EPS)
        nv = xv * rs
        dyw = dyv * w_ref[...]
        dx = rs * (dyw - nv * jnp.mean(dyw * nv, axis=-1, keepdims=True))
        if has_res:
            dx = dx + res_ref[...]
        dx_ref[...] = dx

        @pl.when(pl.program_id(0) == 0)
        def _():
            dw_ref[...] = jnp.zeros_like(dw_ref)

        dw_ref[...] += jnp.sum(dyv * nv, axis=0, keepdims=True)

    row = BS((tr, d), lambda i: (i, 0))
    vec = BS((1, d), lambda i: (0, 0))
    ins = [x, w, dy] + ([res] if has_res else [])
    return pl.pallas_call(
        body, grid=(r // tr,), in_specs=[row, vec, row] + ([row] if has_res else []),
        out_specs=[row, vec], out_shape=[SDS((r, d), F32), SDS((1, d), F32)],
        compiler_params=_cp(), name=name)(*ins)


def _gate_up_swiglu(a, w, *, name):
    m, k = a.shape
    ns, _, c = w.shape
    per = ns // 2
    tm = _tile(m, ROW_TILE)

    def body(a_ref, wg_ref, wu_ref, gu_ref, act_ref):
        av = a_ref[...]
        g = _dot(av, wg_ref[...], 1, 0)
        u = _dot(av, wu_ref[...], 1, 0)
        gu_ref[0] = g
        gu_ref[1] = u
        act_ref[...] = (g * _sigmoid(g) * u).astype(BF16)

    return pl.pallas_call(
        body, grid=(m // tm, per),
        in_specs=[BS((tm, k), lambda i, j: (i, 0)), BS((None, k, c), lambda i, j: (j, 0, 0)),
                  BS((None, k, c), lambda i, j: (per + j, 0, 0))],
        out_specs=[BS((2, tm, c), lambda i, j: (0, i, j)), BS((tm, c), lambda i, j: (i, j))],
        out_shape=[SDS((2, m, per * c), F32), SDS((m, per * c), BF16)],
        compiler_params=_cp(), name=name)(a, w, w)


def _swiglu_bwd(gu, dact, *, name):
    _, r, hid = gu.shape
    tr, tc = _tile(r, 512), _tile(hid, 1408)

    def body(gu_ref, d_ref, o_ref):
        g = gu_ref[0]
        da = d_ref[...]
        sg = _sigmoid(g)
        o_ref[0] = (da * gu_ref[1] * sg * (1.0 + g * (1.0 - sg))).astype(BF16)
        o_ref[1] = (da * g * sg).astype(BF16)

    blk = BS((2, tr, tc), lambda i, j: (0, i, j))
    return pl.pallas_call(
        body, grid=(r // tr, hid // tc), in_specs=[blk, BS((tr, tc), lambda i, j: (i, j))],
        out_specs=blk, out_shape=SDS((2, r, hid), BF16), compiler_params=_cp(), name=name)(gu, dact)


def _loss_grad(h, target, *, name):
    r, d = h.shape
    tr = _tile(r, ROW_TILE)

    def body(h_ref, t_ref, l_ref, g_ref):
        e = h_ref[...] - t_ref[...]
        g_ref[...] = e * (1.0 / d)

        @pl.when(pl.program_id(0) == 0)
        def _():
            l_ref[...] = jnp.zeros_like(l_ref)

        l_ref[...] += jnp.full((1, LANE), 0.5 / d, F32) * jnp.sum(e * e)

    row = BS((tr, d), lambda i: (i, 0))
    return pl.pallas_call(
        body, grid=(r // tr,), in_specs=[row, row],
        out_specs=[BS((1, LANE), lambda i: (0, 0)), row],
        out_shape=[SDS((1, LANE), F32), SDS((r, d), F32)], compiler_params=_cp(), name=name)(h, target)


def _adamw(w, g, m, v, *, name):
    r, c = w.shape
    tr = r if r <= 512 else _rows_tile(r)
    bc1 = 1.0 - ADAM_B1 ** ADAM_STEP
    bc2 = 1.0 - ADAM_B2 ** ADAM_STEP

    def body(w_ref, g_ref, m_ref, v_ref, d_ref, nm_ref, nv_ref):
        gv = g_ref[...]
        nm = ADAM_B1 * m_ref[...] + (1.0 - ADAM_B1) * gv
        nv = ADAM_B2 * v_ref[...] + (1.0 - ADAM_B2) * (gv * gv)
        d_ref[...] = -ADAM_LR * ((nm / bc1) / (jnp.sqrt(nv / bc2) + ADAM_EPS) + ADAM_WD * w_ref[...])
        nm_ref[...] = nm
        nv_ref[...] = nv

    blk = BS((tr, c), lambda i: (i, 0))
    return pl.pallas_call(
        body, grid=(r // tr,), in_specs=[blk] * 4, out_specs=[blk] * 3,
        out_shape=[SDS((r, c), F32)] * 3, compiler_params=_cp(), name=name)(w, g, m, v)


def _rows_tile(r):
    for t in (512, 256, 128, 64, 32, 16, 8):
        if r % t == 0:
            return t
    return r


def _rope_tables(pos, *, name):
    s = pos.shape[0]
    half = ROPE_DIM // 2

    def body(p_ref, c_ref, a_ref, b_ref):
        lane = lax.broadcasted_iota(jnp.int32, (s, ATTN_WIDTH), 1) & (ATTN_HEAD_DIM - 1)
        fi = (lane & (half - 1)).astype(F32)
        inv_freq = jnp.exp(fi * (-2.0 * math.log(ROPE_THETA) / ROPE_DIM))
        ang = p_ref[...].astype(F32) * inv_freq
        cs, sn = jnp.cos(ang), jnp.sin(ang)
        c_ref[...] = jnp.where(lane < ROPE_DIM, cs, 1.0)
        a_ref[...] = jnp.where(lane < half, -sn, 0.0)
        b_ref[...] = jnp.where((lane >= half) & (lane < ROPE_DIM), sn, 0.0)

    full = BS((s, ATTN_WIDTH), lambda i: (0, 0))
    return pl.pallas_call(
        body, grid=(1,), in_specs=[BS((s, 1), lambda i: (0, 0))], out_specs=[full] * 3,
        out_shape=[SDS((s, ATTN_WIDTH), F32)] * 3, compiler_params=_cp(), name=name)(pos)


def _rot(x, c, a, b):
    w = x.shape[1]
    return x * c + pltpu.roll(x, w - ROPE_DIM // 2, 1) * a + pltpu.roll(x, ROPE_DIM // 2, 1) * b


def _rot_t(dy, c, a, b):
    w = dy.shape[1]
    return dy * c + pltpu.roll(dy * a, ROPE_DIM // 2, 1) + pltpu.roll(dy * b, w - ROPE_DIM // 2, 1)


def _attn_count(q0, tq, s):
    dist = (lax.broadcasted_iota(jnp.int32, (tq, s), 0) + q0) - lax.broadcasted_iota(jnp.int32, (tq, s), 1)
    cnt = ((dist <= 128).astype(F32) + (((dist & 3) == 0) & (dist <= 512)).astype(F32)
           + ((dist & 15) == 0).astype(F32))
    return jnp.where(dist >= 0, cnt, 0.0)


ATTN_TQ = 512


def _attn_specs(s, tq):
    def qblk(col):
        return BS((tq, ATTN_WIDTH), lambda i: (i, col))

    def full(col):
        return BS((s, ATTN_WIDTH), lambda i: (0, col))

    return qblk, full


ATTN_TK = 512


def _attn_chunk(i, c, tq, k_ref, v_ref, ck, ak, bk):
    ks = pl.ds(pl.multiple_of(c * ATTN_TK, ATTN_TK), ATTN_TK)
    k = _rot(k_ref[ks, :], ck[ks, :], ak[ks, :], bk[ks, :]).astype(BF16)
    v = v_ref[ks, :].astype(BF16)
    cnt = _attn_count(i * tq - c * ATTN_TK, tq, ATTN_TK)
    return ks, k, v, cnt


def _attn_flash_fwd(proj, tabs, *, name):
    s = proj.shape[0]
    tq = ATTN_TQ
    qblk, full = _attn_specs(s, tq)
    scale = ATTN_HEAD_DIM ** -0.5
    nh = ATTN_HEADS

    def body(q_ref, k_ref, v_ref, cq, aq, bq, ck, ak, bk, o_ref, lse_ref):
        i = pl.program_id(0)
        q = _rot(q_ref[...], cq[...], aq[...], bq[...]) * scale
        head = lax.broadcasted_iota(jnp.int32, (1, ATTN_WIDTH), 1) >> 6
        hms = [(head == h).astype(F32) for h in range(nh)]
        qms = [(q * hm).astype(BF16) for hm in hms]

        def step(c, carry):
            ms, ls, acc = carry
            _, k, v, cnt = _attn_chunk(i, c, tq, k_ref, v_ref, ck, ak, bk)
            valid = cnt > 0.0
            new_ms, new_ls = [], []
            scale_acc = jnp.zeros((tq, ATTN_WIDTH), F32)
            add = jnp.zeros((tq, ATTN_WIDTH), F32)
            for h in range(nh):
                sc = _dot(qms[h], k, 1, 1)
                m_new = jnp.maximum(ms[h], jnp.max(jnp.where(valid, sc, -1e30), axis=-1, keepdims=True))
                alpha = jnp.exp(ms[h] - m_new)
                p = cnt * jnp.exp(jnp.minimum(sc - m_new, 0.0))
                new_ms.append(m_new)
                new_ls.append(alpha * ls[h] + jnp.sum(p, axis=-1, keepdims=True))
                scale_acc = scale_acc + alpha * hms[h]
                add = add + _dot(p, v, 1, 0) * hms[h]
            return new_ms, new_ls, acc * scale_acc + add

        init = ([jnp.full((tq, 1), -1e30, F32)] * nh, [jnp.zeros((tq, 1), F32)] * nh,
                jnp.zeros((tq, ATTN_WIDTH), F32))
        ms, ls, acc = lax.fori_loop(0, i // (ATTN_TK // tq) + 1, step, init)
        inv = jnp.zeros((tq, ATTN_WIDTH), F32)
        lane = lax.broadcasted_iota(jnp.int32, (tq, LANE), 1)
        lse = jnp.zeros((tq, LANE), F32)
        for h in range(nh):
            inv = inv + (1.0 / ls[h]) * hms[h]
            lse = jnp.where(lane == h, ms[h] + jnp.log(ls[h]), lse)
        o_ref[...] = (acc * inv).astype(BF16)
        lse_ref[...] = lse

    return pl.pallas_call(
        body, grid=(s // tq,),
        in_specs=[qblk(0), full(1), full(2), qblk(0), qblk(0), qblk(0), full(0), full(0), full(0)],
        out_specs=[BS((tq, ATTN_WIDTH), lambda i: (i, 0)), BS((tq, LANE), lambda i: (i, 0))],
        out_shape=[SDS((s, ATTN_WIDTH), BF16), SDS((s, LANE), F32)],
        compiler_params=_cp(), name=name)(proj, proj, proj, *tabs, *tabs)


def _attn_flash_bwd(proj, tabs, cat, lse, dcat, *, name):
    s = proj.shape[0]
    tq = ATTN_TQ
    nq = s // tq
    qblk, full = _attn_specs(s, tq)
    scale = ATTN_HEAD_DIM ** -0.5
    nh = ATTN_HEADS

    def body(q_ref, k_ref, v_ref, cq, aq, bq, ck, ak, bk, y_ref, lse_ref, dy_ref,
             dq_ref, dk_ref, dv_ref, dk_acc, dv_acc):
        i = pl.program_id(0)

        @pl.when(i == 0)
        def _():
            dk_acc[...] = jnp.zeros_like(dk_acc)
            dv_acc[...] = jnp.zeros_like(dv_acc)

        q = _rot(q_ref[...], cq[...], aq[...], bq[...]) * scale
        dy = dy_ref[...].astype(F32)
        prod = dy * y_ref[...].astype(F32)
        lse_all = lse_ref[...]
        head = lax.broadcasted_iota(jnp.int32, (1, ATTN_WIDTH), 1) >> 6
        hms = [(head == h).astype(F32) for h in range(nh)]
        qms = [(q * hm).astype(BF16) for hm in hms]
        dyms = [(dy * hm).astype(BF16) for hm in hms]
        deltas = [jnp.sum(prod * hm, axis=-1, keepdims=True) for hm in hms]
        lses = [lse_all[:, h:h + 1] for h in range(nh)]

        def step(c, dq):
            ks, k, v, cnt = _attn_chunk(i, c, tq, k_ref, v_ref, ck, ak, bk)
            dk_c = jnp.zeros((ATTN_TK, ATTN_WIDTH), F32)
            dv_c = jnp.zeros((ATTN_TK, ATTN_WIDTH), F32)
            for h in range(nh):
                sc = _dot(qms[h], k, 1, 1)
                p = cnt * jnp.exp(jnp.minimum(sc - lses[h], 0.0))
                dp = _dot(dyms[h], v, 1, 1)
                ds = p * (dp - deltas[h])
                dq = dq + _dot(ds, k, 1, 0) * hms[h]
                dk_c = dk_c + _dot(ds, qms[h], 0, 0)
                dv_c = dv_c + _dot(p, dyms[h], 0, 0)
            dk_acc[ks, :] += dk_c
            dv_acc[ks, :] += dv_c
            return dq

        dq = lax.fori_loop(0, i // (ATTN_TK // tq) + 1, step, jnp.zeros((tq, ATTN_WIDTH), F32))
        dq_ref[...] = _rot_t(dq * scale, cq[...], aq[...], bq[...]).astype(BF16)

        @pl.when(i == nq - 1)
        def _():
            dk_ref[...] = _rot_t(dk_acc[...], ck[...], ak[...], bk[...]).astype(BF16)
            dv_ref[...] = dv_acc[...].astype(BF16)

    whole = BS((s, ATTN_WIDTH), lambda i: (0, 0))
    return pl.pallas_call(
        body, grid=(nq,),
        in_specs=[qblk(0), full(1), full(2), qblk(0), qblk(0), qblk(0), full(0), full(0), full(0),
                  qblk(0), BS((tq, LANE), lambda i: (i, 0)), qblk(0)],
        out_specs=[BS((tq, ATTN_WIDTH), lambda i: (i, 0)), whole, whole],
        out_shape=[SDS((s, ATTN_WIDTH), BF16)] * 3,
        scratch_shapes=[pltpu.VMEM((s, ATTN_WIDTH), F32), pltpu.VMEM((s, ATTN_WIDTH), F32)],
        compiler_params=_cp(), name=name)(proj, proj, proj, *tabs, *tabs, cat, lse, dcat)


def _shift_down(x, n):
    if n == 0:
        return x
    rows = lax.broadcasted_iota(jnp.int32, x.shape, 0)
    return jnp.where(rows >= n, pltpu.roll(x, n, 0), 0.0)


def _shift_up(x, n):
    if n == 0:
        return x
    t = x.shape[0]
    rows = lax.broadcasted_iota(jnp.int32, x.shape, 0)
    return jnp.where(rows < t - n, pltpu.roll(x, t - n, 0), 0.0)


def _conv_fwd(z, w, kk):
    y = z * w[kk - 1:kk, :]
    for j in range(kk - 1):
        y = y + _shift_down(z, kk - 1 - j) * w[j:j + 1, :]
    return y


def _conv_bwd(z, dy, w, kk):
    dz = dy * w[kk - 1:kk, :]
    dws = []
    for j in range(kk - 1):
        dz = dz + _shift_up(dy, kk - 1 - j) * w[j:j + 1, :]
        dws.append(jnp.sum(dy * _shift_down(z, kk - 1 - j), axis=0, keepdims=True))
    dws.append(jnp.sum(dy * z, axis=0, keepdims=True))
    return dz, jnp.concatenate(dws, axis=0)


def _sconv_fwd(proj, w, *, name):
    s = proj.shape[0]

    def body(b_ref, c_ref, x_ref, w_ref, o_ref):
        y = _conv_fwd(c_ref[...] * x_ref[...], w_ref[...], CONV_K)
        o_ref[...] = (b_ref[...] * y).astype(BF16)

    def col(j):
        return BS((s, LANE), lambda i: (0, j + i))

    return pl.pallas_call(
        body, grid=(CONV_WIDTH // LANE,), in_specs=[col(6), col(8), col(10), BS((CONV_K, LANE), lambda i: (0, i))],
        out_specs=BS((s, LANE), lambda i: (0, i)), out_shape=SDS((s, CONV_WIDTH), BF16),
        compiler_params=_cp(), name=name)(proj, proj, proj, w)


def _sconv_bwd(proj, w, dcat, *, name):
    s = proj.shape[0]

    def body(b_ref, c_ref, x_ref, w_ref, dy_ref, db_ref, dc_ref, dx_ref, dw_ref):
        cv, xv, wv = c_ref[...], x_ref[...], w_ref[...]
        dy = dy_ref[...].astype(F32)
        z = cv * xv
        db_ref[...] = (dy * _conv_fwd(z, wv, CONV_K)).astype(BF16)
        dz, dw = _conv_bwd(z, dy * b_ref[...], wv, CONV_K)
        dc_ref[...] = (dz * xv).astype(BF16)
        dx_ref[...] = (dz * cv).astype(BF16)
        dw_ref[...] = dw

    def col(j):
        return BS((s, LANE), lambda i: (0, j + i))

    out = BS((s, LANE), lambda i: (0, i))
    wspec = BS((CONV_K, LANE), lambda i: (0, i))
    return pl.pallas_call(
        body, grid=(CONV_WIDTH // LANE,), in_specs=[col(6), col(8), col(10), wspec, col(2)],
        out_specs=[out, out, out, wspec],
        out_shape=[SDS((s, CONV_WIDTH), BF16)] * 3 + [SDS((CONV_K, CONV_WIDTH), F32)],
        compiler_params=_cp(), name=name)(proj, proj, proj, w, dcat)


def _l2n(y, scale):
    r = lax.rsqrt(jnp.sum(y * y, axis=-1, keepdims=True) + EPS)
    return y * r * scale, r


def _gdn_pre_fwd(proj, w, *, name):
    s = proj.shape[0]
    nh = GDN_HEADS

    def body(x_ref, w_ref, o_ref):
        j = pl.program_id(0)
        c = _conv_fwd(x_ref[...], w_ref[...], GDN_CONV_K)
        y = c * _sigmoid(c)
        scale = jnp.where(j < nh, GDN_HEAD_DIM ** -0.5, 1.0)
        n, _ = _l2n(y, scale)
        o_ref[...] = jnp.where(j < 2 * nh, n, y)

    return pl.pallas_call(
        body, grid=(3 * nh,),
        in_specs=[BS((s, LANE), lambda j: (0, COL_GDN // LANE + j)), BS((GDN_CONV_K, LANE), lambda j: (0, j))],
        out_specs=BS((s, LANE), lambda j: (0, j)), out_shape=SDS((s, 3 * GDN_WIDTH), F32),
        compiler_params=_cp(), name=name)(proj, w)


def _gdn_pre_bwd(proj, w, dqkv, *, name):
    s = proj.shape[0]
    nh = GDN_HEADS

    def body(x_ref, w_ref, d_ref, dx_ref, dw_ref):
        j = pl.program_id(0)
        xv, wv, dn = x_ref[...], w_ref[...], d_ref[...]
        c = _conv_fwd(xv, wv, GDN_CONV_K)
        sg = _sigmoid(c)
        y = c * sg
        scale = jnp.where(j < nh, GDN_HEAD_DIM ** -0.5, 1.0)
        n, r = _l2n(y, 1.0)
        dns = dn * scale
        dy_norm = r * (dns - n * jnp.sum(dns * n, axis=-1, keepdims=True))
        dy = jnp.where(j < 2 * nh, dy_norm, dn)
        dc = dy * sg * (1.0 + c * (1.0 - sg))
        dx, dw = _conv_bwd(xv, dc, wv, GDN_CONV_K)
        dx_ref[...] = dx.astype(BF16)
        dw_ref[...] = dw

    wspec = BS((GDN_CONV_K, LANE), lambda j: (0, j))
    blk = BS((s, LANE), lambda j: (0, j))
    return pl.pallas_call(
        body, grid=(3 * nh,),
        in_specs=[BS((s, LANE), lambda j: (0, COL_GDN // LANE + j)), wspec, blk],
        out_specs=[blk, wspec], out_shape=[SDS((s, 3 * GDN_WIDTH), BF16), SDS((GDN_CONV_K, 3 * GDN_WIDTH), F32)],
        compiler_params=_cp(), name=name)(proj, w, dqkv)


def _softplus(x):
    return jnp.maximum(x, 0.0) + jnp.log(1.0 + jnp.exp(-jnp.abs(x)))


def _gdn_gates_fwd(proj, a_log, dt_bias, *, name):
    s = proj.shape[0]

    def body(x_ref, al_ref, dt_ref, o_ref):
        xv = x_ref[...]
        lane = lax.broadcasted_iota(jnp.int32, xv.shape, 1)
        g = -jnp.exp(al_ref[...]) * _softplus(xv + dt_ref[...])
        o_ref[...] = jnp.where(lane < GDN_HEADS, g, jnp.where(lane < 2 * GDN_HEADS, _sigmoid(xv), 0.0))

    vec = BS((1, LANE), lambda i: (0, 0))
    return pl.pallas_call(
        body, grid=(1,), in_specs=[BS((s, LANE), lambda i: (0, COL_AB // LANE)), vec, vec],
        out_specs=BS((s, LANE), lambda i: (0, 0)), out_shape=SDS((s, LANE), F32),
        compiler_params=_cp(), name=name)(proj, a_log, dt_bias)


def _gdn_gates_bwd(proj, a_log, dt_bias, dgb, *, name):
    s = proj.shape[0]

    def body(x_ref, al_ref, dt_ref, d_ref, dx_ref, dal_ref, ddt_ref):
        xv, dv = x_ref[...], d_ref[...]
        lane = lax.broadcasted_iota(jnp.int32, xv.shape, 1)
        is_g = lane < GDN_HEADS
        ea = -jnp.exp(al_ref[...])
        z = xv + dt_ref[...]
        da = jnp.where(is_g, dv * ea * _sigmoid(z), 0.0)
        beta = _sigmoid(xv)
        dx_ref[...] = jnp.where(is_g, da, jnp.where(lane < 2 * GDN_HEADS, dv * beta * (1.0 - beta), 0.0)).astype(BF16)
        dal_ref[...] = jnp.sum(jnp.where(is_g, dv * ea * _softplus(z), 0.0), axis=0, keepdims=True)
        ddt_ref[...] = jnp.sum(da, axis=0, keepdims=True)

    vec = BS((1, LANE), lambda i: (0, 0))
    blk = BS((s, LANE), lambda i: (0, 0))
    return pl.pallas_call(
        body, grid=(1,), in_specs=[BS((s, LANE), lambda i: (0, COL_AB // LANE)), vec, vec, blk],
        out_specs=[blk, vec, vec], out_shape=[SDS((s, LANE), BF16), SDS((1, LANE), F32), SDS((1, LANE), F32)],
        compiler_params=_cp(), name=name)(proj, a_log, dt_bias, dgb)


def _col_to_row(col, eye):
    return jnp.sum(jnp.where(eye, col, 0.0), axis=0, keepdims=True)


def _row_to_col(row, eye):
    return jnp.sum(jnp.where(eye, row, 0.0), axis=1, keepdims=True)


GDN_GROUP = 4
TRI_BLOCK_SHIFT = 4


def _gdn_masks(c):
    row = lax.broadcasted_iota(jnp.int32, (c, c), 0)
    col = lax.broadcasted_iota(jnp.int32, (c, c), 1)
    return dict(row=row, col=col, eye=row == col, low=row >= col, strict=row > col, upper=row <= col,
                on_diag=(row >> TRI_BLOCK_SHIFT) == (col >> TRI_BLOCK_SHIFT))


def _tri_inv(a_list, mk):
    eye_f = mk["eye"].astype(F32)
    ds = [jnp.where(mk["on_diag"], a, 0.0) for a in a_list]
    xs = [eye_f - d for d in ds]
    ps = ds
    for _ in range(3):
        ps = [_dot(p, p, 1, 0, precise=True) for p in ps]
        xs = [x + _dot(x, p, 1, 0, precise=True) for x, p in zip(xs, ps)]
    ms = [_dot(x, a - d, 1, 0, precise=True) for x, a, d in zip(xs, a_list, ds)]
    m2s = [_dot(m, m, 1, 0, precise=True) for m in ms]
    ys = [eye_f - m for m in ms]
    ys = [y + _dot(y, m2, 1, 0, precise=True) for y, m2 in zip(ys, m2s)]
    return [_dot(y, x, 1, 0, precise=True) for y, x in zip(ys, xs)]


def _gdn_pre(qs, ks, vs, gs, betas, mk, ts=None):
    c, hd = qs[0].shape
    eye, low = mk["eye"], mk["low"]
    g_rows = [_col_to_row(g, eye) for g in gs]
    d_cols = [jnp.sum(jnp.where(low, gr, 0.0), axis=1, keepdims=True) for gr in g_rows]
    d_rows = [jnp.sum(jnp.where(mk["upper"], g, 0.0), axis=0, keepdims=True) for g in gs]
    rels = [jnp.where(low, jnp.exp(jnp.minimum(dc - dr, 0.0)), 0.0) for dc, dr in zip(d_cols, d_rows)]
    d_lasts = [dc[c - 1:c, :] for dc in d_cols]
    es = [jnp.exp(dc) for dc in d_cols]
    fs = [jnp.exp(dl - dc) for dl, dc in zip(d_lasts, d_cols)]
    cds = [jnp.exp(dl) for dl in d_lasts]
    kbs = [k * b for k, b in zip(ks, betas)]
    kbqs = [jnp.concatenate([kb, q], axis=0) for kb, q in zip(kbs, qs)]
    kqk = [_dot(kbq, k, 1, 1) for kbq, k in zip(kbqs, ks)]
    kks = [x[:c, :] for x in kqk]
    qks = [x[c:, :] for x in kqk]
    if ts is None:
        ts = _tri_inv([jnp.where(mk["strict"], kk * rel, 0.0) for kk, rel in zip(kks, rels)], mk)
    vbs = [v * b for v, b in zip(vs, betas)]
    kbes = [kb * e for kb, e in zip(kbs, es)]
    uws = [_dot(t, jnp.concatenate([vb, kbe], axis=1), 1, 0) for t, vb, kbe in zip(ts, vbs, kbes)]
    out = []
    for i in range(len(qs)):
        out.append(dict(rel=rels[i], e=es[i], f=fs[i], cd=cds[i], kb=kbs[i], kbq=kbqs[i], kk=kks[i], qk=qks[i],
                        t=ts[i], u=uws[i][:, :hd], w=uws[i][:, hd:], uw=uws[i], attn=qks[i] * rels[i],
                        qd=qs[i] * es[i], kd=ks[i] * fs[i]))
    return out


def _gdn_apply(pres, sts, leaving=True):
    c = pres[0]["u"].shape[0]
    wqs = [_dot(jnp.concatenate([p["w"], p["qd"]], axis=0), st, 1, 0) for p, st in zip(pres, sts)]
    vns = [p["u"] - x[:c, :] for p, x in zip(pres, wqs)]
    os_ = [x[c:, :] + _dot(p["attn"], vn, 1, 0) for p, x, vn in zip(pres, wqs, vns)]
    if not leaving:
        return vns, os_, None
    new = [p["cd"] * st + _dot(p["kd"], vn, 0, 0) for p, st, vn in zip(pres, sts, vns)]
    return vns, os_, new


def _gdn_bwd_rest(qs, ks, vs, betas, sts, pres, vns, dos, dvns, dsts, mk):
    c, hd = qs[0].shape
    eye = mk["eye"]
    n = range(len(qs))
    dkds = [_dot(vns[i], dsts[i], 1, 1) for i in n]
    dcds = [jnp.sum(sts[i] * dsts[i]) for i in n]
    dattns = [jnp.where(mk["low"], _dot(dos[i], vns[i], 1, 1), 0.0) for i in n]
    dqdws = [_dot(jnp.concatenate([dos[i], -dvns[i]], axis=0), sts[i], 1, 1) for i in n]
    dqds = [x[:c, :] for x in dqdws]
    dws = [x[c:, :] for x in dqdws]
    dvks = [_dot(pres[i]["t"], jnp.concatenate([dvns[i], dws[i]], axis=1), 0, 0) for i in n]
    das = [jnp.where(mk["strict"], -_dot(dvks[i], pres[i]["uw"], 1, 1), 0.0) for i in n]
    dkqs = [jnp.concatenate([das[i] * pres[i]["rel"], dattns[i] * pres[i]["rel"]], axis=0) for i in n]
    dkbdq = [_dot(dkqs[i], ks[i], 1, 0) for i in n]
    dk0 = [_dot(dkqs[i], pres[i]["kbq"], 0, 0) for i in n]
    out = []
    rows1 = lax.broadcasted_iota(jnp.int32, (c, 1), 0)
    for i in n:
        p = pres[i]
        dvb, dkbe = dvks[i][:, :hd], dvks[i][:, hd:]
        grel = (das[i] * p["kk"] + dattns[i] * p["qk"]) * p["rel"]
        dkb = dkbdq[i][:c, :] + dkbe * p["e"]
        dk = dk0[i] + dkds[i] * p["f"] + dkb * betas[i]
        dq = dkbdq[i][c:, :] + dqds[i] * p["e"]
        dv = dvb * betas[i]
        dbeta = jnp.sum(dkb * ks[i], axis=1, keepdims=True) + jnp.sum(dvb * vs[i], axis=1, keepdims=True)
        de = jnp.sum(dqds[i] * qs[i], axis=1, keepdims=True) + jnp.sum(dkbe * p["kb"], axis=1, keepdims=True)
        dff = jnp.sum(dkds[i] * ks[i], axis=1, keepdims=True) * p["f"]
        dd = (de * p["e"] - dff + jnp.sum(grel, axis=1, keepdims=True)
              - _row_to_col(jnp.sum(grel, axis=0, keepdims=True), eye))
        dd = dd + jnp.where(rows1 == c - 1, jnp.sum(dff) + dcds[i] * p["cd"], 0.0)
        dg = jnp.sum(jnp.where(mk["upper"], _col_to_row(dd, eye), 0.0), axis=1, keepdims=True)
        out.append((dq, dk, dv, dg, dbeta))
    return out


def _gdn_specs(c):
    def qkv(j):
        return BS((c, GDN_WIDTH), lambda n: (n, j))

    return qkv


def _gdn_core_fwd(qkv, gbeta, proj, norm_w, *, name):
    s = qkv.shape[0]
    c, nh, hd, grp = GDN_CHUNK, GDN_HEADS, GDN_HEAD_DIM, GDN_GROUP
    n_chunks = s // c
    blk = _gdn_specs(grp * c)
    inst = [(sub, h) for sub in range(grp) for h in range(nh)]

    def body(q_ref, k_ref, v_ref, gb_ref, gate_ref, nw_ref, y_ref, st_ref, t_ref, state):
        @pl.when(pl.program_id(0) == 0)
        def _():
            state[...] = jnp.zeros_like(state)

        mk = _gdn_masks(c)
        rows = [slice(sub * c, (sub + 1) * c) for sub in range(grp)]
        lanes = [slice(h * hd, (h + 1) * hd) for h in range(nh)]
        gbs = [gb_ref[r, :] for r in rows]
        pres = _gdn_pre([q_ref[rows[sub], lanes[h]] for sub, h in inst], [k_ref[rows[sub], lanes[h]] for sub, h in inst],
                        [v_ref[rows[sub], lanes[h]] for sub, h in inst], [gbs[sub][:, h:h + 1] for sub, h in inst],
                        [gbs[sub][:, nh + h:nh + h + 1] for sub, h in inst], mk)
        sts = [state[ls, :] for ls in lanes]
        outs = []
        for sub in range(grp):
            for h in range(nh):
                st_ref[pl.ds((sub * nh + h) * hd, hd), :] = sts[h]
            _, os_, sts = _gdn_apply(pres[sub * nh:(sub + 1) * nh], sts)
            outs += os_
        for h in range(nh):
            state[lanes[h], :] = sts[h]
        nw = nw_ref[...]
        for i, (sub, h) in enumerate(inst):
            t_ref[pl.ds(i * c, c), :] = pres[i]["t"]
            o = outs[i]
            gate = gate_ref[rows[sub], lanes[h]]
            rs = lax.rsqrt(jnp.mean(o * o, axis=-1, keepdims=True) + EPS)
            y_ref[rows[sub], lanes[h]] = (o * rs * nw * (gate * _sigmoid(gate))).astype(BF16)

    return pl.pallas_call(
        body, grid=(n_chunks // grp,),
        in_specs=[blk(0), blk(1), blk(2), BS((grp * c, LANE), lambda n: (n, 0)),
                  BS((grp * c, GDN_WIDTH), lambda n: (n, COL_GATE // GDN_WIDTH)), BS((1, hd), lambda n: (0, 0))],
        out_specs=[BS((grp * c, GDN_WIDTH), lambda n: (n, 0)), BS((grp * nh * hd, hd), lambda n: (n, 0)),
                   BS((grp * nh * c, c), lambda n: (n, 0))],
        out_shape=[SDS((s, GDN_WIDTH), BF16), SDS((n_chunks * nh * hd, hd), F32), SDS((n_chunks * nh * c, c), F32)],
        scratch_shapes=[pltpu.VMEM((nh * hd, hd), F32)],
        compiler_params=_cp(), name=name)(qkv, qkv, qkv, gbeta, proj, norm_w)


def _gdn_core_bwd(qkv, gbeta, proj, norm_w, states, tinv, dcat, *, name):
    s = qkv.shape[0]
    c, nh, hd, grp = GDN_CHUNK, GDN_HEADS, GDN_HEAD_DIM, GDN_GROUP
    n_chunks = s // c
    last = n_chunks // grp - 1
    inst = [(sub, h) for sub in range(grp) for h in range(nh)]

    def rev(j, w):
        return BS((grp * c, w), lambda n: (last - n, j))

    def body(q_ref, k_ref, v_ref, gb_ref, gate_ref, nw_ref, st_ref, t_ref, dy_ref,
             dqkv_ref, dgb_ref, dgate_ref, dnw_ref, dstate):
        @pl.when(pl.program_id(0) == 0)
        def _():
            dstate[...] = jnp.zeros_like(dstate)
            dnw_ref[...] = jnp.zeros_like(dnw_ref)

        mk = _gdn_masks(c)
        rows = [slice(sub * c, (sub + 1) * c) for sub in range(grp)]
        lanes = [slice(h * hd, (h + 1) * hd) for h in range(nh)]
        gbs = [gb_ref[r, :] for r in rows]
        qs = [q_ref[rows[sub], lanes[h]] for sub, h in inst]
        ks = [k_ref[rows[sub], lanes[h]] for sub, h in inst]
        vs = [v_ref[rows[sub], lanes[h]] for sub, h in inst]
        betas = [gbs[sub][:, nh + h:nh + h + 1] for sub, h in inst]
        sts = [st_ref[pl.ds(i * hd, hd), :] for i in range(len(inst))]
        pres = _gdn_pre(qs, ks, vs, [gbs[sub][:, h:h + 1] for sub, h in inst], betas, mk,
                        ts=[t_ref[pl.ds(i * c, c), :] for i in range(len(inst))])
        vns, outs, _ = _gdn_apply(pres, sts, leaving=False)

        nw = nw_ref[...]
        dnw = jnp.zeros((1, hd), F32)
        dos = []
        for i, (sub, h) in enumerate(inst):
            o = outs[i]
            gate = gate_ref[rows[sub], lanes[h]]
            dy = dy_ref[rows[sub], lanes[h]].astype(F32)
            sg = _sigmoid(gate)
            rs = lax.rsqrt(jnp.mean(o * o, axis=-1, keepdims=True) + EPS)
            nrm = o * rs
            dgate_ref[rows[sub], lanes[h]] = (dy * nrm * nw * sg * (1.0 + gate * (1.0 - sg))).astype(BF16)
            dnv = dy * (gate * sg)
            dnw = dnw + jnp.sum(dnv * nrm, axis=0, keepdims=True)
            dno = dnv * nw
            dos.append(rs * (dno - nrm * jnp.mean(dno * nrm, axis=-1, keepdims=True)))
        dnw_ref[...] += dnw

        from_o = [_dot(p["attn"], do, 0, 0) for p, do in zip(pres, dos)]
        to_st = [_dot(p["qd"], do, 0, 0) for p, do in zip(pres, dos)]
        dst = [dstate[ls, :] for ls in lanes]
        dsts = [None] * len(inst)
        dvns = [None] * len(inst)
        for sub in reversed(range(grp)):
            idx = [sub * nh + h for h in range(nh)]
            for h, i in enumerate(idx):
                dsts[i] = dst[h]
                dvns[i] = from_o[i] + _dot(pres[i]["kd"], dst[h], 1, 0)
            dst = [pres[i]["cd"] * dst[h] + to_st[i] - _dot(pres[i]["w"], dvns[i], 0, 0) for h, i in enumerate(idx)]
        for h in range(nh):
            dstate[lanes[h], :] = dst[h]

        grads = _gdn_bwd_rest(qs, ks, vs, betas, sts, pres, vns, dos, dvns, dsts, mk)
        lane = lax.broadcasted_iota(jnp.int32, (c, LANE), 1)
        dgb = [jnp.zeros((c, LANE), F32) for _ in range(grp)]
        for (sub, h), (dq, dk, dv, dg, dbeta) in zip(inst, grads):
            dqkv_ref[rows[sub], lanes[h]] = dq
            dqkv_ref[rows[sub], slice(GDN_WIDTH + h * hd, GDN_WIDTH + (h + 1) * hd)] = dk
            dqkv_ref[rows[sub], slice(2 * GDN_WIDTH + h * hd, 2 * GDN_WIDTH + (h + 1) * hd)] = dv
            dgb[sub] = jnp.where(lane == h, dg, jnp.where(lane == nh + h, dbeta, dgb[sub]))
        for sub in range(grp):
            dgb_ref[rows[sub], :] = dgb[sub]

    return pl.pallas_call(
        body, grid=(n_chunks // grp,),
        in_specs=[rev(0, GDN_WIDTH), rev(1, GDN_WIDTH), rev(2, GDN_WIDTH), rev(0, LANE),
                  rev(COL_GATE // GDN_WIDTH, GDN_WIDTH), BS((1, hd), lambda n: (0, 0)),
                  BS((grp * nh * hd, hd), lambda n: (last - n, 0)), BS((grp * nh * c, c), lambda n: (last - n, 0)),
                  rev(1, GDN_WIDTH)],
        out_specs=[rev(0, 3 * GDN_WIDTH), rev(0, LANE), rev(0, GDN_WIDTH), BS((1, hd), lambda n: (0, 0))],
        out_shape=[SDS((s, 3 * GDN_WIDTH), F32), SDS((s, LANE), F32), SDS((s, GDN_WIDTH), BF16), SDS((1, hd), F32)],
        scratch_shapes=[pltpu.VMEM((nh * hd, hd), F32)],
        compiler_params=_cp(), name=name)(qkv, qkv, qkv, gbeta, proj, norm_w, states, tinv, dcat)


XATTN_TQ = 512


def _xattn_probs(qh, kh):
    sc = _dot(qh, kh, 1, 1) * (XATTN_HEAD_DIM ** -0.5)
    p = jnp.exp(sc - jnp.max(sc, axis=-1, keepdims=True))
    return p / jnp.sum(p, axis=-1, keepdims=True)


def _xattn_fwd(q, kv, *, name):
    s, d = q.shape
    m = kv.shape[0]
    tq, hd = _tile(s, XATTN_TQ), XATTN_HEAD_DIM

    def body(q_ref, k_ref, v_ref, o_ref):
        for h in range(XATTN_HEADS):
            ls = slice(h * hd, (h + 1) * hd)
            p = _xattn_probs(q_ref[:, ls], k_ref[:, ls])
            o_ref[:, ls] = _dot(p, v_ref[:, ls], 1, 0).astype(BF16)

    return pl.pallas_call(
        body, grid=(s // tq,),
        in_specs=[BS((tq, d), lambda i: (i, 0)), BS((m, d), lambda i: (0, 0)), BS((m, d), lambda i: (0, 1))],
        out_specs=BS((tq, d), lambda i: (i, 0)), out_shape=SDS((s, d), BF16),
        compiler_params=_cp(), name=name)(q, kv, kv)


def _xattn_bwd(q, kv, do, *, name):
    s, d = q.shape
    m = kv.shape[0]
    tq, hd = _tile(s, XATTN_TQ), XATTN_HEAD_DIM
    scale = hd ** -0.5

    def body(q_ref, k_ref, v_ref, do_ref, dq_ref, dkv_ref):
        @pl.when(pl.program_id(0) == 0)
        def _():
            dkv_ref[...] = jnp.zeros_like(dkv_ref)

        for h in range(XATTN_HEADS):
            ls = slice(h * hd, (h + 1) * hd)
            vs = slice(d + h * hd, d + (h + 1) * hd)
            qh, kh, doh = q_ref[:, ls], k_ref[:, ls], do_ref[:, ls]
            p = _xattn_probs(qh, kh)
            dp = _dot(doh, v_ref[:, ls], 1, 1)
            ds = p * (dp - jnp.sum(p * dp, axis=-1, keepdims=True)) * scale
            dq_ref[:, ls] = _dot(ds, kh, 1, 0).astype(BF16)
            dkv_ref[:, ls] += _dot(ds, qh, 0, 0)
            dkv_ref[:, vs] += _dot(p, doh, 0, 0)

    row = BS((tq, d), lambda i: (i, 0))
    return pl.pallas_call(
        body, grid=(s // tq,),
        in_specs=[row, BS((m, d), lambda i: (0, 0)), BS((m, d), lambda i: (0, 1)), row],
        out_specs=[row, BS((m, 2 * d), lambda i: (0, 0))],
        out_shape=[SDS((s, d), BF16), SDS((m, 2 * d), F32)],
        compiler_params=_cp(), name=name)(q, kv, kv, do)


def _pad_lanes(vec4):
    return jnp.zeros((1, LANE), F32).at[0, :GDN_HEADS].set(vec4)


def _layer_fwd(h0, mem, tabs, p):
    sv = dict(h0=h0)
    hn1 = _rmsnorm(h0, p["norm_mix_pre"], name="norm_mix_pre")
    proj = _mm(hn1, p["w_in"], name="mm_in")
    ya, lse = _attn_flash_fwd(proj, tabs, name="attn_fwd")
    yc = _sconv_fwd(proj, p["conv_short"], name="sconv_fwd")
    qkv = _gdn_pre_fwd(proj, p["conv_gdn"], name="gdn_pre_fwd")
    gbeta = _gdn_gates_fwd(proj, p["gdn_a_log"], p["gdn_dt_bias"], name="gdn_gates_fwd")
    yg, states, tinv = _gdn_core_fwd(qkv, gbeta, proj, p["gdn_norm"], name="gdn_core_fwd")
    cat = jnp.concatenate([ya, yc, yg], axis=-1)
    mix = _mm(cat, p["w_out"], name="mm_out")
    h1 = _resnorm(h0, mix, p["norm_mix_post"], name="norm_mix_post")
    hn2 = _rmsnorm(h1, p["norm_xattn_pre"], name="norm_xattn_pre")
    memn = _rmsnorm(mem, p["norm_mem"], name="norm_mem")
    xq = _mm(hn2, p["w_xq"], out_dtype=BF16, name="mm_xq")
    kv = _mm(memn, p["w_xkv"], out_dtype=BF16, b_shards=True, name="mm_xkv")
    xo = _xattn_fwd(xq, kv, name="xattn_fwd")
    xa = _mm(xo, p["w_xo"], name="mm_xo")
    h2 = _resnorm(h1, xa, p["norm_xattn_post"], name="norm_xattn_post")
    hn3 = _rmsnorm(h2, p["norm_ffn_pre"], name="norm_ffn_pre")
    gu, act = _gate_up_swiglu(hn3, p["w_gate_up"], name="mm_gate_up")
    f = _mm(act, p["w_down"], name="mm_down")
    h3 = _resnorm(h2, f, p["norm_ffn_post"], name="norm_ffn_post")
    sv.update(hn1=hn1, proj=proj, lse=lse, qkv=qkv, gbeta=gbeta, states=states, tinv=tinv, cat=cat, mix=mix, h1=h1, hn2=hn2,
              memn=memn, xq=xq, kv=kv, xo=xo, xa=xa, h2=h2, hn3=hn3, gu=gu, act=act, f=f)
    return h3, sv


def _layer_bwd(dh3, mem, tabs, p, sv, after_ffn=None, after_xattn=None):
    g = {}
    df, g["norm_ffn_post"] = _rmsnorm_bwd(sv["f"], p["norm_ffn_post"], dh3, name="norm_ffn_post_bwd")
    dact = _mm(df, p["w_down"], tb=True, name="mm_down_da")
    g["w_down"] = _mm(sv["act"], df, ta=True, name="mm_down_dw")
    dgu = _swiglu_bwd(sv["gu"], dact, name="swiglu_bwd")
    dhn3 = _mm(dgu, p["w_gate_up"], tb=True, b_shards=True, halves=True, name="mm_gate_up_da")
    g["w_gate_up"] = _mm(sv["hn3"], dgu, ta=True, out_shards=True, halves=True, name="mm_gate_up_dw")
    dh2, g["norm_ffn_pre"] = _rmsnorm_bwd(sv["h2"], p["norm_ffn_pre"], dhn3, res=dh3, name="norm_ffn_pre_bwd")
    token = after_ffn(dh2) if after_ffn is not None else None
    w_post = p["norm_xattn_post"] if token is None else p["norm_xattn_post"] + token[:1, :1]
    dxa, g["norm_xattn_post"] = _rmsnorm_bwd(sv["xa"], w_post, dh2, name="norm_xattn_post_bwd")
    dxo = _mm(dxa, p["w_xo"], tb=True, name="mm_xo_da")
    g["w_xo"] = _mm(sv["xo"], dxa, ta=True, name="mm_xo_dw")
    dxq, dkv = _xattn_bwd(sv["xq"], sv["kv"], dxo, name="xattn_bwd")
    dhn2 = _mm(dxq, p["w_xq"], tb=True, name="mm_xq_da")
    g["w_xq"] = _mm(sv["hn2"], dxq, ta=True, name="mm_xq_dw")
    dmemn = _mm(dkv, p["w_xkv"], tb=True, b_shards=True, name="mm_xkv_da")
    g["w_xkv"] = _mm(sv["memn"], dkv, ta=True, out_shards=True, name="mm_xkv_dw")
    _, g["norm_mem"] = _rmsnorm_bwd(mem, p["norm_mem"], dmemn, name="norm_mem_bwd")
    dh1, g["norm_xattn_pre"] = _rmsnorm_bwd(sv["h1"], p["norm_xattn_pre"], dhn2, res=dh2, name="norm_xattn_pre_bwd")
    token = after_xattn(g, dh1) if after_xattn is not None else None
    w_post = p["norm_mix_post"] if token is None else p["norm_mix_post"] + token[:1, :1]
    dmix, g["norm_mix_post"] = _rmsnorm_bwd(sv["mix"], w_post, dh1, name="norm_mix_post_bwd")
    dcat = _mm(dmix, p["w_out"], tb=True, name="mm_out_da")
    g["w_out"] = _mm(sv["cat"], dmix, ta=True, name="mm_out_dw")
    proj = sv["proj"]
    daq, dak, dav = _attn_flash_bwd(proj, tabs, sv["cat"], sv["lse"], dcat, name="attn_bwd")
    dcb, dcc, dcx, g["conv_short"] = _sconv_bwd(proj, p["conv_short"], dcat, name="sconv_bwd")
    dqkv, dgbeta, dgate, g["gdn_norm"] = _gdn_core_bwd(sv["qkv"], sv["gbeta"], proj, p["gdn_norm"], sv["states"], sv["tinv"],
                                                        dcat, name="gdn_core_bwd")
    dgqkv, g["conv_gdn"] = _gdn_pre_bwd(proj, p["conv_gdn"], dqkv, name="gdn_pre_bwd")
    dab, g["gdn_a_log"], g["gdn_dt_bias"] = _gdn_gates_bwd(proj, p["gdn_a_log"], p["gdn_dt_bias"], dgbeta,
                                                          name="gdn_gates_bwd")
    s = proj.shape[0]
    dproj = jnp.concatenate([daq, dak, dav, dcb, dcc, dcx, dgqkv, dgate, dab,
                             jnp.zeros((s, IN_PAD - COL_AB - LANE), BF16)], axis=-1)
    dhn1 = _mm(dproj, p["w_in"], tb=True, name="mm_in_da")
    g["w_in"] = _mm(sv["hn1"], dproj, ta=True, name="mm_in_dw")
    dh0, g["norm_mix_pre"] = _rmsnorm_bwd(sv["h0"], p["norm_mix_pre"], dhn1, res=dh1, name="norm_mix_pre_bwd")
    return dh0, g


MATRICES = ("w_in", "w_out", "w_xq", "w_xkv", "w_xo", "w_gate_up", "w_down")
VECTORS = ("norm_mix_pre", "norm_mix_post", "conv_short", "conv_gdn", "gdn_a_log", "gdn_dt_bias", "gdn_norm",
           "norm_mem", "norm_xattn_pre", "norm_xattn_post", "norm_ffn_pre", "norm_ffn_post")


def _w_in_segments(n_shards=4):
    c = IN_WIDTH // n_shards
    moves = ((0, COL_GATE, 0), (COL_GATE, COL_GATE + 8, COL_AB), (COL_GATE + 8, IN_WIDTH, COL_GATE))
    segs = []
    for s in range(n_shards):
        for lo, hi, dst in moves:
            a, b = max(lo, s * c), min(hi, (s + 1) * c)
            if a < b:
                segs.append((s, a - s * c, dst + a - lo, b - a))
    return segs


def _w_in_pack(g, *, name):
    ns, r, c = g.shape
    tr = _tile(r, 256)

    def body(g_ref, o_ref):
        o_ref[:, IN_WIDTH:] = jnp.zeros((tr, IN_PAD - IN_WIDTH), o_ref.dtype)
        for s, src, dst, width in _w_in_segments(ns):
            o_ref[:, dst:dst + width] = g_ref[s, :, src:src + width]

    return pl.pallas_call(
        body, grid=(r // tr,), in_specs=[BS((ns, tr, c), lambda i: (0, i, 0))],
        out_specs=BS((tr, IN_PAD), lambda i: (i, 0)), out_shape=SDS((r, IN_PAD), g.dtype),
        compiler_params=_cp(), name=name)(g)


def _w_in_unpack(dw, *, name):
    r = dw.shape[0]
    ns, c = 4, IN_WIDTH // 4
    tr = _tile(r, 256)

    def body(d_ref, o_ref):
        for s, src, dst, width in _w_in_segments(ns):
            o_ref[s, :, src:src + width] = d_ref[:, dst:dst + width]

    return pl.pallas_call(
        body, grid=(r // tr,), in_specs=[BS((tr, IN_PAD), lambda i: (i, 0))],
        out_specs=BS((ns, tr, c), lambda i: (0, i, 0)), out_shape=SDS((ns, r, c), dw.dtype),
        compiler_params=_cp(), name=name)(dw)


def _layer_params(full, l):
    p = {n: full[n][l] for n in MATRICES}
    for n in VECTORS:
        v = full[n][l]
        if n in ("gdn_a_log", "gdn_dt_bias"):
            p[n] = _pad_lanes(v)
        elif v.ndim == 1:
            p[n] = v.reshape(1, -1)
        else:
            p[n] = v
    return p


def _local_step(x, mem, pos, target, full, matrices_for=None, on_grads=None, mid_backward=None,
                after_xattn=None):
    tabs = _rope_tables(pos, name="rope_tables")
    h = x
    saved, params = [], []
    for l in range(DEPTH):
        if matrices_for is not None:
            full = {**full, **{n: {l: v} for n, v in matrices_for(l, h).items()}}
        p = _layer_params(full, l)
        h, sv = _layer_fwd(h, mem, tabs, p)
        params.append(p)
        saved.append(sv)
    loss_row, dh = _loss_grad(h, target, name="loss_grad")
    grads = [None] * DEPTH
    token = None
    for l in reversed(range(DEPTH)):
        p = params[l]
        if token is not None:
            p = {**p, "norm_ffn_post": p["norm_ffn_post"] + token[:1, :1]}
        late = None if after_xattn is None else (lambda g, dh1, l=l: after_xattn(l, g, dh1))
        dh, grads[l] = _layer_bwd(dh, mem, tabs, p, saved[l], after_ffn=mid_backward, after_xattn=late)
        if on_grads is not None:
            token = on_grads(l, grads[l], dh)
    return loss_row, dh, grads


ANY = pl.BlockSpec(memory_space=pl.ANY)
MESH = pl.DeviceIdType.MESH


def _flip(pos, mask):
    return tuple(1 - v if m else v for v, m in zip(pos, mask))


def _exchange(ins, out_shapes, remote, local, *, name):
    n_in = len(ins)
    n_out = len(out_shapes)

    def at(ref, idx):
        return ref.at[idx] if idx else ref

    def body(*refs):
        in_refs = refs[:n_in]
        out_refs = refs[n_in:n_in + n_out]
        send_sems, recv_sems, local_sems = refs[n_in + n_out:]
        me = (lax.axis_index("x"), lax.axis_index("y"), lax.axis_index("c"))
        waits = []
        for k, (ii, src_at, oi, dst_at, mask) in enumerate(remote):
            peer = _flip(me, mask)
            pltpu.make_async_remote_copy(
                src_ref=at(in_refs[ii], src_at(me, peer)), dst_ref=at(out_refs[oi], dst_at(me)),
                send_sem=send_sems.at[k], recv_sem=recv_sems.at[k], device_id=peer, device_id_type=MESH).start()
            waits.append(pltpu.make_async_remote_copy(
                src_ref=at(in_refs[ii], src_at(peer, me)), dst_ref=at(out_refs[oi], dst_at(peer)),
                send_sem=send_sems.at[k], recv_sem=recv_sems.at[k], device_id=peer, device_id_type=MESH))
        own = []
        for k, (ii, src_at, oi, dst_at) in enumerate(local):
            cp = pltpu.make_async_copy(at(in_refs[ii], src_at(me)), at(out_refs[oi], dst_at(me)), local_sems.at[k])
            cp.start()
            own.append(cp)
        for w in waits:
            w.wait_send()
            w.wait_recv()
        for cp in own:
            cp.wait()

    return pl.pallas_call(
        body, in_specs=[ANY] * n_in, out_specs=[ANY] * n_out, out_shape=list(out_shapes),
        scratch_shapes=[pltpu.SemaphoreType.DMA((len(remote),)), pltpu.SemaphoreType.DMA((len(remote),)),
                        pltpu.SemaphoreType.DMA((max(len(local), 1),))],
        name=name)(*ins)


HBM = pl.BlockSpec(memory_space=pltpu.HBM)
SEM = pl.BlockSpec(memory_space=pltpu.SEMAPHORE)
SPLIT_EFFECT = pltpu.SideEffectType.DATAFLOW_SIDE_EFFECTING


def _exchange_start(ins, land_shapes, remote, *, name):
    n_in, n_land, n_cp = len(ins), len(land_shapes), len(remote)

    def body(*refs):
        in_refs, land_refs = refs[:n_in], refs[n_in:n_in + n_land]
        send_sems, recv_sems = refs[n_in + n_land:n_in + n_land + 2]
        token = refs[-1]
        me = (lax.axis_index("x"), lax.axis_index("y"), lax.axis_index("c"))
        for k, (ii, src_at, oi, dst_at, mask) in enumerate(remote):
            peer = _flip(me, mask)
            idx_s, idx_d = src_at(me, peer), dst_at(me)
            pltpu.make_async_remote_copy(
                src_ref=in_refs[ii].at[idx_s] if idx_s else in_refs[ii],
                dst_ref=land_refs[oi].at[idx_d] if idx_d else land_refs[oi],
                send_sem=send_sems.at[k], recv_sem=recv_sems.at[k], device_id=peer, device_id_type=MESH).start()
        token[...] = jnp.zeros_like(token)

    buffers = [pltpu.with_memory_space_constraint(a, pltpu.HBM) for a in ins]
    buffers += [pltpu.with_memory_space_constraint(lax.empty(s.shape, s.dtype), pltpu.HBM) for s in land_shapes]
    out = pl.pallas_call(
        body, name=name,
        out_shape=(pltpu.SemaphoreType.DMA((n_cp,)), pltpu.SemaphoreType.DMA((n_cp,)),
                   *[pltpu.HBM(b.shape, b.dtype) for b in buffers], SDS((8, LANE), F32)),
        in_specs=[HBM] * len(buffers),
        out_specs=(SEM, SEM, *[HBM] * len(buffers), pl.BlockSpec(memory_space=pltpu.VMEM)),
        input_output_aliases={i: 2 + i for i in range(len(buffers))},
        compiler_params=pltpu.CompilerParams(has_side_effects=SPLIT_EFFECT))(*buffers)
    return out[0], out[1], list(out[2:2 + n_in]), list(out[2 + n_in:2 + n_in + n_land]), out[-1]


def _exchange_wait(send_sems, recv_sems, ins, lands, remote, after, *, name):
    n_in, n_land = len(ins), len(lands)

    def body(*refs):
        in_refs, land_refs = refs[:n_in], refs[n_in:n_in + n_land]
        send_sems_, recv_sems_ = refs[n_in + n_land:n_in + n_land + 2]
        me = (lax.axis_index("x"), lax.axis_index("y"), lax.axis_index("c"))
        for k, (ii, src_at, oi, dst_at, mask) in enumerate(remote):
            peer = _flip(me, mask)
            idx_s, idx_d = src_at(peer, me), dst_at(peer)
            cp = pltpu.make_async_remote_copy(
                src_ref=in_refs[ii].at[idx_s] if idx_s else in_refs[ii],
                dst_ref=land_refs[oi].at[idx_d] if idx_d else land_refs[oi],
                send_sem=send_sems_.at[k], recv_sem=recv_sems_.at[k], device_id=peer, device_id_type=MESH)
            cp.wait_send()
            cp.wait_recv()

    buffers = list(ins) + list(lands)
    out = pl.pallas_call(
        body, name=name, out_shape=tuple(pltpu.HBM(b.shape, b.dtype) for b in buffers),
        in_specs=[HBM] * len(buffers) + [SEM, SEM, ANY], out_specs=tuple([HBM] * len(buffers)),
        input_output_aliases={i: i for i in range(len(buffers))},
        compiler_params=pltpu.CompilerParams(has_side_effects=SPLIT_EFFECT))(*buffers, send_sems, recv_sems, after)
    return list(out[:n_in]), list(out[n_in:])


def _chip(pos):
    return 2 * pos[0] + pos[1]


XY_MASKS = ((1, 0, 0), (0, 1, 0), (1, 1, 0))
SIBLING = (0, 0, 1)
ALL_MASKS = tuple((a, b, c) for a in (0, 1) for b in (0, 1) for c in (0, 1))[1:]


def _gather_xy(arrs, *, name):
    n = len(arrs)
    outs = [SDS((4,) + a.shape, a.dtype) for a in arrs]
    halves = [a.shape[0] // 2 for a in arrs]

    def body(*refs):
        in_refs, out_refs = refs[:n], refs[n:2 * n]
        ici_send, ici_recv, d2d_send, d2d_recv = refs[2 * n:]
        me = (lax.axis_index("x"), lax.axis_index("y"), lax.axis_index("c"))
        sibling = _flip(me, SIBLING)
        flows = []
        for i in range(n):
            mine = pl.ds(me[2] * halves[i], halves[i])
            other = pl.ds(sibling[2] * halves[i], halves[i])
            for m in XY_MASKS:
                k = len(flows)
                peer = _flip(me, m)

                def remote(src, dst, sems, to, k=k):
                    return pltpu.make_async_remote_copy(src_ref=src, dst_ref=dst, send_sem=sems[0].at[k],
                                                        recv_sem=sems[1].at[k], device_id=to, device_id_type=MESH)

                landed = out_refs[i].at[_chip(peer), mine]
                send = remote(in_refs[i].at[mine], out_refs[i].at[_chip(me), mine], (ici_send, ici_recv), peer)
                send.start()
                arrive = remote(in_refs[i].at[mine], landed, (ici_send, ici_recv), peer)
                forward = remote(landed, landed, (d2d_send, d2d_recv), sibling)
                handed = remote(out_refs[i].at[_chip(peer), other], out_refs[i].at[_chip(peer), other],
                                (d2d_send, d2d_recv), sibling)
                flows.append((send, arrive, forward, handed))
        for _, arrive, forward, _ in flows:
            arrive.wait_recv()
            forward.start()
        for send, _, forward, handed in flows:
            handed.wait_recv()
            send.wait_send()
            forward.wait_send()

    n_flows = 3 * n
    return pl.pallas_call(
        body, in_specs=[ANY] * n, out_specs=[ANY] * n, out_shape=outs,
        scratch_shapes=[pltpu.SemaphoreType.DMA((n_flows,))] * 4, name=name)(*arrs)


def _gather_all(arr, *, name):
    slot = lambda pos: (4 * pos[0] + 2 * pos[1] + pos[2],)
    whole = lambda *_: ()
    remote = [(0, whole, 0, slot, m) for m in ALL_MASKS]
    return _exchange([arr], [SDS((8,) + arr.shape, arr.dtype)], remote, [(0, whole, 0, slot)], name=name)[0]


def _send_to_sibling(arrs, *, name):
    outs = [SDS(a.shape, a.dtype) for a in arrs]
    whole = lambda *_: ()
    remote = [(i, whole, i, whole, SIBLING) for i in range(len(arrs))]
    return _exchange(arrs, outs, remote, [], name=name)


def _add_half(g, other, core, *, name):
    n4, nl, r, c = g.shape
    half = r // 2
    g3 = g.reshape(n4 * nl, 2, half, c)
    o3 = other.reshape(n4 * nl, half, c)
    tr = _rows_tile(half) if half > 512 else half

    def body(core_ref, g_ref, o_ref, out_ref):
        out_ref[...] = (g_ref[...] + o_ref[...]).astype(BF16)

    return pl.pallas_call(
        body,
        grid_spec=pltpu.PrefetchScalarGridSpec(
            num_scalar_prefetch=1, grid=(n4 * nl, half // tr),
            in_specs=[BS((None, None, tr, c), lambda i, j, core_ref: (i, core_ref[0], j, 0)),
                      BS((None, tr, c), lambda i, j, core_ref: (i, j, 0))],
            out_specs=BS((None, tr, c), lambda i, j, core_ref: (i, j, 0))),
        out_shape=SDS((n4 * nl, half, c), BF16), compiler_params=_cp(), name=name)(core, g3, o3).reshape(n4, nl, half, c)


def _sum_chips(parts, mine, chip, *, name):
    n4, nl, h, c = mine.shape
    tr = _rows_tile(h) if h > 512 else h

    def body(chip_ref, p_ref, own_ref, out_ref):
        me = chip_ref[0]
        own = own_ref[...].astype(F32)
        across = [p_ref[j].astype(F32) for j in range(len(XY_MASKS))]
        t = []
        for s in range(n4):
            rel = s ^ me
            t.append(jnp.where(rel == 0, own, jnp.where(rel == 2, across[0], jnp.where(rel == 1, across[1], across[2]))))
        out_ref[...] = ((t[0] + t[1]) + t[2]) + t[3]

    return pl.pallas_call(
        body,
        grid_spec=pltpu.PrefetchScalarGridSpec(
            num_scalar_prefetch=1, grid=(nl, h // tr),
            in_specs=[BS((len(XY_MASKS), None, tr, c), lambda i, j, chip_ref: (0, i, j, 0)),
                      BS((None, None, tr, c), lambda i, j, chip_ref: (chip_ref[0], i, j, 0))],
            out_specs=BS((None, tr, c), lambda i, j, chip_ref: (i, j, 0))),
        out_shape=SDS((nl, h, c), F32), compiler_params=_cp(), name=name)(chip, parts, mine)


def _sum_devices(parts, *, name):
    n, r, c = parts.shape

    def body(p_ref, out_ref):
        acc = p_ref[0]
        for d in range(1, n):
            acc = acc + p_ref[d]
        out_ref[...] = acc

    return pl.pallas_call(
        body, grid=(1,), in_specs=[BS((n, r, c), lambda i: (0, 0, 0))], out_specs=BS((r, c), lambda i: (0, 0)),
        out_shape=SDS((r, c), F32), compiler_params=_cp(), name=name)(parts)


WEIGHTS = ("norm_mix_pre", "norm_mix_post", "w_in", "conv_short", "conv_gdn", "gdn_a_log", "gdn_dt_bias",
           "gdn_norm", "w_out", "norm_mem", "norm_xattn_pre", "norm_xattn_post", "w_xq", "w_xkv", "w_xo",
           "norm_ffn_pre", "norm_ffn_post", "w_gate_up", "w_down")
COL_SHARDED = ("w_in", "w_xkv", "w_gate_up", "conv_short", "conv_gdn")
ROW_SHARDED = ("w_out", "w_xq", "w_xo", "w_down")
SMALL_SHARDED = ("conv_short", "conv_gdn")
SMALL_ROW_PAD = 8


KEPT_AS_SHARDS = ("w_xkv", "w_gate_up")


def _from_shards(n, g):
    if n in KEPT_AS_SHARDS:
        return g
    if n == "w_in":
        return _w_in_pack(g, name="w_in_pack")
    if n in COL_SHARDED:
        t = jnp.moveaxis(g, 0, -2)
        return t.reshape(t.shape[:-2] + (-1,))
    return g.reshape(-1, g.shape[-1])


def _to_shards(n, g):
    if n in KEPT_AS_SHARDS:
        return g
    if n == "w_in":
        return _w_in_unpack(g, name="w_in_unpack")
    return g.reshape(4, -1, g.shape[-1])


def _pack_small(grads):
    rows = []
    for g in grads:
        for n in WEIGHTS:
            if n not in MATRICES:
                part = g[n].reshape(-1, LANE)
                rows.append(jnp.pad(part, ((0, -part.shape[0] % SMALL_ROW_PAD), (0, 0))))
    return jnp.concatenate(rows, axis=0)


def _unpack_small(packed, like):
    out, at = [], 0
    for _ in range(DEPTH):
        g = {}
        for n in WEIGHTS:
            if n not in MATRICES:
                shape = like[n].shape
                k = math.prod(shape) // LANE
                g[n] = packed[at:at + k].reshape(shape)
                at += k + (-k % SMALL_ROW_PAD)
        out.append(g)
    return out


def kernel(x, mem, positions, norm_mix_pre, norm_mix_post, w_in, conv_short, conv_gdn, gdn_a_log, gdn_dt_bias, gdn_norm, w_out, norm_mem, norm_xattn_pre, norm_xattn_post, w_xq, w_xkv, w_xo, norm_ffn_pre, norm_ffn_post, w_gate_up, w_down, loss_target, m_norm_mix_pre, m_norm_mix_post, m_w_in, m_conv_short, m_conv_gdn, m_gdn_a_log, m_gdn_dt_bias, m_gdn_norm, m_w_out, m_norm_mem, m_norm_xattn_pre, m_norm_xattn_post, m_w_xq, m_w_xkv, m_w_xo, m_norm_ffn_pre, m_norm_ffn_post, m_w_gate_up, m_w_down, v_norm_mix_pre, v_norm_mix_post, v_w_in, v_conv_short, v_conv_gdn, v_gdn_a_log, v_gdn_dt_bias, v_gdn_norm, v_w_out, v_norm_mem, v_norm_xattn_pre, v_norm_xattn_post, v_w_xq, v_w_xkv, v_w_xo, v_norm_ffn_pre, v_norm_ffn_post, v_w_gate_up, v_w_down):
    args = dict(locals())
    w = {n: args[n] for n in WEIGHTS}
    m = {n: args["m_" + n] for n in WEIGHTS}
    v = {n: args["v_" + n] for n in WEIGHTS}
    seq = x.shape[1]
    chip = 2 * lax.axis_index("x") + lax.axis_index("y")
    core = lax.axis_index("c").astype(jnp.int32).reshape(1)

    def cast(n, l):
        return w[n][l].astype(BF16)

    first = [cast(n, 0) for n in MATRICES] + [w[n] for n in SMALL_SHARDED]
    blocks = _gather_xy(first, name="gather_weights")
    blocks = [lax.dynamic_update_index_in_dim(b, o, chip, axis=0) for b, o in zip(blocks, first)]
    full = {n: _from_shards(n, b) for n, b in zip(list(MATRICES) + list(SMALL_SHARDED), blocks)}
    layer0 = {n: full.pop(n) for n in MATRICES}
    for n in WEIGHTS:
        if n not in full and n not in MATRICES:
            full[n] = w[n]
    whole = lambda *_: ()
    gather_plan = [(i, whole, i, (lambda sender: (_chip(sender),)), mask)
                   for i in range(len(MATRICES)) for mask in XY_MASKS]
    in_flight = {}
    started = jnp.zeros((1, 1), F32)
    for l in range(1, DEPTH):
        own = [cast(n, l) for n in MATRICES]
        out = _exchange_start(own, [SDS((4,) + o.shape, o.dtype) for o in own], gather_plan, name=f"gather_start_{l}")
        in_flight[l] = (own,) + out[:4]
        started = started + out[4][:1, :1]
    full["norm_mix_pre"] = full["norm_mix_pre"] + started

    def matrices_for(l, h):
        if l == 0:
            return layer0
        own, send_sems, recv_sems, thru, lands = in_flight.pop(l)
        _, lands = _exchange_wait(send_sems, recv_sems, thru, lands, gather_plan, h, name=f"gather_wait_{l}")
        lands = [lax.dynamic_update_index_in_dim(b, o, chip, axis=0) for b, o in zip(lands, own)]
        return {n: _from_shards(n, b) for n, b in zip(MATRICES, lands)}

    chip1 = chip.astype(jnp.int32).reshape(1)
    early = ("w_xq", "w_xkv", "w_xo", "w_gate_up", "w_down")
    reduced = {l: {} for l in range(DEPTH)}
    swapping, sending, sharing = [], [], []

    def scatter_plan(count):
        return [(i, (lambda sender, receiver: (_chip(receiver),)), i, (lambda sender, j=j: (j,)), mask)
                for i in range(count) for j, mask in enumerate(XY_MASKS)]

    def swap_plan(mine):
        def half_rows(shape):
            half = shape[2] // 2
            return lambda sender, receiver: (slice(None), slice(None), pl.ds(receiver[2] * half, half))

        plan = [(i, half_rows(a.shape), i, whole, SIBLING) for i, a in enumerate(mine)]
        return plan, [SDS(a.shape[:2] + (a.shape[2] // 2, a.shape[3]), a.dtype) for a in mine]

    def start_scatter(tag, names, mine, theirs):
        pair = [_add_half(a, b, core, name="grads_pair_sum") for a, b in zip(mine, theirs)]
        lands = [SDS((len(XY_MASKS),) + p.shape[1:], p.dtype) for p in pair]
        send_sems, recv_sems, pair, lands, token = _exchange_start(pair, lands, scatter_plan(len(names)),
                                                                   name=f"scatter_start_{tag}")
        sending.append((tag, names, send_sems, recv_sems, pair, lands))
        return token

    def finish_scatter(after):
        tag, names, send_sems, recv_sems, pair, lands = sending.pop(0)
        pair, parts = _exchange_wait(send_sems, recv_sems, pair, lands, scatter_plan(len(names)), after,
                                     name=f"scatter_wait_{tag}")
        layer = int(tag[0])
        sums = [_sum_chips(p, pr, chip1, name="grads_chip_sum") for p, pr in zip(parts, pair)]
        token = None
        if layer > 0:
            plan = [(i, whole, i, whole, SIBLING) for i in range(len(sums))]
            send_sems, recv_sems, sums, lands, token = _exchange_start(
                sums, [SDS(r.shape, r.dtype) for r in sums], plan, name=f"share_start_{tag}")
            sharing.append((layer, names, send_sems, recv_sems, sums, lands, plan, tag))
        reduced[layer].update(zip(names, sums))
        return token

    def on_grads(l, g, dh):
        tokens = []
        while sending:
            tokens.append(finish_scatter(dh))
        names = [n for n in MATRICES if not (l == 0 and n in early)]
        mine = [_to_shards(n, g[n])[:, None] for n in names]
        plan, lands = swap_plan(mine)
        send_sems, recv_sems, mine, lands, token = _exchange_start(mine, lands, plan, name=f"swap_start_{l}")
        swapping.append((str(l), names, send_sems, recv_sems, mine, lands, plan))
        for t in tokens:
            if t is not None:
                token = token + t
        return token

    def mid_backward(after):
        if not swapping:
            return None
        tag, names, send_sems, recv_sems, mine, lands, plan = swapping.pop()
        mine, theirs = _exchange_wait(send_sems, recv_sems, mine, lands, plan, after, name=f"swap_wait_{tag}")
        return start_scatter(tag, names, mine, theirs)

    def after_xattn(l, g, dh1):
        if l != 0:
            return None
        mine = [_to_shards(n, g[n])[:, None] for n in early]
        plan, lands = swap_plan(mine)
        theirs = _exchange(mine, lands, plan, [], name="grads_swap_early")
        return start_scatter("0a", early, mine, theirs)

    loss_row, dx, grads = _local_step(x[0], mem[0], positions.reshape(seq, 1), loss_target[0], full,
                                      matrices_for=matrices_for, on_grads=on_grads, mid_backward=mid_backward,
                                      after_xattn=after_xattn)
    mid_backward(dx)
    while sending:
        finish_scatter(dx)
    others = {l: {} for l in range(DEPTH)}
    for layer, names, send_sems, recv_sems, sums, lands, plan, tag in sharing:
        sums, lands = _exchange_wait(send_sems, recv_sems, sums, lands, plan, dx, name=f"share_wait_{tag}")
        reduced[layer].update(zip(names, sums))
        others[layer].update(zip(names, lands))
    last = [reduced[0][n] for n in MATRICES]
    others[0].update(zip(MATRICES, _send_to_sibling(last, name="grads_share_halves")))
    south = lax.axis_index("c") == 0
    grad = {}
    for n in MATRICES:
        layers = []
        for l in range(DEPTH):
            a, b = reduced[l][n], others[l][n]
            layers += [jnp.where(south, a, b), jnp.where(south, b, a)]
        grad[n] = jnp.concatenate(layers, axis=0).reshape(w[n].shape)

    packed = _pack_small(grads)
    total = _sum_devices(_gather_all(packed, name="small_grads_gather"), name="small_grads_sum")
    small = _unpack_small(total, grads[0])
    for n in WEIGHTS:
        if n in MATRICES:
            continue
        g = jnp.stack([s[n] for s in small])
        if n in ("gdn_a_log", "gdn_dt_bias"):
            g = g[:, 0, :GDN_HEADS]
        elif n in SMALL_SHARDED:
            width = w[n].shape[-1]
            g = lax.dynamic_slice_in_dim(g, chip * width, width, axis=2)
        grad[n] = g.reshape(w[n].shape)

    delta, new_m, new_v = {}, {}, {}
    for n in WEIGHTS:
        shape = w[n].shape
        two_d = (-1, shape[-1])
        d, nm, nv = _adamw(w[n].reshape(two_d), grad[n].reshape(two_d), m[n].reshape(two_d), v[n].reshape(two_d),
                           name="adamw_" + n)
        delta[n], new_m[n], new_v[n] = d.reshape(shape), nm.reshape(shape), nv.reshape(shape)

    loss = lax.psum(loss_row[0, 0], ("x", "y", "c"))
    return (loss, dx.reshape(x.shape), *[grad[n] for n in WEIGHTS], *[delta[n] for n in WEIGHTS],
            *[new_m[n] for n in WEIGHTS], *[new_v[n] for n in WEIGHTS])
```

```python
import math

import jax
import jax.numpy as jnp
from jax import lax
from jax.experimental import pallas as pl
from jax.experimental.pallas import tpu as pltpu

F32 = jnp.float32
BF16 = jnp.bfloat16
BS = pl.BlockSpec
SDS = jax.ShapeDtypeStruct
PRECISE = lax.Precision.HIGH

D_MODEL = 1024
DEPTH = 4
EPS = 1e-6
ATTN_HEADS = 4
ATTN_HEAD_DIM = 64
ATTN_WIDTH = 256
ROPE_THETA = 500000.0
ROPE_DIM = 16
CONV_WIDTH = 256
CONV_K = 3
GDN_HEADS = 4
GDN_HEAD_DIM = 128
GDN_WIDTH = 512
GDN_CONV_K = 4
GDN_CHUNK = 64
IN_WIDTH = 3592
XATTN_HEADS = 4
XATTN_HEAD_DIM = 256
FFN_HIDDEN = 2816
ADAM_LR = 0.001
ADAM_B1 = 0.9
ADAM_B2 = 0.999
ADAM_EPS = 1e-08
ADAM_WD = 0.01
ADAM_STEP = 10

IN_PAD = 3840
COL_GDN = 1536
COL_GATE = 3072
COL_AB = 3584

VMEM_LIMIT_V7X = 56 * 1024 * 1024
LANE = 128


def _cp(**kw):
    return pltpu.CompilerParams(vmem_limit_bytes=VMEM_LIMIT_V7X, **kw)


def _tile(n, cap):
    if n <= cap:
        return n
    best = None
    for t in range(LANE, cap + 1, LANE):
        if n % t == 0:
            best = t
    assert best is not None, (n, cap)
    return best


def _dot(a, b, ca, cb, precise=False):
    dims = (((ca,), (cb,)), ((), ()))
    if precise:
        return lax.dot_general(a.astype(F32), b.astype(F32), dims, precision=PRECISE,
                               preferred_element_type=F32)
    return lax.dot_general(a.astype(BF16), b.astype(BF16), dims, preferred_element_type=F32)


def _sigmoid(x):
    return 1.0 / (1.0 + jnp.exp(-x))


ROW_TILE = 1024
MM_ROWS = 1408
MM_BLOCK_BYTES = 6 * 1024 * 1024
MM_A_BYTES = 8 * 1024 * 1024


def _mm_tn(width, k, itemsize):
    if k * width * itemsize <= MM_BLOCK_BYTES:
        return width
    return _tile(width, max(LANE, min(1024, MM_BLOCK_BYTES // (k * itemsize) // LANE * LANE)))


def _mm(a, b, *, ta=False, tb=False, out_dtype=F32, b_shards=False, out_shards=False, halves=False, name):
    if halves and tb:
        m, k = a.shape[1], 2 * a.shape[2]
    else:
        m, k = (a.shape[1], a.shape[0]) if ta else a.shape
    tm = _tile(m, MM_ROWS)
    ca = 0 if ta else 1

    if b_shards and tb:
        ns, n, c = b.shape
        assert k == ns * c and not ta
        tn = _tile(n, max(LANE, min(1024, MM_BLOCK_BYTES // (k * b.dtype.itemsize) // LANE * LANE)))

        def a_block(a_ref, s):
            if halves:
                per = ns // 2
                return a_ref[s // per, :, (s % per) * c:(s % per + 1) * c]
            return a_ref[:, s * c:(s + 1) * c]

        def body(a_ref, b_ref, o_ref):
            acc = _dot(a_block(a_ref, 0), b_ref[0], 1, 1)
            for s in range(1, ns):
                acc = acc + _dot(a_block(a_ref, s), b_ref[s], 1, 1)
            o_ref[...] = acc.astype(out_dtype)

        b_spec = BS((ns, tn, c), lambda i, j: (0, j, 0))
    else:
        if b_shards:
            ns, kb, c = b.shape
            n = ns * c
            tn = _mm_tn(c, k, b.dtype.itemsize)
            nb = c // tn
            b_spec = BS((None, k, tn), lambda i, j: (j // nb, 0, j % nb))
        elif halves:
            kb, n = b.shape[1], 2 * b.shape[2]
            c = n // 4
            tn = _mm_tn(c, k, b.dtype.itemsize)
            nb = c // tn
            b_spec = BS((None, k, tn), lambda i, j: (j // (2 * nb), 0, j % (2 * nb)))
        else:
            kb, n = (b.shape[1], b.shape[0]) if tb else b.shape
            c = n // 4 if out_shards else n
            tn = _mm_tn(c, k, b.dtype.itemsize)
            nb = c // tn
            b_spec = BS((tn, k), lambda i, j: (j, 0)) if tb else BS((k, tn), lambda i, j: (0, j))
        assert kb == k
        cb = 1 if tb else 0

        def body(a_ref, b_ref, o_ref):
            o_ref[...] = _dot(a_ref[...], b_ref[...], ca, cb).astype(out_dtype)

    out_bytes = jnp.dtype(out_dtype).itemsize
    while (tm > 256 and tm % 256 == 0 and
           (tm * k * a.dtype.itemsize > MM_A_BYTES or tm * tn * out_bytes > MM_BLOCK_BYTES)):
        tm //= 2
    if halves and tb:
        a_spec = BS((2, tm, k // 2), lambda i, j: (0, i, 0))
    else:
        a_spec = BS((k, tm), lambda i, j: (0, i)) if ta else BS((tm, k), lambda i, j: (i, 0))
    if out_shards:
        out_spec = BS((None, tm, tn), lambda i, j: (j // nb, i, j % nb))
        out_shape = SDS((4, m, n // 4), out_dtype)
    elif halves and not (ta or tb):
        out_spec = BS((None, tm, tn), lambda i, j: (j // (2 * nb), i, j % (2 * nb)))
        out_shape = SDS((2, m, n // 2), out_dtype)
    else:
        out_spec = BS((tm, tn), lambda i, j: (i, j))
        out_shape = SDS((m, n), out_dtype)
    return pl.pallas_call(
        body, grid=(m // tm, n // tn), in_specs=[a_spec, b_spec], out_specs=out_spec, out_shape=out_shape,
        compiler_params=_cp(), name=name)(a, b)


def _rmsnorm(x, w, *, name):
    r, d = x.shape
    tr = _tile(r, ROW_TILE)

    def body(x_ref, w_ref, o_ref):
        xv = x_ref[...]
        rs = lax.rsqrt(jnp.mean(xv * xv, axis=-1, keepdims=True) + EPS)
        o_ref[...] = (xv * rs * w_ref[...]).astype(BF16)

    return pl.pallas_call(
        body, grid=(r // tr,), in_specs=[BS((tr, d), lambda i: (i, 0)), BS((1, d), lambda i: (0, 0))],
        out_specs=BS((tr, d), lambda i: (i, 0)), out_shape=SDS((r, d), BF16),
        compiler_params=_cp(), name=name)(x, w)


def _resnorm(h, m, w, *, name):
    r, d = h.shape
    tr = _tile(r, ROW_TILE)

    def body(h_ref, m_ref, w_ref, o_ref):
        mv = m_ref[...]
        rs = lax.rsqrt(jnp.mean(mv * mv, axis=-1, keepdims=True) + EPS)
        o_ref[...] = h_ref[...] + mv * rs * w_ref[...]

    row = BS((tr, d), lambda i: (i, 0))
    return pl.pallas_call(
        body, grid=(r // tr,), in_specs=[row, row, BS((1, d), lambda i: (0, 0))],
        out_specs=row, out_shape=SDS((r, d), F32), compiler_params=_cp(), name=name)(h, m, w)


def _rmsnorm_bwd(x, w, dy, res=None, *, name):
    r, d = x.shape
    tr = _tile(r, ROW_TILE)
    has_res = res is not None

    def body(*refs):
        if has_res:
            x_ref, w_ref, dy_ref, res_ref, dx_ref, dw_ref = refs
        else:
            x_ref, w_ref, dy_ref, dx_ref, dw_ref = refs
        xv = x_ref[...]
        dyv = dy_ref[...].astype(F32)
        rs = lax.rsqrt(jnp.mean(xv * xv, axis=-1, keepdims=True) + EPS)
        nv = xv * rs
        dyw = dyv * w_ref[...]
        dx = rs * (dyw - nv * jnp.mean(dyw * nv, axis=-1, keepdims=True))
        if has_res:
            dx = dx + res_ref[...]
        dx_ref[...] = dx

        @pl.when(pl.program_id(0) == 0)
        def _():
            dw_ref[...] = jnp.zeros_like(dw_ref)

        dw_ref[...] += jnp.sum(dyv * nv, axis=0, keepdims=True)

    row = BS((tr, d), lambda i: (i, 0))
    vec = BS((1, d), lambda i: (0, 0))
    ins = [x, w, dy] + ([res] if has_res else [])
    return pl.pallas_call(
        body, grid=(r // tr,), in_specs=[row, vec, row] + ([row] if has_res else []),
        out_specs=[row, vec], out_shape=[SDS((r, d), F32), SDS((1, d), F32)],
        compiler_params=_cp(), name=name)(*ins)


def _gate_up_swiglu(a, w, *, name):
    m, k = a.shape
    ns, _, c = w.shape
    per = ns // 2
    tm = _tile(m, ROW_TILE)

    def body(a_ref, wg_ref, wu_ref, gu_ref, act_ref):
        av = a_ref[...]
        g = _dot(av, wg_ref[...], 1, 0)
        u = _dot(av, wu_ref[...], 1, 0)
        gu_ref[0] = g
        gu_ref[1] = u
        act_ref[...] = (g * _sigmoid(g) * u).astype(BF16)

    return pl.pallas_call(
        body, grid=(m // tm, per),
        in_specs=[BS((tm, k), lambda i, j: (i, 0)), BS((None, k, c), lambda i, j: (j, 0, 0)),
                  BS((None, k, c), lambda i, j: (per + j, 0, 0))],
        out_specs=[BS((2, tm, c), lambda i, j: (0, i, j)), BS((tm, c), lambda i, j: (i, j))],
        out_shape=[SDS((2, m, per * c), F32), SDS((m, per * c), BF16)],
        compiler_params=_cp(), name=name)(a, w, w)


def _swiglu_bwd(gu, dact, *, name):
    _, r, hid = gu.shape
    tr, tc = _tile(r, 512), _tile(hid, 1408)

    def body(gu_ref, d_ref, o_ref):
        g = gu_ref[0]
        da = d_ref[...]
        sg = _sigmoid(g)
        o_ref[0] = (da * gu_ref[1] * sg * (1.0 + g * (1.0 - sg))).astype(BF16)
        o_ref[1] = (da * g * sg).astype(BF16)

    blk = BS((2, tr, tc), lambda i, j: (0, i, j))
    return pl.pallas_call(
        body, grid=(r // tr, hid // tc), in_specs=[blk, BS((tr, tc), lambda i, j: (i, j))],
        out_specs=blk, out_shape=SDS((2, r, hid), BF16), compiler_params=_cp(), name=name)(gu, dact)


def _loss_grad(h, target, *, name):
    r, d = h.shape
    tr = _tile(r, ROW_TILE)

    def body(h_ref, t_ref, l_ref, g_ref):
        e = h_ref[...] - t_ref[...]
        g_ref[...] = e * (1.0 / d)

        @pl.when(pl.program_id(0) == 0)
        def _():
            l_ref[...] = jnp.zeros_like(l_ref)

        l_ref[...] += jnp.full((1, LANE), 0.5 / d, F32) * jnp.sum(e * e)

    row = BS((tr, d), lambda i: (i, 0))
    return pl.pallas_call(
        body, grid=(r // tr,), in_specs=[row, row],
        out_specs=[BS((1, LANE), lambda i: (0, 0)), row],
        out_shape=[SDS((1, LANE), F32), SDS((r, d), F32)], compiler_params=_cp(), name=name)(h, target)


def _adamw(w, g, m, v, *, name):
    r, c = w.shape
    tr = r if r <= 512 else _rows_tile(r)
    bc1 = 1.0 - ADAM_B1 ** ADAM_STEP
    bc2 = 1.0 - ADAM_B2 ** ADAM_STEP

    def body(w_ref, g_ref, m_ref, v_ref, d_ref, nm_ref, nv_ref):
        gv = g_ref[...]
        nm = ADAM_B1 * m_ref[...] + (1.0 - ADAM_B1) * gv
        nv = ADAM_B2 * v_ref[...] + (1.0 - ADAM_B2) * (gv * gv)
        d_ref[...] = -ADAM_LR * ((nm / bc1) / (jnp.sqrt(nv / bc2) + ADAM_EPS) + ADAM_WD * w_ref[...])
        nm_ref[...] = nm
        nv_ref[...] = nv

    blk = BS((tr, c), lambda i: (i, 0))
    return pl.pallas_call(
        body, grid=(r // tr,), in_specs=[blk] * 4, out_specs=[blk] * 3,
        out_shape=[SDS((r, c), F32)] * 3, compiler_params=_cp(), name=name)(w, g, m, v)


def _rows_tile(r):
    for t in (512, 256, 128, 64, 32, 16, 8):
        if r % t == 0:
            return t
    return r


def _rope_tables(pos, *, name):
    s = pos.shape[0]
    half = ROPE_DIM // 2

    def body(p_ref, c_ref, a_ref, b_ref):
        lane = lax.broadcasted_iota(jnp.int32, (s, ATTN_WIDTH), 1) & (ATTN_HEAD_DIM - 1)
        fi = (lane & (half - 1)).astype(F32)
        inv_freq = jnp.exp(fi * (-2.0 * math.log(ROPE_THETA) / ROPE_DIM))
        ang = p_ref[...].astype(F32) * inv_freq
        cs, sn = jnp.cos(ang), jnp.sin(ang)
        c_ref[...] = jnp.where(lane < ROPE_DIM, cs, 1.0)
        a_ref[...] = jnp.where(lane < half, -sn, 0.0)
        b_ref[...] = jnp.where((lane >= half) & (lane < ROPE_DIM), sn, 0.0)

    full = BS((s, ATTN_WIDTH), lambda i: (0, 0))
    return pl.pallas_call(
        body, grid=(1,), in_specs=[BS((s, 1), lambda i: (0, 0))], out_specs=[full] * 3,
        out_shape=[SDS((s, ATTN_WIDTH), F32)] * 3, compiler_params=_cp(), name=name)(pos)


def _rot(x, c, a, b):
    w = x.shape[1]
    return x * c + pltpu.roll(x, w - ROPE_DIM // 2, 1) * a + pltpu.roll(x, ROPE_DIM // 2, 1) * b


def _rot_t(dy, c, a, b):
    w = dy.shape[1]
    return dy * c + pltpu.roll(dy * a, ROPE_DIM // 2, 1) + pltpu.roll(dy * b, w - ROPE_DIM // 2, 1)


def _attn_count(q0, tq, s):
    dist = (lax.broadcasted_iota(jnp.int32, (tq, s), 0) + q0) - lax.broadcasted_iota(jnp.int32, (tq, s), 1)
    cnt = ((dist <= 128).astype(F32) + (((dist & 3) == 0) & (dist <= 512)).astype(F32)
           + ((dist & 15) == 0).astype(F32))
    return jnp.where(dist >= 0, cnt, 0.0)


ATTN_TQ = 512


def _attn_specs(s, tq):
    def qblk(col):
        return BS((tq, ATTN_WIDTH), lambda i: (i, col))

    def full(col):
        return BS((s, ATTN_WIDTH), lambda i: (0, col))

    return qblk, full


ATTN_TK = 512


def _attn_chunk(i, c, tq, k_ref, v_ref, ck, ak, bk):
    ks = pl.ds(pl.multiple_of(c * ATTN_TK, ATTN_TK), ATTN_TK)
    k = _rot(k_ref[ks, :], ck[ks, :], ak[ks, :], bk[ks, :]).astype(BF16)
    v = v_ref[ks, :].astype(BF16)
    cnt = _attn_count(i * tq - c * ATTN_TK, tq, ATTN_TK)
    return ks, k, v, cnt


def _attn_flash_fwd(proj, tabs, *, name):
    s = proj.shape[0]
    tq = ATTN_TQ
    qblk, full = _attn_specs(s, tq)
    scale = ATTN_HEAD_DIM ** -0.5
    nh = ATTN_HEADS

    def body(q_ref, k_ref, v_ref, cq, aq, bq, ck, ak, bk, o_ref, lse_ref):
        i = pl.program_id(0)
        q = _rot(q_ref[...], cq[...], aq[...], bq[...]) * scale
        head = lax.broadcasted_iota(jnp.int32, (1, ATTN_WIDTH), 1) >> 6
        hms = [(head == h).astype(F32) for h in range(nh)]
        qms = [(q * hm).astype(BF16) for hm in hms]

        def step(c, carry):
            ms, ls, acc = carry
            _, k, v, cnt = _attn_chunk(i, c, tq, k_ref, v_ref, ck, ak, bk)
            valid = cnt > 0.0
            new_ms, new_ls = [], []
            scale_acc = jnp.zeros((tq, ATTN_WIDTH), F32)
            add = jnp.zeros((tq, ATTN_WIDTH), F32)
            for h in range(nh):
                sc = _dot(qms[h], k, 1, 1)
                m_new = jnp.maximum(ms[h], jnp.max(jnp.where(valid, sc, -1e30), axis=-1, keepdims=True))
                alpha = jnp.exp(ms[h] - m_new)
                p = cnt * jnp.exp(jnp.minimum(sc - m_new, 0.0))
                new_ms.append(m_new)
                new_ls.append(alpha * ls[h] + jnp.sum(p, axis=-1, keepdims=True))
                scale_acc = scale_acc + alpha * hms[h]
                add = add + _dot(p, v, 1, 0) * hms[h]
            return new_ms, new_ls, acc * scale_acc + add

        init = ([jnp.full((tq, 1), -1e30, F32)] * nh, [jnp.zeros((tq, 1), F32)] * nh,
                jnp.zeros((tq, ATTN_WIDTH), F32))
        ms, ls, acc = lax.fori_loop(0, i // (ATTN_TK // tq) + 1, step, init)
        inv = jnp.zeros((tq, ATTN_WIDTH), F32)
        lane = lax.broadcasted_iota(jnp.int32, (tq, LANE), 1)
        lse = jnp.zeros((tq, LANE), F32)
        for h in range(nh):
            inv = inv + (1.0 / ls[h]) * hms[h]
            lse = jnp.where(lane == h, ms[h] + jnp.log(ls[h]), lse)
        o_ref[...] = (acc * inv).astype(BF16)
        lse_ref[...] = lse

    return pl.pallas_call(
        body, grid=(s // tq,),
        in_specs=[qblk(0), full(1), full(2), qblk(0), qblk(0), qblk(0), full(0), full(0), full(0)],
        out_specs=[BS((tq, ATTN_WIDTH), lambda i: (i, 0)), BS((tq, LANE), lambda i: (i, 0))],
        out_shape=[SDS((s, ATTN_WIDTH), BF16), SDS((s, LANE), F32)],
        compiler_params=_cp(), name=name)(proj, proj, proj, *tabs, *tabs)


def _attn_flash_bwd(proj, tabs, cat, lse, dcat, *, name):
    s = proj.shape[0]
    tq = ATTN_TQ
    nq = s // tq
    qblk, full = _attn_specs(s, tq)
    scale = ATTN_HEAD_DIM ** -0.5
    nh = ATTN_HEADS

    def body(q_ref, k_ref, v_ref, cq, aq, bq, ck, ak, bk, y_ref, lse_ref, dy_ref,
             dq_ref, dk_ref, dv_ref, dk_acc, dv_acc):
        i = pl.program_id(0)

        @pl.when(i == 0)
        def _():
            dk_acc[...] = jnp.zeros_like(dk_acc)
            dv_acc[...] = jnp.zeros_like(dv_acc)

        q = _rot(q_ref[...], cq[...], aq[...], bq[...]) * scale
        dy = dy_ref[...].astype(F32)
        prod = dy * y_ref[...].astype(F32)
        lse_all = lse_ref[...]
        head = lax.broadcasted_iota(jnp.int32, (1, ATTN_WIDTH), 1) >> 6
        hms = [(head == h).astype(F32) for h in range(nh)]
        qms = [(q * hm).astype(BF16) for hm in hms]
        dyms = [(dy * hm).astype(BF16) for hm in hms]
        deltas = [jnp.sum(prod * hm, axis=-1, keepdims=True) for hm in hms]
        lses = [lse_all[:, h:h + 1] for h in range(nh)]

        def step(c, dq):
            ks, k, v, cnt = _attn_chunk(i, c, tq, k_ref, v_ref, ck, ak, bk)
            dk_c = jnp.zeros((ATTN_TK, ATTN_WIDTH), F32)
            dv_c = jnp.zeros((ATTN_TK, ATTN_WIDTH), F32)
            for h in range(nh):
                sc = _dot(qms[h], k, 1, 1)
                p = cnt * jnp.exp(jnp.minimum(sc - lses[h], 0.0))
                dp = _dot(dyms[h], v, 1, 1)
                ds = p * (dp - deltas[h])
                dq = dq + _dot(ds, k, 1, 0) * hms[h]
                dk_c = dk_c + _dot(ds, qms[h], 0, 0)
                dv_c = dv_c + _dot(p, dyms[h], 0, 0)
            dk_acc[ks, :] += dk_c
            dv_acc[ks, :] += dv_c
            return dq

        dq = lax.fori_loop(0, i // (ATTN_TK // tq) + 1, step, jnp.zeros((tq, ATTN_WIDTH), F32))
        dq_ref[...] = _rot_t(dq * scale, cq[...], aq[...], bq[...]).astype(BF16)

        @pl.when(i == nq - 1)
        def _():
            dk_ref[...] = _rot_t(dk_acc[...], ck[...], ak[...], bk[...]).astype(BF16)
            dv_ref[...] = dv_acc[...].astype(BF16)

    whole = BS((s, ATTN_WIDTH), lambda i: (0, 0))
    return pl.pallas_call(
        body, grid=(nq,),
        in_specs=[qblk(0), full(1), full(2), qblk(0), qblk(0), qblk(0), full(0), full(0), full(0),
                  qblk(0), BS((tq, LANE), lambda i: (i, 0)), qblk(0)],
        out_specs=[BS((tq, ATTN_WIDTH), lambda i: (i, 0)), whole, whole],
        out_shape=[SDS((s, ATTN_WIDTH), BF16)] * 3,
        scratch_shapes=[pltpu.VMEM((s, ATTN_WIDTH), F32), pltpu.VMEM((s, ATTN_WIDTH), F32)],
        compiler_params=_cp(), name=name)(proj, proj, proj, *tabs, *tabs, cat, lse, dcat)


def _shift_down(x, n):
    if n == 0:
        return x
    rows = lax.broadcasted_iota(jnp.int32, x.shape, 0)
    return jnp.where(rows >= n, pltpu.roll(x, n, 0), 0.0)


def _shift_up(x, n):
    if n == 0:
        return x
    t = x.shape[0]
    rows = lax.broadcasted_iota(jnp.int32, x.shape, 0)
    return jnp.where(rows < t - n, pltpu.roll(x, t - n, 0), 0.0)


def _conv_fwd(z, w, kk):
    y = z * w[kk - 1:kk, :]
    for j in range(kk - 1):
        y = y + _shift_down(z, kk - 1 - j) * w[j:j + 1, :]
    return y


def _conv_bwd(z, dy, w, kk):
    dz = dy * w[kk - 1:kk, :]
    dws = []
    for j in range(kk - 1):
        dz = dz + _shift_up(dy, kk - 1 - j) * w[j:j + 1, :]
        dws.append(jnp.sum(dy * _shift_down(z, kk - 1 - j), axis=0, keepdims=True))
    dws.append(jnp.sum(dy * z, axis=0, keepdims=True))
    return dz, jnp.concatenate(dws, axis=0)


def _sconv_fwd(proj, w, *, name):
    s = proj.shape[0]

    def body(b_ref, c_ref, x_ref, w_ref, o_ref):
        y = _conv_fwd(c_ref[...] * x_ref[...], w_ref[...], CONV_K)
        o_ref[...] = (b_ref[...] * y).astype(BF16)

    def col(j):
        return BS((s, LANE), lambda i: (0, j + i))

    return pl.pallas_call(
        body, grid=(CONV_WIDTH // LANE,), in_specs=[col(6), col(8), col(10), BS((CONV_K, LANE), lambda i: (0, i))],
        out_specs=BS((s, LANE), lambda i: (0, i)), out_shape=SDS((s, CONV_WIDTH), BF16),
        compiler_params=_cp(), name=name)(proj, proj, proj, w)


def _sconv_bwd(proj, w, dcat, *, name):
    s = proj.shape[0]

    def body(b_ref, c_ref, x_ref, w_ref, dy_ref, db_ref, dc_ref, dx_ref, dw_ref):
        cv, xv, wv = c_ref[...], x_ref[...], w_ref[...]
        dy = dy_ref[...].astype(F32)
        z = cv * xv
        db_ref[...] = (dy * _conv_fwd(z, wv, CONV_K)).astype(BF16)
        dz, dw = _conv_bwd(z, dy * b_ref[...], wv, CONV_K)
        dc_ref[...] = (dz * xv).astype(BF16)
        dx_ref[...] = (dz * cv).astype(BF16)
        dw_ref[...] = dw

    def col(j):
        return BS((s, LANE), lambda i: (0, j + i))

    out = BS((s, LANE), lambda i: (0, i))
    wspec = BS((CONV_K, LANE), lambda i: (0, i))
    return pl.pallas_call(
        body, grid=(CONV_WIDTH // LANE,), in_specs=[col(6), col(8), col(10), wspec, col(2)],
        out_specs=[out, out, out, wspec],
        out_shape=[SDS((s, CONV_WIDTH), BF16)] * 3 + [SDS((CONV_K, CONV_WIDTH), F32)],
        compiler_params=_cp(), name=name)(proj, proj, proj, w, dcat)


def _l2n(y, scale):
    r = lax.rsqrt(jnp.sum(y * y, axis=-1, keepdims=True) + EPS)
    return y * r * scale, r


def _gdn_pre_fwd(proj, w, *, name):
    s = proj.shape[0]
    nh = GDN_HEADS

    def body(x_ref, w_ref, o_ref):
        j = pl.program_id(0)
        c = _conv_fwd(x_ref[...], w_ref[...], GDN_CONV_K)
        y = c * _sigmoid(c)
        scale = jnp.where(j < nh, GDN_HEAD_DIM ** -0.5, 1.0)
        n, _ = _l2n(y, scale)
        o_ref[...] = jnp.where(j < 2 * nh, n, y)

    return pl.pallas_call(
        body, grid=(3 * nh,),
        in_specs=[BS((s, LANE), lambda j: (0, COL_GDN // LANE + j)), BS((GDN_CONV_K, LANE), lambda j: (0, j))],
        out_specs=BS((s, LANE), lambda j: (0, j)), out_shape=SDS((s, 3 * GDN_WIDTH), F32),
        compiler_params=_cp(), name=name)(proj, w)


def _gdn_pre_bwd(proj, w, dqkv, *, name):
    s = proj.shape[0]
    nh = GDN_HEADS

    def body(x_ref, w_ref, d_ref, dx_ref, dw_ref):
        j = pl.program_id(0)
        xv, wv, dn = x_ref[...], w_ref[...], d_ref[...]
        c = _conv_fwd(xv, wv, GDN_CONV_K)
        sg = _sigmoid(c)
        y = c * sg
        scale = jnp.where(j < nh, GDN_HEAD_DIM ** -0.5, 1.0)
        n, r = _l2n(y, 1.0)
        dns = dn * scale
        dy_norm = r * (dns - n * jnp.sum(dns * n, axis=-1, keepdims=True))
        dy = jnp.where(j < 2 * nh, dy_norm, dn)
        dc = dy * sg * (1.0 + c * (1.0 - sg))
        dx, dw = _conv_bwd(xv, dc, wv, GDN_CONV_K)
        dx_ref[...] = dx.astype(BF16)
        dw_ref[...] = dw

    wspec = BS((GDN_CONV_K, LANE), lambda j: (0, j))
    blk = BS((s, LANE), lambda j: (0, j))
    return pl.pallas_call(
        body, grid=(3 * nh,),
        in_specs=[BS((s, LANE), lambda j: (0, COL_GDN // LANE + j)), wspec, blk],
        out_specs=[blk, wspec], out_shape=[SDS((s, 3 * GDN_WIDTH), BF16), SDS((GDN_CONV_K, 3 * GDN_WIDTH), F32)],
        compiler_params=_cp(), name=name)(proj, w, dqkv)


def _softplus(x):
    return jnp.maximum(x, 0.0) + jnp.log(1.0 + jnp.exp(-jnp.abs(x)))


def _gdn_gates_fwd(proj, a_log, dt_bias, *, name):
    s = proj.shape[0]

    def body(x_ref, al_ref, dt_ref, o_ref):
        xv = x_ref[...]
        lane = lax.broadcasted_iota(jnp.int32, xv.shape, 1)
        g = -jnp.exp(al_ref[...]) * _softplus(xv + dt_ref[...])
        o_ref[...] = jnp.where(lane < GDN_HEADS, g, jnp.where(lane < 2 * GDN_HEADS, _sigmoid(xv), 0.0))

    vec = BS((1, LANE), lambda i: (0, 0))
    return pl.pallas_call(
        body, grid=(1,), in_specs=[BS((s, LANE), lambda i: (0, COL_AB // LANE)), vec, vec],
        out_specs=BS((s, LANE), lambda i: (0, 0)), out_shape=SDS((s, LANE), F32),
        compiler_params=_cp(), name=name)(proj, a_log, dt_bias)


def _gdn_gates_bwd(proj, a_log, dt_bias, dgb, *, name):
    s = proj.shape[0]

    def body(x_ref, al_ref, dt_ref, d_ref, dx_ref, dal_ref, ddt_ref):
        xv, dv = x_ref[...], d_ref[...]
        lane = lax.broadcasted_iota(jnp.int32, xv.shape, 1)
        is_g = lane < GDN_HEADS
        ea = -jnp.exp(al_ref[...])
        z = xv + dt_ref[...]
        da = jnp.where(is_g, dv * ea * _sigmoid(z), 0.0)
        beta = _sigmoid(xv)
        dx_ref[...] = jnp.where(is_g, da, jnp.where(lane < 2 * GDN_HEADS, dv * beta * (1.0 - beta), 0.0)).astype(BF16)
        dal_ref[...] = jnp.sum(jnp.where(is_g, dv * ea * _softplus(z), 0.0), axis=0, keepdims=True)
        ddt_ref[...] = jnp.sum(da, axis=0, keepdims=True)

    vec = BS((1, LANE), lambda i: (0, 0))
    blk = BS((s, LANE), lambda i: (0, 0))
    return pl.pallas_call(
        body, grid=(1,), in_specs=[BS((s, LANE), lambda i: (0, COL_AB // LANE)), vec, vec, blk],
        out_specs=[blk, vec, vec], out_shape=[SDS((s, LANE), BF16), SDS((1, LANE), F32), SDS((1, LANE), F32)],
        compiler_params=_cp(), name=name)(proj, a_log, dt_bias, dgb)


def _col_to_row(col, eye):
    return jnp.sum(jnp.where(eye, col, 0.0), axis=0, keepdims=True)


def _row_to_col(row, eye):
    return jnp.sum(jnp.where(eye, row, 0.0), axis=1, keepdims=True)


GDN_GROUP = 4
TRI_BLOCK_SHIFT = 4


def _gdn_masks(c):
    row = lax.broadcasted_iota(jnp.int32, (c, c), 0)
    col = lax.broadcasted_iota(jnp.int32, (c, c), 1)
    return dict(row=row, col=col, eye=row == col, low=row >= col, strict=row > col, upper=row <= col,
                on_diag=(row >> TRI_BLOCK_SHIFT) == (col >> TRI_BLOCK_SHIFT))


def _tri_inv(a_list, mk):
    eye_f = mk["eye"].astype(F32)
    ds = [jnp.where(mk["on_diag"], a, 0.0) for a in a_list]
    xs = [eye_f - d for d in ds]
    ps = ds
    for _ in range(3):
        ps = [_dot(p, p, 1, 0, precise=True) for p in ps]
        xs = [x + _dot(x, p, 1, 0, precise=True) for x, p in zip(xs, ps)]
    ms = [_dot(x, a - d, 1, 0) for x, a, d in zip(xs, a_list, ds)]
    m2s = [_dot(m, m, 1, 0) for m in ms]
    ys = [eye_f - m for m in ms]
    ys = [y + _dot(y, m2, 1, 0) for y, m2 in zip(ys, m2s)]
    return [_dot(y, x, 1, 0) for y, x in zip(ys, xs)]


def _gdn_pre(qs, ks, vs, gs, betas, mk, ts=None):
    c, hd = qs[0].shape
    eye, low = mk["eye"], mk["low"]
    g_rows = [_col_to_row(g, eye) for g in gs]
    d_cols = [jnp.sum(jnp.where(low, gr, 0.0), axis=1, keepdims=True) for gr in g_rows]
    d_rows = [jnp.sum(jnp.where(mk["upper"], g, 0.0), axis=0, keepdims=True) for g in gs]
    rels = [jnp.where(low, jnp.exp(jnp.minimum(dc - dr, 0.0)), 0.0) for dc, dr in zip(d_cols, d_rows)]
    d_lasts = [dc[c - 1:c, :] for dc in d_cols]
    es = [jnp.exp(dc) for dc in d_cols]
    fs = [jnp.exp(dl - dc) for dl, dc in zip(d_lasts, d_cols)]
    cds = [jnp.exp(dl) for dl in d_lasts]
    kbs = [k * b for k, b in zip(ks, betas)]
    kbqs = [jnp.concatenate([kb, q], axis=0) for kb, q in zip(kbs, qs)]
    kqk = [_dot(kbq, k, 1, 1) for kbq, k in zip(kbqs, ks)]
    kks = [x[:c, :] for x in kqk]
    qks = [x[c:, :] for x in kqk]
    if ts is None:
        ts = _tri_inv([jnp.where(mk["strict"], kk * rel, 0.0) for kk, rel in zip(kks, rels)], mk)
    vbs = [v * b for v, b in zip(vs, betas)]
    kbes = [kb * e for kb, e in zip(kbs, es)]
    uws = [_dot(t, jnp.concatenate([vb, kbe], axis=1), 1, 0) for t, vb, kbe in zip(ts, vbs, kbes)]
    out = []
    for i in range(len(qs)):
        out.append(dict(rel=rels[i], e=es[i], f=fs[i], cd=cds[i], kb=kbs[i], kbq=kbqs[i], kk=kks[i], qk=qks[i],
                        t=ts[i], u=uws[i][:, :hd], w=uws[i][:, hd:], uw=uws[i], attn=qks[i] * rels[i],
                        qd=qs[i] * es[i], kd=ks[i] * fs[i]))
    return out


def _gdn_apply(pres, sts, leaving=True):
    c = pres[0]["u"].shape[0]
    wqs = [_dot(jnp.concatenate([p["w"], p["qd"]], axis=0), st, 1, 0) for p, st in zip(pres, sts)]
    vns = [p["u"] - x[:c, :] for p, x in zip(pres, wqs)]
    os_ = [x[c:, :] + _dot(p["attn"], vn, 1, 0) for p, x, vn in zip(pres, wqs, vns)]
    if not leaving:
        return vns, os_, None
    new = [p["cd"] * st + _dot(p["kd"], vn, 0, 0) for p, st, vn in zip(pres, sts, vns)]
    return vns, os_, new


def _gdn_bwd_rest(qs, ks, vs, betas, sts, pres, vns, dos, dvns, dsts, mk):
    c, hd = qs[0].shape
    eye = mk["eye"]
    n = range(len(qs))
    dkds = [_dot(vns[i], dsts[i], 1, 1) for i in n]
    dcds = [jnp.sum(sts[i] * dsts[i]) for i in n]
    dattns = [jnp.where(mk["low"], _dot(dos[i], vns[i], 1, 1), 0.0) for i in n]
    dqdws = [_dot(jnp.concatenate([dos[i], -dvns[i]], axis=0), sts[i], 1, 1) for i in n]
    dqds = [x[:c, :] for x in dqdws]
    dws = [x[c:, :] for x in dqdws]
    dvks = [_dot(pres[i]["t"], jnp.concatenate([dvns[i], dws[i]], axis=1), 0, 0) for i in n]
    das = [jnp.where(mk["strict"], -_dot(dvks[i], pres[i]["uw"], 1, 1), 0.0) for i in n]
    dkqs = [jnp.concatenate([das[i] * pres[i]["rel"], dattns[i] * pres[i]["rel"]], axis=0) for i in n]
    dkbdq = [_dot(dkqs[i], ks[i], 1, 0) for i in n]
    dk0 = [_dot(dkqs[i], pres[i]["kbq"], 0, 0) for i in n]
    out = []
    rows1 = lax.broadcasted_iota(jnp.int32, (c, 1), 0)
    for i in n:
        p = pres[i]
        dvb, dkbe = dvks[i][:, :hd], dvks[i][:, hd:]
        grel = (das[i] * p["kk"] + dattns[i] * p["qk"]) * p["rel"]
        dkb = dkbdq[i][:c, :] + dkbe * p["e"]
        dk = dk0[i] + dkds[i] * p["f"] + dkb * betas[i]
        dq = dkbdq[i][c:, :] + dqds[i] * p["e"]
        dv = dvb * betas[i]
        dbeta = jnp.sum(dkb * ks[i], axis=1, keepdims=True) + jnp.sum(dvb * vs[i], axis=1, keepdims=True)
        de = jnp.sum(dqds[i] * qs[i], axis=1, keepdims=True) + jnp.sum(dkbe * p["kb"], axis=1, keepdims=True)
        dff = jnp.sum(dkds[i] * ks[i], axis=1, keepdims=True) * p["f"]
        dd = (de * p["e"] - dff + jnp.sum(grel, axis=1, keepdims=True)
              - _row_to_col(jnp.sum(grel, axis=0, keepdims=True), eye))
        dd = dd + jnp.where(rows1 == c - 1, jnp.sum(dff) + dcds[i] * p["cd"], 0.0)
        dg = jnp.sum(jnp.where(mk["upper"], _col_to_row(dd, eye), 0.0), axis=1, keepdims=True)
        out.append((dq, dk, dv, dg, dbeta))
    return out


def _gdn_specs(c):
    def qkv(j):
        return BS((c, GDN_WIDTH), lambda n: (n, j))

    return qkv


def _gdn_core_fwd(qkv, gbeta, proj, norm_w, *, name):
    s = qkv.shape[0]
    c, nh, hd, grp = GDN_CHUNK, GDN_HEADS, GDN_HEAD_DIM, GDN_GROUP
    n_chunks = s // c
    blk = _gdn_specs(grp * c)
    inst = [(sub, h) for sub in range(grp) for h in range(nh)]

    def body(q_ref, k_ref, v_ref, gb_ref, gate_ref, nw_ref, y_ref, st_ref, t_ref, state):
        @pl.when(pl.program_id(0) == 0)
        def _():
            state[...] = jnp.zeros_like(state)

        mk = _gdn_masks(c)
        rows = [slice(sub * c, (sub + 1) * c) for sub in range(grp)]
        lanes = [slice(h * hd, (h + 1) * hd) for h in range(nh)]
        gbs = [gb_ref[r, :] for r in rows]
        pres = _gdn_pre([q_ref[rows[sub], lanes[h]] for sub, h in inst], [k_ref[rows[sub], lanes[h]] for sub, h in inst],
                        [v_ref[rows[sub], lanes[h]] for sub, h in inst], [gbs[sub][:, h:h + 1] for sub, h in inst],
                        [gbs[sub][:, nh + h:nh + h + 1] for sub, h in inst], mk)
        sts = [state[ls, :] for ls in lanes]
        outs = []
        for sub in range(grp):
            for h in range(nh):
                st_ref[pl.ds((sub * nh + h) * hd, hd), :] = sts[h]
            _, os_, sts = _gdn_apply(pres[sub * nh:(sub + 1) * nh], sts)
            outs += os_
        for h in range(nh):
            state[lanes[h], :] = sts[h]
        nw = nw_ref[...]
        for i, (sub, h) in enumerate(inst):
            t_ref[pl.ds(i * c, c), :] = pres[i]["t"]
            o = outs[i]
            gate = gate_ref[rows[sub], lanes[h]]
            rs = lax.rsqrt(jnp.mean(o * o, axis=-1, keepdims=True) + EPS)
            y_ref[rows[sub], lanes[h]] = (o * rs * nw * (gate * _sigmoid(gate))).astype(BF16)

    return pl.pallas_call(
        body, grid=(n_chunks // grp,),
        in_specs=[blk(0), blk(1), blk(2), BS((grp * c, LANE), lambda n: (n, 0)),
                  BS((grp * c, GDN_WIDTH), lambda n: (n, COL_GATE // GDN_WIDTH)), BS((1, hd), lambda n: (0, 0))],
        out_specs=[BS((grp * c, GDN_WIDTH), lambda n: (n, 0)), BS((grp * nh * hd, hd), lambda n: (n, 0)),
                   BS((grp * nh * c, c), lambda n: (n, 0))],
        out_shape=[SDS((s, GDN_WIDTH), BF16), SDS((n_chunks * nh * hd, hd), F32), SDS((n_chunks * nh * c, c), F32)],
        scratch_shapes=[pltpu.VMEM((nh * hd, hd), F32)],
        compiler_params=_cp(), name=name)(qkv, qkv, qkv, gbeta, proj, norm_w)


def _gdn_core_bwd(qkv, gbeta, proj, norm_w, states, tinv, dcat, *, name):
    s = qkv.shape[0]
    c, nh, hd, grp = GDN_CHUNK, GDN_HEADS, GDN_HEAD_DIM, GDN_GROUP
    n_chunks = s // c
    last = n_chunks // grp - 1
    inst = [(sub, h) for sub in range(grp) for h in range(nh)]

    def rev(j, w):
        return BS((grp * c, w), lambda n: (last - n, j))

    def body(q_ref, k_ref, v_ref, gb_ref, gate_ref, nw_ref, st_ref, t_ref, dy_ref,
             dqkv_ref, dgb_ref, dgate_ref, dnw_ref, dstate):
        @pl.when(pl.program_id(0) == 0)
        def _():
            dstate[...] = jnp.zeros_like(dstate)
            dnw_ref[...] = jnp.zeros_like(dnw_ref)

        mk = _gdn_masks(c)
        rows = [slice(sub * c, (sub + 1) * c) for sub in range(grp)]
        lanes = [slice(h * hd, (h + 1) * hd) for h in range(nh)]
        gbs = [gb_ref[r, :] for r in rows]
        qs = [q_ref[rows[sub], lanes[h]] for sub, h in inst]
        ks = [k_ref[rows[sub], lanes[h]] for sub, h in inst]
        vs = [v_ref[rows[sub], lanes[h]] for sub, h in inst]
        betas = [gbs[sub][:, nh + h:nh + h + 1] for sub, h in inst]
        sts = [st_ref[pl.ds(i * hd, hd), :] for i in range(len(inst))]
        pres = _gdn_pre(qs, ks, vs, [gbs[sub][:, h:h + 1] for sub, h in inst], betas, mk,
                        ts=[t_ref[pl.ds(i * c, c), :] for i in range(len(inst))])
        vns, outs, _ = _gdn_apply(pres, sts, leaving=False)

        nw = nw_ref[...]
        dnw = jnp.zeros((1, hd), F32)
        dos = []
        for i, (sub, h) in enumerate(inst):
            o = outs[i]
            gate = gate_ref[rows[sub], lanes[h]]
            dy = dy_ref[rows[sub], lanes[h]].astype(F32)
            sg = _sigmoid(gate)
            rs = lax.rsqrt(jnp.mean(o * o, axis=-1, keepdims=True) + EPS)
            nrm = o * rs
            dgate_ref[rows[sub], lanes[h]] = (dy * nrm * nw * sg * (1.0 + gate * (1.0 - sg))).astype(BF16)
            dnv = dy * (gate * sg)
            dnw = dnw + jnp.sum(dnv * nrm, axis=0, keepdims=True)
            dno = dnv * nw
            dos.append(rs * (dno - nrm * jnp.mean(dno * nrm, axis=-1, keepdims=True)))
        dnw_ref[...] += dnw

        from_o = [_dot(p["attn"], do, 0, 0) for p, do in zip(pres, dos)]
        to_st = [_dot(p["qd"], do, 0, 0) for p, do in zip(pres, dos)]
        dst = [dstate[ls, :] for ls in lanes]
        dsts = [None] * len(inst)
        dvns = [None] * len(inst)
        for sub in reversed(range(grp)):
            idx = [sub * nh + h for h in range(nh)]
            for h, i in enumerate(idx):
                dsts[i] = dst[h]
                dvns[i] = from_o[i] + _dot(pres[i]["kd"], dst[h], 1, 0)
            dst = [pres[i]["cd"] * dst[h] + to_st[i] - _dot(pres[i]["w"], dvns[i], 0, 0) for h, i in enumerate(idx)]
        for h in range(nh):
            dstate[lanes[h], :] = dst[h]

        grads = _gdn_bwd_rest(qs, ks, vs, betas, sts, pres, vns, dos, dvns, dsts, mk)
        lane = lax.broadcasted_iota(jnp.int32, (c, LANE), 1)
        dgb = [jnp.zeros((c, LANE), F32) for _ in range(grp)]
        for (sub, h), (dq, dk, dv, dg, dbeta) in zip(inst, grads):
            dqkv_ref[rows[sub], lanes[h]] = dq
            dqkv_ref[rows[sub], slice(GDN_WIDTH + h * hd, GDN_WIDTH + (h + 1) * hd)] = dk
            dqkv_ref[rows[sub], slice(2 * GDN_WIDTH + h * hd, 2 * GDN_WIDTH + (h + 1) * hd)] = dv
            dgb[sub] = jnp.where(lane == h, dg, jnp.where(lane == nh + h, dbeta, dgb[sub]))
        for sub in range(grp):
            dgb_ref[rows[sub], :] = dgb[sub]

    return pl.pallas_call(
        body, grid=(n_chunks // grp,),
        in_specs=[rev(0, GDN_WIDTH), rev(1, GDN_WIDTH), rev(2, GDN_WIDTH), rev(0, LANE),
                  rev(COL_GATE // GDN_WIDTH, GDN_WIDTH), BS((1, hd), lambda n: (0, 0)),
                  BS((grp * nh * hd, hd), lambda n: (last - n, 0)), BS((grp * nh * c, c), lambda n: (last - n, 0)),
                  rev(1, GDN_WIDTH)],
        out_specs=[rev(0, 3 * GDN_WIDTH), rev(0, LANE), rev(0, GDN_WIDTH), BS((1, hd), lambda n: (0, 0))],
        out_shape=[SDS((s, 3 * GDN_WIDTH), F32), SDS((s, LANE), F32), SDS((s, GDN_WIDTH), BF16), SDS((1, hd), F32)],
        scratch_shapes=[pltpu.VMEM((nh * hd, hd), F32)],
        compiler_params=_cp(), name=name)(qkv, qkv, qkv, gbeta, proj, norm_w, states, tinv, dcat)


XATTN_TQ = 512


def _xattn_probs(qh, kh):
    sc = _dot(qh, kh, 1, 1) * (XATTN_HEAD_DIM ** -0.5)
    p = jnp.exp(sc - jnp.max(sc, axis=-1, keepdims=True))
    return p / jnp.sum(p, axis=-1, keepdims=True)


def _xattn_fwd(q, kv, *, name):
    s, d = q.shape
    m = kv.shape[0]
    tq, hd = _tile(s, XATTN_TQ), XATTN_HEAD_DIM

    def body(q_ref, k_ref, v_ref, o_ref):
        for h in range(XATTN_HEADS):
            ls = slice(h * hd, (h + 1) * hd)
            p = _xattn_probs(q_ref[:, ls], k_ref[:, ls])
            o_ref[:, ls] = _dot(p, v_ref[:, ls], 1, 0).astype(BF16)

    return pl.pallas_call(
        body, grid=(s // tq,),
        in_specs=[BS((tq, d), lambda i: (i, 0)), BS((m, d), lambda i: (0, 0)), BS((m, d), lambda i: (0, 1))],
        out_specs=BS((tq, d), lambda i: (i, 0)), out_shape=SDS((s, d), BF16),
        compiler_params=_cp(), name=name)(q, kv, kv)


def _xattn_bwd(q, kv, do, *, name):
    s, d = q.shape
    m = kv.shape[0]
    tq, hd = _tile(s, XATTN_TQ), XATTN_HEAD_DIM
    scale = hd ** -0.5

    def body(q_ref, k_ref, v_ref, do_ref, dq_ref, dkv_ref):
        @pl.when(pl.program_id(0) == 0)
        def _():
            dkv_ref[...] = jnp.zeros_like(dkv_ref)

        for h in range(XATTN_HEADS):
            ls = slice(h * hd, (h + 1) * hd)
            vs = slice(d + h * hd, d + (h + 1) * hd)
            qh, kh, doh = q_ref[:, ls], k_ref[:, ls], do_ref[:, ls]
            p = _xattn_probs(qh, kh)
            dp = _dot(doh, v_ref[:, ls], 1, 1)
            ds = p * (dp - jnp.sum(p * dp, axis=-1, keepdims=True)) * scale
            dq_ref[:, ls] = _dot(ds, kh, 1, 0).astype(BF16)
            dkv_ref[:, ls] += _dot(ds, qh, 0, 0)
            dkv_ref[:, vs] += _dot(p, doh, 0, 0)

    row = BS((tq, d), lambda i: (i, 0))
    return pl.pallas_call(
        body, grid=(s // tq,),
        in_specs=[row, BS((m, d), lambda i: (0, 0)), BS((m, d), lambda i: (0, 1)), row],
        out_specs=[row, BS((m, 2 * d), lambda i: (0, 0))],
        out_shape=[SDS((s, d), BF16), SDS((m, 2 * d), F32)],
        compiler_params=_cp(), name=name)(q, kv, kv, do)


def _pad_lanes(vec4):
    return jnp.zeros((1, LANE), F32).at[0, :GDN_HEADS].set(vec4)


def _layer_fwd(h0, mem, tabs, p):
    sv = dict(h0=h0)
    hn1 = _rmsnorm(h0, p["norm_mix_pre"], name="norm_mix_pre")
    proj = _mm(hn1, p["w_in"], name="mm_in")
    ya, lse = _attn_flash_fwd(proj, tabs, name="attn_fwd")
    yc = _sconv_fwd(proj, p["conv_short"], name="sconv_fwd")
    qkv = _gdn_pre_fwd(proj, p["conv_gdn"], name="gdn_pre_fwd")
    gbeta = _gdn_gates_fwd(proj, p["gdn_a_log"], p["gdn_dt_bias"], name="gdn_gates_fwd")
    yg, states, tinv = _gdn_core_fwd(qkv, gbeta, proj, p["gdn_norm"], name="gdn_core_fwd")
    cat = jnp.concatenate([ya, yc, yg], axis=-1)
    mix = _mm(cat, p["w_out"], name="mm_out")
    h1 = _resnorm(h0, mix, p["norm_mix_post"], name="norm_mix_post")
    hn2 = _rmsnorm(h1, p["norm_xattn_pre"], name="norm_xattn_pre")
    memn = _rmsnorm(mem, p["norm_mem"], name="norm_mem")
    xq = _mm(hn2, p["w_xq"], out_dtype=BF16, name="mm_xq")
    kv = _mm(memn, p["w_xkv"], out_dtype=BF16, b_shards=True, name="mm_xkv")
    xo = _xattn_fwd(xq, kv, name="xattn_fwd")
    xa = _mm(xo, p["w_xo"], name="mm_xo")
    h2 = _resnorm(h1, xa, p["norm_xattn_post"], name="norm_xattn_post")
    hn3 = _rmsnorm(h2, p["norm_ffn_pre"], name="norm_ffn_pre")
    gu, act = _gate_up_swiglu(hn3, p["w_gate_up"], name="mm_gate_up")
    f = _mm(act, p["w_down"], name="mm_down")
    h3 = _resnorm(h2, f, p["norm_ffn_post"], name="norm_ffn_post")
    sv.update(hn1=hn1, proj=proj, lse=lse, qkv=qkv, gbeta=gbeta, states=states, tinv=tinv, cat=cat, mix=mix, h1=h1, hn2=hn2,
              memn=memn, xq=xq, kv=kv, xo=xo, xa=xa, h2=h2, hn3=hn3, gu=gu, act=act, f=f)
    return h3, sv


def _layer_bwd(dh3, mem, tabs, p, sv, after_ffn=None, after_xattn=None):
    g = {}
    df, g["norm_ffn_post"] = _rmsnorm_bwd(sv["f"], p["norm_ffn_post"], dh3, name="norm_ffn_post_bwd")
    dact = _mm(df, p["w_down"], tb=True, name="mm_down_da")
    g["w_down"] = _mm(sv["act"], df, ta=True, name="mm_down_dw")
    dgu = _swiglu_bwd(sv["gu"], dact, name="swiglu_bwd")
    dhn3 = _mm(dgu, p["w_gate_up"], tb=True, b_shards=True, halves=True, name="mm_gate_up_da")
    g["w_gate_up"] = _mm(sv["hn3"], dgu, ta=True, out_shards=True, halves=True, name="mm_gate_up_dw")
    dh2, g["norm_ffn_pre"] = _rmsnorm_bwd(sv["h2"], p["norm_ffn_pre"], dhn3, res=dh3, name="norm_ffn_pre_bwd")
    token = after_ffn(dh2) if after_ffn is not None else None
    w_post = p["norm_xattn_post"] if token is None else p["norm_xattn_post"] + token[:1, :1]
    dxa, g["norm_xattn_post"] = _rmsnorm_bwd(sv["xa"], w_post, dh2, name="norm_xattn_post_bwd")
    dxo = _mm(dxa, p["w_xo"], tb=True, name="mm_xo_da")
    g["w_xo"] = _mm(sv["xo"], dxa, ta=True, name="mm_xo_dw")
    dxq, dkv = _xattn_bwd(sv["xq"], sv["kv"], dxo, name="xattn_bwd")
    dhn2 = _mm(dxq, p["w_xq"], tb=True, name="mm_xq_da")
    g["w_xq"] = _mm(sv["hn2"], dxq, ta=True, name="mm_xq_dw")
    dmemn = _mm(dkv, p["w_xkv"], tb=True, b_shards=True, name="mm_xkv_da")
    g["w_xkv"] = _mm(sv["memn"], dkv, ta=True, out_shards=True, name="mm_xkv_dw")
    _, g["norm_mem"] = _rmsnorm_bwd(mem, p["norm_mem"], dmemn, name="norm_mem_bwd")
    dh1, g["norm_xattn_pre"] = _rmsnorm_bwd(sv["h1"], p["norm_xattn_pre"], dhn2, res=dh2, name="norm_xattn_pre_bwd")
    token = after_xattn(g, dh1) if after_xattn is not None else None
    w_post = p["norm_mix_post"] if token is None else p["norm_mix_post"] + token[:1, :1]
    dmix, g["norm_mix_post"] = _rmsnorm_bwd(sv["mix"], w_post, dh1, name="norm_mix_post_bwd")
    dcat = _mm(dmix, p["w_out"], tb=True, name="mm_out_da")
    g["w_out"] = _mm(sv["cat"], dmix, ta=True, name="mm_out_dw")
    proj = sv["proj"]
    daq, dak, dav = _attn_flash_bwd(proj, tabs, sv["cat"], sv["lse"], dcat, name="attn_bwd")
    dcb, dcc, dcx, g["conv_short"] = _sconv_bwd(proj, p["conv_short"], dcat, name="sconv_bwd")
    dqkv, dgbeta, dgate, g["gdn_norm"] = _gdn_core_bwd(sv["qkv"], sv["gbeta"], proj, p["gdn_norm"], sv["states"], sv["tinv"],
                                                        dcat, name="gdn_core_bwd")
    dgqkv, g["conv_gdn"] = _gdn_pre_bwd(proj, p["conv_gdn"], dqkv, name="gdn_pre_bwd")
    dab, g["gdn_a_log"], g["gdn_dt_bias"] = _gdn_gates_bwd(proj, p["gdn_a_log"], p["gdn_dt_bias"], dgbeta,
                                                          name="gdn_gates_bwd")
    s = proj.shape[0]
    dproj = jnp.concatenate([daq, dak, dav, dcb, dcc, dcx, dgqkv, dgate, dab,
                             jnp.zeros((s, IN_PAD - COL_AB - LANE), BF16)], axis=-1)
    dhn1 = _mm(dproj, p["w_in"], tb=True, name="mm_in_da")
    g["w_in"] = _mm(sv["hn1"], dproj, ta=True, name="mm_in_dw")
    dh0, g["norm_mix_pre"] = _rmsnorm_bwd(sv["h0"], p["norm_mix_pre"], dhn1, res=dh1, name="norm_mix_pre_bwd")
    return dh0, g


MATRICES = ("w_in", "w_out", "w_xq", "w_xkv", "w_xo", "w_gate_up", "w_down")
VECTORS = ("norm_mix_pre", "norm_mix_post", "conv_short", "conv_gdn", "gdn_a_log", "gdn_dt_bias", "gdn_norm",
           "norm_mem", "norm_xattn_pre", "norm_xattn_post", "norm_ffn_pre", "norm_ffn_post")


def _w_in_segments(n_shards=4):
    c = IN_WIDTH // n_shards
    moves = ((0, COL_GATE, 0), (COL_GATE, COL_GATE + 8, COL_AB), (COL_GATE + 8, IN_WIDTH, COL_GATE))
    segs = []
    for s in range(n_shards):
        for lo, hi, dst in moves:
            a, b = max(lo, s * c), min(hi, (s + 1) * c)
            if a < b:
                segs.append((s, a - s * c, dst + a - lo, b - a))
    return segs


def _w_in_pack(g, *, name):
    ns, r, c = g.shape
    tr = _tile(r, 256)

    def body(g_ref, o_ref):
        o_ref[:, IN_WIDTH:] = jnp.zeros((tr, IN_PAD - IN_WIDTH), o_ref.dtype)
        for s, src, dst, width in _w_in_segments(ns):
            o_ref[:, dst:dst + width] = g_ref[s, :, src:src + width]

    return pl.pallas_call(
        body, grid=(r // tr,), in_specs=[BS((ns, tr, c), lambda i: (0, i, 0))],
        out_specs=BS((tr, IN_PAD), lambda i: (i, 0)), out_shape=SDS((r, IN_PAD), g.dtype),
        compiler_params=_cp(), name=name)(g)


def _w_in_unpack(dw, *, name):
    r = dw.shape[0]
    ns, c = 4, IN_WIDTH // 4
    tr = _tile(r, 256)

    def body(d_ref, o_ref):
        for s, src, dst, width in _w_in_segments(ns):
            o_ref[s, :, src:src + width] = d_ref[:, dst:dst + width]

    return pl.pallas_call(
        body, grid=(r // tr,), in_specs=[BS((tr, IN_PAD), lambda i: (i, 0))],
        out_specs=BS((ns, tr, c), lambda i: (0, i, 0)), out_shape=SDS((ns, r, c), dw.dtype),
        compiler_params=_cp(), name=name)(dw)


def _layer_params(full, l):
    p = {n: full[n][l] for n in MATRICES}
    for n in VECTORS:
        v = full[n][l]
        if n in ("gdn_a_log", "gdn_dt_bias"):
            p[n] = _pad_lanes(v)
        elif v.ndim == 1:
            p[n] = v.reshape(1, -1)
        else:
            p[n] = v
    return p


def _local_step(x, mem, pos, target, full, matrices_for=None, on_grads=None, mid_backward=None,
                after_xattn=None):
    tabs = _rope_tables(pos, name="rope_tables")
    h = x
    saved, params = [], []
    for l in range(DEPTH):
        if matrices_for is not None:
            full = {**full, **{n: {l: v} for n, v in matrices_for(l, h).items()}}
        p = _layer_params(full, l)
        h, sv = _layer_fwd(h, mem, tabs, p)
        params.append(p)
        saved.append(sv)
    loss_row, dh = _loss_grad(h, target, name="loss_grad")
    grads = [None] * DEPTH
    token = None
    for l in reversed(range(DEPTH)):
        p = params[l]
        if token is not None:
            p = {**p, "norm_ffn_post": p["norm_ffn_post"] + token[:1, :1]}
        late = None if after_xattn is None else (lambda g, dh1, l=l: after_xattn(l, g, dh1))
        dh, grads[l] = _layer_bwd(dh, mem, tabs, p, saved[l], after_ffn=mid_backward, after_xattn=late)
        if on_grads is not None:
            token = on_grads(l, grads[l], dh)
    return loss_row, dh, grads


ANY = pl.BlockSpec(memory_space=pl.ANY)
MESH = pl.DeviceIdType.MESH


def _flip(pos, mask):
    return tuple(1 - v if m else v for v, m in zip(pos, mask))


def _exchange(ins, out_shapes, remote, local, *, name):
    n_in = len(ins)
    n_out = len(out_shapes)

    def at(ref, idx):
        return ref.at[idx] if idx else ref

    def body(*refs):
        in_refs = refs[:n_in]
        out_refs = refs[n_in:n_in + n_out]
        send_sems, recv_sems, local_sems = refs[n_in + n_out:]
        me = (lax.axis_index("x"), lax.axis_index("y"), lax.axis_index("c"))
        waits = []
        for k, (ii, src_at, oi, dst_at, mask) in enumerate(remote):
            peer = _flip(me, mask)
            pltpu.make_async_remote_copy(
                src_ref=at(in_refs[ii], src_at(me, peer)), dst_ref=at(out_refs[oi], dst_at(me)),
                send_sem=send_sems.at[k], recv_sem=recv_sems.at[k], device_id=peer, device_id_type=MESH).start()
            waits.append(pltpu.make_async_remote_copy(
                src_ref=at(in_refs[ii], src_at(peer, me)), dst_ref=at(out_refs[oi], dst_at(peer)),
                send_sem=send_sems.at[k], recv_sem=recv_sems.at[k], device_id=peer, device_id_type=MESH))
        own = []
        for k, (ii, src_at, oi, dst_at) in enumerate(local):
            cp = pltpu.make_async_copy(at(in_refs[ii], src_at(me)), at(out_refs[oi], dst_at(me)), local_sems.at[k])
            cp.start()
            own.append(cp)
        for w in waits:
            w.wait_send()
            w.wait_recv()
        for cp in own:
            cp.wait()

    return pl.pallas_call(
        body, in_specs=[ANY] * n_in, out_specs=[ANY] * n_out, out_shape=list(out_shapes),
        scratch_shapes=[pltpu.SemaphoreType.DMA((len(remote),)), pltpu.SemaphoreType.DMA((len(remote),)),
                        pltpu.SemaphoreType.DMA((max(len(local), 1),))],
        name=name)(*ins)


HBM = pl.BlockSpec(memory_space=pltpu.HBM)
SEM = pl.BlockSpec(memory_space=pltpu.SEMAPHORE)
SPLIT_EFFECT = pltpu.SideEffectType.DATAFLOW_SIDE_EFFECTING


def _exchange_start(ins, land_shapes, remote, *, name):
    n_in, n_land, n_cp = len(ins), len(land_shapes), len(remote)

    def body(*refs):
        in_refs, land_refs = refs[:n_in], refs[n_in:n_in + n_land]
        send_sems, recv_sems = refs[n_in + n_land:n_in + n_land + 2]
        token = refs[-1]
        me = (lax.axis_index("x"), lax.axis_index("y"), lax.axis_index("c"))
        for k, (ii, src_at, oi, dst_at, mask) in enumerate(remote):
            peer = _flip(me, mask)
            idx_s, idx_d = src_at(me, peer), dst_at(me)
            pltpu.make_async_remote_copy(
                src_ref=in_refs[ii].at[idx_s] if idx_s else in_refs[ii],
                dst_ref=land_refs[oi].at[idx_d] if idx_d else land_refs[oi],
                send_sem=send_sems.at[k], recv_sem=recv_sems.at[k], device_id=peer, device_id_type=MESH).start()
        token[...] = jnp.zeros_like(token)

    buffers = [pltpu.with_memory_space_constraint(a, pltpu.HBM) for a in ins]
    buffers += [pltpu.with_memory_space_constraint(lax.empty(s.shape, s.dtype), pltpu.HBM) for s in land_shapes]
    out = pl.pallas_call(
        body, name=name,
        out_shape=(pltpu.SemaphoreType.DMA((n_cp,)), pltpu.SemaphoreType.DMA((n_cp,)),
                   *[pltpu.HBM(b.shape, b.dtype) for b in buffers], SDS((8, LANE), F32)),
        in_specs=[HBM] * len(buffers),
        out_specs=(SEM, SEM, *[HBM] * len(buffers), pl.BlockSpec(memory_space=pltpu.VMEM)),
        input_output_aliases={i: 2 + i for i in range(len(buffers))},
        compiler_params=pltpu.CompilerParams(has_side_effects=SPLIT_EFFECT))(*buffers)
    return out[0], out[1], list(out[2:2 + n_in]), list(out[2 + n_in:2 + n_in + n_land]), out[-1]


def _exchange_wait(send_sems, recv_sems, ins, lands, remote, after, *, name):
    n_in, n_land = len(ins), len(lands)

    def body(*refs):
        in_refs, land_refs = refs[:n_in], refs[n_in:n_in + n_land]
        send_sems_, recv_sems_ = refs[n_in + n_land:n_in + n_land + 2]
        me = (lax.axis_index("x"), lax.axis_index("y"), lax.axis_index("c"))
        for k, (ii, src_at, oi, dst_at, mask) in enumerate(remote):
            peer = _flip(me, mask)
            idx_s, idx_d = src_at(peer, me), dst_at(peer)
            cp = pltpu.make_async_remote_copy(
                src_ref=in_refs[ii].at[idx_s] if idx_s else in_refs[ii],
                dst_ref=land_refs[oi].at[idx_d] if idx_d else land_refs[oi],
                send_sem=send_sems_.at[k], recv_sem=recv_sems_.at[k], device_id=peer, device_id_type=MESH)
            cp.wait_send()
            cp.wait_recv()

    buffers = list(ins) + list(lands)
    out = pl.pallas_call(
        body, name=name, out_shape=tuple(pltpu.HBM(b.shape, b.dtype) for b in buffers),
        in_specs=[HBM] * len(buffers) + [SEM, SEM, ANY], out_specs=tuple([HBM] * len(buffers)),
        input_output_aliases={i: i for i in range(len(buffers))},
        compiler_params=pltpu.CompilerParams(has_side_effects=SPLIT_EFFECT))(*buffers, send_sems, recv_sems, after)
    return list(out[:n_in]), list(out[n_in:])


def _chip(pos):
    return 2 * pos[0] + pos[1]


XY_MASKS = ((1, 0, 0), (0, 1, 0), (1, 1, 0))
SIBLING = (0, 0, 1)
ALL_MASKS = tuple((a, b, c) for a in (0, 1) for b in (0, 1) for c in (0, 1))[1:]


def _gather_xy(arrs, *, name):
    n = len(arrs)
    outs = [SDS((4,) + a.shape, a.dtype) for a in arrs]
    halves = [a.shape[0] // 2 for a in arrs]

    def body(*refs):
        in_refs, out_refs = refs[:n], refs[n:2 * n]
        ici_send, ici_recv, d2d_send, d2d_recv = refs[2 * n:]
        me = (lax.axis_index("x"), lax.axis_index("y"), lax.axis_index("c"))
        sibling = _flip(me, SIBLING)
        flows = []
        for i in range(n):
            mine = pl.ds(me[2] * halves[i], halves[i])
            other = pl.ds(sibling[2] * halves[i], halves[i])
            for m in XY_MASKS:
                k = len(flows)
                peer = _flip(me, m)

                def remote(src, dst, sems, to, k=k):
                    return pltpu.make_async_remote_copy(src_ref=src, dst_ref=dst, send_sem=sems[0].at[k],
                                                        recv_sem=sems[1].at[k], device_id=to, device_id_type=MESH)

                landed = out_refs[i].at[_chip(peer), mine]
                send = remote(in_refs[i].at[mine], out_refs[i].at[_chip(me), mine], (ici_send, ici_recv), peer)
                send.start()
                arrive = remote(in_refs[i].at[mine], landed, (ici_send, ici_recv), peer)
                forward = remote(landed, landed, (d2d_send, d2d_recv), sibling)
                handed = remote(out_refs[i].at[_chip(peer), other], out_refs[i].at[_chip(peer), other],
                                (d2d_send, d2d_recv), sibling)
                flows.append((send, arrive, forward, handed))
        for _, arrive, forward, _ in flows:
            arrive.wait_recv()
            forward.start()
        for send, _, forward, handed in flows:
            handed.wait_recv()
            send.wait_send()
            forward.wait_send()

    n_flows = 3 * n
    return pl.pallas_call(
        body, in_specs=[ANY] * n, out_specs=[ANY] * n, out_shape=outs,
        scratch_shapes=[pltpu.SemaphoreType.DMA((n_flows,))] * 4, name=name)(*arrs)


def _gather_all(arr, *, name):
    slot = lambda pos: (4 * pos[0] + 2 * pos[1] + pos[2],)
    whole = lambda *_: ()
    remote = [(0, whole, 0, slot, m) for m in ALL_MASKS]
    return _exchange([arr], [SDS((8,) + arr.shape, arr.dtype)], remote, [(0, whole, 0, slot)], name=name)[0]


def _send_to_sibling(arrs, *, name):
    outs = [SDS(a.shape, a.dtype) for a in arrs]
    whole = lambda *_: ()
    remote = [(i, whole, i, whole, SIBLING) for i in range(len(arrs))]
    return _exchange(arrs, outs, remote, [], name=name)


def _add_half(g, other, core, *, name):
    n4, nl, r, c = g.shape
    half = r // 2
    g3 = g.reshape(n4 * nl, 2, half, c)
    o3 = other.reshape(n4 * nl, half, c)
    tr = _rows_tile(half) if half > 512 else half

    def body(core_ref, g_ref, o_ref, out_ref):
        out_ref[...] = (g_ref[...] + o_ref[...]).astype(BF16)

    return pl.pallas_call(
        body,
        grid_spec=pltpu.PrefetchScalarGridSpec(
            num_scalar_prefetch=1, grid=(n4 * nl, half // tr),
            in_specs=[BS((None, None, tr, c), lambda i, j, core_ref: (i, core_ref[0], j, 0)),
                      BS((None, tr, c), lambda i, j, core_ref: (i, j, 0))],
            out_specs=BS((None, tr, c), lambda i, j, core_ref: (i, j, 0))),
        out_shape=SDS((n4 * nl, half, c), BF16), compiler_params=_cp(), name=name)(core, g3, o3).reshape(n4, nl, half, c)


def _sum_chips(parts, mine, chip, *, name):
    n4, nl, h, c = mine.shape
    tr = _rows_tile(h) if h > 512 else h

    def body(chip_ref, p_ref, own_ref, out_ref):
        me = chip_ref[0]
        own = own_ref[...].astype(F32)
        across = [p_ref[j].astype(F32) for j in range(len(XY_MASKS))]
        t = []
        for s in range(n4):
            rel = s ^ me
            t.append(jnp.where(rel == 0, own, jnp.where(rel == 2, across[0], jnp.where(rel == 1, across[1], across[2]))))
        out_ref[...] = ((t[0] + t[1]) + t[2]) + t[3]

    return pl.pallas_call(
        body,
        grid_spec=pltpu.PrefetchScalarGridSpec(
            num_scalar_prefetch=1, grid=(nl, h // tr),
            in_specs=[BS((len(XY_MASKS), None, tr, c), lambda i, j, chip_ref: (0, i, j, 0)),
                      BS((None, None, tr, c), lambda i, j, chip_ref: (chip_ref[0], i, j, 0))],
            out_specs=BS((None, tr, c), lambda i, j, chip_ref: (i, j, 0))),
        out_shape=SDS((nl, h, c), F32), compiler_params=_cp(), name=name)(chip, parts, mine)


def _sum_devices(parts, *, name):
    n, r, c = parts.shape

    def body(p_ref, out_ref):
        acc = p_ref[0]
        for d in range(1, n):
            acc = acc + p_ref[d]
        out_ref[...] = acc

    return pl.pallas_call(
        body, grid=(1,), in_specs=[BS((n, r, c), lambda i: (0, 0, 0))], out_specs=BS((r, c), lambda i: (0, 0)),
        out_shape=SDS((r, c), F32), compiler_params=_cp(), name=name)(parts)


WEIGHTS = ("norm_mix_pre", "norm_mix_post", "w_in", "conv_short", "conv_gdn", "gdn_a_log", "gdn_dt_bias",
           "gdn_norm", "w_out", "norm_mem", "norm_xattn_pre", "norm_xattn_post", "w_xq", "w_xkv", "w_xo",
           "norm_ffn_pre", "norm_ffn_post", "w_gate_up", "w_down")
COL_SHARDED = ("w_in", "w_xkv", "w_gate_up", "conv_short", "conv_gdn")
ROW_SHARDED = ("w_out", "w_xq", "w_xo", "w_down")
SMALL_SHARDED = ("conv_short", "conv_gdn")
SMALL_ROW_PAD = 8


KEPT_AS_SHARDS = ("w_xkv", "w_gate_up")


def _from_shards(n, g):
    if n in KEPT_AS_SHARDS:
        return g
    if n == "w_in":
        return _w_in_pack(g, name="w_in_pack")
    if n in COL_SHARDED:
        t = jnp.moveaxis(g, 0, -2)
        return t.reshape(t.shape[:-2] + (-1,))
    return g.reshape(-1, g.shape[-1])


def _to_shards(n, g):
    if n in KEPT_AS_SHARDS:
        return g
    if n == "w_in":
        return _w_in_unpack(g, name="w_in_unpack")
    return g.reshape(4, -1, g.shape[-1])


def _pack_small(grads):
    rows = []
    for g in grads:
        for n in WEIGHTS:
            if n not in MATRICES:
                part = g[n].reshape(-1, LANE)
                rows.append(jnp.pad(part, ((0, -part.shape[0] % SMALL_ROW_PAD), (0, 0))))
    return jnp.concatenate(rows, axis=0)


def _unpack_small(packed, like):
    out, at = [], 0
    for _ in range(DEPTH):
        g = {}
        for n in WEIGHTS:
            if n not in MATRICES:
                shape = like[n].shape
                k = math.prod(shape) // LANE
                g[n] = packed[at:at + k].reshape(shape)
                at += k + (-k % SMALL_ROW_PAD)
        out.append(g)
    return out


def kernel(x, mem, positions, norm_mix_pre, norm_mix_post, w_in, conv_short, conv_gdn, gdn_a_log, gdn_dt_bias, gdn_norm, w_out, norm_mem, norm_xattn_pre, norm_xattn_post, w_xq, w_xkv, w_xo, norm_ffn_pre, norm_ffn_post, w_gate_up, w_down, loss_target, m_norm_mix_pre, m_norm_mix_post, m_w_in, m_conv_short, m_conv_gdn, m_gdn_a_log, m_gdn_dt_bias, m_gdn_norm, m_w_out, m_norm_mem, m_norm_xattn_pre, m_norm_xattn_post, m_w_xq, m_w_xkv, m_w_xo, m_norm_ffn_pre, m_norm_ffn_post, m_w_gate_up, m_w_down, v_norm_mix_pre, v_norm_mix_post, v_w_in, v_conv_short, v_conv_gdn, v_gdn_a_log, v_gdn_dt_bias, v_gdn_norm, v_w_out, v_norm_mem, v_norm_xattn_pre, v_norm_xattn_post, v_w_xq, v_w_xkv, v_w_xo, v_norm_ffn_pre, v_norm_ffn_post, v_w_gate_up, v_w_down):
    args = dict(locals())
    w = {n: args[n] for n in WEIGHTS}
    m = {n: args["m_" + n] for n in WEIGHTS}
    v = {n: args["v_" + n] for n in WEIGHTS}
    seq = x.shape[1]
    chip = 2 * lax.axis_index("x") + lax.axis_index("y")
    core = lax.axis_index("c").astype(jnp.int32).reshape(1)

    def cast(n, l):
        return w[n][l].astype(BF16)

    first = [cast(n, 0) for n in MATRICES] + [w[n] for n in SMALL_SHARDED]
    blocks = _gather_xy(first, name="gather_weights")
    blocks = [lax.dynamic_update_index_in_dim(b, o, chip, axis=0) for b, o in zip(blocks, first)]
    full = {n: _from_shards(n, b) for n, b in zip(list(MATRICES) + list(SMALL_SHARDED), blocks)}
    layer0 = {n: full.pop(n) for n in MATRICES}
    for n in WEIGHTS:
        if n not in full and n not in MATRICES:
            full[n] = w[n]
    whole = lambda *_: ()
    gather_plan = [(i, whole, i, (lambda sender: (_chip(sender),)), mask)
                   for i in range(len(MATRICES)) for mask in XY_MASKS]
    in_flight = {}
    started = jnp.zeros((1, 1), F32)
    for l in range(1, DEPTH):
        own = [cast(n, l) for n in MATRICES]
        out = _exchange_start(own, [SDS((4,) + o.shape, o.dtype) for o in own], gather_plan, name=f"gather_start_{l}")
        in_flight[l] = (own,) + out[:4]
        started = started + out[4][:1, :1]
    full["norm_mix_pre"] = full["norm_mix_pre"] + started

    def matrices_for(l, h):
        if l == 0:
            return layer0
        own, send_sems, recv_sems, thru, lands = in_flight.pop(l)
        _, lands = _exchange_wait(send_sems, recv_sems, thru, lands, gather_plan, h, name=f"gather_wait_{l}")
        lands = [lax.dynamic_update_index_in_dim(b, o, chip, axis=0) for b, o in zip(lands, own)]
        return {n: _from_shards(n, b) for n, b in zip(MATRICES, lands)}

    chip1 = chip.astype(jnp.int32).reshape(1)
    early = ("w_xq", "w_xkv", "w_xo", "w_gate_up", "w_down")
    reduced = {l: {} for l in range(DEPTH)}
    swapping, sending, sharing = [], [], []

    def scatter_plan(count):
        return [(i, (lambda sender, receiver: (_chip(receiver),)), i, (lambda sender, j=j: (j,)), mask)
                for i in range(count) for j, mask in enumerate(XY_MASKS)]

    def swap_plan(mine):
        def half_rows(shape):
            half = shape[2] // 2
            return lambda sender, receiver: (slice(None), slice(None), pl.ds(receiver[2] * half, half))

        plan = [(i, half_rows(a.shape), i, whole, SIBLING) for i, a in enumerate(mine)]
        return plan, [SDS(a.shape[:2] + (a.shape[2] // 2, a.shape[3]), a.dtype) for a in mine]

    def start_scatter(tag, names, mine, theirs):
        pair = [_add_half(a, b, core, name="grads_pair_sum") for a, b in zip(mine, theirs)]
        lands = [SDS((len(XY_MASKS),) + p.shape[1:], p.dtype) for p in pair]
        send_sems, recv_sems, pair, lands, token = _exchange_start(pair, lands, scatter_plan(len(names)),
                                                                   name=f"scatter_start_{tag}")
        sending.append((tag, names, send_sems, recv_sems, pair, lands))
        return token

    def finish_scatter(after):
        tag, names, send_sems, recv_sems, pair, lands = sending.pop(0)
        pair, parts = _exchange_wait(send_sems, recv_sems, pair, lands, scatter_plan(len(names)), after,
                                     name=f"scatter_wait_{tag}")
        layer = int(tag[0])
        sums = [_sum_chips(p, pr, chip1, name="grads_chip_sum") for p, pr in zip(parts, pair)]
        token = None
        if layer > 0:
            plan = [(i, whole, i, whole, SIBLING) for i in range(len(sums))]
            send_sems, recv_sems, sums, lands, token = _exchange_start(
                sums, [SDS(r.shape, r.dtype) for r in sums], plan, name=f"share_start_{tag}")
            sharing.append((layer, names, send_sems, recv_sems, sums, lands, plan, tag))
        reduced[layer].update(zip(names, sums))
        return token

    def on_grads(l, g, dh):
        tokens = []
        while sending:
            tokens.append(finish_scatter(dh))
        names = [n for n in MATRICES if not (l == 0 and n in early)]
        mine = [_to_shards(n, g[n])[:, None] for n in names]
        plan, lands = swap_plan(mine)
        send_sems, recv_sems, mine, lands, token = _exchange_start(mine, lands, plan, name=f"swap_start_{l}")
        swapping.append((str(l), names, send_sems, recv_sems, mine, lands, plan))
        for t in tokens:
            if t is not None:
                token = token + t
        return token

    def mid_backward(after):
        if not swapping:
            return None
        tag, names, send_sems, recv_sems, mine, lands, plan = swapping.pop()
        mine, theirs = _exchange_wait(send_sems, recv_sems, mine, lands, plan, after, name=f"swap_wait_{tag}")
        return start_scatter(tag, names, mine, theirs)

    def after_xattn(l, g, dh1):
        if l != 0:
            return None
        mine = [_to_shards(n, g[n])[:, None] for n in early]
        plan, lands = swap_plan(mine)
        theirs = _exchange(mine, lands, plan, [], name="grads_swap_early")
        return start_scatter("0a", early, mine, theirs)

    loss_row, dx, grads = _local_step(x[0], mem[0], positions.reshape(seq, 1), loss_target[0], full,
                                      matrices_for=matrices_for, on_grads=on_grads, mid_backward=mid_backward,
                                      after_xattn=after_xattn)
    mid_backward(dx)
    while sending:
        finish_scatter(dx)
    others = {l: {} for l in range(DEPTH)}
    for layer, names, send_sems, recv_sems, sums, lands, plan, tag in sharing:
        sums, lands = _exchange_wait(send_sems, recv_sems, sums, lands, plan, dx, name=f"share_wait_{tag}")
        reduced[layer].update(zip(names, sums))
        others[layer].update(zip(names, lands))
    last = [reduced[0][n] for n in MATRICES]
    others[0].update(zip(MATRICES, _send_to_sibling(last, name="grads_share_halves")))
    south = lax.axis_index("c") == 0
    grad = {}
    for n in MATRICES:
        layers = []
        for l in range(DEPTH):
            a, b = reduced[l][n], others[l][n]
            layers += [jnp.where(south, a, b), jnp.where(south, b, a)]
        grad[n] = jnp.concatenate(layers, axis=0).reshape(w[n].shape)

    packed = _pack_small(grads)
    total = _sum_devices(_gather_all(packed, name="small_grads_gather"), name="small_grads_sum")
    small = _unpack_small(total, grads[0])
    for n in WEIGHTS:
        if n in MATRICES:
            continue
        g = jnp.stack([s[n] for s in small])
        if n in ("gdn_a_log", "gdn_dt_bias"):
            g = g[:, 0, :GDN_HEADS]
        elif n in SMALL_SHARDED:
            width = w[n].shape[-1]
            g = lax.dynamic_slice_in_dim(g, chip * width, width, axis=2)
        grad[n] = g.reshape(w[n].shape)

    delta, new_m, new_v = {}, {}, {}
    for n in WEIGHTS:
        shape = w[n].shape
        two_d = (-1, shape[-1])
        d, nm, nv = _adamw(w[n].reshape(two_d), grad[n].reshape(two_d), m[n].reshape(two_d), v[n].reshape(two_d),
                           name="adamw_" + n)
        delta[n], new_m[n], new_v[n] = d.reshape(shape), nm.reshape(shape), nv.reshape(shape)

    loss = lax.psum(loss_row[0, 0], ("x", "y", "c"))
    return (loss, dx.reshape(x.shape), *[grad[n] for n in WEIGHTS], *[delta[n] for n in WEIGHTS],
            *[new_m[n] for n in WEIGHTS], *[new_v[n] for n in WEIGHTS])
```
